```python
import math
import jax, jax.numpy as jnp
from jax import lax
import numpy as np

D_MODEL = 1024
BATCH = 8
SEQ = 16384
DEPTH = 4

CHUNK = 64
Q_BLOCK = 128
EPS = 1e-6
N_A_LAYERS = DEPTH // 2
N_B_LAYERS = DEPTH - N_A_LAYERS

A_HEADS = 6
A_HEAD_DIM = 128
A_WIDTH = A_HEADS * A_HEAD_DIM
CONV_K = 4

B_HEADS = 6
QK_NOPE = 128
QK_ROPE = 64
V_HEAD = 128
Q_LORA = 256
KV_LORA = 256
B_WIDTH = B_HEADS * V_HEAD
ROPE_THETA = 10000.0

N_MEM = 256
MEM_HEADS = 4
MEM_HEAD_DIM = 64
MEM_WIDTH = MEM_HEADS * MEM_HEAD_DIM

D_FF = 2816

MIX_WIDTH = A_WIDTH + MEM_WIDTH
A_IN = 4 * A_WIDTH + 2 * A_HEADS + MEM_WIDTH
B_IN = Q_LORA + MEM_WIDTH

kernel_name = "hybrid_gdn_mla_yoco_macaron"


def rms_norm(x, g):
    xf = x.astype(jnp.float32)
    y = xf * lax.rsqrt(jnp.mean(xf * xf, axis=-1, keepdims=True) + EPS)
    return (y * g.astype(jnp.float32)).astype(x.dtype)


def l2_norm(x):
    xf = x.astype(jnp.float32)
    return (xf * lax.rsqrt(jnp.sum(xf * xf, axis=-1, keepdims=True) + EPS)).astype(x.dtype)


def swiglu(x, w_gu, w_down):
    gate, up = jnp.split(x @ w_gu, 2, axis=-1)
    return (jax.nn.silu(gate) * up) @ w_down


def rope_tables(positions, dim):
    inv = ROPE_THETA ** (-jnp.arange(0, dim, 2, dtype=jnp.float32) / dim)
    ang = positions.astype(jnp.float32)[..., None] * inv
    return jnp.cos(ang), jnp.sin(ang)


def apply_rope(x, cos, sin):
    xf = x.astype(jnp.float32)
    x1, x2 = jnp.split(xf, 2, axis=-1)
    c, s = cos[:, :, None, :], sin[:, :, None, :]
    return jnp.concatenate([x1 * c - x2 * s, x1 * s + x2 * c], axis=-1).astype(x.dtype)


def causal_depthwise_conv(x, w):
    k = w.shape[0]
    return lax.conv_general_dilated(
        x, w[:, None, :].astype(x.dtype), window_strides=(1,), padding=[(k - 1, 0)],
        dimension_numbers=("NWC", "WIO", "NWC"), feature_group_count=x.shape[-1])


def chunked_gated_delta_rule(q, k, v, beta, g):
    b, s, h, dk = q.shape
    dv = v.shape[-1]
    n = s // CHUNK
    f32 = jnp.float32

    def chunks(t):
        return t.astype(f32).reshape(b, n, CHUNK, h, -1).transpose(0, 3, 1, 2, 4)

    q, k, v = chunks(q), chunks(k), chunks(v)
    beta = chunks(beta[..., None])[..., 0]
    g_cum = jnp.cumsum(chunks(g[..., None])[..., 0], axis=-1)
    causal = jnp.tril(jnp.ones((CHUNK, CHUNK), dtype=bool))
    strict = jnp.tril(jnp.ones((CHUNK, CHUNK), dtype=bool), -1)
    diff = g_cum[..., :, None] - g_cum[..., None, :]
    decay = jnp.where(causal, jnp.exp(jnp.where(causal, diff, 0.0)), 0.0)
    k_beta = k * beta[..., None]
    t_mat = jnp.where(strict, jnp.einsum("bhnid,bhnjd->bhnij", k_beta, k) * decay, 0.0) \
        + jnp.eye(CHUNK, dtype=f32)
    u = lax.linalg.triangular_solve(t_mat, v * beta[..., None], left_side=True,
                                    lower=True, unit_diagonal=True)
    w = lax.linalg.triangular_solve(t_mat, k_beta * jnp.exp(g_cum)[..., None], left_side=True,
                                    lower=True, unit_diagonal=True)
    qk = jnp.einsum("bhnid,bhnjd->bhnij", q, k) * decay
    g_last = g_cum[..., -1:]
    q_dec = q * jnp.exp(g_cum)[..., None]
    k_dec = k * jnp.exp(g_last - g_cum)[..., None]
    chunk_decay = jnp.exp(g_last[..., 0])
    xs = tuple(jnp.moveaxis(t, 2, 0) for t in (q_dec, k_dec, w, u, qk, chunk_decay))

    def step(state, inp):
        q_c, k_c, w_c, u_c, qk_c, d_c = inp
        v_new = u_c - jnp.einsum("bhcd,bhde->bhce", w_c, state)
        out = jnp.einsum("bhcd,bhde->bhce", q_c, state) + jnp.einsum("bhij,bhje->bhie", qk_c, v_new)
        state = state * d_c[..., None, None] + jnp.einsum("bhcd,bhce->bhde", k_c, v_new)
        return state, out

    _, o = lax.scan(step, jnp.zeros((b, h, dk, dv), f32), xs)
    return o.transpose(1, 0, 3, 2, 4).reshape(b, s, h, dv)


def gated_deltanet(qkv, gate, b_raw, a_raw, conv_w, A_log, dt_bias, out_gain):
    b, s, _ = qkv.shape
    qkv = jax.nn.silu(causal_depthwise_conv(qkv, conv_w))
    q, k, v = (t.reshape(b, s, A_HEADS, A_HEAD_DIM) for t in jnp.split(qkv, 3, axis=-1))
    q = l2_norm(q) * A_HEAD_DIM ** -0.5
    k = l2_norm(k)
    beta = jax.nn.sigmoid(b_raw.astype(jnp.float32))
    g = -jnp.exp(A_log.astype(jnp.float32)) * jax.nn.softplus(
        a_raw.astype(jnp.float32) + dt_bias.astype(jnp.float32))
    o = chunked_gated_delta_rule(q, k, v, beta, g).astype(qkv.dtype)
    o = rms_norm(o, out_gain) * jax.nn.silu(gate.reshape(b, s, A_HEADS, A_HEAD_DIM))
    return o.reshape(b, s, A_WIDTH)


def mla_attention(q_nope, q_rope, k_nope, k_rope, v):
    b, s, h, _ = q_nope.shape
    nb = s // Q_BLOCK
    scale = (QK_NOPE + QK_ROPE) ** -0.5
    key_chunk = jnp.arange(s) // CHUNK
    qn = q_nope.reshape(b, nb, Q_BLOCK, h, QK_NOPE).swapaxes(0, 1)
    qr = q_rope.reshape(b, nb, Q_BLOCK, h, QK_ROPE).swapaxes(0, 1)

    def block(args):
        i, qn_b, qr_b = args
        sc = (jnp.einsum("bqhd,bkhd->bhqk", qn_b, k_nope)
              + jnp.einsum("bqhd,bkd->bhqk", qr_b, k_rope)).astype(jnp.float32) * scale
        q_chunk = (i * Q_BLOCK + jnp.arange(Q_BLOCK)) // CHUNK
        mask = key_chunk[None, :] <= q_chunk[:, None]
        p = jax.nn.softmax(jnp.where(mask, sc, -jnp.inf), axis=-1).astype(v.dtype)
        return jnp.einsum("bhqk,bkhd->bqhd", p, v)

    out = lax.map(block, (jnp.arange(nb), qn, qr))
    return out.swapaxes(0, 1).reshape(b, s, h * V_HEAD)


def memory_attention(q, mem_kv):
    b, s, _ = q.shape
    q = q.reshape(b, s, MEM_HEADS, MEM_HEAD_DIM)
    k, v = (t.reshape(b, N_MEM, MEM_HEADS, MEM_HEAD_DIM) for t in jnp.split(mem_kv, 2, axis=-1))
    sc = jnp.einsum("bqhd,bmhd->bhqm", q, k).astype(jnp.float32) * MEM_HEAD_DIM ** -0.5
    p = jax.nn.softmax(sc, axis=-1).astype(v.dtype)
    return jnp.einsum("bhqm,bmhd->bqhd", p, v).reshape(b, s, MEM_WIDTH)


def _fwd_setup_inputs(seed: int = 0) -> dict:
    key = jax.random.key(seed)
    ks = iter(jax.random.split(key, 40))
    f32 = jnp.float32

    def dense(shape, fan_in):
        return jax.random.normal(next(ks), shape, f32) * fan_in ** -0.5

    def gain(shape):
        return 1.0 + 0.02 * jax.random.normal(next(ks), shape, f32)

    x = jax.random.normal(next(ks), (BATCH, SEQ, D_MODEL), f32)
    mem = jax.random.normal(next(ks), (BATCH, N_MEM, D_MODEL), f32)
    offset = jax.random.randint(next(ks), (BATCH, 1), 0, 64, dtype=jnp.int32) * CHUNK
    positions = (offset + jnp.arange(SEQ, dtype=jnp.int32)[None, :]).astype(jnp.int32)

    ffn1_norm = gain((DEPTH, D_MODEL))
    ffn1_w_gu = dense((DEPTH, D_MODEL, 2 * D_FF), D_MODEL)
    ffn1_w_down = dense((DEPTH, D_FF, D_MODEL), D_FF)
    mix_norm = gain((DEPTH, D_MODEL))
    ffn2_norm = gain((DEPTH, D_MODEL))
    ffn2_w_gu = dense((DEPTH, D_MODEL, 2 * D_FF), D_MODEL)
    ffn2_w_down = dense((DEPTH, D_FF, D_MODEL), D_FF)
    w_out = dense((DEPTH, MIX_WIDTH, D_MODEL), MIX_WIDTH)
    mem_norm = gain((D_MODEL,))
    w_mem_kv = dense((DEPTH, D_MODEL, 2 * MEM_WIDTH), D_MODEL)

    a_w_in = dense((N_A_LAYERS, D_MODEL, A_IN), D_MODEL)
    a_conv = dense((N_A_LAYERS, CONV_K, 3 * A_WIDTH), CONV_K)
    a_A_log = jnp.log(jax.random.uniform(next(ks), (N_A_LAYERS, A_HEADS), f32, 1.0, 16.0))
    dt = jnp.exp(jax.random.uniform(next(ks), (N_A_LAYERS, A_HEADS), f32,
                                    math.log(1e-3), math.log(1e-1)))
    a_dt_bias = dt + jnp.log(-jnp.expm1(-dt))
    a_out_norm = gain((N_A_LAYERS, A_HEAD_DIM))

    b_w_in = dense((N_B_LAYERS, D_MODEL, B_IN), D_MODEL)
    b_q_norm = gain((N_B_LAYERS, Q_LORA))
    b_w_uq = dense((N_B_LAYERS, Q_LORA, B_HEADS * (QK_NOPE + QK_ROPE)), Q_LORA)

    kv_in_norm = gain((D_MODEL,))
    w_dkv = dense((D_MODEL, KV_LORA + QK_ROPE), D_MODEL)
    kv_lat_norm = gain((KV_LORA,))
    w_ukv = dense((KV_LORA, B_HEADS * (QK_NOPE + V_HEAD)), KV_LORA)
    final_norm = gain((D_MODEL,))

    return {"x": x, "mem": mem, "positions": positions,
            "ffn1_norm": ffn1_norm, "ffn1_w_gu": ffn1_w_gu, "ffn1_w_down": ffn1_w_down,
            "mix_norm": mix_norm,
            "ffn2_norm": ffn2_norm, "ffn2_w_gu": ffn2_w_gu, "ffn2_w_down": ffn2_w_down,
            "w_out": w_out, "mem_norm": mem_norm, "w_mem_kv": w_mem_kv,
            "a_w_in": a_w_in, "a_conv": a_conv, "a_A_log": a_A_log, "a_dt_bias": a_dt_bias,
            "a_out_norm": a_out_norm,
            "b_w_in": b_w_in, "b_q_norm": b_q_norm, "b_w_uq": b_w_uq,
            "kv_in_norm": kv_in_norm, "w_dkv": w_dkv, "kv_lat_norm": kv_lat_norm, "w_ukv": w_ukv,
            "final_norm": final_norm}


def _fwd_reference(x, mem, positions, ffn1_norm, ffn1_w_gu, ffn1_w_down, mix_norm,
              ffn2_norm, ffn2_w_gu, ffn2_w_down, w_out, mem_norm, w_mem_kv,
              a_w_in, a_conv, a_A_log, a_dt_bias, a_out_norm,
              b_w_in, b_q_norm, b_w_uq, kv_in_norm, w_dkv, kv_lat_norm, w_ukv, final_norm):
    b, s, _ = x.shape
    mem_n = rms_norm(mem, mem_norm)
    cos, sin = rope_tables(positions, QK_ROPE)

    for i in range(N_A_LAYERS):
        l = i
        x = x + 0.5 * swiglu(rms_norm(x, ffn1_norm[l]), ffn1_w_gu[l], ffn1_w_down[l])
        h = rms_norm(x, mix_norm[l]) @ a_w_in[i]
        qkv, gate, b_raw, a_raw, q_mem = jnp.split(
            h, [3 * A_WIDTH, 4 * A_WIDTH, 4 * A_WIDTH + A_HEADS, 4 * A_WIDTH + 2 * A_HEADS], axis=-1)
        o_a = gated_deltanet(qkv, gate, b_raw, a_raw, a_conv[i], a_A_log[i], a_dt_bias[i], a_out_norm[i])
        o_m = memory_attention(q_mem, mem_n @ w_mem_kv[l])
        x = x + jnp.concatenate([o_a, o_m], axis=-1) @ w_out[l]
        x = x + 0.5 * swiglu(rms_norm(x, ffn2_norm[l]), ffn2_w_gu[l], ffn2_w_down[l])

    ckr = rms_norm(x, kv_in_norm) @ w_dkv
    c_kv = rms_norm(ckr[..., :KV_LORA], kv_lat_norm)
    k_rope = apply_rope(ckr[..., None, KV_LORA:], cos, sin)[:, :, 0]
    k_nope, v_mla = jnp.split((c_kv @ w_ukv).reshape(b, s, B_HEADS, QK_NOPE + V_HEAD), [QK_NOPE], axis=-1)

    for j in range(N_B_LAYERS):
        l = N_A_LAYERS + j
        x = x + 0.5 * swiglu(rms_norm(x, ffn1_norm[l]), ffn1_w_gu[l], ffn1_w_down[l])
        h = rms_norm(x, mix_norm[l]) @ b_w_in[j]
        cq, q_mem = jnp.split(h, [Q_LORA], axis=-1)
        q = (rms_norm(cq, b_q_norm[j]) @ b_w_uq[j]).reshape(b, s, B_HEADS, QK_NOPE + QK_ROPE)
        q_nope, q_rope = jnp.split(q, [QK_NOPE], axis=-1)
        q_rope = apply_rope(q_rope, cos, sin)
        o_b = mla_attention(q_nope, q_rope, k_nope, k_rope, v_mla)
        o_m = memory_attention(q_mem, mem_n @ w_mem_kv[l])
        x = x + jnp.concatenate([o_b, o_m], axis=-1) @ w_out[l]
        x = x + 0.5 * swiglu(rms_norm(x, ffn2_norm[l]), ffn2_w_gu[l], ffn2_w_down[l])

    return rms_norm(x, final_norm)


import jax as _jax
import jax.numpy as _jnp

TWIN_FORMAT = 'train_step'
FWD_PARAMS = ['x', 'mem', 'positions', 'ffn1_norm', 'ffn1_w_gu', 'ffn1_w_down', 'mix_norm', 'ffn2_norm', 'ffn2_w_gu', 'ffn2_w_down', 'w_out', 'mem_norm', 'w_mem_kv', 'a_w_in', 'a_conv', 'a_A_log', 'a_dt_bias', 'a_out_norm', 'b_w_in', 'b_q_norm', 'b_w_uq', 'kv_in_norm', 'w_dkv', 'kv_lat_norm', 'w_ukv', 'final_norm']
TWIN_WEIGHTS = ['ffn1_norm', 'ffn1_w_gu', 'ffn1_w_down', 'mix_norm', 'ffn2_norm', 'ffn2_w_gu', 'ffn2_w_down', 'w_out', 'mem_norm', 'w_mem_kv', 'a_w_in', 'a_conv', 'a_A_log', 'a_dt_bias', 'a_out_norm', 'b_w_in', 'b_q_norm', 'b_w_uq', 'kv_in_norm', 'w_dkv', 'kv_lat_norm', 'w_ukv', 'final_norm']
TWIN_DIFF_INPUT = 'x'
TWIN_INPUTS = ['x', 'mem', 'positions', 'ffn1_norm', 'ffn1_w_gu', 'ffn1_w_down', 'mix_norm', 'ffn2_norm', 'ffn2_w_gu', 'ffn2_w_down', 'w_out', 'mem_norm', 'w_mem_kv', 'a_w_in', 'a_conv', 'a_A_log', 'a_dt_bias', 'a_out_norm', 'b_w_in', 'b_q_norm', 'b_w_uq', 'kv_in_norm', 'w_dkv', 'kv_lat_norm', 'w_ukv', 'final_norm', 'loss_target', 'm_ffn1_norm', 'm_ffn1_w_gu', 'm_ffn1_w_down', 'm_mix_norm', 'm_ffn2_norm', 'm_ffn2_w_gu', 'm_ffn2_w_down', 'm_w_out', 'm_mem_norm', 'm_w_mem_kv', 'm_a_w_in', 'm_a_conv', 'm_a_A_log', 'm_a_dt_bias', 'm_a_out_norm', 'm_b_w_in', 'm_b_q_norm', 'm_b_w_uq', 'm_kv_in_norm', 'm_w_dkv', 'm_kv_lat_norm', 'm_w_ukv', 'm_final_norm', 'v_ffn1_norm', 'v_ffn1_w_gu', 'v_ffn1_w_down', 'v_mix_norm', 'v_ffn2_norm', 'v_ffn2_w_gu', 'v_ffn2_w_down', 'v_w_out', 'v_mem_norm', 'v_w_mem_kv', 'v_a_w_in', 'v_a_conv', 'v_a_A_log', 'v_a_dt_bias', 'v_a_out_norm', 'v_b_w_in', 'v_b_q_norm', 'v_b_w_uq', 'v_kv_in_norm', 'v_w_dkv', 'v_kv_lat_norm', 'v_w_ukv', 'v_final_norm']
TWIN_OUTPUTS = ['loss', 'grad_x', 'grad_ffn1_norm', 'grad_ffn1_w_gu', 'grad_ffn1_w_down', 'grad_mix_norm', 'grad_ffn2_norm', 'grad_ffn2_w_gu', 'grad_ffn2_w_down', 'grad_w_out', 'grad_mem_norm', 'grad_w_mem_kv', 'grad_a_w_in', 'grad_a_conv', 'grad_a_A_log', 'grad_a_dt_bias', 'grad_a_out_norm', 'grad_b_w_in', 'grad_b_q_norm', 'grad_b_w_uq', 'grad_kv_in_norm', 'grad_w_dkv', 'grad_kv_lat_norm', 'grad_w_ukv', 'grad_final_norm', 'delta_ffn1_norm', 'delta_ffn1_w_gu', 'delta_ffn1_w_down', 'delta_mix_norm', 'delta_ffn2_norm', 'delta_ffn2_w_gu', 'delta_ffn2_w_down', 'delta_w_out', 'delta_mem_norm', 'delta_w_mem_kv', 'delta_a_w_in', 'delta_a_conv', 'delta_a_A_log', 'delta_a_dt_bias', 'delta_a_out_norm', 'delta_b_w_in', 'delta_b_q_norm', 'delta_b_w_uq', 'delta_kv_in_norm', 'delta_w_dkv', 'delta_kv_lat_norm', 'delta_w_ukv', 'delta_final_norm', 'new_m_ffn1_norm', 'new_m_ffn1_w_gu', 'new_m_ffn1_w_down', 'new_m_mix_norm', 'new_m_ffn2_norm', 'new_m_ffn2_w_gu', 'new_m_ffn2_w_down', 'new_m_w_out', 'new_m_mem_norm', 'new_m_w_mem_kv', 'new_m_a_w_in', 'new_m_a_conv', 'new_m_a_A_log', 'new_m_a_dt_bias', 'new_m_a_out_norm', 'new_m_b_w_in', 'new_m_b_q_norm', 'new_m_b_w_uq', 'new_m_kv_in_norm', 'new_m_w_dkv', 'new_m_kv_lat_norm', 'new_m_w_ukv', 'new_m_final_norm', 'new_v_ffn1_norm', 'new_v_ffn1_w_gu', 'new_v_ffn1_w_down', 'new_v_mix_norm', 'new_v_ffn2_norm', 'new_v_ffn2_w_gu', 'new_v_ffn2_w_down', 'new_v_w_out', 'new_v_mem_norm', 'new_v_w_mem_kv', 'new_v_a_w_in', 'new_v_a_conv', 'new_v_a_A_log', 'new_v_a_dt_bias', 'new_v_a_out_norm', 'new_v_b_w_in', 'new_v_b_q_norm', 'new_v_b_w_uq', 'new_v_kv_in_norm', 'new_v_w_dkv', 'new_v_kv_lat_norm', 'new_v_w_ukv', 'new_v_final_norm']
TWIN_LEAF_KINDS = {'loss': 'loss', 'grad_x': 'grad_x', 'grad_ffn1_norm': 'grad_w', 'grad_ffn1_w_gu': 'grad_w', 'grad_ffn1_w_down': 'grad_w', 'grad_mix_norm': 'grad_w', 'grad_ffn2_norm': 'grad_w', 'grad_ffn2_w_gu': 'grad_w', 'grad_ffn2_w_down': 'grad_w', 'grad_w_out': 'grad_w', 'grad_mem_norm': 'grad_w', 'grad_w_mem_kv': 'grad_w', 'grad_a_w_in': 'grad_w', 'grad_a_conv': 'grad_w', 'grad_a_A_log': 'grad_w', 'grad_a_dt_bias': 'grad_w', 'grad_a_out_norm': 'grad_w', 'grad_b_w_in': 'grad_w', 'grad_b_q_norm': 'grad_w', 'grad_b_w_uq': 'grad_w', 'grad_kv_in_norm': 'grad_w', 'grad_w_dkv': 'grad_w', 'grad_kv_lat_norm': 'grad_w', 'grad_w_ukv': 'grad_w', 'grad_final_norm': 'grad_w', 'delta_ffn1_norm': 'delta_w', 'delta_ffn1_w_gu': 'delta_w', 'delta_ffn1_w_down': 'delta_w', 'delta_mix_norm': 'delta_w', 'delta_ffn2_norm': 'delta_w', 'delta_ffn2_w_gu': 'delta_w', 'delta_ffn2_w_down': 'delta_w', 'delta_w_out': 'delta_w', 'delta_mem_norm': 'delta_w', 'delta_w_mem_kv': 'delta_w', 'delta_a_w_in': 'delta_w', 'delta_a_conv': 'delta_w', 'delta_a_A_log': 'delta_w', 'delta_a_dt_bias': 'delta_w', 'delta_a_out_norm': 'delta_w', 'delta_b_w_in': 'delta_w', 'delta_b_q_norm': 'delta_w', 'delta_b_w_uq': 'delta_w', 'delta_kv_in_norm': 'delta_w', 'delta_w_dkv': 'delta_w', 'delta_kv_lat_norm': 'delta_w', 'delta_w_ukv': 'delta_w', 'delta_final_norm': 'delta_w', 'new_m_ffn1_norm': 'new_m', 'new_m_ffn1_w_gu': 'new_m', 'new_m_ffn1_w_down': 'new_m', 'new_m_mix_norm': 'new_m', 'new_m_ffn2_norm': 'new_m', 'new_m_ffn2_w_gu': 'new_m', 'new_m_ffn2_w_down': 'new_m', 'new_m_w_out': 'new_m', 'new_m_mem_norm': 'new_m', 'new_m_w_mem_kv': 'new_m', 'new_m_a_w_in': 'new_m', 'new_m_a_conv': 'new_m', 'new_m_a_A_log': 'new_m', 'new_m_a_dt_bias': 'new_m', 'new_m_a_out_norm': 'new_m', 'new_m_b_w_in': 'new_m', 'new_m_b_q_norm': 'new_m', 'new_m_b_w_uq': 'new_m', 'new_m_kv_in_norm': 'new_m', 'new_m_w_dkv': 'new_m', 'new_m_kv_lat_norm': 'new_m', 'new_m_w_ukv': 'new_m', 'new_m_final_norm': 'new_m', 'new_v_ffn1_norm': 'new_v', 'new_v_ffn1_w_gu': 'new_v', 'new_v_ffn1_w_down': 'new_v', 'new_v_mix_norm': 'new_v', 'new_v_ffn2_norm': 'new_v', 'new_v_ffn2_w_gu': 'new_v', 'new_v_ffn2_w_down': 'new_v', 'new_v_w_out': 'new_v', 'new_v_mem_norm': 'new_v', 'new_v_w_mem_kv': 'new_v', 'new_v_a_w_in': 'new_v', 'new_v_a_conv': 'new_v', 'new_v_a_A_log': 'new_v', 'new_v_a_dt_bias': 'new_v', 'new_v_a_out_norm': 'new_v', 'new_v_b_w_in': 'new_v', 'new_v_b_q_norm': 'new_v', 'new_v_b_w_uq': 'new_v', 'new_v_kv_in_norm': 'new_v', 'new_v_w_dkv': 'new_v', 'new_v_kv_lat_norm': 'new_v', 'new_v_w_ukv': 'new_v', 'new_v_final_norm': 'new_v'}


def _forward(args):
    return _fwd_reference(*[args[k] for k in FWD_PARAMS])


def _output_shape():
    def fwd():
        inp = _fwd_setup_inputs(0)
        return _fwd_reference(*[inp[k] for k in FWD_PARAMS])
    out = _jax.eval_shape(fwd)
    return out.shape, out.dtype

N_MICROBATCH = 1
ADAM_LR = 0.001
ADAM_B1 = 0.9
ADAM_B2 = 0.999
ADAM_EPS = 1e-08
ADAM_WD = 0.01
ADAM_STEP = 10
PER_EXAMPLE_BATCH_AXIS = {'x': 0, 'mem': 0, 'positions': 0, 'loss_target': 0}
SHARED_INPUTS = []
_WEIGHT_DTYPES = {'ffn1_norm': _jnp.float32, 'ffn1_w_gu': _jnp.float32, 'ffn1_w_down': _jnp.float32, 'mix_norm': _jnp.float32, 'ffn2_norm': _jnp.float32, 'ffn2_w_gu': _jnp.float32, 'ffn2_w_down': _jnp.float32, 'w_out': _jnp.float32, 'mem_norm': _jnp.float32, 'w_mem_kv': _jnp.float32, 'a_w_in': _jnp.float32, 'a_conv': _jnp.float32, 'a_A_log': _jnp.float32, 'a_dt_bias': _jnp.float32, 'a_out_norm': _jnp.float32, 'b_w_in': _jnp.float32, 'b_q_norm': _jnp.float32, 'b_w_uq': _jnp.float32, 'kv_in_norm': _jnp.float32, 'w_dkv': _jnp.float32, 'kv_lat_norm': _jnp.float32, 'w_ukv': _jnp.float32, 'final_norm': _jnp.float32}
MOMENT_SCALE = {'ffn1_norm': 1.610796e-01, 'ffn1_w_gu': 6.775832e-02, 'ffn1_w_down': 1.104903e-01, 'mix_norm': 2.186098e-01, 'ffn2_norm': 1.340431e-01, 'ffn2_w_gu': 5.431243e-02, 'ffn2_w_down': 8.870276e-02, 'w_out': 1.335369e-01, 'mem_norm': 5.389078e-02, 'w_mem_kv': 3.440268e-02, 'a_w_in': 1.699111e-01, 'a_conv': 1.641254e-01, 'a_A_log': 8.339469e-01, 'a_dt_bias': 8.010430e-01, 'a_out_norm': 6.006784e-01, 'b_w_in': 4.966307e-02, 'b_q_norm': 6.529199e-02, 'b_w_uq': 3.051240e-02, 'kv_in_norm': 7.139258e-02, 'w_dkv': 1.295788e-01, 'kv_lat_norm': 1.474019e-01, 'w_ukv': 5.844171e-02, 'final_norm': 1.281085e+02}


def _to_microbatches(a, axis):
    t = _jnp.moveaxis(a, axis, 0)
    t = t.reshape((N_MICROBATCH, t.shape[0] // N_MICROBATCH) + t.shape[1:])
    return _jnp.moveaxis(t, 1, axis + 1)


def setup_inputs(seed: int = 0) -> dict:
    inp = _fwd_setup_inputs(seed)
    key = _jax.random.fold_in(_jax.random.key(seed), 7919)
    shape, _ = _output_shape()
    out = dict(inp)
    out["loss_target"] = _jax.random.normal(_jax.random.fold_in(key, 0), shape, _jnp.float32)
    for i, name in enumerate(TWIN_WEIGHTS):
        w = inp[name].astype(_jnp.float32)
        if MOMENT_SCALE is None:
            s = _jnp.sqrt(_jnp.mean(_jnp.square(w)) + 1e-30)
        else:
            s = MOMENT_SCALE[name]
        km, kv = _jax.random.split(_jax.random.fold_in(key, i + 1))
        out[name] = w
        out["m_" + name] = s * _jax.random.normal(km, w.shape, _jnp.float32)
        out["v_" + name] = (s * s) * _jax.random.uniform(kv, w.shape, _jnp.float32, 0.5, 1.5)
    if N_MICROBATCH > 1:
        for name, axis in PER_EXAMPLE_BATCH_AXIS.items():
            out[name] = _to_microbatches(out[name], axis)
    return {'x': out['x'], 'mem': out['mem'], 'positions': out['positions'], 'ffn1_norm': out['ffn1_norm'], 'ffn1_w_gu': out['ffn1_w_gu'], 'ffn1_w_down': out['ffn1_w_down'], 'mix_norm': out['mix_norm'], 'ffn2_norm': out['ffn2_norm'], 'ffn2_w_gu': out['ffn2_w_gu'], 'ffn2_w_down': out['ffn2_w_down'], 'w_out': out['w_out'], 'mem_norm': out['mem_norm'], 'w_mem_kv': out['w_mem_kv'], 'a_w_in': out['a_w_in'], 'a_conv': out['a_conv'], 'a_A_log': out['a_A_log'], 'a_dt_bias': out['a_dt_bias'], 'a_out_norm': out['a_out_norm'], 'b_w_in': out['b_w_in'], 'b_q_norm': out['b_q_norm'], 'b_w_uq': out['b_w_uq'], 'kv_in_norm': out['kv_in_norm'], 'w_dkv': out['w_dkv'], 'kv_lat_norm': out['kv_lat_norm'], 'w_ukv': out['w_ukv'], 'final_norm': out['final_norm'], 'loss_target': out['loss_target'], 'm_ffn1_norm': out['m_ffn1_norm'], 'm_ffn1_w_gu': out['m_ffn1_w_gu'], 'm_ffn1_w_down': out['m_ffn1_w_down'], 'm_mix_norm': out['m_mix_norm'], 'm_ffn2_norm': out['m_ffn2_norm'], 'm_ffn2_w_gu': out['m_ffn2_w_gu'], 'm_ffn2_w_down': out['m_ffn2_w_down'], 'm_w_out': out['m_w_out'], 'm_mem_norm': out['m_mem_norm'], 'm_w_mem_kv': out['m_w_mem_kv'], 'm_a_w_in': out['m_a_w_in'], 'm_a_conv': out['m_a_conv'], 'm_a_A_log': out['m_a_A_log'], 'm_a_dt_bias': out['m_a_dt_bias'], 'm_a_out_norm': out['m_a_out_norm'], 'm_b_w_in': out['m_b_w_in'], 'm_b_q_norm': out['m_b_q_norm'], 'm_b_w_uq': out['m_b_w_uq'], 'm_kv_in_norm': out['m_kv_in_norm'], 'm_w_dkv': out['m_w_dkv'], 'm_kv_lat_norm': out['m_kv_lat_norm'], 'm_w_ukv': out['m_w_ukv'], 'm_final_norm': out['m_final_norm'], 'v_ffn1_norm': out['v_ffn1_norm'], 'v_ffn1_w_gu': out['v_ffn1_w_gu'], 'v_ffn1_w_down': out['v_ffn1_w_down'], 'v_mix_norm': out['v_mix_norm'], 'v_ffn2_norm': out['v_ffn2_norm'], 'v_ffn2_w_gu': out['v_ffn2_w_gu'], 'v_ffn2_w_down': out['v_ffn2_w_down'], 'v_w_out': out['v_w_out'], 'v_mem_norm': out['v_mem_norm'], 'v_w_mem_kv': out['v_w_mem_kv'], 'v_a_w_in': out['v_a_w_in'], 'v_a_conv': out['v_a_conv'], 'v_a_A_log': out['v_a_A_log'], 'v_a_dt_bias': out['v_a_dt_bias'], 'v_a_out_norm': out['v_a_out_norm'], 'v_b_w_in': out['v_b_w_in'], 'v_b_q_norm': out['v_b_q_norm'], 'v_b_w_uq': out['v_b_w_uq'], 'v_kv_in_norm': out['v_kv_in_norm'], 'v_w_dkv': out['v_w_dkv'], 'v_kv_lat_norm': out['v_kv_lat_norm'], 'v_w_ukv': out['v_w_ukv'], 'v_final_norm': out['v_final_norm']}


def _loss(weights, diff, rest, loss_target):
    with _jax.named_scope("forward"):
        args = {**rest, TWIN_DIFF_INPUT: diff, **{k: w.astype(_WEIGHT_DTYPES[k]) for k, w in weights.items()}}
        y = _forward(args)
    with _jax.named_scope("loss_head"):
        err = _jnp.square(y.astype(_jnp.float32) - loss_target)
        return 0.5 * _jnp.sum(_jnp.mean(err, axis=-1)) if err.ndim else 0.5 * err


def _adamw(w, g, m, v):
    m = ADAM_B1 * m + (1.0 - ADAM_B1) * g
    v = ADAM_B2 * v + (1.0 - ADAM_B2) * _jnp.square(g)
    m_hat = m / (1.0 - ADAM_B1 ** ADAM_STEP)
    v_hat = v / (1.0 - ADAM_B2 ** ADAM_STEP)
    delta = -ADAM_LR * (m_hat / (_jnp.sqrt(v_hat) + ADAM_EPS) + ADAM_WD * w)
    return delta, m, v


def reference(x, mem, positions, ffn1_norm, ffn1_w_gu, ffn1_w_down, mix_norm, ffn2_norm, ffn2_w_gu, ffn2_w_down, w_out, mem_norm, w_mem_kv, a_w_in, a_conv, a_A_log, a_dt_bias, a_out_norm, b_w_in, b_q_norm, b_w_uq, kv_in_norm, w_dkv, kv_lat_norm, w_ukv, final_norm, loss_target, m_ffn1_norm, m_ffn1_w_gu, m_ffn1_w_down, m_mix_norm, m_ffn2_norm, m_ffn2_w_gu, m_ffn2_w_down, m_w_out, m_mem_norm, m_w_mem_kv, m_a_w_in, m_a_conv, m_a_A_log, m_a_dt_bias, m_a_out_norm, m_b_w_in, m_b_q_norm, m_b_w_uq, m_kv_in_norm, m_w_dkv, m_kv_lat_norm, m_w_ukv, m_final_norm, v_ffn1_norm, v_ffn1_w_gu, v_ffn1_w_down, v_mix_norm, v_ffn2_norm, v_ffn2_w_gu, v_ffn2_w_down, v_w_out, v_mem_norm, v_w_mem_kv, v_a_w_in, v_a_conv, v_a_A_log, v_a_dt_bias, v_a_out_norm, v_b_w_in, v_b_q_norm, v_b_w_uq, v_kv_in_norm, v_w_dkv, v_kv_lat_norm, v_w_ukv, v_final_norm):
    given = dict(x=x, mem=mem, positions=positions, ffn1_norm=ffn1_norm, ffn1_w_gu=ffn1_w_gu, ffn1_w_down=ffn1_w_down, mix_norm=mix_norm, ffn2_norm=ffn2_norm, ffn2_w_gu=ffn2_w_gu, ffn2_w_down=ffn2_w_down, w_out=w_out, mem_norm=mem_norm, w_mem_kv=w_mem_kv, a_w_in=a_w_in, a_conv=a_conv, a_A_log=a_A_log, a_dt_bias=a_dt_bias, a_out_norm=a_out_norm, b_w_in=b_w_in, b_q_norm=b_q_norm, b_w_uq=b_w_uq, kv_in_norm=kv_in_norm, w_dkv=w_dkv, kv_lat_norm=kv_lat_norm, w_ukv=w_ukv, final_norm=final_norm, loss_target=loss_target, m_ffn1_norm=m_ffn1_norm, m_ffn1_w_gu=m_ffn1_w_gu, m_ffn1_w_down=m_ffn1_w_down, m_mix_norm=m_mix_norm, m_ffn2_norm=m_ffn2_norm, m_ffn2_w_gu=m_ffn2_w_gu, m_ffn2_w_down=m_ffn2_w_down, m_w_out=m_w_out, m_mem_norm=m_mem_norm, m_w_mem_kv=m_w_mem_kv, m_a_w_in=m_a_w_in, m_a_conv=m_a_conv, m_a_A_log=m_a_A_log, m_a_dt_bias=m_a_dt_bias, m_a_out_norm=m_a_out_norm, m_b_w_in=m_b_w_in, m_b_q_norm=m_b_q_norm, m_b_w_uq=m_b_w_uq, m_kv_in_norm=m_kv_in_norm, m_w_dkv=m_w_dkv, m_kv_lat_norm=m_kv_lat_norm, m_w_ukv=m_w_ukv, m_final_norm=m_final_norm, v_ffn1_norm=v_ffn1_norm, v_ffn1_w_gu=v_ffn1_w_gu, v_ffn1_w_down=v_ffn1_w_down, v_mix_norm=v_mix_norm, v_ffn2_norm=v_ffn2_norm, v_ffn2_w_gu=v_ffn2_w_gu, v_ffn2_w_down=v_ffn2_w_down, v_w_out=v_w_out, v_mem_norm=v_mem_norm, v_w_mem_kv=v_w_mem_kv, v_a_w_in=v_a_w_in, v_a_conv=v_a_conv, v_a_A_log=v_a_A_log, v_a_dt_bias=v_a_dt_bias, v_a_out_norm=v_a_out_norm, v_b_w_in=v_b_w_in, v_b_q_norm=v_b_q_norm, v_b_w_uq=v_b_w_uq, v_kv_in_norm=v_kv_in_norm, v_w_dkv=v_w_dkv, v_kv_lat_norm=v_kv_lat_norm, v_w_ukv=v_w_ukv, v_final_norm=v_final_norm)
    weights = {n: given[n] for n in TWIN_WEIGHTS}
    shared = {n: given[n] for n in SHARED_INPUTS}
    per_example = {n: given[n] for n in ['x', 'mem', 'positions']}
    grad_fn = _jax.value_and_grad(_loss, argnums=(0, 1))

    def one_microbatch(ex, loss_target):
        ex = dict(ex)
        diff = ex.pop(TWIN_DIFF_INPUT)
        return grad_fn(weights, diff, {**shared, **ex}, loss_target)

    if N_MICROBATCH == 1:
        loss, (grad_w, grad_x) = one_microbatch(per_example, given["loss_target"])
    else:
        def body(carry, xs):
            loss_sum, grad_sum = carry
            l_k, (gw_k, gx_k) = one_microbatch(xs[0], xs[1])
            with _jax.named_scope("update"):
                return (loss_sum + l_k, _jax.tree.map(_jnp.add, grad_sum, gw_k)), gx_k

        init = (_jnp.zeros((), _jnp.float32), _jax.tree.map(_jnp.zeros_like, weights))
        (loss, grad_w), grad_x = _jax.lax.scan(body, init, (per_example, given["loss_target"]))
    with _jax.named_scope("update"):
        delta_w, new_m, new_v = {}, {}, {}
        for n in TWIN_WEIGHTS:
            delta_w[n], new_m[n], new_v[n] = _adamw(weights[n], grad_w[n], given["m_" + n], given["v_" + n])
    return (loss, grad_x, *[grad_w[n] for n in TWIN_WEIGHTS], *[delta_w[n] for n in TWIN_WEIGHTS],
            *[new_m[n] for n in TWIN_WEIGHTS], *[new_v[n] for n in TWIN_WEIGHTS])
```

```python
import functools
import math

import jax
import jax.numpy as jnp
from jax import lax
from jax.experimental import pallas as pl
from jax.experimental.pallas import tpu as pltpu

_BF = jnp.bfloat16
_F32 = jnp.float32
_HI = lax.Precision.HIGHEST
_MESH = pl.DeviceIdType.MESH

D_MODEL = 1024
DEPTH = 4
N_A = 2
N_B = 2
CHUNK = 64
EPS = 1e-6
HEADS = 6
HEAD_DIM = 128
A_WIDTH = HEADS * HEAD_DIM
CONV_K = 4
QK_ROPE = 64
Q_LORA = 256
KV_LORA = 256
N_MEM = 256
MEM_HEADS = 4
MEM_HEAD_DIM = 64
MEM_WIDTH = MEM_HEADS * MEM_HEAD_DIM
D_FF = 2816
ROPE_THETA = 10000.0
A_IN = 4 * A_WIDTH + 2 * HEADS + MEM_WIDTH
A_IN_PAD = 3456
UQ_PAD = 2 * A_WIDTH
DKV_PAD = KV_LORA + 128
LANE = 128
SUBLANE = 8

ADAM_LR = 0.001
ADAM_B1 = 0.9
ADAM_B2 = 0.999
ADAM_EPS = 1e-08
ADAM_WD = 0.01
ADAM_STEP = 10

N_CHIPS = 4
N_DEV = 8
PACK_COLS = 1024
PACK_ROW_TILE = 256


def _pcall(body, **kw):
    return pl.pallas_call(body, **kw)


VMEM_LIMIT_V7X = 48 * 2 ** 20
TILE_BYTES = 6 * 2 ** 20


def _cp(grid_rank):
    return pltpu.CompilerParams(dimension_semantics=("arbitrary",) * grid_rank, vmem_limit_bytes=VMEM_LIMIT_V7X)


def _fit_rows(rows, row_bytes):
    while rows > LANE and rows * row_bytes > TILE_BYTES:
        rows //= 2
    return rows


def _fit_cols(n, target, col_bytes):
    return _tile(n, max(LANE, min(target, TILE_BYTES // col_bytes)))


def _tile(n, target):
    best = None
    for t in range(LANE, min(n, target) + 1, LANE):
        if n % t == 0:
            best = t
    return best if best is not None else n


def _dot(a, b):
    return jnp.dot(a.astype(_BF), b.astype(_BF), preferred_element_type=_F32)


def _dot_nt(a, b):
    return lax.dot_general(a.astype(_BF), b.astype(_BF), (((1,), (1,)), ((), ())), preferred_element_type=_F32)


def _dot_tn(a, b):
    return lax.dot_general(a.astype(_BF), b.astype(_BF), (((0,), (0,)), ((), ())), preferred_element_type=_F32)


def _dot_hi(a, b):
    return jnp.dot(a, b, precision=_HI, preferred_element_type=_F32)


def _rowmap(fn, tiles, params, outs, accs=(), *, tm, name, carry=(), reverse=False):
    rows = tiles[0][0].shape[0]
    steps = rows // tm
    nt, npar, no, na, nc = len(tiles), len(params), len(outs), len(accs), len(carry)

    def step_index(i):
        return steps - 1 - i if reverse else i

    in_specs, operands = [], []
    for arr, r, w, cb in tiles:
        r = tm if r is None else r
        w = arr.shape[1] if w is None else w
        assert arr.shape[0] == steps * r and (w % LANE == 0 or w == arr.shape[1]), (name, arr.shape, r, w)
        in_specs.append(pl.BlockSpec((r, w), functools.partial(lambda i, cb: (step_index(i), cb), cb=cb)))
        operands.append(arr)
    for p in params:
        in_specs.append(pl.BlockSpec(p.shape, functools.partial(lambda i, nd: (0,) * nd, nd=p.ndim)))
        operands.append(p)
    out_specs, out_shape = [], []
    for r, cols, dt in outs:
        r = tm if r is None else r
        out_specs.append(pl.BlockSpec((r, cols), lambda i: (step_index(i), 0)))
        out_shape.append(jax.ShapeDtypeStruct((steps * r, cols), dt))
    for shp, dt in accs:
        out_specs.append(pl.BlockSpec(shp, functools.partial(lambda i, nd: (0,) * nd, nd=len(shp))))
        out_shape.append(jax.ShapeDtypeStruct(shp, dt))

    def body(*refs):
        t_refs = refs[:nt]
        p_refs = refs[nt:nt + npar]
        o_refs = refs[nt + npar:nt + npar + no]
        a_refs = refs[nt + npar + no:nt + npar + no + na]
        c_refs = refs[nt + npar + no + na:]
        if na or nc:
            @pl.when(pl.program_id(0) == 0)
            def _():
                for r in a_refs + c_refs:
                    r[...] = jnp.zeros(r.shape, r.dtype)
        vals = fn(*[r[...] for r in t_refs], *[r[...] for r in p_refs], *[r[...] for r in c_refs])
        vals = tuple(vals) if isinstance(vals, (tuple, list)) else (vals,)
        assert len(vals) == no + na + nc, (name, len(vals), no, na, nc)
        for r, v in zip(o_refs, vals[:no]):
            r[...] = v.astype(r.dtype)
        for r, v in zip(a_refs, vals[no:no + na]):
            r[...] += v.astype(r.dtype)
        for r, v in zip(c_refs, vals[no + na:]):
            r[...] = v.astype(r.dtype)

    res = _pcall(
        body, name=name, grid=(steps,), in_specs=in_specs, out_specs=out_specs, out_shape=out_shape,
        scratch_shapes=[pltpu.VMEM(shp, dt) for shp, dt in carry],
        compiler_params=_cp(1),
    )(*operands)
    return res


def _t(arr, width=None, cb=0, rows=None):
    return (arr, rows, width, cb)


def _mm_nn(a, b, *, out_dtype, name, scale=None, res=None, tm=1024, tn=1536):
    m, k = a.shape
    n = b.shape[1]
    tm, tn = _fit_rows(min(tm, m), k * a.dtype.itemsize), _fit_cols(n, tn, k * b.dtype.itemsize)

    def body(a_ref, b_ref, *rest):
        acc = _dot(a_ref[...], b_ref[...])
        if scale is not None:
            acc = acc * scale
        if res is not None:
            acc = acc + rest[0][...]
        rest[-1][...] = acc.astype(rest[-1].dtype)

    in_specs = [pl.BlockSpec((tm, k), lambda i, j: (i, 0)), pl.BlockSpec((k, tn), lambda i, j: (0, j))]
    operands = [a, b]
    if res is not None:
        in_specs.append(pl.BlockSpec((tm, tn), lambda i, j: (i, j)))
        operands.append(res)
    return _pcall(
        body, name=name, grid=(m // tm, n // tn), in_specs=in_specs,
        out_specs=pl.BlockSpec((tm, tn), lambda i, j: (i, j)), out_shape=jax.ShapeDtypeStruct((m, n), out_dtype),
        compiler_params=_cp(2),
    )(*operands)


def _mm_nt(a, b, *, out_dtype, name, scale=None, tm=1024, tn=1536):
    m, k = a.shape
    n = b.shape[0]
    tm, tn = _fit_rows(min(tm, m), k * a.dtype.itemsize), _fit_cols(n, tn, k * b.dtype.itemsize)

    def body(a_ref, b_ref, o_ref):
        acc = _dot_nt(a_ref[...], b_ref[...])
        if scale is not None:
            acc = acc * scale
        o_ref[...] = acc.astype(o_ref.dtype)

    return _pcall(
        body, name=name, grid=(m // tm, n // tn),
        in_specs=[pl.BlockSpec((tm, k), lambda i, j: (i, 0)), pl.BlockSpec((tn, k), lambda i, j: (j, 0))],
        out_specs=pl.BlockSpec((tm, tn), lambda i, j: (i, j)), out_shape=jax.ShapeDtypeStruct((m, n), out_dtype),
        compiler_params=_cp(2),
    )(a, b)


def _mm_tn(a, b, *, name, scale=None, t1=1024, tn=1536, ts=1024):
    s, k1 = a.shape
    n = b.shape[1]
    t1, tn, ts = _tile(k1, t1), _tile(n, tn), min(ts, s)
    steps = s // ts

    def body(a_ref, b_ref, o_ref):
        @pl.when(pl.program_id(2) == 0)
        def _():
            o_ref[...] = jnp.zeros(o_ref.shape, o_ref.dtype)

        o_ref[...] += _dot_tn(a_ref[...], b_ref[...])
        if scale is not None:
            @pl.when(pl.program_id(2) == steps - 1)
            def _():
                o_ref[...] = o_ref[...] * scale

    return _pcall(
        body, name=name, grid=(k1 // t1, n // tn, steps),
        in_specs=[pl.BlockSpec((ts, t1), lambda i, j, r: (r, i)), pl.BlockSpec((ts, tn), lambda i, j, r: (r, j))],
        out_specs=pl.BlockSpec((t1, tn), lambda i, j, r: (i, j)), out_shape=jax.ShapeDtypeStruct((k1, n), _F32),
        compiler_params=_cp(3),
    )(a, b)


def _heads(t, n, w=LANE):
    return [t[:, w * h:w * (h + 1)] for h in range(n)]


def _cat(parts):
    return jnp.concatenate(parts, axis=1)


def _rms(x, g):
    return x * lax.rsqrt(jnp.mean(x * x, axis=-1, keepdims=True) + EPS) * g


def _l2n(x):
    return x * lax.rsqrt(jnp.sum(x * x, axis=-1, keepdims=True) + EPS)


def _sigmoid(x):
    return 0.5 * (jnp.tanh(0.5 * x) + 1.0)


def _silu(x):
    return x * _sigmoid(x)


def _softplus(x):
    return jnp.maximum(x, 0.0) + jnp.log(1.0 + jnp.exp(-jnp.abs(x)))


def _lane_pick(t, h):
    lane = lax.broadcasted_iota(jnp.int32, t.shape, 1)
    return jnp.sum(jnp.where(lane == h, t, 0.0), axis=1, keepdims=True)


def _lane_put(col, h, width=LANE):
    lane = lax.broadcasted_iota(jnp.int32, (col.shape[0], width), 1)
    return jnp.where(lane == h, col, 0.0)


def _vjp(fwd, ins, cts):
    outs, pull = jax.vjp(fwd, *ins)
    outs = outs if isinstance(outs, (tuple, list)) else (outs,)
    cts = tuple(c.astype(o.dtype) for c, o in zip(cts, outs))
    return pull(cts if len(cts) > 1 else cts[0])


def _rot_half_matrix():
    r = lax.broadcasted_iota(jnp.int32, (LANE, LANE), 0)
    c = lax.broadcasted_iota(jnp.int32, (LANE, LANE), 1)
    half = QK_ROPE // 2
    return jnp.where((c < half) & (r == c + half), -1.0, jnp.where((c >= half) & (c < QK_ROPE) & (r == c - half), 1.0, 0.0))


def _rope(x, cos_t, sin_t):
    return x * cos_t + _dot_hi(x, _rot_half_matrix()) * sin_t


def _mem_attn(q, km, vm):
    lane_q = lax.broadcasted_iota(jnp.int32, q.shape, 1)
    lane_v = lax.broadcasted_iota(jnp.int32, vm.shape, 1)
    out = jnp.zeros(q.shape, _F32)
    for h in range(MEM_HEADS):
        lo, hi = MEM_HEAD_DIM * h, MEM_HEAD_DIM * (h + 1)
        qh = jnp.where((lane_q >= lo) & (lane_q < hi), q, 0.0)
        vh = jnp.where((lane_v >= lo) & (lane_v < hi), vm, 0.0)
        sc = _dot_nt(qh, km) * MEM_HEAD_DIM ** -0.5
        sc = sc - lax.stop_gradient(jnp.max(sc, axis=-1, keepdims=True))
        p = jnp.exp(sc)
        p = p / jnp.sum(p, axis=-1, keepdims=True)
        out = out + _dot(p, vh)
    return out


_PAIR = 2 * CHUNK


def _pair_masks():
    ri = lax.broadcasted_iota(jnp.int32, (_PAIR, _PAIR), 0)
    ci = lax.broadcasted_iota(jnp.int32, (_PAIR, _PAIR), 1)
    same = (ri >= CHUNK) == (ci >= CHUNK)
    return same, same & (ri >= ci), same & (ri > ci), ri == ci, same & (ri <= ci)


def _gdn_intra_head(q, k, v, beta, gl):
    same, causal, strict, eye, upper = _pair_masks()
    gl_row = jnp.sum(jnp.where(eye, gl, 0.0), axis=0, keepdims=True)
    g_col = jnp.sum(jnp.where(causal, gl_row, 0.0), axis=1, keepdims=True)
    g_row = jnp.sum(jnp.where(upper, gl, 0.0), axis=0, keepdims=True)
    g_last = jnp.sum(jnp.where(same, gl_row, 0.0), axis=1, keepdims=True)
    decay = jnp.where(causal, jnp.exp(jnp.where(causal, g_col - g_row, 0.0)), 0.0)
    kb = k * beta
    a = jnp.where(strict, _dot_nt(kb, k) * decay, 0.0)
    n = -a
    eye_f = jnp.where(eye, 1.0, 0.0)
    t_inv = eye_f + n
    for _ in range(5):
        n = _dot_hi(n, n)
        t_inv = t_inv + _dot_hi(t_inv, n)
    e_g = jnp.exp(g_col)
    u = _dot_hi(t_inv, v * beta)
    w = _dot_hi(t_inv, kb * e_g)
    qk = _dot_nt(q, k) * decay
    return w, u, q * e_g, k * jnp.exp(g_last - g_col), qk, jnp.exp(g_last)


def _gdn_scan_head(s, qd_a, kd_a, w_a, u_a, qk_a, dc_a, qd_b, kd_b, w_b, u_b, qk_b, dc_b):
    zeros = jnp.zeros((CHUNK, HEAD_DIM), _F32)
    vn_a = u_a - _dot(w_a, s)
    o_a = _dot(qd_a, s) + _dot(qk_a, jnp.concatenate([vn_a, zeros], axis=0))
    s1 = s * dc_a + _dot_tn(kd_a, vn_a)
    vn_b = u_b - _dot(w_b, s1)
    o_b = _dot(qd_b, s1) + _dot(qk_b, jnp.concatenate([zeros, vn_b], axis=0))
    s2 = s1 * dc_b + _dot_tn(kd_b, vn_b)
    return o_a, o_b, s2


def _pick_scalar(t, row, lane_i):
    ri = lax.broadcasted_iota(jnp.int32, t.shape, 0)
    ci = lax.broadcasted_iota(jnp.int32, t.shape, 1)
    return jnp.sum(jnp.sum(jnp.where((ri == row) & (ci == lane_i), t, 0.0), axis=1, keepdims=True), axis=0, keepdims=True)


def _put_scalar(val, row, lane_i, shape):
    ri = lax.broadcasted_iota(jnp.int32, shape, 0)
    ci = lax.broadcasted_iota(jnp.int32, shape, 1)
    return jnp.where((ri == row) & (ci == lane_i), val, 0.0)


def _scan_split(qd, kd, w, u, qk, dcrow, h):
    sl = slice(HEAD_DIM * h, HEAD_DIM * (h + 1))
    args = []
    for r0 in (0, CHUNK):
        rs = slice(r0, r0 + CHUNK)
        args += [qd[rs, sl], kd[rs, sl], w[rs, sl], u[rs, sl], qk[rs, sl], _pick_scalar(dcrow, r0, h)]
    return args


def _gdn_scan_fwd_fn(qd, kd, w, u, qk, dcrow, state):
    outs, new_state = [], []
    for h in range(HEADS):
        s = state[HEAD_DIM * h:HEAD_DIM * (h + 1), :]
        o_a, o_b, s2 = _gdn_scan_head(s, *_scan_split(qd, kd, w, u, qk, dcrow, h))
        outs.append(jnp.concatenate([o_a, o_b], axis=0))
        new_state.append(s2)
    return _cat(outs), state, jnp.concatenate(new_state, axis=0)


def _gdn_scan_bwd_fn(qd, kd, w, u, qk, dcrow, state, d_o, d_state):
    grads = [[] for _ in range(5)]
    d_dcrow = jnp.zeros(dcrow.shape, _F32)
    new_d_state = []
    for h in range(HEADS):
        sl = slice(HEAD_DIM * h, HEAD_DIM * (h + 1))
        s = state[sl, :]
        ins = [s] + _scan_split(qd, kd, w, u, qk, dcrow, h)
        cts = (d_o[0:CHUNK, sl], d_o[CHUNK:_PAIR, sl], d_state[sl, :])
        g = _vjp(_gdn_scan_head, ins, cts)
        new_d_state.append(g[0])
        for t in range(5):
            grads[t].append(jnp.concatenate([g[1 + t], g[7 + t]], axis=0))
        d_dcrow = d_dcrow + _put_scalar(g[6], 0, h, dcrow.shape) + _put_scalar(g[12], CHUNK, h, dcrow.shape)
    return tuple(_cat(gr) for gr in grads) + (d_dcrow, jnp.concatenate(new_d_state, axis=0))


def _gdn_intra_fwd_fn(q, k, v, bg):
    outs = [[] for _ in range(5)]
    dcrow = jnp.zeros(bg.shape, _F32)
    for h in range(HEADS):
        sl = slice(HEAD_DIM * h, HEAD_DIM * (h + 1))
        res = _gdn_intra_head(q[:, sl], k[:, sl], v[:, sl], _lane_pick(bg, h), _lane_pick(bg, HEADS + h))
        for t in range(5):
            outs[t].append(res[t])
        dcrow = dcrow + _lane_put(res[5], h)
    return tuple(_cat(o) for o in outs) + (dcrow,)


def _gdn_intra_bwd_fn(q, k, v, bg, d_w, d_u, d_qd, d_kd, d_qk, d_dcrow):
    grads = [[] for _ in range(3)]
    d_bg = jnp.zeros(bg.shape, _F32)
    for h in range(HEADS):
        sl = slice(HEAD_DIM * h, HEAD_DIM * (h + 1))
        ins = [q[:, sl], k[:, sl], v[:, sl], _lane_pick(bg, h), _lane_pick(bg, HEADS + h)]
        cts = (d_w[:, sl], d_u[:, sl], d_qd[:, sl], d_kd[:, sl], d_qk[:, sl], _lane_pick(d_dcrow, h))
        g = _vjp(_gdn_intra_head, ins, cts)
        for t in range(3):
            grads[t].append(g[t])
        d_bg = d_bg + _lane_put(g[3], h) + _lane_put(g[4], HEADS + h)
    return tuple(_cat(gr) for gr in grads) + (d_bg,)


def _gdn_gates(ba, alog, dtb):
    lane = lax.broadcasted_iota(jnp.int32, ba.shape, 1)
    beta = _sigmoid(ba)
    g = -jnp.exp(alog) * _softplus(ba + dtb)
    return jnp.where(lane < HEADS, beta, jnp.where(lane < 2 * HEADS, g, 0.0))


def _gdn_q_head(c):
    return _l2n(_silu(c)) * HEAD_DIM ** -0.5


def _gdn_k_head(c):
    return _l2n(_silu(c))


def _gdn_prep_fwd_fn(qkv_c, ba, alog, dtb):
    hs = _heads(qkv_c, 3 * HEADS)
    q = _cat([_gdn_q_head(c) for c in hs[:HEADS]])
    k = _cat([_gdn_k_head(c) for c in hs[HEADS:2 * HEADS]])
    v = _cat([_silu(c) for c in hs[2 * HEADS:]])
    return q, k, v, _gdn_gates(ba, alog, dtb)


def _gdn_prep_bwd_fn(qkv_c, ba, d_q, d_k, d_v, d_bg, alog, dtb):
    hs = _heads(qkv_c, 3 * HEADS)
    dqs, dks, dvs = _heads(d_q, HEADS), _heads(d_k, HEADS), _heads(d_v, HEADS)
    parts = [_vjp(_gdn_q_head, [hs[h]], (dqs[h],))[0] for h in range(HEADS)]
    parts += [_vjp(_gdn_k_head, [hs[HEADS + h]], (dks[h],))[0] for h in range(HEADS)]
    parts += [_vjp(_silu, [hs[2 * HEADS + h]], (dvs[h],))[0] for h in range(HEADS)]
    d_ba, d_alog, d_dtb = _vjp(_gdn_gates, [ba, alog, dtb], (d_bg,))
    return _cat(parts), d_ba, d_alog, d_dtb


def _a_out_head(o, gate, gain):
    return _rms(o, gain) * _silu(gate)


def _a_post_fwd_fn(o, gate, qm, gain, mem_kv):
    parts = [_a_out_head(oh, gh, gain) for oh, gh in zip(_heads(o, HEADS), _heads(gate, HEADS))]
    parts.append(_mem_attn(qm, mem_kv[:, :MEM_WIDTH], mem_kv[:, MEM_WIDTH:]))
    return (_cat(parts),)


def _a_post_bwd_fn(o, gate, qm, d_cat, gain, mem_kv):
    d_os, d_gates = [], []
    d_gain = jnp.zeros(gain.shape, _F32)
    dc = _heads(d_cat, HEADS + 2)
    for h, (oh, gh) in enumerate(zip(_heads(o, HEADS), _heads(gate, HEADS))):
        g = _vjp(_a_out_head, [oh, gh, gain], (dc[h],))
        d_os.append(g[0])
        d_gates.append(g[1])
        d_gain = d_gain + g[2]
    d_qm, d_km, d_vm = _vjp(_mem_attn, [qm, mem_kv[:, :MEM_WIDTH], mem_kv[:, MEM_WIDTH:]], (d_cat[:, A_WIDTH:],))
    return _cat(d_os), _cat(d_gates + [d_qm]), d_gain, _cat([d_km, d_vm])


def _b_post_fwd_fn(o, qm, mem_kv):
    return (_cat([o.astype(_F32), _mem_attn(qm, mem_kv[:, :MEM_WIDTH], mem_kv[:, MEM_WIDTH:])]),)


def _b_post_bwd_fn(qm, d_cat_m, mem_kv):
    d_qm, d_km, d_vm = _vjp(_mem_attn, [qm, mem_kv[:, :MEM_WIDTH], mem_kv[:, MEM_WIDTH:]], (d_cat_m,))
    return d_qm, _cat([d_km, d_vm])


def _rope_q_fwd_fn(qf, cos_t, sin_t):
    hs = _heads(qf, 2 * HEADS)
    return (_cat(hs[:HEADS] + [_rope(x, cos_t, sin_t) for x in hs[HEADS:]]),)


def _rope_q_bwd_fn(d_qn, d_qr, cos_t, sin_t):
    f = lambda x: _rope(x, cos_t, sin_t)
    return (_cat([d_qn] + [_vjp(f, [x], (x,))[0] for x in _heads(d_qr, HEADS)]),)


def _kv_prep_fwd_fn(ckr, cos_t, sin_t, gain):
    return _rms(ckr[:, :KV_LORA], gain), _rope(ckr[:, KV_LORA:], cos_t, sin_t)


def _kv_prep_bwd_fn(ckr, d_ckv, d_kr, cos_t, sin_t, gain):
    d_lat, d_gain = _vjp(_rms, [ckr[:, :KV_LORA], gain], (d_ckv,))
    f = lambda x: _rope(x, cos_t, sin_t)
    d_rope = _vjp(f, [ckr[:, KV_LORA:]], (d_kr,))[0]
    return _cat([d_lat, d_rope]), d_gain


def _rms_fwd(x, gain, *, name, tm=1024, out_dtype=_BF):
    tm = min(tm, x.shape[0])
    return _rowmap(lambda x_, g_: (_rms(x_.astype(_F32), g_),), [_t(x)], [gain], [(None, x.shape[1], out_dtype)], tm=tm, name=name)[0]


def _rms_bwd(x, d_xn, d_res, gain, *, name, tm=512):
    tm = min(tm, x.shape[0])

    def fn(x_, dxn_, *rest):
        g_ = rest[-1]
        dx, dg = _vjp(_rms, [x_.astype(_F32), g_], (dxn_.astype(_F32),))
        if d_res is not None:
            dx = dx + rest[0]
        return dx, dg

    tiles = [_t(x), _t(d_xn)] + ([_t(d_res)] if d_res is not None else [])
    return _rowmap(fn, tiles, [gain], [(None, x.shape[1], _F32)], [(gain.shape, _F32)], tm=tm, name=name)


def _ffn_fwd(x, gain, w_gu, w_down, *, name):
    s = x.shape[0]
    xn = _rms_fwd(x, gain, name=name + "_norm")
    gu = _mm_nn(xn, w_gu, out_dtype=_BF, name=name + "_gu", tn=1408)

    def act(g_, u_):
        return (_silu(g_.astype(_F32)) * u_.astype(_F32),)

    a = _rowmap(act, [_t(gu, D_FF, 0), _t(gu, D_FF, 1)], [], [(None, D_FF, _BF)], tm=min(512, s), name=name + "_act")[0]
    y = _mm_nn(a, w_down, out_dtype=_F32, name=name + "_down", scale=0.5, res=x, tn=1024)
    return y, (x, xn, gu, a)


def _ffn_bwd(d_y, saved, gain, w_gu, w_down, *, name):
    x, xn, gu, a = saved
    s = x.shape[0]
    d_a = _mm_nt(d_y, w_down, out_dtype=_BF, name=name + "_da", scale=0.5, tn=1408)

    def act_bwd(g_, u_, da_):
        g, u, da = g_.astype(_F32), u_.astype(_F32), da_.astype(_F32)
        sg = _sigmoid(g)
        return (_cat([da * u * sg * (1.0 + g * (1.0 - sg)), da * g * sg]),)

    d_gu = _rowmap(act_bwd, [_t(gu, D_FF, 0), _t(gu, D_FF, 1), _t(d_a)], [], [(None, 2 * D_FF, _BF)], tm=min(512, s), name=name + "_dact")[0]
    d_w_down = _mm_tn(a, d_y, name=name + "_dwd", scale=0.5, t1=1408, tn=1024)
    d_w_gu = _mm_tn(xn, d_gu, name=name + "_dwgu", t1=1024, tn=1408)
    d_xn = _mm_nt(d_gu, w_gu, out_dtype=_F32, name=name + "_dxn", tn=1024)
    d_x, d_gain = _rms_bwd(x, d_xn, d_y, gain, name=name + "_dnorm")
    return d_x, d_gain, d_w_gu, d_w_down


def _conv_fwd(h, w, *, name, tm=256):
    s = h.shape[0]
    tm = min(tm, s)
    c = 3 * A_WIDTH
    halo = SUBLANE

    def body(x_ref, prev_ref, w_ref, o_ref, buf):
        i = pl.program_id(0)
        buf[0:halo, :] = jnp.where(i == 0, 0.0, prev_ref[...])
        buf[halo:halo + tm, :] = x_ref[...]
        acc = jnp.zeros((tm, c), _F32)
        for j in range(CONV_K):
            acc = acc + buf[pl.ds(halo - (CONV_K - 1) + j, tm), :] * w_ref[j:j + 1, :]
        o_ref[...] = acc

    return _pcall(
        body, name=name, grid=(s // tm,),
        in_specs=[pl.BlockSpec((tm, c), lambda i: (i, 0)),
                  pl.BlockSpec((halo, c), lambda i: (jnp.maximum(i * (tm // halo) - 1, 0), 0)),
                  pl.BlockSpec(w.shape, lambda i: (0, 0))],
        out_specs=pl.BlockSpec((tm, c), lambda i: (i, 0)), out_shape=jax.ShapeDtypeStruct((s, c), _F32),
        scratch_shapes=[pltpu.VMEM((tm + 2 * halo, c), _F32)],
        compiler_params=_cp(1),
    )(h, h, w)


def _conv_bwd(h, d_y, w, *, name, tm=256):
    s = h.shape[0]
    tm = min(tm, s)
    c = 3 * A_WIDTH
    halo = SUBLANE
    steps = s // tm

    def body(x_ref, prev_ref, dy_ref, next_ref, w_ref, dx_ref, dw_ref, xbuf, dybuf):
        i = pl.program_id(0)

        @pl.when(i == 0)
        def _():
            dw_ref[...] = jnp.zeros(dw_ref.shape, dw_ref.dtype)

        xbuf[0:halo, :] = jnp.where(i == 0, 0.0, prev_ref[...])
        xbuf[halo:halo + tm, :] = x_ref[...]
        dybuf[0:tm, :] = dy_ref[...]
        dybuf[tm:tm + halo, :] = jnp.where(i == steps - 1, 0.0, next_ref[...])
        dy = dy_ref[...]
        acc = jnp.zeros((tm, c), _F32)
        for j in range(CONV_K):
            acc = acc + dybuf[pl.ds(CONV_K - 1 - j, tm), :] * w_ref[j:j + 1, :]
            dw_ref[j:j + 1, :] += jnp.sum(dy * xbuf[pl.ds(halo - (CONV_K - 1) + j, tm), :], axis=0, keepdims=True)
        dx_ref[...] = acc

    return _pcall(
        body, name=name, grid=(steps,),
        in_specs=[pl.BlockSpec((tm, c), lambda i: (i, 0)),
                  pl.BlockSpec((halo, c), lambda i: (jnp.maximum(i * (tm // halo) - 1, 0), 0)),
                  pl.BlockSpec((tm, c), lambda i: (i, 0)),
                  pl.BlockSpec((halo, c), lambda i: (jnp.minimum((i + 1) * (tm // halo), s // halo - 1), 0)),
                  pl.BlockSpec(w.shape, lambda i: (0, 0))],
        out_specs=[pl.BlockSpec((tm, c), lambda i: (i, 0)), pl.BlockSpec(w.shape, lambda i: (0, 0))],
        out_shape=[jax.ShapeDtypeStruct((s, c), _F32), jax.ShapeDtypeStruct(w.shape, _F32)],
        scratch_shapes=[pltpu.VMEM((tm + 2 * halo, c), _F32), pltpu.VMEM((tm + 2 * halo, c), _F32)],
        compiler_params=_cp(1),
    )(h, h, d_y, d_y, w)


def _flash_fwd(q_all, kv, kr, *, name, tq=512):
    s = q_all.shape[0]
    tq = min(tq, s)
    nq = s // tq
    scale = (HEAD_DIM + QK_ROPE) ** -0.5

    def body(qn_ref, qr_ref, kn_ref, kr_ref, v_ref, o_ref, lse_ref, m_sc, l_sc, acc_sc):
        i, j = pl.program_id(1), pl.program_id(2)

        @pl.when(j == 0)
        def _():
            m_sc[...] = jnp.full(m_sc.shape, -1e30, _F32)
            l_sc[...] = jnp.zeros(l_sc.shape, _F32)
            acc_sc[...] = jnp.zeros(acc_sc.shape, _F32)

        @pl.when(j <= i)
        def _():
            q = _cat([qn_ref[...], qr_ref[...]])
            k = _cat([kn_ref[...], kr_ref[...]])
            sc = _dot_nt(q, k) * scale
            ri = lax.broadcasted_iota(jnp.int32, sc.shape, 0) // CHUNK
            ci = lax.broadcasted_iota(jnp.int32, sc.shape, 1) // CHUNK
            sc = jnp.where((j < i) | (ci <= ri), sc, -1e30)
            m_prev = m_sc[...]
            m_new = jnp.maximum(m_prev, jnp.max(sc, axis=-1, keepdims=True))
            alpha = jnp.exp(m_prev - m_new)
            p = jnp.exp(sc - m_new[:, 0:1])
            l_sc[...] = alpha * l_sc[...] + jnp.sum(p, axis=-1, keepdims=True)
            acc_sc[...] = alpha[:, 0:1] * acc_sc[...] + _dot(p, v_ref[...])
            m_sc[...] = m_new

        @pl.when(j == i)
        def _():
            o_ref[...] = acc_sc[...] / l_sc[...][:, 0:1]
            lse_ref[...] = m_sc[...] + jnp.log(l_sc[...])

    kmap = lambda cb: (lambda h, i, j: (jnp.minimum(j, i), cb(h)))
    return _pcall(
        body, name=name, grid=(HEADS, nq, nq),
        in_specs=[pl.BlockSpec((tq, LANE), lambda h, i, j: (i, h)),
                  pl.BlockSpec((tq, LANE), lambda h, i, j: (i, HEADS + h)),
                  pl.BlockSpec((tq, LANE), kmap(lambda h: h)),
                  pl.BlockSpec((tq, LANE), kmap(lambda h: 0)),
                  pl.BlockSpec((tq, LANE), kmap(lambda h: HEADS + h))],
        out_specs=[pl.BlockSpec((tq, LANE), lambda h, i, j: (i, h)),
                   pl.BlockSpec((tq, LANE), lambda h, i, j: (i, h))],
        out_shape=[jax.ShapeDtypeStruct((s, A_WIDTH), _F32), jax.ShapeDtypeStruct((s, A_WIDTH), _F32)],
        scratch_shapes=[pltpu.VMEM((tq, LANE), _F32), pltpu.VMEM((tq, LANE), _F32), pltpu.VMEM((tq, LANE), _F32)],
        compiler_params=_cp(3),
    )(q_all, q_all, kv, kr, kv)


def _flash_delta(o, d_o, *, name, tm=512):
    tm = min(tm, o.shape[0])

    def fn(o_, do_):
        parts = [jnp.broadcast_to(jnp.sum(a * b, axis=-1, keepdims=True), a.shape) for a, b in zip(_heads(o_, HEADS), _heads(do_.astype(_F32), HEADS))]
        return (_cat(parts),)

    return _rowmap(fn, [_t(o), _t(d_o, A_WIDTH, 0)], [], [(None, A_WIDTH, _F32)], tm=tm, name=name)[0]


def _flash_probs(qn_ref, qr_ref, kn_ref, kr_ref, lse_ref, diag, scale):
    q = _cat([qn_ref[...], qr_ref[...]])
    k = _cat([kn_ref[...], kr_ref[...]])
    sc = _dot_nt(q, k) * scale
    ri = lax.broadcasted_iota(jnp.int32, sc.shape, 0) // CHUNK
    ci = lax.broadcasted_iota(jnp.int32, sc.shape, 1) // CHUNK
    p = jnp.where(jnp.logical_not(diag) | (ci <= ri), jnp.exp(sc - lse_ref[...][:, 0:1]), 0.0)
    return q, k, p


def _flash_bwd_dq(q_all, kv, kr, lse, delta, d_cat, *, name, tq=512):
    s = q_all.shape[0]
    tq = min(tq, s)
    nq = s // tq
    scale = (HEAD_DIM + QK_ROPE) ** -0.5

    def body(qn_ref, qr_ref, kn_ref, kr_ref, v_ref, lse_ref, dl_ref, do_ref, dqn_ref, dqr_ref, acc_sc):
        i, j = pl.program_id(1), pl.program_id(2)

        @pl.when(j == 0)
        def _():
            acc_sc[...] = jnp.zeros(acc_sc.shape, _F32)

        @pl.when(j <= i)
        def _():
            q, k, p = _flash_probs(qn_ref, qr_ref, kn_ref, kr_ref, lse_ref, j == i, scale)
            dp = _dot_nt(do_ref[...], v_ref[...])
            ds = p * (dp - dl_ref[...][:, 0:1]) * scale
            acc_sc[...] += _dot(ds, k)

        @pl.when(j == i)
        def _():
            dqn_ref[...] = acc_sc[:, 0:LANE]
            dqr_ref[...] = acc_sc[:, LANE:2 * LANE]

    kmap = lambda cb: (lambda h, i, j: (jnp.minimum(j, i), cb(h)))
    outs = _pcall(
        body, name=name, grid=(HEADS, nq, nq),
        in_specs=[pl.BlockSpec((tq, LANE), lambda h, i, j: (i, h)),
                  pl.BlockSpec((tq, LANE), lambda h, i, j: (i, HEADS + h)),
                  pl.BlockSpec((tq, LANE), kmap(lambda h: h)),
                  pl.BlockSpec((tq, LANE), kmap(lambda h: 0)),
                  pl.BlockSpec((tq, LANE), kmap(lambda h: HEADS + h)),
                  pl.BlockSpec((tq, LANE), lambda h, i, j: (i, h)),
                  pl.BlockSpec((tq, LANE), lambda h, i, j: (i, h)),
                  pl.BlockSpec((tq, LANE), lambda h, i, j: (i, h))],
        out_specs=[pl.BlockSpec((tq, LANE), lambda h, i, j: (i, h)), pl.BlockSpec((tq, LANE), lambda h, i, j: (i, h))],
        out_shape=[jax.ShapeDtypeStruct((s, A_WIDTH), _F32), jax.ShapeDtypeStruct((s, A_WIDTH), _F32)],
        scratch_shapes=[pltpu.VMEM((tq, 2 * LANE), _F32)],
        compiler_params=_cp(3),
    )(q_all, q_all, kv, kr, kv, lse, delta, d_cat)
    return outs


def _flash_bwd_dkv(q_all, kv, kr, lse, delta, d_cat, *, name, tq=512):
    s = q_all.shape[0]
    tq = min(tq, s)
    nq = s // tq
    scale = (HEAD_DIM + QK_ROPE) ** -0.5

    def body(qn_ref, qr_ref, kn_ref, kr_ref, v_ref, lse_ref, dl_ref, do_ref, dkn_ref, dv_ref, dkr_ref, dk_sc, dv_sc):
        j, i = pl.program_id(1), pl.program_id(2)

        @pl.when(i == 0)
        def _():
            dk_sc[...] = jnp.zeros(dk_sc.shape, _F32)
            dv_sc[...] = jnp.zeros(dv_sc.shape, _F32)

        @pl.when(i >= j)
        def _():
            q, k, p = _flash_probs(qn_ref, qr_ref, kn_ref, kr_ref, lse_ref, j == i, scale)
            do = do_ref[...]
            dv_sc[...] += _dot_tn(p, do)
            dp = _dot_nt(do, v_ref[...])
            ds = p * (dp - dl_ref[...][:, 0:1]) * scale
            dk_sc[...] += _dot_tn(ds, q)

        @pl.when(i == nq - 1)
        def _():
            dkn_ref[...] = dk_sc[:, 0:LANE]
            dkr_ref[...] = dk_sc[:, LANE:2 * LANE]
            dv_ref[...] = dv_sc[...]

    qmap = lambda cb: (lambda h, j, i: (jnp.maximum(i, j), cb(h)))
    return _pcall(
        body, name=name, grid=(HEADS, nq, nq),
        in_specs=[pl.BlockSpec((tq, LANE), qmap(lambda h: h)),
                  pl.BlockSpec((tq, LANE), qmap(lambda h: HEADS + h)),
                  pl.BlockSpec((tq, LANE), lambda h, j, i: (j, h)),
                  pl.BlockSpec((tq, LANE), lambda h, j, i: (j, 0)),
                  pl.BlockSpec((tq, LANE), lambda h, j, i: (j, HEADS + h)),
                  pl.BlockSpec((tq, LANE), qmap(lambda h: h)),
                  pl.BlockSpec((tq, LANE), qmap(lambda h: h)),
                  pl.BlockSpec((tq, LANE), qmap(lambda h: h))],
        out_specs=[pl.BlockSpec((tq, LANE), lambda h, j, i: (j, h)), pl.BlockSpec((tq, LANE), lambda h, j, i: (j, h)),
                   pl.BlockSpec((tq, LANE), lambda h, j, i: (j, h))],
        out_shape=[jax.ShapeDtypeStruct((s, A_WIDTH), _F32)] * 3,
        scratch_shapes=[pltpu.VMEM((tq, 2 * LANE), _F32), pltpu.VMEM((tq, LANE), _F32)],
        compiler_params=_cp(3),
    )(q_all, q_all, kv, kr, kv, lse, delta, d_cat)


def _final_loss(x, tgt, gain, *, name, tm=512):
    tm = min(tm, x.shape[0])

    def fn(x_, t_, g_):
        def f(xx, gg):
            err = _rms(xx, gg) - t_
            return 0.5 * jnp.sum(jnp.sum(err * err, axis=1, keepdims=True) / D_MODEL, axis=0, keepdims=True)

        loss, pull = jax.vjp(f, x_, g_)
        dx, dg = pull(jnp.ones((1, 1), _F32))
        return dx, dg, jnp.broadcast_to(loss, (SUBLANE, LANE))

    return _rowmap(fn, [_t(x), _t(tgt)], [gain], [(None, D_MODEL, _F32)], [(gain.shape, _F32), ((SUBLANE, LANE), _F32)], tm=tm, name=name)


def _local_step(x, mem, cos_t, sin_t, tgt, p):
    s = x.shape[0]
    g = {}
    row = lambda a: a.reshape(1, -1)
    tm_e = min(256, s)

    mem_n = _rms_fwd(mem, row(p["mem_norm"]), name="mem_norm")
    mem_kv_all = _mm_nn(mem_n, p["w_mem_all"], out_dtype=_F32, name="mem_kv", tn=1024)
    mem_kv = [mem_kv_all[:, 2 * MEM_WIDTH * l:2 * MEM_WIDTH * (l + 1)] for l in range(DEPTH)]

    sv = []
    for i in range(N_A):
        l = i
        r = {}
        r["x0"] = x
        x, r["ffn1"] = _ffn_fwd(x, row(p["ffn1_norm"][l]), p["ffn1_w_gu"][l], p["ffn1_w_down"][l], name=f"a{i}_ffn1")
        r["x1"] = x
        xn = _rms_fwd(x, row(p["mix_norm"][l]), name=f"a{i}_mixnorm")
        h = _mm_nn(xn, p["a_w_in"][i], out_dtype=_F32, name=f"a{i}_in", tn=1152)
        qkv_c = _conv_fwd(h, p["a_conv"][i], name=f"a{i}_conv")
        alog, dtb = p["a_A_log_row"][i], p["a_dt_bias_row"][i]
        q, k, v, bg = _rowmap(_gdn_prep_fwd_fn, [_t(qkv_c), _t(h, LANE, 26)], [alog, dtb],
                              [(None, A_WIDTH, _F32)] * 3 + [(None, LANE, _F32)], tm=tm_e, name=f"a{i}_prep")
        w_, u_, qd, kd, qk, dcrow = _rowmap(_gdn_intra_fwd_fn, [_t(q), _t(k), _t(v), _t(bg)], [],
                                            [(None, A_WIDTH, _F32)] * 5 + [(None, LANE, _F32)], tm=_PAIR, name=f"a{i}_intra")
        o, states = _rowmap(_gdn_scan_fwd_fn, [_t(qd), _t(kd), _t(w_), _t(u_), _t(qk), _t(dcrow)], [],
                            [(None, A_WIDTH, _F32), (A_WIDTH, HEAD_DIM, _F32)], tm=_PAIR, name=f"a{i}_scan",
                            carry=[((A_WIDTH, HEAD_DIM), _F32)])
        gain_o = row(p["a_out_norm"][i])
        cat = _rowmap(_a_post_fwd_fn, [_t(o), _t(h, A_WIDTH, 3), _t(h, MEM_WIDTH, 12)], [gain_o, mem_kv[l]],
                      [(None, D_MODEL, _BF)], tm=tm_e, name=f"a{i}_post")[0]
        x = _mm_nn(cat, p["w_out"][l], out_dtype=_F32, name=f"a{i}_out", res=x, tn=1024)
        r.update(xn=xn, h=h, qkv_c=qkv_c, q=q, k=k, v=v, bg=bg, w=w_, u=u_, qd=qd, kd=kd, qk=qk, dcrow=dcrow, o=o, states=states, cat=cat)
        r["x2"] = x
        x, r["ffn2"] = _ffn_fwd(x, row(p["ffn2_norm"][l]), p["ffn2_w_gu"][l], p["ffn2_w_down"][l], name=f"a{i}_ffn2")
        sv.append(r)

    kvs = {"x": x}
    xn_kv = _rms_fwd(x, row(p["kv_in_norm"]), name="kv_innorm")
    ckr = _mm_nn(xn_kv, p["w_dkv"], out_dtype=_F32, name="kv_down")
    ckv, k_rope = _rowmap(_kv_prep_fwd_fn, [_t(ckr), _t(cos_t), _t(sin_t)], [row(p["kv_lat_norm"])],
                          [(None, KV_LORA, _BF), (None, LANE, _BF)], tm=tm_e, name="kv_prep")
    kvu = _mm_nn(ckv, p["w_ukv"], out_dtype=_BF, name="kv_up")
    kvs.update(xn=xn_kv, ckr=ckr, ckv=ckv)

    for j in range(N_B):
        l = N_A + j
        r = {}
        x, r["ffn1"] = _ffn_fwd(x, row(p["ffn1_norm"][l]), p["ffn1_w_gu"][l], p["ffn1_w_down"][l], name=f"b{j}_ffn1")
        r["x1"] = x
        xn = _rms_fwd(x, row(p["mix_norm"][l]), name=f"b{j}_mixnorm")
        h = _mm_nn(xn, p["b_w_in"][j], out_dtype=_F32, name=f"b{j}_in")
        gain_q = row(p["b_q_norm"][j])
        cqn = _rowmap(lambda c_, g_: (_rms(c_, g_),), [_t(h, Q_LORA, 0)], [gain_q], [(None, Q_LORA, _BF)], tm=tm_e, name=f"b{j}_qnorm")[0]
        qf = _mm_nn(cqn, p["b_w_uq"][j], out_dtype=_F32, name=f"b{j}_uq")
        q_all = _rowmap(_rope_q_fwd_fn, [_t(qf), _t(cos_t), _t(sin_t)], [], [(None, UQ_PAD, _BF)], tm=tm_e, name=f"b{j}_rope")[0]
        o_b, lse = _flash_fwd(q_all, kvu, k_rope, name=f"b{j}_attn")
        cat = _rowmap(_b_post_fwd_fn, [_t(o_b), _t(h, MEM_WIDTH, 1)], [mem_kv[l]], [(None, D_MODEL, _BF)], tm=tm_e, name=f"b{j}_post")[0]
        x = _mm_nn(cat, p["w_out"][l], out_dtype=_F32, name=f"b{j}_out", res=x, tn=1024)
        r.update(xn=xn, h=h, cqn=cqn, q_all=q_all, o_b=o_b, lse=lse, cat=cat)
        x, r["ffn2"] = _ffn_fwd(x, row(p["ffn2_norm"][l]), p["ffn2_w_gu"][l], p["ffn2_w_down"][l], name=f"b{j}_ffn2")
        sv.append(r)

    dx, g["final_norm"], loss = _final_loss(x, tgt, row(p["final_norm"]), name="loss")

    per_layer = lambda: [None] * DEPTH
    for n in ("ffn1_norm", "ffn1_w_gu", "ffn1_w_down", "mix_norm", "ffn2_norm", "ffn2_w_gu", "ffn2_w_down", "w_out", "mem_kv"):
        g[n] = per_layer()
    for n in ("a_w_in", "a_conv", "a_A_log_row", "a_dt_bias_row", "a_out_norm", "b_w_in", "b_q_norm", "b_w_uq"):
        g[n] = [None] * N_A
    d_kv_parts = []

    for j in reversed(range(N_B)):
        l = N_A + j
        r = sv[l]
        dx, g["ffn2_norm"][l], g["ffn2_w_gu"][l], g["ffn2_w_down"][l] = _ffn_bwd(
            dx, r["ffn2"], row(p["ffn2_norm"][l]), p["ffn2_w_gu"][l], p["ffn2_w_down"][l], name=f"b{j}_ffn2b")
        d_cat = _mm_nt(dx, p["w_out"][l], out_dtype=_F32, name=f"b{j}_dcat", tn=1024)
        g["w_out"][l] = _mm_tn(r["cat"], dx, name=f"b{j}_dwout", tn=1024)
        d_qm, g["mem_kv"][l] = _rowmap(_b_post_bwd_fn, [_t(r["h"], MEM_WIDTH, 1), _t(d_cat, MEM_WIDTH, 3)], [mem_kv[l]],
                                      [(None, MEM_WIDTH, _F32)], [((N_MEM, 2 * MEM_WIDTH), _F32)], tm=tm_e, name=f"b{j}_postb")
        delta = _flash_delta(r["o_b"], d_cat, name=f"b{j}_delta")
        dqn, dqr = _flash_bwd_dq(r["q_all"], kvu, k_rope, r["lse"], delta, d_cat, name=f"b{j}_attn_dq")
        d_kv_parts.append(_flash_bwd_dkv(r["q_all"], kvu, k_rope, r["lse"], delta, d_cat, name=f"b{j}_attn_dkv"))
        d_qf = _rowmap(_rope_q_bwd_fn, [_t(dqn), _t(dqr), _t(cos_t), _t(sin_t)], [], [(None, UQ_PAD, _F32)], tm=tm_e, name=f"b{j}_ropeb")[0]
        d_cqn = _mm_nt(d_qf, p["b_w_uq"][j], out_dtype=_F32, name=f"b{j}_dcqn")
        g["b_w_uq"][j] = _mm_tn(r["cqn"], d_qf, name=f"b{j}_dwuq")
        gain_q = row(p["b_q_norm"][j])
        d_cq, g["b_q_norm"][j] = _rowmap(lambda c_, d_, g_: _vjp(_rms, [c_, g_], (d_,)), [_t(r["h"], Q_LORA, 0), _t(d_cqn)], [gain_q],
                                        [(None, Q_LORA, _F32)], [((1, Q_LORA), _F32)], tm=tm_e, name=f"b{j}_qnormb")
        d_h = jnp.concatenate([d_cq, d_qm], axis=1)
        d_xn = _mm_nt(d_h, p["b_w_in"][j], out_dtype=_F32, name=f"b{j}_dxn", tn=1024)
        g["b_w_in"][j] = _mm_tn(r["xn"], d_h, name=f"b{j}_dwin")
        dx, g["mix_norm"][l] = _rms_bwd(r["x1"], d_xn, dx, row(p["mix_norm"][l]), name=f"b{j}_mixnormb")
        dx, g["ffn1_norm"][l], g["ffn1_w_gu"][l], g["ffn1_w_down"][l] = _ffn_bwd(
            dx, r["ffn1"], row(p["ffn1_norm"][l]), p["ffn1_w_gu"][l], p["ffn1_w_down"][l], name=f"b{j}_ffn1b")

    def kv_sum(*parts):
        dkn = sum(parts[0::3][1:], parts[0])
        dv = sum(parts[1::3][1:], parts[1])
        dkr = sum(parts[2::3][1:], parts[2])
        return _cat([dkn, dv]), sum(_heads(dkr, HEADS)[1:], _heads(dkr, HEADS)[0])

    d_kvu, d_kr = _rowmap(kv_sum, [_t(a) for part in d_kv_parts for a in part], [], [(None, 2 * A_WIDTH, _F32), (None, LANE, _F32)],
                          tm=tm_e, name="kv_dsum")
    d_ckv = _mm_nt(d_kvu, p["w_ukv"], out_dtype=_F32, name="kv_dckv")
    g["w_ukv"] = _mm_tn(kvs["ckv"], d_kvu, name="kv_dwukv")
    d_ckr, g["kv_lat_norm"] = _rowmap(_kv_prep_bwd_fn, [_t(kvs["ckr"]), _t(d_ckv), _t(d_kr), _t(cos_t), _t(sin_t)], [row(p["kv_lat_norm"])],
                                     [(None, DKV_PAD, _F32)], [((1, KV_LORA), _F32)], tm=tm_e, name="kv_prepb")
    d_xn = _mm_nt(d_ckr, p["w_dkv"], out_dtype=_F32, name="kv_dxn", tn=1024)
    g["w_dkv"] = _mm_tn(kvs["xn"], d_ckr, name="kv_dwdkv")
    dx, g["kv_in_norm"] = _rms_bwd(kvs["x"], d_xn, dx, row(p["kv_in_norm"]), name="kv_innormb")

    for i in reversed(range(N_A)):
        l = i
        r = sv[l]
        dx, g["ffn2_norm"][l], g["ffn2_w_gu"][l], g["ffn2_w_down"][l] = _ffn_bwd(
            dx, r["ffn2"], row(p["ffn2_norm"][l]), p["ffn2_w_gu"][l], p["ffn2_w_down"][l], name=f"a{i}_ffn2b")
        d_cat = _mm_nt(dx, p["w_out"][l], out_dtype=_F32, name=f"a{i}_dcat", tn=1024)
        g["w_out"][l] = _mm_tn(r["cat"], dx, name=f"a{i}_dwout", tn=1024)
        gain_o = row(p["a_out_norm"][i])
        h = r["h"]
        d_o, d_hpart, g["a_out_norm"][i], g["mem_kv"][l] = _rowmap(
            _a_post_bwd_fn, [_t(r["o"]), _t(h, A_WIDTH, 3), _t(h, MEM_WIDTH, 12), _t(d_cat)], [gain_o, mem_kv[l]],
            [(None, A_WIDTH, _F32), (None, D_MODEL, _F32)], [((1, HEAD_DIM), _F32), ((N_MEM, 2 * MEM_WIDTH), _F32)], tm=tm_e, name=f"a{i}_postb")
        d_qd, d_kd, d_w, d_u, d_qk, d_dcrow = _rowmap(
            _gdn_scan_bwd_fn, [_t(r["qd"]), _t(r["kd"]), _t(r["w"]), _t(r["u"]), _t(r["qk"]), _t(r["dcrow"]), _t(r["states"], rows=A_WIDTH), _t(d_o)], [],
            [(None, A_WIDTH, _F32)] * 5 + [(None, LANE, _F32)], tm=_PAIR, name=f"a{i}_scanb", carry=[((A_WIDTH, HEAD_DIM), _F32)], reverse=True)
        d_q, d_k, d_v, d_bg = _rowmap(
            _gdn_intra_bwd_fn, [_t(r["q"]), _t(r["k"]), _t(r["v"]), _t(r["bg"]), _t(d_w), _t(d_u), _t(d_qd), _t(d_kd), _t(d_qk), _t(d_dcrow)], [],
            [(None, A_WIDTH, _F32)] * 3 + [(None, LANE, _F32)], tm=_PAIR, name=f"a{i}_intrab")
        alog, dtb = p["a_A_log_row"][i], p["a_dt_bias_row"][i]
        d_qkv_c, d_ba, g["a_A_log_row"][i], g["a_dt_bias_row"][i] = _rowmap(
            _gdn_prep_bwd_fn, [_t(r["qkv_c"]), _t(h, LANE, 26), _t(d_q), _t(d_k), _t(d_v), _t(d_bg)], [alog, dtb],
            [(None, 3 * A_WIDTH, _F32), (None, LANE, _F32)], [((1, LANE), _F32), ((1, LANE), _F32)], tm=tm_e, name=f"a{i}_prepb")
        d_qkv, g["a_conv"][i] = _conv_bwd(h, d_qkv_c, p["a_conv"][i], name=f"a{i}_convb")
        d_h = jnp.concatenate([d_qkv, d_hpart, d_ba], axis=1)
        d_xn = _mm_nt(d_h, p["a_w_in"][i], out_dtype=_F32, name=f"a{i}_dxn", tn=1024)
        g["a_w_in"][i] = _mm_tn(r["xn"], d_h, name=f"a{i}_dwin", tn=1152)
        dx, g["mix_norm"][l] = _rms_bwd(r["x1"], d_xn, dx, row(p["mix_norm"][l]), name=f"a{i}_mixnormb")
        dx, g["ffn1_norm"][l], g["ffn1_w_gu"][l], g["ffn1_w_down"][l] = _ffn_bwd(
            dx, r["ffn1"], row(p["ffn1_norm"][l]), p["ffn1_w_gu"][l], p["ffn1_w_down"][l], name=f"a{i}_ffn1b")

    d_mem_kv_all = jnp.concatenate(g.pop("mem_kv"), axis=1)
    d_mem_n = _mm_nt(d_mem_kv_all, p["w_mem_all"], out_dtype=_F32, name="mem_dn", tn=1024)
    g["w_mem_all"] = _mm_tn(mem_n, d_mem_kv_all, name="mem_dw", tn=1024)
    _, g["mem_norm"] = _rms_bwd(mem, d_mem_n, None, row(p["mem_norm"]), name="mem_normb")
    return loss, dx, g


_NOPE_ROPE = HEAD_DIM + QK_ROPE
_QKV_GATE = 4 * A_WIDTH
_BETA_AT = _QKV_GATE + MEM_WIDTH


def _lane_row(vals, at):
    return jnp.zeros((1, LANE), _F32).at[0, at:at + vals.shape[0]].set(vals.astype(_F32))


def _compute_form(w, conv_f32):
    p = {n: w[n] for n in ("ffn1_norm", "ffn1_w_gu", "ffn1_w_down", "mix_norm", "ffn2_norm", "ffn2_w_gu", "ffn2_w_down", "w_out",
                           "mem_norm", "a_out_norm", "b_w_in", "b_q_norm", "kv_in_norm", "kv_lat_norm", "final_norm")}
    wm = w["w_mem_kv"]
    p["w_mem_all"] = jnp.transpose(wm, (1, 0, 2)).reshape(D_MODEL, DEPTH * 2 * MEM_WIDTH)
    a = w["a_w_in"]
    pad = jnp.zeros((N_A, D_MODEL, A_IN_PAD - A_IN), a.dtype)
    p["a_w_in"] = jnp.concatenate([a[:, :, :_QKV_GATE], a[:, :, _QKV_GATE + 2 * HEADS:], a[:, :, _QKV_GATE:_QKV_GATE + 2 * HEADS], pad], axis=2)
    p["a_conv"] = jnp.concatenate([conv_f32, jnp.zeros((N_A, SUBLANE - CONV_K, 3 * A_WIDTH), _F32)], axis=1)
    p["a_A_log_row"] = [_lane_row(w["a_A_log"][i], HEADS) for i in range(N_A)]
    p["a_dt_bias_row"] = [_lane_row(w["a_dt_bias"][i], HEADS) for i in range(N_A)]
    uq = w["b_w_uq"].reshape(N_B, Q_LORA, HEADS, _NOPE_ROPE)
    rope = jnp.concatenate([uq[..., HEAD_DIM:], jnp.zeros((N_B, Q_LORA, HEADS, LANE - QK_ROPE), uq.dtype)], axis=-1)
    p["b_w_uq"] = jnp.concatenate([uq[..., :HEAD_DIM].reshape(N_B, Q_LORA, A_WIDTH), rope.reshape(N_B, Q_LORA, A_WIDTH)], axis=-1)
    dkv = w["w_dkv"]
    p["w_dkv"] = jnp.concatenate([dkv, jnp.zeros((D_MODEL, DKV_PAD - dkv.shape[1]), dkv.dtype)], axis=1)
    ukv = w["w_ukv"].reshape(KV_LORA, HEADS, 2 * HEAD_DIM)
    p["w_ukv"] = jnp.concatenate([ukv[..., :HEAD_DIM].reshape(KV_LORA, A_WIDTH), ukv[..., HEAD_DIM:].reshape(KV_LORA, A_WIDTH)], axis=-1)
    return p


def _natural_grads(g):
    st = lambda xs: jnp.stack(xs, axis=0)
    n = {}
    for k in ("ffn1_norm", "mix_norm", "ffn2_norm"):
        n[k] = st(g[k]).reshape(DEPTH, D_MODEL)
    for k in ("ffn1_w_gu", "ffn1_w_down", "ffn2_w_gu", "ffn2_w_down", "w_out", "b_w_in"):
        n[k] = st(g[k])
    n["mem_norm"] = g["mem_norm"].reshape(D_MODEL)
    n["w_mem_kv"] = jnp.transpose(g["w_mem_all"].reshape(D_MODEL, DEPTH, 2 * MEM_WIDTH), (1, 0, 2))
    a = st(g["a_w_in"])
    n["a_w_in"] = jnp.concatenate([a[:, :, :_QKV_GATE], a[:, :, _BETA_AT:_BETA_AT + 2 * HEADS], a[:, :, _QKV_GATE:_BETA_AT]], axis=2)
    n["a_conv"] = st(g["a_conv"])[:, :CONV_K]
    n["a_A_log"] = st(g["a_A_log_row"])[:, 0, HEADS:2 * HEADS]
    n["a_dt_bias"] = st(g["a_dt_bias_row"])[:, 0, HEADS:2 * HEADS]
    n["a_out_norm"] = st(g["a_out_norm"]).reshape(N_A, HEAD_DIM)
    n["b_q_norm"] = st(g["b_q_norm"]).reshape(N_B, Q_LORA)
    uq = st(g["b_w_uq"])
    nope = uq[:, :, :A_WIDTH].reshape(N_B, Q_LORA, HEADS, HEAD_DIM)
    rope = uq[:, :, A_WIDTH:].reshape(N_B, Q_LORA, HEADS, LANE)[..., :QK_ROPE]
    n["b_w_uq"] = jnp.concatenate([nope, rope], axis=-1).reshape(N_B, Q_LORA, HEADS * _NOPE_ROPE)
    n["kv_in_norm"] = g["kv_in_norm"].reshape(D_MODEL)
    n["w_dkv"] = g["w_dkv"][:, :KV_LORA + QK_ROPE]
    n["kv_lat_norm"] = g["kv_lat_norm"].reshape(KV_LORA)
    ukv = g["w_ukv"]
    n["w_ukv"] = jnp.concatenate([ukv[:, :A_WIDTH].reshape(KV_LORA, HEADS, HEAD_DIM), ukv[:, A_WIDTH:].reshape(KV_LORA, HEADS, HEAD_DIM)],
                                 axis=-1).reshape(KV_LORA, HEADS * 2 * HEAD_DIM)
    n["final_norm"] = g["final_norm"].reshape(D_MODEL)
    return n


def _rope_tables(positions):
    inv = ROPE_THETA ** (-jnp.arange(0, QK_ROPE, 2, dtype=_F32) / QK_ROPE)
    ang = positions.astype(_F32)[:, None] * inv
    z = jnp.zeros((positions.shape[0], LANE - QK_ROPE), _F32)
    cos, sin = jnp.cos(ang), jnp.sin(ang)
    return jnp.concatenate([cos, cos, z], axis=1), jnp.concatenate([sin, sin, z], axis=1)


_HBM = pl.BlockSpec(memory_space=pltpu.HBM)


def _place():
    x, y, c = lax.axis_index("x"), lax.axis_index("y"), lax.axis_index("c")
    return x, y, c, [(1 - x, y), (x, 1 - y), (1 - x, 1 - y)]


def _remote(src, dst, send_sem, recv_sem, to):
    return pltpu.make_async_remote_copy(src_ref=src, dst_ref=dst, send_sem=send_sem, recv_sem=recv_sem, device_id=to, device_id_type=_MESH)


def _gather_over_chips(shard, *, name):
    rows, cols = shard.shape
    half = rows // 2

    def body(w_ref, out_ref, send_sems, recv_sems, local_sem):
        x, y, c, chips = _place()
        k = 2 * x + y

        def part(chip, h):
            return out_ref.at[chip, pl.ds(h * half, half), :]

        mine = pltpu.make_async_copy(w_ref, out_ref.at[k], local_sem)
        mine.start()
        first = [_remote(w_ref.at[pl.ds(c * half, half), :], part(k, c), send_sems.at[j], recv_sems.at[j], (px, py, c))
                 for j, (px, py) in enumerate(chips)]
        for cp in first:
            cp.start()
        passed = []
        for j, (px, py) in enumerate(chips):
            got = part(2 * px + py, c)
            _remote(got, got, send_sems.at[j], recv_sems.at[j], (px, py, c)).wait_recv()
            fw = _remote(got, got, send_sems.at[3 + j], recv_sems.at[3 + j], (x, y, 1 - c))
            fw.start()
            passed.append(fw)
        for j, (px, py) in enumerate(chips):
            got = part(2 * px + py, 1 - c)
            _remote(got, got, send_sems.at[3 + j], recv_sems.at[3 + j], (x, y, 1 - c)).wait_recv()
        for cp in first + passed:
            cp.wait_send()
        mine.wait()

    return _pcall(
        body, name=name, in_specs=[_HBM], out_specs=_HBM, out_shape=jax.ShapeDtypeStruct((N_CHIPS, rows, cols), shard.dtype),
        scratch_shapes=[pltpu.SemaphoreType.DMA((6,)), pltpu.SemaphoreType.DMA((6,)), pltpu.SemaphoreType.DMA],
    )(shard)


def _pair_exchange(v, *, name):
    def body(v_ref, out_ref, send_sem, recv_sem):
        x, y, c, _ = _place()
        cp = _remote(v_ref, out_ref, send_sem, recv_sem, (x, y, 1 - c))
        cp.start()
        cp.wait()

    return _pcall(body, name=name, in_specs=[_HBM], out_specs=_HBM, out_shape=jax.ShapeDtypeStruct(v.shape, v.dtype),
                  scratch_shapes=[pltpu.SemaphoreType.DMA, pltpu.SemaphoreType.DMA])(v)


def _scatter_over_chips(v, *, name):
    def body(v_ref, out_ref, send_sems, recv_sems):
        x, y, c, chips = _place()
        cps = [_remote(v_ref.at[2 * px + py], out_ref.at[j], send_sems.at[j], recv_sems.at[j], (px, py, c)) for j, (px, py) in enumerate(chips)]
        for cp in cps:
            cp.start()
        for cp in cps:
            cp.wait()

    return _pcall(body, name=name, in_specs=[_HBM], out_specs=_HBM, out_shape=jax.ShapeDtypeStruct((N_CHIPS - 1,) + v.shape[1:], v.dtype),
                  scratch_shapes=[pltpu.SemaphoreType.DMA((3,)), pltpu.SemaphoreType.DMA((3,))])(v)


def _all_reduce_small(v, *, name):
    def body(v_ref, out_ref, all_ref, send_sems, recv_sems):
        x, y, c, _ = _place()
        me = 4 * x + 2 * y + c
        all_ref[me] = v_ref[...]
        cps = []
        for f in range(1, N_DEV):
            fx, fy, fc = (f >> 2) & 1, (f >> 1) & 1, f & 1
            to = (x + fx - 2 * x * fx, y + fy - 2 * y * fy, c + fc - 2 * c * fc)
            cps.append(_remote(v_ref, all_ref.at[me], send_sems.at[f - 1], recv_sems.at[f - 1], to))
        for cp in cps:
            cp.start()
        for cp in cps:
            cp.wait()
        acc = all_ref[0]
        for d in range(1, N_DEV):
            acc = acc + all_ref[d]
        out_ref[...] = acc

    vm = pl.BlockSpec(memory_space=pltpu.VMEM)
    return _pcall(body, name=name, in_specs=[vm], out_specs=vm, out_shape=jax.ShapeDtypeStruct(v.shape, v.dtype),
                  scratch_shapes=[pltpu.VMEM((N_DEV,) + v.shape, v.dtype), pltpu.SemaphoreType.DMA((N_DEV - 1,)), pltpu.SemaphoreType.DMA((N_DEV - 1,))])(v)


_BIG = (("ffn1_w_gu", 2), ("ffn1_w_down", 1), ("ffn2_w_gu", 2), ("ffn2_w_down", 1), ("w_out", 1), ("w_mem_kv", 1), ("a_w_in", 2),
        ("a_conv", 2), ("b_w_in", 1), ("b_w_uq", 2), ("w_dkv", 0), ("w_ukv", 1))
_SMALL = ("ffn1_norm", "mix_norm", "ffn2_norm", "mem_norm", "a_A_log", "a_dt_bias", "a_out_norm", "b_q_norm", "kv_in_norm", "kv_lat_norm",
          "final_norm")
_WEIGHTS = ("ffn1_norm", "ffn1_w_gu", "ffn1_w_down", "mix_norm", "ffn2_norm", "ffn2_w_gu", "ffn2_w_down", "w_out", "mem_norm", "w_mem_kv",
            "a_w_in", "a_conv", "a_A_log", "a_dt_bias", "a_out_norm", "b_w_in", "b_q_norm", "b_w_uq", "kv_in_norm", "w_dkv", "kv_lat_norm",
            "w_ukv", "final_norm")
SMALL_ROWS = 24


def _pack(arrs, rows):
    flat = jnp.concatenate([a.reshape(-1) for a in arrs])
    return jnp.pad(flat, (0, rows * PACK_COLS - flat.shape[0])).reshape(rows, PACK_COLS)


def _unpack(packed, shapes):
    flat, off, out = packed.reshape(-1), 0, []
    for shp in shapes:
        n = math.prod(shp)
        out.append(flat[off:off + n].reshape(shp))
        off += n
    return out


def _add(parts, *, name):
    def fn(*ts):
        acc = ts[0]
        for t in ts[1:]:
            acc = acc + t
        return (acc,)

    return _rowmap(fn, [_t(a) for a in parts], [], [(None, parts[0].shape[1], _F32)], tm=min(PACK_ROW_TILE, parts[0].shape[0]), name=name)[0]


def _adamw(w, g, m, v, *, name):
    def fn(w_, g_, m_, v_):
        m2 = ADAM_B1 * m_ + (1.0 - ADAM_B1) * g_
        v2 = ADAM_B2 * v_ + (1.0 - ADAM_B2) * (g_ * g_)
        m_hat = m2 / (1.0 - ADAM_B1 ** ADAM_STEP)
        v_hat = v2 / (1.0 - ADAM_B2 ** ADAM_STEP)
        return -ADAM_LR * (m_hat / (jnp.sqrt(v_hat) + ADAM_EPS) + ADAM_WD * w_), m2, v2

    return _rowmap(fn, [_t(w), _t(g), _t(m), _t(v)], [], [(None, w.shape[1], _F32)] * 3, tm=min(PACK_ROW_TILE, w.shape[0]), name=name)


def _step(x, mem, positions, loss_target, w, m, v):
    cx, cy, cc = lax.axis_index("x"), lax.axis_index("y"), lax.axis_index("c")
    chip = 2 * cx + cy
    big = [n for n, _ in _BIG]
    shard_shapes = [w[n].shape for n in big]
    total = sum(math.prod(s) for s in shard_shapes)
    rows = -(-total // (PACK_COLS * 2 * PACK_ROW_TILE)) * 2 * PACK_ROW_TILE
    half = rows // 2

    w_pack = _pack([w[n] for n in big], rows)
    gathered = _gather_over_chips(w_pack.astype(_BF), name="gather_weights")
    pieces = [_unpack(gathered[k], shard_shapes) for k in range(N_CHIPS)]
    full = {n: jnp.concatenate([pieces[k][i] for k in range(N_CHIPS)], axis=ax) for i, (n, ax) in enumerate(_BIG)}
    for n in _SMALL:
        full[n] = w[n]
    conv = w["a_conv"]
    slots = jnp.zeros((N_CHIPS,) + conv.shape, _F32).at[chip].set(jnp.where(cc == 0, conv, 0.0))
    conv_all = _unpack(_all_reduce_small(_pack([slots], SMALL_ROWS), name="gather_conv"), [slots.shape])[0]
    conv_full = jnp.concatenate([conv_all[k] for k in range(N_CHIPS)], axis=2)

    p = _compute_form(full, conv_full)
    cos_t, sin_t = _rope_tables(positions[0])
    loss_tile, d_x, g = _local_step(x[0], mem[0], cos_t, sin_t, loss_target[0], p)
    gn = _natural_grads(g)

    def shard_of(a, ax, k):
        size = a.shape[ax] // N_CHIPS
        return lax.slice_in_dim(a, k * size, (k + 1) * size, axis=ax)

    g_pack = jnp.stack([_pack([shard_of(gn[n], ax, k) for n, ax in _BIG], rows) for k in range(N_CHIPS)]).reshape(N_CHIPS, 2, half, PACK_COLS)
    keep = lax.dynamic_index_in_dim(g_pack, cc, axis=1, keepdims=False).reshape(N_CHIPS * half, PACK_COLS)
    send = lax.dynamic_index_in_dim(g_pack, 1 - cc, axis=1, keepdims=False).reshape(N_CHIPS * half, PACK_COLS)
    chip_sum = _add([keep, _pair_exchange(send, name="grad_pair_sum")], name="grad_add_pair").reshape(N_CHIPS, half, PACK_COLS)
    from_chips = _scatter_over_chips(chip_sum, name="grad_scatter")
    own = lax.dynamic_index_in_dim(chip_sum, chip, axis=0, keepdims=False)
    mine = _add([own, from_chips[0], from_chips[1], from_chips[2]], name="grad_add_chips")
    theirs = _pair_exchange(mine, name="grad_pair_gather")
    g_big = jnp.concatenate([jnp.where(cc == 0, mine, theirs), jnp.where(cc == 0, theirs, mine)], axis=0)

    d_big, m_big, v_big = _adamw(w_pack, g_big, _pack([m[n] for n in big], rows), _pack([v[n] for n in big], rows), name="adamw_big")

    small_shapes = [w[n].shape for n in _SMALL]
    g_small = _all_reduce_small(_pack([gn[n] for n in _SMALL], SMALL_ROWS), name="grad_small")
    d_small, m_small, v_small = _adamw(_pack([w[n] for n in _SMALL], SMALL_ROWS), g_small, _pack([m[n] for n in _SMALL], SMALL_ROWS),
                                       _pack([v[n] for n in _SMALL], SMALL_ROWS), name="adamw_small")

    def by_name(big_pack, small_pack):
        out = dict(zip(big, _unpack(big_pack, shard_shapes)))
        out.update(zip(_SMALL, _unpack(small_pack, small_shapes)))
        return out

    grads, deltas, new_m, new_v = by_name(g_big, g_small), by_name(d_big, d_small), by_name(m_big, m_small), by_name(v_big, v_small)
    loss = lax.psum(loss_tile[0, 0], ("x", "y", "c"))
    return (loss, d_x[None], *[grads[n] for n in _WEIGHTS], *[deltas[n] for n in _WEIGHTS], *[new_m[n] for n in _WEIGHTS],
            *[new_v[n] for n in _WEIGHTS])


def kernel(x, mem, positions, ffn1_norm, ffn1_w_gu, ffn1_w_down, mix_norm, ffn2_norm, ffn2_w_gu, ffn2_w_down, w_out, mem_norm, w_mem_kv, a_w_in, a_conv, a_A_log, a_dt_bias, a_out_norm, b_w_in, b_q_norm, b_w_uq, kv_in_norm, w_dkv, kv_lat_norm, w_ukv, final_norm, loss_target, m_ffn1_norm, m_ffn1_w_gu, m_ffn1_w_down, m_mix_norm, m_ffn2_norm, m_ffn2_w_gu, m_ffn2_w_down, m_w_out, m_mem_norm, m_w_mem_kv, m_a_w_in, m_a_conv, m_a_A_log, m_a_dt_bias, m_a_out_norm, m_b_w_in, m_b_q_norm, m_b_w_uq, m_kv_in_norm, m_w_dkv, m_kv_lat_norm, m_w_ukv, m_final_norm, v_ffn1_norm, v_ffn1_w_gu, v_ffn1_w_down, v_mix_norm, v_ffn2_norm, v_ffn2_w_gu, v_ffn2_w_down, v_w_out, v_mem_norm, v_w_mem_kv, v_a_w_in, v_a_conv, v_a_A_log, v_a_dt_bias, v_a_out_norm, v_b_w_in, v_b_q_norm, v_b_w_uq, v_kv_in_norm, v_w_dkv, v_kv_lat_norm, v_w_ukv, v_final_norm):
    given = dict(locals())
    w = {n: given[n] for n in _WEIGHTS}
    m = {n: given["m_" + n] for n in _WEIGHTS}
    v = {n: given["v_" + n] for n in _WEIGHTS}
    return _step(x, mem, positions, loss_target, w, m, v)
```

```python
import functools
import math

import jax
import jax.numpy as jnp
from jax import lax
from jax.experimental import pallas as pl
from jax.experimental.pallas import tpu as pltpu

_BF = jnp.bfloat16
_F32 = jnp.float32
_HI = lax.Precision.HIGHEST
_MESH = pl.DeviceIdType.MESH

D_MODEL = 1024
DEPTH = 4
N_A = 2
N_B = 2
CHUNK = 64
EPS = 1e-6
HEADS = 6
HEAD_DIM = 128
A_WIDTH = HEADS * HEAD_DIM
CONV_K = 4
QK_ROPE = 64
Q_LORA = 256
KV_LORA = 256
N_MEM = 256
MEM_HEADS = 4
MEM_HEAD_DIM = 64
MEM_WIDTH = MEM_HEADS * MEM_HEAD_DIM
D_FF = 2816
ROPE_THETA = 10000.0
A_IN = 4 * A_WIDTH + 2 * HEADS + MEM_WIDTH
A_IN_PAD = 3456
UQ_PAD = 2 * A_WIDTH
DKV_PAD = KV_LORA + 128
LANE = 128
SUBLANE = 8

ADAM_LR = 0.001
ADAM_B1 = 0.9
ADAM_B2 = 0.999
ADAM_EPS = 1e-08
ADAM_WD = 0.01
ADAM_STEP = 10

N_CHIPS = 4
N_DEV = 8
PACK_COLS = 1024
PACK_ROW_TILE = 256


def _pcall(body, **kw):
    return pl.pallas_call(body, **kw)


VMEM_LIMIT_V7X = 48 * 2 ** 20
TILE_BYTES = 6 * 2 ** 20


def _cp(grid_rank):
    return pltpu.CompilerParams(dimension_semantics=("arbitrary",) * grid_rank, vmem_limit_bytes=VMEM_LIMIT_V7X)


def _fit_rows(rows, row_bytes):
    while rows > LANE and rows * row_bytes > TILE_BYTES:
        rows //= 2
    return rows


def _fit_cols(n, target, col_bytes):
    return _tile(n, max(LANE, min(target, TILE_BYTES // col_bytes)))


def _tile(n, target):
    best = None
    for t in range(LANE, min(n, target) + 1, LANE):
        if n % t == 0:
            best = t
    return best if best is not None else n


def _dot(a, b):
    return jnp.dot(a.astype(_BF), b.astype(_BF), preferred_element_type=_F32)


def _dot_nt(a, b):
    return lax.dot_general(a.astype(_BF), b.astype(_BF), (((1,), (1,)), ((), ())), preferred_element_type=_F32)


def _dot_tn(a, b):
    return lax.dot_general(a.astype(_BF), b.astype(_BF), (((0,), (0,)), ((), ())), preferred_element_type=_F32)


def _dot_hi(a, b):
    return jnp.dot(a, b, precision=_HI, preferred_element_type=_F32)


def _rowmap(fn, tiles, params, outs, accs=(), *, tm, name, carry=(), reverse=False):
    rows = tiles[0][0].shape[0]
    steps = rows // tm
    nt, npar, no, na, nc = len(tiles), len(params), len(outs), len(accs), len(carry)

    def step_index(i):
        return steps - 1 - i if reverse else i

    in_specs, operands = [], []
    for arr, r, w, cb in tiles:
        r = tm if r is None else r
        w = arr.shape[1] if w is None else w
        assert arr.shape[0] == steps * r and (w % LANE == 0 or w == arr.shape[1]), (name, arr.shape, r, w)
        in_specs.append(pl.BlockSpec((r, w), functools.partial(lambda i, cb: (step_index(i), cb), cb=cb)))
        operands.append(arr)
    for p in params:
        in_specs.append(pl.BlockSpec(p.shape, functools.partial(lambda i, nd: (0,) * nd, nd=p.ndim)))
        operands.append(p)
    out_specs, out_shape = [], []
    for r, cols, dt in outs:
        r = tm if r is None else r
        out_specs.append(pl.BlockSpec((r, cols), lambda i: (step_index(i), 0)))
        out_shape.append(jax.ShapeDtypeStruct((steps * r, cols), dt))
    for shp, dt in accs:
        out_specs.append(pl.BlockSpec(shp, functools.partial(lambda i, nd: (0,) * nd, nd=len(shp))))
        out_shape.append(jax.ShapeDtypeStruct(shp, dt))

    def body(*refs):
        t_refs = refs[:nt]
        p_refs = refs[nt:nt + npar]
        o_refs = refs[nt + npar:nt + npar + no]
        a_refs = refs[nt + npar + no:nt + npar + no + na]
        c_refs = refs[nt + npar + no + na:]
        if na or nc:
            @pl.when(pl.program_id(0) == 0)
            def _():
                for r in a_refs + c_refs:
                    r[...] = jnp.zeros(r.shape, r.dtype)
        vals = fn(*[r[...] for r in t_refs], *[r[...] for r in p_refs], *[r[...] for r in c_refs])
        vals = tuple(vals) if isinstance(vals, (tuple, list)) else (vals,)
        assert len(vals) == no + na + nc, (name, len(vals), no, na, nc)
        for r, v in zip(o_refs, vals[:no]):
            r[...] = v.astype(r.dtype)
        for r, v in zip(a_refs, vals[no:no + na]):
            r[...] += v.astype(r.dtype)
        for r, v in zip(c_refs, vals[no + na:]):
            r[...] = v.astype(r.dtype)

    res = _pcall(
        body, name=name, grid=(steps,), in_specs=in_specs, out_specs=out_specs, out_shape=out_shape,
        scratch_shapes=[pltpu.VMEM(shp, dt) for shp, dt in carry],
        compiler_params=_cp(1),
    )(*operands)
    return res


def _t(arr, width=None, cb=0, rows=None):
    return (arr, rows, width, cb)


def _mm_nn(a, b, *, out_dtype, name, scale=None, res=None, tm=1024, tn=1536):
    m, k = a.shape
    n = b.shape[1]
    tm, tn = _fit_rows(min(tm, m), k * a.dtype.itemsize), _fit_cols(n, tn, k * b.dtype.itemsize)

    def body(a_ref, b_ref, *rest):
        acc = _dot(a_ref[...], b_ref[...])
        if scale is not None:
            acc = acc * scale
        if res is not None:
            acc = acc + rest[0][...]
        rest[-1][...] = acc.astype(rest[-1].dtype)

    in_specs = [pl.BlockSpec((tm, k), lambda i, j: (i, 0)), pl.BlockSpec((k, tn), lambda i, j: (0, j))]
    operands = [a, b]
    if res is not None:
        in_specs.append(pl.BlockSpec((tm, tn), lambda i, j: (i, j)))
        operands.append(res)
    return _pcall(
        body, name=name, grid=(m // tm, n // tn), in_specs=in_specs,
        out_specs=pl.BlockSpec((tm, tn), lambda i, j: (i, j)), out_shape=jax.ShapeDtypeStruct((m, n), out_dtype),
        compiler_params=_cp(2),
    )(*operands)


def _mm_nt(a, b, *, out_dtype, name, scale=None, tm=1024, tn=1536):
    m, k = a.shape
    n = b.shape[0]
    tm, tn = _fit_rows(min(tm, m), k * a.dtype.itemsize), _fit_cols(n, tn, k * b.dtype.itemsize)

    def body(a_ref, b_ref, o_ref):
        acc = _dot_nt(a_ref[...], b_ref[...])
        if scale is not None:
            acc = acc * scale
        o_ref[...] = acc.astype(o_ref.dtype)

    return _pcall(
        body, name=name, grid=(m // tm, n // tn),
        in_specs=[pl.BlockSpec((tm, k), lambda i, j: (i, 0)), pl.BlockSpec((tn, k), lambda i, j: (j, 0))],
        out_specs=pl.BlockSpec((tm, tn), lambda i, j: (i, j)), out_shape=jax.ShapeDtypeStruct((m, n), out_dtype),
        compiler_params=_cp(2),
    )(a, b)


def _mm_tn(a, b, *, name, scale=None, t1=1024, tn=1536, ts=1024):
    s, k1 = a.shape
    n = b.shape[1]
    t1, tn, ts = _tile(k1, t1), _tile(n, tn), min(ts, s)
    steps = s // ts

    def body(a_ref, b_ref, o_ref):
        @pl.when(pl.program_id(2) == 0)
        def _():
            o_ref[...] = jnp.zeros(o_ref.shape, o_ref.dtype)

        o_ref[...] += _dot_tn(a_ref[...], b_ref[...])
        if scale is not None:
            @pl.when(pl.program_id(2) == steps - 1)
            def _():
                o_ref[...] = o_ref[...] * scale

    return _pcall(
        body, name=name, grid=(k1 // t1, n // tn, steps),
        in_specs=[pl.BlockSpec((ts, t1), lambda i, j, r: (r, i)), pl.BlockSpec((ts, tn), lambda i, j, r: (r, j))],
        out_specs=pl.BlockSpec((t1, tn), lambda i, j, r: (i, j)), out_shape=jax.ShapeDtypeStruct((k1, n), _F32),
        compiler_params=_cp(3),
    )(a, b)


def _heads(t, n, w=LANE):
    return [t[:, w * h:w * (h + 1)] for h in range(n)]


def _cat(parts):
    return jnp.concatenate(parts, axis=1)


def _rms(x, g):
    return x * lax.rsqrt(jnp.mean(x * x, axis=-1, keepdims=True) + EPS) * g


def _l2n(x):
    return x * lax.rsqrt(jnp.sum(x * x, axis=-1, keepdims=True) + EPS)


def _sigmoid(x):
    return 0.5 * (jnp.tanh(0.5 * x) + 1.0)


def _silu(x):
    return x * _sigmoid(x)


def _softplus(x):
    return jnp.maximum(x, 0.0) + jnp.log(1.0 + jnp.exp(-jnp.abs(x)))


def _lane_pick(t, h):
    lane = lax.broadcasted_iota(jnp.int32, t.shape, 1)
    return jnp.sum(jnp.where(lane == h, t, 0.0), axis=1, keepdims=True)


def _lane_put(col, h, width=LANE):
    lane = lax.broadcasted_iota(jnp.int32, (col.shape[0], width), 1)
    return jnp.where(lane == h, col, 0.0)


def _vjp(fwd, ins, cts):
    outs, pull = jax.vjp(fwd, *ins)
    outs = outs if isinstance(outs, (tuple, list)) else (outs,)
    cts = tuple(c.astype(o.dtype) for c, o in zip(cts, outs))
    return pull(cts if len(cts) > 1 else cts[0])


def _rot_half_matrix():
    r = lax.broadcasted_iota(jnp.int32, (LANE, LANE), 0)
    c = lax.broadcasted_iota(jnp.int32, (LANE, LANE), 1)
    half = QK_ROPE // 2
    return jnp.where((c < half) & (r == c + half), -1.0, jnp.where((c >= half) & (c < QK_ROPE) & (r == c - half), 1.0, 0.0))


def _rope(x, cos_t, sin_t):
    return x * cos_t + _dot_hi(x, _rot_half_matrix()) * sin_t


def _mem_attn(q, km, vm):
    lane_q = lax.broadcasted_iota(jnp.int32, q.shape, 1)
    lane_v = lax.broadcasted_iota(jnp.int32, vm.shape, 1)
    out = jnp.zeros(q.shape, _F32)
    for h in range(MEM_HEADS):
        lo, hi = MEM_HEAD_DIM * h, MEM_HEAD_DIM * (h + 1)
        qh = jnp.where((lane_q >= lo) & (lane_q < hi), q, 0.0)
        vh = jnp.where((lane_v >= lo) & (lane_v < hi), vm, 0.0)
        sc = _dot_nt(qh, km) * MEM_HEAD_DIM ** -0.5
        sc = sc - lax.stop_gradient(jnp.max(sc, axis=-1, keepdims=True))
        p = jnp.exp(sc)
        p = p / jnp.sum(p, axis=-1, keepdims=True)
        out = out + _dot(p, vh)
    return out


_PAIR = 2 * CHUNK


def _pair_masks():
    ri = lax.broadcasted_iota(jnp.int32, (_PAIR, _PAIR), 0)
    ci = lax.broadcasted_iota(jnp.int32, (_PAIR, _PAIR), 1)
    same = (ri >= CHUNK) == (ci >= CHUNK)
    return same, same & (ri >= ci), same & (ri > ci), ri == ci, same & (ri <= ci)


_NN = (((1,), (0,)), ((), ()))
_NT = (((1,), (1,)), ((), ()))
_TN = (((0,), (0,)), ((), ()))


def _dot3(a, b, dims):
    a_hi, b_hi = a.astype(_BF), b.astype(_BF)
    a_lo, b_lo = (a - a_hi.astype(_F32)).astype(_BF), (b - b_hi.astype(_F32)).astype(_BF)
    d = lambda x, y: lax.dot_general(x, y, dims, preferred_element_type=_F32)
    return d(a_hi, b_hi) + (d(a_hi, b_lo) + d(a_lo, b_hi))


@jax.custom_vjp
def _mm3(a, b):
    return _dot3(a, b, _NN)


_mm3.defvjp(lambda a, b: (_dot3(a, b, _NN), (a, b)), lambda res, g: (_dot3(g, res[1], _NT), _dot3(res[0], g, _TN)))


def _neumann_inverse(a):
    eye = jnp.where(_pair_masks()[3], 1.0, 0.0)
    n = -a
    t_inv = eye + n
    for _ in range(5):
        n = _dot3(n, n, _NN)
        t_inv = t_inv + _dot3(t_inv, n, _NN)
    return t_inv


@jax.custom_vjp
def _unit_lower_inverse(a):
    return _neumann_inverse(a)


def _unit_lower_inverse_fwd(a):
    t_inv = _neumann_inverse(a)
    return t_inv, t_inv


def _unit_lower_inverse_bwd(t_inv, g):
    return (-_dot3(t_inv, _dot3(g, t_inv, _NT), _TN),)


_unit_lower_inverse.defvjp(_unit_lower_inverse_fwd, _unit_lower_inverse_bwd)


def _gdn_intra_head(q, k, v, beta, gl):
    same, causal, strict, eye, upper = _pair_masks()
    gl_row = jnp.sum(jnp.where(eye, gl, 0.0), axis=0, keepdims=True)
    g_col = jnp.sum(jnp.where(causal, gl_row, 0.0), axis=1, keepdims=True)
    g_row = jnp.sum(jnp.where(upper, gl, 0.0), axis=0, keepdims=True)
    g_last = jnp.sum(jnp.where(same, gl_row, 0.0), axis=1, keepdims=True)
    decay = jnp.where(causal, jnp.exp(jnp.where(causal, g_col - g_row, 0.0)), 0.0)
    kb = k * beta
    a = jnp.where(strict, _dot_nt(kb, k) * decay, 0.0)
    t_inv = _unit_lower_inverse(a)
    e_g = jnp.exp(g_col)
    u = _mm3(t_inv, v * beta)
    w = _mm3(t_inv, kb * e_g)
    qk = _dot_nt(q, k) * decay
    return w, u, q * e_g, k * jnp.exp(g_last - g_col), qk, jnp.exp(g_last)


def _gdn_scan_head(s, qd_a, kd_a, w_a, u_a, qk_a, dc_a, qd_b, kd_b, w_b, u_b, qk_b, dc_b):
    zeros = jnp.zeros((CHUNK, HEAD_DIM), _F32)
    vn_a = u_a - _dot(w_a, s)
    o_a = _dot(qd_a, s) + _dot(qk_a, jnp.concatenate([vn_a, zeros], axis=0))
    s1 = s * dc_a + _dot_tn(kd_a, vn_a)
    vn_b = u_b - _dot(w_b, s1)
    o_b = _dot(qd_b, s1) + _dot(qk_b, jnp.concatenate([zeros, vn_b], axis=0))
    s2 = s1 * dc_b + _dot_tn(kd_b, vn_b)
    return o_a, o_b, s2


def _pick_scalar(t, row, lane_i):
    ri = lax.broadcasted_iota(jnp.int32, t.shape, 0)
    ci = lax.broadcasted_iota(jnp.int32, t.shape, 1)
    return jnp.sum(jnp.sum(jnp.where((ri == row) & (ci == lane_i), t, 0.0), axis=1, keepdims=True), axis=0, keepdims=True)


def _put_scalar(val, row, lane_i, shape):
    ri = lax.broadcasted_iota(jnp.int32, shape, 0)
    ci = lax.broadcasted_iota(jnp.int32, shape, 1)
    return jnp.where((ri == row) & (ci == lane_i), val, 0.0)


def _scan_split(qd, kd, w, u, qk, dcrow, h):
    sl = slice(HEAD_DIM * h, HEAD_DIM * (h + 1))
    args = []
    for r0 in (0, CHUNK):
        rs = slice(r0, r0 + CHUNK)
        args += [qd[rs, sl], kd[rs, sl], w[rs, sl], u[rs, sl], qk[rs, sl], _pick_scalar(dcrow, r0, h)]
    return args


def _gdn_scan_fwd_fn(qd, kd, w, u, qk, dcrow, state):
    outs, new_state = [], []
    for h in range(HEADS):
        s = state[HEAD_DIM * h:HEAD_DIM * (h + 1), :]
        o_a, o_b, s2 = _gdn_scan_head(s, *_scan_split(qd, kd, w, u, qk, dcrow, h))
        outs.append(jnp.concatenate([o_a, o_b], axis=0))
        new_state.append(s2)
    return _cat(outs), state, jnp.concatenate(new_state, axis=0)


def _gdn_scan_bwd_fn(qd, kd, w, u, qk, dcrow, state, d_o, d_state):
    grads = [[] for _ in range(5)]
    d_dcrow = jnp.zeros(dcrow.shape, _F32)
    new_d_state = []
    for h in range(HEADS):
        sl = slice(HEAD_DIM * h, HEAD_DIM * (h + 1))
        s = state[sl, :]
        ins = [s] + _scan_split(qd, kd, w, u, qk, dcrow, h)
        cts = (d_o[0:CHUNK, sl], d_o[CHUNK:_PAIR, sl], d_state[sl, :])
        g = _vjp(_gdn_scan_head, ins, cts)
        new_d_state.append(g[0])
        for t in range(5):
            grads[t].append(jnp.concatenate([g[1 + t], g[7 + t]], axis=0))
        d_dcrow = d_dcrow + _put_scalar(g[6], 0, h, dcrow.shape) + _put_scalar(g[12], CHUNK, h, dcrow.shape)
    return tuple(_cat(gr) for gr in grads) + (d_dcrow, jnp.concatenate(new_d_state, axis=0))


def _gdn_intra_fwd_fn(q, k, v, bg):
    outs = [[] for _ in range(5)]
    dcrow = jnp.zeros(bg.shape, _F32)
    for h in range(HEADS):
        sl = slice(HEAD_DIM * h, HEAD_DIM * (h + 1))
        res = _gdn_intra_head(q[:, sl], k[:, sl], v[:, sl], _lane_pick(bg, h), _lane_pick(bg, HEADS + h))
        for t in range(5):
            outs[t].append(res[t])
        dcrow = dcrow + _lane_put(res[5], h)
    return tuple(_cat(o) for o in outs) + (dcrow,)


def _gdn_intra_bwd_fn(q, k, v, bg, d_w, d_u, d_qd, d_kd, d_qk, d_dcrow):
    grads = [[] for _ in range(3)]
    d_bg = jnp.zeros(bg.shape, _F32)
    for h in range(HEADS):
        sl = slice(HEAD_DIM * h, HEAD_DIM * (h + 1))
        ins = [q[:, sl], k[:, sl], v[:, sl], _lane_pick(bg, h), _lane_pick(bg, HEADS + h)]
        cts = (d_w[:, sl], d_u[:, sl], d_qd[:, sl], d_kd[:, sl], d_qk[:, sl], _lane_pick(d_dcrow, h))
        g = _vjp(_gdn_intra_head, ins, cts)
        for t in range(3):
            grads[t].append(g[t])
        d_bg = d_bg + _lane_put(g[3], h) + _lane_put(g[4], HEADS + h)
    return tuple(_cat(gr) for gr in grads) + (d_bg,)


def _gdn_gates(ba, alog, dtb):
    lane = lax.broadcasted_iota(jnp.int32, ba.shape, 1)
    beta = _sigmoid(ba)
    g = -jnp.exp(alog) * _softplus(ba + dtb)
    return jnp.where(lane < HEADS, beta, jnp.where(lane < 2 * HEADS, g, 0.0))


def _gdn_q_head(c):
    return _l2n(_silu(c)) * HEAD_DIM ** -0.5


def _gdn_k_head(c):
    return _l2n(_silu(c))


def _gdn_prep_fwd_fn(qkv_c, ba, alog, dtb):
    hs = _heads(qkv_c, 3 * HEADS)
    q = _cat([_gdn_q_head(c) for c in hs[:HEADS]])
    k = _cat([_gdn_k_head(c) for c in hs[HEADS:2 * HEADS]])
    v = _cat([_silu(c) for c in hs[2 * HEADS:]])
    return q, k, v, _gdn_gates(ba, alog, dtb)


def _gdn_prep_bwd_fn(qkv_c, ba, d_q, d_k, d_v, d_bg, alog, dtb):
    hs = _heads(qkv_c, 3 * HEADS)
    dqs, dks, dvs = _heads(d_q, HEADS), _heads(d_k, HEADS), _heads(d_v, HEADS)
    parts = [_vjp(_gdn_q_head, [hs[h]], (dqs[h],))[0] for h in range(HEADS)]
    parts += [_vjp(_gdn_k_head, [hs[HEADS + h]], (dks[h],))[0] for h in range(HEADS)]
    parts += [_vjp(_silu, [hs[2 * HEADS + h]], (dvs[h],))[0] for h in range(HEADS)]
    d_ba, d_alog, d_dtb = _vjp(_gdn_gates, [ba, alog, dtb], (d_bg,))
    return _cat(parts), d_ba, d_alog, d_dtb


def _a_out_head(o, gate, gain):
    return _rms(o, gain) * _silu(gate)


def _a_post_fwd_fn(o, gate, qm, gain, mem_kv):
    parts = [_a_out_head(oh, gh, gain) for oh, gh in zip(_heads(o, HEADS), _heads(gate, HEADS))]
    parts.append(_mem_attn(qm, mem_kv[:, :MEM_WIDTH], mem_kv[:, MEM_WIDTH:]))
    return (_cat(parts),)


def _a_post_bwd_fn(o, gate, qm, d_cat, gain, mem_kv):
    d_os, d_gates = [], []
    d_gain = jnp.zeros(gain.shape, _F32)
    dc = _heads(d_cat, HEADS + 2)
    for h, (oh, gh) in enumerate(zip(_heads(o, HEADS), _heads(gate, HEADS))):
        g = _vjp(_a_out_head, [oh, gh, gain], (dc[h],))
        d_os.append(g[0])
        d_gates.append(g[1])
        d_gain = d_gain + g[2]
    d_qm, d_km, d_vm = _vjp(_mem_attn, [qm, mem_kv[:, :MEM_WIDTH], mem_kv[:, MEM_WIDTH:]], (d_cat[:, A_WIDTH:],))
    return _cat(d_os), _cat(d_gates + [d_qm]), d_gain, _cat([d_km, d_vm])


def _b_post_fwd_fn(o, qm, mem_kv):
    return (_cat([o.astype(_F32), _mem_attn(qm, mem_kv[:, :MEM_WIDTH], mem_kv[:, MEM_WIDTH:])]),)


def _b_post_bwd_fn(qm, d_cat_m, mem_kv):
    d_qm, d_km, d_vm = _vjp(_mem_attn, [qm, mem_kv[:, :MEM_WIDTH], mem_kv[:, MEM_WIDTH:]], (d_cat_m,))
    return d_qm, _cat([d_km, d_vm])


ATTN_SCALE = (HEAD_DIM + QK_ROPE) ** -0.5


def _rope_q_fwd_fn(qf, cos_t, sin_t):
    hs = _heads(qf, 2 * HEADS)
    return (_cat(hs[:HEADS] + [_rope(x, cos_t, sin_t) for x in hs[HEADS:]]) * ATTN_SCALE,)


def _rope_q_bwd_fn(d_qn, d_qr, cos_t, sin_t):
    f = lambda x: _rope(x, cos_t, sin_t)
    return (_cat([d_qn] + [_vjp(f, [x], (x,))[0] for x in _heads(d_qr, HEADS)]) * ATTN_SCALE,)


def _kv_prep_fwd_fn(ckr, cos_t, sin_t, gain):
    return _rms(ckr[:, :KV_LORA], gain), _rope(ckr[:, KV_LORA:], cos_t, sin_t)


def _kv_prep_bwd_fn(ckr, d_ckv, d_kr, cos_t, sin_t, gain):
    d_lat, d_gain = _vjp(_rms, [ckr[:, :KV_LORA], gain], (d_ckv,))
    f = lambda x: _rope(x, cos_t, sin_t)
    d_rope = _vjp(f, [ckr[:, KV_LORA:]], (d_kr,))[0]
    return _cat([d_lat, d_rope]), d_gain


def _rms_fwd(x, gain, *, name, tm=1024, out_dtype=_BF):
    tm = min(tm, x.shape[0])
    return _rowmap(lambda x_, g_: (_rms(x_.astype(_F32), g_),), [_t(x)], [gain], [(None, x.shape[1], out_dtype)], tm=tm, name=name)[0]


def _rms_bwd(x, d_xn, d_res, gain, *, name, tm=512):
    tm = min(tm, x.shape[0])

    def fn(x_, dxn_, *rest):
        g_ = rest[-1]
        dx, dg = _vjp(_rms, [x_.astype(_F32), g_], (dxn_.astype(_F32),))
        if d_res is not None:
            dx = dx + rest[0]
        return dx, dg

    tiles = [_t(x), _t(d_xn)] + ([_t(d_res)] if d_res is not None else [])
    return _rowmap(fn, tiles, [gain], [(None, x.shape[1], _F32)], [(gain.shape, _F32)], tm=tm, name=name)


def _ffn_fwd(x, gain, w_gu, w_down, *, name):
    s = x.shape[0]
    xn = _rms_fwd(x, gain, name=name + "_norm")
    gu = _mm_nn(xn, w_gu, out_dtype=_BF, name=name + "_gu", tn=1408)

    def act(g_, u_):
        return (_silu(g_.astype(_F32)) * u_.astype(_F32),)

    a = _rowmap(act, [_t(gu, D_FF, 0), _t(gu, D_FF, 1)], [], [(None, D_FF, _BF)], tm=min(512, s), name=name + "_act")[0]
    y = _mm_nn(a, w_down, out_dtype=_F32, name=name + "_down", scale=0.5, res=x, tn=1024)
    return y, (x, xn, gu, a)


def _ffn_bwd(d_y, saved, gain, w_gu, w_down, *, name):
    x, xn, gu, a = saved
    s = x.shape[0]
    d_a = _mm_nt(d_y, w_down, out_dtype=_BF, name=name + "_da", scale=0.5, tn=1408)

    def act_bwd(g_, u_, da_):
        g, u, da = g_.astype(_F32), u_.astype(_F32), da_.astype(_F32)
        sg = _sigmoid(g)
        return (_cat([da * u * sg * (1.0 + g * (1.0 - sg)), da * g * sg]),)

    d_gu = _rowmap(act_bwd, [_t(gu, D_FF, 0), _t(gu, D_FF, 1), _t(d_a)], [], [(None, 2 * D_FF, _BF)], tm=min(512, s), name=name + "_dact")[0]
    d_w_down = _mm_tn(a, d_y, name=name + "_dwd", scale=0.5, t1=1408, tn=1024)
    d_w_gu = _mm_tn(xn, d_gu, name=name + "_dwgu", t1=1024, tn=1408)
    d_xn = _mm_nt(d_gu, w_gu, out_dtype=_F32, name=name + "_dxn", tn=1024)
    d_x, d_gain = _rms_bwd(x, d_xn, d_y, gain, name=name + "_dnorm")
    return d_x, d_gain, d_w_gu, d_w_down


def _conv_fwd(h, w, *, name, tm=256):
    s = h.shape[0]
    tm = min(tm, s)
    c = 3 * A_WIDTH
    halo = SUBLANE

    def body(x_ref, prev_ref, w_ref, o_ref, buf):
        i = pl.program_id(0)
        buf[0:halo, :] = jnp.where(i == 0, 0.0, prev_ref[...])
        buf[halo:halo + tm, :] = x_ref[...]
        acc = jnp.zeros((tm, c), _F32)
        for j in range(CONV_K):
            acc = acc + buf[pl.ds(halo - (CONV_K - 1) + j, tm), :] * w_ref[j:j + 1, :]
        o_ref[...] = acc

    return _pcall(
        body, name=name, grid=(s // tm,),
        in_specs=[pl.BlockSpec((tm, c), lambda i: (i, 0)),
                  pl.BlockSpec((halo, c), lambda i: (jnp.maximum(i * (tm // halo) - 1, 0), 0)),
                  pl.BlockSpec(w.shape, lambda i: (0, 0))],
        out_specs=pl.BlockSpec((tm, c), lambda i: (i, 0)), out_shape=jax.ShapeDtypeStruct((s, c), _F32),
        scratch_shapes=[pltpu.VMEM((tm + 2 * halo, c), _F32)],
        compiler_params=_cp(1),
    )(h, h, w)


def _conv_bwd(h, d_y, w, *, name, tm=256):
    s = h.shape[0]
    tm = min(tm, s)
    c = 3 * A_WIDTH
    halo = SUBLANE
    steps = s // tm

    def body(x_ref, prev_ref, dy_ref, next_ref, w_ref, dx_ref, dw_ref, xbuf, dybuf):
        i = pl.program_id(0)

        @pl.when(i == 0)
        def _():
            dw_ref[...] = jnp.zeros(dw_ref.shape, dw_ref.dtype)

        xbuf[0:halo, :] = jnp.where(i == 0, 0.0, prev_ref[...])
        xbuf[halo:halo + tm, :] = x_ref[...]
        dybuf[0:tm, :] = dy_ref[...]
        dybuf[tm:tm + halo, :] = jnp.where(i == steps - 1, 0.0, next_ref[...])
        dy = dy_ref[...]
        acc = jnp.zeros((tm, c), _F32)
        for j in range(CONV_K):
            acc = acc + dybuf[pl.ds(CONV_K - 1 - j, tm), :] * w_ref[j:j + 1, :]
            dw_ref[j:j + 1, :] += jnp.sum(dy * xbuf[pl.ds(halo - (CONV_K - 1) + j, tm), :], axis=0, keepdims=True)
        dx_ref[...] = acc

    return _pcall(
        body, name=name, grid=(steps,),
        in_specs=[pl.BlockSpec((tm, c), lambda i: (i, 0)),
                  pl.BlockSpec((halo, c), lambda i: (jnp.maximum(i * (tm // halo) - 1, 0), 0)),
                  pl.BlockSpec((tm, c), lambda i: (i, 0)),
                  pl.BlockSpec((halo, c), lambda i: (jnp.minimum((i + 1) * (tm // halo), s // halo - 1), 0)),
                  pl.BlockSpec(w.shape, lambda i: (0, 0))],
        out_specs=[pl.BlockSpec((tm, c), lambda i: (i, 0)), pl.BlockSpec(w.shape, lambda i: (0, 0))],
        out_shape=[jax.ShapeDtypeStruct((s, c), _F32), jax.ShapeDtypeStruct(w.shape, _F32)],
        scratch_shapes=[pltpu.VMEM((tm + 2 * halo, c), _F32), pltpu.VMEM((tm + 2 * halo, c), _F32)],
        compiler_params=_cp(1),
    )(h, h, d_y, d_y, w)


ATTN_BLOCK = 512


def _chunk_mask(shape, q_axis):
    qi = lax.broadcasted_iota(jnp.int32, shape, q_axis) // CHUNK
    ki = lax.broadcasted_iota(jnp.int32, shape, 1 - q_axis) // CHUNK
    return ki <= qi


def _rows(j, t):
    return pl.ds(pl.multiple_of(j * t, t), t)


def _attn_fwd(q_all, kv, kr, *, name):
    s = q_all.shape[0]
    t = min(ATTN_BLOCK, s)
    nq = s // t

    def body(qn_ref, qr_ref, kn_ref, kr_ref, v_ref, o_ref, lse_ref, lset_ref, m_sc, l_sc, acc_sc):
        i = pl.program_id(1)
        q = _cat([qn_ref[...], qr_ref[...]])
        m_sc[...] = jnp.full(m_sc.shape, -1e30, _F32)
        l_sc[...] = jnp.zeros(l_sc.shape, _F32)
        acc_sc[...] = jnp.zeros(acc_sc.shape, _F32)

        def block(j, masked):
            rows = _rows(j, t)
            sc = _dot_nt(q, _cat([kn_ref[rows, :], kr_ref[rows, :]]))
            if masked:
                sc = jnp.where(_chunk_mask(sc.shape, 0), sc, -1e30)
            m_prev = m_sc[...]
            m_new = jnp.maximum(m_prev, jnp.max(sc, axis=-1, keepdims=True))
            alpha = jnp.exp(m_prev - m_new)
            p = jnp.exp(sc - m_new)
            l_sc[...] = alpha * l_sc[...] + jnp.sum(p, axis=-1, keepdims=True)
            acc_sc[...] = alpha * acc_sc[...] + _dot(p, v_ref[rows, :])
            m_sc[...] = m_new

        def step(j, carry):
            block(j, False)
            return carry

        lax.fori_loop(0, i, step, 0)
        block(i, True)
        o_ref[...] = acc_sc[...] / l_sc[...]
        lse = jnp.broadcast_to(m_sc[...] + jnp.log(l_sc[...]), (t, LANE))
        lse_ref[...] = lse
        lset_ref[...] = lse.T[0:SUBLANE, :]

    return _pcall(
        body, name=name, grid=(HEADS, nq),
        in_specs=[pl.BlockSpec((t, LANE), lambda h, i: (i, h)),
                  pl.BlockSpec((t, LANE), lambda h, i: (i, HEADS + h)),
                  pl.BlockSpec((s, LANE), lambda h, i: (0, h)),
                  pl.BlockSpec((s, LANE), lambda h, i: (0, 0)),
                  pl.BlockSpec((s, LANE), lambda h, i: (0, HEADS + h))],
        out_specs=[pl.BlockSpec((t, LANE), lambda h, i: (i, h)),
                   pl.BlockSpec((t, LANE), lambda h, i: (i, h)),
                   pl.BlockSpec((SUBLANE, t), lambda h, i: (i * HEADS + h, 0))],
        out_shape=[jax.ShapeDtypeStruct((s, A_WIDTH), _F32), jax.ShapeDtypeStruct((s, A_WIDTH), _F32),
                   jax.ShapeDtypeStruct((nq * HEADS * SUBLANE, t), _F32)],
        scratch_shapes=[pltpu.VMEM((t, 1), _F32), pltpu.VMEM((t, 1), _F32), pltpu.VMEM((t, LANE), _F32)],
        compiler_params=_cp(2),
    )(q_all, q_all, kv, kr, kv)


def _attn_bwd_prep(o, d_cat, *, name):
    s = o.shape[0]
    t = min(ATTN_BLOCK, s)
    nq = s // t

    def body(o_ref, do_ref, dob_ref, dl_ref, dlt_ref):
        for h in range(HEADS):
            sl = slice(LANE * h, LANE * (h + 1))
            do = do_ref[:, sl]
            dl = jnp.broadcast_to(jnp.sum(o_ref[:, sl] * do, axis=-1, keepdims=True), (t, LANE))
            dob_ref[:, sl] = do.astype(dob_ref.dtype)
            dl_ref[:, sl] = dl
            dlt_ref[SUBLANE * h:SUBLANE * (h + 1), :] = dl.T[0:SUBLANE, :]

    return _pcall(
        body, name=name, grid=(nq,),
        in_specs=[pl.BlockSpec((t, A_WIDTH), lambda i: (i, 0)), pl.BlockSpec((t, A_WIDTH), lambda i: (i, 0))],
        out_specs=[pl.BlockSpec((t, A_WIDTH), lambda i: (i, 0)), pl.BlockSpec((t, A_WIDTH), lambda i: (i, 0)),
                   pl.BlockSpec((HEADS * SUBLANE, t), lambda i: (i, 0))],
        out_shape=[jax.ShapeDtypeStruct((s, A_WIDTH), _BF), jax.ShapeDtypeStruct((s, A_WIDTH), _F32),
                   jax.ShapeDtypeStruct((nq * HEADS * SUBLANE, t), _F32)],
        compiler_params=_cp(1),
    )(o, d_cat)


def _attn_dq(q_all, kv, kr, lse, delta, d_o, *, name):
    s = q_all.shape[0]
    t = min(ATTN_BLOCK, s)
    nq = s // t

    def body(qn_ref, qr_ref, kn_ref, kr_ref, v_ref, lse_ref, dl_ref, do_ref, dqn_ref, dqr_ref, acc_sc):
        i = pl.program_id(1)
        q = _cat([qn_ref[...], qr_ref[...]])
        do = do_ref[...]
        lse = lse_ref[...][:, 0:1]
        dl = dl_ref[...][:, 0:1]
        acc_sc[...] = jnp.zeros(acc_sc.shape, _F32)

        def block(j, masked):
            rows = _rows(j, t)
            k = _cat([kn_ref[rows, :], kr_ref[rows, :]])
            p = jnp.exp(_dot_nt(q, k) - lse)
            if masked:
                p = jnp.where(_chunk_mask(p.shape, 0), p, 0.0)
            ds = p * (_dot_nt(do, v_ref[rows, :]) - dl)
            acc_sc[...] += _dot(ds, k)

        def step(j, carry):
            block(j, False)
            return carry

        lax.fori_loop(0, i, step, 0)
        block(i, True)
        dqn_ref[...] = acc_sc[:, 0:LANE]
        dqr_ref[...] = acc_sc[:, LANE:2 * LANE]

    return _pcall(
        body, name=name, grid=(HEADS, nq),
        in_specs=[pl.BlockSpec((t, LANE), lambda h, i: (i, h)),
                  pl.BlockSpec((t, LANE), lambda h, i: (i, HEADS + h)),
                  pl.BlockSpec((s, LANE), lambda h, i: (0, h)),
                  pl.BlockSpec((s, LANE), lambda h, i: (0, 0)),
                  pl.BlockSpec((s, LANE), lambda h, i: (0, HEADS + h)),
                  pl.BlockSpec((t, LANE), lambda h, i: (i, h)),
                  pl.BlockSpec((t, LANE), lambda h, i: (i, h)),
                  pl.BlockSpec((t, LANE), lambda h, i: (i, h))],
        out_specs=[pl.BlockSpec((t, LANE), lambda h, i: (i, h)), pl.BlockSpec((t, LANE), lambda h, i: (i, h))],
        out_shape=[jax.ShapeDtypeStruct((s, A_WIDTH), _F32), jax.ShapeDtypeStruct((s, A_WIDTH), _F32)],
        scratch_shapes=[pltpu.VMEM((t, 2 * LANE), _F32)],
        compiler_params=_cp(2),
    )(q_all, q_all, kv, kr, kv, lse, delta, d_o)


def _attn_dkv(q_all, kv, kr, lse_t, delta_t, d_o, *, name):
    s = q_all.shape[0]
    t = min(ATTN_BLOCK, s)
    nq = s // t

    def body(kn_ref, kr_ref, v_ref, qn_ref, qr_ref, do_ref, lset_ref, dlt_ref, dkn_ref, dv_ref, dkr_ref, dk_sc, dv_sc):
        h, j = pl.program_id(0), pl.program_id(1)
        k = _cat([kn_ref[...], kr_ref[...]])
        v = v_ref[...]
        dk_sc[...] = jnp.zeros(dk_sc.shape, _F32)
        dv_sc[...] = jnp.zeros(dv_sc.shape, _F32)

        def block(i, masked):
            rows = _rows(i, t)
            stat = pl.ds(pl.multiple_of((i * HEADS + h) * SUBLANE, SUBLANE), 1)
            q = _cat([qn_ref[rows, :], qr_ref[rows, :]])
            do = do_ref[rows, :]
            p = jnp.exp(_dot_nt(k, q) - lset_ref[stat, :])
            if masked:
                p = jnp.where(_chunk_mask(p.shape, 1), p, 0.0)
            dv_sc[...] += _dot(p, do)
            ds = p * (_dot_nt(v, do) - dlt_ref[stat, :])
            dk_sc[...] += _dot(ds, q)

        def step(i, carry):
            block(i, False)
            return carry

        block(j, True)
        lax.fori_loop(j + 1, nq, step, 0)
        dkn_ref[...] = dk_sc[:, 0:LANE]
        dkr_ref[...] = dk_sc[:, LANE:2 * LANE]
        dv_ref[...] = dv_sc[...]

    stats = pl.BlockSpec((nq * HEADS * SUBLANE, t), lambda h, j: (0, 0))
    return _pcall(
        body, name=name, grid=(HEADS, nq),
        in_specs=[pl.BlockSpec((t, LANE), lambda h, j: (j, h)),
                  pl.BlockSpec((t, LANE), lambda h, j: (j, 0)),
                  pl.BlockSpec((t, LANE), lambda h, j: (j, HEADS + h)),
                  pl.BlockSpec((s, LANE), lambda h, j: (0, h)),
                  pl.BlockSpec((s, LANE), lambda h, j: (0, HEADS + h)),
                  pl.BlockSpec((s, LANE), lambda h, j: (0, h)),
                  stats, stats],
        out_specs=[pl.BlockSpec((t, LANE), lambda h, j: (j, h))] * 3,
        out_shape=[jax.ShapeDtypeStruct((s, A_WIDTH), _F32)] * 3,
        scratch_shapes=[pltpu.VMEM((t, 2 * LANE), _F32), pltpu.VMEM((t, LANE), _F32)],
        compiler_params=_cp(2),
    )(kv, kr, kv, q_all, q_all, d_o, lse_t, delta_t)


def _final_loss(x, tgt, gain, *, name, tm=512):
    tm = min(tm, x.shape[0])

    def fn(x_, t_, g_):
        def f(xx, gg):
            err = _rms(xx, gg) - t_
            return 0.5 * jnp.sum(jnp.sum(err * err, axis=1, keepdims=True) / D_MODEL, axis=0, keepdims=True)

        loss, pull = jax.vjp(f, x_, g_)
        dx, dg = pull(jnp.ones((1, 1), _F32))
        return dx, dg, jnp.broadcast_to(loss, (SUBLANE, LANE))

    return _rowmap(fn, [_t(x), _t(tgt)], [gain], [(None, D_MODEL, _F32)], [(gain.shape, _F32), ((SUBLANE, LANE), _F32)], tm=tm, name=name)


def _local_step(x, mem, cos_t, sin_t, tgt, p):
    s = x.shape[0]
    g = {}
    row = lambda a: a.reshape(1, -1)
    tm_e = min(256, s)

    mem_n = _rms_fwd(mem, row(p["mem_norm"]), name="mem_norm")
    mem_kv_all = _mm_nn(mem_n, p["w_mem_all"], out_dtype=_F32, name="mem_kv", tn=1024)
    mem_kv = [mem_kv_all[:, 2 * MEM_WIDTH * l:2 * MEM_WIDTH * (l + 1)] for l in range(DEPTH)]

    sv = []
    for i in range(N_A):
        l = i
        r = {}
        r["x0"] = x
        x, r["ffn1"] = _ffn_fwd(x, row(p["ffn1_norm"][l]), p["ffn1_w_gu"][l], p["ffn1_w_down"][l], name=f"a{i}_ffn1")
        r["x1"] = x
        xn = _rms_fwd(x, row(p["mix_norm"][l]), name=f"a{i}_mixnorm")
        h = _mm_nn(xn, p["a_w_in"][i], out_dtype=_F32, name=f"a{i}_in", tn=1152)
        qkv_c = _conv_fwd(h, p["a_conv"][i], name=f"a{i}_conv")
        alog, dtb = p["a_A_log_row"][i], p["a_dt_bias_row"][i]
        q, k, v, bg = _rowmap(_gdn_prep_fwd_fn, [_t(qkv_c), _t(h, LANE, 26)], [alog, dtb],
                              [(None, A_WIDTH, _F32)] * 3 + [(None, LANE, _F32)], tm=tm_e, name=f"a{i}_prep")
        w_, u_, qd, kd, qk, dcrow = _rowmap(_gdn_intra_fwd_fn, [_t(q), _t(k), _t(v), _t(bg)], [],
                                            [(None, A_WIDTH, _F32)] * 5 + [(None, LANE, _F32)], tm=_PAIR, name=f"a{i}_intra")
        o, states = _rowmap(_gdn_scan_fwd_fn, [_t(qd), _t(kd), _t(w_), _t(u_), _t(qk), _t(dcrow)], [],
                            [(None, A_WIDTH, _F32), (A_WIDTH, HEAD_DIM, _F32)], tm=_PAIR, name=f"a{i}_scan",
                            carry=[((A_WIDTH, HEAD_DIM), _F32)])
        gain_o = row(p["a_out_norm"][i])
        cat = _rowmap(_a_post_fwd_fn, [_t(o), _t(h, A_WIDTH, 3), _t(h, MEM_WIDTH, 12)], [gain_o, mem_kv[l]],
                      [(None, D_MODEL, _BF)], tm=tm_e, name=f"a{i}_post")[0]
        x = _mm_nn(cat, p["w_out"][l], out_dtype=_F32, name=f"a{i}_out", res=x, tn=1024)
        r.update(xn=xn, h=h, qkv_c=qkv_c, q=q, k=k, v=v, bg=bg, w=w_, u=u_, qd=qd, kd=kd, qk=qk, dcrow=dcrow, o=o, states=states, cat=cat)
        r["x2"] = x
        x, r["ffn2"] = _ffn_fwd(x, row(p["ffn2_norm"][l]), p["ffn2_w_gu"][l], p["ffn2_w_down"][l], name=f"a{i}_ffn2")
        sv.append(r)

    kvs = {"x": x}
    xn_kv = _rms_fwd(x, row(p["kv_in_norm"]), name="kv_innorm")
    ckr = _mm_nn(xn_kv, p["w_dkv"], out_dtype=_F32, name="kv_down")
    ckv, k_rope = _rowmap(_kv_prep_fwd_fn, [_t(ckr), _t(cos_t), _t(sin_t)], [row(p["kv_lat_norm"])],
                          [(None, KV_LORA, _BF), (None, LANE, _BF)], tm=tm_e, name="kv_prep")
    kvu = _mm_nn(ckv, p["w_ukv"], out_dtype=_BF, name="kv_up")
    kvs.update(xn=xn_kv, ckr=ckr, ckv=ckv)

    for j in range(N_B):
        l = N_A + j
        r = {}
        x, r["ffn1"] = _ffn_fwd(x, row(p["ffn1_norm"][l]), p["ffn1_w_gu"][l], p["ffn1_w_down"][l], name=f"b{j}_ffn1")
        r["x1"] = x
        xn = _rms_fwd(x, row(p["mix_norm"][l]), name=f"b{j}_mixnorm")
        h = _mm_nn(xn, p["b_w_in"][j], out_dtype=_F32, name=f"b{j}_in")
        gain_q = row(p["b_q_norm"][j])
        cqn = _rowmap(lambda c_, g_: (_rms(c_, g_),), [_t(h, Q_LORA, 0)], [gain_q], [(None, Q_LORA, _BF)], tm=tm_e, name=f"b{j}_qnorm")[0]
        qf = _mm_nn(cqn, p["b_w_uq"][j], out_dtype=_F32, name=f"b{j}_uq")
        q_all = _rowmap(_rope_q_fwd_fn, [_t(qf), _t(cos_t), _t(sin_t)], [], [(None, UQ_PAD, _BF)], tm=tm_e, name=f"b{j}_rope")[0]
        o_b, lse, lse_t = _attn_fwd(q_all, kvu, k_rope, name=f"b{j}_attn")
        cat = _rowmap(_b_post_fwd_fn, [_t(o_b), _t(h, MEM_WIDTH, 1)], [mem_kv[l]], [(None, D_MODEL, _BF)], tm=tm_e, name=f"b{j}_post")[0]
        x = _mm_nn(cat, p["w_out"][l], out_dtype=_F32, name=f"b{j}_out", res=x, tn=1024)
        r.update(xn=xn, h=h, cqn=cqn, q_all=q_all, o_b=o_b, lse=lse, lse_t=lse_t, cat=cat)
        x, r["ffn2"] = _ffn_fwd(x, row(p["ffn2_norm"][l]), p["ffn2_w_gu"][l], p["ffn2_w_down"][l], name=f"b{j}_ffn2")
        sv.append(r)

    dx, g["final_norm"], loss = _final_loss(x, tgt, row(p["final_norm"]), name="loss")

    per_layer = lambda: [None] * DEPTH
    for n in ("ffn1_norm", "ffn1_w_gu", "ffn1_w_down", "mix_norm", "ffn2_norm", "ffn2_w_gu", "ffn2_w_down", "w_out", "mem_kv"):
        g[n] = per_layer()
    for n in ("a_w_in", "a_conv", "a_A_log_row", "a_dt_bias_row", "a_out_norm", "b_w_in", "b_q_norm", "b_w_uq"):
        g[n] = [None] * N_A
    d_kv_parts = []

    for j in reversed(range(N_B)):
        l = N_A + j
        r = sv[l]
        dx, g["ffn2_norm"][l], g["ffn2_w_gu"][l], g["ffn2_w_down"][l] = _ffn_bwd(
            dx, r["ffn2"], row(p["ffn2_norm"][l]), p["ffn2_w_gu"][l], p["ffn2_w_down"][l], name=f"b{j}_ffn2b")
        d_cat = _mm_nt(dx, p["w_out"][l], out_dtype=_F32, name=f"b{j}_dcat", tn=1024)
        g["w_out"][l] = _mm_tn(r["cat"], dx, name=f"b{j}_dwout", tn=1024)
        d_qm, g["mem_kv"][l] = _rowmap(_b_post_bwd_fn, [_t(r["h"], MEM_WIDTH, 1), _t(d_cat, MEM_WIDTH, 3)], [mem_kv[l]],
                                      [(None, MEM_WIDTH, _F32)], [((N_MEM, 2 * MEM_WIDTH), _F32)], tm=tm_e, name=f"b{j}_postb")
        d_o, delta, delta_t = _attn_bwd_prep(r["o_b"], d_cat, name=f"b{j}_delta")
        dqn, dqr = _attn_dq(r["q_all"], kvu, k_rope, r["lse"], delta, d_o, name=f"b{j}_attn_dq")
        d_kv_parts.append(_attn_dkv(r["q_all"], kvu, k_rope, r["lse_t"], delta_t, d_o, name=f"b{j}_attn_dkv"))
        d_qf = _rowmap(_rope_q_bwd_fn, [_t(dqn), _t(dqr), _t(cos_t), _t(sin_t)], [], [(None, UQ_PAD, _F32)], tm=tm_e, name=f"b{j}_ropeb")[0]
        d_cqn = _mm_nt(d_qf, p["b_w_uq"][j], out_dtype=_F32, name=f"b{j}_dcqn")
        g["b_w_uq"][j] = _mm_tn(r["cqn"], d_qf, name=f"b{j}_dwuq")
        gain_q = row(p["b_q_norm"][j])
        d_cq, g["b_q_norm"][j] = _rowmap(lambda c_, d_, g_: _vjp(_rms, [c_, g_], (d_,)), [_t(r["h"], Q_LORA, 0), _t(d_cqn)], [gain_q],
                                        [(None, Q_LORA, _F32)], [((1, Q_LORA), _F32)], tm=tm_e, name=f"b{j}_qnormb")
        d_h = jnp.concatenate([d_cq, d_qm], axis=1)
        d_xn = _mm_nt(d_h, p["b_w_in"][j], out_dtype=_F32, name=f"b{j}_dxn", tn=1024)
        g["b_w_in"][j] = _mm_tn(r["xn"], d_h, name=f"b{j}_dwin")
        dx, g["mix_norm"][l] = _rms_bwd(r["x1"], d_xn, dx, row(p["mix_norm"][l]), name=f"b{j}_mixnormb")
        dx, g["ffn1_norm"][l], g["ffn1_w_gu"][l], g["ffn1_w_down"][l] = _ffn_bwd(
            dx, r["ffn1"], row(p["ffn1_norm"][l]), p["ffn1_w_gu"][l], p["ffn1_w_down"][l], name=f"b{j}_ffn1b")

    def kv_sum(*parts):
        dkn = sum(parts[0::3][1:], parts[0])
        dv = sum(parts[1::3][1:], parts[1])
        dkr = sum(parts[2::3][1:], parts[2])
        return _cat([dkn, dv]), sum(_heads(dkr, HEADS)[1:], _heads(dkr, HEADS)[0])

    d_kvu, d_kr = _rowmap(kv_sum, [_t(a) for part in d_kv_parts for a in part], [], [(None, 2 * A_WIDTH, _F32), (None, LANE, _F32)],
                          tm=tm_e, name="kv_dsum")
    d_ckv = _mm_nt(d_kvu, p["w_ukv"], out_dtype=_F32, name="kv_dckv")
    g["w_ukv"] = _mm_tn(kvs["ckv"], d_kvu, name="kv_dwukv")
    d_ckr, g["kv_lat_norm"] = _rowmap(_kv_prep_bwd_fn, [_t(kvs["ckr"]), _t(d_ckv), _t(d_kr), _t(cos_t), _t(sin_t)], [row(p["kv_lat_norm"])],
                                     [(None, DKV_PAD, _F32)], [((1, KV_LORA), _F32)], tm=tm_e, name="kv_prepb")
    d_xn = _mm_nt(d_ckr, p["w_dkv"], out_dtype=_F32, name="kv_dxn", tn=1024)
    g["w_dkv"] = _mm_tn(kvs["xn"], d_ckr, name="kv_dwdkv")
    dx, g["kv_in_norm"] = _rms_bwd(kvs["x"], d_xn, dx, row(p["kv_in_norm"]), name="kv_innormb")

    for i in reversed(range(N_A)):
        l = i
        r = sv[l]
        dx, g["ffn2_norm"][l], g["ffn2_w_gu"][l], g["ffn2_w_down"][l] = _ffn_bwd(
            dx, r["ffn2"], row(p["ffn2_norm"][l]), p["ffn2_w_gu"][l], p["ffn2_w_down"][l], name=f"a{i}_ffn2b")
        d_cat = _mm_nt(dx, p["w_out"][l], out_dtype=_F32, name=f"a{i}_dcat", tn=1024)
        g["w_out"][l] = _mm_tn(r["cat"], dx, name=f"a{i}_dwout", tn=1024)
        gain_o = row(p["a_out_norm"][i])
        h = r["h"]
        d_o, d_hpart, g["a_out_norm"][i], g["mem_kv"][l] = _rowmap(
            _a_post_bwd_fn, [_t(r["o"]), _t(h, A_WIDTH, 3), _t(h, MEM_WIDTH, 12), _t(d_cat)], [gain_o, mem_kv[l]],
            [(None, A_WIDTH, _F32), (None, D_MODEL, _F32)], [((1, HEAD_DIM), _F32), ((N_MEM, 2 * MEM_WIDTH), _F32)], tm=tm_e, name=f"a{i}_postb")
        d_qd, d_kd, d_w, d_u, d_qk, d_dcrow = _rowmap(
            _gdn_scan_bwd_fn, [_t(r["qd"]), _t(r["kd"]), _t(r["w"]), _t(r["u"]), _t(r["qk"]), _t(r["dcrow"]), _t(r["states"], rows=A_WIDTH), _t(d_o)], [],
            [(None, A_WIDTH, _F32)] * 5 + [(None, LANE, _F32)], tm=_PAIR, name=f"a{i}_scanb", carry=[((A_WIDTH, HEAD_DIM), _F32)], reverse=True)
        d_q, d_k, d_v, d_bg = _rowmap(
            _gdn_intra_bwd_fn, [_t(r["q"]), _t(r["k"]), _t(r["v"]), _t(r["bg"]), _t(d_w), _t(d_u), _t(d_qd), _t(d_kd), _t(d_qk), _t(d_dcrow)], [],
            [(None, A_WIDTH, _F32)] * 3 + [(None, LANE, _F32)], tm=_PAIR, name=f"a{i}_intrab")
        alog, dtb = p["a_A_log_row"][i], p["a_dt_bias_row"][i]
        d_qkv_c, d_ba, g["a_A_log_row"][i], g["a_dt_bias_row"][i] = _rowmap(
            _gdn_prep_bwd_fn, [_t(r["qkv_c"]), _t(h, LANE, 26), _t(d_q), _t(d_k), _t(d_v), _t(d_bg)], [alog, dtb],
            [(None, 3 * A_WIDTH, _F32), (None, LANE, _F32)], [((1, LANE), _F32), ((1, LANE), _F32)], tm=tm_e, name=f"a{i}_prepb")
        d_qkv, g["a_conv"][i] = _conv_bwd(h, d_qkv_c, p["a_conv"][i], name=f"a{i}_convb")
        d_h = jnp.concatenate([d_qkv, d_hpart, d_ba], axis=1)
        d_xn = _mm_nt(d_h, p["a_w_in"][i], out_dtype=_F32, name=f"a{i}_dxn", tn=1024)
        g["a_w_in"][i] = _mm_tn(r["xn"], d_h, name=f"a{i}_dwin", tn=1152)
        dx, g["mix_norm"][l] = _rms_bwd(r["x1"], d_xn, dx, row(p["mix_norm"][l]), name=f"a{i}_mixnormb")
        dx, g["ffn1_norm"][l], g["ffn1_w_gu"][l], g["ffn1_w_down"][l] = _ffn_bwd(
            dx, r["ffn1"], row(p["ffn1_norm"][l]), p["ffn1_w_gu"][l], p["ffn1_w_down"][l], name=f"a{i}_ffn1b")

    d_mem_kv_all = jnp.concatenate(g.pop("mem_kv"), axis=1)
    d_mem_n = _mm_nt(d_mem_kv_all, p["w_mem_all"], out_dtype=_F32, name="mem_dn", tn=1024)
    g["w_mem_all"] = _mm_tn(mem_n, d_mem_kv_all, name="mem_dw", tn=1024)
    _, g["mem_norm"] = _rms_bwd(mem, d_mem_n, None, row(p["mem_norm"]), name="mem_normb")
    return loss, dx, g


_NOPE_ROPE = HEAD_DIM + QK_ROPE
_QKV_GATE = 4 * A_WIDTH
_BETA_AT = _QKV_GATE + MEM_WIDTH


def _lane_row(vals, at):
    return jnp.zeros((1, LANE), _F32).at[0, at:at + vals.shape[0]].set(vals.astype(_F32))


def _compute_form(w, conv_f32):
    p = {n: w[n] for n in ("ffn1_norm", "ffn1_w_gu", "ffn1_w_down", "mix_norm", "ffn2_norm", "ffn2_w_gu", "ffn2_w_down", "w_out",
                           "mem_norm", "a_out_norm", "b_w_in", "b_q_norm", "kv_in_norm", "kv_lat_norm", "final_norm")}
    wm = w["w_mem_kv"]
    p["w_mem_all"] = jnp.transpose(wm, (1, 0, 2)).reshape(D_MODEL, DEPTH * 2 * MEM_WIDTH)
    a = w["a_w_in"]
    pad = jnp.zeros((N_A, D_MODEL, A_IN_PAD - A_IN), a.dtype)
    p["a_w_in"] = jnp.concatenate([a[:, :, :_QKV_GATE], a[:, :, _QKV_GATE + 2 * HEADS:], a[:, :, _QKV_GATE:_QKV_GATE + 2 * HEADS], pad], axis=2)
    p["a_conv"] = jnp.concatenate([conv_f32, jnp.zeros((N_A, SUBLANE - CONV_K, 3 * A_WIDTH), _F32)], axis=1)
    p["a_A_log_row"] = [_lane_row(w["a_A_log"][i], HEADS) for i in range(N_A)]
    p["a_dt_bias_row"] = [_lane_row(w["a_dt_bias"][i], HEADS) for i in range(N_A)]
    uq = w["b_w_uq"].reshape(N_B, Q_LORA, HEADS, _NOPE_ROPE)
    rope = jnp.concatenate([uq[..., HEAD_DIM:], jnp.zeros((N_B, Q_LORA, HEADS, LANE - QK_ROPE), uq.dtype)], axis=-1)
    p["b_w_uq"] = jnp.concatenate([uq[..., :HEAD_DIM].reshape(N_B, Q_LORA, A_WIDTH), rope.reshape(N_B, Q_LORA, A_WIDTH)], axis=-1)
    dkv = w["w_dkv"]
    p["w_dkv"] = jnp.concatenate([dkv, jnp.zeros((D_MODEL, DKV_PAD - dkv.shape[1]), dkv.dtype)], axis=1)
    ukv = w["w_ukv"].reshape(KV_LORA, HEADS, 2 * HEAD_DIM)
    p["w_ukv"] = jnp.concatenate([ukv[..., :HEAD_DIM].reshape(KV_LORA, A_WIDTH), ukv[..., HEAD_DIM:].reshape(KV_LORA, A_WIDTH)], axis=-1)
    return p


def _natural_grads(g):
    st = lambda xs: jnp.stack(xs, axis=0)
    n = {}
    for k in ("ffn1_norm", "mix_norm", "ffn2_norm"):
        n[k] = st(g[k]).reshape(DEPTH, D_MODEL)
    for k in ("ffn1_w_gu", "ffn1_w_down", "ffn2_w_gu", "ffn2_w_down", "w_out", "b_w_in"):
        n[k] = st(g[k])
    n["mem_norm"] = g["mem_norm"].reshape(D_MODEL)
    n["w_mem_kv"] = jnp.transpose(g["w_mem_all"].reshape(D_MODEL, DEPTH, 2 * MEM_WIDTH), (1, 0, 2))
    a = st(g["a_w_in"])
    n["a_w_in"] = jnp.concatenate([a[:, :, :_QKV_GATE], a[:, :, _BETA_AT:_BETA_AT + 2 * HEADS], a[:, :, _QKV_GATE:_BETA_AT]], axis=2)
    n["a_conv"] = st(g["a_conv"])[:, :CONV_K]
    n["a_A_log"] = st(g["a_A_log_row"])[:, 0, HEADS:2 * HEADS]
    n["a_dt_bias"] = st(g["a_dt_bias_row"])[:, 0, HEADS:2 * HEADS]
    n["a_out_norm"] = st(g["a_out_norm"]).reshape(N_A, HEAD_DIM)
    n["b_q_norm"] = st(g["b_q_norm"]).reshape(N_B, Q_LORA)
    uq = st(g["b_w_uq"])
    nope = uq[:, :, :A_WIDTH].reshape(N_B, Q_LORA, HEADS, HEAD_DIM)
    rope = uq[:, :, A_WIDTH:].reshape(N_B, Q_LORA, HEADS, LANE)[..., :QK_ROPE]
    n["b_w_uq"] = jnp.concatenate([nope, rope], axis=-1).reshape(N_B, Q_LORA, HEADS * _NOPE_ROPE)
    n["kv_in_norm"] = g["kv_in_norm"].reshape(D_MODEL)
    n["w_dkv"] = g["w_dkv"][:, :KV_LORA + QK_ROPE]
    n["kv_lat_norm"] = g["kv_lat_norm"].reshape(KV_LORA)
    ukv = g["w_ukv"]
    n["w_ukv"] = jnp.concatenate([ukv[:, :A_WIDTH].reshape(KV_LORA, HEADS, HEAD_DIM), ukv[:, A_WIDTH:].reshape(KV_LORA, HEADS, HEAD_DIM)],
                                 axis=-1).reshape(KV_LORA, HEADS * 2 * HEAD_DIM)
    n["final_norm"] = g["final_norm"].reshape(D_MODEL)
    return n


def _rope_tables(positions):
    inv = ROPE_THETA ** (-jnp.arange(0, QK_ROPE, 2, dtype=_F32) / QK_ROPE)
    ang = positions.astype(_F32)[:, None] * inv
    z = jnp.zeros((positions.shape[0], LANE - QK_ROPE), _F32)
    cos, sin = jnp.cos(ang), jnp.sin(ang)
    return jnp.concatenate([cos, cos, z], axis=1), jnp.concatenate([sin, sin, z], axis=1)


_HBM = pl.BlockSpec(memory_space=pltpu.HBM)


def _place():
    x, y, c = lax.axis_index("x"), lax.axis_index("y"), lax.axis_index("c")
    return x, y, c, [(1 - x, y), (x, 1 - y), (1 - x, 1 - y)]


def _remote(src, dst, send_sem, recv_sem, to):
    return pltpu.make_async_remote_copy(src_ref=src, dst_ref=dst, send_sem=send_sem, recv_sem=recv_sem, device_id=to, device_id_type=_MESH)


def _gather_over_chips(shard, *, name):
    rows, cols = shard.shape
    half = rows // 2

    def body(w_ref, out_ref, send_sems, recv_sems, local_sem):
        x, y, c, chips = _place()
        k = 2 * x + y

        def part(chip, h):
            return out_ref.at[chip, pl.ds(h * half, half), :]

        mine = pltpu.make_async_copy(w_ref, out_ref.at[k], local_sem)
        mine.start()
        first = [_remote(w_ref.at[pl.ds(c * half, half), :], part(k, c), send_sems.at[j], recv_sems.at[j], (px, py, c))
                 for j, (px, py) in enumerate(chips)]
        for cp in first:
            cp.start()
        passed = []
        for j, (px, py) in enumerate(chips):
            got = part(2 * px + py, c)
            _remote(got, got, send_sems.at[j], recv_sems.at[j], (px, py, c)).wait_recv()
            fw = _remote(got, got, send_sems.at[3 + j], recv_sems.at[3 + j], (x, y, 1 - c))
            fw.start()
            passed.append(fw)
        for j, (px, py) in enumerate(chips):
            got = part(2 * px + py, 1 - c)
            _remote(got, got, send_sems.at[3 + j], recv_sems.at[3 + j], (x, y, 1 - c)).wait_recv()
        for cp in first + passed:
            cp.wait_send()
        mine.wait()

    return _pcall(
        body, name=name, in_specs=[_HBM], out_specs=_HBM, out_shape=jax.ShapeDtypeStruct((N_CHIPS, rows, cols), shard.dtype),
        scratch_shapes=[pltpu.SemaphoreType.DMA((6,)), pltpu.SemaphoreType.DMA((6,)), pltpu.SemaphoreType.DMA],
    )(shard)


def _pair_exchange(v, *, name):
    def body(v_ref, out_ref, send_sem, recv_sem):
        x, y, c, _ = _place()
        cp = _remote(v_ref, out_ref, send_sem, recv_sem, (x, y, 1 - c))
        cp.start()
        cp.wait()

    return _pcall(body, name=name, in_specs=[_HBM], out_specs=_HBM, out_shape=jax.ShapeDtypeStruct(v.shape, v.dtype),
                  scratch_shapes=[pltpu.SemaphoreType.DMA, pltpu.SemaphoreType.DMA])(v)


def _scatter_over_chips(v, *, name):
    def body(v_ref, out_ref, send_sems, recv_sems):
        x, y, c, chips = _place()
        cps = [_remote(v_ref.at[2 * px + py], out_ref.at[j], send_sems.at[j], recv_sems.at[j], (px, py, c)) for j, (px, py) in enumerate(chips)]
        for cp in cps:
            cp.start()
        for cp in cps:
            cp.wait()

    return _pcall(body, name=name, in_specs=[_HBM], out_specs=_HBM, out_shape=jax.ShapeDtypeStruct((N_CHIPS - 1,) + v.shape[1:], v.dtype),
                  scratch_shapes=[pltpu.SemaphoreType.DMA((3,)), pltpu.SemaphoreType.DMA((3,))])(v)


def _all_reduce_small(v, *, name):
    def body(v_ref, out_ref, all_ref, send_sems, recv_sems):
        x, y, c, _ = _place()
        me = 4 * x + 2 * y + c
        all_ref[me] = v_ref[...]
        cps = []
        for f in range(1, N_DEV):
            fx, fy, fc = (f >> 2) & 1, (f >> 1) & 1, f & 1
            to = (x + fx - 2 * x * fx, y + fy - 2 * y * fy, c + fc - 2 * c * fc)
            cps.append(_remote(v_ref, all_ref.at[me], send_sems.at[f - 1], recv_sems.at[f - 1], to))
        for cp in cps:
            cp.start()
        for cp in cps:
            cp.wait()
        acc = all_ref[0]
        for d in range(1, N_DEV):
            acc = acc + all_ref[d]
        out_ref[...] = acc

    vm = pl.BlockSpec(memory_space=pltpu.VMEM)
    return _pcall(body, name=name, in_specs=[vm], out_specs=vm, out_shape=jax.ShapeDtypeStruct(v.shape, v.dtype),
                  scratch_shapes=[pltpu.VMEM((N_DEV,) + v.shape, v.dtype), pltpu.SemaphoreType.DMA((N_DEV - 1,)), pltpu.SemaphoreType.DMA((N_DEV - 1,))])(v)


_BIG = (("ffn1_w_gu", 2), ("ffn1_w_down", 1), ("ffn2_w_gu", 2), ("ffn2_w_down", 1), ("w_out", 1), ("w_mem_kv", 1), ("a_w_in", 2),
        ("a_conv", 2), ("b_w_in", 1), ("b_w_uq", 2), ("w_dkv", 0), ("w_ukv", 1))
_SMALL = ("ffn1_norm", "mix_norm", "ffn2_norm", "mem_norm", "a_A_log", "a_dt_bias", "a_out_norm", "b_q_norm", "kv_in_norm", "kv_lat_norm",
          "final_norm")
_WEIGHTS = ("ffn1_norm", "ffn1_w_gu", "ffn1_w_down", "mix_norm", "ffn2_norm", "ffn2_w_gu", "ffn2_w_down", "w_out", "mem_norm", "w_mem_kv",
            "a_w_in", "a_conv", "a_A_log", "a_dt_bias", "a_out_norm", "b_w_in", "b_q_norm", "b_w_uq", "kv_in_norm", "w_dkv", "kv_lat_norm",
            "w_ukv", "final_norm")
SMALL_ROWS = 24


def _pack(arrs, rows):
    flat = jnp.concatenate([a.reshape(-1) for a in arrs])
    return jnp.pad(flat, (0, rows * PACK_COLS - flat.shape[0])).reshape(rows, PACK_COLS)


def _unpack(packed, shapes):
    flat, off, out = packed.reshape(-1), 0, []
    for shp in shapes:
        n = math.prod(shp)
        out.append(flat[off:off + n].reshape(shp))
        off += n
    return out


def _add(parts, *, name, narrow_copy=False):
    def fn(*ts):
        acc = ts[0].astype(_F32)
        for t in ts[1:]:
            acc = acc + t.astype(_F32)
        return (acc, acc) if narrow_copy else (acc,)

    outs = [(None, parts[0].shape[1], _F32)] + ([(None, parts[0].shape[1], jnp.bfloat16)] if narrow_copy else [])
    res = _rowmap(fn, [_t(a) for a in parts], [], outs, tm=min(PACK_ROW_TILE, parts[0].shape[0]), name=name)
    return res if narrow_copy else res[0]


def _adamw(w, g, m, v, *, name):
    def fn(w_, g_, m_, v_):
        m2 = ADAM_B1 * m_ + (1.0 - ADAM_B1) * g_
        v2 = ADAM_B2 * v_ + (1.0 - ADAM_B2) * (g_ * g_)
        m_hat = m2 / (1.0 - ADAM_B1 ** ADAM_STEP)
        v_hat = v2 / (1.0 - ADAM_B2 ** ADAM_STEP)
        return -ADAM_LR * (m_hat / (jnp.sqrt(v_hat) + ADAM_EPS) + ADAM_WD * w_), m2, v2

    return _rowmap(fn, [_t(w), _t(g), _t(m), _t(v)], [], [(None, w.shape[1], _F32)] * 3, tm=min(PACK_ROW_TILE, w.shape[0]), name=name)


def _step(x, mem, positions, loss_target, w, m, v):
    cx, cy, cc = lax.axis_index("x"), lax.axis_index("y"), lax.axis_index("c")
    chip = 2 * cx + cy
    big = [n for n, _ in _BIG]
    shard_shapes = [w[n].shape for n in big]
    total = sum(math.prod(s) for s in shard_shapes)
    rows = -(-total // (PACK_COLS * 2 * PACK_ROW_TILE)) * 2 * PACK_ROW_TILE
    half = rows // 2

    w_pack = _pack([w[n] for n in big], rows)
    gathered = _gather_over_chips(w_pack.astype(_BF), name="gather_weights")
    pieces = [_unpack(gathered[k], shard_shapes) for k in range(N_CHIPS)]
    full = {n: jnp.concatenate([pieces[k][i] for k in range(N_CHIPS)], axis=ax) for i, (n, ax) in enumerate(_BIG)}
    for n in _SMALL:
        full[n] = w[n]
    conv = w["a_conv"]
    slots = jnp.stack([jnp.where((chip == k) & (cc == 0), conv, 0.0) for k in range(N_CHIPS)])
    conv_all = _unpack(_all_reduce_small(_pack([slots], SMALL_ROWS), name="gather_conv"), [slots.shape])[0]
    conv_full = jnp.concatenate([conv_all[k] for k in range(N_CHIPS)], axis=2)

    p = _compute_form(full, conv_full)
    cos_t, sin_t = _rope_tables(positions[0])
    loss_tile, d_x, g = _local_step(x[0], mem[0], cos_t, sin_t, loss_target[0], p)
    gn = _natural_grads(g)

    def shard_of(a, ax, k):
        size = a.shape[ax] // N_CHIPS
        return lax.slice_in_dim(a, k * size, (k + 1) * size, axis=ax)

    g_pack = jnp.stack([_pack([shard_of(gn[n], ax, k) for n, ax in _BIG], rows) for k in range(N_CHIPS)]).reshape(N_CHIPS, 2, half, PACK_COLS)
    keep = lax.dynamic_index_in_dim(g_pack, cc, axis=1, keepdims=False).reshape(N_CHIPS * half, PACK_COLS)
    send = lax.dynamic_index_in_dim(g_pack, 1 - cc, axis=1, keepdims=False).reshape(N_CHIPS * half, PACK_COLS)
    chip_sum, chip_sum_narrow = _add([keep, _pair_exchange(send, name="grad_pair_sum")], name="grad_add_pair", narrow_copy=True)
    chip_sum = chip_sum.reshape(N_CHIPS, half, PACK_COLS)
    from_chips = _scatter_over_chips(chip_sum_narrow.reshape(N_CHIPS, half, PACK_COLS), name="grad_scatter")
    own = lax.dynamic_index_in_dim(chip_sum, chip, axis=0, keepdims=False)
    mine = _add([own, from_chips[0], from_chips[1], from_chips[2]], name="grad_add_chips")
    theirs = _pair_exchange(mine, name="grad_pair_gather")
    g_big = jnp.concatenate([jnp.where(cc == 0, mine, theirs), jnp.where(cc == 0, theirs, mine)], axis=0)

    d_big, m_big, v_big = _adamw(w_pack, g_big, _pack([m[n] for n in big], rows), _pack([v[n] for n in big], rows), name="adamw_big")

    small_shapes = [w[n].shape for n in _SMALL]
    g_small = _all_reduce_small(_pack([gn[n] for n in _SMALL], SMALL_ROWS), name="grad_small")
    d_small, m_small, v_small = _adamw(_pack([w[n] for n in _SMALL], SMALL_ROWS), g_small, _pack([m[n] for n in _SMALL], SMALL_ROWS),
                                       _pack([v[n] for n in _SMALL], SMALL_ROWS), name="adamw_small")

    def by_name(big_pack, small_pack):
        out = dict(zip(big, _unpack(big_pack, shard_shapes)))
        out.update(zip(_SMALL, _unpack(small_pack, small_shapes)))
        return out

    grads, deltas, new_m, new_v = by_name(g_big, g_small), by_name(d_big, d_small), by_name(m_big, m_small), by_name(v_big, v_small)
    loss = lax.psum(loss_tile[0, 0], ("x", "y", "c"))
    return (loss, d_x[None], *[grads[n] for n in _WEIGHTS], *[deltas[n] for n in _WEIGHTS], *[new_m[n] for n in _WEIGHTS],
            *[new_v[n] for n in _WEIGHTS])


def kernel(x, mem, positions, ffn1_norm, ffn1_w_gu, ffn1_w_down, mix_norm, ffn2_norm, ffn2_w_gu, ffn2_w_down, w_out, mem_norm, w_mem_kv, a_w_in, a_conv, a_A_log, a_dt_bias, a_out_norm, b_w_in, b_q_norm, b_w_uq, kv_in_norm, w_dkv, kv_lat_norm, w_ukv, final_norm, loss_target, m_ffn1_norm, m_ffn1_w_gu, m_ffn1_w_down, m_mix_norm, m_ffn2_norm, m_ffn2_w_gu, m_ffn2_w_down, m_w_out, m_mem_norm, m_w_mem_kv, m_a_w_in, m_a_conv, m_a_A_log, m_a_dt_bias, m_a_out_norm, m_b_w_in, m_b_q_norm, m_b_w_uq, m_kv_in_norm, m_w_dkv, m_kv_lat_norm, m_w_ukv, m_final_norm, v_ffn1_norm, v_ffn1_w_gu, v_ffn1_w_down, v_mix_norm, v_ffn2_norm, v_ffn2_w_gu, v_ffn2_w_down, v_w_out, v_mem_norm, v_w_mem_kv, v_a_w_in, v_a_conv, v_a_A_log, v_a_dt_bias, v_a_out_norm, v_b_w_in, v_b_q_norm, v_b_w_uq, v_kv_in_norm, v_w_dkv, v_kv_lat_norm, v_w_ukv, v_final_norm):
    given = dict(locals())
    w = {n: given[n] for n in _WEIGHTS}
    m = {n: given["m_" + n] for n in _WEIGHTS}
    v = {n: given["v_" + n] for n in _WEIGHTS}
    return _step(x, mem, positions, loss_target, w, m, v)
```

```python
import functools
import math

import jax
import jax.numpy as jnp
from jax import lax
from jax.experimental import pallas as pl
from jax.experimental.pallas import tpu as pltpu

_BF = jnp.bfloat16
_F32 = jnp.float32
_HI = lax.Precision.HIGHEST
_MESH = pl.DeviceIdType.MESH

D_MODEL = 1024
DEPTH = 4
N_A = 2
N_B = 2
CHUNK = 64
EPS = 1e-6
HEADS = 6
HEAD_DIM = 128
A_WIDTH = HEADS * HEAD_DIM
CONV_K = 4
QK_ROPE = 64
Q_LORA = 256
KV_LORA = 256
N_MEM = 256
MEM_HEADS = 4
MEM_HEAD_DIM = 64
MEM_WIDTH = MEM_HEADS * MEM_HEAD_DIM
D_FF = 2816
ROPE_THETA = 10000.0
A_IN = 4 * A_WIDTH + 2 * HEADS + MEM_WIDTH
A_IN_PAD = 3456
UQ_PAD = 2 * A_WIDTH
DKV_PAD = KV_LORA + 128
LANE = 128
SUBLANE = 8

ADAM_LR = 0.001
ADAM_B1 = 0.9
ADAM_B2 = 0.999
ADAM_EPS = 1e-08
ADAM_WD = 0.01
ADAM_STEP = 10

N_CHIPS = 4
N_DEV = 8
PACK_COLS = 1024
PACK_ROW_TILE = 256


def _pcall(body, **kw):
    return pl.pallas_call(body, **kw)


VMEM_LIMIT_V7X = 48 * 2 ** 20
TILE_BYTES = 6 * 2 ** 20


def _cp(grid_rank):
    return pltpu.CompilerParams(dimension_semantics=("arbitrary",) * grid_rank, vmem_limit_bytes=VMEM_LIMIT_V7X)


def _fit_rows(rows, row_bytes):
    while rows > LANE and rows * row_bytes > TILE_BYTES:
        rows //= 2
    return rows


def _fit_cols(n, target, col_bytes):
    return _tile(n, max(LANE, min(target, TILE_BYTES // col_bytes)))


def _tile(n, target):
    best = None
    for t in range(LANE, min(n, target) + 1, LANE):
        if n % t == 0:
            best = t
    return best if best is not None else n


def _dot(a, b):
    return jnp.dot(a.astype(_BF), b.astype(_BF), preferred_element_type=_F32)


def _dot_nt(a, b):
    return lax.dot_general(a.astype(_BF), b.astype(_BF), (((1,), (1,)), ((), ())), preferred_element_type=_F32)


def _dot_tn(a, b):
    return lax.dot_general(a.astype(_BF), b.astype(_BF), (((0,), (0,)), ((), ())), preferred_element_type=_F32)


def _dot_hi(a, b):
    return jnp.dot(a, b, precision=_HI, preferred_element_type=_F32)


def _rowmap(fn, tiles, params, outs, accs=(), *, tm, name, carry=(), reverse=False):
    rows = tiles[0][0].shape[0]
    steps = rows // tm
    nt, npar, no, na, nc = len(tiles), len(params), len(outs), len(accs), len(carry)

    def step_index(i):
        return steps - 1 - i if reverse else i

    in_specs, operands = [], []
    for arr, r, w, cb in tiles:
        r = tm if r is None else r
        w = arr.shape[1] if w is None else w
        assert arr.shape[0] == steps * r and (w % LANE == 0 or w == arr.shape[1]), (name, arr.shape, r, w)
        in_specs.append(pl.BlockSpec((r, w), functools.partial(lambda i, cb: (step_index(i), cb), cb=cb)))
        operands.append(arr)
    for p in params:
        in_specs.append(pl.BlockSpec(p.shape, functools.partial(lambda i, nd: (0,) * nd, nd=p.ndim)))
        operands.append(p)
    out_specs, out_shape = [], []
    for r, cols, dt in outs:
        r = tm if r is None else r
        out_specs.append(pl.BlockSpec((r, cols), lambda i: (step_index(i), 0)))
        out_shape.append(jax.ShapeDtypeStruct((steps * r, cols), dt))
    for shp, dt in accs:
        out_specs.append(pl.BlockSpec(shp, functools.partial(lambda i, nd: (0,) * nd, nd=len(shp))))
        out_shape.append(jax.ShapeDtypeStruct(shp, dt))

    def body(*refs):
        t_refs = refs[:nt]
        p_refs = refs[nt:nt + npar]
        o_refs = refs[nt + npar:nt + npar + no]
        a_refs = refs[nt + npar + no:nt + npar + no + na]
        c_refs = refs[nt + npar + no + na:]
        if na or nc:
            @pl.when(pl.program_id(0) == 0)
            def _():
                for r in a_refs + c_refs:
                    r[...] = jnp.zeros(r.shape, r.dtype)
        vals = fn(*[r[...] for r in t_refs], *[r[...] for r in p_refs], *[r[...] for r in c_refs])
        vals = tuple(vals) if isinstance(vals, (tuple, list)) else (vals,)
        assert len(vals) == no + na + nc, (name, len(vals), no, na, nc)
        for r, v in zip(o_refs, vals[:no]):
            r[...] = v.astype(r.dtype)
        for r, v in zip(a_refs, vals[no:no + na]):
            r[...] += v.astype(r.dtype)
        for r, v in zip(c_refs, vals[no + na:]):
            r[...] = v.astype(r.dtype)

    res = _pcall(
        body, name=name, grid=(steps,), in_specs=in_specs, out_specs=out_specs, out_shape=out_shape,
        scratch_shapes=[pltpu.VMEM(shp, dt) for shp, dt in carry],
        compiler_params=_cp(1),
    )(*operands)
    return res


def _t(arr, width=None, cb=0, rows=None):
    return (arr, rows, width, cb)


def _mm_nn(a, b, *, out_dtype, name, scale=None, res=None, tm=1024, tn=1536):
    m, k = a.shape
    n = b.shape[1]
    tm, tn = _fit_rows(min(tm, m), k * a.dtype.itemsize), _fit_cols(n, tn, k * b.dtype.itemsize)

    def body(a_ref, b_ref, *rest):
        acc = _dot(a_ref[...], b_ref[...])
        if scale is not None:
            acc = acc * scale
        if res is not None:
            acc = acc + rest[0][...]
        rest[-1][...] = acc.astype(rest[-1].dtype)

    in_specs = [pl.BlockSpec((tm, k), lambda i, j: (i, 0)), pl.BlockSpec((k, tn), lambda i, j: (0, j))]
    operands = [a, b]
    if res is not None:
        in_specs.append(pl.BlockSpec((tm, tn), lambda i, j: (i, j)))
        operands.append(res)
    return _pcall(
        body, name=name, grid=(m // tm, n // tn), in_specs=in_specs,
        out_specs=pl.BlockSpec((tm, tn), lambda i, j: (i, j)), out_shape=jax.ShapeDtypeStruct((m, n), out_dtype),
        compiler_params=_cp(2),
    )(*operands)


def _mm_nt(a, b, *, out_dtype, name, scale=None, tm=1024, tn=1536):
    m, k = a.shape
    n = b.shape[0]
    tm, tn = _fit_rows(min(tm, m), k * a.dtype.itemsize), _fit_cols(n, tn, k * b.dtype.itemsize)

    def body(a_ref, b_ref, o_ref):
        acc = _dot_nt(a_ref[...], b_ref[...])
        if scale is not None:
            acc = acc * scale
        o_ref[...] = acc.astype(o_ref.dtype)

    return _pcall(
        body, name=name, grid=(m // tm, n // tn),
        in_specs=[pl.BlockSpec((tm, k), lambda i, j: (i, 0)), pl.BlockSpec((tn, k), lambda i, j: (j, 0))],
        out_specs=pl.BlockSpec((tm, tn), lambda i, j: (i, j)), out_shape=jax.ShapeDtypeStruct((m, n), out_dtype),
        compiler_params=_cp(2),
    )(a, b)


def _mm_tn(a, b, *, name, scale=None, t1=1024, tn=1536, ts=1024):
    s, k1 = a.shape
    n = b.shape[1]
    t1, tn, ts = _tile(k1, t1), _tile(n, tn), min(ts, s)
    steps = s // ts

    def body(a_ref, b_ref, o_ref):
        @pl.when(pl.program_id(2) == 0)
        def _():
            o_ref[...] = jnp.zeros(o_ref.shape, o_ref.dtype)

        o_ref[...] += _dot_tn(a_ref[...], b_ref[...])
        if scale is not None:
            @pl.when(pl.program_id(2) == steps - 1)
            def _():
                o_ref[...] = o_ref[...] * scale

    return _pcall(
        body, name=name, grid=(k1 // t1, n // tn, steps),
        in_specs=[pl.BlockSpec((ts, t1), lambda i, j, r: (r, i)), pl.BlockSpec((ts, tn), lambda i, j, r: (r, j))],
        out_specs=pl.BlockSpec((t1, tn), lambda i, j, r: (i, j)), out_shape=jax.ShapeDtypeStruct((k1, n), _F32),
        compiler_params=_cp(3),
    )(a, b)


def _heads(t, n, w=LANE):
    return [t[:, w * h:w * (h + 1)] for h in range(n)]


def _cat(parts):
    return jnp.concatenate(parts, axis=1)


def _rms(x, g):
    return x * lax.rsqrt(jnp.mean(x * x, axis=-1, keepdims=True) + EPS) * g


def _l2n(x):
    return x * lax.rsqrt(jnp.sum(x * x, axis=-1, keepdims=True) + EPS)


def _sigmoid(x):
    return 0.5 * (jnp.tanh(0.5 * x) + 1.0)


def _silu(x):
    return x * _sigmoid(x)


def _softplus(x):
    return jnp.maximum(x, 0.0) + jnp.log(1.0 + jnp.exp(-jnp.abs(x)))


def _lane_pick(t, h):
    lane = lax.broadcasted_iota(jnp.int32, t.shape, 1)
    return jnp.sum(jnp.where(lane == h, t, 0.0), axis=1, keepdims=True)


def _lane_put(col, h, width=LANE):
    lane = lax.broadcasted_iota(jnp.int32, (col.shape[0], width), 1)
    return jnp.where(lane == h, col, 0.0)


def _vjp(fwd, ins, cts):
    outs, pull = jax.vjp(fwd, *ins)
    outs = outs if isinstance(outs, (tuple, list)) else (outs,)
    cts = tuple(c.astype(o.dtype) for c, o in zip(cts, outs))
    return pull(cts if len(cts) > 1 else cts[0])


def _rot_half_matrix():
    r = lax.broadcasted_iota(jnp.int32, (LANE, LANE), 0)
    c = lax.broadcasted_iota(jnp.int32, (LANE, LANE), 1)
    half = QK_ROPE // 2
    return jnp.where((c < half) & (r == c + half), -1.0, jnp.where((c >= half) & (c < QK_ROPE) & (r == c - half), 1.0, 0.0))


def _rope(x, cos_t, sin_t):
    return x * cos_t + _dot_hi(x, _rot_half_matrix()) * sin_t


def _mem_attn(q, km, vm):
    lane_q = lax.broadcasted_iota(jnp.int32, q.shape, 1)
    lane_v = lax.broadcasted_iota(jnp.int32, vm.shape, 1)
    out = jnp.zeros(q.shape, _F32)
    for h in range(MEM_HEADS):
        lo, hi = MEM_HEAD_DIM * h, MEM_HEAD_DIM * (h + 1)
        qh = jnp.where((lane_q >= lo) & (lane_q < hi), q, 0.0)
        vh = jnp.where((lane_v >= lo) & (lane_v < hi), vm, 0.0)
        sc = _dot_nt(qh, km) * MEM_HEAD_DIM ** -0.5
        sc = sc - lax.stop_gradient(jnp.max(sc, axis=-1, keepdims=True))
        p = jnp.exp(sc)
        p = p / jnp.sum(p, axis=-1, keepdims=True)
        out = out + _dot(p, vh)
    return out


_PAIR = 2 * CHUNK


def _pair_masks():
    ri = lax.broadcasted_iota(jnp.int32, (_PAIR, _PAIR), 0)
    ci = lax.broadcasted_iota(jnp.int32, (_PAIR, _PAIR), 1)
    same = (ri >= CHUNK) == (ci >= CHUNK)
    return same, same & (ri >= ci), same & (ri > ci), ri == ci, same & (ri <= ci)


_NN = (((2,), (1,)), ((0,), (0,)))
_NT = (((2,), (2,)), ((0,), (0,)))
_TN = (((1,), (1,)), ((0,), (0,)))


def _bdot(a, b, dims):
    return lax.dot_general(a.astype(_BF), b.astype(_BF), dims, preferred_element_type=_F32)


def _dot3(a, b, dims):
    a_hi, b_hi = a.astype(_BF), b.astype(_BF)
    a_lo, b_lo = (a - a_hi.astype(_F32)).astype(_BF), (b - b_hi.astype(_F32)).astype(_BF)
    d = lambda x, y: lax.dot_general(x, y, dims, preferred_element_type=_F32)
    return d(a_hi, b_hi) + (d(a_hi, b_lo) + d(a_lo, b_hi))


@jax.custom_vjp
def _mm3(a, b):
    return _dot3(a, b, _NN)


_mm3.defvjp(lambda a, b: (_dot3(a, b, _NN), (a, b)), lambda res, g: (_dot3(g, res[1], _NT), _dot3(res[0], g, _TN)))


def _neumann_inverse(a):
    eye = jnp.where(_pair_masks()[3], 1.0, 0.0)
    n = -a
    t_inv = eye + n
    for _ in range(5):
        n = _dot3(n, n, _NN)
        t_inv = t_inv + _dot3(t_inv, n, _NN)
    return t_inv


@jax.custom_vjp
def _unit_lower_inverse(a):
    return _neumann_inverse(a)


def _unit_lower_inverse_fwd(a):
    t_inv = _neumann_inverse(a)
    return t_inv, t_inv


def _unit_lower_inverse_bwd(t_inv, g):
    return (-_dot3(t_inv, _dot3(g, t_inv, _NT), _TN),)


_unit_lower_inverse.defvjp(_unit_lower_inverse_fwd, _unit_lower_inverse_bwd)


def _gdn_intra_head(q, k, v, beta, gl):
    same, causal, strict, eye, upper = _pair_masks()
    gl_row = jnp.sum(jnp.where(eye, gl, 0.0), axis=-2, keepdims=True)
    g_col = jnp.sum(jnp.where(causal, gl_row, 0.0), axis=-1, keepdims=True)
    g_row = jnp.sum(jnp.where(upper, gl, 0.0), axis=-2, keepdims=True)
    g_last = jnp.sum(jnp.where(same, gl_row, 0.0), axis=-1, keepdims=True)
    decay = jnp.where(causal, jnp.exp(jnp.where(causal, g_col - g_row, 0.0)), 0.0)
    kb = k * beta
    a = jnp.where(strict, _bdot(kb, k, _NT) * decay, 0.0)
    t_inv = _unit_lower_inverse(a)
    e_g = jnp.exp(g_col)
    u = _mm3(t_inv, v * beta)
    w = _mm3(t_inv, kb * e_g)
    qk = _bdot(q, k, _NT) * decay
    return w, u, q * e_g, k * jnp.exp(g_last - g_col), qk, jnp.exp(g_last)


def _gdn_scan_head(s, qd_a, kd_a, w_a, u_a, qk_a, dc_a, qd_b, kd_b, w_b, u_b, qk_b, dc_b):
    zeros = jnp.zeros((HEADS, CHUNK, HEAD_DIM), _F32)
    vn_a = u_a - _bdot(w_a, s, _NN)
    o_a = _bdot(qd_a, s, _NN) + _bdot(qk_a, jnp.concatenate([vn_a, zeros], axis=1), _NN)
    s1 = s * dc_a + _bdot(kd_a, vn_a, _TN)
    vn_b = u_b - _bdot(w_b, s1, _NN)
    o_b = _bdot(qd_b, s1, _NN) + _bdot(qk_b, jnp.concatenate([zeros, vn_b], axis=1), _NN)
    s2 = s1 * dc_b + _bdot(kd_b, vn_b, _TN)
    return o_a, o_b, s2


def _pick_scalar(t, row, lane_i):
    ri = lax.broadcasted_iota(jnp.int32, t.shape, 0)
    ci = lax.broadcasted_iota(jnp.int32, t.shape, 1)
    return jnp.sum(jnp.sum(jnp.where((ri == row) & (ci == lane_i), t, 0.0), axis=1, keepdims=True), axis=0, keepdims=True)


def _put_scalar(val, row, lane_i, shape):
    ri = lax.broadcasted_iota(jnp.int32, shape, 0)
    ci = lax.broadcasted_iota(jnp.int32, shape, 1)
    return jnp.where((ri == row) & (ci == lane_i), val, 0.0)


def _by_head(t):
    return jnp.stack(_heads(t, HEADS))


def _from_heads(t):
    return _cat([t[h] for h in range(HEADS)])


def _state_by_head(s):
    return jnp.stack([s[HEAD_DIM * h:HEAD_DIM * (h + 1), :] for h in range(HEADS)])


def _scan_ins(qd, kd, w, u, qk, dcrow, state):
    ins = [_state_by_head(state)]
    for r0 in (0, CHUNK):
        rs = slice(r0, r0 + CHUNK)
        ins += [_by_head(t[rs, :]) for t in (qd, kd, w, u, qk)]
        ins.append(jnp.stack([_pick_scalar(dcrow, r0, h) for h in range(HEADS)]))
    return ins


def _gdn_scan_fwd_fn(qd, kd, w, u, qk, dcrow, state):
    o_a, o_b, s2 = _gdn_scan_head(*_scan_ins(qd, kd, w, u, qk, dcrow, state))
    return jnp.concatenate([_from_heads(o_a), _from_heads(o_b)], axis=0), state, s2.reshape(state.shape)


def _gdn_scan_bwd_fn(qd, kd, w, u, qk, dcrow, state, d_o, d_state):
    cts = (_by_head(d_o[0:CHUNK, :]), _by_head(d_o[CHUNK:_PAIR, :]), _state_by_head(d_state))
    g = _vjp(_gdn_scan_head, _scan_ins(qd, kd, w, u, qk, dcrow, state), cts)
    grads = tuple(jnp.concatenate([_from_heads(g[1 + t]), _from_heads(g[7 + t])], axis=0) for t in range(5))
    d_dcrow = sum(_put_scalar(g[6][h], 0, h, dcrow.shape) + _put_scalar(g[12][h], CHUNK, h, dcrow.shape) for h in range(HEADS))
    return grads + (d_dcrow, g[0].reshape(state.shape))


def _gdn_intra_ins(q, k, v, bg):
    return [_by_head(q), _by_head(k), _by_head(v), jnp.stack([_lane_pick(bg, h) for h in range(HEADS)]),
            jnp.stack([_lane_pick(bg, HEADS + h) for h in range(HEADS)])]


def _gdn_intra_fwd_fn(q, k, v, bg):
    res = _gdn_intra_head(*_gdn_intra_ins(q, k, v, bg))
    dcrow = sum(_lane_put(res[5][h], h) for h in range(HEADS))
    return tuple(_from_heads(r) for r in res[:5]) + (dcrow,)


def _gdn_intra_bwd_fn(q, k, v, bg, d_w, d_u, d_qd, d_kd, d_qk, d_dcrow):
    cts = tuple(_by_head(d) for d in (d_w, d_u, d_qd, d_kd, d_qk)) + (jnp.stack([_lane_pick(d_dcrow, h) for h in range(HEADS)]),)
    g = _vjp(_gdn_intra_head, _gdn_intra_ins(q, k, v, bg), cts)
    d_bg = sum(_lane_put(g[3][h], h) + _lane_put(g[4][h], HEADS + h) for h in range(HEADS))
    return tuple(_from_heads(g[t]) for t in range(3)) + (d_bg,)


def _gdn_gates(ba, alog, dtb):
    lane = lax.broadcasted_iota(jnp.int32, ba.shape, 1)
    beta = _sigmoid(ba)
    g = -jnp.exp(alog) * _softplus(ba + dtb)
    return jnp.where(lane < HEADS, beta, jnp.where(lane < 2 * HEADS, g, 0.0))


def _gdn_q_head(c):
    return _l2n(_silu(c)) * HEAD_DIM ** -0.5


def _gdn_k_head(c):
    return _l2n(_silu(c))


def _gdn_prep_fwd_fn(qkv_c, ba, alog, dtb):
    hs = _heads(qkv_c, 3 * HEADS)
    q = _cat([_gdn_q_head(c) for c in hs[:HEADS]])
    k = _cat([_gdn_k_head(c) for c in hs[HEADS:2 * HEADS]])
    v = _cat([_silu(c) for c in hs[2 * HEADS:]])
    return q, k, v, _gdn_gates(ba, alog, dtb)


def _gdn_prep_bwd_fn(qkv_c, ba, d_q, d_k, d_v, d_bg, alog, dtb):
    hs = _heads(qkv_c, 3 * HEADS)
    dqs, dks, dvs = _heads(d_q, HEADS), _heads(d_k, HEADS), _heads(d_v, HEADS)
    parts = [_vjp(_gdn_q_head, [hs[h]], (dqs[h],))[0] for h in range(HEADS)]
    parts += [_vjp(_gdn_k_head, [hs[HEADS + h]], (dks[h],))[0] for h in range(HEADS)]
    parts += [_vjp(_silu, [hs[2 * HEADS + h]], (dvs[h],))[0] for h in range(HEADS)]
    d_ba, d_alog, d_dtb = _vjp(_gdn_gates, [ba, alog, dtb], (d_bg,))
    return _cat(parts), d_ba, d_alog, d_dtb


def _a_out_head(o, gate, gain):
    return _rms(o, gain) * _silu(gate)


def _a_post_fwd_fn(o, gate, qm, gain, mem_kv):
    parts = [_a_out_head(oh, gh, gain) for oh, gh in zip(_heads(o, HEADS), _heads(gate, HEADS))]
    parts.append(_mem_attn(qm, mem_kv[:, :MEM_WIDTH], mem_kv[:, MEM_WIDTH:]))
    return (_cat(parts),)


def _a_post_bwd_fn(o, gate, qm, d_cat, gain, mem_kv):
    d_os, d_gates = [], []
    d_gain = jnp.zeros(gain.shape, _F32)
    dc = _heads(d_cat, HEADS + 2)
    for h, (oh, gh) in enumerate(zip(_heads(o, HEADS), _heads(gate, HEADS))):
        g = _vjp(_a_out_head, [oh, gh, gain], (dc[h],))
        d_os.append(g[0])
        d_gates.append(g[1])
        d_gain = d_gain + g[2]
    d_qm, d_km, d_vm = _vjp(_mem_attn, [qm, mem_kv[:, :MEM_WIDTH], mem_kv[:, MEM_WIDTH:]], (d_cat[:, A_WIDTH:],))
    return _cat(d_os), _cat(d_gates + [d_qm]), d_gain, _cat([d_km, d_vm])


def _b_post_fwd_fn(o, qm, mem_kv):
    return (_cat([o.astype(_F32), _mem_attn(qm, mem_kv[:, :MEM_WIDTH], mem_kv[:, MEM_WIDTH:])]),)


def _b_post_bwd_fn(qm, d_cat_m, mem_kv):
    d_qm, d_km, d_vm = _vjp(_mem_attn, [qm, mem_kv[:, :MEM_WIDTH], mem_kv[:, MEM_WIDTH:]], (d_cat_m,))
    return d_qm, _cat([d_km, d_vm])


ATTN_SCALE = (HEAD_DIM + QK_ROPE) ** -0.5


def _rope_q_fwd_fn(qf, cos_t, sin_t):
    hs = _heads(qf, 2 * HEADS)
    return (_cat(hs[:HEADS] + [_rope(x, cos_t, sin_t) for x in hs[HEADS:]]) * ATTN_SCALE,)


def _rope_q_bwd_fn(d_qn, d_qr, cos_t, sin_t):
    f = lambda x: _rope(x, cos_t, sin_t)
    return (_cat([d_qn] + [_vjp(f, [x], (x,))[0] for x in _heads(d_qr, HEADS)]) * ATTN_SCALE,)


def _kv_prep_fwd_fn(ckr, cos_t, sin_t, gain):
    return _rms(ckr[:, :KV_LORA], gain), _rope(ckr[:, KV_LORA:], cos_t, sin_t)


def _kv_prep_bwd_fn(ckr, d_ckv, d_kr, cos_t, sin_t, gain):
    d_lat, d_gain = _vjp(_rms, [ckr[:, :KV_LORA], gain], (d_ckv,))
    f = lambda x: _rope(x, cos_t, sin_t)
    d_rope = _vjp(f, [ckr[:, KV_LORA:]], (d_kr,))[0]
    return _cat([d_lat, d_rope]), d_gain


def _rms_fwd(x, gain, *, name, tm=1024, out_dtype=_BF):
    tm = min(tm, x.shape[0])
    return _rowmap(lambda x_, g_: (_rms(x_.astype(_F32), g_),), [_t(x)], [gain], [(None, x.shape[1], out_dtype)], tm=tm, name=name)[0]


def _rms_bwd(x, d_xn, d_res, gain, *, name, tm=512):
    tm = min(tm, x.shape[0])

    def fn(x_, dxn_, *rest):
        g_ = rest[-1]
        dx, dg = _vjp(_rms, [x_.astype(_F32), g_], (dxn_.astype(_F32),))
        if d_res is not None:
            dx = dx + rest[0]
        return dx, dg

    tiles = [_t(x), _t(d_xn)] + ([_t(d_res)] if d_res is not None else [])
    return _rowmap(fn, tiles, [gain], [(None, x.shape[1], _F32)], [(gain.shape, _F32)], tm=tm, name=name)


FFN_COL_TILE = 1408
FFN_ROW_TILE = 512
FFN_DXN_ROW_TILE = 256


def _ffn_gate_up(x, gain, w_gu, *, name):
    s = x.shape[0]
    tm, tf = min(FFN_ROW_TILE, s), FFN_COL_TILE
    nf = D_FF // tf

    def body(x_ref, gain_ref, wg_ref, wu_ref, xn_ref, g_ref, u_ref, a_ref):
        @pl.when(pl.program_id(1) == 0)
        def _():
            xn_ref[...] = _rms(x_ref[...], gain_ref[...]).astype(xn_ref.dtype)

        xn = xn_ref[...]
        g, u = _dot(xn, wg_ref[...]), _dot(xn, wu_ref[...])
        g_ref[...] = g.astype(g_ref.dtype)
        u_ref[...] = u.astype(u_ref.dtype)
        a_ref[...] = (_silu(g) * u).astype(a_ref.dtype)

    col = pl.BlockSpec((tm, tf), lambda i, j: (i, j))
    wide = jax.ShapeDtypeStruct((s, D_FF), _BF)
    return _pcall(
        body, name=name, grid=(s // tm, nf),
        in_specs=[pl.BlockSpec((tm, D_MODEL), lambda i, j: (i, 0)), pl.BlockSpec((1, D_MODEL), lambda i, j: (0, 0)),
                  pl.BlockSpec((D_MODEL, tf), lambda i, j: (0, j)), pl.BlockSpec((D_MODEL, tf), lambda i, j: (0, nf + j))],
        out_specs=[pl.BlockSpec((tm, D_MODEL), lambda i, j: (i, 0)), col, col, col],
        out_shape=[jax.ShapeDtypeStruct((s, D_MODEL), _BF), wide, wide, wide],
        compiler_params=_cp(2),
    )(x, gain, w_gu, w_gu)


def _ffn_fwd(x, gain, w_gu, w_down, *, name):
    xn, g, u, a = _ffn_gate_up(x, gain, w_gu, name=name + "_gu")
    y = _mm_nn(a, w_down, out_dtype=_F32, name=name + "_down", scale=0.5, res=x, tn=1024)
    return y, (x, xn, g, u, a)


def _ffn_d_gate_up(d_y, g, u, w_down, *, name):
    s = d_y.shape[0]
    tm, tf = min(FFN_ROW_TILE, s), FFN_COL_TILE

    def body(dy_ref, wd_ref, g_ref, u_ref, dg_ref, du_ref):
        da = _dot_nt(dy_ref[...], wd_ref[...]) * 0.5
        gg, uu = g_ref[...].astype(_F32), u_ref[...].astype(_F32)
        sg = _sigmoid(gg)
        dg_ref[...] = (da * uu * sg * (1.0 + gg * (1.0 - sg))).astype(dg_ref.dtype)
        du_ref[...] = (da * gg * sg).astype(du_ref.dtype)

    col = pl.BlockSpec((tm, tf), lambda i, j: (i, j))
    wide = jax.ShapeDtypeStruct((s, D_FF), _BF)
    return _pcall(
        body, name=name, grid=(s // tm, D_FF // tf),
        in_specs=[pl.BlockSpec((tm, D_MODEL), lambda i, j: (i, 0)), pl.BlockSpec((tf, D_MODEL), lambda i, j: (j, 0)), col, col],
        out_specs=[col, col], out_shape=[wide, wide], compiler_params=_cp(2),
    )(d_y, w_down, g, u)


def _ffn_d_x(d_g, d_u, w_gu, x, d_y, gain, *, name):
    s = x.shape[0]
    tm = min(FFN_DXN_ROW_TILE, s)

    def body(dg_ref, du_ref, wg_ref, wu_ref, x_ref, dy_ref, gain_ref, dx_ref, dgain_ref):
        @pl.when(pl.program_id(0) == 0)
        def _():
            dgain_ref[...] = jnp.zeros(dgain_ref.shape, _F32)

        d_xn = _dot_nt(dg_ref[...], wg_ref[...]) + _dot_nt(du_ref[...], wu_ref[...])
        dx, dgain = _vjp(_rms, [x_ref[...], gain_ref[...]], (d_xn,))
        dx_ref[...] = dx + dy_ref[...]
        dgain_ref[...] += dgain

    wide = pl.BlockSpec((tm, D_FF), lambda i: (i, 0))
    rows = pl.BlockSpec((tm, D_MODEL), lambda i: (i, 0))
    one = pl.BlockSpec((1, D_MODEL), lambda i: (0, 0))
    return _pcall(
        body, name=name, grid=(s // tm,),
        in_specs=[wide, wide, pl.BlockSpec((D_MODEL, D_FF), lambda i: (0, 0)), pl.BlockSpec((D_MODEL, D_FF), lambda i: (0, 1)), rows, rows, one],
        out_specs=[rows, one], out_shape=[jax.ShapeDtypeStruct((s, D_MODEL), _F32), jax.ShapeDtypeStruct((1, D_MODEL), _F32)],
        compiler_params=_cp(1),
    )(d_g, d_u, w_gu, w_gu, x, d_y, gain)


def _ffn_bwd(d_y, saved, gain, w_gu, w_down, *, name):
    x, xn, g, u, a = saved
    d_g, d_u = _ffn_d_gate_up(d_y, g, u, w_down, name=name + "_dgu")
    d_w_down = _mm_tn(a, d_y, name=name + "_dwd", scale=0.5, t1=1408, tn=1024)
    d_w_gu = jnp.concatenate([_mm_tn(xn, d_g, name=name + "_dwg", t1=1024, tn=1408), _mm_tn(xn, d_u, name=name + "_dwu", t1=1024, tn=1408)], axis=1)
    d_x, d_gain = _ffn_d_x(d_g, d_u, w_gu, x, d_y, gain, name=name + "_dx")
    return d_x, d_gain, d_w_gu, d_w_down


def _conv_fwd(h, w, *, name, tm=256):
    s = h.shape[0]
    tm = min(tm, s)
    c = 3 * A_WIDTH
    halo = SUBLANE

    def body(x_ref, prev_ref, w_ref, o_ref, buf):
        i = pl.program_id(0)
        buf[0:halo, :] = jnp.where(i == 0, 0.0, prev_ref[...])
        buf[halo:halo + tm, :] = x_ref[...]
        acc = jnp.zeros((tm, c), _F32)
        for j in range(CONV_K):
            acc = acc + buf[pl.ds(halo - (CONV_K - 1) + j, tm), :] * w_ref[j:j + 1, :]
        o_ref[...] = acc

    return _pcall(
        body, name=name, grid=(s // tm,),
        in_specs=[pl.BlockSpec((tm, c), lambda i: (i, 0)),
                  pl.BlockSpec((halo, c), lambda i: (jnp.maximum(i * (tm // halo) - 1, 0), 0)),
                  pl.BlockSpec(w.shape, lambda i: (0, 0))],
        out_specs=pl.BlockSpec((tm, c), lambda i: (i, 0)), out_shape=jax.ShapeDtypeStruct((s, c), _F32),
        scratch_shapes=[pltpu.VMEM((tm + 2 * halo, c), _F32)],
        compiler_params=_cp(1),
    )(h, h, w)


def _conv_bwd(h, d_y, w, *, name, tm=256):
    s = h.shape[0]
    tm = min(tm, s)
    c = 3 * A_WIDTH
    halo = SUBLANE
    steps = s // tm

    def body(x_ref, prev_ref, dy_ref, next_ref, w_ref, dx_ref, dw_ref, xbuf, dybuf):
        i = pl.program_id(0)

        @pl.when(i == 0)
        def _():
            dw_ref[...] = jnp.zeros(dw_ref.shape, dw_ref.dtype)

        xbuf[0:halo, :] = jnp.where(i == 0, 0.0, prev_ref[...])
        xbuf[halo:halo + tm, :] = x_ref[...]
        dybuf[0:tm, :] = dy_ref[...]
        dybuf[tm:tm + halo, :] = jnp.where(i == steps - 1, 0.0, next_ref[...])
        dy = dy_ref[...]
        acc = jnp.zeros((tm, c), _F32)
        for j in range(CONV_K):
            acc = acc + dybuf[pl.ds(CONV_K - 1 - j, tm), :] * w_ref[j:j + 1, :]
            dw_ref[j:j + 1, :] += jnp.sum(dy * xbuf[pl.ds(halo - (CONV_K - 1) + j, tm), :], axis=0, keepdims=True)
        dx_ref[...] = acc

    return _pcall(
        body, name=name, grid=(steps,),
        in_specs=[pl.BlockSpec((tm, c), lambda i: (i, 0)),
                  pl.BlockSpec((halo, c), lambda i: (jnp.maximum(i * (tm // halo) - 1, 0), 0)),
                  pl.BlockSpec((tm, c), lambda i: (i, 0)),
                  pl.BlockSpec((halo, c), lambda i: (jnp.minimum((i + 1) * (tm // halo), s // halo - 1), 0)),
                  pl.BlockSpec(w.shape, lambda i: (0, 0))],
        out_specs=[pl.BlockSpec((tm, c), lambda i: (i, 0)), pl.BlockSpec(w.shape, lambda i: (0, 0))],
        out_shape=[jax.ShapeDtypeStruct((s, c), _F32), jax.ShapeDtypeStruct(w.shape, _F32)],
        scratch_shapes=[pltpu.VMEM((tm + 2 * halo, c), _F32), pltpu.VMEM((tm + 2 * halo, c), _F32)],
        compiler_params=_cp(1),
    )(h, h, d_y, d_y, w)


ATTN_Q_BLOCK = 2048
ATTN_K_SUB = 256
ATTN_BWD_BLOCK = 1024
ATTN_BWD_SUB = 512


def _chunk_mask(shape, q_axis):
    qi = lax.broadcasted_iota(jnp.int32, shape, q_axis) // CHUNK
    ki = lax.broadcasted_iota(jnp.int32, shape, 1 - q_axis) // CHUNK
    return ki <= qi


def _rows(j, t):
    return pl.ds(pl.multiple_of(j * t, t), t)


def _chunk_mask_at(shape, q_axis, q_off):
    qi = (lax.broadcasted_iota(jnp.int32, shape, q_axis) + q_off) // CHUNK
    ki = lax.broadcasted_iota(jnp.int32, shape, 1 - q_axis) // CHUNK
    return ki <= qi


def _attn_fwd(q_all, kv, kr, *, name):
    s = q_all.shape[0]
    t = min(ATTN_Q_BLOCK, s)
    tk = min(ATTN_K_SUB, t)
    nq, sub, rep = s // t, t // tk, tk // LANE

    def body(qn_ref, qr_ref, kn_ref, kr_ref, v_ref, o_ref, lse_ref, m_sc, acc_sc):
        i = pl.program_id(1)
        m_sc[...] = jnp.full(m_sc.shape, -1e30, _F32)
        acc_sc[...] = jnp.zeros(acc_sc.shape, _F32)
        ones = jnp.ones((tk, LANE), _BF)

        def block(j, first_row):
            qs = slice(first_row, t)
            rows = _rows(j, tk)
            q = _cat([qn_ref[qs, :], qr_ref[qs, :]])
            sc = _dot_nt(q, _cat([kn_ref[rows, :], kr_ref[rows, :]]))
            if first_row is not None:
                sc = jnp.where(_chunk_mask(sc.shape, 0), sc, -1e30)
            m_prev = m_sc[qs, :]
            m_new = jnp.maximum(m_prev, jnp.max(sc, axis=-1, keepdims=True))
            alpha = jnp.exp(m_prev - m_new)
            p = jnp.exp(sc - _cat([m_new] * rep))
            acc_sc[qs, :] = _cat([alpha, alpha]) * acc_sc[qs, :] + _dot(p, _cat([v_ref[rows, :], ones]))
            m_sc[qs, :] = m_new

        def step(j, carry):
            block(j, None)
            return carry

        lax.fori_loop(0, i * sub, step, 0)
        for u in range(sub):
            block(i * sub + u, u * tk)
        row_sum = acc_sc[:, LANE:2 * LANE]
        o_ref[...] = acc_sc[:, 0:LANE] / row_sum
        lse_ref[...] = m_sc[...] + jnp.log(row_sum)

    return _pcall(
        body, name=name, grid=(HEADS, nq),
        in_specs=[pl.BlockSpec((t, LANE), lambda h, i: (i, h)),
                  pl.BlockSpec((t, LANE), lambda h, i: (i, HEADS + h)),
                  pl.BlockSpec((s, LANE), lambda h, i: (0, h)),
                  pl.BlockSpec((s, LANE), lambda h, i: (0, 0)),
                  pl.BlockSpec((s, LANE), lambda h, i: (0, HEADS + h))],
        out_specs=[pl.BlockSpec((t, LANE), lambda h, i: (i, h)), pl.BlockSpec((t, LANE), lambda h, i: (i, h))],
        out_shape=[jax.ShapeDtypeStruct((s, A_WIDTH), _F32), jax.ShapeDtypeStruct((s, A_WIDTH), _F32)],
        scratch_shapes=[pltpu.VMEM((t, LANE), _F32), pltpu.VMEM((t, 2 * LANE), _F32)],
        compiler_params=_cp(2),
    )(q_all, q_all, kv, kr, kv)


def _attn_bwd_prep(o, lse, d_cat, *, name):
    s = o.shape[0]
    t = min(ATTN_BWD_SUB, s)
    nq = s // t

    def body(o_ref, lse_ref, do_ref, dob_ref, dl_ref, lset_ref, dlt_ref):
        for h in range(HEADS):
            sl = slice(LANE * h, LANE * (h + 1))
            rows = slice(SUBLANE * h, SUBLANE * (h + 1))
            do = do_ref[:, sl]
            dl = jnp.broadcast_to(jnp.sum(o_ref[:, sl] * do, axis=-1, keepdims=True), (t, LANE))
            dob_ref[:, sl] = do.astype(dob_ref.dtype)
            dl_ref[:, sl] = dl
            dlt_ref[rows, :] = dl.T[0:SUBLANE, :]
            lset_ref[rows, :] = lse_ref[:, sl].T[0:SUBLANE, :]

    wide = pl.BlockSpec((t, A_WIDTH), lambda i: (i, 0))
    stat = pl.BlockSpec((HEADS * SUBLANE, t), lambda i: (i, 0))
    stat_shape = jax.ShapeDtypeStruct((nq * HEADS * SUBLANE, t), _F32)
    return _pcall(
        body, name=name, grid=(nq,), in_specs=[wide, wide, wide], out_specs=[wide, wide, stat, stat],
        out_shape=[jax.ShapeDtypeStruct((s, A_WIDTH), _BF), jax.ShapeDtypeStruct((s, A_WIDTH), _F32), stat_shape, stat_shape],
        compiler_params=_cp(1),
    )(o, lse, d_cat)


def _attn_dq(q_all, kv, kr, lse, delta, d_o, *, name):
    s = q_all.shape[0]
    t = min(ATTN_BWD_BLOCK, s)
    tk = min(ATTN_BWD_SUB, t)
    nq, sub, rep = s // t, t // tk, tk // LANE

    def body(qn_ref, qr_ref, kn_ref, kr_ref, v_ref, lse_ref, dl_ref, do_ref, dqn_ref, dqr_ref, acc_sc):
        i = pl.program_id(1)
        q = _cat([qn_ref[...], qr_ref[...]])
        do = do_ref[...]
        lse = _cat([lse_ref[...]] * rep)
        dl = _cat([dl_ref[...]] * rep)
        acc_sc[...] = jnp.zeros(acc_sc.shape, _F32)

        def block(j, key_off):
            rows = _rows(j, tk)
            k = _cat([kn_ref[rows, :], kr_ref[rows, :]])
            p = jnp.exp(_dot_nt(q, k) - lse)
            if key_off is not None:
                p = jnp.where(_chunk_mask_at(p.shape, 0, -key_off), p, 0.0)
            ds = p * (_dot_nt(do, v_ref[rows, :]) - dl)
            acc_sc[...] += _dot(ds, k)

        def step(j, carry):
            block(j, None)
            return carry

        lax.fori_loop(0, i * sub, step, 0)
        for u in range(sub):
            block(i * sub + u, u * tk)
        dqn_ref[...] = acc_sc[:, 0:LANE]
        dqr_ref[...] = acc_sc[:, LANE:2 * LANE]

    return _pcall(
        body, name=name, grid=(HEADS, nq),
        in_specs=[pl.BlockSpec((t, LANE), lambda h, i: (i, h)),
                  pl.BlockSpec((t, LANE), lambda h, i: (i, HEADS + h)),
                  pl.BlockSpec((s, LANE), lambda h, i: (0, h)),
                  pl.BlockSpec((s, LANE), lambda h, i: (0, 0)),
                  pl.BlockSpec((s, LANE), lambda h, i: (0, HEADS + h)),
                  pl.BlockSpec((t, LANE), lambda h, i: (i, h)),
                  pl.BlockSpec((t, LANE), lambda h, i: (i, h)),
                  pl.BlockSpec((t, LANE), lambda h, i: (i, h))],
        out_specs=[pl.BlockSpec((t, LANE), lambda h, i: (i, h)), pl.BlockSpec((t, LANE), lambda h, i: (i, h))],
        out_shape=[jax.ShapeDtypeStruct((s, A_WIDTH), _F32), jax.ShapeDtypeStruct((s, A_WIDTH), _F32)],
        scratch_shapes=[pltpu.VMEM((t, 2 * LANE), _F32)],
        compiler_params=_cp(2),
    )(q_all, q_all, kv, kr, kv, lse, delta, d_o)


def _attn_dkv(q_all, kv, kr, lse_t, delta_t, d_o, *, name):
    s = q_all.shape[0]
    t = min(ATTN_BWD_BLOCK, s)
    tq = min(ATTN_BWD_SUB, t)
    nk, sub, nqs = s // t, t // tq, s // tq

    def body(kn_ref, kr_ref, v_ref, qn_ref, qr_ref, do_ref, lset_ref, dlt_ref, dkn_ref, dv_ref, dkr_ref, dk_sc, dv_sc):
        h, j = pl.program_id(0), pl.program_id(1)
        k = _cat([kn_ref[...], kr_ref[...]])
        v = v_ref[...]
        dk_sc[...] = jnp.zeros(dk_sc.shape, _F32)
        dv_sc[...] = jnp.zeros(dv_sc.shape, _F32)

        def block(i, query_off):
            rows = _rows(i, tq)
            stat = pl.ds(pl.multiple_of((i * HEADS + h) * SUBLANE, SUBLANE), 1)
            q = _cat([qn_ref[rows, :], qr_ref[rows, :]])
            do = do_ref[rows, :]
            p = jnp.exp(_dot_nt(k, q) - lset_ref[stat, :])
            if query_off is not None:
                p = jnp.where(_chunk_mask_at(p.shape, 1, query_off), p, 0.0)
            dv_sc[...] += _dot(p, do)
            ds = p * (_dot_nt(v, do) - dlt_ref[stat, :])
            dk_sc[...] += _dot(ds, q)

        def step(i, carry):
            block(i, None)
            return carry

        for u in range(sub):
            block(j * sub + u, u * tq)
        lax.fori_loop((j + 1) * sub, nqs, step, 0)
        dkn_ref[...] = dk_sc[:, 0:LANE]
        dkr_ref[...] = dk_sc[:, LANE:2 * LANE]
        dv_ref[...] = dv_sc[...]

    stats = pl.BlockSpec((nqs * HEADS * SUBLANE, tq), lambda h, j: (0, 0))
    return _pcall(
        body, name=name, grid=(HEADS, nk),
        in_specs=[pl.BlockSpec((t, LANE), lambda h, j: (j, h)),
                  pl.BlockSpec((t, LANE), lambda h, j: (j, 0)),
                  pl.BlockSpec((t, LANE), lambda h, j: (j, HEADS + h)),
                  pl.BlockSpec((s, LANE), lambda h, j: (0, h)),
                  pl.BlockSpec((s, LANE), lambda h, j: (0, HEADS + h)),
                  pl.BlockSpec((s, LANE), lambda h, j: (0, h)),
                  stats, stats],
        out_specs=[pl.BlockSpec((t, LANE), lambda h, j: (j, h))] * 3,
        out_shape=[jax.ShapeDtypeStruct((s, A_WIDTH), _F32)] * 3,
        scratch_shapes=[pltpu.VMEM((t, 2 * LANE), _F32), pltpu.VMEM((t, LANE), _F32)],
        compiler_params=_cp(2),
    )(kv, kr, kv, q_all, q_all, d_o, lse_t, delta_t)


def _final_loss(x, tgt, gain, *, name, tm=512):
    tm = min(tm, x.shape[0])

    def fn(x_, t_, g_):
        def f(xx, gg):
            err = _rms(xx, gg) - t_
            return 0.5 * jnp.sum(jnp.sum(err * err, axis=1, keepdims=True) / D_MODEL, axis=0, keepdims=True)

        loss, pull = jax.vjp(f, x_, g_)
        dx, dg = pull(jnp.ones((1, 1), _F32))
        return dx, dg, jnp.broadcast_to(loss, (SUBLANE, LANE))

    return _rowmap(fn, [_t(x), _t(tgt)], [gain], [(None, D_MODEL, _F32)], [(gain.shape, _F32), ((SUBLANE, LANE), _F32)], tm=tm, name=name)


def _local_step(x, mem, cos_t, sin_t, tgt, p):
    s = x.shape[0]
    g = {}
    row = lambda a: a.reshape(1, -1)
    tm_e = min(256, s)

    mem_n = _rms_fwd(mem, row(p["mem_norm"]), name="mem_norm")
    mem_kv_all = _mm_nn(mem_n, p["w_mem_all"], out_dtype=_F32, name="mem_kv", tn=1024)
    mem_kv = [mem_kv_all[:, 2 * MEM_WIDTH * l:2 * MEM_WIDTH * (l + 1)] for l in range(DEPTH)]

    sv = []
    for i in range(N_A):
        l = i
        r = {}
        r["x0"] = x
        x, r["ffn1"] = _ffn_fwd(x, row(p["ffn1_norm"][l]), p["ffn1_w_gu"][l], p["ffn1_w_down"][l], name=f"a{i}_ffn1")
        r["x1"] = x
        xn = _rms_fwd(x, row(p["mix_norm"][l]), name=f"a{i}_mixnorm")
        h = _mm_nn(xn, p["a_w_in"][i], out_dtype=_F32, name=f"a{i}_in", tn=1152)
        qkv_c = _conv_fwd(h, p["a_conv"][i], name=f"a{i}_conv")
        alog, dtb = p["a_A_log_row"][i], p["a_dt_bias_row"][i]
        q, k, v, bg = _rowmap(_gdn_prep_fwd_fn, [_t(qkv_c), _t(h, LANE, 26)], [alog, dtb],
                              [(None, A_WIDTH, _F32)] * 3 + [(None, LANE, _F32)], tm=tm_e, name=f"a{i}_prep")
        w_, u_, qd, kd, qk, dcrow = _rowmap(_gdn_intra_fwd_fn, [_t(q), _t(k), _t(v), _t(bg)], [],
                                            [(None, A_WIDTH, _F32)] * 5 + [(None, LANE, _F32)], tm=_PAIR, name=f"a{i}_intra")
        o, states = _rowmap(_gdn_scan_fwd_fn, [_t(qd), _t(kd), _t(w_), _t(u_), _t(qk), _t(dcrow)], [],
                            [(None, A_WIDTH, _F32), (A_WIDTH, HEAD_DIM, _F32)], tm=_PAIR, name=f"a{i}_scan",
                            carry=[((A_WIDTH, HEAD_DIM), _F32)])
        gain_o = row(p["a_out_norm"][i])
        cat = _rowmap(_a_post_fwd_fn, [_t(o), _t(h, A_WIDTH, 3), _t(h, MEM_WIDTH, 12)], [gain_o, mem_kv[l]],
                      [(None, D_MODEL, _BF)], tm=tm_e, name=f"a{i}_post")[0]
        x = _mm_nn(cat, p["w_out"][l], out_dtype=_F32, name=f"a{i}_out", res=x, tn=1024)
        r.update(xn=xn, h=h, qkv_c=qkv_c, q=q, k=k, v=v, bg=bg, w=w_, u=u_, qd=qd, kd=kd, qk=qk, dcrow=dcrow, o=o, states=states, cat=cat)
        r["x2"] = x
        x, r["ffn2"] = _ffn_fwd(x, row(p["ffn2_norm"][l]), p["ffn2_w_gu"][l], p["ffn2_w_down"][l], name=f"a{i}_ffn2")
        sv.append(r)

    kvs = {"x": x}
    xn_kv = _rms_fwd(x, row(p["kv_in_norm"]), name="kv_innorm")
    ckr = _mm_nn(xn_kv, p["w_dkv"], out_dtype=_F32, name="kv_down")
    ckv, k_rope = _rowmap(_kv_prep_fwd_fn, [_t(ckr), _t(cos_t), _t(sin_t)], [row(p["kv_lat_norm"])],
                          [(None, KV_LORA, _BF), (None, LANE, _BF)], tm=tm_e, name="kv_prep")
    kvu = _mm_nn(ckv, p["w_ukv"], out_dtype=_BF, name="kv_up")
    kvs.update(xn=xn_kv, ckr=ckr, ckv=ckv)

    for j in range(N_B):
        l = N_A + j
        r = {}
        x, r["ffn1"] = _ffn_fwd(x, row(p["ffn1_norm"][l]), p["ffn1_w_gu"][l], p["ffn1_w_down"][l], name=f"b{j}_ffn1")
        r["x1"] = x
        xn = _rms_fwd(x, row(p["mix_norm"][l]), name=f"b{j}_mixnorm")
        h = _mm_nn(xn, p["b_w_in"][j], out_dtype=_F32, name=f"b{j}_in")
        gain_q = row(p["b_q_norm"][j])
        cqn = _rowmap(lambda c_, g_: (_rms(c_, g_),), [_t(h, Q_LORA, 0)], [gain_q], [(None, Q_LORA, _BF)], tm=tm_e, name=f"b{j}_qnorm")[0]
        qf = _mm_nn(cqn, p["b_w_uq"][j], out_dtype=_F32, name=f"b{j}_uq")
        q_all = _rowmap(_rope_q_fwd_fn, [_t(qf), _t(cos_t), _t(sin_t)], [], [(None, UQ_PAD, _BF)], tm=tm_e, name=f"b{j}_rope")[0]
        o_b, lse = _attn_fwd(q_all, kvu, k_rope, name=f"b{j}_attn")
        cat = _rowmap(_b_post_fwd_fn, [_t(o_b), _t(h, MEM_WIDTH, 1)], [mem_kv[l]], [(None, D_MODEL, _BF)], tm=tm_e, name=f"b{j}_post")[0]
        x = _mm_nn(cat, p["w_out"][l], out_dtype=_F32, name=f"b{j}_out", res=x, tn=1024)
        r.update(xn=xn, h=h, cqn=cqn, q_all=q_all, o_b=o_b, lse=lse, cat=cat)
        x, r["ffn2"] = _ffn_fwd(x, row(p["ffn2_norm"][l]), p["ffn2_w_gu"][l], p["ffn2_w_down"][l], name=f"b{j}_ffn2")
        sv.append(r)

    dx, g["final_norm"], loss = _final_loss(x, tgt, row(p["final_norm"]), name="loss")

    per_layer = lambda: [None] * DEPTH
    for n in ("ffn1_norm", "ffn1_w_gu", "ffn1_w_down", "mix_norm", "ffn2_norm", "ffn2_w_gu", "ffn2_w_down", "w_out", "mem_kv"):
        g[n] = per_layer()
    for n in ("a_w_in", "a_conv", "a_A_log_row", "a_dt_bias_row", "a_out_norm", "b_w_in", "b_q_norm", "b_w_uq"):
        g[n] = [None] * N_A
    d_kv_parts = []

    for j in reversed(range(N_B)):
        l = N_A + j
        r = sv[l]
        dx, g["ffn2_norm"][l], g["ffn2_w_gu"][l], g["ffn2_w_down"][l] = _ffn_bwd(
            dx, r["ffn2"], row(p["ffn2_norm"][l]), p["ffn2_w_gu"][l], p["ffn2_w_down"][l], name=f"b{j}_ffn2b")
        d_cat = _mm_nt(dx, p["w_out"][l], out_dtype=_F32, name=f"b{j}_dcat", tn=1024)
        g["w_out"][l] = _mm_tn(r["cat"], dx, name=f"b{j}_dwout", tn=1024)
        d_qm, g["mem_kv"][l] = _rowmap(_b_post_bwd_fn, [_t(r["h"], MEM_WIDTH, 1), _t(d_cat, MEM_WIDTH, 3)], [mem_kv[l]],
                                      [(None, MEM_WIDTH, _F32)], [((N_MEM, 2 * MEM_WIDTH), _F32)], tm=tm_e, name=f"b{j}_postb")
        d_o, delta, lse_t, delta_t = _attn_bwd_prep(r["o_b"], r["lse"], d_cat, name=f"b{j}_delta")
        dqn, dqr = _attn_dq(r["q_all"], kvu, k_rope, r["lse"], delta, d_o, name=f"b{j}_attn_dq")
        d_kv_parts.append(_attn_dkv(r["q_all"], kvu, k_rope, lse_t, delta_t, d_o, name=f"b{j}_attn_dkv"))
        d_qf = _rowmap(_rope_q_bwd_fn, [_t(dqn), _t(dqr), _t(cos_t), _t(sin_t)], [], [(None, UQ_PAD, _F32)], tm=tm_e, name=f"b{j}_ropeb")[0]
        d_cqn = _mm_nt(d_qf, p["b_w_uq"][j], out_dtype=_F32, name=f"b{j}_dcqn")
        g["b_w_uq"][j] = _mm_tn(r["cqn"], d_qf, name=f"b{j}_dwuq")
        gain_q = row(p["b_q_norm"][j])
        d_cq, g["b_q_norm"][j] = _rowmap(lambda c_, d_, g_: _vjp(_rms, [c_, g_], (d_,)), [_t(r["h"], Q_LORA, 0), _t(d_cqn)], [gain_q],
                                        [(None, Q_LORA, _F32)], [((1, Q_LORA), _F32)], tm=tm_e, name=f"b{j}_qnormb")
        d_h = jnp.concatenate([d_cq, d_qm], axis=1)
        d_xn = _mm_nt(d_h, p["b_w_in"][j], out_dtype=_F32, name=f"b{j}_dxn", tn=1024)
        g["b_w_in"][j] = _mm_tn(r["xn"], d_h, name=f"b{j}_dwin")
        dx, g["mix_norm"][l] = _rms_bwd(r["x1"], d_xn, dx, row(p["mix_norm"][l]), name=f"b{j}_mixnormb")
        dx, g["ffn1_norm"][l], g["ffn1_w_gu"][l], g["ffn1_w_down"][l] = _ffn_bwd(
            dx, r["ffn1"], row(p["ffn1_norm"][l]), p["ffn1_w_gu"][l], p["ffn1_w_down"][l], name=f"b{j}_ffn1b")

    def kv_sum(*parts):
        dkn = sum(parts[0::3][1:], parts[0])
        dv = sum(parts[1::3][1:], parts[1])
        dkr = sum(parts[2::3][1:], parts[2])
        return _cat([dkn, dv]), sum(_heads(dkr, HEADS)[1:], _heads(dkr, HEADS)[0])

    d_kvu, d_kr = _rowmap(kv_sum, [_t(a) for part in d_kv_parts for a in part], [], [(None, 2 * A_WIDTH, _F32), (None, LANE, _F32)],
                          tm=tm_e, name="kv_dsum")
    d_ckv = _mm_nt(d_kvu, p["w_ukv"], out_dtype=_F32, name="kv_dckv")
    g["w_ukv"] = _mm_tn(kvs["ckv"], d_kvu, name="kv_dwukv")
    d_ckr, g["kv_lat_norm"] = _rowmap(_kv_prep_bwd_fn, [_t(kvs["ckr"]), _t(d_ckv), _t(d_kr), _t(cos_t), _t(sin_t)], [row(p["kv_lat_norm"])],
                                     [(None, DKV_PAD, _F32)], [((1, KV_LORA), _F32)], tm=tm_e, name="kv_prepb")
    d_xn = _mm_nt(d_ckr, p["w_dkv"], out_dtype=_F32, name="kv_dxn", tn=1024)
    g["w_dkv"] = _mm_tn(kvs["xn"], d_ckr, name="kv_dwdkv")
    dx, g["kv_in_norm"] = _rms_bwd(kvs["x"], d_xn, dx, row(p["kv_in_norm"]), name="kv_innormb")

    for i in reversed(range(N_A)):
        l = i
        r = sv[l]
        dx, g["ffn2_norm"][l], g["ffn2_w_gu"][l], g["ffn2_w_down"][l] = _ffn_bwd(
            dx, r["ffn2"], row(p["ffn2_norm"][l]), p["ffn2_w_gu"][l], p["ffn2_w_down"][l], name=f"a{i}_ffn2b")
        d_cat = _mm_nt(dx, p["w_out"][l], out_dtype=_F32, name=f"a{i}_dcat", tn=1024)
        g["w_out"][l] = _mm_tn(r["cat"], dx, name=f"a{i}_dwout", tn=1024)
        gain_o = row(p["a_out_norm"][i])
        h = r["h"]
        d_o, d_hpart, g["a_out_norm"][i], g["mem_kv"][l] = _rowmap(
            _a_post_bwd_fn, [_t(r["o"]), _t(h, A_WIDTH, 3), _t(h, MEM_WIDTH, 12), _t(d_cat)], [gain_o, mem_kv[l]],
            [(None, A_WIDTH, _F32), (None, D_MODEL, _F32)], [((1, HEAD_DIM), _F32), ((N_MEM, 2 * MEM_WIDTH), _F32)], tm=tm_e, name=f"a{i}_postb")
        d_qd, d_kd, d_w, d_u, d_qk, d_dcrow = _rowmap(
            _gdn_scan_bwd_fn, [_t(r["qd"]), _t(r["kd"]), _t(r["w"]), _t(r["u"]), _t(r["qk"]), _t(r["dcrow"]), _t(r["states"], rows=A_WIDTH), _t(d_o)], [],
            [(None, A_WIDTH, _F32)] * 5 + [(None, LANE, _F32)], tm=_PAIR, name=f"a{i}_scanb", carry=[((A_WIDTH, HEAD_DIM), _F32)], reverse=True)
        d_q, d_k, d_v, d_bg = _rowmap(
            _gdn_intra_bwd_fn, [_t(r["q"]), _t(r["k"]), _t(r["v"]), _t(r["bg"]), _t(d_w), _t(d_u), _t(d_qd), _t(d_kd), _t(d_qk), _t(d_dcrow)], [],
            [(None, A_WIDTH, _F32)] * 3 + [(None, LANE, _F32)], tm=_PAIR, name=f"a{i}_intrab")
        alog, dtb = p["a_A_log_row"][i], p["a_dt_bias_row"][i]
        d_qkv_c, d_ba, g["a_A_log_row"][i], g["a_dt_bias_row"][i] = _rowmap(
            _gdn_prep_bwd_fn, [_t(r["qkv_c"]), _t(h, LANE, 26), _t(d_q), _t(d_k), _t(d_v), _t(d_bg)], [alog, dtb],
            [(None, 3 * A_WIDTH, _F32), (None, LANE, _F32)], [((1, LANE), _F32), ((1, LANE), _F32)], tm=tm_e, name=f"a{i}_prepb")
        d_qkv, g["a_conv"][i] = _conv_bwd(h, d_qkv_c, p["a_conv"][i], name=f"a{i}_convb")
        d_h = jnp.concatenate([d_qkv, d_hpart, d_ba], axis=1)
        d_xn = _mm_nt(d_h, p["a_w_in"][i], out_dtype=_F32, name=f"a{i}_dxn", tn=1024)
        g["a_w_in"][i] = _mm_tn(r["xn"], d_h, name=f"a{i}_dwin", tn=1152)
        dx, g["mix_norm"][l] = _rms_bwd(r["x1"], d_xn, dx, row(p["mix_norm"][l]), name=f"a{i}_mixnormb")
        dx, g["ffn1_norm"][l], g["ffn1_w_gu"][l], g["ffn1_w_down"][l] = _ffn_bwd(
            dx, r["ffn1"], row(p["ffn1_norm"][l]), p["ffn1_w_gu"][l], p["ffn1_w_down"][l], name=f"a{i}_ffn1b")

    d_mem_kv_all = jnp.concatenate(g.pop("mem_kv"), axis=1)
    d_mem_n = _mm_nt(d_mem_kv_all, p["w_mem_all"], out_dtype=_F32, name="mem_dn", tn=1024)
    g["w_mem_all"] = _mm_tn(mem_n, d_mem_kv_all, name="mem_dw", tn=1024)
    _, g["mem_norm"] = _rms_bwd(mem, d_mem_n, None, row(p["mem_norm"]), name="mem_normb")
    return loss, dx, g


_NOPE_ROPE = HEAD_DIM + QK_ROPE
_QKV_GATE = 4 * A_WIDTH
_BETA_AT = _QKV_GATE + MEM_WIDTH


def _lane_row(vals, at):
    n = vals.shape[0]
    return jnp.concatenate([jnp.zeros((at,), _F32), vals.astype(_F32), jnp.zeros((LANE - at - n,), _F32)]).reshape(1, LANE)


def _compute_form(w, conv_f32):
    p = {n: w[n] for n in ("ffn1_norm", "ffn1_w_gu", "ffn1_w_down", "mix_norm", "ffn2_norm", "ffn2_w_gu", "ffn2_w_down", "w_out",
                           "mem_norm", "a_out_norm", "b_w_in", "b_q_norm", "kv_in_norm", "kv_lat_norm", "final_norm")}
    wm = w["w_mem_kv"]
    p["w_mem_all"] = jnp.transpose(wm, (1, 0, 2)).reshape(D_MODEL, DEPTH * 2 * MEM_WIDTH)
    a = w["a_w_in"]
    pad = jnp.zeros((N_A, D_MODEL, A_IN_PAD - A_IN), a.dtype)
    p["a_w_in"] = jnp.concatenate([a[:, :, :_QKV_GATE], a[:, :, _QKV_GATE + 2 * HEADS:], a[:, :, _QKV_GATE:_QKV_GATE + 2 * HEADS], pad], axis=2)
    p["a_conv"] = jnp.concatenate([conv_f32, jnp.zeros((N_A, SUBLANE - CONV_K, 3 * A_WIDTH), _F32)], axis=1)
    p["a_A_log_row"] = [_lane_row(w["a_A_log"][i], HEADS) for i in range(N_A)]
    p["a_dt_bias_row"] = [_lane_row(w["a_dt_bias"][i], HEADS) for i in range(N_A)]
    uq = w["b_w_uq"].reshape(N_B, Q_LORA, HEADS, _NOPE_ROPE)
    rope = jnp.concatenate([uq[..., HEAD_DIM:], jnp.zeros((N_B, Q_LORA, HEADS, LANE - QK_ROPE), uq.dtype)], axis=-1)
    p["b_w_uq"] = jnp.concatenate([uq[..., :HEAD_DIM].reshape(N_B, Q_LORA, A_WIDTH), rope.reshape(N_B, Q_LORA, A_WIDTH)], axis=-1)
    dkv = w["w_dkv"]
    p["w_dkv"] = jnp.concatenate([dkv, jnp.zeros((D_MODEL, DKV_PAD - dkv.shape[1]), dkv.dtype)], axis=1)
    ukv = w["w_ukv"].reshape(KV_LORA, HEADS, 2 * HEAD_DIM)
    p["w_ukv"] = jnp.concatenate([ukv[..., :HEAD_DIM].reshape(KV_LORA, A_WIDTH), ukv[..., HEAD_DIM:].reshape(KV_LORA, A_WIDTH)], axis=-1)
    return p


def _natural_grads(g):
    st = lambda xs: jnp.stack(xs, axis=0)
    n = {}
    for k in ("ffn1_norm", "mix_norm", "ffn2_norm"):
        n[k] = st(g[k]).reshape(DEPTH, D_MODEL)
    for k in ("ffn1_w_gu", "ffn1_w_down", "ffn2_w_gu", "ffn2_w_down", "w_out", "b_w_in"):
        n[k] = st(g[k])
    n["mem_norm"] = g["mem_norm"].reshape(D_MODEL)
    n["w_mem_kv"] = jnp.transpose(g["w_mem_all"].reshape(D_MODEL, DEPTH, 2 * MEM_WIDTH), (1, 0, 2))
    a = st(g["a_w_in"])
    n["a_w_in"] = jnp.concatenate([a[:, :, :_QKV_GATE], a[:, :, _BETA_AT:_BETA_AT + 2 * HEADS], a[:, :, _QKV_GATE:_BETA_AT]], axis=2)
    n["a_conv"] = st(g["a_conv"])[:, :CONV_K]
    n["a_A_log"] = st(g["a_A_log_row"])[:, 0, HEADS:2 * HEADS]
    n["a_dt_bias"] = st(g["a_dt_bias_row"])[:, 0, HEADS:2 * HEADS]
    n["a_out_norm"] = st(g["a_out_norm"]).reshape(N_A, HEAD_DIM)
    n["b_q_norm"] = st(g["b_q_norm"]).reshape(N_B, Q_LORA)
    uq = st(g["b_w_uq"])
    nope = uq[:, :, :A_WIDTH].reshape(N_B, Q_LORA, HEADS, HEAD_DIM)
    rope = uq[:, :, A_WIDTH:].reshape(N_B, Q_LORA, HEADS, LANE)[..., :QK_ROPE]
    n["b_w_uq"] = jnp.concatenate([nope, rope], axis=-1).reshape(N_B, Q_LORA, HEADS * _NOPE_ROPE)
    n["kv_in_norm"] = g["kv_in_norm"].reshape(D_MODEL)
    n["w_dkv"] = g["w_dkv"][:, :KV_LORA + QK_ROPE]
    n["kv_lat_norm"] = g["kv_lat_norm"].reshape(KV_LORA)
    ukv = g["w_ukv"]
    n["w_ukv"] = jnp.concatenate([ukv[:, :A_WIDTH].reshape(KV_LORA, HEADS, HEAD_DIM), ukv[:, A_WIDTH:].reshape(KV_LORA, HEADS, HEAD_DIM)],
                                 axis=-1).reshape(KV_LORA, HEADS * 2 * HEAD_DIM)
    n["final_norm"] = g["final_norm"].reshape(D_MODEL)
    return n


def _rope_tables(positions):
    inv = ROPE_THETA ** (-jnp.arange(0, QK_ROPE, 2, dtype=_F32) / QK_ROPE)
    ang = positions.astype(_F32)[:, None] * inv
    z = jnp.zeros((positions.shape[0], LANE - QK_ROPE), _F32)
    cos, sin = jnp.cos(ang), jnp.sin(ang)
    return jnp.concatenate([cos, cos, z], axis=1), jnp.concatenate([sin, sin, z], axis=1)


_HBM = pl.BlockSpec(memory_space=pltpu.HBM)


def _place():
    x, y, c = lax.axis_index("x"), lax.axis_index("y"), lax.axis_index("c")
    return x, y, c, [(1 - x, y), (x, 1 - y), (1 - x, 1 - y)]


def _remote(src, dst, send_sem, recv_sem, to):
    return pltpu.make_async_remote_copy(src_ref=src, dst_ref=dst, send_sem=send_sem, recv_sem=recv_sem, device_id=to, device_id_type=_MESH)


def _gather_over_chips(shard, *, name):
    rows, cols = shard.shape
    half = rows // 2

    def body(w_ref, out_ref, send_sems, recv_sems, local_sem):
        x, y, c, chips = _place()
        k = 2 * x + y

        def part(chip, h):
            return out_ref.at[chip, pl.ds(h * half, half), :]

        mine = pltpu.make_async_copy(w_ref, out_ref.at[k], local_sem)
        mine.start()
        first = [_remote(w_ref.at[pl.ds(c * half, half), :], part(k, c), send_sems.at[j], recv_sems.at[j], (px, py, c))
                 for j, (px, py) in enumerate(chips)]
        for cp in first:
            cp.start()
        passed = []
        for j, (px, py) in enumerate(chips):
            got = part(2 * px + py, c)
            _remote(got, got, send_sems.at[j], recv_sems.at[j], (px, py, c)).wait_recv()
            fw = _remote(got, got, send_sems.at[3 + j], recv_sems.at[3 + j], (x, y, 1 - c))
            fw.start()
            passed.append(fw)
        for j, (px, py) in enumerate(chips):
            got = part(2 * px + py, 1 - c)
            _remote(got, got, send_sems.at[3 + j], recv_sems.at[3 + j], (x, y, 1 - c)).wait_recv()
        for cp in first + passed:
            cp.wait_send()
        mine.wait()

    return _pcall(
        body, name=name, in_specs=[_HBM], out_specs=_HBM, out_shape=jax.ShapeDtypeStruct((N_CHIPS, rows, cols), shard.dtype),
        scratch_shapes=[pltpu.SemaphoreType.DMA((6,)), pltpu.SemaphoreType.DMA((6,)), pltpu.SemaphoreType.DMA],
    )(shard)


PAIR_COPIES = 4


def _pair_exchange(v, *, name):
    per = v.shape[0] // PAIR_COPIES
    assert per * PAIR_COPIES == v.shape[0] and per % SUBLANE == 0, v.shape

    def body(v_ref, out_ref, send_sems, recv_sems):
        x, y, c, _ = _place()
        cps = [_remote(v_ref.at[pl.ds(q * per, per), :], out_ref.at[pl.ds(q * per, per), :], send_sems.at[q], recv_sems.at[q], (x, y, 1 - c))
               for q in range(PAIR_COPIES)]
        for cp in cps:
            cp.start()
        for cp in cps:
            cp.wait()

    return _pcall(body, name=name, in_specs=[_HBM], out_specs=_HBM, out_shape=jax.ShapeDtypeStruct(v.shape, v.dtype),
                  scratch_shapes=[pltpu.SemaphoreType.DMA((PAIR_COPIES,)), pltpu.SemaphoreType.DMA((PAIR_COPIES,))])(v)


def _scatter_over_chips(v, *, name):
    def body(v_ref, out_ref, send_sems, recv_sems):
        x, y, c, chips = _place()
        cps = [_remote(v_ref.at[2 * px + py], out_ref.at[j], send_sems.at[j], recv_sems.at[j], (px, py, c)) for j, (px, py) in enumerate(chips)]
        for cp in cps:
            cp.start()
        for cp in cps:
            cp.wait()

    return _pcall(body, name=name, in_specs=[_HBM], out_specs=_HBM, out_shape=jax.ShapeDtypeStruct((N_CHIPS - 1,) + v.shape[1:], v.dtype),
                  scratch_shapes=[pltpu.SemaphoreType.DMA((3,)), pltpu.SemaphoreType.DMA((3,))])(v)


def _all_reduce_small(v, *, name):
    def body(v_ref, out_ref, all_ref, send_sems, recv_sems):
        x, y, c, _ = _place()
        me = 4 * x + 2 * y + c
        all_ref[me] = v_ref[...]
        cps = []
        for f in range(1, N_DEV):
            fx, fy, fc = (f >> 2) & 1, (f >> 1) & 1, f & 1
            to = (x + fx - 2 * x * fx, y + fy - 2 * y * fy, c + fc - 2 * c * fc)
            cps.append(_remote(v_ref, all_ref.at[me], send_sems.at[f - 1], recv_sems.at[f - 1], to))
        for cp in cps:
            cp.start()
        for cp in cps:
            cp.wait()
        acc = all_ref[0]
        for d in range(1, N_DEV):
            acc = acc + all_ref[d]
        out_ref[...] = acc

    vm = pl.BlockSpec(memory_space=pltpu.VMEM)
    return _pcall(body, name=name, in_specs=[vm], out_specs=vm, out_shape=jax.ShapeDtypeStruct(v.shape, v.dtype),
                  scratch_shapes=[pltpu.VMEM((N_DEV,) + v.shape, v.dtype), pltpu.SemaphoreType.DMA((N_DEV - 1,)), pltpu.SemaphoreType.DMA((N_DEV - 1,))])(v)


_BIG = (("ffn1_w_gu", 2), ("ffn1_w_down", 1), ("ffn2_w_gu", 2), ("ffn2_w_down", 1), ("w_out", 1), ("w_mem_kv", 1), ("a_w_in", 2),
        ("a_conv", 2), ("b_w_in", 1), ("b_w_uq", 2), ("w_dkv", 0), ("w_ukv", 1))
_SMALL = ("ffn1_norm", "mix_norm", "ffn2_norm", "mem_norm", "a_A_log", "a_dt_bias", "a_out_norm", "b_q_norm", "kv_in_norm", "kv_lat_norm",
          "final_norm")
_WEIGHTS = ("ffn1_norm", "ffn1_w_gu", "ffn1_w_down", "mix_norm", "ffn2_norm", "ffn2_w_gu", "ffn2_w_down", "w_out", "mem_norm", "w_mem_kv",
            "a_w_in", "a_conv", "a_A_log", "a_dt_bias", "a_out_norm", "b_w_in", "b_q_norm", "b_w_uq", "kv_in_norm", "w_dkv", "kv_lat_norm",
            "w_ukv", "final_norm")
SMALL_ROWS = 24


def _pack(arrs, rows):
    flat = jnp.concatenate([a.reshape(-1) for a in arrs])
    return jnp.pad(flat, (0, rows * PACK_COLS - flat.shape[0])).reshape(rows, PACK_COLS)


def _unpack(packed, shapes):
    flat, off, out = packed.reshape(-1), 0, []
    for shp in shapes:
        n = math.prod(shp)
        out.append(flat[off:off + n].reshape(shp))
        off += n
    return out


def _add(parts, *, name, narrow_copy=False):
    def fn(*ts):
        acc = ts[0].astype(_F32)
        for t in ts[1:]:
            acc = acc + t.astype(_F32)
        return (acc, acc) if narrow_copy else (acc,)

    outs = [(None, parts[0].shape[1], _F32)] + ([(None, parts[0].shape[1], jnp.bfloat16)] if narrow_copy else [])
    res = _rowmap(fn, [_t(a) for a in parts], [], outs, tm=min(PACK_ROW_TILE, parts[0].shape[0]), name=name)
    return res if narrow_copy else res[0]


def _adamw(w, g, m, v, *, name):
    def fn(w_, g_, m_, v_):
        m2 = ADAM_B1 * m_ + (1.0 - ADAM_B1) * g_
        v2 = ADAM_B2 * v_ + (1.0 - ADAM_B2) * (g_ * g_)
        m_hat = m2 / (1.0 - ADAM_B1 ** ADAM_STEP)
        v_hat = v2 / (1.0 - ADAM_B2 ** ADAM_STEP)
        return -ADAM_LR * (m_hat / (jnp.sqrt(v_hat) + ADAM_EPS) + ADAM_WD * w_), m2, v2

    return _rowmap(fn, [_t(w), _t(g), _t(m), _t(v)], [], [(None, w.shape[1], _F32)] * 3, tm=min(PACK_ROW_TILE, w.shape[0]), name=name)


def _step(x, mem, positions, loss_target, w, m, v):
    cx, cy, cc = lax.axis_index("x"), lax.axis_index("y"), lax.axis_index("c")
    chip = 2 * cx + cy
    big = [n for n, _ in _BIG]
    shard_shapes = [w[n].shape for n in big]
    total = sum(math.prod(s) for s in shard_shapes)
    rows = -(-total // (PACK_COLS * 2 * PACK_ROW_TILE)) * 2 * PACK_ROW_TILE
    half = rows // 2

    w_pack = _pack([w[n] for n in big], rows)
    gathered = _gather_over_chips(w_pack.astype(_BF), name="gather_weights")
    pieces = [_unpack(gathered[k], shard_shapes) for k in range(N_CHIPS)]
    full = {n: jnp.concatenate([pieces[k][i] for k in range(N_CHIPS)], axis=ax) for i, (n, ax) in enumerate(_BIG)}
    for n in _SMALL:
        full[n] = w[n]
    conv = w["a_conv"]
    slots = jnp.stack([jnp.where((chip == k) & (cc == 0), conv, 0.0) for k in range(N_CHIPS)])
    conv_all = _unpack(_all_reduce_small(_pack([slots], SMALL_ROWS), name="gather_conv"), [slots.shape])[0]
    conv_full = jnp.concatenate([conv_all[k] for k in range(N_CHIPS)], axis=2)

    p = _compute_form(full, conv_full)
    cos_t, sin_t = _rope_tables(positions[0])
    loss_tile, d_x, g = _local_step(x[0], mem[0], cos_t, sin_t, loss_target[0], p)
    gn = _natural_grads(g)

    def shard_of(a, ax, k):
        size = a.shape[ax] // N_CHIPS
        return lax.slice_in_dim(a, k * size, (k + 1) * size, axis=ax)

    g_pack = jnp.stack([_pack([shard_of(gn[n], ax, k) for n, ax in _BIG], rows) for k in range(N_CHIPS)]).reshape(N_CHIPS, 2, half, PACK_COLS)
    keep = lax.dynamic_index_in_dim(g_pack, cc, axis=1, keepdims=False).reshape(N_CHIPS * half, PACK_COLS)
    send = lax.dynamic_index_in_dim(g_pack, 1 - cc, axis=1, keepdims=False).reshape(N_CHIPS * half, PACK_COLS)
    chip_sum, chip_sum_narrow = _add([keep, _pair_exchange(send, name="grad_pair_sum")], name="grad_add_pair", narrow_copy=True)
    chip_sum = chip_sum.reshape(N_CHIPS, half, PACK_COLS)
    from_chips = _scatter_over_chips(chip_sum_narrow.reshape(N_CHIPS, half, PACK_COLS), name="grad_scatter")
    own = lax.dynamic_index_in_dim(chip_sum, chip, axis=0, keepdims=False)
    mine = _add([own, from_chips[0], from_chips[1], from_chips[2]], name="grad_add_chips")
    theirs = _pair_exchange(mine, name="grad_pair_gather")
    g_big = jnp.concatenate([jnp.where(cc == 0, mine, theirs), jnp.where(cc == 0, theirs, mine)], axis=0)

    d_big, m_big, v_big = _adamw(w_pack, g_big, _pack([m[n] for n in big], rows), _pack([v[n] for n in big], rows), name="adamw_big")

    small_shapes = [w[n].shape for n in _SMALL]
    g_small = _all_reduce_small(_pack([gn[n] for n in _SMALL], SMALL_ROWS), name="grad_small")
    d_small, m_small, v_small = _adamw(_pack([w[n] for n in _SMALL], SMALL_ROWS), g_small, _pack([m[n] for n in _SMALL], SMALL_ROWS),
                                       _pack([v[n] for n in _SMALL], SMALL_ROWS), name="adamw_small")

    def by_name(big_pack, small_pack):
        out = dict(zip(big, _unpack(big_pack, shard_shapes)))
        out.update(zip(_SMALL, _unpack(small_pack, small_shapes)))
        return out

    grads, deltas, new_m, new_v = by_name(g_big, g_small), by_name(d_big, d_small), by_name(m_big, m_small), by_name(v_big, v_small)
    loss = lax.psum(loss_tile[0, 0], ("x", "y", "c"))
    return (loss, d_x[None], *[grads[n] for n in _WEIGHTS], *[deltas[n] for n in _WEIGHTS], *[new_m[n] for n in _WEIGHTS],
            *[new_v[n] for n in _WEIGHTS])


def kernel(x, mem, positions, ffn1_norm, ffn1_w_gu, ffn1_w_down, mix_norm, ffn2_norm, ffn2_w_gu, ffn2_w_down, w_out, mem_norm, w_mem_kv, a_w_in, a_conv, a_A_log, a_dt_bias, a_out_norm, b_w_in, b_q_norm, b_w_uq, kv_in_norm, w_dkv, kv_lat_norm, w_ukv, final_norm, loss_target, m_ffn1_norm, m_ffn1_w_gu, m_ffn1_w_down, m_mix_norm, m_ffn2_norm, m_ffn2_w_gu, m_ffn2_w_down, m_w_out, m_mem_norm, m_w_mem_kv, m_a_w_in, m_a_conv, m_a_A_log, m_a_dt_bias, m_a_out_norm, m_b_w_in, m_b_q_norm, m_b_w_uq, m_kv_in_norm, m_w_dkv, m_kv_lat_norm, m_w_ukv, m_final_norm, v_ffn1_norm, v_ffn1_w_gu, v_ffn1_w_down, v_mix_norm, v_ffn2_norm, v_ffn2_w_gu, v_ffn2_w_down, v_w_out, v_mem_norm, v_w_mem_kv, v_a_w_in, v_a_conv, v_a_A_log, v_a_dt_bias, v_a_out_norm, v_b_w_in, v_b_q_norm, v_b_w_uq, v_kv_in_norm, v_w_dkv, v_kv_lat_norm, v_w_ukv, v_final_norm):
    given = dict(locals())
    w = {n: given[n] for n in _WEIGHTS}
    m = {n: given["m_" + n] for n in _WEIGHTS}
    v = {n: given["v_" + n] for n in _WEIGHTS}
    return _step(x, mem, positions, loss_target, w, m, v)
```

```python
import functools
import math

import jax
import jax.numpy as jnp
from jax import lax
from jax.experimental import pallas as pl
from jax.experimental.pallas import tpu as pltpu

_BF = jnp.bfloat16
_F32 = jnp.float32
_HI = lax.Precision.HIGHEST
_MESH = pl.DeviceIdType.MESH

D_MODEL = 1024
DEPTH = 4
N_A = 2
N_B = 2
CHUNK = 64
EPS = 1e-6
HEADS = 6
HEAD_DIM = 128
A_WIDTH = HEADS * HEAD_DIM
CONV_K = 4
QK_ROPE = 64
Q_LORA = 256
KV_LORA = 256
N_MEM = 256
MEM_HEADS = 4
MEM_HEAD_DIM = 64
MEM_WIDTH = MEM_HEADS * MEM_HEAD_DIM
D_FF = 2816
ROPE_THETA = 10000.0
A_IN = 4 * A_WIDTH + 2 * HEADS + MEM_WIDTH
A_IN_PAD = 3456
UQ_PAD = 2 * A_WIDTH
DKV_PAD = KV_LORA + 128
LANE = 128
SUBLANE = 8

ADAM_LR = 0.001
ADAM_B1 = 0.9
ADAM_B2 = 0.999
ADAM_EPS = 1e-08
ADAM_WD = 0.01
ADAM_STEP = 10

N_CHIPS = 4
N_DEV = 8
PACK_COLS = 1024
PACK_ROW_TILE = 256


def _pcall(body, **kw):
    return pl.pallas_call(body, **kw)


VMEM_LIMIT_V7X = 48 * 2 ** 20
TILE_BYTES = 6 * 2 ** 20


def _cp(grid_rank):
    return pltpu.CompilerParams(dimension_semantics=("arbitrary",) * grid_rank, vmem_limit_bytes=VMEM_LIMIT_V7X)


def _fit_rows(rows, row_bytes):
    while rows > LANE and rows * row_bytes > TILE_BYTES:
        rows //= 2
    return rows


def _fit_cols(n, target, col_bytes):
    return _tile(n, max(LANE, min(target, TILE_BYTES // col_bytes)))


def _tile(n, target):
    best = None
    for t in range(LANE, min(n, target) + 1, LANE):
        if n % t == 0:
            best = t
    return best if best is not None else n


def _dot(a, b):
    return jnp.dot(a.astype(_BF), b.astype(_BF), preferred_element_type=_F32)


def _dot_nt(a, b):
    return lax.dot_general(a.astype(_BF), b.astype(_BF), (((1,), (1,)), ((), ())), preferred_element_type=_F32)


def _dot_tn(a, b):
    return lax.dot_general(a.astype(_BF), b.astype(_BF), (((0,), (0,)), ((), ())), preferred_element_type=_F32)


def _dot_hi(a, b):
    return jnp.dot(a, b, precision=_HI, preferred_element_type=_F32)


def _rowmap(fn, tiles, params, outs, accs=(), *, tm, name, carry=(), reverse=False):
    rows = tiles[0][0].shape[0]
    steps = rows // tm
    nt, npar, no, na, nc = len(tiles), len(params), len(outs), len(accs), len(carry)

    def step_index(i):
        return steps - 1 - i if reverse else i

    in_specs, operands = [], []
    for arr, r, w, cb in tiles:
        r = tm if r is None else r
        w = arr.shape[1] if w is None else w
        assert arr.shape[0] == steps * r and (w % LANE == 0 or w == arr.shape[1]), (name, arr.shape, r, w)
        in_specs.append(pl.BlockSpec((r, w), functools.partial(lambda i, cb: (step_index(i), cb), cb=cb)))
        operands.append(arr)
    for p in params:
        in_specs.append(pl.BlockSpec(p.shape, functools.partial(lambda i, nd: (0,) * nd, nd=p.ndim)))
        operands.append(p)
    out_specs, out_shape = [], []
    for r, cols, dt in outs:
        r = tm if r is None else r
        out_specs.append(pl.BlockSpec((r, cols), lambda i: (step_index(i), 0)))
        out_shape.append(jax.ShapeDtypeStruct((steps * r, cols), dt))
    for shp, dt in accs:
        out_specs.append(pl.BlockSpec(shp, functools.partial(lambda i, nd: (0,) * nd, nd=len(shp))))
        out_shape.append(jax.ShapeDtypeStruct(shp, dt))

    def body(*refs):
        t_refs = refs[:nt]
        p_refs = refs[nt:nt + npar]
        o_refs = refs[nt + npar:nt + npar + no]
        a_refs = refs[nt + npar + no:nt + npar + no + na]
        c_refs = refs[nt + npar + no + na:]
        if na or nc:
            @pl.when(pl.program_id(0) == 0)
            def _():
                for r in a_refs + c_refs:
                    r[...] = jnp.zeros(r.shape, r.dtype)
        vals = fn(*[r[...] for r in t_refs], *[r[...] for r in p_refs], *[r[...] for r in c_refs])
        vals = tuple(vals) if isinstance(vals, (tuple, list)) else (vals,)
        assert len(vals) == no + na + nc, (name, len(vals), no, na, nc)
        for r, v in zip(o_refs, vals[:no]):
            r[...] = v.astype(r.dtype)
        for r, v in zip(a_refs, vals[no:no + na]):
            r[...] += v.astype(r.dtype)
        for r, v in zip(c_refs, vals[no + na:]):
            r[...] = v.astype(r.dtype)

    res = _pcall(
        body, name=name, grid=(steps,), in_specs=in_specs, out_specs=out_specs, out_shape=out_shape,
        scratch_shapes=[pltpu.VMEM(shp, dt) for shp, dt in carry],
        compiler_params=_cp(1),
    )(*operands)
    return res


def _t(arr, width=None, cb=0, rows=None):
    return (arr, rows, width, cb)


def _mm_nn(a, b, *, out_dtype, name, scale=None, res=None, tm=1024, tn=1536):
    m, k = a.shape
    n = b.shape[1]
    tm, tn = _fit_rows(min(tm, m), k * a.dtype.itemsize), _fit_cols(n, tn, k * b.dtype.itemsize)

    def body(a_ref, b_ref, *rest):
        acc = _dot(a_ref[...], b_ref[...])
        if scale is not None:
            acc = acc * scale
        if res is not None:
            acc = acc + rest[0][...]
        rest[-1][...] = acc.astype(rest[-1].dtype)

    in_specs = [pl.BlockSpec((tm, k), lambda i, j: (i, 0)), pl.BlockSpec((k, tn), lambda i, j: (0, j))]
    operands = [a, b]
    if res is not None:
        in_specs.append(pl.BlockSpec((tm, tn), lambda i, j: (i, j)))
        operands.append(res)
    return _pcall(
        body, name=name, grid=(m // tm, n // tn), in_specs=in_specs,
        out_specs=pl.BlockSpec((tm, tn), lambda i, j: (i, j)), out_shape=jax.ShapeDtypeStruct((m, n), out_dtype),
        compiler_params=_cp(2),
    )(*operands)


def _mm_nt(a, b, *, out_dtype, name, scale=None, tm=1024, tn=1536):
    m, k = a.shape
    n = b.shape[0]
    tm, tn = _fit_rows(min(tm, m), k * a.dtype.itemsize), _fit_cols(n, tn, k * b.dtype.itemsize)

    def body(a_ref, b_ref, o_ref):
        acc = _dot_nt(a_ref[...], b_ref[...])
        if scale is not None:
            acc = acc * scale
        o_ref[...] = acc.astype(o_ref.dtype)

    return _pcall(
        body, name=name, grid=(m // tm, n // tn),
        in_specs=[pl.BlockSpec((tm, k), lambda i, j: (i, 0)), pl.BlockSpec((tn, k), lambda i, j: (j, 0))],
        out_specs=pl.BlockSpec((tm, tn), lambda i, j: (i, j)), out_shape=jax.ShapeDtypeStruct((m, n), out_dtype),
        compiler_params=_cp(2),
    )(a, b)


def _mm_tn(a, b, *, name, scale=None, t1=1024, tn=1536, ts=1024):
    s, k1 = a.shape
    n = b.shape[1]
    t1, tn, ts = _tile(k1, t1), _tile(n, tn), min(ts, s)
    steps = s // ts

    def body(a_ref, b_ref, o_ref):
        @pl.when(pl.program_id(2) == 0)
        def _():
            o_ref[...] = jnp.zeros(o_ref.shape, o_ref.dtype)

        o_ref[...] += _dot_tn(a_ref[...], b_ref[...])
        if scale is not None:
            @pl.when(pl.program_id(2) == steps - 1)
            def _():
                o_ref[...] = o_ref[...] * scale

    return _pcall(
        body, name=name, grid=(k1 // t1, n // tn, steps),
        in_specs=[pl.BlockSpec((ts, t1), lambda i, j, r: (r, i)), pl.BlockSpec((ts, tn), lambda i, j, r: (r, j))],
        out_specs=pl.BlockSpec((t1, tn), lambda i, j, r: (i, j)), out_shape=jax.ShapeDtypeStruct((k1, n), _F32),
        compiler_params=_cp(3),
    )(a, b)


def _heads(t, n, w=LANE):
    return [t[:, w * h:w * (h + 1)] for h in range(n)]


def _cat(parts):
    return jnp.concatenate(parts, axis=1)


def _rms(x, g):
    return x * lax.rsqrt(jnp.mean(x * x, axis=-1, keepdims=True) + EPS) * g


def _l2n(x):
    return x * lax.rsqrt(jnp.sum(x * x, axis=-1, keepdims=True) + EPS)


def _sigmoid(x):
    return 0.5 * (jnp.tanh(0.5 * x) + 1.0)


def _silu(x):
    return x * _sigmoid(x)


def _softplus(x):
    return jnp.maximum(x, 0.0) + jnp.log(1.0 + jnp.exp(-jnp.abs(x)))


def _lane_pick(t, h):
    lane = lax.broadcasted_iota(jnp.int32, t.shape, 1)
    return jnp.sum(jnp.where(lane == h, t, 0.0), axis=1, keepdims=True)


def _lane_put(col, h, width=LANE):
    lane = lax.broadcasted_iota(jnp.int32, (col.shape[0], width), 1)
    return jnp.where(lane == h, col, 0.0)


def _vjp(fwd, ins, cts):
    outs, pull = jax.vjp(fwd, *ins)
    outs = outs if isinstance(outs, (tuple, list)) else (outs,)
    cts = tuple(c.astype(o.dtype) for c, o in zip(cts, outs))
    return pull(cts if len(cts) > 1 else cts[0])


def _rot_half_matrix():
    r = lax.broadcasted_iota(jnp.int32, (LANE, LANE), 0)
    c = lax.broadcasted_iota(jnp.int32, (LANE, LANE), 1)
    half = QK_ROPE // 2
    return jnp.where((c < half) & (r == c + half), -1.0, jnp.where((c >= half) & (c < QK_ROPE) & (r == c - half), 1.0, 0.0))


def _rope(x, cos_t, sin_t):
    return x * cos_t + _dot_hi(x, _rot_half_matrix()) * sin_t


def _mem_attn(q, km, vm):
    lane_q = lax.broadcasted_iota(jnp.int32, q.shape, 1)
    lane_v = lax.broadcasted_iota(jnp.int32, vm.shape, 1)
    out = jnp.zeros(q.shape, _F32)
    for h in range(MEM_HEADS):
        lo, hi = MEM_HEAD_DIM * h, MEM_HEAD_DIM * (h + 1)
        qh = jnp.where((lane_q >= lo) & (lane_q < hi), q, 0.0)
        vh = jnp.where((lane_v >= lo) & (lane_v < hi), vm, 0.0)
        sc = _dot_nt(qh, km) * MEM_HEAD_DIM ** -0.5
        sc = sc - lax.stop_gradient(jnp.max(sc, axis=-1, keepdims=True))
        p = jnp.exp(sc)
        p = p / jnp.sum(p, axis=-1, keepdims=True)
        out = out + _dot(p, vh)
    return out


_PAIR = 2 * CHUNK


def _pair_masks():
    ri = lax.broadcasted_iota(jnp.int32, (_PAIR, _PAIR), 0)
    ci = lax.broadcasted_iota(jnp.int32, (_PAIR, _PAIR), 1)
    same = (ri >= CHUNK) == (ci >= CHUNK)
    return same, same & (ri >= ci), same & (ri > ci), ri == ci, same & (ri <= ci)


_NN = (((2,), (1,)), ((0,), (0,)))
_NT = (((2,), (2,)), ((0,), (0,)))
_TN = (((1,), (1,)), ((0,), (0,)))


def _bdot(a, b, dims):
    return lax.dot_general(a.astype(_BF), b.astype(_BF), dims, preferred_element_type=_F32)


def _dot3(a, b, dims):
    a_hi, b_hi = a.astype(_BF), b.astype(_BF)
    a_lo, b_lo = (a - a_hi.astype(_F32)).astype(_BF), (b - b_hi.astype(_F32)).astype(_BF)
    d = lambda x, y: lax.dot_general(x, y, dims, preferred_element_type=_F32)
    return d(a_hi, b_hi) + (d(a_hi, b_lo) + d(a_lo, b_hi))


@jax.custom_vjp
def _mm3(a, b):
    return _dot3(a, b, _NN)


_mm3.defvjp(lambda a, b: (_dot3(a, b, _NN), (a, b)), lambda res, g: (_dot3(g, res[1], _NT), _dot3(res[0], g, _TN)))


def _neumann_inverse(a):
    eye = jnp.where(_pair_masks()[3], 1.0, 0.0)
    n = -a
    t_inv = eye + n
    for _ in range(5):
        n = _dot3(n, n, _NN)
        t_inv = t_inv + _dot3(t_inv, n, _NN)
    return t_inv


@jax.custom_vjp
def _unit_lower_inverse(a):
    return _neumann_inverse(a)


def _unit_lower_inverse_fwd(a):
    t_inv = _neumann_inverse(a)
    return t_inv, t_inv


def _unit_lower_inverse_bwd(t_inv, g):
    return (-_dot3(t_inv, _dot3(g, t_inv, _NT), _TN),)


_unit_lower_inverse.defvjp(_unit_lower_inverse_fwd, _unit_lower_inverse_bwd)


def _gdn_intra_head(q, k, v, beta, gl):
    same, causal, strict, eye, upper = _pair_masks()
    gl_row = jnp.sum(jnp.where(eye, gl, 0.0), axis=-2, keepdims=True)
    g_col = jnp.sum(jnp.where(causal, gl_row, 0.0), axis=-1, keepdims=True)
    g_row = jnp.sum(jnp.where(upper, gl, 0.0), axis=-2, keepdims=True)
    g_last = jnp.sum(jnp.where(same, gl_row, 0.0), axis=-1, keepdims=True)
    decay = jnp.where(causal, jnp.exp(jnp.where(causal, g_col - g_row, 0.0)), 0.0)
    kb = k * beta
    a = jnp.where(strict, _bdot(kb, k, _NT) * decay, 0.0)
    t_inv = _unit_lower_inverse(a)
    e_g = jnp.exp(g_col)
    u = _mm3(t_inv, v * beta)
    w = _mm3(t_inv, kb * e_g)
    qk = _bdot(q, k, _NT) * decay
    return w, u, q * e_g, k * jnp.exp(g_last - g_col), qk, jnp.exp(g_last)


def _gdn_scan_head(s, qd_a, kd_a, w_a, u_a, qk_a, dc_a, qd_b, kd_b, w_b, u_b, qk_b, dc_b):
    zeros = jnp.zeros((HEADS, CHUNK, HEAD_DIM), _F32)
    vn_a = u_a - _bdot(w_a, s, _NN)
    o_a = _bdot(qd_a, s, _NN) + _bdot(qk_a, jnp.concatenate([vn_a, zeros], axis=1), _NN)
    s1 = s * dc_a + _bdot(kd_a, vn_a, _TN)
    vn_b = u_b - _bdot(w_b, s1, _NN)
    o_b = _bdot(qd_b, s1, _NN) + _bdot(qk_b, jnp.concatenate([zeros, vn_b], axis=1), _NN)
    s2 = s1 * dc_b + _bdot(kd_b, vn_b, _TN)
    return o_a, o_b, s2


def _pick_scalar(t, row, lane_i):
    ri = lax.broadcasted_iota(jnp.int32, t.shape, 0)
    ci = lax.broadcasted_iota(jnp.int32, t.shape, 1)
    return jnp.sum(jnp.sum(jnp.where((ri == row) & (ci == lane_i), t, 0.0), axis=1, keepdims=True), axis=0, keepdims=True)


def _put_scalar(val, row, lane_i, shape):
    ri = lax.broadcasted_iota(jnp.int32, shape, 0)
    ci = lax.broadcasted_iota(jnp.int32, shape, 1)
    return jnp.where((ri == row) & (ci == lane_i), val, 0.0)


def _by_head(t):
    return jnp.stack(_heads(t, HEADS))


def _from_heads(t):
    return _cat([t[h] for h in range(HEADS)])


def _state_by_head(s):
    return jnp.stack([s[HEAD_DIM * h:HEAD_DIM * (h + 1), :] for h in range(HEADS)])


def _scan_ins(qd, kd, w, u, qk, dcrow, state):
    ins = [_state_by_head(state)]
    for r0 in (0, CHUNK):
        rs = slice(r0, r0 + CHUNK)
        ins += [_by_head(t[rs, :]) for t in (qd, kd, w, u, qk)]
        ins.append(jnp.stack([_pick_scalar(dcrow, r0, h) for h in range(HEADS)]))
    return ins


def _gdn_scan_fwd_fn(qd, kd, w, u, qk, dcrow, state):
    o_a, o_b, s2 = _gdn_scan_head(*_scan_ins(qd, kd, w, u, qk, dcrow, state))
    return jnp.concatenate([_from_heads(o_a), _from_heads(o_b)], axis=0), state, s2.reshape(state.shape)


def _gdn_scan_bwd_fn(qd, kd, w, u, qk, dcrow, state, d_o, d_state):
    cts = (_by_head(d_o[0:CHUNK, :]), _by_head(d_o[CHUNK:_PAIR, :]), _state_by_head(d_state))
    g = _vjp(_gdn_scan_head, _scan_ins(qd, kd, w, u, qk, dcrow, state), cts)
    grads = tuple(jnp.concatenate([_from_heads(g[1 + t]), _from_heads(g[7 + t])], axis=0) for t in range(5))
    d_dcrow = sum(_put_scalar(g[6][h], 0, h, dcrow.shape) + _put_scalar(g[12][h], CHUNK, h, dcrow.shape) for h in range(HEADS))
    return grads + (d_dcrow, g[0].reshape(state.shape))


def _gdn_intra_ins(q, k, v, bg):
    return [_by_head(q), _by_head(k), _by_head(v), jnp.stack([_lane_pick(bg, h) for h in range(HEADS)]),
            jnp.stack([_lane_pick(bg, HEADS + h) for h in range(HEADS)])]


def _gdn_intra_fwd_fn(q, k, v, bg):
    res = _gdn_intra_head(*_gdn_intra_ins(q, k, v, bg))
    dcrow = sum(_lane_put(res[5][h], h) for h in range(HEADS))
    return tuple(_from_heads(r) for r in res[:5]) + (dcrow,)


def _gdn_intra_bwd_fn(q, k, v, bg, d_w, d_u, d_qd, d_kd, d_qk, d_dcrow):
    cts = tuple(_by_head(d) for d in (d_w, d_u, d_qd, d_kd, d_qk)) + (jnp.stack([_lane_pick(d_dcrow, h) for h in range(HEADS)]),)
    g = _vjp(_gdn_intra_head, _gdn_intra_ins(q, k, v, bg), cts)
    d_bg = sum(_lane_put(g[3][h], h) + _lane_put(g[4][h], HEADS + h) for h in range(HEADS))
    return tuple(_from_heads(g[t]) for t in range(3)) + (d_bg,)


def _gdn_gates(ba, alog, dtb):
    lane = lax.broadcasted_iota(jnp.int32, ba.shape, 1)
    beta = _sigmoid(ba)
    g = -jnp.exp(alog) * _softplus(ba + dtb)
    return jnp.where(lane < HEADS, beta, jnp.where(lane < 2 * HEADS, g, 0.0))


def _gdn_q_head(c):
    return _l2n(_silu(c)) * HEAD_DIM ** -0.5


def _gdn_k_head(c):
    return _l2n(_silu(c))


def _gdn_prep_fwd_fn(qkv_c, ba, alog, dtb):
    hs = _heads(qkv_c, 3 * HEADS)
    q = _cat([_gdn_q_head(c) for c in hs[:HEADS]])
    k = _cat([_gdn_k_head(c) for c in hs[HEADS:2 * HEADS]])
    v = _cat([_silu(c) for c in hs[2 * HEADS:]])
    return q, k, v, _gdn_gates(ba, alog, dtb)


def _gdn_prep_bwd_fn(qkv_c, ba, d_q, d_k, d_v, d_bg, alog, dtb):
    hs = _heads(qkv_c, 3 * HEADS)
    dqs, dks, dvs = _heads(d_q, HEADS), _heads(d_k, HEADS), _heads(d_v, HEADS)
    parts = [_vjp(_gdn_q_head, [hs[h]], (dqs[h],))[0] for h in range(HEADS)]
    parts += [_vjp(_gdn_k_head, [hs[HEADS + h]], (dks[h],))[0] for h in range(HEADS)]
    parts += [_vjp(_silu, [hs[2 * HEADS + h]], (dvs[h],))[0] for h in range(HEADS)]
    d_ba, d_alog, d_dtb = _vjp(_gdn_gates, [ba, alog, dtb], (d_bg,))
    return _cat(parts), d_ba, d_alog, d_dtb


def _a_out_head(o, gate, gain):
    return _rms(o, gain) * _silu(gate)


def _a_post_fwd_fn(o, gate, qm, gain, mem_kv):
    parts = [_a_out_head(oh, gh, gain) for oh, gh in zip(_heads(o, HEADS), _heads(gate, HEADS))]
    parts.append(_mem_attn(qm, mem_kv[:, :MEM_WIDTH], mem_kv[:, MEM_WIDTH:]))
    return (_cat(parts),)


def _a_post_bwd_fn(o, gate, qm, d_cat, gain, mem_kv):
    d_os, d_gates = [], []
    d_gain = jnp.zeros(gain.shape, _F32)
    dc = _heads(d_cat, HEADS + 2)
    for h, (oh, gh) in enumerate(zip(_heads(o, HEADS), _heads(gate, HEADS))):
        g = _vjp(_a_out_head, [oh, gh, gain], (dc[h],))
        d_os.append(g[0])
        d_gates.append(g[1])
        d_gain = d_gain + g[2]
    d_qm, d_km, d_vm = _vjp(_mem_attn, [qm, mem_kv[:, :MEM_WIDTH], mem_kv[:, MEM_WIDTH:]], (d_cat[:, A_WIDTH:],))
    return _cat(d_os), _cat(d_gates + [d_qm]), d_gain, _cat([d_km, d_vm])


def _b_post_fwd_fn(o, qm, mem_kv):
    return (_cat([o.astype(_F32), _mem_attn(qm, mem_kv[:, :MEM_WIDTH], mem_kv[:, MEM_WIDTH:])]),)


def _b_post_bwd_fn(qm, d_cat_m, mem_kv):
    d_qm, d_km, d_vm = _vjp(_mem_attn, [qm, mem_kv[:, :MEM_WIDTH], mem_kv[:, MEM_WIDTH:]], (d_cat_m,))
    return d_qm, _cat([d_km, d_vm])


ATTN_SCALE = (HEAD_DIM + QK_ROPE) ** -0.5


def _rope_q_fwd_fn(qf, cos_t, sin_t):
    hs = _heads(qf, 2 * HEADS)
    return (_cat(hs[:HEADS] + [_rope(x, cos_t, sin_t) for x in hs[HEADS:]]) * ATTN_SCALE,)


def _rope_q_bwd_fn(d_qn, d_qr, cos_t, sin_t):
    f = lambda x: _rope(x, cos_t, sin_t)
    return (_cat([d_qn] + [_vjp(f, [x], (x,))[0] for x in _heads(d_qr, HEADS)]) * ATTN_SCALE,)


def _kv_prep_fwd_fn(ckr, cos_t, sin_t, gain):
    return _rms(ckr[:, :KV_LORA], gain), _rope(ckr[:, KV_LORA:], cos_t, sin_t)


def _kv_prep_bwd_fn(ckr, d_ckv, d_kr, cos_t, sin_t, gain):
    d_lat, d_gain = _vjp(_rms, [ckr[:, :KV_LORA], gain], (d_ckv,))
    f = lambda x: _rope(x, cos_t, sin_t)
    d_rope = _vjp(f, [ckr[:, KV_LORA:]], (d_kr,))[0]
    return _cat([d_lat, d_rope]), d_gain


def _rms_fwd(x, gain, *, name, tm=1024, out_dtype=_BF):
    tm = min(tm, x.shape[0])
    return _rowmap(lambda x_, g_: (_rms(x_.astype(_F32), g_),), [_t(x)], [gain], [(None, x.shape[1], out_dtype)], tm=tm, name=name)[0]


def _rms_bwd(x, d_xn, d_res, gain, *, name, tm=512):
    tm = min(tm, x.shape[0])

    def fn(x_, dxn_, *rest):
        g_ = rest[-1]
        dx, dg = _vjp(_rms, [x_.astype(_F32), g_], (dxn_.astype(_F32),))
        if d_res is not None:
            dx = dx + rest[0]
        return dx, dg

    tiles = [_t(x), _t(d_xn)] + ([_t(d_res)] if d_res is not None else [])
    return _rowmap(fn, tiles, [gain], [(None, x.shape[1], _F32)], [(gain.shape, _F32)], tm=tm, name=name)


FFN_COL_TILE = 1408
FFN_ROW_TILE = 512
FFN_DXN_ROW_TILE = 256


def _ffn_gate_up(x, gain, w_gu, *, name):
    s = x.shape[0]
    tm, tf = min(FFN_ROW_TILE, s), FFN_COL_TILE
    nf = D_FF // tf

    def body(x_ref, gain_ref, wg_ref, wu_ref, xn_ref, g_ref, u_ref, a_ref):
        @pl.when(pl.program_id(1) == 0)
        def _():
            xn_ref[...] = _rms(x_ref[...], gain_ref[...]).astype(xn_ref.dtype)

        xn = xn_ref[...]
        g, u = _dot(xn, wg_ref[...]), _dot(xn, wu_ref[...])
        g_ref[...] = g.astype(g_ref.dtype)
        u_ref[...] = u.astype(u_ref.dtype)
        a_ref[...] = (_silu(g) * u).astype(a_ref.dtype)

    col = pl.BlockSpec((tm, tf), lambda i, j: (i, j))
    wide = jax.ShapeDtypeStruct((s, D_FF), _BF)
    return _pcall(
        body, name=name, grid=(s // tm, nf),
        in_specs=[pl.BlockSpec((tm, D_MODEL), lambda i, j: (i, 0)), pl.BlockSpec((1, D_MODEL), lambda i, j: (0, 0)),
                  pl.BlockSpec((D_MODEL, tf), lambda i, j: (0, j)), pl.BlockSpec((D_MODEL, tf), lambda i, j: (0, nf + j))],
        out_specs=[pl.BlockSpec((tm, D_MODEL), lambda i, j: (i, 0)), col, col, col],
        out_shape=[jax.ShapeDtypeStruct((s, D_MODEL), _BF), wide, wide, wide],
        compiler_params=_cp(2),
    )(x, gain, w_gu, w_gu)


def _ffn_fwd(x, gain, w_gu, w_down, *, name):
    xn, g, u, a = _ffn_gate_up(x, gain, w_gu, name=name + "_gu")
    y = _mm_nn(a, w_down, out_dtype=_F32, name=name + "_down", scale=0.5, res=x, tn=1024)
    return y, (x, xn, g, u, a)


def _ffn_d_gate_up(d_y, g, u, w_down, *, name):
    s = d_y.shape[0]
    tm, tf = min(FFN_ROW_TILE, s), FFN_COL_TILE

    def body(dy_ref, wd_ref, g_ref, u_ref, dg_ref, du_ref):
        da = _dot_nt(dy_ref[...], wd_ref[...]) * 0.5
        gg, uu = g_ref[...].astype(_F32), u_ref[...].astype(_F32)
        sg = _sigmoid(gg)
        dg_ref[...] = (da * uu * sg * (1.0 + gg * (1.0 - sg))).astype(dg_ref.dtype)
        du_ref[...] = (da * gg * sg).astype(du_ref.dtype)

    col = pl.BlockSpec((tm, tf), lambda i, j: (i, j))
    wide = jax.ShapeDtypeStruct((s, D_FF), _BF)
    return _pcall(
        body, name=name, grid=(s // tm, D_FF // tf),
        in_specs=[pl.BlockSpec((tm, D_MODEL), lambda i, j: (i, 0)), pl.BlockSpec((tf, D_MODEL), lambda i, j: (j, 0)), col, col],
        out_specs=[col, col], out_shape=[wide, wide], compiler_params=_cp(2),
    )(d_y, w_down, g, u)


def _ffn_d_x(d_g, d_u, w_gu, x, d_y, gain, *, name):
    s = x.shape[0]
    tm = min(FFN_DXN_ROW_TILE, s)

    def body(dg_ref, du_ref, wg_ref, wu_ref, x_ref, dy_ref, gain_ref, dx_ref, dgain_ref):
        @pl.when(pl.program_id(0) == 0)
        def _():
            dgain_ref[...] = jnp.zeros(dgain_ref.shape, _F32)

        d_xn = _dot_nt(dg_ref[...], wg_ref[...]) + _dot_nt(du_ref[...], wu_ref[...])
        dx, dgain = _vjp(_rms, [x_ref[...], gain_ref[...]], (d_xn,))
        dx_ref[...] = dx + dy_ref[...]
        dgain_ref[...] += dgain

    wide = pl.BlockSpec((tm, D_FF), lambda i: (i, 0))
    rows = pl.BlockSpec((tm, D_MODEL), lambda i: (i, 0))
    one = pl.BlockSpec((1, D_MODEL), lambda i: (0, 0))
    return _pcall(
        body, name=name, grid=(s // tm,),
        in_specs=[wide, wide, pl.BlockSpec((D_MODEL, D_FF), lambda i: (0, 0)), pl.BlockSpec((D_MODEL, D_FF), lambda i: (0, 1)), rows, rows, one],
        out_specs=[rows, one], out_shape=[jax.ShapeDtypeStruct((s, D_MODEL), _F32), jax.ShapeDtypeStruct((1, D_MODEL), _F32)],
        compiler_params=_cp(1),
    )(d_g, d_u, w_gu, w_gu, x, d_y, gain)


def _ffn_bwd(d_y, saved, gain, w_gu, w_down, *, name):
    x, xn, g, u, a = saved
    d_g, d_u = _ffn_d_gate_up(d_y, g, u, w_down, name=name + "_dgu")
    d_w_down = _mm_tn(a, d_y, name=name + "_dwd", scale=0.5, t1=1408, tn=1024)
    d_w_gu = jnp.concatenate([_mm_tn(xn, d_g, name=name + "_dwg", t1=1024, tn=1408), _mm_tn(xn, d_u, name=name + "_dwu", t1=1024, tn=1408)], axis=1)
    d_x, d_gain = _ffn_d_x(d_g, d_u, w_gu, x, d_y, gain, name=name + "_dx")
    return d_x, d_gain, d_w_gu, d_w_down


def _conv_fwd(h, w, *, name, tm=256):
    s = h.shape[0]
    tm = min(tm, s)
    c = 3 * A_WIDTH
    halo = SUBLANE

    def body(x_ref, prev_ref, w_ref, o_ref, buf):
        i = pl.program_id(0)
        buf[0:halo, :] = jnp.where(i == 0, 0.0, prev_ref[...])
        buf[halo:halo + tm, :] = x_ref[...]
        acc = jnp.zeros((tm, c), _F32)
        for j in range(CONV_K):
            acc = acc + buf[pl.ds(halo - (CONV_K - 1) + j, tm), :] * w_ref[j:j + 1, :]
        o_ref[...] = acc

    return _pcall(
        body, name=name, grid=(s // tm,),
        in_specs=[pl.BlockSpec((tm, c), lambda i: (i, 0)),
                  pl.BlockSpec((halo, c), lambda i: (jnp.maximum(i * (tm // halo) - 1, 0), 0)),
                  pl.BlockSpec(w.shape, lambda i: (0, 0))],
        out_specs=pl.BlockSpec((tm, c), lambda i: (i, 0)), out_shape=jax.ShapeDtypeStruct((s, c), _F32),
        scratch_shapes=[pltpu.VMEM((tm + 2 * halo, c), _F32)],
        compiler_params=_cp(1),
    )(h, h, w)


def _conv_bwd(h, d_y, w, *, name, tm=256):
    s = h.shape[0]
    tm = min(tm, s)
    c = 3 * A_WIDTH
    halo = SUBLANE
    steps = s // tm

    def body(x_ref, prev_ref, dy_ref, next_ref, w_ref, dx_ref, dw_ref, xbuf, dybuf):
        i = pl.program_id(0)

        @pl.when(i == 0)
        def _():
            dw_ref[...] = jnp.zeros(dw_ref.shape, dw_ref.dtype)

        xbuf[0:halo, :] = jnp.where(i == 0, 0.0, prev_ref[...])
        xbuf[halo:halo + tm, :] = x_ref[...]
        dybuf[0:tm, :] = dy_ref[...]
        dybuf[tm:tm + halo, :] = jnp.where(i == steps - 1, 0.0, next_ref[...])
        dy = dy_ref[...]
        acc = jnp.zeros((tm, c), _F32)
        for j in range(CONV_K):
            acc = acc + dybuf[pl.ds(CONV_K - 1 - j, tm), :] * w_ref[j:j + 1, :]
            dw_ref[j:j + 1, :] += jnp.sum(dy * xbuf[pl.ds(halo - (CONV_K - 1) + j, tm), :], axis=0, keepdims=True)
        dx_ref[...] = acc

    return _pcall(
        body, name=name, grid=(steps,),
        in_specs=[pl.BlockSpec((tm, c), lambda i: (i, 0)),
                  pl.BlockSpec((halo, c), lambda i: (jnp.maximum(i * (tm // halo) - 1, 0), 0)),
                  pl.BlockSpec((tm, c), lambda i: (i, 0)),
                  pl.BlockSpec((halo, c), lambda i: (jnp.minimum((i + 1) * (tm // halo), s // halo - 1), 0)),
                  pl.BlockSpec(w.shape, lambda i: (0, 0))],
        out_specs=[pl.BlockSpec((tm, c), lambda i: (i, 0)), pl.BlockSpec(w.shape, lambda i: (0, 0))],
        out_shape=[jax.ShapeDtypeStruct((s, c), _F32), jax.ShapeDtypeStruct(w.shape, _F32)],
        scratch_shapes=[pltpu.VMEM((tm + 2 * halo, c), _F32), pltpu.VMEM((tm + 2 * halo, c), _F32)],
        compiler_params=_cp(1),
    )(h, h, d_y, d_y, w)


ATTN_Q_BLOCK = 2048
ATTN_K_SUB = 256
ATTN_DQ_BLOCK = 2048
ATTN_BWD_BLOCK = 1024
ATTN_BWD_SUB = 512


def _chunk_mask(shape, q_axis):
    qi = lax.broadcasted_iota(jnp.int32, shape, q_axis) // CHUNK
    ki = lax.broadcasted_iota(jnp.int32, shape, 1 - q_axis) // CHUNK
    return ki <= qi


def _rows(j, t):
    return pl.ds(pl.multiple_of(j * t, t), t)


def _chunk_mask_at(shape, q_axis, q_off):
    qi = (lax.broadcasted_iota(jnp.int32, shape, q_axis) + q_off) // CHUNK
    ki = lax.broadcasted_iota(jnp.int32, shape, 1 - q_axis) // CHUNK
    return ki <= qi


def _attn_fwd(q_all, kv, kr, *, name):
    s = q_all.shape[0]
    t = min(ATTN_Q_BLOCK, s)
    tk = min(ATTN_K_SUB, t)
    nq, sub, rep = s // t, t // tk, tk // LANE

    def body(qn_ref, qr_ref, kn_ref, kr_ref, v_ref, o_ref, lse_ref, m_sc, acc_sc):
        i = pl.program_id(1)
        m_sc[...] = jnp.full(m_sc.shape, -1e30, _F32)
        acc_sc[...] = jnp.zeros(acc_sc.shape, _F32)
        ones = jnp.ones((tk, LANE), _BF)

        def block(j, first_row):
            qs = slice(first_row, t)
            rows = _rows(j, tk)
            q = _cat([qn_ref[qs, :], qr_ref[qs, :]])
            sc = _dot_nt(q, _cat([kn_ref[rows, :], kr_ref[rows, :]]))
            if first_row is not None:
                sc = jnp.where(_chunk_mask(sc.shape, 0), sc, -1e30)
            m_prev = m_sc[qs, :]
            m_new = jnp.maximum(m_prev, jnp.max(sc, axis=-1, keepdims=True))
            alpha = jnp.exp(m_prev - m_new)
            p = jnp.exp(sc - _cat([m_new] * rep))
            acc_sc[qs, :] = _cat([alpha, alpha]) * acc_sc[qs, :] + _dot(p, _cat([v_ref[rows, :], ones]))
            m_sc[qs, :] = m_new

        def step(j, carry):
            block(j, None)
            return carry

        lax.fori_loop(0, i * sub, step, 0)
        for u in range(sub):
            block(i * sub + u, u * tk)
        row_sum = acc_sc[:, LANE:2 * LANE]
        o_ref[...] = acc_sc[:, 0:LANE] / row_sum
        lse_ref[...] = m_sc[...] + jnp.log(row_sum)

    return _pcall(
        body, name=name, grid=(HEADS, nq),
        in_specs=[pl.BlockSpec((t, LANE), lambda h, i: (i, h)),
                  pl.BlockSpec((t, LANE), lambda h, i: (i, HEADS + h)),
                  pl.BlockSpec((s, LANE), lambda h, i: (0, h)),
                  pl.BlockSpec((s, LANE), lambda h, i: (0, 0)),
                  pl.BlockSpec((s, LANE), lambda h, i: (0, HEADS + h))],
        out_specs=[pl.BlockSpec((t, LANE), lambda h, i: (i, h)), pl.BlockSpec((t, LANE), lambda h, i: (i, h))],
        out_shape=[jax.ShapeDtypeStruct((s, A_WIDTH), _F32), jax.ShapeDtypeStruct((s, A_WIDTH), _F32)],
        scratch_shapes=[pltpu.VMEM((t, LANE), _F32), pltpu.VMEM((t, 2 * LANE), _F32)],
        compiler_params=_cp(2),
    )(q_all, q_all, kv, kr, kv)


def _attn_bwd_prep(o, lse, d_cat, *, name):
    s = o.shape[0]
    t = min(ATTN_BWD_SUB, s)
    nq = s // t

    def body(o_ref, lse_ref, do_ref, dob_ref, dl_ref, lset_ref, dlt_ref):
        for h in range(HEADS):
            sl = slice(LANE * h, LANE * (h + 1))
            rows = slice(SUBLANE * h, SUBLANE * (h + 1))
            do = do_ref[:, sl]
            dl = jnp.broadcast_to(jnp.sum(o_ref[:, sl] * do, axis=-1, keepdims=True), (t, LANE))
            dob_ref[:, sl] = do.astype(dob_ref.dtype)
            dl_ref[:, sl] = dl
            dlt_ref[rows, :] = dl.T[0:SUBLANE, :]
            lset_ref[rows, :] = lse_ref[:, sl].T[0:SUBLANE, :]

    wide = pl.BlockSpec((t, A_WIDTH), lambda i: (i, 0))
    stat = pl.BlockSpec((HEADS * SUBLANE, t), lambda i: (i, 0))
    stat_shape = jax.ShapeDtypeStruct((nq * HEADS * SUBLANE, t), _F32)
    return _pcall(
        body, name=name, grid=(nq,), in_specs=[wide, wide, wide], out_specs=[wide, wide, stat, stat],
        out_shape=[jax.ShapeDtypeStruct((s, A_WIDTH), _BF), jax.ShapeDtypeStruct((s, A_WIDTH), _F32), stat_shape, stat_shape],
        compiler_params=_cp(1),
    )(o, lse, d_cat)


def _attn_dq(q_all, kv, kr, lse, delta, d_o, *, name):
    s = q_all.shape[0]
    t = min(ATTN_DQ_BLOCK, s)
    tk = min(ATTN_BWD_SUB, t)
    nq, sub, rep = s // t, t // tk, tk // LANE

    def body(qn_ref, qr_ref, kn_ref, kr_ref, v_ref, lse_ref, dl_ref, do_ref, dqn_ref, dqr_ref, acc_sc):
        i = pl.program_id(1)
        acc_sc[...] = jnp.zeros(acc_sc.shape, _F32)

        def block(j, first_row):
            qs = slice(first_row, t)
            rows = _rows(j, tk)
            q = _cat([qn_ref[qs, :], qr_ref[qs, :]])
            k = _cat([kn_ref[rows, :], kr_ref[rows, :]])
            p = jnp.exp(_dot_nt(q, k) - _cat([lse_ref[qs, :]] * rep))
            if first_row is not None:
                p = jnp.where(_chunk_mask(p.shape, 0), p, 0.0)
            ds = p * (_dot_nt(do_ref[qs, :], v_ref[rows, :]) - _cat([dl_ref[qs, :]] * rep))
            acc_sc[qs, :] += _dot(ds, k)

        def step(j, carry):
            block(j, None)
            return carry

        lax.fori_loop(0, i * sub, step, 0)
        for u in range(sub):
            block(i * sub + u, u * tk)
        dqn_ref[...] = acc_sc[:, 0:LANE]
        dqr_ref[...] = acc_sc[:, LANE:2 * LANE]

    return _pcall(
        body, name=name, grid=(HEADS, nq),
        in_specs=[pl.BlockSpec((t, LANE), lambda h, i: (i, h)),
                  pl.BlockSpec((t, LANE), lambda h, i: (i, HEADS + h)),
                  pl.BlockSpec((s, LANE), lambda h, i: (0, h)),
                  pl.BlockSpec((s, LANE), lambda h, i: (0, 0)),
                  pl.BlockSpec((s, LANE), lambda h, i: (0, HEADS + h)),
                  pl.BlockSpec((t, LANE), lambda h, i: (i, h)),
                  pl.BlockSpec((t, LANE), lambda h, i: (i, h)),
                  pl.BlockSpec((t, LANE), lambda h, i: (i, h))],
        out_specs=[pl.BlockSpec((t, LANE), lambda h, i: (i, h)), pl.BlockSpec((t, LANE), lambda h, i: (i, h))],
        out_shape=[jax.ShapeDtypeStruct((s, A_WIDTH), _F32), jax.ShapeDtypeStruct((s, A_WIDTH), _F32)],
        scratch_shapes=[pltpu.VMEM((t, 2 * LANE), _F32)],
        compiler_params=_cp(2),
    )(q_all, q_all, kv, kr, kv, lse, delta, d_o)


def _attn_dkv(q_all, kv, kr, lse_t, delta_t, d_o, *, name):
    s = q_all.shape[0]
    t = min(ATTN_BWD_BLOCK, s)
    tq = min(ATTN_BWD_SUB, t)
    nk, sub, nqs = s // t, t // tq, s // tq

    def body(kn_ref, kr_ref, v_ref, qn_ref, qr_ref, do_ref, lset_ref, dlt_ref, dkn_ref, dv_ref, dkr_ref, dk_sc, dv_sc):
        h, j = pl.program_id(0), pl.program_id(1)
        dk_sc[...] = jnp.zeros(dk_sc.shape, _F32)
        dv_sc[...] = jnp.zeros(dv_sc.shape, _F32)

        def block(i, query_off):
            ks = slice(0, t if query_off is None else query_off + tq)
            rows = _rows(i, tq)
            stat = pl.ds(pl.multiple_of((i * HEADS + h) * SUBLANE, SUBLANE), 1)
            q = _cat([qn_ref[rows, :], qr_ref[rows, :]])
            do = do_ref[rows, :]
            p = jnp.exp(_dot_nt(_cat([kn_ref[ks, :], kr_ref[ks, :]]), q) - lset_ref[stat, :])
            if query_off is not None:
                p = jnp.where(_chunk_mask_at(p.shape, 1, query_off), p, 0.0)
            dv_sc[ks, :] += _dot(p, do)
            ds = p * (_dot_nt(v_ref[ks, :], do) - dlt_ref[stat, :])
            dk_sc[ks, :] += _dot(ds, q)

        def step(i, carry):
            block(i, None)
            return carry

        for u in range(sub):
            block(j * sub + u, u * tq)
        lax.fori_loop((j + 1) * sub, nqs, step, 0)
        dkn_ref[...] = dk_sc[:, 0:LANE]
        dkr_ref[...] = dk_sc[:, LANE:2 * LANE]
        dv_ref[...] = dv_sc[...]

    stats = pl.BlockSpec((nqs * HEADS * SUBLANE, tq), lambda h, j: (0, 0))
    return _pcall(
        body, name=name, grid=(HEADS, nk),
        in_specs=[pl.BlockSpec((t, LANE), lambda h, j: (j, h)),
                  pl.BlockSpec((t, LANE), lambda h, j: (j, 0)),
                  pl.BlockSpec((t, LANE), lambda h, j: (j, HEADS + h)),
                  pl.BlockSpec((s, LANE), lambda h, j: (0, h)),
                  pl.BlockSpec((s, LANE), lambda h, j: (0, HEADS + h)),
                  pl.BlockSpec((s, LANE), lambda h, j: (0, h)),
                  stats, stats],
        out_specs=[pl.BlockSpec((t, LANE), lambda h, j: (j, h))] * 3,
        out_shape=[jax.ShapeDtypeStruct((s, A_WIDTH), _F32)] * 3,
        scratch_shapes=[pltpu.VMEM((t, 2 * LANE), _F32), pltpu.VMEM((t, LANE), _F32)],
        compiler_params=_cp(2),
    )(kv, kr, kv, q_all, q_all, d_o, lse_t, delta_t)


def _final_loss(x, tgt, gain, *, name, tm=512):
    tm = min(tm, x.shape[0])

    def fn(x_, t_, g_):
        def f(xx, gg):
            err = _rms(xx, gg) - t_
            return 0.5 * jnp.sum(jnp.sum(err * err, axis=1, keepdims=True) / D_MODEL, axis=0, keepdims=True)

        loss, pull = jax.vjp(f, x_, g_)
        dx, dg = pull(jnp.ones((1, 1), _F32))
        return dx, dg, jnp.broadcast_to(loss, (SUBLANE, LANE))

    return _rowmap(fn, [_t(x), _t(tgt)], [gain], [(None, D_MODEL, _F32)], [(gain.shape, _F32), ((SUBLANE, LANE), _F32)], tm=tm, name=name)


def _local_step(x, mem, cos_t, sin_t, tgt, p):
    s = x.shape[0]
    g = {}
    row = lambda a: a.reshape(1, -1)
    tm_e = min(256, s)

    mem_n = _rms_fwd(mem, row(p["mem_norm"]), name="mem_norm")
    mem_kv_all = _mm_nn(mem_n, p["w_mem_all"], out_dtype=_F32, name="mem_kv", tn=1024)
    mem_kv = [mem_kv_all[:, 2 * MEM_WIDTH * l:2 * MEM_WIDTH * (l + 1)] for l in range(DEPTH)]

    sv = []
    for i in range(N_A):
        l = i
        r = {}
        r["x0"] = x
        x, r["ffn1"] = _ffn_fwd(x, row(p["ffn1_norm"][l]), p["ffn1_w_gu"][l], p["ffn1_w_down"][l], name=f"a{i}_ffn1")
        r["x1"] = x
        xn = _rms_fwd(x, row(p["mix_norm"][l]), name=f"a{i}_mixnorm")
        h = _mm_nn(xn, p["a_w_in"][i], out_dtype=_F32, name=f"a{i}_in", tn=1152)
        qkv_c = _conv_fwd(h, p["a_conv"][i], name=f"a{i}_conv")
        alog, dtb = p["a_A_log_row"][i], p["a_dt_bias_row"][i]
        q, k, v, bg = _rowmap(_gdn_prep_fwd_fn, [_t(qkv_c), _t(h, LANE, 26)], [alog, dtb],
                              [(None, A_WIDTH, _F32)] * 3 + [(None, LANE, _F32)], tm=tm_e, name=f"a{i}_prep")
        w_, u_, qd, kd, qk, dcrow = _rowmap(_gdn_intra_fwd_fn, [_t(q), _t(k), _t(v), _t(bg)], [],
                                            [(None, A_WIDTH, _F32)] * 5 + [(None, LANE, _F32)], tm=_PAIR, name=f"a{i}_intra")
        o, states = _rowmap(_gdn_scan_fwd_fn, [_t(qd), _t(kd), _t(w_), _t(u_), _t(qk), _t(dcrow)], [],
                            [(None, A_WIDTH, _F32), (A_WIDTH, HEAD_DIM, _F32)], tm=_PAIR, name=f"a{i}_scan",
                            carry=[((A_WIDTH, HEAD_DIM), _F32)])
        gain_o = row(p["a_out_norm"][i])
        cat = _rowmap(_a_post_fwd_fn, [_t(o), _t(h, A_WIDTH, 3), _t(h, MEM_WIDTH, 12)], [gain_o, mem_kv[l]],
                      [(None, D_MODEL, _BF)], tm=tm_e, name=f"a{i}_post")[0]
        x = _mm_nn(cat, p["w_out"][l], out_dtype=_F32, name=f"a{i}_out", res=x, tn=1024)
        r.update(xn=xn, h=h, qkv_c=qkv_c, q=q, k=k, v=v, bg=bg, w=w_, u=u_, qd=qd, kd=kd, qk=qk, dcrow=dcrow, o=o, states=states, cat=cat)
        r["x2"] = x
        x, r["ffn2"] = _ffn_fwd(x, row(p["ffn2_norm"][l]), p["ffn2_w_gu"][l], p["ffn2_w_down"][l], name=f"a{i}_ffn2")
        sv.append(r)

    kvs = {"x": x}
    xn_kv = _rms_fwd(x, row(p["kv_in_norm"]), name="kv_innorm")
    ckr = _mm_nn(xn_kv, p["w_dkv"], out_dtype=_F32, name="kv_down")
    ckv, k_rope = _rowmap(_kv_prep_fwd_fn, [_t(ckr), _t(cos_t), _t(sin_t)], [row(p["kv_lat_norm"])],
                          [(None, KV_LORA, _BF), (None, LANE, _BF)], tm=tm_e, name="kv_prep")
    kvu = _mm_nn(ckv, p["w_ukv"], out_dtype=_BF, name="kv_up")
    kvs.update(xn=xn_kv, ckr=ckr, ckv=ckv)

    for j in range(N_B):
        l = N_A + j
        r = {}
        x, r["ffn1"] = _ffn_fwd(x, row(p["ffn1_norm"][l]), p["ffn1_w_gu"][l], p["ffn1_w_down"][l], name=f"b{j}_ffn1")
        r["x1"] = x
        xn = _rms_fwd(x, row(p["mix_norm"][l]), name=f"b{j}_mixnorm")
        h = _mm_nn(xn, p["b_w_in"][j], out_dtype=_F32, name=f"b{j}_in")
        gain_q = row(p["b_q_norm"][j])
        cqn = _rowmap(lambda c_, g_: (_rms(c_, g_),), [_t(h, Q_LORA, 0)], [gain_q], [(None, Q_LORA, _BF)], tm=tm_e, name=f"b{j}_qnorm")[0]
        qf = _mm_nn(cqn, p["b_w_uq"][j], out_dtype=_F32, name=f"b{j}_uq")
        q_all = _rowmap(_rope_q_fwd_fn, [_t(qf), _t(cos_t), _t(sin_t)], [], [(None, UQ_PAD, _BF)], tm=tm_e, name=f"b{j}_rope")[0]
        o_b, lse = _attn_fwd(q_all, kvu, k_rope, name=f"b{j}_attn")
        cat = _rowmap(_b_post_fwd_fn, [_t(o_b), _t(h, MEM_WIDTH, 1)], [mem_kv[l]], [(None, D_MODEL, _BF)], tm=tm_e, name=f"b{j}_post")[0]
        x = _mm_nn(cat, p["w_out"][l], out_dtype=_F32, name=f"b{j}_out", res=x, tn=1024)
        r.update(xn=xn, h=h, cqn=cqn, q_all=q_all, o_b=o_b, lse=lse, cat=cat)
        x, r["ffn2"] = _ffn_fwd(x, row(p["ffn2_norm"][l]), p["ffn2_w_gu"][l], p["ffn2_w_down"][l], name=f"b{j}_ffn2")
        sv.append(r)

    dx, g["final_norm"], loss = _final_loss(x, tgt, row(p["final_norm"]), name="loss")

    per_layer = lambda: [None] * DEPTH
    for n in ("ffn1_norm", "ffn1_w_gu", "ffn1_w_down", "mix_norm", "ffn2_norm", "ffn2_w_gu", "ffn2_w_down", "w_out", "mem_kv"):
        g[n] = per_layer()
    for n in ("a_w_in", "a_conv", "a_A_log_row", "a_dt_bias_row", "a_out_norm", "b_w_in", "b_q_norm", "b_w_uq"):
        g[n] = [None] * N_A
    d_kv_parts = []

    for j in reversed(range(N_B)):
        l = N_A + j
        r = sv[l]
        dx, g["ffn2_norm"][l], g["ffn2_w_gu"][l], g["ffn2_w_down"][l] = _ffn_bwd(
            dx, r["ffn2"], row(p["ffn2_norm"][l]), p["ffn2_w_gu"][l], p["ffn2_w_down"][l], name=f"b{j}_ffn2b")
        d_cat = _mm_nt(dx, p["w_out"][l], out_dtype=_F32, name=f"b{j}_dcat", tn=1024)
        g["w_out"][l] = _mm_tn(r["cat"], dx, name=f"b{j}_dwout", tn=1024)
        d_qm, g["mem_kv"][l] = _rowmap(_b_post_bwd_fn, [_t(r["h"], MEM_WIDTH, 1), _t(d_cat, MEM_WIDTH, 3)], [mem_kv[l]],
                                      [(None, MEM_WIDTH, _F32)], [((N_MEM, 2 * MEM_WIDTH), _F32)], tm=tm_e, name=f"b{j}_postb")
        d_o, delta, lse_t, delta_t = _attn_bwd_prep(r["o_b"], r["lse"], d_cat, name=f"b{j}_delta")
        dqn, dqr = _attn_dq(r["q_all"], kvu, k_rope, r["lse"], delta, d_o, name=f"b{j}_attn_dq")
        d_kv_parts.append(_attn_dkv(r["q_all"], kvu, k_rope, lse_t, delta_t, d_o, name=f"b{j}_attn_dkv"))
        d_qf = _rowmap(_rope_q_bwd_fn, [_t(dqn), _t(dqr), _t(cos_t), _t(sin_t)], [], [(None, UQ_PAD, _F32)], tm=tm_e, name=f"b{j}_ropeb")[0]
        d_cqn = _mm_nt(d_qf, p["b_w_uq"][j], out_dtype=_F32, name=f"b{j}_dcqn")
        g["b_w_uq"][j] = _mm_tn(r["cqn"], d_qf, name=f"b{j}_dwuq")
        gain_q = row(p["b_q_norm"][j])
        d_cq, g["b_q_norm"][j] = _rowmap(lambda c_, d_, g_: _vjp(_rms, [c_, g_], (d_,)), [_t(r["h"], Q_LORA, 0), _t(d_cqn)], [gain_q],
                                        [(None, Q_LORA, _F32)], [((1, Q_LORA), _F32)], tm=tm_e, name=f"b{j}_qnormb")
        d_h = jnp.concatenate([d_cq, d_qm], axis=1)
        d_xn = _mm_nt(d_h, p["b_w_in"][j], out_dtype=_F32, name=f"b{j}_dxn", tn=1024)
        g["b_w_in"][j] = _mm_tn(r["xn"], d_h, name=f"b{j}_dwin")
        dx, g["mix_norm"][l] = _rms_bwd(r["x1"], d_xn, dx, row(p["mix_norm"][l]), name=f"b{j}_mixnormb")
        dx, g["ffn1_norm"][l], g["ffn1_w_gu"][l], g["ffn1_w_down"][l] = _ffn_bwd(
            dx, r["ffn1"], row(p["ffn1_norm"][l]), p["ffn1_w_gu"][l], p["ffn1_w_down"][l], name=f"b{j}_ffn1b")

    def kv_sum(*parts):
        dkn = sum(parts[0::3][1:], parts[0])
        dv = sum(parts[1::3][1:], parts[1])
        dkr = sum(parts[2::3][1:], parts[2])
        return _cat([dkn, dv]), sum(_heads(dkr, HEADS)[1:], _heads(dkr, HEADS)[0])

    d_kvu, d_kr = _rowmap(kv_sum, [_t(a) for part in d_kv_parts for a in part], [], [(None, 2 * A_WIDTH, _F32), (None, LANE, _F32)],
                          tm=tm_e, name="kv_dsum")
    d_ckv = _mm_nt(d_kvu, p["w_ukv"], out_dtype=_F32, name="kv_dckv")
    g["w_ukv"] = _mm_tn(kvs["ckv"], d_kvu, name="kv_dwukv")
    d_ckr, g["kv_lat_norm"] = _rowmap(_kv_prep_bwd_fn, [_t(kvs["ckr"]), _t(d_ckv), _t(d_kr), _t(cos_t), _t(sin_t)], [row(p["kv_lat_norm"])],
                                     [(None, DKV_PAD, _F32)], [((1, KV_LORA), _F32)], tm=tm_e, name="kv_prepb")
    d_xn = _mm_nt(d_ckr, p["w_dkv"], out_dtype=_F32, name="kv_dxn", tn=1024)
    g["w_dkv"] = _mm_tn(kvs["xn"], d_ckr, name="kv_dwdkv")
    dx, g["kv_in_norm"] = _rms_bwd(kvs["x"], d_xn, dx, row(p["kv_in_norm"]), name="kv_innormb")

    for i in reversed(range(N_A)):
        l = i
        r = sv[l]
        dx, g["ffn2_norm"][l], g["ffn2_w_gu"][l], g["ffn2_w_down"][l] = _ffn_bwd(
            dx, r["ffn2"], row(p["ffn2_norm"][l]), p["ffn2_w_gu"][l], p["ffn2_w_down"][l], name=f"a{i}_ffn2b")
        d_cat = _mm_nt(dx, p["w_out"][l], out_dtype=_F32, name=f"a{i}_dcat", tn=1024)
        g["w_out"][l] = _mm_tn(r["cat"], dx, name=f"a{i}_dwout", tn=1024)
        gain_o = row(p["a_out_norm"][i])
        h = r["h"]
        d_o, d_hpart, g["a_out_norm"][i], g["mem_kv"][l] = _rowmap(
            _a_post_bwd_fn, [_t(r["o"]), _t(h, A_WIDTH, 3), _t(h, MEM_WIDTH, 12), _t(d_cat)], [gain_o, mem_kv[l]],
            [(None, A_WIDTH, _F32), (None, D_MODEL, _F32)], [((1, HEAD_DIM), _F32), ((N_MEM, 2 * MEM_WIDTH), _F32)], tm=tm_e, name=f"a{i}_postb")
        d_qd, d_kd, d_w, d_u, d_qk, d_dcrow = _rowmap(
            _gdn_scan_bwd_fn, [_t(r["qd"]), _t(r["kd"]), _t(r["w"]), _t(r["u"]), _t(r["qk"]), _t(r["dcrow"]), _t(r["states"], rows=A_WIDTH), _t(d_o)], [],
            [(None, A_WIDTH, _F32)] * 5 + [(None, LANE, _F32)], tm=_PAIR, name=f"a{i}_scanb", carry=[((A_WIDTH, HEAD_DIM), _F32)], reverse=True)
        d_q, d_k, d_v, d_bg = _rowmap(
            _gdn_intra_bwd_fn, [_t(r["q"]), _t(r["k"]), _t(r["v"]), _t(r["bg"]), _t(d_w), _t(d_u), _t(d_qd), _t(d_kd), _t(d_qk), _t(d_dcrow)], [],
            [(None, A_WIDTH, _F32)] * 3 + [(None, LANE, _F32)], tm=_PAIR, name=f"a{i}_intrab")
        alog, dtb = p["a_A_log_row"][i], p["a_dt_bias_row"][i]
        d_qkv_c, d_ba, g["a_A_log_row"][i], g["a_dt_bias_row"][i] = _rowmap(
            _gdn_prep_bwd_fn, [_t(r["qkv_c"]), _t(h, LANE, 26), _t(d_q), _t(d_k), _t(d_v), _t(d_bg)], [alog, dtb],
            [(None, 3 * A_WIDTH, _F32), (None, LANE, _F32)], [((1, LANE), _F32), ((1, LANE), _F32)], tm=tm_e, name=f"a{i}_prepb")
        d_qkv, g["a_conv"][i] = _conv_bwd(h, d_qkv_c, p["a_conv"][i], name=f"a{i}_convb")
        d_h = jnp.concatenate([d_qkv, d_hpart, d_ba], axis=1)
        d_xn = _mm_nt(d_h, p["a_w_in"][i], out_dtype=_F32, name=f"a{i}_dxn", tn=1024)
        g["a_w_in"][i] = _mm_tn(r["xn"], d_h, name=f"a{i}_dwin", tn=1152)
        dx, g["mix_norm"][l] = _rms_bwd(r["x1"], d_xn, dx, row(p["mix_norm"][l]), name=f"a{i}_mixnormb")
        dx, g["ffn1_norm"][l], g["ffn1_w_gu"][l], g["ffn1_w_down"][l] = _ffn_bwd(
            dx, r["ffn1"], row(p["ffn1_norm"][l]), p["ffn1_w_gu"][l], p["ffn1_w_down"][l], name=f"a{i}_ffn1b")

    d_mem_kv_all = jnp.concatenate(g.pop("mem_kv"), axis=1)
    d_mem_n = _mm_nt(d_mem_kv_all, p["w_mem_all"], out_dtype=_F32, name="mem_dn", tn=1024)
    g["w_mem_all"] = _mm_tn(mem_n, d_mem_kv_all, name="mem_dw", tn=1024)
    _, g["mem_norm"] = _rms_bwd(mem, d_mem_n, None, row(p["mem_norm"]), name="mem_normb")
    return loss, dx, g


_NOPE_ROPE = HEAD_DIM + QK_ROPE
_QKV_GATE = 4 * A_WIDTH
_BETA_AT = _QKV_GATE + MEM_WIDTH


def _lane_row(vals, at):
    n = vals.shape[0]
    return jnp.concatenate([jnp.zeros((at,), _F32), vals.astype(_F32), jnp.zeros((LANE - at - n,), _F32)]).reshape(1, LANE)


def _compute_form(w, conv_f32):
    p = {n: w[n] for n in ("ffn1_norm", "ffn1_w_gu", "ffn1_w_down", "mix_norm", "ffn2_norm", "ffn2_w_gu", "ffn2_w_down", "w_out",
                           "mem_norm", "a_out_norm", "b_w_in", "b_q_norm", "kv_in_norm", "kv_lat_norm", "final_norm")}
    wm = w["w_mem_kv"]
    p["w_mem_all"] = jnp.transpose(wm, (1, 0, 2)).reshape(D_MODEL, DEPTH * 2 * MEM_WIDTH)
    a = w["a_w_in"]
    pad = jnp.zeros((N_A, D_MODEL, A_IN_PAD - A_IN), a.dtype)
    p["a_w_in"] = jnp.concatenate([a[:, :, :_QKV_GATE], a[:, :, _QKV_GATE + 2 * HEADS:], a[:, :, _QKV_GATE:_QKV_GATE + 2 * HEADS], pad], axis=2)
    p["a_conv"] = jnp.concatenate([conv_f32, jnp.zeros((N_A, SUBLANE - CONV_K, 3 * A_WIDTH), _F32)], axis=1)
    p["a_A_log_row"] = [_lane_row(w["a_A_log"][i], HEADS) for i in range(N_A)]
    p["a_dt_bias_row"] = [_lane_row(w["a_dt_bias"][i], HEADS) for i in range(N_A)]
    uq = w["b_w_uq"].reshape(N_B, Q_LORA, HEADS, _NOPE_ROPE)
    rope = jnp.concatenate([uq[..., HEAD_DIM:], jnp.zeros((N_B, Q_LORA, HEADS, LANE - QK_ROPE), uq.dtype)], axis=-1)
    p["b_w_uq"] = jnp.concatenate([uq[..., :HEAD_DIM].reshape(N_B, Q_LORA, A_WIDTH), rope.reshape(N_B, Q_LORA, A_WIDTH)], axis=-1)
    dkv = w["w_dkv"]
    p["w_dkv"] = jnp.concatenate([dkv, jnp.zeros((D_MODEL, DKV_PAD - dkv.shape[1]), dkv.dtype)], axis=1)
    ukv = w["w_ukv"].reshape(KV_LORA, HEADS, 2 * HEAD_DIM)
    p["w_ukv"] = jnp.concatenate([ukv[..., :HEAD_DIM].reshape(KV_LORA, A_WIDTH), ukv[..., HEAD_DIM:].reshape(KV_LORA, A_WIDTH)], axis=-1)
    return p


def _natural_grads(g):
    st = lambda xs: jnp.stack(xs, axis=0)
    n = {}
    for k in ("ffn1_norm", "mix_norm", "ffn2_norm"):
        n[k] = st(g[k]).reshape(DEPTH, D_MODEL)
    for k in ("ffn1_w_gu", "ffn1_w_down", "ffn2_w_gu", "ffn2_w_down", "w_out", "b_w_in"):
        n[k] = st(g[k])
    n["mem_norm"] = g["mem_norm"].reshape(D_MODEL)
    n["w_mem_kv"] = jnp.transpose(g["w_mem_all"].reshape(D_MODEL, DEPTH, 2 * MEM_WIDTH), (1, 0, 2))
    a = st(g["a_w_in"])
    n["a_w_in"] = jnp.concatenate([a[:, :, :_QKV_GATE], a[:, :, _BETA_AT:_BETA_AT + 2 * HEADS], a[:, :, _QKV_GATE:_BETA_AT]], axis=2)
    n["a_conv"] = st(g["a_conv"])[:, :CONV_K]
    n["a_A_log"] = st(g["a_A_log_row"])[:, 0, HEADS:2 * HEADS]
    n["a_dt_bias"] = st(g["a_dt_bias_row"])[:, 0, HEADS:2 * HEADS]
    n["a_out_norm"] = st(g["a_out_norm"]).reshape(N_A, HEAD_DIM)
    n["b_q_norm"] = st(g["b_q_norm"]).reshape(N_B, Q_LORA)
    uq = st(g["b_w_uq"])
    nope = uq[:, :, :A_WIDTH].reshape(N_B, Q_LORA, HEADS, HEAD_DIM)
    rope = uq[:, :, A_WIDTH:].reshape(N_B, Q_LORA, HEADS, LANE)[..., :QK_ROPE]
    n["b_w_uq"] = jnp.concatenate([nope, rope], axis=-1).reshape(N_B, Q_LORA, HEADS * _NOPE_ROPE)
    n["kv_in_norm"] = g["kv_in_norm"].reshape(D_MODEL)
    n["w_dkv"] = g["w_dkv"][:, :KV_LORA + QK_ROPE]
    n["kv_lat_norm"] = g["kv_lat_norm"].reshape(KV_LORA)
    ukv = g["w_ukv"]
    n["w_ukv"] = jnp.concatenate([ukv[:, :A_WIDTH].reshape(KV_LORA, HEADS, HEAD_DIM), ukv[:, A_WIDTH:].reshape(KV_LORA, HEADS, HEAD_DIM)],
                                 axis=-1).reshape(KV_LORA, HEADS * 2 * HEAD_DIM)
    n["final_norm"] = g["final_norm"].reshape(D_MODEL)
    return n


def _rope_tables(positions):
    inv = ROPE_THETA ** (-jnp.arange(0, QK_ROPE, 2, dtype=_F32) / QK_ROPE)
    ang = positions.astype(_F32)[:, None] * inv
    z = jnp.zeros((positions.shape[0], LANE - QK_ROPE), _F32)
    cos, sin = jnp.cos(ang), jnp.sin(ang)
    return jnp.concatenate([cos, cos, z], axis=1), jnp.concatenate([sin, sin, z], axis=1)


_HBM = pl.BlockSpec(memory_space=pltpu.HBM)


def _place():
    x, y, c = lax.axis_index("x"), lax.axis_index("y"), lax.axis_index("c")
    return x, y, c, [(1 - x, y), (x, 1 - y), (1 - x, 1 - y)]


def _remote(src, dst, send_sem, recv_sem, to):
    return pltpu.make_async_remote_copy(src_ref=src, dst_ref=dst, send_sem=send_sem, recv_sem=recv_sem, device_id=to, device_id_type=_MESH)


def _gather_over_chips(shard, *, name):
    rows, cols = shard.shape
    half = rows // 2

    def body(w_ref, out_ref, send_sems, recv_sems, local_sem):
        x, y, c, chips = _place()
        k = 2 * x + y

        def part(chip, h):
            return out_ref.at[chip, pl.ds(h * half, half), :]

        mine = pltpu.make_async_copy(w_ref, out_ref.at[k], local_sem)
        mine.start()
        first = [_remote(w_ref.at[pl.ds(c * half, half), :], part(k, c), send_sems.at[j], recv_sems.at[j], (px, py, c))
                 for j, (px, py) in enumerate(chips)]
        for cp in first:
            cp.start()
        passed = []
        for j, (px, py) in enumerate(chips):
            got = part(2 * px + py, c)
            _remote(got, got, send_sems.at[j], recv_sems.at[j], (px, py, c)).wait_recv()
            fw = _remote(got, got, send_sems.at[3 + j], recv_sems.at[3 + j], (x, y, 1 - c))
            fw.start()
            passed.append(fw)
        for j, (px, py) in enumerate(chips):
            got = part(2 * px + py, 1 - c)
            _remote(got, got, send_sems.at[3 + j], recv_sems.at[3 + j], (x, y, 1 - c)).wait_recv()
        for cp in first + passed:
            cp.wait_send()
        mine.wait()

    return _pcall(
        body, name=name, in_specs=[_HBM], out_specs=_HBM, out_shape=jax.ShapeDtypeStruct((N_CHIPS, rows, cols), shard.dtype),
        scratch_shapes=[pltpu.SemaphoreType.DMA((6,)), pltpu.SemaphoreType.DMA((6,)), pltpu.SemaphoreType.DMA],
    )(shard)


PAIR_COPIES = 4


def _pair_exchange(v, *, name):
    per = v.shape[0] // PAIR_COPIES
    assert per * PAIR_COPIES == v.shape[0] and per % SUBLANE == 0, v.shape

    def body(v_ref, out_ref, send_sems, recv_sems):
        x, y, c, _ = _place()
        cps = [_remote(v_ref.at[pl.ds(q * per, per), :], out_ref.at[pl.ds(q * per, per), :], send_sems.at[q], recv_sems.at[q], (x, y, 1 - c))
               for q in range(PAIR_COPIES)]
        for cp in cps:
            cp.start()
        for cp in cps:
            cp.wait()

    return _pcall(body, name=name, in_specs=[_HBM], out_specs=_HBM, out_shape=jax.ShapeDtypeStruct(v.shape, v.dtype),
                  scratch_shapes=[pltpu.SemaphoreType.DMA((PAIR_COPIES,)), pltpu.SemaphoreType.DMA((PAIR_COPIES,))])(v)


def _scatter_over_chips(v, *, name):
    def body(v_ref, out_ref, send_sems, recv_sems):
        x, y, c, chips = _place()
        cps = [_remote(v_ref.at[2 * px + py], out_ref.at[j], send_sems.at[j], recv_sems.at[j], (px, py, c)) for j, (px, py) in enumerate(chips)]
        for cp in cps:
            cp.start()
        for cp in cps:
            cp.wait()

    return _pcall(body, name=name, in_specs=[_HBM], out_specs=_HBM, out_shape=jax.ShapeDtypeStruct((N_CHIPS - 1,) + v.shape[1:], v.dtype),
                  scratch_shapes=[pltpu.SemaphoreType.DMA((3,)), pltpu.SemaphoreType.DMA((3,))])(v)


def _all_reduce_small(v, *, name):
    def body(v_ref, out_ref, all_ref, send_sems, recv_sems):
        x, y, c, _ = _place()
        me = 4 * x + 2 * y + c
        all_ref[me] = v_ref[...]
        cps = []
        for f in range(1, N_DEV):
            fx, fy, fc = (f >> 2) & 1, (f >> 1) & 1, f & 1
            to = (x + fx - 2 * x * fx, y + fy - 2 * y * fy, c + fc - 2 * c * fc)
            cps.append(_remote(v_ref, all_ref.at[me], send_sems.at[f - 1], recv_sems.at[f - 1], to))
        for cp in cps:
            cp.start()
        for cp in cps:
            cp.wait()
        acc = all_ref[0]
        for d in range(1, N_DEV):
            acc = acc + all_ref[d]
        out_ref[...] = acc

    vm = pl.BlockSpec(memory_space=pltpu.VMEM)
    return _pcall(body, name=name, in_specs=[vm], out_specs=vm, out_shape=jax.ShapeDtypeStruct(v.shape, v.dtype),
                  scratch_shapes=[pltpu.VMEM((N_DEV,) + v.shape, v.dtype), pltpu.SemaphoreType.DMA((N_DEV - 1,)), pltpu.SemaphoreType.DMA((N_DEV - 1,))])(v)


_BIG = (("ffn1_w_gu", 2), ("ffn1_w_down", 1), ("ffn2_w_gu", 2), ("ffn2_w_down", 1), ("w_out", 1), ("w_mem_kv", 1), ("a_w_in", 2),
        ("a_conv", 2), ("b_w_in", 1), ("b_w_uq", 2), ("w_dkv", 0), ("w_ukv", 1))
_SMALL = ("ffn1_norm", "mix_norm", "ffn2_norm", "mem_norm", "a_A_log", "a_dt_bias", "a_out_norm", "b_q_norm", "kv_in_norm", "kv_lat_norm",
          "final_norm")
_WEIGHTS = ("ffn1_norm", "ffn1_w_gu", "ffn1_w_down", "mix_norm", "ffn2_norm", "ffn2_w_gu", "ffn2_w_down", "w_out", "mem_norm", "w_mem_kv",
            "a_w_in", "a_conv", "a_A_log", "a_dt_bias", "a_out_norm", "b_w_in", "b_q_norm", "b_w_uq", "kv_in_norm", "w_dkv", "kv_lat_norm",
            "w_ukv", "final_norm")


PACK_PIECE_ROWS = 16


def _piece_rows(shape):
    return -(-math.prod(shape) // (PACK_COLS * PACK_PIECE_ROWS)) * PACK_PIECE_ROWS


def _packed_rows(shapes):
    return sum(_piece_rows(s) for s in shapes)


def _pack(arrs, rows):
    pieces = []
    for a in arrs:
        n, r = a.size, _piece_rows(a.shape)
        flat = a.reshape(-1)
        if r * PACK_COLS != n:
            flat = jnp.concatenate([flat, jnp.zeros((r * PACK_COLS - n,), a.dtype)])
        pieces.append(flat.reshape(r, PACK_COLS))
    used = sum(p.shape[0] for p in pieces)
    if rows > used:
        pieces.append(jnp.zeros((rows - used, PACK_COLS), arrs[0].dtype))
    return jnp.concatenate(pieces, axis=0)


def _unpack(packed, shapes):
    off, out = 0, []
    for shp in shapes:
        n, r = math.prod(shp), _piece_rows(shp)
        piece = packed[off:off + r]
        out.append((piece if r * PACK_COLS == n else piece.reshape(-1)[:n]).reshape(shp))
        off += r
    return out


def _add(parts, *, name, narrow_copy=False):
    def fn(*ts):
        acc = ts[0].astype(_F32)
        for t in ts[1:]:
            acc = acc + t.astype(_F32)
        return (acc, acc) if narrow_copy else (acc,)

    outs = [(None, parts[0].shape[1], _F32)] + ([(None, parts[0].shape[1], jnp.bfloat16)] if narrow_copy else [])
    res = _rowmap(fn, [_t(a) for a in parts], [], outs, tm=min(PACK_ROW_TILE, parts[0].shape[0]), name=name)
    return res if narrow_copy else res[0]


def _adamw(w, g, m, v, *, name):
    def fn(w_, g_, m_, v_):
        m2 = ADAM_B1 * m_ + (1.0 - ADAM_B1) * g_
        v2 = ADAM_B2 * v_ + (1.0 - ADAM_B2) * (g_ * g_)
        m_hat = m2 / (1.0 - ADAM_B1 ** ADAM_STEP)
        v_hat = v2 / (1.0 - ADAM_B2 ** ADAM_STEP)
        return -ADAM_LR * (m_hat / (jnp.sqrt(v_hat) + ADAM_EPS) + ADAM_WD * w_), m2, v2

    return _rowmap(fn, [_t(w), _t(g), _t(m), _t(v)], [], [(None, w.shape[1], _F32)] * 3, tm=min(PACK_ROW_TILE, w.shape[0]), name=name)


def _step(x, mem, positions, loss_target, w, m, v):
    cx, cy, cc = lax.axis_index("x"), lax.axis_index("y"), lax.axis_index("c")
    chip = 2 * cx + cy
    big = [n for n, _ in _BIG]
    shard_shapes = [w[n].shape for n in big]
    rows = -(-_packed_rows(shard_shapes) // (2 * PACK_ROW_TILE)) * 2 * PACK_ROW_TILE
    half = rows // 2

    w_pack = _pack([w[n] for n in big], rows)
    gathered = _gather_over_chips(w_pack.astype(_BF), name="gather_weights")
    pieces = [_unpack(gathered[k], shard_shapes) for k in range(N_CHIPS)]
    full = {n: jnp.concatenate([pieces[k][i] for k in range(N_CHIPS)], axis=ax) for i, (n, ax) in enumerate(_BIG)}
    for n in _SMALL:
        full[n] = w[n]
    conv = w["a_conv"]
    slots = jnp.stack([jnp.where((chip == k) & (cc == 0), conv, 0.0) for k in range(N_CHIPS)])
    conv_all = _unpack(_all_reduce_small(_pack([slots], _piece_rows(slots.shape)), name="gather_conv"), [slots.shape])[0]
    conv_full = jnp.concatenate([conv_all[k] for k in range(N_CHIPS)], axis=2)

    p = _compute_form(full, conv_full)
    cos_t, sin_t = _rope_tables(positions[0])
    loss_tile, d_x, g = _local_step(x[0], mem[0], cos_t, sin_t, loss_target[0], p)
    gn = _natural_grads(g)

    def shard_of(a, ax, k):
        size = a.shape[ax] // N_CHIPS
        return lax.slice_in_dim(a, k * size, (k + 1) * size, axis=ax)

    g_pack = jnp.stack([_pack([shard_of(gn[n], ax, k) for n, ax in _BIG], rows) for k in range(N_CHIPS)]).reshape(N_CHIPS, 2, half, PACK_COLS)
    keep = lax.dynamic_index_in_dim(g_pack, cc, axis=1, keepdims=False).reshape(N_CHIPS * half, PACK_COLS)
    send = lax.dynamic_index_in_dim(g_pack, 1 - cc, axis=1, keepdims=False).reshape(N_CHIPS * half, PACK_COLS)
    chip_sum, chip_sum_narrow = _add([keep, _pair_exchange(send, name="grad_pair_sum")], name="grad_add_pair", narrow_copy=True)
    chip_sum = chip_sum.reshape(N_CHIPS, half, PACK_COLS)
    from_chips = _scatter_over_chips(chip_sum_narrow.reshape(N_CHIPS, half, PACK_COLS), name="grad_scatter")
    own = lax.dynamic_index_in_dim(chip_sum, chip, axis=0, keepdims=False)
    mine = _add([own, from_chips[0], from_chips[1], from_chips[2]], name="grad_add_chips")
    theirs = _pair_exchange(mine, name="grad_pair_gather")
    g_big = jnp.concatenate([jnp.where(cc == 0, mine, theirs), jnp.where(cc == 0, theirs, mine)], axis=0)

    d_big, m_big, v_big = _adamw(w_pack, g_big, _pack([m[n] for n in big], rows), _pack([v[n] for n in big], rows), name="adamw_big")

    small_shapes = [w[n].shape for n in _SMALL]
    small_rows = _packed_rows(small_shapes)
    g_small = _all_reduce_small(_pack([gn[n] for n in _SMALL], small_rows), name="grad_small")
    d_small, m_small, v_small = _adamw(_pack([w[n] for n in _SMALL], small_rows), g_small, _pack([m[n] for n in _SMALL], small_rows),
                                       _pack([v[n] for n in _SMALL], small_rows), name="adamw_small")

    def by_name(big_pack, small_pack):
        out = dict(zip(big, _unpack(big_pack, shard_shapes)))
        out.update(zip(_SMALL, _unpack(small_pack, small_shapes)))
        return out

    grads, deltas, new_m, new_v = by_name(g_big, g_small), by_name(d_big, d_small), by_name(m_big, m_small), by_name(v_big, v_small)
    loss = lax.psum(loss_tile[0, 0], ("x", "y", "c"))
    return (loss, d_x[None], *[grads[n] for n in _WEIGHTS], *[deltas[n] for n in _WEIGHTS], *[new_m[n] for n in _WEIGHTS],
            *[new_v[n] for n in _WEIGHTS])


def kernel(x, mem, positions, ffn1_norm, ffn1_w_gu, ffn1_w_down, mix_norm, ffn2_norm, ffn2_w_gu, ffn2_w_down, w_out, mem_norm, w_mem_kv, a_w_in, a_conv, a_A_log, a_dt_bias, a_out_norm, b_w_in, b_q_norm, b_w_uq, kv_in_norm, w_dkv, kv_lat_norm, w_ukv, final_norm, loss_target, m_ffn1_norm, m_ffn1_w_gu, m_ffn1_w_down, m_mix_norm, m_ffn2_norm, m_ffn2_w_gu, m_ffn2_w_down, m_w_out, m_mem_norm, m_w_mem_kv, m_a_w_in, m_a_conv, m_a_A_log, m_a_dt_bias, m_a_out_norm, m_b_w_in, m_b_q_norm, m_b_w_uq, m_kv_in_norm, m_w_dkv, m_kv_lat_norm, m_w_ukv, m_final_norm, v_ffn1_norm, v_ffn1_w_gu, v_ffn1_w_down, v_mix_norm, v_ffn2_norm, v_ffn2_w_gu, v_ffn2_w_down, v_w_out, v_mem_norm, v_w_mem_kv, v_a_w_in, v_a_conv, v_a_A_log, v_a_dt_bias, v_a_out_norm, v_b_w_in, v_b_q_norm, v_b_w_uq, v_kv_in_norm, v_w_dkv, v_kv_lat_norm, v_w_ukv, v_final_norm):
    given = dict(locals())
    w = {n: given[n] for n in _WEIGHTS}
    m = {n: given["m_" + n] for n in _WEIGHTS}
    v = {n: given["v_" + n] for n in _WEIGHTS}
    return _step(x, mem, positions, loss_target, w, m, v)
```

```python
import functools
import math

import jax
import jax.numpy as jnp
from jax import lax
from jax.experimental import pallas as pl
from jax.experimental.pallas import tpu as pltpu

_BF = jnp.bfloat16
_F32 = jnp.float32
_HI = lax.Precision.HIGHEST
_MESH = pl.DeviceIdType.MESH

D_MODEL = 1024
DEPTH = 4
N_A = 2
N_B = 2
CHUNK = 64
EPS = 1e-6
HEADS = 6
HEAD_DIM = 128
A_WIDTH = HEADS * HEAD_DIM
CONV_K = 4
QK_ROPE = 64
Q_LORA = 256
KV_LORA = 256
N_MEM = 256
MEM_HEADS = 4
MEM_HEAD_DIM = 64
MEM_WIDTH = MEM_HEADS * MEM_HEAD_DIM
D_FF = 2816
ROPE_THETA = 10000.0
A_IN = 4 * A_WIDTH + 2 * HEADS + MEM_WIDTH
A_IN_PAD = 3456
UQ_PAD = 2 * A_WIDTH
DKV_PAD = KV_LORA + 128
LANE = 128
SUBLANE = 8

ADAM_LR = 0.001
ADAM_B1 = 0.9
ADAM_B2 = 0.999
ADAM_EPS = 1e-08
ADAM_WD = 0.01
ADAM_STEP = 10

N_CHIPS = 4
N_DEV = 8
PACK_COLS = 1024
PACK_ROW_TILE = 256


def _pcall(body, **kw):
    return pl.pallas_call(body, **kw)


VMEM_LIMIT_V7X = 48 * 2 ** 20
TILE_BYTES = 6 * 2 ** 20


def _cp(grid_rank):
    return pltpu.CompilerParams(dimension_semantics=("arbitrary",) * grid_rank, vmem_limit_bytes=VMEM_LIMIT_V7X)


def _fit_rows(rows, row_bytes):
    while rows > LANE and rows * row_bytes > TILE_BYTES:
        rows //= 2
    return rows


def _fit_cols(n, target, col_bytes):
    return _tile(n, max(LANE, min(target, TILE_BYTES // col_bytes)))


def _tile(n, target):
    best = None
    for t in range(LANE, min(n, target) + 1, LANE):
        if n % t == 0:
            best = t
    return best if best is not None else n


def _dot(a, b):
    return jnp.dot(a.astype(_BF), b.astype(_BF), preferred_element_type=_F32)


def _dot_nt(a, b):
    return lax.dot_general(a.astype(_BF), b.astype(_BF), (((1,), (1,)), ((), ())), preferred_element_type=_F32)


def _dot_tn(a, b):
    return lax.dot_general(a.astype(_BF), b.astype(_BF), (((0,), (0,)), ((), ())), preferred_element_type=_F32)


def _dot_hi(a, b):
    return jnp.dot(a, b, precision=_HI, preferred_element_type=_F32)


def _rowmap(fn, tiles, params, outs, accs=(), *, tm, name, carry=(), reverse=False):
    rows = tiles[0][0].shape[0]
    steps = rows // tm
    nt, npar, no, na, nc = len(tiles), len(params), len(outs), len(accs), len(carry)

    def step_index(i):
        return steps - 1 - i if reverse else i

    in_specs, operands = [], []
    for arr, r, w, cb, first in tiles:
        r = tm if r is None else r
        w = arr.shape[1] if w is None else w
        assert arr.shape[0] >= (first + steps) * r and (w % LANE == 0 or w == arr.shape[1]), (name, arr.shape, r, w)
        in_specs.append(pl.BlockSpec((r, w), functools.partial(lambda i, cb, first: (first + step_index(i), cb), cb=cb, first=first)))
        operands.append(arr)
    for p in params:
        in_specs.append(pl.BlockSpec(p.shape, functools.partial(lambda i, nd: (0,) * nd, nd=p.ndim)))
        operands.append(p)
    out_specs, out_shape = [], []
    for r, cols, dt in outs:
        r = tm if r is None else r
        out_specs.append(pl.BlockSpec((r, cols), lambda i: (step_index(i), 0)))
        out_shape.append(jax.ShapeDtypeStruct((steps * r, cols), dt))
    for shp, dt in accs:
        out_specs.append(pl.BlockSpec(shp, functools.partial(lambda i, nd: (0,) * nd, nd=len(shp))))
        out_shape.append(jax.ShapeDtypeStruct(shp, dt))

    def body(*refs):
        t_refs = refs[:nt]
        p_refs = refs[nt:nt + npar]
        o_refs = refs[nt + npar:nt + npar + no]
        a_refs = refs[nt + npar + no:nt + npar + no + na]
        c_refs = refs[nt + npar + no + na:]
        if na or nc:
            @pl.when(pl.program_id(0) == 0)
            def _():
                for r in a_refs + c_refs:
                    r[...] = jnp.zeros(r.shape, r.dtype)
        vals = fn(*[r[...] for r in t_refs], *[r[...] for r in p_refs], *[r[...] for r in c_refs])
        vals = tuple(vals) if isinstance(vals, (tuple, list)) else (vals,)
        assert len(vals) == no + na + nc, (name, len(vals), no, na, nc)
        for r, v in zip(o_refs, vals[:no]):
            r[...] = v.astype(r.dtype)
        for r, v in zip(a_refs, vals[no:no + na]):
            r[...] += v.astype(r.dtype)
        for r, v in zip(c_refs, vals[no + na:]):
            r[...] = v.astype(r.dtype)

    res = _pcall(
        body, name=name, grid=(steps,), in_specs=in_specs, out_specs=out_specs, out_shape=out_shape,
        scratch_shapes=[pltpu.VMEM(shp, dt) for shp, dt in carry],
        compiler_params=_cp(1),
    )(*operands)
    return res


def _t(arr, width=None, cb=0, rows=None, first=0):
    return (arr, rows, width, cb, first)


def _mm_nn(a, b, *, out_dtype, name, scale=None, res=None, tm=1024, tn=1536):
    m, k = a.shape
    n = b.shape[1]
    tm, tn = _fit_rows(min(tm, m), k * a.dtype.itemsize), _fit_cols(n, tn, k * b.dtype.itemsize)

    def body(a_ref, b_ref, *rest):
        acc = _dot(a_ref[...], b_ref[...])
        if scale is not None:
            acc = acc * scale
        if res is not None:
            acc = acc + rest[0][...]
        rest[-1][...] = acc.astype(rest[-1].dtype)

    in_specs = [pl.BlockSpec((tm, k), lambda i, j: (i, 0)), pl.BlockSpec((k, tn), lambda i, j: (0, j))]
    operands = [a, b]
    if res is not None:
        in_specs.append(pl.BlockSpec((tm, tn), lambda i, j: (i, j)))
        operands.append(res)
    return _pcall(
        body, name=name, grid=(m // tm, n // tn), in_specs=in_specs,
        out_specs=pl.BlockSpec((tm, tn), lambda i, j: (i, j)), out_shape=jax.ShapeDtypeStruct((m, n), out_dtype),
        compiler_params=_cp(2),
    )(*operands)


def _mm_nt(a, b, *, out_dtype, name, scale=None, tm=1024, tn=1536):
    m, k = a.shape
    n = b.shape[0]
    tm, tn = _fit_rows(min(tm, m), k * a.dtype.itemsize), _fit_cols(n, tn, k * b.dtype.itemsize)

    def body(a_ref, b_ref, o_ref):
        acc = _dot_nt(a_ref[...], b_ref[...])
        if scale is not None:
            acc = acc * scale
        o_ref[...] = acc.astype(o_ref.dtype)

    return _pcall(
        body, name=name, grid=(m // tm, n // tn),
        in_specs=[pl.BlockSpec((tm, k), lambda i, j: (i, 0)), pl.BlockSpec((tn, k), lambda i, j: (j, 0))],
        out_specs=pl.BlockSpec((tm, tn), lambda i, j: (i, j)), out_shape=jax.ShapeDtypeStruct((m, n), out_dtype),
        compiler_params=_cp(2),
    )(a, b)


def _mm_tn(a, b, *, name, scale=None, t1=1024, tn=1536, ts=1024):
    s, k1 = a.shape
    n = b.shape[1]
    t1, tn, ts = _tile(k1, t1), _tile(n, tn), min(ts, s)
    steps = s // ts

    def body(a_ref, b_ref, o_ref):
        @pl.when(pl.program_id(2) == 0)
        def _():
            o_ref[...] = jnp.zeros(o_ref.shape, o_ref.dtype)

        o_ref[...] += _dot_tn(a_ref[...], b_ref[...])
        if scale is not None:
            @pl.when(pl.program_id(2) == steps - 1)
            def _():
                o_ref[...] = o_ref[...] * scale

    return _pcall(
        body, name=name, grid=(k1 // t1, n // tn, steps),
        in_specs=[pl.BlockSpec((ts, t1), lambda i, j, r: (r, i)), pl.BlockSpec((ts, tn), lambda i, j, r: (r, j))],
        out_specs=pl.BlockSpec((t1, tn), lambda i, j, r: (i, j)), out_shape=jax.ShapeDtypeStruct((k1, n), _F32),
        compiler_params=_cp(3),
    )(a, b)


def _heads(t, n, w=LANE):
    return [t[:, w * h:w * (h + 1)] for h in range(n)]


def _cat(parts):
    return jnp.concatenate(parts, axis=1)


def _rms(x, g):
    return x * lax.rsqrt(jnp.mean(x * x, axis=-1, keepdims=True) + EPS) * g


def _l2n(x):
    return x * lax.rsqrt(jnp.sum(x * x, axis=-1, keepdims=True) + EPS)


def _sigmoid(x):
    return 0.5 * (jnp.tanh(0.5 * x) + 1.0)


def _silu(x):
    return x * _sigmoid(x)


def _softplus(x):
    return jnp.maximum(x, 0.0) + jnp.log(1.0 + jnp.exp(-jnp.abs(x)))


def _lane_pick(t, h):
    lane = lax.broadcasted_iota(jnp.int32, t.shape, 1)
    return jnp.sum(jnp.where(lane == h, t, 0.0), axis=1, keepdims=True)


def _lane_put(col, h, width=LANE):
    lane = lax.broadcasted_iota(jnp.int32, (col.shape[0], width), 1)
    return jnp.where(lane == h, col, 0.0)


def _vjp(fwd, ins, cts):
    outs, pull = jax.vjp(fwd, *ins)
    outs = outs if isinstance(outs, (tuple, list)) else (outs,)
    cts = tuple(c.astype(o.dtype) for c, o in zip(cts, outs))
    return pull(cts if len(cts) > 1 else cts[0])


def _rot_half_matrix():
    r = lax.broadcasted_iota(jnp.int32, (LANE, LANE), 0)
    c = lax.broadcasted_iota(jnp.int32, (LANE, LANE), 1)
    half = QK_ROPE // 2
    return jnp.where((c < half) & (r == c + half), -1.0, jnp.where((c >= half) & (c < QK_ROPE) & (r == c - half), 1.0, 0.0))


def _rope(x, cos_t, sin_t):
    return x * cos_t + _dot_hi(x, _rot_half_matrix()) * sin_t


def _mem_attn(q, km, vm):
    lane_q = lax.broadcasted_iota(jnp.int32, q.shape, 1)
    lane_v = lax.broadcasted_iota(jnp.int32, vm.shape, 1)
    out = jnp.zeros(q.shape, _F32)
    for h in range(MEM_HEADS):
        lo, hi = MEM_HEAD_DIM * h, MEM_HEAD_DIM * (h + 1)
        qh = jnp.where((lane_q >= lo) & (lane_q < hi), q, 0.0)
        vh = jnp.where((lane_v >= lo) & (lane_v < hi), vm, 0.0)
        sc = _dot_nt(qh, km) * MEM_HEAD_DIM ** -0.5
        sc = sc - lax.stop_gradient(jnp.max(sc, axis=-1, keepdims=True))
        p = jnp.exp(sc)
        p = p / jnp.sum(p, axis=-1, keepdims=True)
        out = out + _dot(p, vh)
    return out


_PAIR = 2 * CHUNK


def _pair_masks():
    ri = lax.broadcasted_iota(jnp.int32, (_PAIR, _PAIR), 0)
    ci = lax.broadcasted_iota(jnp.int32, (_PAIR, _PAIR), 1)
    same = (ri >= CHUNK) == (ci >= CHUNK)
    return same, same & (ri >= ci), same & (ri > ci), ri == ci, same & (ri <= ci)


_NN = (((2,), (1,)), ((0,), (0,)))
_NT = (((2,), (2,)), ((0,), (0,)))
_TN = (((1,), (1,)), ((0,), (0,)))


def _bdot(a, b, dims):
    return lax.dot_general(a.astype(_BF), b.astype(_BF), dims, preferred_element_type=_F32)


def _dot3(a, b, dims):
    a_hi, b_hi = a.astype(_BF), b.astype(_BF)
    a_lo, b_lo = (a - a_hi.astype(_F32)).astype(_BF), (b - b_hi.astype(_F32)).astype(_BF)
    d = lambda x, y: lax.dot_general(x, y, dims, preferred_element_type=_F32)
    return d(a_hi, b_hi) + (d(a_hi, b_lo) + d(a_lo, b_hi))


@jax.custom_vjp
def _mm3(a, b):
    return _dot3(a, b, _NN)


_mm3.defvjp(lambda a, b: (_dot3(a, b, _NN), (a, b)), lambda res, g: (_dot3(g, res[1], _NT), _dot3(res[0], g, _TN)))


def _neumann_inverse(a):
    eye = jnp.where(_pair_masks()[3], 1.0, 0.0)
    n = -a
    t_inv = eye + n
    for _ in range(5):
        n = _dot3(n, n, _NN)
        t_inv = t_inv + _dot3(t_inv, n, _NN)
    return t_inv


@jax.custom_vjp
def _unit_lower_inverse(a):
    return _neumann_inverse(a)


def _unit_lower_inverse_fwd(a):
    t_inv = _neumann_inverse(a)
    return t_inv, t_inv


def _unit_lower_inverse_bwd(t_inv, g):
    return (-_dot3(t_inv, _dot3(g, t_inv, _NT), _TN),)


_unit_lower_inverse.defvjp(_unit_lower_inverse_fwd, _unit_lower_inverse_bwd)


def _gdn_intra_head(q, k, v, beta, gl):
    same, causal, strict, eye, upper = _pair_masks()
    gl_row = jnp.sum(jnp.where(eye, gl, 0.0), axis=-2, keepdims=True)
    g_col = jnp.sum(jnp.where(causal, gl_row, 0.0), axis=-1, keepdims=True)
    g_row = jnp.sum(jnp.where(upper, gl, 0.0), axis=-2, keepdims=True)
    g_last = jnp.sum(jnp.where(same, gl_row, 0.0), axis=-1, keepdims=True)
    decay = jnp.where(causal, jnp.exp(jnp.where(causal, g_col - g_row, 0.0)), 0.0)
    kb = k * beta
    a = jnp.where(strict, _bdot(kb, k, _NT) * decay, 0.0)
    t_inv = _unit_lower_inverse(a)
    e_g = jnp.exp(g_col)
    u = _mm3(t_inv, v * beta)
    w = _mm3(t_inv, kb * e_g)
    qk = _bdot(q, k, _NT) * decay
    return w, u, q * e_g, k * jnp.exp(g_last - g_col), qk, jnp.exp(g_last)


def _gdn_scan_head(s, qd_a, kd_a, w_a, u_a, qk_a, dc_a, qd_b, kd_b, w_b, u_b, qk_b, dc_b):
    zeros = jnp.zeros((HEADS, CHUNK, HEAD_DIM), _F32)
    vn_a = u_a - _bdot(w_a, s, _NN)
    o_a = _bdot(qd_a, s, _NN) + _bdot(qk_a, jnp.concatenate([vn_a, zeros], axis=1), _NN)
    s1 = s * dc_a + _bdot(kd_a, vn_a, _TN)
    vn_b = u_b - _bdot(w_b, s1, _NN)
    o_b = _bdot(qd_b, s1, _NN) + _bdot(qk_b, jnp.concatenate([zeros, vn_b], axis=1), _NN)
    s2 = s1 * dc_b + _bdot(kd_b, vn_b, _TN)
    return o_a, o_b, s2


def _pick_scalar(t, row, lane_i):
    ri = lax.broadcasted_iota(jnp.int32, t.shape, 0)
    ci = lax.broadcasted_iota(jnp.int32, t.shape, 1)
    return jnp.sum(jnp.sum(jnp.where((ri == row) & (ci == lane_i), t, 0.0), axis=1, keepdims=True), axis=0, keepdims=True)


def _put_scalar(val, row, lane_i, shape):
    ri = lax.broadcasted_iota(jnp.int32, shape, 0)
    ci = lax.broadcasted_iota(jnp.int32, shape, 1)
    return jnp.where((ri == row) & (ci == lane_i), val, 0.0)


def _by_head(t):
    return jnp.stack(_heads(t, HEADS))


def _from_heads(t):
    return _cat([t[h] for h in range(HEADS)])


def _state_by_head(s):
    return jnp.stack([s[HEAD_DIM * h:HEAD_DIM * (h + 1), :] for h in range(HEADS)])


def _scan_ins(qd, kd, w, u, qk, dcrow, state):
    ins = [_state_by_head(state)]
    for r0 in (0, CHUNK):
        rs = slice(r0, r0 + CHUNK)
        ins += [_by_head(t[rs, :]) for t in (qd, kd, w, u, qk)]
        ins.append(jnp.stack([_pick_scalar(dcrow, r0, h) for h in range(HEADS)]))
    return ins


def _gdn_scan_fwd_fn(qd, kd, w, u, qk, dcrow, state):
    o_a, o_b, s2 = _gdn_scan_head(*_scan_ins(qd, kd, w, u, qk, dcrow, state))
    return jnp.concatenate([_from_heads(o_a), _from_heads(o_b)], axis=0), state, s2.reshape(state.shape)


def _gdn_scan_bwd_fn(qd, kd, w, u, qk, dcrow, state, d_o, d_state):
    cts = (_by_head(d_o[0:CHUNK, :]), _by_head(d_o[CHUNK:_PAIR, :]), _state_by_head(d_state))
    g = _vjp(_gdn_scan_head, _scan_ins(qd, kd, w, u, qk, dcrow, state), cts)
    grads = tuple(jnp.concatenate([_from_heads(g[1 + t]), _from_heads(g[7 + t])], axis=0) for t in range(5))
    d_dcrow = sum(_put_scalar(g[6][h], 0, h, dcrow.shape) + _put_scalar(g[12][h], CHUNK, h, dcrow.shape) for h in range(HEADS))
    return grads + (d_dcrow, g[0].reshape(state.shape))


def _gdn_intra_ins(q, k, v, bg):
    return [_by_head(q), _by_head(k), _by_head(v), jnp.stack([_lane_pick(bg, h) for h in range(HEADS)]),
            jnp.stack([_lane_pick(bg, HEADS + h) for h in range(HEADS)])]


def _gdn_intra_fwd_fn(q, k, v, bg):
    res = _gdn_intra_head(*_gdn_intra_ins(q, k, v, bg))
    dcrow = sum(_lane_put(res[5][h], h) for h in range(HEADS))
    return tuple(_from_heads(r) for r in res[:5]) + (dcrow,)


def _gdn_intra_bwd_fn(q, k, v, bg, d_w, d_u, d_qd, d_kd, d_qk, d_dcrow):
    cts = tuple(_by_head(d) for d in (d_w, d_u, d_qd, d_kd, d_qk)) + (jnp.stack([_lane_pick(d_dcrow, h) for h in range(HEADS)]),)
    g = _vjp(_gdn_intra_head, _gdn_intra_ins(q, k, v, bg), cts)
    d_bg = sum(_lane_put(g[3][h], h) + _lane_put(g[4][h], HEADS + h) for h in range(HEADS))
    return tuple(_from_heads(g[t]) for t in range(3)) + (d_bg,)


def _gdn_gates(ba, alog, dtb):
    lane = lax.broadcasted_iota(jnp.int32, ba.shape, 1)
    beta = _sigmoid(ba)
    g = -jnp.exp(alog) * _softplus(ba + dtb)
    return jnp.where(lane < HEADS, beta, jnp.where(lane < 2 * HEADS, g, 0.0))


def _gdn_q_head(c):
    return _l2n(_silu(c)) * HEAD_DIM ** -0.5


def _gdn_k_head(c):
    return _l2n(_silu(c))


def _gdn_prep_fwd_fn(qkv_c, ba, alog, dtb):
    hs = _heads(qkv_c, 3 * HEADS)
    q = _cat([_gdn_q_head(c) for c in hs[:HEADS]])
    k = _cat([_gdn_k_head(c) for c in hs[HEADS:2 * HEADS]])
    v = _cat([_silu(c) for c in hs[2 * HEADS:]])
    return q, k, v, _gdn_gates(ba, alog, dtb)


def _gdn_prep_bwd_fn(qkv_c, ba, d_q, d_k, d_v, d_bg, alog, dtb):
    hs = _heads(qkv_c, 3 * HEADS)
    dqs, dks, dvs = _heads(d_q, HEADS), _heads(d_k, HEADS), _heads(d_v, HEADS)
    parts = [_vjp(_gdn_q_head, [hs[h]], (dqs[h],))[0] for h in range(HEADS)]
    parts += [_vjp(_gdn_k_head, [hs[HEADS + h]], (dks[h],))[0] for h in range(HEADS)]
    parts += [_vjp(_silu, [hs[2 * HEADS + h]], (dvs[h],))[0] for h in range(HEADS)]
    d_ba, d_alog, d_dtb = _vjp(_gdn_gates, [ba, alog, dtb], (d_bg,))
    return _cat(parts), d_ba, d_alog, d_dtb


def _a_out_head(o, gate, gain):
    return _rms(o, gain) * _silu(gate)


def _a_post_fwd_fn(o, gate, qm, gain, mem_kv):
    parts = [_a_out_head(oh, gh, gain) for oh, gh in zip(_heads(o, HEADS), _heads(gate, HEADS))]
    parts.append(_mem_attn(qm, mem_kv[:, :MEM_WIDTH], mem_kv[:, MEM_WIDTH:]))
    return (_cat(parts),)


def _a_post_bwd_fn(o, gate, qm, d_cat, gain, mem_kv):
    d_os, d_gates = [], []
    d_gain = jnp.zeros(gain.shape, _F32)
    dc = _heads(d_cat, HEADS + 2)
    for h, (oh, gh) in enumerate(zip(_heads(o, HEADS), _heads(gate, HEADS))):
        g = _vjp(_a_out_head, [oh, gh, gain], (dc[h],))
        d_os.append(g[0])
        d_gates.append(g[1])
        d_gain = d_gain + g[2]
    d_qm, d_km, d_vm = _vjp(_mem_attn, [qm, mem_kv[:, :MEM_WIDTH], mem_kv[:, MEM_WIDTH:]], (d_cat[:, A_WIDTH:],))
    return _cat(d_os), _cat(d_gates + [d_qm]), d_gain, _cat([d_km, d_vm])


def _b_post_fwd_fn(o, qm, mem_kv):
    return (_cat([o.astype(_F32), _mem_attn(qm, mem_kv[:, :MEM_WIDTH], mem_kv[:, MEM_WIDTH:])]),)


def _b_post_bwd_fn(qm, d_cat_m, mem_kv):
    d_qm, d_km, d_vm = _vjp(_mem_attn, [qm, mem_kv[:, :MEM_WIDTH], mem_kv[:, MEM_WIDTH:]], (d_cat_m,))
    return d_qm, _cat([d_km, d_vm])


ATTN_SCALE = (HEAD_DIM + QK_ROPE) ** -0.5


def _rope_q_fwd_fn(qf, cos_t, sin_t):
    hs = _heads(qf, 2 * HEADS)
    return (_cat(hs[:HEADS] + [_rope(x, cos_t, sin_t) for x in hs[HEADS:]]) * ATTN_SCALE,)


def _rope_q_bwd_fn(d_qn, d_qr, cos_t, sin_t):
    f = lambda x: _rope(x, cos_t, sin_t)
    return (_cat([d_qn] + [_vjp(f, [x], (x,))[0] for x in _heads(d_qr, HEADS)]) * ATTN_SCALE,)


def _kv_prep_fwd_fn(ckr, cos_t, sin_t, gain):
    return _rms(ckr[:, :KV_LORA], gain), _rope(ckr[:, KV_LORA:], cos_t, sin_t)


def _kv_prep_bwd_fn(ckr, d_ckv, d_kr, cos_t, sin_t, gain):
    d_lat, d_gain = _vjp(_rms, [ckr[:, :KV_LORA], gain], (d_ckv,))
    f = lambda x: _rope(x, cos_t, sin_t)
    d_rope = _vjp(f, [ckr[:, KV_LORA:]], (d_kr,))[0]
    return _cat([d_lat, d_rope]), d_gain


def _rms_fwd(x, gain, *, name, tm=1024, out_dtype=_BF):
    tm = min(tm, x.shape[0])
    return _rowmap(lambda x_, g_: (_rms(x_.astype(_F32), g_),), [_t(x)], [gain], [(None, x.shape[1], out_dtype)], tm=tm, name=name)[0]


def _rms_bwd(x, d_xn, d_res, gain, *, name, tm=512):
    tm = min(tm, x.shape[0])

    def fn(x_, dxn_, *rest):
        g_ = rest[-1]
        dx, dg = _vjp(_rms, [x_.astype(_F32), g_], (dxn_.astype(_F32),))
        if d_res is not None:
            dx = dx + rest[0]
        return dx, dg

    tiles = [_t(x), _t(d_xn)] + ([_t(d_res)] if d_res is not None else [])
    return _rowmap(fn, tiles, [gain], [(None, x.shape[1], _F32)], [(gain.shape, _F32)], tm=tm, name=name)


FFN_COL_TILE = 1408
FFN_ROW_TILE = 512
FFN_DXN_ROW_TILE = 256


def _ffn_gate_up(x, gain, w_gu, idx, *, name):
    s = x.shape[0]
    tm, tf = min(FFN_ROW_TILE, s), FFN_COL_TILE
    nf = D_FF // tf

    def body(x_ref, gain_ref, wg_ref, wu_ref, xn_ref, g_ref, u_ref, a_ref):
        @pl.when(pl.program_id(1) == 0)
        def _():
            xn_ref[...] = _rms(x_ref[...], gain_ref[...]).astype(xn_ref.dtype)

        xn = xn_ref[...]
        g, u = _dot(xn, wg_ref[...]), _dot(xn, wu_ref[...])
        g_ref[...] = g.astype(g_ref.dtype)
        u_ref[...] = u.astype(u_ref.dtype)
        a_ref[...] = (_silu(g) * u).astype(a_ref.dtype)

    col = pl.BlockSpec((tm, tf), lambda i, j: (i, j))
    wide = jax.ShapeDtypeStruct((s, D_FF), _BF)
    return _pcall(
        body, name=name, grid=(s // tm, nf),
        in_specs=[pl.BlockSpec((tm, D_MODEL), lambda i, j: (i, 0)), pl.BlockSpec((1, D_MODEL), lambda i, j: (0, 0)),
                  pl.BlockSpec((None, D_MODEL, tf), lambda i, j: (j, idx, 0)), pl.BlockSpec((None, D_MODEL, tf), lambda i, j: (nf + j, idx, 0))],
        out_specs=[pl.BlockSpec((tm, D_MODEL), lambda i, j: (i, 0)), col, col, col],
        out_shape=[jax.ShapeDtypeStruct((s, D_MODEL), _BF), wide, wide, wide],
        compiler_params=_cp(2),
    )(x, gain, w_gu, w_gu)


FFN_SHARD_ROWS = D_FF // N_CHIPS


def _w_down_specs(idx):
    return [pl.BlockSpec((None, FFN_SHARD_ROWS, D_MODEL), functools.partial(lambda i, j, q: (2 * j + q, idx, 0), q=q)) for q in (0, 1)]


def _ffn_down(a, w_down, idx, x, *, name):
    s = a.shape[0]
    tm, tf = min(FFN_ROW_TILE, s), FFN_COL_TILE

    def body(a_ref, wa_ref, wb_ref, x_ref, o_ref):
        part = 0.5 * _dot(a_ref[...], jnp.concatenate([wa_ref[...], wb_ref[...]], axis=0))

        @pl.when(pl.program_id(1) == 0)
        def _():
            o_ref[...] = x_ref[...] + part

        @pl.when(pl.program_id(1) > 0)
        def _():
            o_ref[...] += part

    rows = pl.BlockSpec((tm, D_MODEL), lambda i, j: (i, 0))
    return _pcall(
        body, name=name, grid=(s // tm, D_FF // tf),
        in_specs=[pl.BlockSpec((tm, tf), lambda i, j: (i, j))] + _w_down_specs(idx) + [rows],
        out_specs=rows, out_shape=jax.ShapeDtypeStruct((s, D_MODEL), _F32), compiler_params=_cp(2),
    )(a, w_down, w_down, x)


def _ffn_fwd(x, gain, w_gu, w_down, idx, *, name):
    xn, g, u, a = _ffn_gate_up(x, gain, w_gu, idx, name=name + "_gu")
    y = _ffn_down(a, w_down, idx, x, name=name + "_down")
    return y, (x, xn, g, u, a)


def _ffn_d_gate_up(d_y, g, u, w_down, idx, *, name):
    s = d_y.shape[0]
    tm, tf = min(FFN_ROW_TILE, s), FFN_COL_TILE

    def body(dy_ref, wa_ref, wb_ref, g_ref, u_ref, dg_ref, du_ref):
        da = _dot_nt(dy_ref[...], jnp.concatenate([wa_ref[...], wb_ref[...]], axis=0)) * 0.5
        gg, uu = g_ref[...].astype(_F32), u_ref[...].astype(_F32)
        sg = _sigmoid(gg)
        dg_ref[...] = (da * uu * sg * (1.0 + gg * (1.0 - sg))).astype(dg_ref.dtype)
        du_ref[...] = (da * gg * sg).astype(du_ref.dtype)

    col = pl.BlockSpec((tm, tf), lambda i, j: (i, j))
    wide = jax.ShapeDtypeStruct((s, D_FF), _BF)
    return _pcall(
        body, name=name, grid=(s // tm, D_FF // tf),
        in_specs=[pl.BlockSpec((tm, D_MODEL), lambda i, j: (i, 0))] + _w_down_specs(idx) + [col, col],
        out_specs=[col, col], out_shape=[wide, wide], compiler_params=_cp(2),
    )(d_y, w_down, w_down, g, u)


def _ffn_d_x(d_g, d_u, w_gu, idx, x, d_y, gain, *, name):
    s = x.shape[0]
    tm, tf = min(FFN_DXN_ROW_TILE, s), FFN_COL_TILE

    def body(dg_ref, du_ref, w0_ref, w1_ref, w2_ref, w3_ref, x_ref, dy_ref, gain_ref, dx_ref, dgain_ref):
        @pl.when(pl.program_id(0) == 0)
        def _():
            dgain_ref[...] = jnp.zeros(dgain_ref.shape, _F32)

        d_xn = (_dot_nt(dg_ref[:, 0:tf], w0_ref[...]) + _dot_nt(dg_ref[:, tf:2 * tf], w1_ref[...])
                + _dot_nt(du_ref[:, 0:tf], w2_ref[...]) + _dot_nt(du_ref[:, tf:2 * tf], w3_ref[...]))
        dx, dgain = _vjp(_rms, [x_ref[...], gain_ref[...]], (d_xn,))
        dx_ref[...] = dx + dy_ref[...]
        dgain_ref[...] += dgain

    wide = pl.BlockSpec((tm, D_FF), lambda i: (i, 0))
    rows = pl.BlockSpec((tm, D_MODEL), lambda i: (i, 0))
    one = pl.BlockSpec((1, D_MODEL), lambda i: (0, 0))
    w_specs = [pl.BlockSpec((None, D_MODEL, tf), functools.partial(lambda i, k: (k, idx, 0), k=k)) for k in range(N_CHIPS)]
    return _pcall(
        body, name=name, grid=(s // tm,),
        in_specs=[wide, wide] + w_specs + [rows, rows, one],
        out_specs=[rows, one], out_shape=[jax.ShapeDtypeStruct((s, D_MODEL), _F32), jax.ShapeDtypeStruct((1, D_MODEL), _F32)],
        compiler_params=_cp(1),
    )(d_g, d_u, w_gu, w_gu, w_gu, w_gu, x, d_y, gain)


def _ffn_d_w_gu(xn, d_act, into, idx, first_chip, *, name, ts=1024):
    s = xn.shape[0]
    ts = min(ts, s)
    steps = s // ts

    def body(a_ref, b_ref, into_ref, o_ref):
        @pl.when(pl.program_id(1) == 0)
        def _():
            o_ref[...] = jnp.zeros(o_ref.shape, o_ref.dtype)

        o_ref[...] += _dot_tn(a_ref[...], b_ref[...])

    return _pcall(
        body, name=name, grid=(D_FF // FFN_COL_TILE, steps),
        in_specs=[pl.BlockSpec((ts, D_MODEL), lambda j, r: (r, 0)), pl.BlockSpec((ts, FFN_COL_TILE), lambda j, r: (r, j)),
                  pl.BlockSpec(memory_space=pl.ANY)],
        out_specs=pl.BlockSpec((None, None, D_MODEL, FFN_COL_TILE), lambda j, r: (first_chip + j, idx, 0, 0)),
        out_shape=jax.ShapeDtypeStruct(into.shape, into.dtype), input_output_aliases={2: 0}, compiler_params=_cp(2),
    )(xn, d_act, into)


def _ffn_d_w_down(a, d_y, into, idx, *, name, ts=1024):
    s = a.shape[0]
    ts = min(ts, s)
    steps = s // ts

    def body(a_ref, b_ref, into_ref, o_ref):
        @pl.when(pl.program_id(1) == 0)
        def _():
            o_ref[...] = jnp.zeros(o_ref.shape, o_ref.dtype)

        part = _dot_tn(a_ref[...], b_ref[...]) * 0.5
        o_ref[0] += part[0:FFN_SHARD_ROWS, :]
        o_ref[1] += part[FFN_SHARD_ROWS:2 * FFN_SHARD_ROWS, :]

    return _pcall(
        body, name=name, grid=(D_FF // FFN_COL_TILE, steps),
        in_specs=[pl.BlockSpec((ts, FFN_COL_TILE), lambda i, r: (r, i)), pl.BlockSpec((ts, D_MODEL), lambda i, r: (r, 0)),
                  pl.BlockSpec(memory_space=pl.ANY)],
        out_specs=pl.BlockSpec((2, None, FFN_SHARD_ROWS, D_MODEL), lambda i, r: (i, idx, 0, 0)),
        out_shape=jax.ShapeDtypeStruct(into.shape, into.dtype), input_output_aliases={2: 0}, compiler_params=_cp(2),
    )(a, d_y, into)


def _ffn_bwd(d_y, saved, gain, w_gu, w_down, idx, g_gu, g_down, *, name):
    x, xn, g, u, a = saved
    d_g, d_u = _ffn_d_gate_up(d_y, g, u, w_down, idx, name=name + "_dgu")
    g_down = _ffn_d_w_down(a, d_y, g_down, idx, name=name + "_dwd")
    g_gu = _ffn_d_w_gu(xn, d_g, g_gu, idx, 0, name=name + "_dwg")
    g_gu = _ffn_d_w_gu(xn, d_u, g_gu, idx, 2, name=name + "_dwu")
    d_x, d_gain = _ffn_d_x(d_g, d_u, w_gu, idx, x, d_y, gain, name=name + "_dx")
    return d_x, d_gain, g_gu, g_down


def _conv_fwd(h, w, *, name, tm=256):
    s = h.shape[0]
    tm = min(tm, s)
    c = 3 * A_WIDTH
    halo = SUBLANE

    def body(x_ref, prev_ref, w_ref, o_ref, buf):
        i = pl.program_id(0)
        buf[0:halo, :] = jnp.where(i == 0, 0.0, prev_ref[...])
        buf[halo:halo + tm, :] = x_ref[...]
        acc = jnp.zeros((tm, c), _F32)
        for j in range(CONV_K):
            acc = acc + buf[pl.ds(halo - (CONV_K - 1) + j, tm), :] * w_ref[j:j + 1, :]
        o_ref[...] = acc

    return _pcall(
        body, name=name, grid=(s // tm,),
        in_specs=[pl.BlockSpec((tm, c), lambda i: (i, 0)),
                  pl.BlockSpec((halo, c), lambda i: (jnp.maximum(i * (tm // halo) - 1, 0), 0)),
                  pl.BlockSpec(w.shape, lambda i: (0, 0))],
        out_specs=pl.BlockSpec((tm, c), lambda i: (i, 0)), out_shape=jax.ShapeDtypeStruct((s, c), _F32),
        scratch_shapes=[pltpu.VMEM((tm + 2 * halo, c), _F32)],
        compiler_params=_cp(1),
    )(h, h, w)


def _conv_bwd(h, d_y, w, *, name, tm=256):
    s = h.shape[0]
    tm = min(tm, s)
    c = 3 * A_WIDTH
    halo = SUBLANE
    steps = s // tm

    def body(x_ref, prev_ref, dy_ref, next_ref, w_ref, dx_ref, dw_ref, xbuf, dybuf):
        i = pl.program_id(0)

        @pl.when(i == 0)
        def _():
            dw_ref[...] = jnp.zeros(dw_ref.shape, dw_ref.dtype)

        xbuf[0:halo, :] = jnp.where(i == 0, 0.0, prev_ref[...])
        xbuf[halo:halo + tm, :] = x_ref[...]
        dybuf[0:tm, :] = dy_ref[...]
        dybuf[tm:tm + halo, :] = jnp.where(i == steps - 1, 0.0, next_ref[...])
        dy = dy_ref[...]
        acc = jnp.zeros((tm, c), _F32)
        for j in range(CONV_K):
            acc = acc + dybuf[pl.ds(CONV_K - 1 - j, tm), :] * w_ref[j:j + 1, :]
            dw_ref[j:j + 1, :] += jnp.sum(dy * xbuf[pl.ds(halo - (CONV_K - 1) + j, tm), :], axis=0, keepdims=True)
        dx_ref[...] = acc

    return _pcall(
        body, name=name, grid=(steps,),
        in_specs=[pl.BlockSpec((tm, c), lambda i: (i, 0)),
                  pl.BlockSpec((halo, c), lambda i: (jnp.maximum(i * (tm // halo) - 1, 0), 0)),
                  pl.BlockSpec((tm, c), lambda i: (i, 0)),
                  pl.BlockSpec((halo, c), lambda i: (jnp.minimum((i + 1) * (tm // halo), s // halo - 1), 0)),
                  pl.BlockSpec(w.shape, lambda i: (0, 0))],
        out_specs=[pl.BlockSpec((tm, c), lambda i: (i, 0)), pl.BlockSpec(w.shape, lambda i: (0, 0))],
        out_shape=[jax.ShapeDtypeStruct((s, c), _F32), jax.ShapeDtypeStruct(w.shape, _F32)],
        scratch_shapes=[pltpu.VMEM((tm + 2 * halo, c), _F32), pltpu.VMEM((tm + 2 * halo, c), _F32)],
        compiler_params=_cp(1),
    )(h, h, d_y, d_y, w)


ATTN_Q_BLOCK = 2048
ATTN_K_SUB = 256
ATTN_DQ_BLOCK = 2048
ATTN_BWD_BLOCK = 1024
ATTN_BWD_SUB = 512


def _chunk_mask(shape, q_axis):
    qi = lax.broadcasted_iota(jnp.int32, shape, q_axis) // CHUNK
    ki = lax.broadcasted_iota(jnp.int32, shape, 1 - q_axis) // CHUNK
    return ki <= qi


def _rows(j, t):
    return pl.ds(pl.multiple_of(j * t, t), t)


def _chunk_mask_at(shape, q_axis, q_off):
    qi = (lax.broadcasted_iota(jnp.int32, shape, q_axis) + q_off) // CHUNK
    ki = lax.broadcasted_iota(jnp.int32, shape, 1 - q_axis) // CHUNK
    return ki <= qi


def _attn_fwd(q_all, kv, kr, *, name):
    s = q_all.shape[0]
    t = min(ATTN_Q_BLOCK, s)
    tk = min(ATTN_K_SUB, t)
    nq, sub, rep = s // t, t // tk, tk // LANE

    def body(qn_ref, qr_ref, kn_ref, kr_ref, v_ref, o_ref, lse_ref, m_sc, acc_sc):
        i = pl.program_id(1)
        m_sc[...] = jnp.full(m_sc.shape, -1e30, _F32)
        acc_sc[...] = jnp.zeros(acc_sc.shape, _F32)
        ones = jnp.ones((tk, LANE), _BF)

        def block(j, first_row):
            qs = slice(first_row, t)
            rows = _rows(j, tk)
            q = _cat([qn_ref[qs, :], qr_ref[qs, :]])
            sc = _dot_nt(q, _cat([kn_ref[rows, :], kr_ref[rows, :]]))
            if first_row is not None:
                sc = jnp.where(_chunk_mask(sc.shape, 0), sc, -1e30)
            m_prev = m_sc[qs, :]
            m_new = jnp.maximum(m_prev, jnp.max(sc, axis=-1, keepdims=True))
            alpha = jnp.exp(m_prev - m_new)
            p = jnp.exp(sc - _cat([m_new] * rep))
            acc_sc[qs, :] = _cat([alpha, alpha]) * acc_sc[qs, :] + _dot(p, _cat([v_ref[rows, :], ones]))
            m_sc[qs, :] = m_new

        def step(j, carry):
            block(j, None)
            return carry

        lax.fori_loop(0, i * sub, step, 0)
        for u in range(sub):
            block(i * sub + u, u * tk)
        row_sum = acc_sc[:, LANE:2 * LANE]
        o_ref[...] = acc_sc[:, 0:LANE] / row_sum
        lse_ref[...] = m_sc[...] + jnp.log(row_sum)

    return _pcall(
        body, name=name, grid=(HEADS, nq),
        in_specs=[pl.BlockSpec((t, LANE), lambda h, i: (i, h)),
                  pl.BlockSpec((t, LANE), lambda h, i: (i, HEADS + h)),
                  pl.BlockSpec((s, LANE), lambda h, i: (0, h)),
                  pl.BlockSpec((s, LANE), lambda h, i: (0, 0)),
                  pl.BlockSpec((s, LANE), lambda h, i: (0, HEADS + h))],
        out_specs=[pl.BlockSpec((t, LANE), lambda h, i: (i, h)), pl.BlockSpec((t, LANE), lambda h, i: (i, h))],
        out_shape=[jax.ShapeDtypeStruct((s, A_WIDTH), _F32), jax.ShapeDtypeStruct((s, A_WIDTH), _F32)],
        scratch_shapes=[pltpu.VMEM((t, LANE), _F32), pltpu.VMEM((t, 2 * LANE), _F32)],
        compiler_params=_cp(2),
    )(q_all, q_all, kv, kr, kv)


def _attn_bwd_prep(o, lse, d_cat, *, name):
    s = o.shape[0]
    t = min(ATTN_BWD_SUB, s)
    nq = s // t

    def body(o_ref, lse_ref, do_ref, dob_ref, dl_ref, lset_ref, dlt_ref):
        for h in range(HEADS):
            sl = slice(LANE * h, LANE * (h + 1))
            rows = slice(SUBLANE * h, SUBLANE * (h + 1))
            do = do_ref[:, sl]
            dl = jnp.broadcast_to(jnp.sum(o_ref[:, sl] * do, axis=-1, keepdims=True), (t, LANE))
            dob_ref[:, sl] = do.astype(dob_ref.dtype)
            dl_ref[:, sl] = dl
            dlt_ref[rows, :] = dl.T[0:SUBLANE, :]
            lset_ref[rows, :] = lse_ref[:, sl].T[0:SUBLANE, :]

    wide = pl.BlockSpec((t, A_WIDTH), lambda i: (i, 0))
    stat = pl.BlockSpec((HEADS * SUBLANE, t), lambda i: (i, 0))
    stat_shape = jax.ShapeDtypeStruct((nq * HEADS * SUBLANE, t), _F32)
    return _pcall(
        body, name=name, grid=(nq,), in_specs=[wide, wide, wide], out_specs=[wide, wide, stat, stat],
        out_shape=[jax.ShapeDtypeStruct((s, A_WIDTH), _BF), jax.ShapeDtypeStruct((s, A_WIDTH), _F32), stat_shape, stat_shape],
        compiler_params=_cp(1),
    )(o, lse, d_cat)


def _attn_dq(q_all, kv, kr, lse, delta, d_o, *, name):
    s = q_all.shape[0]
    t = min(ATTN_DQ_BLOCK, s)
    tk = min(ATTN_BWD_SUB, t)
    nq, sub, rep = s // t, t // tk, tk // LANE

    def body(qn_ref, qr_ref, kn_ref, kr_ref, v_ref, lse_ref, dl_ref, do_ref, dqn_ref, dqr_ref, acc_sc):
        i = pl.program_id(1)
        acc_sc[...] = jnp.zeros(acc_sc.shape, _F32)

        def block(j, first_row):
            qs = slice(first_row, t)
            rows = _rows(j, tk)
            q = _cat([qn_ref[qs, :], qr_ref[qs, :]])
            k = _cat([kn_ref[rows, :], kr_ref[rows, :]])
            p = jnp.exp(_dot_nt(q, k) - _cat([lse_ref[qs, :]] * rep))
            if first_row is not None:
                p = jnp.where(_chunk_mask(p.shape, 0), p, 0.0)
            ds = p * (_dot_nt(do_ref[qs, :], v_ref[rows, :]) - _cat([dl_ref[qs, :]] * rep))
            acc_sc[qs, :] += _dot(ds, k)

        def step(j, carry):
            block(j, None)
            return carry

        lax.fori_loop(0, i * sub, step, 0)
        for u in range(sub):
            block(i * sub + u, u * tk)
        dqn_ref[...] = acc_sc[:, 0:LANE]
        dqr_ref[...] = acc_sc[:, LANE:2 * LANE]

    return _pcall(
        body, name=name, grid=(HEADS, nq),
        in_specs=[pl.BlockSpec((t, LANE), lambda h, i: (i, h)),
                  pl.BlockSpec((t, LANE), lambda h, i: (i, HEADS + h)),
                  pl.BlockSpec((s, LANE), lambda h, i: (0, h)),
                  pl.BlockSpec((s, LANE), lambda h, i: (0, 0)),
                  pl.BlockSpec((s, LANE), lambda h, i: (0, HEADS + h)),
                  pl.BlockSpec((t, LANE), lambda h, i: (i, h)),
                  pl.BlockSpec((t, LANE), lambda h, i: (i, h)),
                  pl.BlockSpec((t, LANE), lambda h, i: (i, h))],
        out_specs=[pl.BlockSpec((t, LANE), lambda h, i: (i, h)), pl.BlockSpec((t, LANE), lambda h, i: (i, h))],
        out_shape=[jax.ShapeDtypeStruct((s, A_WIDTH), _F32), jax.ShapeDtypeStruct((s, A_WIDTH), _F32)],
        scratch_shapes=[pltpu.VMEM((t, 2 * LANE), _F32)],
        compiler_params=_cp(2),
    )(q_all, q_all, kv, kr, kv, lse, delta, d_o)


def _attn_dkv(q_all, kv, kr, lse_t, delta_t, d_o, *, name):
    s = q_all.shape[0]
    t = min(ATTN_BWD_BLOCK, s)
    tq = min(ATTN_BWD_SUB, t)
    nk, sub, nqs = s // t, t // tq, s // tq

    def body(kn_ref, kr_ref, v_ref, qn_ref, qr_ref, do_ref, lset_ref, dlt_ref, dkn_ref, dv_ref, dkr_ref, dk_sc, dv_sc):
        h, j = pl.program_id(0), pl.program_id(1)
        dk_sc[...] = jnp.zeros(dk_sc.shape, _F32)
        dv_sc[...] = jnp.zeros(dv_sc.shape, _F32)

        def block(i, query_off):
            ks = slice(0, t if query_off is None else query_off + tq)
            rows = _rows(i, tq)
            stat = pl.ds(pl.multiple_of((i * HEADS + h) * SUBLANE, SUBLANE), 1)
            q = _cat([qn_ref[rows, :], qr_ref[rows, :]])
            do = do_ref[rows, :]
            p = jnp.exp(_dot_nt(_cat([kn_ref[ks, :], kr_ref[ks, :]]), q) - lset_ref[stat, :])
            if query_off is not None:
                p = jnp.where(_chunk_mask_at(p.shape, 1, query_off), p, 0.0)
            dv_sc[ks, :] += _dot(p, do)
            ds = p * (_dot_nt(v_ref[ks, :], do) - dlt_ref[stat, :])
            dk_sc[ks, :] += _dot(ds, q)

        def step(i, carry):
            block(i, None)
            return carry

        for u in range(sub):
            block(j * sub + u, u * tq)
        lax.fori_loop((j + 1) * sub, nqs, step, 0)
        dkn_ref[...] = dk_sc[:, 0:LANE]
        dkr_ref[...] = dk_sc[:, LANE:2 * LANE]
        dv_ref[...] = dv_sc[...]

    stats = pl.BlockSpec((nqs * HEADS * SUBLANE, tq), lambda h, j: (0, 0))
    return _pcall(
        body, name=name, grid=(HEADS, nk),
        in_specs=[pl.BlockSpec((t, LANE), lambda h, j: (j, h)),
                  pl.BlockSpec((t, LANE), lambda h, j: (j, 0)),
                  pl.BlockSpec((t, LANE), lambda h, j: (j, HEADS + h)),
                  pl.BlockSpec((s, LANE), lambda h, j: (0, h)),
                  pl.BlockSpec((s, LANE), lambda h, j: (0, HEADS + h)),
                  pl.BlockSpec((s, LANE), lambda h, j: (0, h)),
                  stats, stats],
        out_specs=[pl.BlockSpec((t, LANE), lambda h, j: (j, h))] * 3,
        out_shape=[jax.ShapeDtypeStruct((s, A_WIDTH), _F32)] * 3,
        scratch_shapes=[pltpu.VMEM((t, 2 * LANE), _F32), pltpu.VMEM((t, LANE), _F32)],
        compiler_params=_cp(2),
    )(kv, kr, kv, q_all, q_all, d_o, lse_t, delta_t)


def _final_loss(x, tgt, gain, *, name, tm=512):
    tm = min(tm, x.shape[0])

    def fn(x_, t_, g_):
        def f(xx, gg):
            err = _rms(xx, gg) - t_
            return 0.5 * jnp.sum(jnp.sum(err * err, axis=1, keepdims=True) / D_MODEL, axis=0, keepdims=True)

        loss, pull = jax.vjp(f, x_, g_)
        dx, dg = pull(jnp.ones((1, 1), _F32))
        return dx, dg, jnp.broadcast_to(loss, (SUBLANE, LANE))

    return _rowmap(fn, [_t(x), _t(tgt)], [gain], [(None, D_MODEL, _F32)], [(gain.shape, _F32), ((SUBLANE, LANE), _F32)], tm=tm, name=name)


def _local_step(x, mem, cos_t, sin_t, tgt, p):
    s = x.shape[0]
    g = {}
    row = lambda a: a.reshape(1, -1)
    tm_e = min(256, s)

    mem_n = _rms_fwd(mem, row(p["mem_norm"]), name="mem_norm")
    mem_kv_all = _mm_nn(mem_n, p["w_mem_all"], out_dtype=_F32, name="mem_kv", tn=1024)
    mem_kv = [mem_kv_all[:, 2 * MEM_WIDTH * l:2 * MEM_WIDTH * (l + 1)] for l in range(DEPTH)]

    sv = []
    for i in range(N_A):
        l = i
        r = {}
        r["x0"] = x
        x, r["ffn1"] = _ffn_fwd(x, row(p["ffn1_norm"][l]), p["ffn_w_gu"], p["ffn_w_down"], l, name=f"a{i}_ffn1")
        r["x1"] = x
        xn = _rms_fwd(x, row(p["mix_norm"][l]), name=f"a{i}_mixnorm")
        h = _mm_nn(xn, p["a_w_in"][i], out_dtype=_F32, name=f"a{i}_in", tn=1152)
        qkv_c = _conv_fwd(h, p["a_conv"][i], name=f"a{i}_conv")
        alog, dtb = p["a_A_log_row"][i], p["a_dt_bias_row"][i]
        q, k, v, bg = _rowmap(_gdn_prep_fwd_fn, [_t(qkv_c), _t(h, LANE, 26)], [alog, dtb],
                              [(None, A_WIDTH, _F32)] * 3 + [(None, LANE, _F32)], tm=tm_e, name=f"a{i}_prep")
        w_, u_, qd, kd, qk, dcrow = _rowmap(_gdn_intra_fwd_fn, [_t(q), _t(k), _t(v), _t(bg)], [],
                                            [(None, A_WIDTH, _F32)] * 5 + [(None, LANE, _F32)], tm=_PAIR, name=f"a{i}_intra")
        o, states = _rowmap(_gdn_scan_fwd_fn, [_t(qd), _t(kd), _t(w_), _t(u_), _t(qk), _t(dcrow)], [],
                            [(None, A_WIDTH, _F32), (A_WIDTH, HEAD_DIM, _F32)], tm=_PAIR, name=f"a{i}_scan",
                            carry=[((A_WIDTH, HEAD_DIM), _F32)])
        gain_o = row(p["a_out_norm"][i])
        cat = _rowmap(_a_post_fwd_fn, [_t(o), _t(h, A_WIDTH, 3), _t(h, MEM_WIDTH, 12)], [gain_o, mem_kv[l]],
                      [(None, D_MODEL, _BF)], tm=tm_e, name=f"a{i}_post")[0]
        x = _mm_nn(cat, p["w_out"][l], out_dtype=_F32, name=f"a{i}_out", res=x, tn=1024)
        r.update(xn=xn, h=h, qkv_c=qkv_c, q=q, k=k, v=v, bg=bg, w=w_, u=u_, qd=qd, kd=kd, qk=qk, dcrow=dcrow, o=o, states=states, cat=cat)
        r["x2"] = x
        x, r["ffn2"] = _ffn_fwd(x, row(p["ffn2_norm"][l]), p["ffn_w_gu"], p["ffn_w_down"], DEPTH + l, name=f"a{i}_ffn2")
        sv.append(r)

    kvs = {"x": x}
    xn_kv = _rms_fwd(x, row(p["kv_in_norm"]), name="kv_innorm")
    ckr = _mm_nn(xn_kv, p["w_dkv"], out_dtype=_F32, name="kv_down")
    ckv, k_rope = _rowmap(_kv_prep_fwd_fn, [_t(ckr), _t(cos_t), _t(sin_t)], [row(p["kv_lat_norm"])],
                          [(None, KV_LORA, _BF), (None, LANE, _BF)], tm=tm_e, name="kv_prep")
    kvu = _mm_nn(ckv, p["w_ukv"], out_dtype=_BF, name="kv_up")
    kvs.update(xn=xn_kv, ckr=ckr, ckv=ckv)

    for j in range(N_B):
        l = N_A + j
        r = {}
        x, r["ffn1"] = _ffn_fwd(x, row(p["ffn1_norm"][l]), p["ffn_w_gu"], p["ffn_w_down"], l, name=f"b{j}_ffn1")
        r["x1"] = x
        xn = _rms_fwd(x, row(p["mix_norm"][l]), name=f"b{j}_mixnorm")
        h = _mm_nn(xn, p["b_w_in"][j], out_dtype=_F32, name=f"b{j}_in")
        gain_q = row(p["b_q_norm"][j])
        cqn = _rowmap(lambda c_, g_: (_rms(c_, g_),), [_t(h, Q_LORA, 0)], [gain_q], [(None, Q_LORA, _BF)], tm=tm_e, name=f"b{j}_qnorm")[0]
        qf = _mm_nn(cqn, p["b_w_uq"][j], out_dtype=_F32, name=f"b{j}_uq")
        q_all = _rowmap(_rope_q_fwd_fn, [_t(qf), _t(cos_t), _t(sin_t)], [], [(None, UQ_PAD, _BF)], tm=tm_e, name=f"b{j}_rope")[0]
        o_b, lse = _attn_fwd(q_all, kvu, k_rope, name=f"b{j}_attn")
        cat = _rowmap(_b_post_fwd_fn, [_t(o_b), _t(h, MEM_WIDTH, 1)], [mem_kv[l]], [(None, D_MODEL, _BF)], tm=tm_e, name=f"b{j}_post")[0]
        x = _mm_nn(cat, p["w_out"][l], out_dtype=_F32, name=f"b{j}_out", res=x, tn=1024)
        r.update(xn=xn, h=h, cqn=cqn, q_all=q_all, o_b=o_b, lse=lse, cat=cat)
        x, r["ffn2"] = _ffn_fwd(x, row(p["ffn2_norm"][l]), p["ffn_w_gu"], p["ffn_w_down"], DEPTH + l, name=f"b{j}_ffn2")
        sv.append(r)

    dx, g["final_norm"], loss = _final_loss(x, tgt, row(p["final_norm"]), name="loss")

    per_layer = lambda: [None] * DEPTH
    for n in ("ffn1_norm", "mix_norm", "ffn2_norm", "w_out", "mem_kv"):
        g[n] = per_layer()
    g["ffn_w_gu"] = jnp.zeros((N_CHIPS, 2 * DEPTH, D_MODEL, FFN_COL_TILE), _F32)
    g["ffn_w_down"] = jnp.zeros((N_CHIPS, 2 * DEPTH, FFN_SHARD_ROWS, D_MODEL), _F32)
    for n in ("a_w_in", "a_conv", "a_A_log_row", "a_dt_bias_row", "a_out_norm", "b_w_in", "b_q_norm", "b_w_uq"):
        g[n] = [None] * N_A
    d_kv_parts = []

    for j in reversed(range(N_B)):
        l = N_A + j
        r = sv[l]
        dx, g["ffn2_norm"][l], g["ffn_w_gu"], g["ffn_w_down"] = _ffn_bwd(
            dx, r["ffn2"], row(p["ffn2_norm"][l]), p["ffn_w_gu"], p["ffn_w_down"], DEPTH + l, g["ffn_w_gu"], g["ffn_w_down"], name=f"b{j}_ffn2b")
        d_cat = _mm_nt(dx, p["w_out"][l], out_dtype=_F32, name=f"b{j}_dcat", tn=1024)
        g["w_out"][l] = _mm_tn(r["cat"], dx, name=f"b{j}_dwout", tn=1024)
        d_qm, g["mem_kv"][l] = _rowmap(_b_post_bwd_fn, [_t(r["h"], MEM_WIDTH, 1), _t(d_cat, MEM_WIDTH, 3)], [mem_kv[l]],
                                      [(None, MEM_WIDTH, _F32)], [((N_MEM, 2 * MEM_WIDTH), _F32)], tm=tm_e, name=f"b{j}_postb")
        d_o, delta, lse_t, delta_t = _attn_bwd_prep(r["o_b"], r["lse"], d_cat, name=f"b{j}_delta")
        dqn, dqr = _attn_dq(r["q_all"], kvu, k_rope, r["lse"], delta, d_o, name=f"b{j}_attn_dq")
        d_kv_parts.append(_attn_dkv(r["q_all"], kvu, k_rope, lse_t, delta_t, d_o, name=f"b{j}_attn_dkv"))
        d_qf = _rowmap(_rope_q_bwd_fn, [_t(dqn), _t(dqr), _t(cos_t), _t(sin_t)], [], [(None, UQ_PAD, _F32)], tm=tm_e, name=f"b{j}_ropeb")[0]
        d_cqn = _mm_nt(d_qf, p["b_w_uq"][j], out_dtype=_F32, name=f"b{j}_dcqn")
        g["b_w_uq"][j] = _mm_tn(r["cqn"], d_qf, name=f"b{j}_dwuq")
        gain_q = row(p["b_q_norm"][j])
        d_cq, g["b_q_norm"][j] = _rowmap(lambda c_, d_, g_: _vjp(_rms, [c_, g_], (d_,)), [_t(r["h"], Q_LORA, 0), _t(d_cqn)], [gain_q],
                                        [(None, Q_LORA, _F32)], [((1, Q_LORA), _F32)], tm=tm_e, name=f"b{j}_qnormb")
        d_h = jnp.concatenate([d_cq, d_qm], axis=1)
        d_xn = _mm_nt(d_h, p["b_w_in"][j], out_dtype=_F32, name=f"b{j}_dxn", tn=1024)
        g["b_w_in"][j] = _mm_tn(r["xn"], d_h, name=f"b{j}_dwin")
        dx, g["mix_norm"][l] = _rms_bwd(r["x1"], d_xn, dx, row(p["mix_norm"][l]), name=f"b{j}_mixnormb")
        dx, g["ffn1_norm"][l], g["ffn_w_gu"], g["ffn_w_down"] = _ffn_bwd(
            dx, r["ffn1"], row(p["ffn1_norm"][l]), p["ffn_w_gu"], p["ffn_w_down"], l, g["ffn_w_gu"], g["ffn_w_down"], name=f"b{j}_ffn1b")

    def kv_sum(*parts):
        dkn = sum(parts[0::3][1:], parts[0])
        dv = sum(parts[1::3][1:], parts[1])
        dkr = sum(parts[2::3][1:], parts[2])
        return _cat([dkn, dv]), sum(_heads(dkr, HEADS)[1:], _heads(dkr, HEADS)[0])

    d_kvu, d_kr = _rowmap(kv_sum, [_t(a) for part in d_kv_parts for a in part], [], [(None, 2 * A_WIDTH, _F32), (None, LANE, _F32)],
                          tm=tm_e, name="kv_dsum")
    d_ckv = _mm_nt(d_kvu, p["w_ukv"], out_dtype=_F32, name="kv_dckv")
    g["w_ukv"] = _mm_tn(kvs["ckv"], d_kvu, name="kv_dwukv")
    d_ckr, g["kv_lat_norm"] = _rowmap(_kv_prep_bwd_fn, [_t(kvs["ckr"]), _t(d_ckv), _t(d_kr), _t(cos_t), _t(sin_t)], [row(p["kv_lat_norm"])],
                                     [(None, DKV_PAD, _F32)], [((1, KV_LORA), _F32)], tm=tm_e, name="kv_prepb")
    d_xn = _mm_nt(d_ckr, p["w_dkv"], out_dtype=_F32, name="kv_dxn", tn=1024)
    g["w_dkv"] = _mm_tn(kvs["xn"], d_ckr, name="kv_dwdkv")
    dx, g["kv_in_norm"] = _rms_bwd(kvs["x"], d_xn, dx, row(p["kv_in_norm"]), name="kv_innormb")

    for i in reversed(range(N_A)):
        l = i
        r = sv[l]
        dx, g["ffn2_norm"][l], g["ffn_w_gu"], g["ffn_w_down"] = _ffn_bwd(
            dx, r["ffn2"], row(p["ffn2_norm"][l]), p["ffn_w_gu"], p["ffn_w_down"], DEPTH + l, g["ffn_w_gu"], g["ffn_w_down"], name=f"a{i}_ffn2b")
        d_cat = _mm_nt(dx, p["w_out"][l], out_dtype=_F32, name=f"a{i}_dcat", tn=1024)
        g["w_out"][l] = _mm_tn(r["cat"], dx, name=f"a{i}_dwout", tn=1024)
        gain_o = row(p["a_out_norm"][i])
        h = r["h"]
        d_o, d_hpart, g["a_out_norm"][i], g["mem_kv"][l] = _rowmap(
            _a_post_bwd_fn, [_t(r["o"]), _t(h, A_WIDTH, 3), _t(h, MEM_WIDTH, 12), _t(d_cat)], [gain_o, mem_kv[l]],
            [(None, A_WIDTH, _F32), (None, D_MODEL, _F32)], [((1, HEAD_DIM), _F32), ((N_MEM, 2 * MEM_WIDTH), _F32)], tm=tm_e, name=f"a{i}_postb")
        d_qd, d_kd, d_w, d_u, d_qk, d_dcrow = _rowmap(
            _gdn_scan_bwd_fn, [_t(r["qd"]), _t(r["kd"]), _t(r["w"]), _t(r["u"]), _t(r["qk"]), _t(r["dcrow"]), _t(r["states"], rows=A_WIDTH), _t(d_o)], [],
            [(None, A_WIDTH, _F32)] * 5 + [(None, LANE, _F32)], tm=_PAIR, name=f"a{i}_scanb", carry=[((A_WIDTH, HEAD_DIM), _F32)], reverse=True)
        d_q, d_k, d_v, d_bg = _rowmap(
            _gdn_intra_bwd_fn, [_t(r["q"]), _t(r["k"]), _t(r["v"]), _t(r["bg"]), _t(d_w), _t(d_u), _t(d_qd), _t(d_kd), _t(d_qk), _t(d_dcrow)], [],
            [(None, A_WIDTH, _F32)] * 3 + [(None, LANE, _F32)], tm=_PAIR, name=f"a{i}_intrab")
        alog, dtb = p["a_A_log_row"][i], p["a_dt_bias_row"][i]
        d_qkv_c, d_ba, g["a_A_log_row"][i], g["a_dt_bias_row"][i] = _rowmap(
            _gdn_prep_bwd_fn, [_t(r["qkv_c"]), _t(h, LANE, 26), _t(d_q), _t(d_k), _t(d_v), _t(d_bg)], [alog, dtb],
            [(None, 3 * A_WIDTH, _F32), (None, LANE, _F32)], [((1, LANE), _F32), ((1, LANE), _F32)], tm=tm_e, name=f"a{i}_prepb")
        d_qkv, g["a_conv"][i] = _conv_bwd(h, d_qkv_c, p["a_conv"][i], name=f"a{i}_convb")
        d_h = jnp.concatenate([d_qkv, d_hpart, d_ba], axis=1)
        d_xn = _mm_nt(d_h, p["a_w_in"][i], out_dtype=_F32, name=f"a{i}_dxn", tn=1024)
        g["a_w_in"][i] = _mm_tn(r["xn"], d_h, name=f"a{i}_dwin", tn=1152)
        dx, g["mix_norm"][l] = _rms_bwd(r["x1"], d_xn, dx, row(p["mix_norm"][l]), name=f"a{i}_mixnormb")
        dx, g["ffn1_norm"][l], g["ffn_w_gu"], g["ffn_w_down"] = _ffn_bwd(
            dx, r["ffn1"], row(p["ffn1_norm"][l]), p["ffn_w_gu"], p["ffn_w_down"], l, g["ffn_w_gu"], g["ffn_w_down"], name=f"a{i}_ffn1b")

    d_mem_kv_all = jnp.concatenate(g.pop("mem_kv"), axis=1)
    d_mem_n = _mm_nt(d_mem_kv_all, p["w_mem_all"], out_dtype=_F32, name="mem_dn", tn=1024)
    g["w_mem_all"] = _mm_tn(mem_n, d_mem_kv_all, name="mem_dw", tn=1024)
    _, g["mem_norm"] = _rms_bwd(mem, d_mem_n, None, row(p["mem_norm"]), name="mem_normb")
    return loss, dx, g


_NOPE_ROPE = HEAD_DIM + QK_ROPE
_QKV_GATE = 4 * A_WIDTH
_BETA_AT = _QKV_GATE + MEM_WIDTH


def _lane_row(vals, at):
    n = vals.shape[0]
    return jnp.concatenate([jnp.zeros((at,), _F32), vals.astype(_F32), jnp.zeros((LANE - at - n,), _F32)]).reshape(1, LANE)


def _compute_form(w, conv_f32, ffn_w_gu, ffn_w_down):
    p = {n: w[n] for n in ("ffn1_norm", "mix_norm", "ffn2_norm", "w_out", "mem_norm", "a_out_norm", "b_w_in", "b_q_norm", "kv_in_norm",
                           "kv_lat_norm", "final_norm")}
    p["ffn_w_gu"], p["ffn_w_down"] = ffn_w_gu, ffn_w_down
    wm = w["w_mem_kv"]
    p["w_mem_all"] = jnp.transpose(wm, (1, 0, 2)).reshape(D_MODEL, DEPTH * 2 * MEM_WIDTH)
    a = w["a_w_in"]
    pad = jnp.zeros((N_A, D_MODEL, A_IN_PAD - A_IN), a.dtype)
    p["a_w_in"] = jnp.concatenate([a[:, :, :_QKV_GATE], a[:, :, _QKV_GATE + 2 * HEADS:], a[:, :, _QKV_GATE:_QKV_GATE + 2 * HEADS], pad], axis=2)
    p["a_conv"] = jnp.concatenate([conv_f32, jnp.zeros((N_A, SUBLANE - CONV_K, 3 * A_WIDTH), _F32)], axis=1)
    p["a_A_log_row"] = [_lane_row(w["a_A_log"][i], HEADS) for i in range(N_A)]
    p["a_dt_bias_row"] = [_lane_row(w["a_dt_bias"][i], HEADS) for i in range(N_A)]
    uq = w["b_w_uq"].reshape(N_B, Q_LORA, HEADS, _NOPE_ROPE)
    rope = jnp.concatenate([uq[..., HEAD_DIM:], jnp.zeros((N_B, Q_LORA, HEADS, LANE - QK_ROPE), uq.dtype)], axis=-1)
    p["b_w_uq"] = jnp.concatenate([uq[..., :HEAD_DIM].reshape(N_B, Q_LORA, A_WIDTH), rope.reshape(N_B, Q_LORA, A_WIDTH)], axis=-1)
    dkv = w["w_dkv"]
    p["w_dkv"] = jnp.concatenate([dkv, jnp.zeros((D_MODEL, DKV_PAD - dkv.shape[1]), dkv.dtype)], axis=1)
    ukv = w["w_ukv"].reshape(KV_LORA, HEADS, 2 * HEAD_DIM)
    p["w_ukv"] = jnp.concatenate([ukv[..., :HEAD_DIM].reshape(KV_LORA, A_WIDTH), ukv[..., HEAD_DIM:].reshape(KV_LORA, A_WIDTH)], axis=-1)
    return p


def _natural_grads(g):
    st = lambda xs: jnp.stack(xs, axis=0)
    n = {}
    for k in ("ffn1_norm", "mix_norm", "ffn2_norm"):
        n[k] = st(g[k]).reshape(DEPTH, D_MODEL)
    for k in ("w_out", "b_w_in"):
        n[k] = st(g[k])
    n["mem_norm"] = g["mem_norm"].reshape(D_MODEL)
    n["w_mem_kv"] = jnp.transpose(g["w_mem_all"].reshape(D_MODEL, DEPTH, 2 * MEM_WIDTH), (1, 0, 2))
    a = st(g["a_w_in"])
    n["a_w_in"] = jnp.concatenate([a[:, :, :_QKV_GATE], a[:, :, _BETA_AT:_BETA_AT + 2 * HEADS], a[:, :, _QKV_GATE:_BETA_AT]], axis=2)
    n["a_conv"] = st(g["a_conv"])[:, :CONV_K]
    n["a_A_log"] = st(g["a_A_log_row"])[:, 0, HEADS:2 * HEADS]
    n["a_dt_bias"] = st(g["a_dt_bias_row"])[:, 0, HEADS:2 * HEADS]
    n["a_out_norm"] = st(g["a_out_norm"]).reshape(N_A, HEAD_DIM)
    n["b_q_norm"] = st(g["b_q_norm"]).reshape(N_B, Q_LORA)
    uq = st(g["b_w_uq"])
    nope = uq[:, :, :A_WIDTH].reshape(N_B, Q_LORA, HEADS, HEAD_DIM)
    rope = uq[:, :, A_WIDTH:].reshape(N_B, Q_LORA, HEADS, LANE)[..., :QK_ROPE]
    n["b_w_uq"] = jnp.concatenate([nope, rope], axis=-1).reshape(N_B, Q_LORA, HEADS * _NOPE_ROPE)
    n["kv_in_norm"] = g["kv_in_norm"].reshape(D_MODEL)
    n["w_dkv"] = g["w_dkv"][:, :KV_LORA + QK_ROPE]
    n["kv_lat_norm"] = g["kv_lat_norm"].reshape(KV_LORA)
    ukv = g["w_ukv"]
    n["w_ukv"] = jnp.concatenate([ukv[:, :A_WIDTH].reshape(KV_LORA, HEADS, HEAD_DIM), ukv[:, A_WIDTH:].reshape(KV_LORA, HEADS, HEAD_DIM)],
                                 axis=-1).reshape(KV_LORA, HEADS * 2 * HEAD_DIM)
    n["final_norm"] = g["final_norm"].reshape(D_MODEL)
    return n


def _rope_tables(positions):
    inv = ROPE_THETA ** (-jnp.arange(0, QK_ROPE, 2, dtype=_F32) / QK_ROPE)
    ang = positions.astype(_F32)[:, None] * inv
    z = jnp.zeros((positions.shape[0], LANE - QK_ROPE), _F32)
    cos, sin = jnp.cos(ang), jnp.sin(ang)
    return jnp.concatenate([cos, cos, z], axis=1), jnp.concatenate([sin, sin, z], axis=1)


_HBM = pl.BlockSpec(memory_space=pltpu.HBM)


def _place():
    x, y, c = lax.axis_index("x"), lax.axis_index("y"), lax.axis_index("c")
    return x, y, c, [(1 - x, y), (x, 1 - y), (1 - x, 1 - y)]


def _remote(src, dst, send_sem, recv_sem, to):
    return pltpu.make_async_remote_copy(src_ref=src, dst_ref=dst, send_sem=send_sem, recv_sem=recv_sem, device_id=to, device_id_type=_MESH)


def _gather_over_chips(shard, *, name):
    rows, cols = shard.shape
    half = rows // 2

    def body(w_ref, out_ref, send_sems, recv_sems, local_sem):
        x, y, c, chips = _place()
        k = 2 * x + y

        def part(chip, h):
            return out_ref.at[chip, pl.ds(h * half, half), :]

        mine = pltpu.make_async_copy(w_ref, out_ref.at[k], local_sem)
        mine.start()
        first = [_remote(w_ref.at[pl.ds(c * half, half), :], part(k, c), send_sems.at[j], recv_sems.at[j], (px, py, c))
                 for j, (px, py) in enumerate(chips)]
        for cp in first:
            cp.start()
        passed = []
        for j, (px, py) in enumerate(chips):
            got = part(2 * px + py, c)
            _remote(got, got, send_sems.at[j], recv_sems.at[j], (px, py, c)).wait_recv()
            fw = _remote(got, got, send_sems.at[3 + j], recv_sems.at[3 + j], (x, y, 1 - c))
            fw.start()
            passed.append(fw)
        for j, (px, py) in enumerate(chips):
            got = part(2 * px + py, 1 - c)
            _remote(got, got, send_sems.at[3 + j], recv_sems.at[3 + j], (x, y, 1 - c)).wait_recv()
        for cp in first + passed:
            cp.wait_send()
        mine.wait()

    return _pcall(
        body, name=name, in_specs=[_HBM], out_specs=_HBM, out_shape=jax.ShapeDtypeStruct((N_CHIPS, rows, cols), shard.dtype),
        scratch_shapes=[pltpu.SemaphoreType.DMA((6,)), pltpu.SemaphoreType.DMA((6,)), pltpu.SemaphoreType.DMA],
    )(shard)


PAIR_COPIES = 4


def _scatter_over_chips(v, *, name):
    def body(v_ref, out_ref, send_sems, recv_sems):
        x, y, c, chips = _place()
        cps = [_remote(v_ref.at[2 * px + py], out_ref.at[j], send_sems.at[j], recv_sems.at[j], (px, py, c)) for j, (px, py) in enumerate(chips)]
        for cp in cps:
            cp.start()
        for cp in cps:
            cp.wait()

    return _pcall(body, name=name, in_specs=[_HBM], out_specs=_HBM, out_shape=jax.ShapeDtypeStruct((N_CHIPS - 1,) + v.shape[1:], v.dtype),
                  scratch_shapes=[pltpu.SemaphoreType.DMA((3,)), pltpu.SemaphoreType.DMA((3,))])(v)


def _all_reduce_small(v, *, name):
    def body(v_ref, out_ref, all_ref, send_sems, recv_sems):
        x, y, c, _ = _place()
        me = 4 * x + 2 * y + c
        all_ref[me] = v_ref[...]
        cps = []
        for f in range(1, N_DEV):
            fx, fy, fc = (f >> 2) & 1, (f >> 1) & 1, f & 1
            to = (x + fx - 2 * x * fx, y + fy - 2 * y * fy, c + fc - 2 * c * fc)
            cps.append(_remote(v_ref, all_ref.at[me], send_sems.at[f - 1], recv_sems.at[f - 1], to))
        for cp in cps:
            cp.start()
        for cp in cps:
            cp.wait()
        acc = all_ref[0]
        for d in range(1, N_DEV):
            acc = acc + all_ref[d]
        out_ref[...] = acc

    vm = pl.BlockSpec(memory_space=pltpu.VMEM)
    return _pcall(body, name=name, in_specs=[vm], out_specs=vm, out_shape=jax.ShapeDtypeStruct(v.shape, v.dtype),
                  scratch_shapes=[pltpu.VMEM((N_DEV,) + v.shape, v.dtype), pltpu.SemaphoreType.DMA((N_DEV - 1,)), pltpu.SemaphoreType.DMA((N_DEV - 1,))])(v)


_FFN_GU = ("ffn1_w_gu", "ffn2_w_gu")
_FFN_DOWN = ("ffn1_w_down", "ffn2_w_down")
_BIG = (("w_out", 1), ("w_mem_kv", 1), ("a_w_in", 2), ("a_conv", 2), ("b_w_in", 1), ("b_w_uq", 2), ("w_dkv", 0),
        ("w_ukv", 1))
_SMALL = ("ffn1_norm", "mix_norm", "ffn2_norm", "mem_norm", "a_A_log", "a_dt_bias", "a_out_norm", "b_q_norm", "kv_in_norm", "kv_lat_norm",
          "final_norm")
_WEIGHTS = ("ffn1_norm", "ffn1_w_gu", "ffn1_w_down", "mix_norm", "ffn2_norm", "ffn2_w_gu", "ffn2_w_down", "w_out", "mem_norm", "w_mem_kv",
            "a_w_in", "a_conv", "a_A_log", "a_dt_bias", "a_out_norm", "b_w_in", "b_q_norm", "b_w_uq", "kv_in_norm", "w_dkv", "kv_lat_norm",
            "w_ukv", "final_norm")


PACK_PIECE_ROWS = 16


def _piece_rows(shape):
    return -(-math.prod(shape) // (PACK_COLS * PACK_PIECE_ROWS)) * PACK_PIECE_ROWS


def _packed_rows(shapes):
    return sum(_piece_rows(s) for s in shapes)


def _pack(arrs, rows):
    pieces = []
    for a in arrs:
        n, r = a.size, _piece_rows(a.shape)
        flat = a.reshape(-1)
        if r * PACK_COLS != n:
            flat = jnp.concatenate([flat, jnp.zeros((r * PACK_COLS - n,), a.dtype)])
        pieces.append(flat.reshape(r, PACK_COLS))
    used = sum(p.shape[0] for p in pieces)
    if rows > used:
        pieces.append(jnp.zeros((rows - used, PACK_COLS), arrs[0].dtype))
    return jnp.concatenate(pieces, axis=0)


def _unpack(packed, shapes):
    off, out = 0, []
    for shp in shapes:
        n, r = math.prod(shp), _piece_rows(shp)
        piece = packed[off:off + r]
        out.append((piece if r * PACK_COLS == n else piece.reshape(-1)[:n]).reshape(shp))
        off += r
    return out


def _adamw_math(w_, g_, m_, v_):
    m2 = ADAM_B1 * m_ + (1.0 - ADAM_B1) * g_
    v2 = ADAM_B2 * v_ + (1.0 - ADAM_B2) * (g_ * g_)
    m_hat = m2 / (1.0 - ADAM_B1 ** ADAM_STEP)
    v_hat = v2 / (1.0 - ADAM_B2 ** ADAM_STEP)
    return -ADAM_LR * (m_hat / (jnp.sqrt(v_hat) + ADAM_EPS) + ADAM_WD * w_), m2, v2


def _adamw(w, g, m, v, *, name):
    return _rowmap(_adamw_math, [_t(w), _t(g), _t(m), _t(v)], [], [(None, w.shape[1], _F32)] * 3, tm=min(PACK_ROW_TILE, w.shape[0]), name=name)


def _pair_exchange_half(g, *, name):
    n, _, h, cols = g.shape

    def body(g_ref, out_ref, send_sems, recv_sems):
        x, y, c, _ = _place()
        cps = [_remote(g_ref.at[k, 1 - c], out_ref.at[k], send_sems.at[k], recv_sems.at[k], (x, y, 1 - c)) for k in range(n)]
        for cp in cps:
            cp.start()
        for cp in cps:
            cp.wait()

    return _pcall(body, name=name, in_specs=[_HBM], out_specs=_HBM, out_shape=jax.ShapeDtypeStruct((n, h, cols), g.dtype),
                  scratch_shapes=[pltpu.SemaphoreType.DMA((n,)), pltpu.SemaphoreType.DMA((n,))])(g)


def _add_half(g, other, c, *, name):
    n, _, h, cols = g.shape
    tm = min(PACK_ROW_TILE, h)

    def body(c_ref, g_ref, o_ref, sum_ref, narrow_ref):
        acc = g_ref[...] + o_ref[...]
        sum_ref[...] = acc
        narrow_ref[...] = acc.astype(narrow_ref.dtype)

    slab = pl.BlockSpec((None, tm, cols), lambda k, i, c_ref: (k, i, 0))
    return _pcall(
        body, name=name,
        grid_spec=pltpu.PrefetchScalarGridSpec(
            num_scalar_prefetch=1, grid=(n, h // tm),
            in_specs=[pl.BlockSpec((None, None, tm, cols), lambda k, i, c_ref: (k, c_ref[0], i, 0)), slab], out_specs=[slab, slab]),
        out_shape=[jax.ShapeDtypeStruct((n, h, cols), _F32), jax.ShapeDtypeStruct((n, h, cols), jnp.bfloat16)],
        compiler_params=_cp(2),
    )(jnp.reshape(c, (1,)).astype(jnp.int32), g, other)


def _add_own(chip_sum, from_chips, chip, *, name):
    _, h, cols = chip_sum.shape
    tm = min(PACK_ROW_TILE, h)

    def body(k_ref, own_ref, a_ref, b_ref, c_ref, o_ref):
        o_ref[...] = ((own_ref[...] + a_ref[...].astype(_F32)) + b_ref[...].astype(_F32)) + c_ref[...].astype(_F32)

    sent = [pl.BlockSpec((None, tm, cols), functools.partial(lambda i, k_ref, j: (j, i, 0), j=j)) for j in range(N_CHIPS - 1)]
    return _pcall(
        body, name=name,
        grid_spec=pltpu.PrefetchScalarGridSpec(
            num_scalar_prefetch=1, grid=(h // tm,),
            in_specs=[pl.BlockSpec((None, tm, cols), lambda i, k_ref: (k_ref[0], i, 0))] + sent,
            out_specs=pl.BlockSpec((tm, cols), lambda i, k_ref: (i, 0))),
        out_shape=jax.ShapeDtypeStruct((h, cols), _F32), compiler_params=_cp(1),
    )(jnp.reshape(chip, (1,)).astype(jnp.int32), chip_sum, from_chips, from_chips, from_chips)


def _pair_gather(mine, *, name):
    h, cols = mine.shape
    per = h // PAIR_COPIES
    assert per * PAIR_COPIES == h and per % SUBLANE == 0, mine.shape

    def body(v_ref, out_ref, send_sems, recv_sems, local_sem):
        x, y, c, _ = _place()
        here = pltpu.make_async_copy(v_ref, out_ref.at[c], local_sem)
        here.start()
        cps = [_remote(v_ref.at[pl.ds(q * per, per), :], out_ref.at[c, pl.ds(q * per, per), :], send_sems.at[q], recv_sems.at[q], (x, y, 1 - c))
               for q in range(PAIR_COPIES)]
        for cp in cps:
            cp.start()
        for q, cp in enumerate(cps):
            cp.wait_send()
            _remote(v_ref.at[pl.ds(q * per, per), :], out_ref.at[1 - c, pl.ds(q * per, per), :], send_sems.at[q], recv_sems.at[q],
                    (x, y, 1 - c)).wait_recv()
        here.wait()

    return _pcall(body, name=name, in_specs=[_HBM], out_specs=_HBM, out_shape=jax.ShapeDtypeStruct((2, h, cols), mine.dtype),
                  scratch_shapes=[pltpu.SemaphoreType.DMA((PAIR_COPIES,)), pltpu.SemaphoreType.DMA((PAIR_COPIES,)), pltpu.SemaphoreType.DMA])(mine)


def _reduce_over_devices(g, c, chip, *, name):
    n, rows, cols = g.shape
    g = g.reshape(n, 2, rows // 2, cols)
    chip_sum, narrow = _add_half(g, _pair_exchange_half(g, name=name + "_pair_sum"), c, name=name + "_add_pair")
    mine = _add_own(chip_sum, _scatter_over_chips(narrow, name=name + "_scatter"), chip, name=name + "_add_chips")
    return _pair_gather(mine, name=name + "_pair_gather").reshape(rows, cols)


def _adamw_at(w, m, v, g_all, first_row, *, name):
    tm = min(PACK_ROW_TILE, w.shape[0])
    assert w.shape[0] % tm == 0 and first_row % tm == 0, (name, w.shape, first_row)

    def fn(w_, m_, v_, g_):
        return _adamw_math(w_, g_, m_, v_) + (g_,)

    return _rowmap(fn, [_t(w), _t(m), _t(v), _t(g_all, first=first_row // tm)], [], [(None, w.shape[1], _F32)] * 4, tm=tm, name=name)


def _step(x, mem, positions, loss_target, w, m, v):
    cx, cy, cc = lax.axis_index("x"), lax.axis_index("y"), lax.axis_index("c")
    chip = 2 * cx + cy
    big = [n for n, _ in _BIG]
    shard_shapes = [w[n].shape for n in big]
    rows = -(-_packed_rows(shard_shapes) // (2 * PACK_ROW_TILE)) * 2 * PACK_ROW_TILE
    flat = lambda a: a.reshape(-1, a.shape[-1])

    ffn_w_gu = _gather_over_chips(jnp.concatenate([flat(w[n]) for n in _FFN_GU]).astype(_BF), name="gather_w_gu")
    ffn_w_down = _gather_over_chips(jnp.concatenate([flat(w[n]) for n in _FFN_DOWN]).astype(_BF), name="gather_w_down")
    w_pack = _pack([w[n] for n in big], rows)
    gathered = _gather_over_chips(w_pack.astype(_BF), name="gather_weights")
    pieces = [_unpack(gathered[k], shard_shapes) for k in range(N_CHIPS)]
    full = {n: jnp.concatenate([pieces[k][i] for k in range(N_CHIPS)], axis=ax) for i, (n, ax) in enumerate(_BIG)}
    for n in _SMALL:
        full[n] = w[n]
    conv = w["a_conv"]
    slots = jnp.stack([jnp.where((chip == k) & (cc == 0), conv, 0.0) for k in range(N_CHIPS)])
    conv_all = _unpack(_all_reduce_small(_pack([slots], _piece_rows(slots.shape)), name="gather_conv"), [slots.shape])[0]
    conv_full = jnp.concatenate([conv_all[k] for k in range(N_CHIPS)], axis=2)

    p = _compute_form(full, conv_full, ffn_w_gu, ffn_w_down)
    cos_t, sin_t = _rope_tables(positions[0])
    loss_tile, d_x, g = _local_step(x[0], mem[0], cos_t, sin_t, loss_target[0], p)
    gn = _natural_grads(g)

    def shard_of(a, ax, k):
        size = a.shape[ax] // N_CHIPS
        return lax.slice_in_dim(a, k * size, (k + 1) * size, axis=ax)

    grads, deltas, new_m, new_v = {}, {}, {}, {}
    for names, key in ((_FFN_GU, "ffn_w_gu"), (_FFN_DOWN, "ffn_w_down")):
        buf = g[key]
        reduced = _reduce_over_devices(buf.reshape(N_CHIPS, -1, buf.shape[-1]), cc, chip, name="grad_" + key)
        first = 0
        for n in names:
            d_, m_, v_, g_ = _adamw_at(flat(w[n]), flat(m[n]), flat(v[n]), reduced, first, name="adamw_" + n)
            grads[n], deltas[n], new_m[n], new_v[n] = (t.reshape(w[n].shape) for t in (g_, d_, m_, v_))
            first += flat(w[n]).shape[0]
    g_pack = jnp.stack([_pack([shard_of(gn[n], ax, k) for n, ax in _BIG], rows) for k in range(N_CHIPS)])
    g_big = _reduce_over_devices(g_pack, cc, chip, name="grad_misc")
    d_big, m_big, v_big = _adamw(w_pack, g_big, _pack([m[n] for n in big], rows), _pack([v[n] for n in big], rows), name="adamw_misc")

    small_shapes = [w[n].shape for n in _SMALL]
    small_rows = _packed_rows(small_shapes)
    g_small = _all_reduce_small(_pack([gn[n] for n in _SMALL], small_rows), name="grad_small")
    d_small, m_small, v_small = _adamw(_pack([w[n] for n in _SMALL], small_rows), g_small, _pack([m[n] for n in _SMALL], small_rows),
                                       _pack([v[n] for n in _SMALL], small_rows), name="adamw_small")

    for out, big_pack, small_pack in ((grads, g_big, g_small), (deltas, d_big, d_small), (new_m, m_big, m_small), (new_v, v_big, v_small)):
        out.update(zip(big, _unpack(big_pack, shard_shapes)))
        out.update(zip(_SMALL, _unpack(small_pack, small_shapes)))
    loss = lax.psum(loss_tile[0, 0], ("x", "y", "c"))
    return (loss, d_x[None], *[grads[n] for n in _WEIGHTS], *[deltas[n] for n in _WEIGHTS], *[new_m[n] for n in _WEIGHTS],
            *[new_v[n] for n in _WEIGHTS])


def kernel(x, mem, positions, ffn1_norm, ffn1_w_gu, ffn1_w_down, mix_norm, ffn2_norm, ffn2_w_gu, ffn2_w_down, w_out, mem_norm, w_mem_kv, a_w_in, a_conv, a_A_log, a_dt_bias, a_out_norm, b_w_in, b_q_norm, b_w_uq, kv_in_norm, w_dkv, kv_lat_norm, w_ukv, final_norm, loss_target, m_ffn1_norm, m_ffn1_w_gu, m_ffn1_w_down, m_mix_norm, m_ffn2_norm, m_ffn2_w_gu, m_ffn2_w_down, m_w_out, m_mem_norm, m_w_mem_kv, m_a_w_in, m_a_conv, m_a_A_log, m_a_dt_bias, m_a_out_norm, m_b_w_in, m_b_q_norm, m_b_w_uq, m_kv_in_norm, m_w_dkv, m_kv_lat_norm, m_w_ukv, m_final_norm, v_ffn1_norm, v_ffn1_w_gu, v_ffn1_w_down, v_mix_norm, v_ffn2_norm, v_ffn2_w_gu, v_ffn2_w_down, v_w_out, v_mem_norm, v_w_mem_kv, v_a_w_in, v_a_conv, v_a_A_log, v_a_dt_bias, v_a_out_norm, v_b_w_in, v_b_q_norm, v_b_w_uq, v_kv_in_norm, v_w_dkv, v_kv_lat_norm, v_w_ukv, v_final_norm):
    given = dict(locals())
    w = {n: given[n] for n in _WEIGHTS}
    m = {n: given["m_" + n] for n in _WEIGHTS}
    v = {n: given["v_" + n] for n in _WEIGHTS}
    return _step(x, mem, positions, loss_target, w, m, v)
```

```python
import functools
import math

import jax
import jax.numpy as jnp
from jax import lax
from jax.experimental import pallas as pl
from jax.experimental.pallas import tpu as pltpu

_BF = jnp.bfloat16
_F32 = jnp.float32
_HI = lax.Precision.HIGHEST
_MESH = pl.DeviceIdType.MESH

D_MODEL = 1024
DEPTH = 4
N_A = 2
N_B = 2
CHUNK = 64
EPS = 1e-6
HEADS = 6
HEAD_DIM = 128
A_WIDTH = HEADS * HEAD_DIM
CONV_K = 4
QK_ROPE = 64
Q_LORA = 256
KV_LORA = 256
N_MEM = 256
MEM_HEADS = 4
MEM_HEAD_DIM = 64
MEM_WIDTH = MEM_HEADS * MEM_HEAD_DIM
D_FF = 2816
ROPE_THETA = 10000.0
A_IN = 4 * A_WIDTH + 2 * HEADS + MEM_WIDTH
A_IN_PAD = 3456
UQ_PAD = 2 * A_WIDTH
DKV_PAD = KV_LORA + 128
LANE = 128
SUBLANE = 8

ADAM_LR = 0.001
ADAM_B1 = 0.9
ADAM_B2 = 0.999
ADAM_EPS = 1e-08
ADAM_WD = 0.01
ADAM_STEP = 10

N_CHIPS = 4
N_DEV = 8
PACK_COLS = 1024
PACK_ROW_TILE = 256


def _pcall(body, **kw):
    return pl.pallas_call(body, **kw)


VMEM_LIMIT_V7X = 48 * 2 ** 20
TILE_BYTES = 6 * 2 ** 20


def _cp(grid_rank):
    return pltpu.CompilerParams(dimension_semantics=("arbitrary",) * grid_rank, vmem_limit_bytes=VMEM_LIMIT_V7X)


def _fit_rows(rows, row_bytes):
    while rows > LANE and rows * row_bytes > TILE_BYTES:
        rows //= 2
    return rows


def _fit_cols(n, target, col_bytes):
    return _tile(n, max(LANE, min(target, TILE_BYTES // col_bytes)))


def _tile(n, target):
    best = None
    for t in range(LANE, min(n, target) + 1, LANE):
        if n % t == 0:
            best = t
    return best if best is not None else n


def _dot(a, b):
    return jnp.dot(a.astype(_BF), b.astype(_BF), preferred_element_type=_F32)


def _dot_nt(a, b):
    return lax.dot_general(a.astype(_BF), b.astype(_BF), (((1,), (1,)), ((), ())), preferred_element_type=_F32)


def _dot_tn(a, b):
    return lax.dot_general(a.astype(_BF), b.astype(_BF), (((0,), (0,)), ((), ())), preferred_element_type=_F32)


def _dot_hi(a, b):
    return jnp.dot(a, b, precision=_HI, preferred_element_type=_F32)


def _rowmap(fn, tiles, params, outs, accs=(), *, tm, name, carry=(), reverse=False):
    rows = tiles[0][0].shape[0]
    steps = rows // tm
    nt, npar, no, na, nc = len(tiles), len(params), len(outs), len(accs), len(carry)

    def step_index(i):
        return steps - 1 - i if reverse else i

    in_specs, operands = [], []
    for arr, r, w, cb, first in tiles:
        r = tm if r is None else r
        w = arr.shape[1] if w is None else w
        assert arr.shape[0] >= (first + steps) * r and (w % LANE == 0 or w == arr.shape[1]), (name, arr.shape, r, w)
        in_specs.append(pl.BlockSpec((r, w), functools.partial(lambda i, cb, first: (first + step_index(i), cb), cb=cb, first=first)))
        operands.append(arr)
    for p in params:
        in_specs.append(pl.BlockSpec(p.shape, functools.partial(lambda i, nd: (0,) * nd, nd=p.ndim)))
        operands.append(p)
    out_specs, out_shape = [], []
    for r, cols, dt in outs:
        r = tm if r is None else r
        out_specs.append(pl.BlockSpec((r, cols), lambda i: (step_index(i), 0)))
        out_shape.append(jax.ShapeDtypeStruct((steps * r, cols), dt))
    for shp, dt in accs:
        out_specs.append(pl.BlockSpec(shp, functools.partial(lambda i, nd: (0,) * nd, nd=len(shp))))
        out_shape.append(jax.ShapeDtypeStruct(shp, dt))

    def body(*refs):
        t_refs = refs[:nt]
        p_refs = refs[nt:nt + npar]
        o_refs = refs[nt + npar:nt + npar + no]
        a_refs = refs[nt + npar + no:nt + npar + no + na]
        c_refs = refs[nt + npar + no + na:]
        if na or nc:
            @pl.when(pl.program_id(0) == 0)
            def _():
                for r in a_refs + c_refs:
                    r[...] = jnp.zeros(r.shape, r.dtype)
        vals = fn(*[r[...] for r in t_refs], *[r[...] for r in p_refs], *[r[...] for r in c_refs])
        vals = tuple(vals) if isinstance(vals, (tuple, list)) else (vals,)
        assert len(vals) == no + na + nc, (name, len(vals), no, na, nc)
        for r, v in zip(o_refs, vals[:no]):
            r[...] = v.astype(r.dtype)
        for r, v in zip(a_refs, vals[no:no + na]):
            r[...] += v.astype(r.dtype)
        for r, v in zip(c_refs, vals[no + na:]):
            r[...] = v.astype(r.dtype)

    res = _pcall(
        body, name=name, grid=(steps,), in_specs=in_specs, out_specs=out_specs, out_shape=out_shape,
        scratch_shapes=[pltpu.VMEM(shp, dt) for shp, dt in carry],
        compiler_params=_cp(1),
    )(*operands)
    return res


def _t(arr, width=None, cb=0, rows=None, first=0):
    return (arr, rows, width, cb, first)


def _mm_nn(a, b, *, out_dtype, name, scale=None, res=None, tm=1024, tn=1536):
    m, k = a.shape
    n = b.shape[1]
    tm, tn = _fit_rows(min(tm, m), k * a.dtype.itemsize), _fit_cols(n, tn, k * b.dtype.itemsize)

    def body(a_ref, b_ref, *rest):
        acc = _dot(a_ref[...], b_ref[...])
        if scale is not None:
            acc = acc * scale
        if res is not None:
            acc = acc + rest[0][...]
        rest[-1][...] = acc.astype(rest[-1].dtype)

    in_specs = [pl.BlockSpec((tm, k), lambda i, j: (i, 0)), pl.BlockSpec((k, tn), lambda i, j: (0, j))]
    operands = [a, b]
    if res is not None:
        in_specs.append(pl.BlockSpec((tm, tn), lambda i, j: (i, j)))
        operands.append(res)
    return _pcall(
        body, name=name, grid=(m // tm, n // tn), in_specs=in_specs,
        out_specs=pl.BlockSpec((tm, tn), lambda i, j: (i, j)), out_shape=jax.ShapeDtypeStruct((m, n), out_dtype),
        compiler_params=_cp(2),
    )(*operands)


def _mm_nt(a, b, *, out_dtype, name, scale=None, tm=1024, tn=1536):
    m, k = a.shape
    n = b.shape[0]
    tm, tn = _fit_rows(min(tm, m), k * a.dtype.itemsize), _fit_cols(n, tn, k * b.dtype.itemsize)

    def body(a_ref, b_ref, o_ref):
        acc = _dot_nt(a_ref[...], b_ref[...])
        if scale is not None:
            acc = acc * scale
        o_ref[...] = acc.astype(o_ref.dtype)

    return _pcall(
        body, name=name, grid=(m // tm, n // tn),
        in_specs=[pl.BlockSpec((tm, k), lambda i, j: (i, 0)), pl.BlockSpec((tn, k), lambda i, j: (j, 0))],
        out_specs=pl.BlockSpec((tm, tn), lambda i, j: (i, j)), out_shape=jax.ShapeDtypeStruct((m, n), out_dtype),
        compiler_params=_cp(2),
    )(a, b)


def _mm_tn(a, b, *, name, scale=None, t1=1024, tn=1536, ts=1024):
    s, k1 = a.shape
    n = b.shape[1]
    t1, tn, ts = _tile(k1, t1), _tile(n, tn), min(ts, s)
    steps = s // ts

    def body(a_ref, b_ref, o_ref):
        @pl.when(pl.program_id(2) == 0)
        def _():
            o_ref[...] = jnp.zeros(o_ref.shape, o_ref.dtype)

        o_ref[...] += _dot_tn(a_ref[...], b_ref[...])
        if scale is not None:
            @pl.when(pl.program_id(2) == steps - 1)
            def _():
                o_ref[...] = o_ref[...] * scale

    return _pcall(
        body, name=name, grid=(k1 // t1, n // tn, steps),
        in_specs=[pl.BlockSpec((ts, t1), lambda i, j, r: (r, i)), pl.BlockSpec((ts, tn), lambda i, j, r: (r, j))],
        out_specs=pl.BlockSpec((t1, tn), lambda i, j, r: (i, j)), out_shape=jax.ShapeDtypeStruct((k1, n), _F32),
        compiler_params=_cp(3),
    )(a, b)


def _heads(t, n, w=LANE):
    return [t[:, w * h:w * (h + 1)] for h in range(n)]


def _cat(parts):
    return jnp.concatenate(parts, axis=1)


def _rms(x, g):
    return x * lax.rsqrt(jnp.mean(x * x, axis=-1, keepdims=True) + EPS) * g


def _l2n(x):
    return x * lax.rsqrt(jnp.sum(x * x, axis=-1, keepdims=True) + EPS)


def _sigmoid(x):
    return 0.5 * (jnp.tanh(0.5 * x) + 1.0)


def _silu(x):
    return x * _sigmoid(x)


def _softplus(x):
    return jnp.maximum(x, 0.0) + jnp.log(1.0 + jnp.exp(-jnp.abs(x)))


def _lane_pick(t, h):
    lane = lax.broadcasted_iota(jnp.int32, t.shape, 1)
    return jnp.sum(jnp.where(lane == h, t, 0.0), axis=1, keepdims=True)


def _lane_put(col, h, width=LANE):
    lane = lax.broadcasted_iota(jnp.int32, (col.shape[0], width), 1)
    return jnp.where(lane == h, col, 0.0)


def _vjp(fwd, ins, cts):
    outs, pull = jax.vjp(fwd, *ins)
    outs = outs if isinstance(outs, (tuple, list)) else (outs,)
    cts = tuple(c.astype(o.dtype) for c, o in zip(cts, outs))
    return pull(cts if len(cts) > 1 else cts[0])


def _rot_half_matrix():
    r = lax.broadcasted_iota(jnp.int32, (LANE, LANE), 0)
    c = lax.broadcasted_iota(jnp.int32, (LANE, LANE), 1)
    half = QK_ROPE // 2
    return jnp.where((c < half) & (r == c + half), -1.0, jnp.where((c >= half) & (c < QK_ROPE) & (r == c - half), 1.0, 0.0))


def _rope(x, cos_t, sin_t):
    return x * cos_t + _dot_hi(x, _rot_half_matrix()) * sin_t


def _mem_attn(q, km, vm):
    lane_q = lax.broadcasted_iota(jnp.int32, q.shape, 1)
    lane_v = lax.broadcasted_iota(jnp.int32, vm.shape, 1)
    out = jnp.zeros(q.shape, _F32)
    for h in range(MEM_HEADS):
        lo, hi = MEM_HEAD_DIM * h, MEM_HEAD_DIM * (h + 1)
        qh = jnp.where((lane_q >= lo) & (lane_q < hi), q, 0.0)
        vh = jnp.where((lane_v >= lo) & (lane_v < hi), vm, 0.0)
        sc = _dot_nt(qh, km) * MEM_HEAD_DIM ** -0.5
        sc = sc - lax.stop_gradient(jnp.max(sc, axis=-1, keepdims=True))
        p = jnp.exp(sc)
        p = p / jnp.sum(p, axis=-1, keepdims=True)
        out = out + _dot(p, vh)
    return out


_PAIR = 2 * CHUNK


def _pair_masks():
    ri = lax.broadcasted_iota(jnp.int32, (_PAIR, _PAIR), 0)
    ci = lax.broadcasted_iota(jnp.int32, (_PAIR, _PAIR), 1)
    same = (ri >= CHUNK) == (ci >= CHUNK)
    return same, same & (ri >= ci), same & (ri > ci), ri == ci, same & (ri <= ci)


_NN = (((2,), (1,)), ((0,), (0,)))
_NT = (((2,), (2,)), ((0,), (0,)))
_TN = (((1,), (1,)), ((0,), (0,)))


def _bdot(a, b, dims):
    return lax.dot_general(a.astype(_BF), b.astype(_BF), dims, preferred_element_type=_F32)


def _dot3(a, b, dims):
    a_hi, b_hi = a.astype(_BF), b.astype(_BF)
    a_lo, b_lo = (a - a_hi.astype(_F32)).astype(_BF), (b - b_hi.astype(_F32)).astype(_BF)
    d = lambda x, y: lax.dot_general(x, y, dims, preferred_element_type=_F32)
    return d(a_hi, b_hi) + (d(a_hi, b_lo) + d(a_lo, b_hi))


@jax.custom_vjp
def _mm3(a, b):
    return _dot3(a, b, _NN)


_mm3.defvjp(lambda a, b: (_dot3(a, b, _NN), (a, b)), lambda res, g: (_dot3(g, res[1], _NT), _dot3(res[0], g, _TN)))


def _neumann_inverse(a):
    eye = jnp.where(_pair_masks()[3], 1.0, 0.0)
    n = -a
    t_inv = eye + n
    for _ in range(5):
        n = _dot3(n, n, _NN)
        t_inv = t_inv + _dot3(t_inv, n, _NN)
    return t_inv


@jax.custom_vjp
def _unit_lower_inverse(a):
    return _neumann_inverse(a)


def _unit_lower_inverse_fwd(a):
    t_inv = _neumann_inverse(a)
    return t_inv, t_inv


def _unit_lower_inverse_bwd(t_inv, g):
    return (-_dot3(t_inv, _dot3(g, t_inv, _NT), _TN),)


_unit_lower_inverse.defvjp(_unit_lower_inverse_fwd, _unit_lower_inverse_bwd)


def _gdn_intra_head(q, k, v, beta, gl):
    same, causal, strict, eye, upper = _pair_masks()
    gl_row = jnp.sum(jnp.where(eye, gl, 0.0), axis=-2, keepdims=True)
    g_col = jnp.sum(jnp.where(causal, gl_row, 0.0), axis=-1, keepdims=True)
    g_row = jnp.sum(jnp.where(upper, gl, 0.0), axis=-2, keepdims=True)
    g_last = jnp.sum(jnp.where(same, gl_row, 0.0), axis=-1, keepdims=True)
    decay = jnp.where(causal, jnp.exp(jnp.where(causal, g_col - g_row, 0.0)), 0.0)
    kb = k * beta
    a = jnp.where(strict, _bdot(kb, k, _NT) * decay, 0.0)
    t_inv = _unit_lower_inverse(a)
    e_g = jnp.exp(g_col)
    u = _mm3(t_inv, v * beta)
    w = _mm3(t_inv, kb * e_g)
    qk = _bdot(q, k, _NT) * decay
    return w, u, q * e_g, k * jnp.exp(g_last - g_col), qk, jnp.exp(g_last)


def _gdn_scan_head(s, qd_a, kd_a, w_a, u_a, qk_a, dc_a, qd_b, kd_b, w_b, u_b, qk_b, dc_b):
    zeros = jnp.zeros((HEADS, CHUNK, HEAD_DIM), _F32)
    vn_a = u_a - _bdot(w_a, s, _NN)
    o_a = _bdot(qd_a, s, _NN) + _bdot(qk_a, jnp.concatenate([vn_a, zeros], axis=1), _NN)
    s1 = s * dc_a + _bdot(kd_a, vn_a, _TN)
    vn_b = u_b - _bdot(w_b, s1, _NN)
    o_b = _bdot(qd_b, s1, _NN) + _bdot(qk_b, jnp.concatenate([zeros, vn_b], axis=1), _NN)
    s2 = s1 * dc_b + _bdot(kd_b, vn_b, _TN)
    return o_a, o_b, s2


def _pick_scalar(t, row, lane_i):
    ri = lax.broadcasted_iota(jnp.int32, t.shape, 0)
    ci = lax.broadcasted_iota(jnp.int32, t.shape, 1)
    return jnp.sum(jnp.sum(jnp.where((ri == row) & (ci == lane_i), t, 0.0), axis=1, keepdims=True), axis=0, keepdims=True)


def _put_scalar(val, row, lane_i, shape):
    ri = lax.broadcasted_iota(jnp.int32, shape, 0)
    ci = lax.broadcasted_iota(jnp.int32, shape, 1)
    return jnp.where((ri == row) & (ci == lane_i), val, 0.0)


def _by_head(t):
    return jnp.stack(_heads(t, HEADS))


def _from_heads(t):
    return _cat([t[h] for h in range(HEADS)])


def _state_by_head(s):
    return jnp.stack([s[HEAD_DIM * h:HEAD_DIM * (h + 1), :] for h in range(HEADS)])


def _scan_ins(qd, kd, w, u, qk, dcrow, state):
    ins = [_state_by_head(state)]
    for r0 in (0, CHUNK):
        rs = slice(r0, r0 + CHUNK)
        ins += [_by_head(t[rs, :]) for t in (qd, kd, w, u, qk)]
        ins.append(jnp.stack([_pick_scalar(dcrow, r0, h) for h in range(HEADS)]))
    return ins


def _gdn_scan_fwd_fn(qd, kd, w, u, qk, dcrow, state):
    o_a, o_b, s2 = _gdn_scan_head(*_scan_ins(qd, kd, w, u, qk, dcrow, state))
    return jnp.concatenate([_from_heads(o_a), _from_heads(o_b)], axis=0), state, s2.reshape(state.shape)


def _gdn_scan_bwd_fn(qd, kd, w, u, qk, dcrow, state, d_o, d_state):
    cts = (_by_head(d_o[0:CHUNK, :]), _by_head(d_o[CHUNK:_PAIR, :]), _state_by_head(d_state))
    g = _vjp(_gdn_scan_head, _scan_ins(qd, kd, w, u, qk, dcrow, state), cts)
    grads = tuple(jnp.concatenate([_from_heads(g[1 + t]), _from_heads(g[7 + t])], axis=0) for t in range(5))
    d_dcrow = sum(_put_scalar(g[6][h], 0, h, dcrow.shape) + _put_scalar(g[12][h], CHUNK, h, dcrow.shape) for h in range(HEADS))
    return grads + (d_dcrow, g[0].reshape(state.shape))


def _gdn_intra_ins(q, k, v, bg):
    return [_by_head(q), _by_head(k), _by_head(v), jnp.stack([_lane_pick(bg, h) for h in range(HEADS)]),
            jnp.stack([_lane_pick(bg, HEADS + h) for h in range(HEADS)])]


def _gdn_intra_fwd_fn(q, k, v, bg):
    res = _gdn_intra_head(*_gdn_intra_ins(q, k, v, bg))
    dcrow = sum(_lane_put(res[5][h], h) for h in range(HEADS))
    return tuple(_from_heads(r) for r in res[:5]) + (dcrow,)


def _gdn_intra_bwd_fn(q, k, v, bg, d_w, d_u, d_qd, d_kd, d_qk, d_dcrow):
    cts = tuple(_by_head(d) for d in (d_w, d_u, d_qd, d_kd, d_qk)) + (jnp.stack([_lane_pick(d_dcrow, h) for h in range(HEADS)]),)
    g = _vjp(_gdn_intra_head, _gdn_intra_ins(q, k, v, bg), cts)
    d_bg = sum(_lane_put(g[3][h], h) + _lane_put(g[4][h], HEADS + h) for h in range(HEADS))
    return tuple(_from_heads(g[t]) for t in range(3)) + (d_bg,)


def _gdn_gates(ba, alog, dtb):
    lane = lax.broadcasted_iota(jnp.int32, ba.shape, 1)
    beta = _sigmoid(ba)
    g = -jnp.exp(alog) * _softplus(ba + dtb)
    return jnp.where(lane < HEADS, beta, jnp.where(lane < 2 * HEADS, g, 0.0))


def _gdn_q_head(c):
    return _l2n(_silu(c)) * HEAD_DIM ** -0.5


def _gdn_k_head(c):
    return _l2n(_silu(c))


def _gdn_prep_fwd_fn(qkv_c, ba, alog, dtb):
    hs = _heads(qkv_c, 3 * HEADS)
    q = _cat([_gdn_q_head(c) for c in hs[:HEADS]])
    k = _cat([_gdn_k_head(c) for c in hs[HEADS:2 * HEADS]])
    v = _cat([_silu(c) for c in hs[2 * HEADS:]])
    return q, k, v, _gdn_gates(ba, alog, dtb)


def _gdn_prep_bwd_fn(qkv_c, ba, d_q, d_k, d_v, d_bg, alog, dtb):
    hs = _heads(qkv_c, 3 * HEADS)
    dqs, dks, dvs = _heads(d_q, HEADS), _heads(d_k, HEADS), _heads(d_v, HEADS)
    parts = [_vjp(_gdn_q_head, [hs[h]], (dqs[h],))[0] for h in range(HEADS)]
    parts += [_vjp(_gdn_k_head, [hs[HEADS + h]], (dks[h],))[0] for h in range(HEADS)]
    parts += [_vjp(_silu, [hs[2 * HEADS + h]], (dvs[h],))[0] for h in range(HEADS)]
    d_ba, d_alog, d_dtb = _vjp(_gdn_gates, [ba, alog, dtb], (d_bg,))
    return _cat(parts), d_ba, d_alog, d_dtb


def _a_out_head(o, gate, gain):
    return _rms(o, gain) * _silu(gate)


def _a_post_fwd_fn(o, gate, qm, gain, mem_kv):
    parts = [_a_out_head(oh, gh, gain) for oh, gh in zip(_heads(o, HEADS), _heads(gate, HEADS))]
    parts.append(_mem_attn(qm, mem_kv[:, :MEM_WIDTH], mem_kv[:, MEM_WIDTH:]))
    return (_cat(parts),)


def _a_post_bwd_fn(o, gate, qm, d_cat, gain, mem_kv):
    d_os, d_gates = [], []
    d_gain = jnp.zeros(gain.shape, _F32)
    dc = _heads(d_cat, HEADS + 2)
    for h, (oh, gh) in enumerate(zip(_heads(o, HEADS), _heads(gate, HEADS))):
        g = _vjp(_a_out_head, [oh, gh, gain], (dc[h],))
        d_os.append(g[0])
        d_gates.append(g[1])
        d_gain = d_gain + g[2]
    d_qm, d_km, d_vm = _vjp(_mem_attn, [qm, mem_kv[:, :MEM_WIDTH], mem_kv[:, MEM_WIDTH:]], (d_cat[:, A_WIDTH:],))
    return _cat(d_os), _cat(d_gates + [d_qm]), d_gain, _cat([d_km, d_vm])


def _b_post_fwd_fn(o, qm, mem_kv):
    return (_cat([o.astype(_F32), _mem_attn(qm, mem_kv[:, :MEM_WIDTH], mem_kv[:, MEM_WIDTH:])]),)


def _b_post_bwd_fn(qm, d_cat_m, mem_kv):
    d_qm, d_km, d_vm = _vjp(_mem_attn, [qm, mem_kv[:, :MEM_WIDTH], mem_kv[:, MEM_WIDTH:]], (d_cat_m,))
    return d_qm, _cat([d_km, d_vm])


ATTN_SCALE = (HEAD_DIM + QK_ROPE) ** -0.5


def _rope_q_fwd_fn(qf, cos_t, sin_t):
    hs = _heads(qf, 2 * HEADS)
    return (_cat(hs[:HEADS] + [_rope(x, cos_t, sin_t) for x in hs[HEADS:]]) * ATTN_SCALE,)


def _rope_q_bwd_fn(d_qn, d_qr, cos_t, sin_t):
    f = lambda x: _rope(x, cos_t, sin_t)
    return (_cat([d_qn] + [_vjp(f, [x], (x,))[0] for x in _heads(d_qr, HEADS)]) * ATTN_SCALE,)


def _kv_prep_fwd_fn(ckr, cos_t, sin_t, gain):
    return _rms(ckr[:, :KV_LORA], gain), _rope(ckr[:, KV_LORA:], cos_t, sin_t)


def _kv_prep_bwd_fn(ckr, d_ckv, d_kr, cos_t, sin_t, gain):
    d_lat, d_gain = _vjp(_rms, [ckr[:, :KV_LORA], gain], (d_ckv,))
    f = lambda x: _rope(x, cos_t, sin_t)
    d_rope = _vjp(f, [ckr[:, KV_LORA:]], (d_kr,))[0]
    return _cat([d_lat, d_rope]), d_gain


def _rms_fwd(x, gain, *, name, tm=1024, out_dtype=_BF):
    tm = min(tm, x.shape[0])
    return _rowmap(lambda x_, g_: (_rms(x_.astype(_F32), g_),), [_t(x)], [gain], [(None, x.shape[1], out_dtype)], tm=tm, name=name)[0]


def _rms_bwd(x, d_xn, d_res, gain, *, name, tm=512):
    tm = min(tm, x.shape[0])

    def fn(x_, dxn_, *rest):
        g_ = rest[-1]
        dx, dg = _vjp(_rms, [x_.astype(_F32), g_], (dxn_.astype(_F32),))
        if d_res is not None:
            dx = dx + rest[0]
        return dx, dg

    tiles = [_t(x), _t(d_xn)] + ([_t(d_res)] if d_res is not None else [])
    return _rowmap(fn, tiles, [gain], [(None, x.shape[1], _F32)], [(gain.shape, _F32)], tm=tm, name=name)


FFN_COL_TILE = 1408
FFN_ROW_TILE = 512
FFN_DXN_ROW_TILE = 256
FFN_DOWN_ROW_TILE = 1024


def _ffn_gate_up(x, gain, w_gu, idx, *, name):
    s = x.shape[0]
    tm, tf = min(FFN_ROW_TILE, s), FFN_COL_TILE
    nf = D_FF // tf

    def body(x_ref, gain_ref, wg_ref, wu_ref, xn_ref, g_ref, u_ref, a_ref):
        @pl.when(pl.program_id(1) == 0)
        def _():
            xn_ref[...] = _rms(x_ref[...], gain_ref[...]).astype(xn_ref.dtype)

        xn = xn_ref[...]
        g, u = _dot(xn, wg_ref[...]), _dot(xn, wu_ref[...])
        g_ref[...] = g.astype(g_ref.dtype)
        u_ref[...] = u.astype(u_ref.dtype)
        a_ref[...] = (_silu(g) * u).astype(a_ref.dtype)

    col = pl.BlockSpec((tm, tf), lambda i, j: (i, j))
    wide = jax.ShapeDtypeStruct((s, D_FF), _BF)
    return _pcall(
        body, name=name, grid=(s // tm, nf),
        in_specs=[pl.BlockSpec((tm, D_MODEL), lambda i, j: (i, 0)), pl.BlockSpec((1, D_MODEL), lambda i, j: (0, 0)),
                  pl.BlockSpec((None, D_MODEL, tf), lambda i, j: (j, idx, 0)), pl.BlockSpec((None, D_MODEL, tf), lambda i, j: (nf + j, idx, 0))],
        out_specs=[pl.BlockSpec((tm, D_MODEL), lambda i, j: (i, 0)), col, col, col],
        out_shape=[jax.ShapeDtypeStruct((s, D_MODEL), _BF), wide, wide, wide],
        compiler_params=_cp(2),
    )(x, gain, w_gu, w_gu)


FFN_SHARD_ROWS = D_FF // N_CHIPS


def _w_down_specs(idx):
    return [pl.BlockSpec((None, FFN_SHARD_ROWS, D_MODEL), functools.partial(lambda i, j, q: (2 * j + q, idx, 0), q=q)) for q in (0, 1)]


def _ffn_down(a, w_down, idx, x, *, name):
    s = a.shape[0]
    tm = min(FFN_DOWN_ROW_TILE, s)

    def body(a_ref, w0_ref, w1_ref, w2_ref, w3_ref, x_ref, o_ref):
        w = jnp.concatenate([w0_ref[...], w1_ref[...], w2_ref[...], w3_ref[...]], axis=0)
        o_ref[...] = x_ref[...] + 0.5 * _dot(a_ref[...], w)

    rows = pl.BlockSpec((tm, D_MODEL), lambda i: (i, 0))
    w_specs = [pl.BlockSpec((None, FFN_SHARD_ROWS, D_MODEL), functools.partial(lambda i, k: (k, idx, 0), k=k)) for k in range(N_CHIPS)]
    return _pcall(
        body, name=name, grid=(s // tm,), in_specs=[pl.BlockSpec((tm, D_FF), lambda i: (i, 0))] + w_specs + [rows],
        out_specs=rows, out_shape=jax.ShapeDtypeStruct((s, D_MODEL), _F32), compiler_params=_cp(1),
    )(a, w_down, w_down, w_down, w_down, x)


def _ffn_fwd(x, gain, w_gu, w_down, idx, *, name):
    xn, g, u, a = _ffn_gate_up(x, gain, w_gu, idx, name=name + "_gu")
    y = _ffn_down(a, w_down, idx, x, name=name + "_down")
    return y, (x, xn, g, u, a)


def _ffn_d_gate_up(d_y, g, u, w_down, idx, *, name):
    s = d_y.shape[0]
    tm, tf = min(FFN_ROW_TILE, s), FFN_COL_TILE

    def body(dy_ref, wa_ref, wb_ref, g_ref, u_ref, dg_ref, du_ref):
        da = _dot_nt(dy_ref[...], jnp.concatenate([wa_ref[...], wb_ref[...]], axis=0)) * 0.5
        gg, uu = g_ref[...].astype(_F32), u_ref[...].astype(_F32)
        sg = _sigmoid(gg)
        dg_ref[...] = (da * uu * sg * (1.0 + gg * (1.0 - sg))).astype(dg_ref.dtype)
        du_ref[...] = (da * gg * sg).astype(du_ref.dtype)

    col = pl.BlockSpec((tm, tf), lambda i, j: (i, j))
    wide = jax.ShapeDtypeStruct((s, D_FF), _BF)
    return _pcall(
        body, name=name, grid=(s // tm, D_FF // tf),
        in_specs=[pl.BlockSpec((tm, D_MODEL), lambda i, j: (i, 0))] + _w_down_specs(idx) + [col, col],
        out_specs=[col, col], out_shape=[wide, wide], compiler_params=_cp(2),
    )(d_y, w_down, w_down, g, u)


def _ffn_d_x(d_g, d_u, w_gu, idx, x, d_y, gain, *, name):
    s = x.shape[0]
    tm, tf = min(FFN_DXN_ROW_TILE, s), FFN_COL_TILE

    def body(dg_ref, du_ref, w0_ref, w1_ref, w2_ref, w3_ref, x_ref, dy_ref, gain_ref, dx_ref, dgain_ref):
        @pl.when(pl.program_id(0) == 0)
        def _():
            dgain_ref[...] = jnp.zeros(dgain_ref.shape, _F32)

        d_xn = (_dot_nt(dg_ref[:, 0:tf], w0_ref[...]) + _dot_nt(dg_ref[:, tf:2 * tf], w1_ref[...])
                + _dot_nt(du_ref[:, 0:tf], w2_ref[...]) + _dot_nt(du_ref[:, tf:2 * tf], w3_ref[...]))
        dx, dgain = _vjp(_rms, [x_ref[...], gain_ref[...]], (d_xn,))
        dx_ref[...] = dx + dy_ref[...]
        dgain_ref[...] += dgain

    wide = pl.BlockSpec((tm, D_FF), lambda i: (i, 0))
    rows = pl.BlockSpec((tm, D_MODEL), lambda i: (i, 0))
    one = pl.BlockSpec((1, D_MODEL), lambda i: (0, 0))
    w_specs = [pl.BlockSpec((None, D_MODEL, tf), functools.partial(lambda i, k: (k, idx, 0), k=k)) for k in range(N_CHIPS)]
    return _pcall(
        body, name=name, grid=(s // tm,),
        in_specs=[wide, wide] + w_specs + [rows, rows, one],
        out_specs=[rows, one], out_shape=[jax.ShapeDtypeStruct((s, D_MODEL), _F32), jax.ShapeDtypeStruct((1, D_MODEL), _F32)],
        compiler_params=_cp(1),
    )(d_g, d_u, w_gu, w_gu, w_gu, w_gu, x, d_y, gain)


def _ffn_d_w_gu(xn, d_act, into, idx, first_chip, *, name, ts=1024):
    s = xn.shape[0]
    ts = min(ts, s)
    steps = s // ts

    def body(a_ref, b_ref, into_ref, o_ref):
        @pl.when(pl.program_id(1) == 0)
        def _():
            o_ref[...] = jnp.zeros(o_ref.shape, o_ref.dtype)

        o_ref[...] += _dot_tn(a_ref[...], b_ref[...])

    return _pcall(
        body, name=name, grid=(D_FF // FFN_COL_TILE, steps),
        in_specs=[pl.BlockSpec((ts, D_MODEL), lambda j, r: (r, 0)), pl.BlockSpec((ts, FFN_COL_TILE), lambda j, r: (r, j)),
                  pl.BlockSpec(memory_space=pl.ANY)],
        out_specs=pl.BlockSpec((None, None, D_MODEL, FFN_COL_TILE), lambda j, r: (first_chip + j, idx, 0, 0)),
        out_shape=jax.ShapeDtypeStruct(into.shape, into.dtype), input_output_aliases={2: 0}, compiler_params=_cp(2),
    )(xn, d_act, into)


def _ffn_d_w_down(a, d_y, into, idx, *, name, ts=1024):
    s = a.shape[0]
    ts = min(ts, s)
    steps = s // ts

    def body(a_ref, b_ref, into_ref, o_ref):
        @pl.when(pl.program_id(1) == 0)
        def _():
            o_ref[...] = jnp.zeros(o_ref.shape, o_ref.dtype)

        part = _dot_tn(a_ref[...], b_ref[...]) * 0.5
        o_ref[0] += part[0:FFN_SHARD_ROWS, :]
        o_ref[1] += part[FFN_SHARD_ROWS:2 * FFN_SHARD_ROWS, :]

    return _pcall(
        body, name=name, grid=(D_FF // FFN_COL_TILE, steps),
        in_specs=[pl.BlockSpec((ts, FFN_COL_TILE), lambda i, r: (r, i)), pl.BlockSpec((ts, D_MODEL), lambda i, r: (r, 0)),
                  pl.BlockSpec(memory_space=pl.ANY)],
        out_specs=pl.BlockSpec((2, None, FFN_SHARD_ROWS, D_MODEL), lambda i, r: (i, idx, 0, 0)),
        out_shape=jax.ShapeDtypeStruct(into.shape, into.dtype), input_output_aliases={2: 0}, compiler_params=_cp(2),
    )(a, d_y, into)


def _ffn_bwd(d_y, saved, gain, w_gu, w_down, idx, g_gu, g_down, *, name):
    x, xn, g, u, a = saved
    d_g, d_u = _ffn_d_gate_up(d_y, g, u, w_down, idx, name=name + "_dgu")
    g_down = _ffn_d_w_down(a, d_y, g_down, idx, name=name + "_dwd")
    g_gu = _ffn_d_w_gu(xn, d_g, g_gu, idx, 0, name=name + "_dwg")
    g_gu = _ffn_d_w_gu(xn, d_u, g_gu, idx, 2, name=name + "_dwu")
    d_x, d_gain = _ffn_d_x(d_g, d_u, w_gu, idx, x, d_y, gain, name=name + "_dx")
    return d_x, d_gain, g_gu, g_down


def _conv_fwd(h, w, *, name, tm=256):
    s = h.shape[0]
    tm = min(tm, s)
    c = 3 * A_WIDTH
    halo = SUBLANE

    def body(x_ref, prev_ref, w_ref, o_ref, buf):
        i = pl.program_id(0)
        buf[0:halo, :] = jnp.where(i == 0, 0.0, prev_ref[...])
        buf[halo:halo + tm, :] = x_ref[...]
        acc = jnp.zeros((tm, c), _F32)
        for j in range(CONV_K):
            acc = acc + buf[pl.ds(halo - (CONV_K - 1) + j, tm), :] * w_ref[j:j + 1, :]
        o_ref[...] = acc

    return _pcall(
        body, name=name, grid=(s // tm,),
        in_specs=[pl.BlockSpec((tm, c), lambda i: (i, 0)),
                  pl.BlockSpec((halo, c), lambda i: (jnp.maximum(i * (tm // halo) - 1, 0), 0)),
                  pl.BlockSpec(w.shape, lambda i: (0, 0))],
        out_specs=pl.BlockSpec((tm, c), lambda i: (i, 0)), out_shape=jax.ShapeDtypeStruct((s, c), _F32),
        scratch_shapes=[pltpu.VMEM((tm + 2 * halo, c), _F32)],
        compiler_params=_cp(1),
    )(h, h, w)


def _conv_bwd(h, d_y, w, *, name, tm=256):
    s = h.shape[0]
    tm = min(tm, s)
    c = 3 * A_WIDTH
    halo = SUBLANE
    steps = s // tm

    def body(x_ref, prev_ref, dy_ref, next_ref, w_ref, dx_ref, dw_ref, xbuf, dybuf):
        i = pl.program_id(0)

        @pl.when(i == 0)
        def _():
            dw_ref[...] = jnp.zeros(dw_ref.shape, dw_ref.dtype)

        xbuf[0:halo, :] = jnp.where(i == 0, 0.0, prev_ref[...])
        xbuf[halo:halo + tm, :] = x_ref[...]
        dybuf[0:tm, :] = dy_ref[...]
        dybuf[tm:tm + halo, :] = jnp.where(i == steps - 1, 0.0, next_ref[...])
        dy = dy_ref[...]
        acc = jnp.zeros((tm, c), _F32)
        for j in range(CONV_K):
            acc = acc + dybuf[pl.ds(CONV_K - 1 - j, tm), :] * w_ref[j:j + 1, :]
            dw_ref[j:j + 1, :] += jnp.sum(dy * xbuf[pl.ds(halo - (CONV_K - 1) + j, tm), :], axis=0, keepdims=True)
        dx_ref[...] = acc

    return _pcall(
        body, name=name, grid=(steps,),
        in_specs=[pl.BlockSpec((tm, c), lambda i: (i, 0)),
                  pl.BlockSpec((halo, c), lambda i: (jnp.maximum(i * (tm // halo) - 1, 0), 0)),
                  pl.BlockSpec((tm, c), lambda i: (i, 0)),
                  pl.BlockSpec((halo, c), lambda i: (jnp.minimum((i + 1) * (tm // halo), s // halo - 1), 0)),
                  pl.BlockSpec(w.shape, lambda i: (0, 0))],
        out_specs=[pl.BlockSpec((tm, c), lambda i: (i, 0)), pl.BlockSpec(w.shape, lambda i: (0, 0))],
        out_shape=[jax.ShapeDtypeStruct((s, c), _F32), jax.ShapeDtypeStruct(w.shape, _F32)],
        scratch_shapes=[pltpu.VMEM((tm + 2 * halo, c), _F32), pltpu.VMEM((tm + 2 * halo, c), _F32)],
        compiler_params=_cp(1),
    )(h, h, d_y, d_y, w)


ATTN_Q_BLOCK = 2048
ATTN_K_SUB = 256
ATTN_DQ_BLOCK = 2048
ATTN_BWD_BLOCK = 1024
ATTN_BWD_SUB = 512


def _chunk_mask(shape, q_axis):
    qi = lax.broadcasted_iota(jnp.int32, shape, q_axis) // CHUNK
    ki = lax.broadcasted_iota(jnp.int32, shape, 1 - q_axis) // CHUNK
    return ki <= qi


def _rows(j, t):
    return pl.ds(pl.multiple_of(j * t, t), t)


def _chunk_mask_at(shape, q_axis, q_off):
    qi = (lax.broadcasted_iota(jnp.int32, shape, q_axis) + q_off) // CHUNK
    ki = lax.broadcasted_iota(jnp.int32, shape, 1 - q_axis) // CHUNK
    return ki <= qi


def _attn_fwd(q_all, kv, kr, *, name):
    s = q_all.shape[0]
    t = min(ATTN_Q_BLOCK, s)
    tk = min(ATTN_K_SUB, t)
    nq, sub, rep = s // t, t // tk, tk // LANE

    def body(qn_ref, qr_ref, kn_ref, kr_ref, v_ref, o_ref, lse_ref, m_sc, acc_sc):
        i = pl.program_id(1)
        m_sc[...] = jnp.full(m_sc.shape, -1e30, _F32)
        acc_sc[...] = jnp.zeros(acc_sc.shape, _F32)
        ones = jnp.ones((tk, LANE), _BF)

        def block(j, first_row):
            qs = slice(first_row, t)
            rows = _rows(j, tk)
            q = _cat([qn_ref[qs, :], qr_ref[qs, :]])
            sc = _dot_nt(q, _cat([kn_ref[rows, :], kr_ref[rows, :]]))
            if first_row is not None:
                sc = jnp.where(_chunk_mask(sc.shape, 0), sc, -1e30)
            m_prev = m_sc[qs, :]
            m_new = jnp.maximum(m_prev, jnp.max(sc, axis=-1, keepdims=True))
            alpha = jnp.exp(m_prev - m_new)
            p = jnp.exp(sc - _cat([m_new] * rep))
            acc_sc[qs, :] = _cat([alpha, alpha]) * acc_sc[qs, :] + _dot(p, _cat([v_ref[rows, :], ones]))
            m_sc[qs, :] = m_new

        def step(j, carry):
            block(j, None)
            return carry

        lax.fori_loop(0, i * sub, step, 0)
        for u in range(sub):
            block(i * sub + u, u * tk)
        row_sum = acc_sc[:, LANE:2 * LANE]
        o_ref[...] = acc_sc[:, 0:LANE] / row_sum
        lse_ref[...] = m_sc[...] + jnp.log(row_sum)

    return _pcall(
        body, name=name, grid=(HEADS, nq),
        in_specs=[pl.BlockSpec((t, LANE), lambda h, i: (i, h)),
                  pl.BlockSpec((t, LANE), lambda h, i: (i, HEADS + h)),
                  pl.BlockSpec((s, LANE), lambda h, i: (0, h)),
                  pl.BlockSpec((s, LANE), lambda h, i: (0, 0)),
                  pl.BlockSpec((s, LANE), lambda h, i: (0, HEADS + h))],
        out_specs=[pl.BlockSpec((t, LANE), lambda h, i: (i, h)), pl.BlockSpec((t, LANE), lambda h, i: (i, h))],
        out_shape=[jax.ShapeDtypeStruct((s, A_WIDTH), _F32), jax.ShapeDtypeStruct((s, A_WIDTH), _F32)],
        scratch_shapes=[pltpu.VMEM((t, LANE), _F32), pltpu.VMEM((t, 2 * LANE), _F32)],
        compiler_params=_cp(2),
    )(q_all, q_all, kv, kr, kv)


def _attn_bwd_prep(o, lse, d_cat, *, name):
    s = o.shape[0]
    t = min(ATTN_BWD_SUB, s)
    nq = s // t

    def body(o_ref, lse_ref, do_ref, dob_ref, dl_ref, lset_ref, dlt_ref):
        for h in range(HEADS):
            sl = slice(LANE * h, LANE * (h + 1))
            rows = slice(SUBLANE * h, SUBLANE * (h + 1))
            do = do_ref[:, sl]
            dl = jnp.broadcast_to(jnp.sum(o_ref[:, sl] * do, axis=-1, keepdims=True), (t, LANE))
            dob_ref[:, sl] = do.astype(dob_ref.dtype)
            dl_ref[:, sl] = dl
            dlt_ref[rows, :] = dl.T[0:SUBLANE, :]
            lset_ref[rows, :] = lse_ref[:, sl].T[0:SUBLANE, :]

    wide = pl.BlockSpec((t, A_WIDTH), lambda i: (i, 0))
    stat = pl.BlockSpec((HEADS * SUBLANE, t), lambda i: (i, 0))
    stat_shape = jax.ShapeDtypeStruct((nq * HEADS * SUBLANE, t), _F32)
    return _pcall(
        body, name=name, grid=(nq,), in_specs=[wide, wide, wide], out_specs=[wide, wide, stat, stat],
        out_shape=[jax.ShapeDtypeStruct((s, A_WIDTH), _BF), jax.ShapeDtypeStruct((s, A_WIDTH), _F32), stat_shape, stat_shape],
        compiler_params=_cp(1),
    )(o, lse, d_cat)


def _attn_dq(q_all, kv, kr, lse, delta, d_o, *, name):
    s = q_all.shape[0]
    t = min(ATTN_DQ_BLOCK, s)
    tk = min(ATTN_BWD_SUB, t)
    nq, sub, rep = s // t, t // tk, tk // LANE

    def body(qn_ref, qr_ref, kn_ref, kr_ref, v_ref, lse_ref, dl_ref, do_ref, dqn_ref, dqr_ref, acc_sc):
        i = pl.program_id(1)
        acc_sc[...] = jnp.zeros(acc_sc.shape, _F32)

        def block(j, first_row):
            qs = slice(first_row, t)
            rows = _rows(j, tk)
            q = _cat([qn_ref[qs, :], qr_ref[qs, :]])
            k = _cat([kn_ref[rows, :], kr_ref[rows, :]])
            p = jnp.exp(_dot_nt(q, k) - _cat([lse_ref[qs, :]] * rep))
            if first_row is not None:
                p = jnp.where(_chunk_mask(p.shape, 0), p, 0.0)
            ds = p * (_dot_nt(do_ref[qs, :], v_ref[rows, :]) - _cat([dl_ref[qs, :]] * rep))
            acc_sc[qs, :] += _dot(ds, k)

        def step(j, carry):
            block(j, None)
            return carry

        lax.fori_loop(0, i * sub, step, 0)
        for u in range(sub):
            block(i * sub + u, u * tk)
        dqn_ref[...] = acc_sc[:, 0:LANE]
        dqr_ref[...] = acc_sc[:, LANE:2 * LANE]

    return _pcall(
        body, name=name, grid=(HEADS, nq),
        in_specs=[pl.BlockSpec((t, LANE), lambda h, i: (i, h)),
                  pl.BlockSpec((t, LANE), lambda h, i: (i, HEADS + h)),
                  pl.BlockSpec((s, LANE), lambda h, i: (0, h)),
                  pl.BlockSpec((s, LANE), lambda h, i: (0, 0)),
                  pl.BlockSpec((s, LANE), lambda h, i: (0, HEADS + h)),
                  pl.BlockSpec((t, LANE), lambda h, i: (i, h)),
                  pl.BlockSpec((t, LANE), lambda h, i: (i, h)),
                  pl.BlockSpec((t, LANE), lambda h, i: (i, h))],
        out_specs=[pl.BlockSpec((t, LANE), lambda h, i: (i, h)), pl.BlockSpec((t, LANE), lambda h, i: (i, h))],
        out_shape=[jax.ShapeDtypeStruct((s, A_WIDTH), _F32), jax.ShapeDtypeStruct((s, A_WIDTH), _F32)],
        scratch_shapes=[pltpu.VMEM((t, 2 * LANE), _F32)],
        compiler_params=_cp(2),
    )(q_all, q_all, kv, kr, kv, lse, delta, d_o)


def _attn_dkv(q_all, kv, kr, lse_t, delta_t, d_o, *, name):
    s = q_all.shape[0]
    t = min(ATTN_BWD_BLOCK, s)
    tq = min(ATTN_BWD_SUB, t)
    nk, sub, nqs = s // t, t // tq, s // tq

    def body(kn_ref, kr_ref, v_ref, qn_ref, qr_ref, do_ref, lset_ref, dlt_ref, dkn_ref, dv_ref, dkr_ref, dk_sc, dv_sc):
        h, j = pl.program_id(0), pl.program_id(1)
        dk_sc[...] = jnp.zeros(dk_sc.shape, _F32)
        dv_sc[...] = jnp.zeros(dv_sc.shape, _F32)

        def block(i, query_off):
            ks = slice(0, t if query_off is None else query_off + tq)
            rows = _rows(i, tq)
            stat = pl.ds(pl.multiple_of((i * HEADS + h) * SUBLANE, SUBLANE), 1)
            q = _cat([qn_ref[rows, :], qr_ref[rows, :]])
            do = do_ref[rows, :]
            p = jnp.exp(_dot_nt(_cat([kn_ref[ks, :], kr_ref[ks, :]]), q) - lset_ref[stat, :])
            if query_off is not None:
                p = jnp.where(_chunk_mask_at(p.shape, 1, query_off), p, 0.0)
            dv_sc[ks, :] += _dot(p, do)
            ds = p * (_dot_nt(v_ref[ks, :], do) - dlt_ref[stat, :])
            dk_sc[ks, :] += _dot(ds, q)

        def step(i, carry):
            block(i, None)
            return carry

        for u in range(sub):
            block(j * sub + u, u * tq)
        lax.fori_loop((j + 1) * sub, nqs, step, 0)
        dkn_ref[...] = dk_sc[:, 0:LANE]
        dkr_ref[...] = dk_sc[:, LANE:2 * LANE]
        dv_ref[...] = dv_sc[...]

    stats = pl.BlockSpec((nqs * HEADS * SUBLANE, tq), lambda h, j: (0, 0))
    return _pcall(
        body, name=name, grid=(HEADS, nk),
        in_specs=[pl.BlockSpec((t, LANE), lambda h, j: (j, h)),
                  pl.BlockSpec((t, LANE), lambda h, j: (j, 0)),
                  pl.BlockSpec((t, LANE), lambda h, j: (j, HEADS + h)),
                  pl.BlockSpec((s, LANE), lambda h, j: (0, h)),
                  pl.BlockSpec((s, LANE), lambda h, j: (0, HEADS + h)),
                  pl.BlockSpec((s, LANE), lambda h, j: (0, h)),
                  stats, stats],
        out_specs=[pl.BlockSpec((t, LANE), lambda h, j: (j, h))] * 3,
        out_shape=[jax.ShapeDtypeStruct((s, A_WIDTH), _F32)] * 3,
        scratch_shapes=[pltpu.VMEM((t, 2 * LANE), _F32), pltpu.VMEM((t, LANE), _F32)],
        compiler_params=_cp(2),
    )(kv, kr, kv, q_all, q_all, d_o, lse_t, delta_t)


def _final_loss(x, tgt, gain, *, name, tm=512):
    tm = min(tm, x.shape[0])

    def fn(x_, t_, g_):
        def f(xx, gg):
            err = _rms(xx, gg) - t_
            return 0.5 * jnp.sum(jnp.sum(err * err, axis=1, keepdims=True) / D_MODEL, axis=0, keepdims=True)

        loss, pull = jax.vjp(f, x_, g_)
        dx, dg = pull(jnp.ones((1, 1), _F32))
        return dx, dg, jnp.broadcast_to(loss, (SUBLANE, LANE))

    return _rowmap(fn, [_t(x), _t(tgt)], [gain], [(None, D_MODEL, _F32)], [(gain.shape, _F32), ((SUBLANE, LANE), _F32)], tm=tm, name=name)


def _local_step(x, mem, cos_t, sin_t, tgt, p):
    s = x.shape[0]
    g = {}
    row = lambda a: a.reshape(1, -1)
    tm_e = min(256, s)

    mem_n = _rms_fwd(mem, row(p["mem_norm"]), name="mem_norm")
    mem_kv_all = _mm_nn(mem_n, p["w_mem_all"], out_dtype=_F32, name="mem_kv", tn=1024)
    mem_kv = [mem_kv_all[:, 2 * MEM_WIDTH * l:2 * MEM_WIDTH * (l + 1)] for l in range(DEPTH)]

    sv = []
    for i in range(N_A):
        l = i
        r = {}
        r["x0"] = x
        x, r["ffn1"] = _ffn_fwd(x, row(p["ffn1_norm"][l]), p["ffn_w_gu"], p["ffn_w_down"], l, name=f"a{i}_ffn1")
        r["x1"] = x
        xn = _rms_fwd(x, row(p["mix_norm"][l]), name=f"a{i}_mixnorm")
        h = _mm_nn(xn, p["a_w_in"][i], out_dtype=_F32, name=f"a{i}_in", tn=1152)
        qkv_c = _conv_fwd(h, p["a_conv"][i], name=f"a{i}_conv")
        alog, dtb = p["a_A_log_row"][i], p["a_dt_bias_row"][i]
        q, k, v, bg = _rowmap(_gdn_prep_fwd_fn, [_t(qkv_c), _t(h, LANE, 26)], [alog, dtb],
                              [(None, A_WIDTH, _F32)] * 3 + [(None, LANE, _F32)], tm=tm_e, name=f"a{i}_prep")
        w_, u_, qd, kd, qk, dcrow = _rowmap(_gdn_intra_fwd_fn, [_t(q), _t(k), _t(v), _t(bg)], [],
                                            [(None, A_WIDTH, _F32)] * 5 + [(None, LANE, _F32)], tm=_PAIR, name=f"a{i}_intra")
        o, states = _rowmap(_gdn_scan_fwd_fn, [_t(qd), _t(kd), _t(w_), _t(u_), _t(qk), _t(dcrow)], [],
                            [(None, A_WIDTH, _F32), (A_WIDTH, HEAD_DIM, _F32)], tm=_PAIR, name=f"a{i}_scan",
                            carry=[((A_WIDTH, HEAD_DIM), _F32)])
        gain_o = row(p["a_out_norm"][i])
        cat = _rowmap(_a_post_fwd_fn, [_t(o), _t(h, A_WIDTH, 3), _t(h, MEM_WIDTH, 12)], [gain_o, mem_kv[l]],
                      [(None, D_MODEL, _BF)], tm=tm_e, name=f"a{i}_post")[0]
        x = _mm_nn(cat, p["w_out"][l], out_dtype=_F32, name=f"a{i}_out", res=x, tn=1024)
        r.update(xn=xn, h=h, qkv_c=qkv_c, q=q, k=k, v=v, bg=bg, w=w_, u=u_, qd=qd, kd=kd, qk=qk, dcrow=dcrow, o=o, states=states, cat=cat)
        r["x2"] = x
        x, r["ffn2"] = _ffn_fwd(x, row(p["ffn2_norm"][l]), p["ffn_w_gu"], p["ffn_w_down"], DEPTH + l, name=f"a{i}_ffn2")
        sv.append(r)

    kvs = {"x": x}
    xn_kv = _rms_fwd(x, row(p["kv_in_norm"]), name="kv_innorm")
    ckr = _mm_nn(xn_kv, p["w_dkv"], out_dtype=_F32, name="kv_down")
    ckv, k_rope = _rowmap(_kv_prep_fwd_fn, [_t(ckr), _t(cos_t), _t(sin_t)], [row(p["kv_lat_norm"])],
                          [(None, KV_LORA, _BF), (None, LANE, _BF)], tm=tm_e, name="kv_prep")
    kvu = _mm_nn(ckv, p["w_ukv"], out_dtype=_BF, name="kv_up")
    kvs.update(xn=xn_kv, ckr=ckr, ckv=ckv)

    for j in range(N_B):
        l = N_A + j
        r = {}
        x, r["ffn1"] = _ffn_fwd(x, row(p["ffn1_norm"][l]), p["ffn_w_gu"], p["ffn_w_down"], l, name=f"b{j}_ffn1")
        r["x1"] = x
        xn = _rms_fwd(x, row(p["mix_norm"][l]), name=f"b{j}_mixnorm")
        h = _mm_nn(xn, p["b_w_in"][j], out_dtype=_F32, name=f"b{j}_in")
        gain_q = row(p["b_q_norm"][j])
        cqn = _rowmap(lambda c_, g_: (_rms(c_, g_),), [_t(h, Q_LORA, 0)], [gain_q], [(None, Q_LORA, _BF)], tm=tm_e, name=f"b{j}_qnorm")[0]
        qf = _mm_nn(cqn, p["b_w_uq"][j], out_dtype=_F32, name=f"b{j}_uq")
        q_all = _rowmap(_rope_q_fwd_fn, [_t(qf), _t(cos_t), _t(sin_t)], [], [(None, UQ_PAD, _BF)], tm=tm_e, name=f"b{j}_rope")[0]
        o_b, lse = _attn_fwd(q_all, kvu, k_rope, name=f"b{j}_attn")
        cat = _rowmap(_b_post_fwd_fn, [_t(o_b), _t(h, MEM_WIDTH, 1)], [mem_kv[l]], [(None, D_MODEL, _BF)], tm=tm_e, name=f"b{j}_post")[0]
        x = _mm_nn(cat, p["w_out"][l], out_dtype=_F32, name=f"b{j}_out", res=x, tn=1024)
        r.update(xn=xn, h=h, cqn=cqn, q_all=q_all, o_b=o_b, lse=lse, cat=cat)
        x, r["ffn2"] = _ffn_fwd(x, row(p["ffn2_norm"][l]), p["ffn_w_gu"], p["ffn_w_down"], DEPTH + l, name=f"b{j}_ffn2")
        sv.append(r)

    dx, g["final_norm"], loss = _final_loss(x, tgt, row(p["final_norm"]), name="loss")

    per_layer = lambda: [None] * DEPTH
    for n in ("ffn1_norm", "mix_norm", "ffn2_norm", "w_out", "mem_kv"):
        g[n] = per_layer()
    g["ffn_w_gu"] = jnp.zeros((N_CHIPS, 2 * DEPTH, D_MODEL, FFN_COL_TILE), _F32)
    g["ffn_w_down"] = jnp.zeros((N_CHIPS, 2 * DEPTH, FFN_SHARD_ROWS, D_MODEL), _F32)
    for n in ("a_w_in", "a_conv", "a_A_log_row", "a_dt_bias_row", "a_out_norm", "b_w_in", "b_q_norm", "b_w_uq"):
        g[n] = [None] * N_A
    d_kv_parts = []

    for j in reversed(range(N_B)):
        l = N_A + j
        r = sv[l]
        dx, g["ffn2_norm"][l], g["ffn_w_gu"], g["ffn_w_down"] = _ffn_bwd(
            dx, r["ffn2"], row(p["ffn2_norm"][l]), p["ffn_w_gu"], p["ffn_w_down"], DEPTH + l, g["ffn_w_gu"], g["ffn_w_down"], name=f"b{j}_ffn2b")
        d_cat = _mm_nt(dx, p["w_out"][l], out_dtype=_F32, name=f"b{j}_dcat", tn=1024)
        g["w_out"][l] = _mm_tn(r["cat"], dx, name=f"b{j}_dwout", tn=1024)
        d_qm, g["mem_kv"][l] = _rowmap(_b_post_bwd_fn, [_t(r["h"], MEM_WIDTH, 1), _t(d_cat, MEM_WIDTH, 3)], [mem_kv[l]],
                                      [(None, MEM_WIDTH, _F32)], [((N_MEM, 2 * MEM_WIDTH), _F32)], tm=tm_e, name=f"b{j}_postb")
        d_o, delta, lse_t, delta_t = _attn_bwd_prep(r["o_b"], r["lse"], d_cat, name=f"b{j}_delta")
        dqn, dqr = _attn_dq(r["q_all"], kvu, k_rope, r["lse"], delta, d_o, name=f"b{j}_attn_dq")
        d_kv_parts.append(_attn_dkv(r["q_all"], kvu, k_rope, lse_t, delta_t, d_o, name=f"b{j}_attn_dkv"))
        d_qf = _rowmap(_rope_q_bwd_fn, [_t(dqn), _t(dqr), _t(cos_t), _t(sin_t)], [], [(None, UQ_PAD, _F32)], tm=tm_e, name=f"b{j}_ropeb")[0]
        d_cqn = _mm_nt(d_qf, p["b_w_uq"][j], out_dtype=_F32, name=f"b{j}_dcqn")
        g["b_w_uq"][j] = _mm_tn(r["cqn"], d_qf, name=f"b{j}_dwuq")
        gain_q = row(p["b_q_norm"][j])
        d_cq, g["b_q_norm"][j] = _rowmap(lambda c_, d_, g_: _vjp(_rms, [c_, g_], (d_,)), [_t(r["h"], Q_LORA, 0), _t(d_cqn)], [gain_q],
                                        [(None, Q_LORA, _F32)], [((1, Q_LORA), _F32)], tm=tm_e, name=f"b{j}_qnormb")
        d_h = jnp.concatenate([d_cq, d_qm], axis=1)
        d_xn = _mm_nt(d_h, p["b_w_in"][j], out_dtype=_F32, name=f"b{j}_dxn", tn=1024)
        g["b_w_in"][j] = _mm_tn(r["xn"], d_h, name=f"b{j}_dwin")
        dx, g["mix_norm"][l] = _rms_bwd(r["x1"], d_xn, dx, row(p["mix_norm"][l]), name=f"b{j}_mixnormb")
        dx, g["ffn1_norm"][l], g["ffn_w_gu"], g["ffn_w_down"] = _ffn_bwd(
            dx, r["ffn1"], row(p["ffn1_norm"][l]), p["ffn_w_gu"], p["ffn_w_down"], l, g["ffn_w_gu"], g["ffn_w_down"], name=f"b{j}_ffn1b")

    def kv_sum(*parts):
        dkn = sum(parts[0::3][1:], parts[0])
        dv = sum(parts[1::3][1:], parts[1])
        dkr = sum(parts[2::3][1:], parts[2])
        return _cat([dkn, dv]), sum(_heads(dkr, HEADS)[1:], _heads(dkr, HEADS)[0])

    d_kvu, d_kr = _rowmap(kv_sum, [_t(a) for part in d_kv_parts for a in part], [], [(None, 2 * A_WIDTH, _F32), (None, LANE, _F32)],
                          tm=tm_e, name="kv_dsum")
    d_ckv = _mm_nt(d_kvu, p["w_ukv"], out_dtype=_F32, name="kv_dckv")
    g["w_ukv"] = _mm_tn(kvs["ckv"], d_kvu, name="kv_dwukv")
    d_ckr, g["kv_lat_norm"] = _rowmap(_kv_prep_bwd_fn, [_t(kvs["ckr"]), _t(d_ckv), _t(d_kr), _t(cos_t), _t(sin_t)], [row(p["kv_lat_norm"])],
                                     [(None, DKV_PAD, _F32)], [((1, KV_LORA), _F32)], tm=tm_e, name="kv_prepb")
    d_xn = _mm_nt(d_ckr, p["w_dkv"], out_dtype=_F32, name="kv_dxn", tn=1024)
    g["w_dkv"] = _mm_tn(kvs["xn"], d_ckr, name="kv_dwdkv")
    dx, g["kv_in_norm"] = _rms_bwd(kvs["x"], d_xn, dx, row(p["kv_in_norm"]), name="kv_innormb")

    for i in reversed(range(N_A)):
        l = i
        r = sv[l]
        dx, g["ffn2_norm"][l], g["ffn_w_gu"], g["ffn_w_down"] = _ffn_bwd(
            dx, r["ffn2"], row(p["ffn2_norm"][l]), p["ffn_w_gu"], p["ffn_w_down"], DEPTH + l, g["ffn_w_gu"], g["ffn_w_down"], name=f"a{i}_ffn2b")
        d_cat = _mm_nt(dx, p["w_out"][l], out_dtype=_F32, name=f"a{i}_dcat", tn=1024)
        g["w_out"][l] = _mm_tn(r["cat"], dx, name=f"a{i}_dwout", tn=1024)
        gain_o = row(p["a_out_norm"][i])
        h = r["h"]
        d_o, d_hpart, g["a_out_norm"][i], g["mem_kv"][l] = _rowmap(
            _a_post_bwd_fn, [_t(r["o"]), _t(h, A_WIDTH, 3), _t(h, MEM_WIDTH, 12), _t(d_cat)], [gain_o, mem_kv[l]],
            [(None, A_WIDTH, _F32), (None, D_MODEL, _F32)], [((1, HEAD_DIM), _F32), ((N_MEM, 2 * MEM_WIDTH), _F32)], tm=tm_e, name=f"a{i}_postb")
        d_qd, d_kd, d_w, d_u, d_qk, d_dcrow = _rowmap(
            _gdn_scan_bwd_fn, [_t(r["qd"]), _t(r["kd"]), _t(r["w"]), _t(r["u"]), _t(r["qk"]), _t(r["dcrow"]), _t(r["states"], rows=A_WIDTH), _t(d_o)], [],
            [(None, A_WIDTH, _F32)] * 5 + [(None, LANE, _F32)], tm=_PAIR, name=f"a{i}_scanb", carry=[((A_WIDTH, HEAD_DIM), _F32)], reverse=True)
        d_q, d_k, d_v, d_bg = _rowmap(
            _gdn_intra_bwd_fn, [_t(r["q"]), _t(r["k"]), _t(r["v"]), _t(r["bg"]), _t(d_w), _t(d_u), _t(d_qd), _t(d_kd), _t(d_qk), _t(d_dcrow)], [],
            [(None, A_WIDTH, _F32)] * 3 + [(None, LANE, _F32)], tm=_PAIR, name=f"a{i}_intrab")
        alog, dtb = p["a_A_log_row"][i], p["a_dt_bias_row"][i]
        d_qkv_c, d_ba, g["a_A_log_row"][i], g["a_dt_bias_row"][i] = _rowmap(
            _gdn_prep_bwd_fn, [_t(r["qkv_c"]), _t(h, LANE, 26), _t(d_q), _t(d_k), _t(d_v), _t(d_bg)], [alog, dtb],
            [(None, 3 * A_WIDTH, _F32), (None, LANE, _F32)], [((1, LANE), _F32), ((1, LANE), _F32)], tm=tm_e, name=f"a{i}_prepb")
        d_qkv, g["a_conv"][i] = _conv_bwd(h, d_qkv_c, p["a_conv"][i], name=f"a{i}_convb")
        d_h = jnp.concatenate([d_qkv, d_hpart, d_ba], axis=1)
        d_xn = _mm_nt(d_h, p["a_w_in"][i], out_dtype=_F32, name=f"a{i}_dxn", tn=1024)
        g["a_w_in"][i] = _mm_tn(r["xn"], d_h, name=f"a{i}_dwin", tn=1152)
        dx, g["mix_norm"][l] = _rms_bwd(r["x1"], d_xn, dx, row(p["mix_norm"][l]), name=f"a{i}_mixnormb")
        dx, g["ffn1_norm"][l], g["ffn_w_gu"], g["ffn_w_down"] = _ffn_bwd(
            dx, r["ffn1"], row(p["ffn1_norm"][l]), p["ffn_w_gu"], p["ffn_w_down"], l, g["ffn_w_gu"], g["ffn_w_down"], name=f"a{i}_ffn1b")

    d_mem_kv_all = jnp.concatenate(g.pop("mem_kv"), axis=1)
    d_mem_n = _mm_nt(d_mem_kv_all, p["w_mem_all"], out_dtype=_F32, name="mem_dn", tn=1024)
    g["w_mem_all"] = _mm_tn(mem_n, d_mem_kv_all, name="mem_dw", tn=1024)
    _, g["mem_norm"] = _rms_bwd(mem, d_mem_n, None, row(p["mem_norm"]), name="mem_normb")
    return loss, dx, g


_NOPE_ROPE = HEAD_DIM + QK_ROPE
_QKV_GATE = 4 * A_WIDTH
_BETA_AT = _QKV_GATE + MEM_WIDTH


def _lane_row(vals, at):
    n = vals.shape[0]
    return jnp.concatenate([jnp.zeros((at,), _F32), vals.astype(_F32), jnp.zeros((LANE - at - n,), _F32)]).reshape(1, LANE)


def _compute_form(w, conv_f32, ffn_w_gu, ffn_w_down):
    p = {n: w[n] for n in ("ffn1_norm", "mix_norm", "ffn2_norm", "w_out", "mem_norm", "a_out_norm", "b_w_in", "b_q_norm", "kv_in_norm",
                           "kv_lat_norm", "final_norm")}
    p["ffn_w_gu"], p["ffn_w_down"] = ffn_w_gu, ffn_w_down
    wm = w["w_mem_kv"]
    p["w_mem_all"] = jnp.transpose(wm, (1, 0, 2)).reshape(D_MODEL, DEPTH * 2 * MEM_WIDTH)
    a = w["a_w_in"]
    pad = jnp.zeros((N_A, D_MODEL, A_IN_PAD - A_IN), a.dtype)
    p["a_w_in"] = jnp.concatenate([a[:, :, :_QKV_GATE], a[:, :, _QKV_GATE + 2 * HEADS:], a[:, :, _QKV_GATE:_QKV_GATE + 2 * HEADS], pad], axis=2)
    p["a_conv"] = jnp.concatenate([conv_f32, jnp.zeros((N_A, SUBLANE - CONV_K, 3 * A_WIDTH), _F32)], axis=1)
    p["a_A_log_row"] = [_lane_row(w["a_A_log"][i], HEADS) for i in range(N_A)]
    p["a_dt_bias_row"] = [_lane_row(w["a_dt_bias"][i], HEADS) for i in range(N_A)]
    uq = w["b_w_uq"].reshape(N_B, Q_LORA, HEADS, _NOPE_ROPE)
    rope = jnp.concatenate([uq[..., HEAD_DIM:], jnp.zeros((N_B, Q_LORA, HEADS, LANE - QK_ROPE), uq.dtype)], axis=-1)
    p["b_w_uq"] = jnp.concatenate([uq[..., :HEAD_DIM].reshape(N_B, Q_LORA, A_WIDTH), rope.reshape(N_B, Q_LORA, A_WIDTH)], axis=-1)
    dkv = w["w_dkv"]
    p["w_dkv"] = jnp.concatenate([dkv, jnp.zeros((D_MODEL, DKV_PAD - dkv.shape[1]), dkv.dtype)], axis=1)
    ukv = w["w_ukv"].reshape(KV_LORA, HEADS, 2 * HEAD_DIM)
    p["w_ukv"] = jnp.concatenate([ukv[..., :HEAD_DIM].reshape(KV_LORA, A_WIDTH), ukv[..., HEAD_DIM:].reshape(KV_LORA, A_WIDTH)], axis=-1)
    return p


def _natural_grads(g):
    st = lambda xs: jnp.stack(xs, axis=0)
    n = {}
    for k in ("ffn1_norm", "mix_norm", "ffn2_norm"):
        n[k] = st(g[k]).reshape(DEPTH, D_MODEL)
    for k in ("w_out", "b_w_in"):
        n[k] = st(g[k])
    n["mem_norm"] = g["mem_norm"].reshape(D_MODEL)
    n["w_mem_kv"] = jnp.transpose(g["w_mem_all"].reshape(D_MODEL, DEPTH, 2 * MEM_WIDTH), (1, 0, 2))
    a = st(g["a_w_in"])
    n["a_w_in"] = jnp.concatenate([a[:, :, :_QKV_GATE], a[:, :, _BETA_AT:_BETA_AT + 2 * HEADS], a[:, :, _QKV_GATE:_BETA_AT]], axis=2)
    n["a_conv"] = st(g["a_conv"])[:, :CONV_K]
    n["a_A_log"] = st(g["a_A_log_row"])[:, 0, HEADS:2 * HEADS]
    n["a_dt_bias"] = st(g["a_dt_bias_row"])[:, 0, HEADS:2 * HEADS]
    n["a_out_norm"] = st(g["a_out_norm"]).reshape(N_A, HEAD_DIM)
    n["b_q_norm"] = st(g["b_q_norm"]).reshape(N_B, Q_LORA)
    uq = st(g["b_w_uq"])
    nope = uq[:, :, :A_WIDTH].reshape(N_B, Q_LORA, HEADS, HEAD_DIM)
    rope = uq[:, :, A_WIDTH:].reshape(N_B, Q_LORA, HEADS, LANE)[..., :QK_ROPE]
    n["b_w_uq"] = jnp.concatenate([nope, rope], axis=-1).reshape(N_B, Q_LORA, HEADS * _NOPE_ROPE)
    n["kv_in_norm"] = g["kv_in_norm"].reshape(D_MODEL)
    n["w_dkv"] = g["w_dkv"][:, :KV_LORA + QK_ROPE]
    n["kv_lat_norm"] = g["kv_lat_norm"].reshape(KV_LORA)
    ukv = g["w_ukv"]
    n["w_ukv"] = jnp.concatenate([ukv[:, :A_WIDTH].reshape(KV_LORA, HEADS, HEAD_DIM), ukv[:, A_WIDTH:].reshape(KV_LORA, HEADS, HEAD_DIM)],
                                 axis=-1).reshape(KV_LORA, HEADS * 2 * HEAD_DIM)
    n["final_norm"] = g["final_norm"].reshape(D_MODEL)
    return n


def _rope_tables(positions):
    inv = ROPE_THETA ** (-jnp.arange(0, QK_ROPE, 2, dtype=_F32) / QK_ROPE)
    ang = positions.astype(_F32)[:, None] * inv
    z = jnp.zeros((positions.shape[0], LANE - QK_ROPE), _F32)
    cos, sin = jnp.cos(ang), jnp.sin(ang)
    return jnp.concatenate([cos, cos, z], axis=1), jnp.concatenate([sin, sin, z], axis=1)


_HBM = pl.BlockSpec(memory_space=pltpu.HBM)


def _place():
    x, y, c = lax.axis_index("x"), lax.axis_index("y"), lax.axis_index("c")
    return x, y, c, [(1 - x, y), (x, 1 - y), (1 - x, 1 - y)]


def _remote(src, dst, send_sem, recv_sem, to):
    return pltpu.make_async_remote_copy(src_ref=src, dst_ref=dst, send_sem=send_sem, recv_sem=recv_sem, device_id=to, device_id_type=_MESH)


def _gather_over_chips(shard, *, name):
    rows, cols = shard.shape
    half = rows // 2

    def body(w_ref, out_ref, send_sems, recv_sems):
        x, y, c, chips = _place()
        k = 2 * x + y

        def part(chip, h):
            return out_ref.at[chip, pl.ds(h * half, half), :]

        first = [_remote(w_ref.at[pl.ds(c * half, half), :], part(k, c), send_sems.at[j], recv_sems.at[j], (px, py, c))
                 for j, (px, py) in enumerate(chips)]
        for cp in first:
            cp.start()
        passed = []
        for j, (px, py) in enumerate(chips):
            got = part(2 * px + py, c)
            _remote(got, got, send_sems.at[j], recv_sems.at[j], (px, py, c)).wait_recv()
            fw = _remote(got, got, send_sems.at[3 + j], recv_sems.at[3 + j], (x, y, 1 - c))
            fw.start()
            passed.append(fw)
        for j, (px, py) in enumerate(chips):
            got = part(2 * px + py, 1 - c)
            _remote(got, got, send_sems.at[3 + j], recv_sems.at[3 + j], (x, y, 1 - c)).wait_recv()
        for cp in first + passed:
            cp.wait_send()

    others = _pcall(
        body, name=name, in_specs=[_HBM], out_specs=_HBM, out_shape=jax.ShapeDtypeStruct((N_CHIPS, rows, cols), shard.dtype),
        scratch_shapes=[pltpu.SemaphoreType.DMA((6,)), pltpu.SemaphoreType.DMA((6,))],
    )(shard)
    return lax.dynamic_update_slice(others, shard[None], (2 * lax.axis_index("x") + lax.axis_index("y"), 0, 0))


PAIR_COPIES = 4


def _scatter_over_chips(v, *, name):
    def body(v_ref, out_ref, send_sems, recv_sems):
        x, y, c, chips = _place()
        cps = [_remote(v_ref.at[2 * px + py], out_ref.at[j], send_sems.at[j], recv_sems.at[j], (px, py, c)) for j, (px, py) in enumerate(chips)]
        for cp in cps:
            cp.start()
        for cp in cps:
            cp.wait()

    return _pcall(body, name=name, in_specs=[_HBM], out_specs=_HBM, out_shape=jax.ShapeDtypeStruct((N_CHIPS - 1,) + v.shape[1:], v.dtype),
                  scratch_shapes=[pltpu.SemaphoreType.DMA((3,)), pltpu.SemaphoreType.DMA((3,))])(v)


def _all_reduce_small(v, *, name):
    def body(v_ref, out_ref, all_ref, send_sems, recv_sems):
        x, y, c, _ = _place()
        me = 4 * x + 2 * y + c
        all_ref[me] = v_ref[...]
        cps = []
        for f in range(1, N_DEV):
            fx, fy, fc = (f >> 2) & 1, (f >> 1) & 1, f & 1
            to = (x + fx - 2 * x * fx, y + fy - 2 * y * fy, c + fc - 2 * c * fc)
            cps.append(_remote(v_ref, all_ref.at[me], send_sems.at[f - 1], recv_sems.at[f - 1], to))
        for cp in cps:
            cp.start()
        for cp in cps:
            cp.wait()
        acc = all_ref[0]
        for d in range(1, N_DEV):
            acc = acc + all_ref[d]
        out_ref[...] = acc

    vm = pl.BlockSpec(memory_space=pltpu.VMEM)
    return _pcall(body, name=name, in_specs=[vm], out_specs=vm, out_shape=jax.ShapeDtypeStruct(v.shape, v.dtype),
                  scratch_shapes=[pltpu.VMEM((N_DEV,) + v.shape, v.dtype), pltpu.SemaphoreType.DMA((N_DEV - 1,)), pltpu.SemaphoreType.DMA((N_DEV - 1,))])(v)


_FFN_GU = ("ffn1_w_gu", "ffn2_w_gu")
_FFN_DOWN = ("ffn1_w_down", "ffn2_w_down")
_BIG = (("w_out", 1), ("w_mem_kv", 1), ("a_w_in", 2), ("a_conv", 2), ("b_w_in", 1), ("b_w_uq", 2), ("w_dkv", 0),
        ("w_ukv", 1))
_SMALL = ("ffn1_norm", "mix_norm", "ffn2_norm", "mem_norm", "a_A_log", "a_dt_bias", "a_out_norm", "b_q_norm", "kv_in_norm", "kv_lat_norm",
          "final_norm")
_WEIGHTS = ("ffn1_norm", "ffn1_w_gu", "ffn1_w_down", "mix_norm", "ffn2_norm", "ffn2_w_gu", "ffn2_w_down", "w_out", "mem_norm", "w_mem_kv",
            "a_w_in", "a_conv", "a_A_log", "a_dt_bias", "a_out_norm", "b_w_in", "b_q_norm", "b_w_uq", "kv_in_norm", "w_dkv", "kv_lat_norm",
            "w_ukv", "final_norm")


PACK_PIECE_ROWS = 16


def _piece_rows(shape):
    return -(-math.prod(shape) // (PACK_COLS * PACK_PIECE_ROWS)) * PACK_PIECE_ROWS


def _packed_rows(shapes):
    return sum(_piece_rows(s) for s in shapes)


def _pack(arrs, rows):
    pieces = []
    for a in arrs:
        n, r = a.size, _piece_rows(a.shape)
        flat = a.reshape(-1)
        if r * PACK_COLS != n:
            flat = jnp.concatenate([flat, jnp.zeros((r * PACK_COLS - n,), a.dtype)])
        pieces.append(flat.reshape(r, PACK_COLS))
    used = sum(p.shape[0] for p in pieces)
    if rows > used:
        pieces.append(jnp.zeros((rows - used, PACK_COLS), arrs[0].dtype))
    return jnp.concatenate(pieces, axis=0)


def _unpack(packed, shapes):
    off, out = 0, []
    for shp in shapes:
        n, r = math.prod(shp), _piece_rows(shp)
        piece = packed[off:off + r]
        out.append((piece if r * PACK_COLS == n else piece.reshape(-1)[:n]).reshape(shp))
        off += r
    return out


def _adamw_math(w_, g_, m_, v_):
    m2 = ADAM_B1 * m_ + (1.0 - ADAM_B1) * g_
    v2 = ADAM_B2 * v_ + (1.0 - ADAM_B2) * (g_ * g_)
    m_hat = m2 / (1.0 - ADAM_B1 ** ADAM_STEP)
    v_hat = v2 / (1.0 - ADAM_B2 ** ADAM_STEP)
    return -ADAM_LR * (m_hat / (jnp.sqrt(v_hat) + ADAM_EPS) + ADAM_WD * w_), m2, v2


def _adamw(w, g, m, v, *, name):
    return _rowmap(_adamw_math, [_t(w), _t(g), _t(m), _t(v)], [], [(None, w.shape[1], _F32)] * 3, tm=min(PACK_ROW_TILE, w.shape[0]), name=name)


def _pair_exchange_half(g, *, name):
    n, _, h, cols = g.shape

    def body(g_ref, out_ref, send_sems, recv_sems):
        x, y, c, _ = _place()
        cps = [_remote(g_ref.at[k, 1 - c], out_ref.at[k], send_sems.at[k], recv_sems.at[k], (x, y, 1 - c)) for k in range(n)]
        for cp in cps:
            cp.start()
        for cp in cps:
            cp.wait()

    return _pcall(body, name=name, in_specs=[_HBM], out_specs=_HBM, out_shape=jax.ShapeDtypeStruct((n, h, cols), g.dtype),
                  scratch_shapes=[pltpu.SemaphoreType.DMA((n,)), pltpu.SemaphoreType.DMA((n,))])(g)


def _add_half(g, other, c, *, name):
    n, _, h, cols = g.shape
    tm = min(PACK_ROW_TILE, h)

    def body(c_ref, g_ref, o_ref, sum_ref, narrow_ref):
        acc = g_ref[...] + o_ref[...]
        sum_ref[...] = acc
        narrow_ref[...] = acc.astype(narrow_ref.dtype)

    slab = pl.BlockSpec((None, tm, cols), lambda k, i, c_ref: (k, i, 0))
    return _pcall(
        body, name=name,
        grid_spec=pltpu.PrefetchScalarGridSpec(
            num_scalar_prefetch=1, grid=(n, h // tm),
            in_specs=[pl.BlockSpec((None, None, tm, cols), lambda k, i, c_ref: (k, c_ref[0], i, 0)), slab], out_specs=[slab, slab]),
        out_shape=[jax.ShapeDtypeStruct((n, h, cols), _F32), jax.ShapeDtypeStruct((n, h, cols), jnp.bfloat16)],
        compiler_params=_cp(2),
    )(jnp.reshape(c, (1,)).astype(jnp.int32), g, other)


def _add_own(chip_sum, from_chips, chip, *, name):
    _, h, cols = chip_sum.shape
    tm = min(PACK_ROW_TILE, h)

    def body(k_ref, own_ref, a_ref, b_ref, c_ref, o_ref):
        o_ref[...] = ((own_ref[...] + a_ref[...].astype(_F32)) + b_ref[...].astype(_F32)) + c_ref[...].astype(_F32)

    sent = [pl.BlockSpec((None, tm, cols), functools.partial(lambda i, k_ref, j: (j, i, 0), j=j)) for j in range(N_CHIPS - 1)]
    return _pcall(
        body, name=name,
        grid_spec=pltpu.PrefetchScalarGridSpec(
            num_scalar_prefetch=1, grid=(h // tm,),
            in_specs=[pl.BlockSpec((None, tm, cols), lambda i, k_ref: (k_ref[0], i, 0))] + sent,
            out_specs=pl.BlockSpec((tm, cols), lambda i, k_ref: (i, 0))),
        out_shape=jax.ShapeDtypeStruct((h, cols), _F32), compiler_params=_cp(1),
    )(jnp.reshape(chip, (1,)).astype(jnp.int32), chip_sum, from_chips, from_chips, from_chips)


def _pair_gather(mine, *, name):
    h, cols = mine.shape
    per = h // PAIR_COPIES
    assert per * PAIR_COPIES == h and per % SUBLANE == 0, mine.shape

    def body(v_ref, out_ref, send_sems, recv_sems):
        x, y, c, _ = _place()
        cps = [_remote(v_ref.at[pl.ds(q * per, per), :], out_ref.at[c, pl.ds(q * per, per), :], send_sems.at[q], recv_sems.at[q], (x, y, 1 - c))
               for q in range(PAIR_COPIES)]
        for cp in cps:
            cp.start()
        for q, cp in enumerate(cps):
            cp.wait_send()
            _remote(v_ref.at[pl.ds(q * per, per), :], out_ref.at[1 - c, pl.ds(q * per, per), :], send_sems.at[q], recv_sems.at[q],
                    (x, y, 1 - c)).wait_recv()

    both = _pcall(body, name=name, in_specs=[_HBM], out_specs=_HBM, out_shape=jax.ShapeDtypeStruct((2, h, cols), mine.dtype),
                  scratch_shapes=[pltpu.SemaphoreType.DMA((PAIR_COPIES,)), pltpu.SemaphoreType.DMA((PAIR_COPIES,))])(mine)
    return lax.dynamic_update_slice(both, mine[None], (lax.axis_index("c"), 0, 0))


def _reduce_over_devices(g, c, chip, *, name):
    n, rows, cols = g.shape
    g = g.reshape(n, 2, rows // 2, cols)
    chip_sum, narrow = _add_half(g, _pair_exchange_half(g, name=name + "_pair_sum"), c, name=name + "_add_pair")
    mine = _add_own(chip_sum, _scatter_over_chips(narrow, name=name + "_scatter"), chip, name=name + "_add_chips")
    return _pair_gather(mine, name=name + "_pair_gather").reshape(rows, cols)


def _adamw_at(w, m, v, g_all, first_row, *, name):
    tm = min(PACK_ROW_TILE, w.shape[0])
    assert w.shape[0] % tm == 0 and first_row % tm == 0, (name, w.shape, first_row)

    def fn(w_, m_, v_, g_):
        return _adamw_math(w_, g_, m_, v_) + (g_,)

    return _rowmap(fn, [_t(w), _t(m), _t(v), _t(g_all, first=first_row // tm)], [], [(None, w.shape[1], _F32)] * 4, tm=tm, name=name)


def _step(x, mem, positions, loss_target, w, m, v):
    cx, cy, cc = lax.axis_index("x"), lax.axis_index("y"), lax.axis_index("c")
    chip = 2 * cx + cy
    big = [n for n, _ in _BIG]
    shard_shapes = [w[n].shape for n in big]
    rows = -(-_packed_rows(shard_shapes) // (2 * PACK_ROW_TILE)) * 2 * PACK_ROW_TILE
    flat = lambda a: a.reshape(-1, a.shape[-1])

    ffn_w_gu = _gather_over_chips(jnp.concatenate([flat(w[n]) for n in _FFN_GU]).astype(_BF), name="gather_w_gu")
    ffn_w_down = _gather_over_chips(jnp.concatenate([flat(w[n]) for n in _FFN_DOWN]).astype(_BF), name="gather_w_down")
    w_pack = _pack([w[n] for n in big], rows)
    gathered = _gather_over_chips(w_pack.astype(_BF), name="gather_weights")
    pieces = [_unpack(gathered[k], shard_shapes) for k in range(N_CHIPS)]
    full = {n: jnp.concatenate([pieces[k][i] for k in range(N_CHIPS)], axis=ax) for i, (n, ax) in enumerate(_BIG)}
    for n in _SMALL:
        full[n] = w[n]
    conv = w["a_conv"]
    slots = jnp.stack([jnp.where((chip == k) & (cc == 0), conv, 0.0) for k in range(N_CHIPS)])
    conv_all = _unpack(_all_reduce_small(_pack([slots], _piece_rows(slots.shape)), name="gather_conv"), [slots.shape])[0]
    conv_full = jnp.concatenate([conv_all[k] for k in range(N_CHIPS)], axis=2)

    p = _compute_form(full, conv_full, ffn_w_gu, ffn_w_down)
    cos_t, sin_t = _rope_tables(positions[0])
    loss_tile, d_x, g = _local_step(x[0], mem[0], cos_t, sin_t, loss_target[0], p)
    gn = _natural_grads(g)

    def shard_of(a, ax, k):
        size = a.shape[ax] // N_CHIPS
        return lax.slice_in_dim(a, k * size, (k + 1) * size, axis=ax)

    grads, deltas, new_m, new_v = {}, {}, {}, {}
    for names, key in ((_FFN_GU, "ffn_w_gu"), (_FFN_DOWN, "ffn_w_down")):
        buf = g[key]
        reduced = _reduce_over_devices(buf.reshape(N_CHIPS, -1, buf.shape[-1]), cc, chip, name="grad_" + key)
        first = 0
        for n in names:
            d_, m_, v_, g_ = _adamw_at(flat(w[n]), flat(m[n]), flat(v[n]), reduced, first, name="adamw_" + n)
            grads[n], deltas[n], new_m[n], new_v[n] = (t.reshape(w[n].shape) for t in (g_, d_, m_, v_))
            first += flat(w[n]).shape[0]
    g_pack = jnp.stack([_pack([shard_of(gn[n], ax, k) for n, ax in _BIG], rows) for k in range(N_CHIPS)])
    g_big = _reduce_over_devices(g_pack, cc, chip, name="grad_misc")
    d_big, m_big, v_big = _adamw(w_pack, g_big, _pack([m[n] for n in big], rows), _pack([v[n] for n in big], rows), name="adamw_misc")

    small_shapes = [w[n].shape for n in _SMALL]
    small_rows = _packed_rows(small_shapes)
    g_small = _all_reduce_small(_pack([gn[n] for n in _SMALL], small_rows), name="grad_small")
    d_small, m_small, v_small = _adamw(_pack([w[n] for n in _SMALL], small_rows), g_small, _pack([m[n] for n in _SMALL], small_rows),
                                       _pack([v[n] for n in _SMALL], small_rows), name="adamw_small")

    for out, big_pack, small_pack in ((grads, g_big, g_small), (deltas, d_big, d_small), (new_m, m_big, m_small), (new_v, v_big, v_small)):
        out.update(zip(big, _unpack(big_pack, shard_shapes)))
        out.update(zip(_SMALL, _unpack(small_pack, small_shapes)))
    loss = lax.psum(loss_tile[0, 0], ("x", "y", "c"))
    return (loss, d_x[None], *[grads[n] for n in _WEIGHTS], *[deltas[n] for n in _WEIGHTS], *[new_m[n] for n in _WEIGHTS],
            *[new_v[n] for n in _WEIGHTS])


def kernel(x, mem, positions, ffn1_norm, ffn1_w_gu, ffn1_w_down, mix_norm, ffn2_norm, ffn2_w_gu, ffn2_w_down, w_out, mem_norm, w_mem_kv, a_w_in, a_conv, a_A_log, a_dt_bias, a_out_norm, b_w_in, b_q_norm, b_w_uq, kv_in_norm, w_dkv, kv_lat_norm, w_ukv, final_norm, loss_target, m_ffn1_norm, m_ffn1_w_gu, m_ffn1_w_down, m_mix_norm, m_ffn2_norm, m_ffn2_w_gu, m_ffn2_w_down, m_w_out, m_mem_norm, m_w_mem_kv, m_a_w_in, m_a_conv, m_a_A_log, m_a_dt_bias, m_a_out_norm, m_b_w_in, m_b_q_norm, m_b_w_uq, m_kv_in_norm, m_w_dkv, m_kv_lat_norm, m_w_ukv, m_final_norm, v_ffn1_norm, v_ffn1_w_gu, v_ffn1_w_down, v_mix_norm, v_ffn2_norm, v_ffn2_w_gu, v_ffn2_w_down, v_w_out, v_mem_norm, v_w_mem_kv, v_a_w_in, v_a_conv, v_a_A_log, v_a_dt_bias, v_a_out_norm, v_b_w_in, v_b_q_norm, v_b_w_uq, v_kv_in_norm, v_w_dkv, v_kv_lat_norm, v_w_ukv, v_final_norm):
    given = dict(locals())
    w = {n: given[n] for n in _WEIGHTS}
    m = {n: given["m_" + n] for n in _WEIGHTS}
    v = {n: given["v_" + n] for n in _WEIGHTS}
    return _step(x, mem, positions, loss_target, w, m, v)
```

```python
import functools
import math

import jax
import jax.numpy as jnp
from jax import lax
from jax.experimental import pallas as pl
from jax.experimental.pallas import tpu as pltpu

_BF = jnp.bfloat16
_F32 = jnp.float32
_HI = lax.Precision.HIGHEST
_MESH = pl.DeviceIdType.MESH

D_MODEL = 1024
DEPTH = 4
N_A = 2
N_B = 2
CHUNK = 64
EPS = 1e-6
HEADS = 6
HEAD_DIM = 128
A_WIDTH = HEADS * HEAD_DIM
CONV_K = 4
QK_ROPE = 64
Q_LORA = 256
KV_LORA = 256
N_MEM = 256
MEM_HEADS = 4
MEM_HEAD_DIM = 64
MEM_WIDTH = MEM_HEADS * MEM_HEAD_DIM
D_FF = 2816
ROPE_THETA = 10000.0
A_IN = 4 * A_WIDTH + 2 * HEADS + MEM_WIDTH
A_IN_PAD = 3456
UQ_PAD = 2 * A_WIDTH
DKV_PAD = KV_LORA + 128
LANE = 128
SUBLANE = 8

ADAM_LR = 0.001
ADAM_B1 = 0.9
ADAM_B2 = 0.999
ADAM_EPS = 1e-08
ADAM_WD = 0.01
ADAM_STEP = 10

N_CHIPS = 4
N_DEV = 8
PACK_COLS = 1024
PACK_ROW_TILE = 256


def _pcall(body, **kw):
    return pl.pallas_call(body, **kw)


VMEM_LIMIT_V7X = 48 * 2 ** 20
TILE_BYTES = 6 * 2 ** 20


def _cp(grid_rank):
    return pltpu.CompilerParams(dimension_semantics=("arbitrary",) * grid_rank, vmem_limit_bytes=VMEM_LIMIT_V7X)


def _fit_rows(rows, row_bytes):
    while rows > LANE and rows * row_bytes > TILE_BYTES:
        rows //= 2
    return rows


def _fit_cols(n, target, col_bytes):
    return _tile(n, max(LANE, min(target, TILE_BYTES // col_bytes)))


def _tile(n, target):
    best = None
    for t in range(LANE, min(n, target) + 1, LANE):
        if n % t == 0:
            best = t
    return best if best is not None else n


def _dot(a, b):
    return jnp.dot(a.astype(_BF), b.astype(_BF), preferred_element_type=_F32)


def _dot_nt(a, b):
    return lax.dot_general(a.astype(_BF), b.astype(_BF), (((1,), (1,)), ((), ())), preferred_element_type=_F32)


def _dot_tn(a, b):
    return lax.dot_general(a.astype(_BF), b.astype(_BF), (((0,), (0,)), ((), ())), preferred_element_type=_F32)


def _dot_hi(a, b):
    return jnp.dot(a, b, precision=_HI, preferred_element_type=_F32)


def _rowmap(fn, tiles, params, outs, accs=(), *, tm, name, carry=(), reverse=False):
    rows = tiles[0][0].shape[0]
    steps = rows // tm
    nt, npar, no, na, nc = len(tiles), len(params), len(outs), len(accs), len(carry)

    def step_index(i):
        return steps - 1 - i if reverse else i

    in_specs, operands = [], []
    for arr, r, w, cb, first in tiles:
        r = tm if r is None else r
        w = arr.shape[1] if w is None else w
        assert arr.shape[0] >= (first + steps) * r and (w % LANE == 0 or w == arr.shape[1]), (name, arr.shape, r, w)
        in_specs.append(pl.BlockSpec((r, w), functools.partial(lambda i, cb, first: (first + step_index(i), cb), cb=cb, first=first)))
        operands.append(arr)
    for p in params:
        in_specs.append(pl.BlockSpec(p.shape, functools.partial(lambda i, nd: (0,) * nd, nd=p.ndim)))
        operands.append(p)
    out_specs, out_shape = [], []
    for r, cols, dt in outs:
        r = tm if r is None else r
        out_specs.append(pl.BlockSpec((r, cols), lambda i: (step_index(i), 0)))
        out_shape.append(jax.ShapeDtypeStruct((steps * r, cols), dt))
    for shp, dt in accs:
        out_specs.append(pl.BlockSpec(shp, functools.partial(lambda i, nd: (0,) * nd, nd=len(shp))))
        out_shape.append(jax.ShapeDtypeStruct(shp, dt))

    def body(*refs):
        t_refs = refs[:nt]
        p_refs = refs[nt:nt + npar]
        o_refs = refs[nt + npar:nt + npar + no]
        a_refs = refs[nt + npar + no:nt + npar + no + na]
        c_refs = refs[nt + npar + no + na:]
        if na or nc:
            @pl.when(pl.program_id(0) == 0)
            def _():
                for r in a_refs + c_refs:
                    r[...] = jnp.zeros(r.shape, r.dtype)
        vals = fn(*[r[...] for r in t_refs], *[r[...] for r in p_refs], *[r[...] for r in c_refs])
        vals = tuple(vals) if isinstance(vals, (tuple, list)) else (vals,)
        assert len(vals) == no + na + nc, (name, len(vals), no, na, nc)
        for r, v in zip(o_refs, vals[:no]):
            r[...] = v.astype(r.dtype)
        for r, v in zip(a_refs, vals[no:no + na]):
            r[...] += v.astype(r.dtype)
        for r, v in zip(c_refs, vals[no + na:]):
            r[...] = v.astype(r.dtype)

    res = _pcall(
        body, name=name, grid=(steps,), in_specs=in_specs, out_specs=out_specs, out_shape=out_shape,
        scratch_shapes=[pltpu.VMEM(shp, dt) for shp, dt in carry],
        compiler_params=_cp(1),
    )(*operands)
    return res


def _t(arr, width=None, cb=0, rows=None, first=0):
    return (arr, rows, width, cb, first)


def _mm_nn(a, b, *, out_dtype, name, scale=None, res=None, tm=1024, tn=1536):
    m, k = a.shape
    n = b.shape[1]
    tm, tn = _fit_rows(min(tm, m), k * a.dtype.itemsize), _fit_cols(n, tn, k * b.dtype.itemsize)

    def body(a_ref, b_ref, *rest):
        acc = _dot(a_ref[...], b_ref[...])
        if scale is not None:
            acc = acc * scale
        if res is not None:
            acc = acc + rest[0][...]
        rest[-1][...] = acc.astype(rest[-1].dtype)

    in_specs = [pl.BlockSpec((tm, k), lambda i, j: (i, 0)), pl.BlockSpec((k, tn), lambda i, j: (0, j))]
    operands = [a, b]
    if res is not None:
        in_specs.append(pl.BlockSpec((tm, tn), lambda i, j: (i, j)))
        operands.append(res)
    return _pcall(
        body, name=name, grid=(m // tm, n // tn), in_specs=in_specs,
        out_specs=pl.BlockSpec((tm, tn), lambda i, j: (i, j)), out_shape=jax.ShapeDtypeStruct((m, n), out_dtype),
        compiler_params=_cp(2),
    )(*operands)


def _mm_nt(a, b, *, out_dtype, name, scale=None, tm=1024, tn=1536):
    m, k = a.shape
    n = b.shape[0]
    tm, tn = _fit_rows(min(tm, m), k * a.dtype.itemsize), _fit_cols(n, tn, k * b.dtype.itemsize)

    def body(a_ref, b_ref, o_ref):
        acc = _dot_nt(a_ref[...], b_ref[...])
        if scale is not None:
            acc = acc * scale
        o_ref[...] = acc.astype(o_ref.dtype)

    return _pcall(
        body, name=name, grid=(m // tm, n // tn),
        in_specs=[pl.BlockSpec((tm, k), lambda i, j: (i, 0)), pl.BlockSpec((tn, k), lambda i, j: (j, 0))],
        out_specs=pl.BlockSpec((tm, tn), lambda i, j: (i, j)), out_shape=jax.ShapeDtypeStruct((m, n), out_dtype),
        compiler_params=_cp(2),
    )(a, b)


def _mm_tn(a, b, *, name, scale=None, t1=1024, tn=1536, ts=1024):
    s, k1 = a.shape
    n = b.shape[1]
    t1, tn, ts = _tile(k1, t1), _tile(n, tn), min(ts, s)
    steps = s // ts

    def body(a_ref, b_ref, o_ref):
        @pl.when(pl.program_id(2) == 0)
        def _():
            o_ref[...] = jnp.zeros(o_ref.shape, o_ref.dtype)

        o_ref[...] += _dot_tn(a_ref[...], b_ref[...])
        if scale is not None:
            @pl.when(pl.program_id(2) == steps - 1)
            def _():
                o_ref[...] = o_ref[...] * scale

    return _pcall(
        body, name=name, grid=(k1 // t1, n // tn, steps),
        in_specs=[pl.BlockSpec((ts, t1), lambda i, j, r: (r, i)), pl.BlockSpec((ts, tn), lambda i, j, r: (r, j))],
        out_specs=pl.BlockSpec((t1, tn), lambda i, j, r: (i, j)), out_shape=jax.ShapeDtypeStruct((k1, n), _F32),
        compiler_params=_cp(3),
    )(a, b)


def _heads(t, n, w=LANE):
    return [t[:, w * h:w * (h + 1)] for h in range(n)]


def _cat(parts):
    return jnp.concatenate(parts, axis=1)


def _rms(x, g):
    return x * lax.rsqrt(jnp.mean(x * x, axis=-1, keepdims=True) + EPS) * g


def _l2n(x):
    return x * lax.rsqrt(jnp.sum(x * x, axis=-1, keepdims=True) + EPS)


def _sigmoid(x):
    return 0.5 * (jnp.tanh(0.5 * x) + 1.0)


def _silu(x):
    return x * _sigmoid(x)


def _softplus(x):
    return jnp.maximum(x, 0.0) + jnp.log(1.0 + jnp.exp(-jnp.abs(x)))


def _lane_pick(t, h):
    lane = lax.broadcasted_iota(jnp.int32, t.shape, 1)
    return jnp.sum(jnp.where(lane == h, t, 0.0), axis=1, keepdims=True)


def _lane_put(col, h, width=LANE):
    lane = lax.broadcasted_iota(jnp.int32, (col.shape[0], width), 1)
    return jnp.where(lane == h, col, 0.0)


def _vjp(fwd, ins, cts):
    outs, pull = jax.vjp(fwd, *ins)
    outs = outs if isinstance(outs, (tuple, list)) else (outs,)
    cts = tuple(c.astype(o.dtype) for c, o in zip(cts, outs))
    return pull(cts if len(cts) > 1 else cts[0])


def _rot_half_matrix():
    r = lax.broadcasted_iota(jnp.int32, (LANE, LANE), 0)
    c = lax.broadcasted_iota(jnp.int32, (LANE, LANE), 1)
    half = QK_ROPE // 2
    return jnp.where((c < half) & (r == c + half), -1.0, jnp.where((c >= half) & (c < QK_ROPE) & (r == c - half), 1.0, 0.0))


def _rope(x, cos_t, sin_t):
    return x * cos_t + _dot_hi(x, _rot_half_matrix()) * sin_t


def _mem_attn(q, km, vm):
    lane_q = lax.broadcasted_iota(jnp.int32, q.shape, 1)
    lane_v = lax.broadcasted_iota(jnp.int32, vm.shape, 1)
    out = jnp.zeros(q.shape, _F32)
    for h in range(MEM_HEADS):
        lo, hi = MEM_HEAD_DIM * h, MEM_HEAD_DIM * (h + 1)
        qh = jnp.where((lane_q >= lo) & (lane_q < hi), q, 0.0)
        vh = jnp.where((lane_v >= lo) & (lane_v < hi), vm, 0.0)
        sc = _dot_nt(qh, km) * MEM_HEAD_DIM ** -0.5
        sc = sc - lax.stop_gradient(jnp.max(sc, axis=-1, keepdims=True))
        p = jnp.exp(sc)
        p = p / jnp.sum(p, axis=-1, keepdims=True)
        out = out + _dot(p, vh)
    return out


_PAIR = 2 * CHUNK


def _pair_masks():
    ri = lax.broadcasted_iota(jnp.int32, (_PAIR, _PAIR), 0)
    ci = lax.broadcasted_iota(jnp.int32, (_PAIR, _PAIR), 1)
    same = (ri >= CHUNK) == (ci >= CHUNK)
    return same, same & (ri >= ci), same & (ri > ci), ri == ci, same & (ri <= ci)


_NN = (((2,), (1,)), ((0,), (0,)))
_NT = (((2,), (2,)), ((0,), (0,)))
_TN = (((1,), (1,)), ((0,), (0,)))


def _bdot(a, b, dims):
    return lax.dot_general(a.astype(_BF), b.astype(_BF), dims, preferred_element_type=_F32)


def _dot3(a, b, dims):
    a_hi, b_hi = a.astype(_BF), b.astype(_BF)
    a_lo, b_lo = (a - a_hi.astype(_F32)).astype(_BF), (b - b_hi.astype(_F32)).astype(_BF)
    d = lambda x, y: lax.dot_general(x, y, dims, preferred_element_type=_F32)
    return d(a_hi, b_hi) + (d(a_hi, b_lo) + d(a_lo, b_hi))


@jax.custom_vjp
def _mm3(a, b):
    return _dot3(a, b, _NN)


_mm3.defvjp(lambda a, b: (_dot3(a, b, _NN), (a, b)), lambda res, g: (_dot3(g, res[1], _NT), _dot3(res[0], g, _TN)))


def _neumann_inverse(a):
    eye = jnp.where(_pair_masks()[3], 1.0, 0.0)
    n = -a
    t_inv = eye + n
    for _ in range(5):
        n = _dot3(n, n, _NN)
        t_inv = t_inv + _dot3(t_inv, n, _NN)
    return t_inv


@jax.custom_vjp
def _unit_lower_inverse(a):
    return _neumann_inverse(a)


def _unit_lower_inverse_fwd(a):
    t_inv = _neumann_inverse(a)
    return t_inv, t_inv


def _unit_lower_inverse_bwd(t_inv, g):
    return (-_dot3(t_inv, _dot3(g, t_inv, _NT), _TN),)


_unit_lower_inverse.defvjp(_unit_lower_inverse_fwd, _unit_lower_inverse_bwd)


def _gdn_intra_head(q, k, v, beta, gl):
    same, causal, strict, eye, upper = _pair_masks()
    gl_row = jnp.sum(jnp.where(eye, gl, 0.0), axis=-2, keepdims=True)
    g_col = jnp.sum(jnp.where(causal, gl_row, 0.0), axis=-1, keepdims=True)
    g_row = jnp.sum(jnp.where(upper, gl, 0.0), axis=-2, keepdims=True)
    g_last = jnp.sum(jnp.where(same, gl_row, 0.0), axis=-1, keepdims=True)
    decay = jnp.where(causal, jnp.exp(jnp.where(causal, g_col - g_row, 0.0)), 0.0)
    kb = k * beta
    a = jnp.where(strict, _bdot(kb, k, _NT) * decay, 0.0)
    t_inv = _unit_lower_inverse(a)
    e_g = jnp.exp(g_col)
    u = _mm3(t_inv, v * beta)
    w = _mm3(t_inv, kb * e_g)
    qk = _bdot(q, k, _NT) * decay
    return w, u, q * e_g, k * jnp.exp(g_last - g_col), qk, jnp.exp(g_last)


def _gdn_scan_head(s, qd_a, kd_a, w_a, u_a, qk_a, dc_a, qd_b, kd_b, w_b, u_b, qk_b, dc_b):
    zeros = jnp.zeros((HEADS, CHUNK, HEAD_DIM), _F32)
    vn_a = u_a - _bdot(w_a, s, _NN)
    o_a = _bdot(qd_a, s, _NN) + _bdot(qk_a, jnp.concatenate([vn_a, zeros], axis=1), _NN)
    s1 = s * dc_a + _bdot(kd_a, vn_a, _TN)
    vn_b = u_b - _bdot(w_b, s1, _NN)
    o_b = _bdot(qd_b, s1, _NN) + _bdot(qk_b, jnp.concatenate([zeros, vn_b], axis=1), _NN)
    s2 = s1 * dc_b + _bdot(kd_b, vn_b, _TN)
    return o_a, o_b, s2


def _pick_scalar(t, row, lane_i):
    ri = lax.broadcasted_iota(jnp.int32, t.shape, 0)
    ci = lax.broadcasted_iota(jnp.int32, t.shape, 1)
    return jnp.sum(jnp.sum(jnp.where((ri == row) & (ci == lane_i), t, 0.0), axis=1, keepdims=True), axis=0, keepdims=True)


def _put_scalar(val, row, lane_i, shape):
    ri = lax.broadcasted_iota(jnp.int32, shape, 0)
    ci = lax.broadcasted_iota(jnp.int32, shape, 1)
    return jnp.where((ri == row) & (ci == lane_i), val, 0.0)


def _by_head(t):
    return jnp.stack(_heads(t, HEADS))


def _from_heads(t):
    return _cat([t[h] for h in range(HEADS)])


def _state_by_head(s):
    return jnp.stack([s[HEAD_DIM * h:HEAD_DIM * (h + 1), :] for h in range(HEADS)])


def _scan_ins(qd, kd, w, u, qk, dcrow, state):
    ins = [_state_by_head(state)]
    for r0 in (0, CHUNK):
        rs = slice(r0, r0 + CHUNK)
        ins += [_by_head(t[rs, :]) for t in (qd, kd, w, u, qk)]
        ins.append(jnp.stack([_pick_scalar(dcrow, r0, h) for h in range(HEADS)]))
    return ins


def _gdn_scan_fwd_fn(qd, kd, w, u, qk, dcrow, state):
    o_a, o_b, s2 = _gdn_scan_head(*_scan_ins(qd, kd, w, u, qk, dcrow, state))
    return jnp.concatenate([_from_heads(o_a), _from_heads(o_b)], axis=0), state, s2.reshape(state.shape)


def _gdn_scan_bwd_fn(qd, kd, w, u, qk, dcrow, state, d_o, d_state):
    cts = (_by_head(d_o[0:CHUNK, :]), _by_head(d_o[CHUNK:_PAIR, :]), _state_by_head(d_state))
    g = _vjp(_gdn_scan_head, _scan_ins(qd, kd, w, u, qk, dcrow, state), cts)
    grads = tuple(jnp.concatenate([_from_heads(g[1 + t]), _from_heads(g[7 + t])], axis=0) for t in range(5))
    d_dcrow = sum(_put_scalar(g[6][h], 0, h, dcrow.shape) + _put_scalar(g[12][h], CHUNK, h, dcrow.shape) for h in range(HEADS))
    return grads + (d_dcrow, g[0].reshape(state.shape))


def _gdn_intra_ins(q, k, v, bg):
    return [_by_head(q), _by_head(k), _by_head(v), jnp.stack([_lane_pick(bg, h) for h in range(HEADS)]),
            jnp.stack([_lane_pick(bg, HEADS + h) for h in range(HEADS)])]


def _gdn_intra_fwd_fn(q, k, v, bg):
    res = _gdn_intra_head(*_gdn_intra_ins(q, k, v, bg))
    dcrow = sum(_lane_put(res[5][h], h) for h in range(HEADS))
    return tuple(_from_heads(r) for r in res[:5]) + (dcrow,)


def _gdn_intra_bwd_fn(q, k, v, bg, d_w, d_u, d_qd, d_kd, d_qk, d_dcrow):
    cts = tuple(_by_head(d) for d in (d_w, d_u, d_qd, d_kd, d_qk)) + (jnp.stack([_lane_pick(d_dcrow, h) for h in range(HEADS)]),)
    g = _vjp(_gdn_intra_head, _gdn_intra_ins(q, k, v, bg), cts)
    d_bg = sum(_lane_put(g[3][h], h) + _lane_put(g[4][h], HEADS + h) for h in range(HEADS))
    return tuple(_from_heads(g[t]) for t in range(3)) + (d_bg,)


def _gdn_gates(ba, alog, dtb):
    lane = lax.broadcasted_iota(jnp.int32, ba.shape, 1)
    beta = _sigmoid(ba)
    g = -jnp.exp(alog) * _softplus(ba + dtb)
    return jnp.where(lane < HEADS, beta, jnp.where(lane < 2 * HEADS, g, 0.0))


def _gdn_q_head(c):
    return _l2n(_silu(c)) * HEAD_DIM ** -0.5


def _gdn_k_head(c):
    return _l2n(_silu(c))


def _gdn_prep_fwd_fn(qkv_c, ba, alog, dtb):
    hs = _heads(qkv_c, 3 * HEADS)
    q = _cat([_gdn_q_head(c) for c in hs[:HEADS]])
    k = _cat([_gdn_k_head(c) for c in hs[HEADS:2 * HEADS]])
    v = _cat([_silu(c) for c in hs[2 * HEADS:]])
    return q, k, v, _gdn_gates(ba, alog, dtb)


def _gdn_prep_bwd_fn(qkv_c, ba, d_q, d_k, d_v, d_bg, alog, dtb):
    hs = _heads(qkv_c, 3 * HEADS)
    dqs, dks, dvs = _heads(d_q, HEADS), _heads(d_k, HEADS), _heads(d_v, HEADS)
    parts = [_vjp(_gdn_q_head, [hs[h]], (dqs[h],))[0] for h in range(HEADS)]
    parts += [_vjp(_gdn_k_head, [hs[HEADS + h]], (dks[h],))[0] for h in range(HEADS)]
    parts += [_vjp(_silu, [hs[2 * HEADS + h]], (dvs[h],))[0] for h in range(HEADS)]
    d_ba, d_alog, d_dtb = _vjp(_gdn_gates, [ba, alog, dtb], (d_bg,))
    return _cat(parts), d_ba, d_alog, d_dtb


def _a_out_head(o, gate, gain):
    return _rms(o, gain) * _silu(gate)


def _a_post_fwd_fn(o, gate, qm, gain, mem_kv):
    parts = [_a_out_head(oh, gh, gain) for oh, gh in zip(_heads(o, HEADS), _heads(gate, HEADS))]
    parts.append(_mem_attn(qm, mem_kv[:, :MEM_WIDTH], mem_kv[:, MEM_WIDTH:]))
    return (_cat(parts),)


def _a_post_bwd_fn(o, gate, qm, d_cat, gain, mem_kv):
    d_os, d_gates = [], []
    d_gain = jnp.zeros(gain.shape, _F32)
    dc = _heads(d_cat, HEADS + 2)
    for h, (oh, gh) in enumerate(zip(_heads(o, HEADS), _heads(gate, HEADS))):
        g = _vjp(_a_out_head, [oh, gh, gain], (dc[h],))
        d_os.append(g[0])
        d_gates.append(g[1])
        d_gain = d_gain + g[2]
    d_qm, d_km, d_vm = _vjp(_mem_attn, [qm, mem_kv[:, :MEM_WIDTH], mem_kv[:, MEM_WIDTH:]], (d_cat[:, A_WIDTH:],))
    return _cat(d_os), _cat(d_gates + [d_qm]), d_gain, _cat([d_km, d_vm])


def _b_post_fwd_fn(o, qm, mem_kv):
    return (_cat([o.astype(_F32), _mem_attn(qm, mem_kv[:, :MEM_WIDTH], mem_kv[:, MEM_WIDTH:])]),)


def _b_post_bwd_fn(qm, d_cat_m, mem_kv):
    d_qm, d_km, d_vm = _vjp(_mem_attn, [qm, mem_kv[:, :MEM_WIDTH], mem_kv[:, MEM_WIDTH:]], (d_cat_m,))
    return d_qm, _cat([d_km, d_vm])


ATTN_SCALE = (HEAD_DIM + QK_ROPE) ** -0.5


def _rope_q_fwd_fn(qf, cos_t, sin_t):
    hs = _heads(qf, 2 * HEADS)
    return (_cat(hs[:HEADS] + [_rope(x, cos_t, sin_t) for x in hs[HEADS:]]) * ATTN_SCALE,)


def _rope_q_bwd_fn(d_qn, d_qr, cos_t, sin_t):
    f = lambda x: _rope(x, cos_t, sin_t)
    return (_cat([d_qn] + [_vjp(f, [x], (x,))[0] for x in _heads(d_qr, HEADS)]) * ATTN_SCALE,)


def _kv_prep_fwd_fn(ckr, cos_t, sin_t, gain):
    return _rms(ckr[:, :KV_LORA], gain), _rope(ckr[:, KV_LORA:], cos_t, sin_t)


def _kv_prep_bwd_fn(ckr, d_ckv, d_kr, cos_t, sin_t, gain):
    d_lat, d_gain = _vjp(_rms, [ckr[:, :KV_LORA], gain], (d_ckv,))
    f = lambda x: _rope(x, cos_t, sin_t)
    d_rope = _vjp(f, [ckr[:, KV_LORA:]], (d_kr,))[0]
    return _cat([d_lat, d_rope]), d_gain


def _rms_fwd(x, gain, *, name, tm=1024, out_dtype=_BF):
    tm = min(tm, x.shape[0])
    return _rowmap(lambda x_, g_: (_rms(x_.astype(_F32), g_),), [_t(x)], [gain], [(None, x.shape[1], out_dtype)], tm=tm, name=name)[0]


def _rms_bwd(x, d_xn, d_res, gain, *, name, tm=512):
    tm = min(tm, x.shape[0])

    def fn(x_, dxn_, *rest):
        g_ = rest[-1]
        dx, dg = _vjp(_rms, [x_.astype(_F32), g_], (dxn_.astype(_F32),))
        if d_res is not None:
            dx = dx + rest[0]
        return dx, dg

    tiles = [_t(x), _t(d_xn)] + ([_t(d_res)] if d_res is not None else [])
    return _rowmap(fn, tiles, [gain], [(None, x.shape[1], _F32)], [(gain.shape, _F32)], tm=tm, name=name)


FFN_COL_TILE = 1408
FFN_ROW_TILE = 512
FFN_DXN_ROW_TILE = 256
FFN_DOWN_ROW_TILE = 1024


def _ffn_gate_up(x, gain, w_gu, idx, *, name):
    s = x.shape[0]
    tm, tf = min(FFN_ROW_TILE, s), FFN_COL_TILE
    nf = D_FF // tf

    def body(x_ref, gain_ref, wg_ref, wu_ref, xn_ref, g_ref, u_ref, a_ref):
        @pl.when(pl.program_id(1) == 0)
        def _():
            xn_ref[...] = _rms(x_ref[...], gain_ref[...]).astype(xn_ref.dtype)

        xn = xn_ref[...]
        g, u = _dot(xn, wg_ref[...]), _dot(xn, wu_ref[...])
        g_ref[...] = g.astype(g_ref.dtype)
        u_ref[...] = u.astype(u_ref.dtype)
        a_ref[...] = (_silu(g) * u).astype(a_ref.dtype)

    col = pl.BlockSpec((tm, tf), lambda i, j: (i, j))
    wide = jax.ShapeDtypeStruct((s, D_FF), _BF)
    return _pcall(
        body, name=name, grid=(s // tm, nf),
        in_specs=[pl.BlockSpec((tm, D_MODEL), lambda i, j: (i, 0)), pl.BlockSpec((1, D_MODEL), lambda i, j: (0, 0)),
                  pl.BlockSpec((None, D_MODEL, tf), lambda i, j: (j, idx, 0)), pl.BlockSpec((None, D_MODEL, tf), lambda i, j: (nf + j, idx, 0))],
        out_specs=[pl.BlockSpec((tm, D_MODEL), lambda i, j: (i, 0)), col, col, col],
        out_shape=[jax.ShapeDtypeStruct((s, D_MODEL), _BF), wide, wide, wide],
        compiler_params=_cp(2),
    )(x, gain, w_gu, w_gu)


FFN_SHARD_ROWS = D_FF // N_CHIPS


def _w_down_specs(idx):
    return [pl.BlockSpec((None, FFN_SHARD_ROWS, D_MODEL), functools.partial(lambda i, j, q: (2 * j + q, idx, 0), q=q)) for q in (0, 1)]


def _ffn_down(a, w_down, idx, x, *, name):
    s = a.shape[0]
    tm = min(FFN_DOWN_ROW_TILE, s)

    def body(a_ref, w0_ref, w1_ref, w2_ref, w3_ref, x_ref, o_ref):
        w = jnp.concatenate([w0_ref[...], w1_ref[...], w2_ref[...], w3_ref[...]], axis=0)
        o_ref[...] = x_ref[...] + 0.5 * _dot(a_ref[...], w)

    rows = pl.BlockSpec((tm, D_MODEL), lambda i: (i, 0))
    w_specs = [pl.BlockSpec((None, FFN_SHARD_ROWS, D_MODEL), functools.partial(lambda i, k: (k, idx, 0), k=k)) for k in range(N_CHIPS)]
    return _pcall(
        body, name=name, grid=(s // tm,), in_specs=[pl.BlockSpec((tm, D_FF), lambda i: (i, 0))] + w_specs + [rows],
        out_specs=rows, out_shape=jax.ShapeDtypeStruct((s, D_MODEL), _F32), compiler_params=_cp(1),
    )(a, w_down, w_down, w_down, w_down, x)


def _ffn_fwd(x, gain, w_gu, w_down, idx, *, name):
    xn, g, u, a = _ffn_gate_up(x, gain, w_gu, idx, name=name + "_gu")
    y = _ffn_down(a, w_down, idx, x, name=name + "_down")
    return y, (x, xn, g, u, a)


def _ffn_d_gate_up(d_y, g, u, w_down, idx, *, name):
    s = d_y.shape[0]
    tm, tf = min(FFN_ROW_TILE, s), FFN_COL_TILE

    def body(dy_ref, wa_ref, wb_ref, g_ref, u_ref, dg_ref, du_ref):
        da = _dot_nt(dy_ref[...], jnp.concatenate([wa_ref[...], wb_ref[...]], axis=0)) * 0.5
        gg, uu = g_ref[...].astype(_F32), u_ref[...].astype(_F32)
        sg = _sigmoid(gg)
        dg_ref[...] = (da * uu * sg * (1.0 + gg * (1.0 - sg))).astype(dg_ref.dtype)
        du_ref[...] = (da * gg * sg).astype(du_ref.dtype)

    col = pl.BlockSpec((tm, tf), lambda i, j: (i, j))
    wide = jax.ShapeDtypeStruct((s, D_FF), _BF)
    return _pcall(
        body, name=name, grid=(s // tm, D_FF // tf),
        in_specs=[pl.BlockSpec((tm, D_MODEL), lambda i, j: (i, 0))] + _w_down_specs(idx) + [col, col],
        out_specs=[col, col], out_shape=[wide, wide], compiler_params=_cp(2),
    )(d_y, w_down, w_down, g, u)


def _ffn_d_x(d_g, d_u, w_gu, idx, x, d_y, gain, *, name):
    s = x.shape[0]
    tm, tf = min(FFN_DXN_ROW_TILE, s), FFN_COL_TILE

    def body(dg_ref, du_ref, w0_ref, w1_ref, w2_ref, w3_ref, x_ref, dy_ref, gain_ref, dx_ref, dgain_ref):
        @pl.when(pl.program_id(0) == 0)
        def _():
            dgain_ref[...] = jnp.zeros(dgain_ref.shape, _F32)

        d_xn = (_dot_nt(dg_ref[:, 0:tf], w0_ref[...]) + _dot_nt(dg_ref[:, tf:2 * tf], w1_ref[...])
                + _dot_nt(du_ref[:, 0:tf], w2_ref[...]) + _dot_nt(du_ref[:, tf:2 * tf], w3_ref[...]))
        dx, dgain = _vjp(_rms, [x_ref[...], gain_ref[...]], (d_xn,))
        dx_ref[...] = dx + dy_ref[...]
        dgain_ref[...] += dgain

    wide = pl.BlockSpec((tm, D_FF), lambda i: (i, 0))
    rows = pl.BlockSpec((tm, D_MODEL), lambda i: (i, 0))
    one = pl.BlockSpec((1, D_MODEL), lambda i: (0, 0))
    w_specs = [pl.BlockSpec((None, D_MODEL, tf), functools.partial(lambda i, k: (k, idx, 0), k=k)) for k in range(N_CHIPS)]
    return _pcall(
        body, name=name, grid=(s // tm,),
        in_specs=[wide, wide] + w_specs + [rows, rows, one],
        out_specs=[rows, one], out_shape=[jax.ShapeDtypeStruct((s, D_MODEL), _F32), jax.ShapeDtypeStruct((1, D_MODEL), _F32)],
        compiler_params=_cp(1),
    )(d_g, d_u, w_gu, w_gu, w_gu, w_gu, x, d_y, gain)


def _ffn_d_w_gu(xn, d_act, into, idx, first_chip, *, name, ts=1024):
    s = xn.shape[0]
    ts = min(ts, s)
    steps = s // ts

    def body(a_ref, b_ref, into_ref, o_ref):
        @pl.when(pl.program_id(1) == 0)
        def _():
            o_ref[...] = jnp.zeros(o_ref.shape, o_ref.dtype)

        o_ref[...] += _dot_tn(a_ref[...], b_ref[...])

    return _pcall(
        body, name=name, grid=(D_FF // FFN_COL_TILE, steps),
        in_specs=[pl.BlockSpec((ts, D_MODEL), lambda j, r: (r, 0)), pl.BlockSpec((ts, FFN_COL_TILE), lambda j, r: (r, j)),
                  pl.BlockSpec(memory_space=pl.ANY)],
        out_specs=pl.BlockSpec((None, None, D_MODEL, FFN_COL_TILE), lambda j, r: (first_chip + j, idx, 0, 0)),
        out_shape=jax.ShapeDtypeStruct(into.shape, into.dtype), input_output_aliases={2: 0}, compiler_params=_cp(2),
    )(xn, d_act, into)


def _ffn_d_w_down(a, d_y, into, idx, *, name, ts=1024):
    s = a.shape[0]
    ts = min(ts, s)
    steps = s // ts

    def body(a_ref, b_ref, into_ref, o_ref):
        @pl.when(pl.program_id(1) == 0)
        def _():
            o_ref[...] = jnp.zeros(o_ref.shape, o_ref.dtype)

        part = _dot_tn(a_ref[...], b_ref[...]) * 0.5
        o_ref[0] += part[0:FFN_SHARD_ROWS, :]
        o_ref[1] += part[FFN_SHARD_ROWS:2 * FFN_SHARD_ROWS, :]

    return _pcall(
        body, name=name, grid=(D_FF // FFN_COL_TILE, steps),
        in_specs=[pl.BlockSpec((ts, FFN_COL_TILE), lambda i, r: (r, i)), pl.BlockSpec((ts, D_MODEL), lambda i, r: (r, 0)),
                  pl.BlockSpec(memory_space=pl.ANY)],
        out_specs=pl.BlockSpec((2, None, FFN_SHARD_ROWS, D_MODEL), lambda i, r: (i, idx, 0, 0)),
        out_shape=jax.ShapeDtypeStruct(into.shape, into.dtype), input_output_aliases={2: 0}, compiler_params=_cp(2),
    )(a, d_y, into)


def _ffn_bwd(d_y, saved, gain, w_gu, w_down, idx, g_gu, g_down, *, name):
    x, xn, g, u, a = saved
    d_g, d_u = _ffn_d_gate_up(d_y, g, u, w_down, idx, name=name + "_dgu")
    g_down = _ffn_d_w_down(a, d_y, g_down, idx, name=name + "_dwd")
    g_gu = _ffn_d_w_gu(xn, d_g, g_gu, idx, 0, name=name + "_dwg")
    g_gu = _ffn_d_w_gu(xn, d_u, g_gu, idx, 2, name=name + "_dwu")
    d_x, d_gain = _ffn_d_x(d_g, d_u, w_gu, idx, x, d_y, gain, name=name + "_dx")
    return d_x, d_gain, g_gu, g_down


def _conv_fwd(h, w, *, name, tm=256):
    s = h.shape[0]
    tm = min(tm, s)
    c = 3 * A_WIDTH
    halo = SUBLANE

    def body(x_ref, prev_ref, w_ref, o_ref, buf):
        i = pl.program_id(0)
        buf[0:halo, :] = jnp.where(i == 0, 0.0, prev_ref[...])
        buf[halo:halo + tm, :] = x_ref[...]
        acc = jnp.zeros((tm, c), _F32)
        for j in range(CONV_K):
            acc = acc + buf[pl.ds(halo - (CONV_K - 1) + j, tm), :] * w_ref[j:j + 1, :]
        o_ref[...] = acc

    return _pcall(
        body, name=name, grid=(s // tm,),
        in_specs=[pl.BlockSpec((tm, c), lambda i: (i, 0)),
                  pl.BlockSpec((halo, c), lambda i: (jnp.maximum(i * (tm // halo) - 1, 0), 0)),
                  pl.BlockSpec(w.shape, lambda i: (0, 0))],
        out_specs=pl.BlockSpec((tm, c), lambda i: (i, 0)), out_shape=jax.ShapeDtypeStruct((s, c), _F32),
        scratch_shapes=[pltpu.VMEM((tm + 2 * halo, c), _F32)],
        compiler_params=_cp(1),
    )(h, h, w)


def _conv_bwd(h, d_y, w, *, name, tm=256):
    s = h.shape[0]
    tm = min(tm, s)
    c = 3 * A_WIDTH
    halo = SUBLANE
    steps = s // tm

    def body(x_ref, prev_ref, dy_ref, next_ref, w_ref, dx_ref, dw_ref, xbuf, dybuf):
        i = pl.program_id(0)

        @pl.when(i == 0)
        def _():
            dw_ref[...] = jnp.zeros(dw_ref.shape, dw_ref.dtype)

        xbuf[0:halo, :] = jnp.where(i == 0, 0.0, prev_ref[...])
        xbuf[halo:halo + tm, :] = x_ref[...]
        dybuf[0:tm, :] = dy_ref[...]
        dybuf[tm:tm + halo, :] = jnp.where(i == steps - 1, 0.0, next_ref[...])
        dy = dy_ref[...]
        acc = jnp.zeros((tm, c), _F32)
        for j in range(CONV_K):
            acc = acc + dybuf[pl.ds(CONV_K - 1 - j, tm), :] * w_ref[j:j + 1, :]
            dw_ref[j:j + 1, :] += jnp.sum(dy * xbuf[pl.ds(halo - (CONV_K - 1) + j, tm), :], axis=0, keepdims=True)
        dx_ref[...] = acc

    return _pcall(
        body, name=name, grid=(steps,),
        in_specs=[pl.BlockSpec((tm, c), lambda i: (i, 0)),
                  pl.BlockSpec((halo, c), lambda i: (jnp.maximum(i * (tm // halo) - 1, 0), 0)),
                  pl.BlockSpec((tm, c), lambda i: (i, 0)),
                  pl.BlockSpec((halo, c), lambda i: (jnp.minimum((i + 1) * (tm // halo), s // halo - 1), 0)),
                  pl.BlockSpec(w.shape, lambda i: (0, 0))],
        out_specs=[pl.BlockSpec((tm, c), lambda i: (i, 0)), pl.BlockSpec(w.shape, lambda i: (0, 0))],
        out_shape=[jax.ShapeDtypeStruct((s, c), _F32), jax.ShapeDtypeStruct(w.shape, _F32)],
        scratch_shapes=[pltpu.VMEM((tm + 2 * halo, c), _F32), pltpu.VMEM((tm + 2 * halo, c), _F32)],
        compiler_params=_cp(1),
    )(h, h, d_y, d_y, w)


ATTN_Q_BLOCK = 2048
ATTN_K_SUB = 256
ATTN_BWD_BLOCK = 1024
ATTN_BWD_SUB = 512


def _chunk_mask(shape, q_axis):
    qi = lax.broadcasted_iota(jnp.int32, shape, q_axis) // CHUNK
    ki = lax.broadcasted_iota(jnp.int32, shape, 1 - q_axis) // CHUNK
    return ki <= qi


def _rows(j, t):
    return pl.ds(pl.multiple_of(j * t, t), t)


def _chunk_mask_at(shape, q_axis, q_off):
    qi = (lax.broadcasted_iota(jnp.int32, shape, q_axis) + q_off) // CHUNK
    ki = lax.broadcasted_iota(jnp.int32, shape, 1 - q_axis) // CHUNK
    return ki <= qi


def _attn_fwd(q_all, kv, kr, *, name):
    s = q_all.shape[0]
    t = min(ATTN_Q_BLOCK, s)
    tk = min(ATTN_K_SUB, t)
    nq, sub, rep = s // t, t // tk, tk // LANE

    def body(qn_ref, qr_ref, kn_ref, kr_ref, v_ref, o_ref, lse_ref, m_sc, acc_sc):
        i = pl.program_id(1)
        m_sc[...] = jnp.full(m_sc.shape, -1e30, _F32)
        acc_sc[...] = jnp.zeros(acc_sc.shape, _F32)
        ones = jnp.ones((tk, LANE), _BF)

        def block(j, first_row):
            qs = slice(first_row, t)
            rows = _rows(j, tk)
            q = _cat([qn_ref[qs, :], qr_ref[qs, :]])
            sc = _dot_nt(q, _cat([kn_ref[rows, :], kr_ref[rows, :]]))
            if first_row is not None:
                sc = jnp.where(_chunk_mask(sc.shape, 0), sc, -1e30)
            m_prev = m_sc[qs, :]
            m_new = jnp.maximum(m_prev, jnp.max(sc, axis=-1, keepdims=True))
            alpha = jnp.exp(m_prev - m_new)
            p = jnp.exp(sc - _cat([m_new] * rep))
            acc_sc[qs, :] = _cat([alpha, alpha]) * acc_sc[qs, :] + _dot(p, _cat([v_ref[rows, :], ones]))
            m_sc[qs, :] = m_new

        def step(j, carry):
            block(j, None)
            return carry

        lax.fori_loop(0, i * sub, step, 0)
        for u in range(sub):
            block(i * sub + u, u * tk)
        row_sum = acc_sc[:, LANE:2 * LANE]
        o_ref[...] = acc_sc[:, 0:LANE] / row_sum
        lse_ref[...] = m_sc[...] + jnp.log(row_sum)

    return _pcall(
        body, name=name, grid=(HEADS, nq),
        in_specs=[pl.BlockSpec((t, LANE), lambda h, i: (i, h)),
                  pl.BlockSpec((t, LANE), lambda h, i: (i, HEADS + h)),
                  pl.BlockSpec((s, LANE), lambda h, i: (0, h)),
                  pl.BlockSpec((s, LANE), lambda h, i: (0, 0)),
                  pl.BlockSpec((s, LANE), lambda h, i: (0, HEADS + h))],
        out_specs=[pl.BlockSpec((t, LANE), lambda h, i: (i, h)), pl.BlockSpec((t, LANE), lambda h, i: (i, h))],
        out_shape=[jax.ShapeDtypeStruct((s, A_WIDTH), _F32), jax.ShapeDtypeStruct((s, A_WIDTH), _F32)],
        scratch_shapes=[pltpu.VMEM((t, LANE), _F32), pltpu.VMEM((t, 2 * LANE), _F32)],
        compiler_params=_cp(2),
    )(q_all, q_all, kv, kr, kv)


def _attn_bwd_prep(o, lse, d_cat, *, name):
    s = o.shape[0]
    t = min(ATTN_BWD_SUB, s)
    nq = s // t

    def body(o_ref, lse_ref, do_ref, dob_ref, lset_ref, dlt_ref):
        for h in range(HEADS):
            sl = slice(LANE * h, LANE * (h + 1))
            rows = slice(SUBLANE * h, SUBLANE * (h + 1))
            do = do_ref[:, sl]
            dl = jnp.broadcast_to(jnp.sum(o_ref[:, sl] * do, axis=-1, keepdims=True), (t, LANE))
            dob_ref[:, sl] = do.astype(dob_ref.dtype)
            dlt_ref[rows, :] = dl.T[0:SUBLANE, :]
            lset_ref[rows, :] = lse_ref[:, sl].T[0:SUBLANE, :]

    wide = pl.BlockSpec((t, A_WIDTH), lambda i: (i, 0))
    stat = pl.BlockSpec((HEADS * SUBLANE, t), lambda i: (i, 0))
    stat_shape = jax.ShapeDtypeStruct((nq * HEADS * SUBLANE, t), _F32)
    return _pcall(
        body, name=name, grid=(nq,), in_specs=[wide, wide, wide], out_specs=[wide, stat, stat],
        out_shape=[jax.ShapeDtypeStruct((s, A_WIDTH), _BF), stat_shape, stat_shape], compiler_params=_cp(1),
    )(o, lse, d_cat)


ATTN_BWD_VMEM = 56 * 2 ** 20


def _attn_bwd(q_all, kv, kr, lse_t, delta_t, d_o, *, name):
    s = q_all.shape[0]
    t = min(ATTN_BWD_BLOCK, s)
    tq = min(ATTN_BWD_SUB, t)
    nk, sub, nqs = s // t, t // tq, s // tq

    def body(kn_ref, kr_ref, v_ref, qn_ref, qr_ref, do_ref, lset_ref, dlt_ref, dkn_ref, dv_ref, dkr_ref, dqn_ref, dqr_ref, dk_sc, dv_sc):
        h, j = pl.program_id(0), pl.program_id(1)
        dk_sc[...] = jnp.zeros(dk_sc.shape, _F32)
        dv_sc[...] = jnp.zeros(dv_sc.shape, _F32)

        @pl.when(j == 0)
        def _():
            dqn_ref[...] = jnp.zeros(dqn_ref.shape, _F32)
            dqr_ref[...] = jnp.zeros(dqr_ref.shape, _F32)

        def block(i, query_off):
            ks = slice(0, t if query_off is None else query_off + tq)
            rows = _rows(i, tq)
            stat = pl.ds(pl.multiple_of((i * HEADS + h) * SUBLANE, SUBLANE), 1)
            q = _cat([qn_ref[rows, :], qr_ref[rows, :]])
            do = do_ref[rows, :]
            k = _cat([kn_ref[ks, :], kr_ref[ks, :]])
            p = jnp.exp(_dot_nt(k, q) - lset_ref[stat, :])
            if query_off is not None:
                p = jnp.where(_chunk_mask_at(p.shape, 1, query_off), p, 0.0)
            dv_sc[ks, :] += _dot(p, do)
            ds = p * (_dot_nt(v_ref[ks, :], do) - dlt_ref[stat, :])
            dk_sc[ks, :] += _dot(ds, q)
            dq = _dot_tn(ds, k)
            dqn_ref[rows, :] += dq[:, 0:LANE]
            dqr_ref[rows, :] += dq[:, LANE:2 * LANE]

        def step(i, carry):
            block(i, None)
            return carry

        for u in range(sub):
            block(j * sub + u, u * tq)
        lax.fori_loop((j + 1) * sub, nqs, step, 0)
        dkn_ref[...] = dk_sc[:, 0:LANE]
        dkr_ref[...] = dk_sc[:, LANE:2 * LANE]
        dv_ref[...] = dv_sc[...]

    once = pl.Buffered(1)
    stats = pl.BlockSpec((nqs * HEADS * SUBLANE, tq), lambda h, j: (0, 0), pipeline_mode=once)
    per_head = lambda at: pl.BlockSpec((s, LANE), at, pipeline_mode=once)
    return _pcall(
        body, name=name, grid=(HEADS, nk),
        in_specs=[pl.BlockSpec((t, LANE), lambda h, j: (j, h)),
                  pl.BlockSpec((t, LANE), lambda h, j: (j, 0)),
                  pl.BlockSpec((t, LANE), lambda h, j: (j, HEADS + h)),
                  per_head(lambda h, j: (0, h)), per_head(lambda h, j: (0, HEADS + h)), per_head(lambda h, j: (0, h)),
                  stats, stats],
        out_specs=[pl.BlockSpec((t, LANE), lambda h, j: (j, h))] * 3 + [per_head(lambda h, j: (0, h))] * 2,
        out_shape=[jax.ShapeDtypeStruct((s, A_WIDTH), _F32)] * 5,
        scratch_shapes=[pltpu.VMEM((t, 2 * LANE), _F32), pltpu.VMEM((t, LANE), _F32)],
        compiler_params=pltpu.CompilerParams(dimension_semantics=("arbitrary", "arbitrary"), vmem_limit_bytes=ATTN_BWD_VMEM),
    )(kv, kr, kv, q_all, q_all, d_o, lse_t, delta_t)


def _final_loss(x, tgt, gain, *, name, tm=512):
    tm = min(tm, x.shape[0])

    def fn(x_, t_, g_):
        def f(xx, gg):
            err = _rms(xx, gg) - t_
            return 0.5 * jnp.sum(jnp.sum(err * err, axis=1, keepdims=True) / D_MODEL, axis=0, keepdims=True)

        loss, pull = jax.vjp(f, x_, g_)
        dx, dg = pull(jnp.ones((1, 1), _F32))
        return dx, dg, jnp.broadcast_to(loss, (SUBLANE, LANE))

    return _rowmap(fn, [_t(x), _t(tgt)], [gain], [(None, D_MODEL, _F32)], [(gain.shape, _F32), ((SUBLANE, LANE), _F32)], tm=tm, name=name)


def _local_step(x, mem, cos_t, sin_t, tgt, p):
    s = x.shape[0]
    g = {}
    row = lambda a: a.reshape(1, -1)
    tm_e = min(256, s)

    mem_n = _rms_fwd(mem, row(p["mem_norm"]), name="mem_norm")
    mem_kv_all = _mm_nn(mem_n, p["w_mem_all"], out_dtype=_F32, name="mem_kv", tn=1024)
    mem_kv = [mem_kv_all[:, 2 * MEM_WIDTH * l:2 * MEM_WIDTH * (l + 1)] for l in range(DEPTH)]

    sv = []
    for i in range(N_A):
        l = i
        r = {}
        r["x0"] = x
        x, r["ffn1"] = _ffn_fwd(x, row(p["ffn1_norm"][l]), p["ffn_w_gu"], p["ffn_w_down"], l, name=f"a{i}_ffn1")
        r["x1"] = x
        xn = _rms_fwd(x, row(p["mix_norm"][l]), name=f"a{i}_mixnorm")
        h = _mm_nn(xn, p["a_w_in"][i], out_dtype=_F32, name=f"a{i}_in", tn=1152)
        qkv_c = _conv_fwd(h, p["a_conv"][i], name=f"a{i}_conv")
        alog, dtb = p["a_A_log_row"][i], p["a_dt_bias_row"][i]
        q, k, v, bg = _rowmap(_gdn_prep_fwd_fn, [_t(qkv_c), _t(h, LANE, 26)], [alog, dtb],
                              [(None, A_WIDTH, _F32)] * 3 + [(None, LANE, _F32)], tm=tm_e, name=f"a{i}_prep")
        w_, u_, qd, kd, qk, dcrow = _rowmap(_gdn_intra_fwd_fn, [_t(q), _t(k), _t(v), _t(bg)], [],
                                            [(None, A_WIDTH, _F32)] * 5 + [(None, LANE, _F32)], tm=_PAIR, name=f"a{i}_intra")
        o, states = _rowmap(_gdn_scan_fwd_fn, [_t(qd), _t(kd), _t(w_), _t(u_), _t(qk), _t(dcrow)], [],
                            [(None, A_WIDTH, _F32), (A_WIDTH, HEAD_DIM, _F32)], tm=_PAIR, name=f"a{i}_scan",
                            carry=[((A_WIDTH, HEAD_DIM), _F32)])
        gain_o = row(p["a_out_norm"][i])
        cat = _rowmap(_a_post_fwd_fn, [_t(o), _t(h, A_WIDTH, 3), _t(h, MEM_WIDTH, 12)], [gain_o, mem_kv[l]],
                      [(None, D_MODEL, _BF)], tm=tm_e, name=f"a{i}_post")[0]
        x = _mm_nn(cat, p["w_out"][l], out_dtype=_F32, name=f"a{i}_out", res=x, tn=1024)
        r.update(xn=xn, h=h, qkv_c=qkv_c, q=q, k=k, v=v, bg=bg, w=w_, u=u_, qd=qd, kd=kd, qk=qk, dcrow=dcrow, o=o, states=states, cat=cat)
        r["x2"] = x
        x, r["ffn2"] = _ffn_fwd(x, row(p["ffn2_norm"][l]), p["ffn_w_gu"], p["ffn_w_down"], DEPTH + l, name=f"a{i}_ffn2")
        sv.append(r)

    kvs = {"x": x}
    xn_kv = _rms_fwd(x, row(p["kv_in_norm"]), name="kv_innorm")
    ckr = _mm_nn(xn_kv, p["w_dkv"], out_dtype=_F32, name="kv_down")
    ckv, k_rope = _rowmap(_kv_prep_fwd_fn, [_t(ckr), _t(cos_t), _t(sin_t)], [row(p["kv_lat_norm"])],
                          [(None, KV_LORA, _BF), (None, LANE, _BF)], tm=tm_e, name="kv_prep")
    kvu = _mm_nn(ckv, p["w_ukv"], out_dtype=_BF, name="kv_up")
    kvs.update(xn=xn_kv, ckr=ckr, ckv=ckv)

    for j in range(N_B):
        l = N_A + j
        r = {}
        x, r["ffn1"] = _ffn_fwd(x, row(p["ffn1_norm"][l]), p["ffn_w_gu"], p["ffn_w_down"], l, name=f"b{j}_ffn1")
        r["x1"] = x
        xn = _rms_fwd(x, row(p["mix_norm"][l]), name=f"b{j}_mixnorm")
        h = _mm_nn(xn, p["b_w_in"][j], out_dtype=_F32, name=f"b{j}_in")
        gain_q = row(p["b_q_norm"][j])
        cqn = _rowmap(lambda c_, g_: (_rms(c_, g_),), [_t(h, Q_LORA, 0)], [gain_q], [(None, Q_LORA, _BF)], tm=tm_e, name=f"b{j}_qnorm")[0]
        qf = _mm_nn(cqn, p["b_w_uq"][j], out_dtype=_F32, name=f"b{j}_uq")
        q_all = _rowmap(_rope_q_fwd_fn, [_t(qf), _t(cos_t), _t(sin_t)], [], [(None, UQ_PAD, _BF)], tm=tm_e, name=f"b{j}_rope")[0]
        o_b, lse = _attn_fwd(q_all, kvu, k_rope, name=f"b{j}_attn")
        cat = _rowmap(_b_post_fwd_fn, [_t(o_b), _t(h, MEM_WIDTH, 1)], [mem_kv[l]], [(None, D_MODEL, _BF)], tm=tm_e, name=f"b{j}_post")[0]
        x = _mm_nn(cat, p["w_out"][l], out_dtype=_F32, name=f"b{j}_out", res=x, tn=1024)
        r.update(xn=xn, h=h, cqn=cqn, q_all=q_all, o_b=o_b, lse=lse, cat=cat)
        x, r["ffn2"] = _ffn_fwd(x, row(p["ffn2_norm"][l]), p["ffn_w_gu"], p["ffn_w_down"], DEPTH + l, name=f"b{j}_ffn2")
        sv.append(r)

    dx, g["final_norm"], loss = _final_loss(x, tgt, row(p["final_norm"]), name="loss")

    per_layer = lambda: [None] * DEPTH
    for n in ("ffn1_norm", "mix_norm", "ffn2_norm", "w_out", "mem_kv"):
        g[n] = per_layer()
    g["ffn_w_gu"] = jnp.zeros((N_CHIPS, 2 * DEPTH, D_MODEL, FFN_COL_TILE), _F32)
    g["ffn_w_down"] = jnp.zeros((N_CHIPS, 2 * DEPTH, FFN_SHARD_ROWS, D_MODEL), _F32)
    for n in ("a_w_in", "a_conv", "a_A_log_row", "a_dt_bias_row", "a_out_norm", "b_w_in", "b_q_norm", "b_w_uq"):
        g[n] = [None] * N_A
    d_kv_parts = []

    for j in reversed(range(N_B)):
        l = N_A + j
        r = sv[l]
        dx, g["ffn2_norm"][l], g["ffn_w_gu"], g["ffn_w_down"] = _ffn_bwd(
            dx, r["ffn2"], row(p["ffn2_norm"][l]), p["ffn_w_gu"], p["ffn_w_down"], DEPTH + l, g["ffn_w_gu"], g["ffn_w_down"], name=f"b{j}_ffn2b")
        d_cat = _mm_nt(dx, p["w_out"][l], out_dtype=_F32, name=f"b{j}_dcat", tn=1024)
        g["w_out"][l] = _mm_tn(r["cat"], dx, name=f"b{j}_dwout", tn=1024)
        d_qm, g["mem_kv"][l] = _rowmap(_b_post_bwd_fn, [_t(r["h"], MEM_WIDTH, 1), _t(d_cat, MEM_WIDTH, 3)], [mem_kv[l]],
                                      [(None, MEM_WIDTH, _F32)], [((N_MEM, 2 * MEM_WIDTH), _F32)], tm=tm_e, name=f"b{j}_postb")
        d_o, lse_t, delta_t = _attn_bwd_prep(r["o_b"], r["lse"], d_cat, name=f"b{j}_delta")
        dkn, dv, dkr, dqn, dqr = _attn_bwd(r["q_all"], kvu, k_rope, lse_t, delta_t, d_o, name=f"b{j}_attn_bwd")
        d_kv_parts.append((dkn, dv, dkr))
        d_qf = _rowmap(_rope_q_bwd_fn, [_t(dqn), _t(dqr), _t(cos_t), _t(sin_t)], [], [(None, UQ_PAD, _F32)], tm=tm_e, name=f"b{j}_ropeb")[0]
        d_cqn = _mm_nt(d_qf, p["b_w_uq"][j], out_dtype=_F32, name=f"b{j}_dcqn")
        g["b_w_uq"][j] = _mm_tn(r["cqn"], d_qf, name=f"b{j}_dwuq")
        gain_q = row(p["b_q_norm"][j])
        d_cq, g["b_q_norm"][j] = _rowmap(lambda c_, d_, g_: _vjp(_rms, [c_, g_], (d_,)), [_t(r["h"], Q_LORA, 0), _t(d_cqn)], [gain_q],
                                        [(None, Q_LORA, _F32)], [((1, Q_LORA), _F32)], tm=tm_e, name=f"b{j}_qnormb")
        d_h = jnp.concatenate([d_cq, d_qm], axis=1)
        d_xn = _mm_nt(d_h, p["b_w_in"][j], out_dtype=_F32, name=f"b{j}_dxn", tn=1024)
        g["b_w_in"][j] = _mm_tn(r["xn"], d_h, name=f"b{j}_dwin")
        dx, g["mix_norm"][l] = _rms_bwd(r["x1"], d_xn, dx, row(p["mix_norm"][l]), name=f"b{j}_mixnormb")
        dx, g["ffn1_norm"][l], g["ffn_w_gu"], g["ffn_w_down"] = _ffn_bwd(
            dx, r["ffn1"], row(p["ffn1_norm"][l]), p["ffn_w_gu"], p["ffn_w_down"], l, g["ffn_w_gu"], g["ffn_w_down"], name=f"b{j}_ffn1b")

    def kv_sum(*parts):
        dkn = sum(parts[0::3][1:], parts[0])
        dv = sum(parts[1::3][1:], parts[1])
        dkr = sum(parts[2::3][1:], parts[2])
        return _cat([dkn, dv]), sum(_heads(dkr, HEADS)[1:], _heads(dkr, HEADS)[0])

    d_kvu, d_kr = _rowmap(kv_sum, [_t(a) for part in d_kv_parts for a in part], [], [(None, 2 * A_WIDTH, _F32), (None, LANE, _F32)],
                          tm=tm_e, name="kv_dsum")
    d_ckv = _mm_nt(d_kvu, p["w_ukv"], out_dtype=_F32, name="kv_dckv")
    g["w_ukv"] = _mm_tn(kvs["ckv"], d_kvu, name="kv_dwukv")
    d_ckr, g["kv_lat_norm"] = _rowmap(_kv_prep_bwd_fn, [_t(kvs["ckr"]), _t(d_ckv), _t(d_kr), _t(cos_t), _t(sin_t)], [row(p["kv_lat_norm"])],
                                     [(None, DKV_PAD, _F32)], [((1, KV_LORA), _F32)], tm=tm_e, name="kv_prepb")
    d_xn = _mm_nt(d_ckr, p["w_dkv"], out_dtype=_F32, name="kv_dxn", tn=1024)
    g["w_dkv"] = _mm_tn(kvs["xn"], d_ckr, name="kv_dwdkv")
    dx, g["kv_in_norm"] = _rms_bwd(kvs["x"], d_xn, dx, row(p["kv_in_norm"]), name="kv_innormb")

    for i in reversed(range(N_A)):
        l = i
        r = sv[l]
        dx, g["ffn2_norm"][l], g["ffn_w_gu"], g["ffn_w_down"] = _ffn_bwd(
            dx, r["ffn2"], row(p["ffn2_norm"][l]), p["ffn_w_gu"], p["ffn_w_down"], DEPTH + l, g["ffn_w_gu"], g["ffn_w_down"], name=f"a{i}_ffn2b")
        d_cat = _mm_nt(dx, p["w_out"][l], out_dtype=_F32, name=f"a{i}_dcat", tn=1024)
        g["w_out"][l] = _mm_tn(r["cat"], dx, name=f"a{i}_dwout", tn=1024)
        gain_o = row(p["a_out_norm"][i])
        h = r["h"]
        d_o, d_hpart, g["a_out_norm"][i], g["mem_kv"][l] = _rowmap(
            _a_post_bwd_fn, [_t(r["o"]), _t(h, A_WIDTH, 3), _t(h, MEM_WIDTH, 12), _t(d_cat)], [gain_o, mem_kv[l]],
            [(None, A_WIDTH, _F32), (None, D_MODEL, _F32)], [((1, HEAD_DIM), _F32), ((N_MEM, 2 * MEM_WIDTH), _F32)], tm=tm_e, name=f"a{i}_postb")
        d_qd, d_kd, d_w, d_u, d_qk, d_dcrow = _rowmap(
            _gdn_scan_bwd_fn, [_t(r["qd"]), _t(r["kd"]), _t(r["w"]), _t(r["u"]), _t(r["qk"]), _t(r["dcrow"]), _t(r["states"], rows=A_WIDTH), _t(d_o)], [],
            [(None, A_WIDTH, _F32)] * 5 + [(None, LANE, _F32)], tm=_PAIR, name=f"a{i}_scanb", carry=[((A_WIDTH, HEAD_DIM), _F32)], reverse=True)
        d_q, d_k, d_v, d_bg = _rowmap(
            _gdn_intra_bwd_fn, [_t(r["q"]), _t(r["k"]), _t(r["v"]), _t(r["bg"]), _t(d_w), _t(d_u), _t(d_qd), _t(d_kd), _t(d_qk), _t(d_dcrow)], [],
            [(None, A_WIDTH, _F32)] * 3 + [(None, LANE, _F32)], tm=_PAIR, name=f"a{i}_intrab")
        alog, dtb = p["a_A_log_row"][i], p["a_dt_bias_row"][i]
        d_qkv_c, d_ba, g["a_A_log_row"][i], g["a_dt_bias_row"][i] = _rowmap(
            _gdn_prep_bwd_fn, [_t(r["qkv_c"]), _t(h, LANE, 26), _t(d_q), _t(d_k), _t(d_v), _t(d_bg)], [alog, dtb],
            [(None, 3 * A_WIDTH, _F32), (None, LANE, _F32)], [((1, LANE), _F32), ((1, LANE), _F32)], tm=tm_e, name=f"a{i}_prepb")
        d_qkv, g["a_conv"][i] = _conv_bwd(h, d_qkv_c, p["a_conv"][i], name=f"a{i}_convb")
        d_h = jnp.concatenate([d_qkv, d_hpart, d_ba], axis=1)
        d_xn = _mm_nt(d_h, p["a_w_in"][i], out_dtype=_F32, name=f"a{i}_dxn", tn=1024)
        g["a_w_in"][i] = _mm_tn(r["xn"], d_h, name=f"a{i}_dwin", tn=1152)
        dx, g["mix_norm"][l] = _rms_bwd(r["x1"], d_xn, dx, row(p["mix_norm"][l]), name=f"a{i}_mixnormb")
        dx, g["ffn1_norm"][l], g["ffn_w_gu"], g["ffn_w_down"] = _ffn_bwd(
            dx, r["ffn1"], row(p["ffn1_norm"][l]), p["ffn_w_gu"], p["ffn_w_down"], l, g["ffn_w_gu"], g["ffn_w_down"], name=f"a{i}_ffn1b")

    d_mem_kv_all = jnp.concatenate(g.pop("mem_kv"), axis=1)
    d_mem_n = _mm_nt(d_mem_kv_all, p["w_mem_all"], out_dtype=_F32, name="mem_dn", tn=1024)
    g["w_mem_all"] = _mm_tn(mem_n, d_mem_kv_all, name="mem_dw", tn=1024)
    _, g["mem_norm"] = _rms_bwd(mem, d_mem_n, None, row(p["mem_norm"]), name="mem_normb")
    return loss, dx, g


_NOPE_ROPE = HEAD_DIM + QK_ROPE
_QKV_GATE = 4 * A_WIDTH
_BETA_AT = _QKV_GATE + MEM_WIDTH


def _lane_row(vals, at):
    n = vals.shape[0]
    return jnp.concatenate([jnp.zeros((at,), _F32), vals.astype(_F32), jnp.zeros((LANE - at - n,), _F32)]).reshape(1, LANE)


def _compute_form(w, conv_f32, ffn_w_gu, ffn_w_down):
    p = {n: w[n] for n in ("ffn1_norm", "mix_norm", "ffn2_norm", "w_out", "mem_norm", "a_out_norm", "b_w_in", "b_q_norm", "kv_in_norm",
                           "kv_lat_norm", "final_norm")}
    p["ffn_w_gu"], p["ffn_w_down"] = ffn_w_gu, ffn_w_down
    wm = w["w_mem_kv"]
    p["w_mem_all"] = jnp.transpose(wm, (1, 0, 2)).reshape(D_MODEL, DEPTH * 2 * MEM_WIDTH)
    a = w["a_w_in"]
    pad = jnp.zeros((N_A, D_MODEL, A_IN_PAD - A_IN), a.dtype)
    p["a_w_in"] = jnp.concatenate([a[:, :, :_QKV_GATE], a[:, :, _QKV_GATE + 2 * HEADS:], a[:, :, _QKV_GATE:_QKV_GATE + 2 * HEADS], pad], axis=2)
    p["a_conv"] = jnp.concatenate([conv_f32, jnp.zeros((N_A, SUBLANE - CONV_K, 3 * A_WIDTH), _F32)], axis=1)
    p["a_A_log_row"] = [_lane_row(w["a_A_log"][i], HEADS) for i in range(N_A)]
    p["a_dt_bias_row"] = [_lane_row(w["a_dt_bias"][i], HEADS) for i in range(N_A)]
    uq = w["b_w_uq"].reshape(N_B, Q_LORA, HEADS, _NOPE_ROPE)
    rope = jnp.concatenate([uq[..., HEAD_DIM:], jnp.zeros((N_B, Q_LORA, HEADS, LANE - QK_ROPE), uq.dtype)], axis=-1)
    p["b_w_uq"] = jnp.concatenate([uq[..., :HEAD_DIM].reshape(N_B, Q_LORA, A_WIDTH), rope.reshape(N_B, Q_LORA, A_WIDTH)], axis=-1)
    dkv = w["w_dkv"]
    p["w_dkv"] = jnp.concatenate([dkv, jnp.zeros((D_MODEL, DKV_PAD - dkv.shape[1]), dkv.dtype)], axis=1)
    ukv = w["w_ukv"].reshape(KV_LORA, HEADS, 2 * HEAD_DIM)
    p["w_ukv"] = jnp.concatenate([ukv[..., :HEAD_DIM].reshape(KV_LORA, A_WIDTH), ukv[..., HEAD_DIM:].reshape(KV_LORA, A_WIDTH)], axis=-1)
    return p


def _natural_grads(g):
    st = lambda xs: jnp.stack(xs, axis=0)
    n = {}
    for k in ("ffn1_norm", "mix_norm", "ffn2_norm"):
        n[k] = st(g[k]).reshape(DEPTH, D_MODEL)
    for k in ("w_out", "b_w_in"):
        n[k] = st(g[k])
    n["mem_norm"] = g["mem_norm"].reshape(D_MODEL)
    n["w_mem_kv"] = jnp.transpose(g["w_mem_all"].reshape(D_MODEL, DEPTH, 2 * MEM_WIDTH), (1, 0, 2))
    a = st(g["a_w_in"])
    n["a_w_in"] = jnp.concatenate([a[:, :, :_QKV_GATE], a[:, :, _BETA_AT:_BETA_AT + 2 * HEADS], a[:, :, _QKV_GATE:_BETA_AT]], axis=2)
    n["a_conv"] = st(g["a_conv"])[:, :CONV_K]
    n["a_A_log"] = st(g["a_A_log_row"])[:, 0, HEADS:2 * HEADS]
    n["a_dt_bias"] = st(g["a_dt_bias_row"])[:, 0, HEADS:2 * HEADS]
    n["a_out_norm"] = st(g["a_out_norm"]).reshape(N_A, HEAD_DIM)
    n["b_q_norm"] = st(g["b_q_norm"]).reshape(N_B, Q_LORA)
    uq = st(g["b_w_uq"])
    nope = uq[:, :, :A_WIDTH].reshape(N_B, Q_LORA, HEADS, HEAD_DIM)
    rope = uq[:, :, A_WIDTH:].reshape(N_B, Q_LORA, HEADS, LANE)[..., :QK_ROPE]
    n["b_w_uq"] = jnp.concatenate([nope, rope], axis=-1).reshape(N_B, Q_LORA, HEADS * _NOPE_ROPE)
    n["kv_in_norm"] = g["kv_in_norm"].reshape(D_MODEL)
    n["w_dkv"] = g["w_dkv"][:, :KV_LORA + QK_ROPE]
    n["kv_lat_norm"] = g["kv_lat_norm"].reshape(KV_LORA)
    ukv = g["w_ukv"]
    n["w_ukv"] = jnp.concatenate([ukv[:, :A_WIDTH].reshape(KV_LORA, HEADS, HEAD_DIM), ukv[:, A_WIDTH:].reshape(KV_LORA, HEADS, HEAD_DIM)],
                                 axis=-1).reshape(KV_LORA, HEADS * 2 * HEAD_DIM)
    n["final_norm"] = g["final_norm"].reshape(D_MODEL)
    return n


def _rope_tables(positions):
    inv = ROPE_THETA ** (-jnp.arange(0, QK_ROPE, 2, dtype=_F32) / QK_ROPE)
    ang = positions.astype(_F32)[:, None] * inv
    z = jnp.zeros((positions.shape[0], LANE - QK_ROPE), _F32)
    cos, sin = jnp.cos(ang), jnp.sin(ang)
    return jnp.concatenate([cos, cos, z], axis=1), jnp.concatenate([sin, sin, z], axis=1)


_HBM = pl.BlockSpec(memory_space=pltpu.HBM)


def _place():
    x, y, c = lax.axis_index("x"), lax.axis_index("y"), lax.axis_index("c")
    return x, y, c, [(1 - x, y), (x, 1 - y), (1 - x, 1 - y)]


def _remote(src, dst, send_sem, recv_sem, to):
    return pltpu.make_async_remote_copy(src_ref=src, dst_ref=dst, send_sem=send_sem, recv_sem=recv_sem, device_id=to, device_id_type=_MESH)


def _gather_over_chips(shard, *, name):
    rows, cols = shard.shape
    half = rows // 2

    def body(w_ref, out_ref, send_sems, recv_sems):
        x, y, c, chips = _place()
        k = 2 * x + y

        def part(chip, h):
            return out_ref.at[chip, pl.ds(h * half, half), :]

        first = [_remote(w_ref.at[pl.ds(c * half, half), :], part(k, c), send_sems.at[j], recv_sems.at[j], (px, py, c))
                 for j, (px, py) in enumerate(chips)]
        for cp in first:
            cp.start()
        passed = []
        for j, (px, py) in enumerate(chips):
            got = part(2 * px + py, c)
            _remote(got, got, send_sems.at[j], recv_sems.at[j], (px, py, c)).wait_recv()
            fw = _remote(got, got, send_sems.at[3 + j], recv_sems.at[3 + j], (x, y, 1 - c))
            fw.start()
            passed.append(fw)
        for j, (px, py) in enumerate(chips):
            got = part(2 * px + py, 1 - c)
            _remote(got, got, send_sems.at[3 + j], recv_sems.at[3 + j], (x, y, 1 - c)).wait_recv()
        for cp in first + passed:
            cp.wait_send()

    others = _pcall(
        body, name=name, in_specs=[_HBM], out_specs=_HBM, out_shape=jax.ShapeDtypeStruct((N_CHIPS, rows, cols), shard.dtype),
        scratch_shapes=[pltpu.SemaphoreType.DMA((6,)), pltpu.SemaphoreType.DMA((6,))],
    )(shard)
    return lax.dynamic_update_slice(others, shard[None], (2 * lax.axis_index("x") + lax.axis_index("y"), 0, 0))


PAIR_COPIES = 4


def _scatter_over_chips(v, *, name):
    def body(v_ref, out_ref, send_sems, recv_sems):
        x, y, c, chips = _place()
        cps = [_remote(v_ref.at[2 * px + py], out_ref.at[j], send_sems.at[j], recv_sems.at[j], (px, py, c)) for j, (px, py) in enumerate(chips)]
        for cp in cps:
            cp.start()
        for cp in cps:
            cp.wait()

    return _pcall(body, name=name, in_specs=[_HBM], out_specs=_HBM, out_shape=jax.ShapeDtypeStruct((N_CHIPS - 1,) + v.shape[1:], v.dtype),
                  scratch_shapes=[pltpu.SemaphoreType.DMA((3,)), pltpu.SemaphoreType.DMA((3,))])(v)


def _all_reduce_small(v, *, name):
    def body(v_ref, out_ref, all_ref, send_sems, recv_sems):
        x, y, c, _ = _place()
        me = 4 * x + 2 * y + c
        all_ref[me] = v_ref[...]
        cps = []
        for f in range(1, N_DEV):
            fx, fy, fc = (f >> 2) & 1, (f >> 1) & 1, f & 1
            to = (x + fx - 2 * x * fx, y + fy - 2 * y * fy, c + fc - 2 * c * fc)
            cps.append(_remote(v_ref, all_ref.at[me], send_sems.at[f - 1], recv_sems.at[f - 1], to))
        for cp in cps:
            cp.start()
        for cp in cps:
            cp.wait()
        acc = all_ref[0]
        for d in range(1, N_DEV):
            acc = acc + all_ref[d]
        out_ref[...] = acc

    vm = pl.BlockSpec(memory_space=pltpu.VMEM)
    return _pcall(body, name=name, in_specs=[vm], out_specs=vm, out_shape=jax.ShapeDtypeStruct(v.shape, v.dtype),
                  scratch_shapes=[pltpu.VMEM((N_DEV,) + v.shape, v.dtype), pltpu.SemaphoreType.DMA((N_DEV - 1,)), pltpu.SemaphoreType.DMA((N_DEV - 1,))])(v)


_FFN_GU = ("ffn1_w_gu", "ffn2_w_gu")
_FFN_DOWN = ("ffn1_w_down", "ffn2_w_down")
_BIG = (("w_out", 1), ("w_mem_kv", 1), ("a_w_in", 2), ("a_conv", 2), ("b_w_in", 1), ("b_w_uq", 2), ("w_dkv", 0),
        ("w_ukv", 1))
_SMALL = ("ffn1_norm", "mix_norm", "ffn2_norm", "mem_norm", "a_A_log", "a_dt_bias", "a_out_norm", "b_q_norm", "kv_in_norm", "kv_lat_norm",
          "final_norm")
_WEIGHTS = ("ffn1_norm", "ffn1_w_gu", "ffn1_w_down", "mix_norm", "ffn2_norm", "ffn2_w_gu", "ffn2_w_down", "w_out", "mem_norm", "w_mem_kv",
            "a_w_in", "a_conv", "a_A_log", "a_dt_bias", "a_out_norm", "b_w_in", "b_q_norm", "b_w_uq", "kv_in_norm", "w_dkv", "kv_lat_norm",
            "w_ukv", "final_norm")


PACK_PIECE_ROWS = 16


def _piece_rows(shape):
    return -(-math.prod(shape) // (PACK_COLS * PACK_PIECE_ROWS)) * PACK_PIECE_ROWS


def _packed_rows(shapes):
    return sum(_piece_rows(s) for s in shapes)


def _pack(arrs, rows):
    pieces = []
    for a in arrs:
        n, r = a.size, _piece_rows(a.shape)
        flat = a.reshape(-1)
        if r * PACK_COLS != n:
            flat = jnp.concatenate([flat, jnp.zeros((r * PACK_COLS - n,), a.dtype)])
        pieces.append(flat.reshape(r, PACK_COLS))
    used = sum(p.shape[0] for p in pieces)
    if rows > used:
        pieces.append(jnp.zeros((rows - used, PACK_COLS), arrs[0].dtype))
    return jnp.concatenate(pieces, axis=0)


def _unpack(packed, shapes):
    off, out = 0, []
    for shp in shapes:
        n, r = math.prod(shp), _piece_rows(shp)
        piece = packed[off:off + r]
        out.append((piece if r * PACK_COLS == n else piece.reshape(-1)[:n]).reshape(shp))
        off += r
    return out


def _adamw_math(w_, g_, m_, v_):
    m2 = ADAM_B1 * m_ + (1.0 - ADAM_B1) * g_
    v2 = ADAM_B2 * v_ + (1.0 - ADAM_B2) * (g_ * g_)
    m_hat = m2 / (1.0 - ADAM_B1 ** ADAM_STEP)
    v_hat = v2 / (1.0 - ADAM_B2 ** ADAM_STEP)
    return -ADAM_LR * (m_hat / (jnp.sqrt(v_hat) + ADAM_EPS) + ADAM_WD * w_), m2, v2


def _adamw(w, g, m, v, *, name):
    return _rowmap(_adamw_math, [_t(w), _t(g), _t(m), _t(v)], [], [(None, w.shape[1], _F32)] * 3, tm=min(PACK_ROW_TILE, w.shape[0]), name=name)


def _pair_exchange_half(g, *, name):
    n, _, h, cols = g.shape

    def body(g_ref, out_ref, send_sems, recv_sems):
        x, y, c, _ = _place()
        cps = [_remote(g_ref.at[k, 1 - c], out_ref.at[k], send_sems.at[k], recv_sems.at[k], (x, y, 1 - c)) for k in range(n)]
        for cp in cps:
            cp.start()
        for cp in cps:
            cp.wait()

    return _pcall(body, name=name, in_specs=[_HBM], out_specs=_HBM, out_shape=jax.ShapeDtypeStruct((n, h, cols), g.dtype),
                  scratch_shapes=[pltpu.SemaphoreType.DMA((n,)), pltpu.SemaphoreType.DMA((n,))])(g)


def _add_half(g, other, c, *, name):
    n, _, h, cols = g.shape
    tm = min(PACK_ROW_TILE, h)

    def body(c_ref, g_ref, o_ref, sum_ref, narrow_ref):
        acc = g_ref[...] + o_ref[...]
        sum_ref[...] = acc
        narrow_ref[...] = acc.astype(narrow_ref.dtype)

    slab = pl.BlockSpec((None, tm, cols), lambda k, i, c_ref: (k, i, 0))
    return _pcall(
        body, name=name,
        grid_spec=pltpu.PrefetchScalarGridSpec(
            num_scalar_prefetch=1, grid=(n, h // tm),
            in_specs=[pl.BlockSpec((None, None, tm, cols), lambda k, i, c_ref: (k, c_ref[0], i, 0)), slab], out_specs=[slab, slab]),
        out_shape=[jax.ShapeDtypeStruct((n, h, cols), _F32), jax.ShapeDtypeStruct((n, h, cols), jnp.bfloat16)],
        compiler_params=_cp(2),
    )(jnp.reshape(c, (1,)).astype(jnp.int32), g, other)


def _add_own(chip_sum, from_chips, chip, *, name):
    _, h, cols = chip_sum.shape
    tm = min(PACK_ROW_TILE, h)

    def body(k_ref, own_ref, a_ref, b_ref, c_ref, o_ref):
        o_ref[...] = ((own_ref[...] + a_ref[...].astype(_F32)) + b_ref[...].astype(_F32)) + c_ref[...].astype(_F32)

    sent = [pl.BlockSpec((None, tm, cols), functools.partial(lambda i, k_ref, j: (j, i, 0), j=j)) for j in range(N_CHIPS - 1)]
    return _pcall(
        body, name=name,
        grid_spec=pltpu.PrefetchScalarGridSpec(
            num_scalar_prefetch=1, grid=(h // tm,),
            in_specs=[pl.BlockSpec((None, tm, cols), lambda i, k_ref: (k_ref[0], i, 0))] + sent,
            out_specs=pl.BlockSpec((tm, cols), lambda i, k_ref: (i, 0))),
        out_shape=jax.ShapeDtypeStruct((h, cols), _F32), compiler_params=_cp(1),
    )(jnp.reshape(chip, (1,)).astype(jnp.int32), chip_sum, from_chips, from_chips, from_chips)


def _pair_gather(mine, *, name):
    h, cols = mine.shape
    per = h // PAIR_COPIES
    assert per * PAIR_COPIES == h and per % SUBLANE == 0, mine.shape

    def body(v_ref, out_ref, send_sems, recv_sems):
        x, y, c, _ = _place()
        cps = [_remote(v_ref.at[pl.ds(q * per, per), :], out_ref.at[c, pl.ds(q * per, per), :], send_sems.at[q], recv_sems.at[q], (x, y, 1 - c))
               for q in range(PAIR_COPIES)]
        for cp in cps:
            cp.start()
        for q, cp in enumerate(cps):
            cp.wait_send()
            _remote(v_ref.at[pl.ds(q * per, per), :], out_ref.at[1 - c, pl.ds(q * per, per), :], send_sems.at[q], recv_sems.at[q],
                    (x, y, 1 - c)).wait_recv()

    both = _pcall(body, name=name, in_specs=[_HBM], out_specs=_HBM, out_shape=jax.ShapeDtypeStruct((2, h, cols), mine.dtype),
                  scratch_shapes=[pltpu.SemaphoreType.DMA((PAIR_COPIES,)), pltpu.SemaphoreType.DMA((PAIR_COPIES,))])(mine)
    return lax.dynamic_update_slice(both, mine[None], (lax.axis_index("c"), 0, 0))


def _reduce_over_devices(g, c, chip, *, name):
    n, rows, cols = g.shape
    g = g.reshape(n, 2, rows // 2, cols)
    chip_sum, narrow = _add_half(g, _pair_exchange_half(g, name=name + "_pair_sum"), c, name=name + "_add_pair")
    mine = _add_own(chip_sum, _scatter_over_chips(narrow, name=name + "_scatter"), chip, name=name + "_add_chips")
    return _pair_gather(mine, name=name + "_pair_gather").reshape(rows, cols)


def _adamw_at(w, m, v, g_all, first_row, *, name):
    tm = min(PACK_ROW_TILE, w.shape[0])
    assert w.shape[0] % tm == 0 and first_row % tm == 0, (name, w.shape, first_row)

    def fn(w_, m_, v_, g_):
        return _adamw_math(w_, g_, m_, v_) + (g_,)

    return _rowmap(fn, [_t(w), _t(m), _t(v), _t(g_all, first=first_row // tm)], [], [(None, w.shape[1], _F32)] * 4, tm=tm, name=name)


def _step(x, mem, positions, loss_target, w, m, v):
    cx, cy, cc = lax.axis_index("x"), lax.axis_index("y"), lax.axis_index("c")
    chip = 2 * cx + cy
    big = [n for n, _ in _BIG]
    shard_shapes = [w[n].shape for n in big]
    rows = -(-_packed_rows(shard_shapes) // (2 * PACK_ROW_TILE)) * 2 * PACK_ROW_TILE
    flat = lambda a: a.reshape(-1, a.shape[-1])

    ffn_w_gu = _gather_over_chips(jnp.concatenate([flat(w[n]) for n in _FFN_GU]).astype(_BF), name="gather_w_gu")
    ffn_w_down = _gather_over_chips(jnp.concatenate([flat(w[n]) for n in _FFN_DOWN]).astype(_BF), name="gather_w_down")
    w_pack = _pack([w[n] for n in big], rows)
    gathered = _gather_over_chips(w_pack.astype(_BF), name="gather_weights")
    pieces = [_unpack(gathered[k], shard_shapes) for k in range(N_CHIPS)]
    full = {n: jnp.concatenate([pieces[k][i] for k in range(N_CHIPS)], axis=ax) for i, (n, ax) in enumerate(_BIG)}
    for n in _SMALL:
        full[n] = w[n]
    conv = w["a_conv"]
    slots = jnp.stack([jnp.where((chip == k) & (cc == 0), conv, 0.0) for k in range(N_CHIPS)])
    conv_all = _unpack(_all_reduce_small(_pack([slots], _piece_rows(slots.shape)), name="gather_conv"), [slots.shape])[0]
    conv_full = jnp.concatenate([conv_all[k] for k in range(N_CHIPS)], axis=2)

    p = _compute_form(full, conv_full, ffn_w_gu, ffn_w_down)
    cos_t, sin_t = _rope_tables(positions[0])
    loss_tile, d_x, g = _local_step(x[0], mem[0], cos_t, sin_t, loss_target[0], p)
    gn = _natural_grads(g)

    def shard_of(a, ax, k):
        size = a.shape[ax] // N_CHIPS
        return lax.slice_in_dim(a, k * size, (k + 1) * size, axis=ax)

    grads, deltas, new_m, new_v = {}, {}, {}, {}
    for names, key in ((_FFN_GU, "ffn_w_gu"), (_FFN_DOWN, "ffn_w_down")):
        buf = g[key]
        reduced = _reduce_over_devices(buf.reshape(N_CHIPS, -1, buf.shape[-1]), cc, chip, name="grad_" + key)
        first = 0
        for n in names:
            d_, m_, v_, g_ = _adamw_at(flat(w[n]), flat(m[n]), flat(v[n]), reduced, first, name="adamw_" + n)
            grads[n], deltas[n], new_m[n], new_v[n] = (t.reshape(w[n].shape) for t in (g_, d_, m_, v_))
            first += flat(w[n]).shape[0]
    g_pack = jnp.stack([_pack([shard_of(gn[n], ax, k) for n, ax in _BIG], rows) for k in range(N_CHIPS)])
    g_big = _reduce_over_devices(g_pack, cc, chip, name="grad_misc")
    d_big, m_big, v_big = _adamw(w_pack, g_big, _pack([m[n] for n in big], rows), _pack([v[n] for n in big], rows), name="adamw_misc")

    small_shapes = [w[n].shape for n in _SMALL]
    small_rows = _packed_rows(small_shapes)
    g_small = _all_reduce_small(_pack([gn[n] for n in _SMALL], small_rows), name="grad_small")
    d_small, m_small, v_small = _adamw(_pack([w[n] for n in _SMALL], small_rows), g_small, _pack([m[n] for n in _SMALL], small_rows),
                                       _pack([v[n] for n in _SMALL], small_rows), name="adamw_small")

    for out, big_pack, small_pack in ((grads, g_big, g_small), (deltas, d_big, d_small), (new_m, m_big, m_small), (new_v, v_big, v_small)):
        out.update(zip(big, _unpack(big_pack, shard_shapes)))
        out.update(zip(_SMALL, _unpack(small_pack, small_shapes)))
    loss = lax.psum(loss_tile[0, 0], ("x", "y", "c"))
    return (loss, d_x[None], *[grads[n] for n in _WEIGHTS], *[deltas[n] for n in _WEIGHTS], *[new_m[n] for n in _WEIGHTS],
            *[new_v[n] for n in _WEIGHTS])


def kernel(x, mem, positions, ffn1_norm, ffn1_w_gu, ffn1_w_down, mix_norm, ffn2_norm, ffn2_w_gu, ffn2_w_down, w_out, mem_norm, w_mem_kv, a_w_in, a_conv, a_A_log, a_dt_bias, a_out_norm, b_w_in, b_q_norm, b_w_uq, kv_in_norm, w_dkv, kv_lat_norm, w_ukv, final_norm, loss_target, m_ffn1_norm, m_ffn1_w_gu, m_ffn1_w_down, m_mix_norm, m_ffn2_norm, m_ffn2_w_gu, m_ffn2_w_down, m_w_out, m_mem_norm, m_w_mem_kv, m_a_w_in, m_a_conv, m_a_A_log, m_a_dt_bias, m_a_out_norm, m_b_w_in, m_b_q_norm, m_b_w_uq, m_kv_in_norm, m_w_dkv, m_kv_lat_norm, m_w_ukv, m_final_norm, v_ffn1_norm, v_ffn1_w_gu, v_ffn1_w_down, v_mix_norm, v_ffn2_norm, v_ffn2_w_gu, v_ffn2_w_down, v_w_out, v_mem_norm, v_w_mem_kv, v_a_w_in, v_a_conv, v_a_A_log, v_a_dt_bias, v_a_out_norm, v_b_w_in, v_b_q_norm, v_b_w_uq, v_kv_in_norm, v_w_dkv, v_kv_lat_norm, v_w_ukv, v_final_norm):
    given = dict(locals())
    w = {n: given[n] for n in _WEIGHTS}
    m = {n: given["m_" + n] for n in _WEIGHTS}
    v = {n: given["v_" + n] for n in _WEIGHTS}
    return _step(x, mem, positions, loss_target, w, m, v)
```

```python
import functools
import math

import jax
import jax.numpy as jnp
from jax import lax
from jax.experimental import pallas as pl
from jax.experimental.pallas import tpu as pltpu

_BF = jnp.bfloat16
_F32 = jnp.float32
_HI = lax.Precision.HIGHEST
_MESH = pl.DeviceIdType.MESH

D_MODEL = 1024
DEPTH = 4
N_A = 2
N_B = 2
CHUNK = 64
EPS = 1e-6
HEADS = 6
HEAD_DIM = 128
A_WIDTH = HEADS * HEAD_DIM
CONV_K = 4
QK_ROPE = 64
Q_LORA = 256
KV_LORA = 256
N_MEM = 256
MEM_HEADS = 4
MEM_HEAD_DIM = 64
MEM_WIDTH = MEM_HEADS * MEM_HEAD_DIM
D_FF = 2816
ROPE_THETA = 10000.0
A_IN = 4 * A_WIDTH + 2 * HEADS + MEM_WIDTH
A_IN_PAD = 3456
UQ_PAD = 2 * A_WIDTH
DKV_PAD = KV_LORA + 128
LANE = 128
SUBLANE = 8

ADAM_LR = 0.001
ADAM_B1 = 0.9
ADAM_B2 = 0.999
ADAM_EPS = 1e-08
ADAM_WD = 0.01
ADAM_STEP = 10

N_CHIPS = 4
N_DEV = 8
PACK_COLS = 1024
PACK_ROW_TILE = 256


def _pcall(body, **kw):
    return pl.pallas_call(body, **kw)


VMEM_LIMIT_V7X = 48 * 2 ** 20
TILE_BYTES = 6 * 2 ** 20


def _cp(grid_rank):
    return pltpu.CompilerParams(dimension_semantics=("arbitrary",) * grid_rank, vmem_limit_bytes=VMEM_LIMIT_V7X)


def _fit_rows(rows, row_bytes):
    while rows > LANE and rows * row_bytes > TILE_BYTES:
        rows //= 2
    return rows


def _fit_cols(n, target, col_bytes):
    return _tile(n, max(LANE, min(target, TILE_BYTES // col_bytes)))


def _tile(n, target):
    best = None
    for t in range(LANE, min(n, target) + 1, LANE):
        if n % t == 0:
            best = t
    return best if best is not None else n


def _dot(a, b):
    return jnp.dot(a.astype(_BF), b.astype(_BF), preferred_element_type=_F32)


def _dot_nt(a, b):
    return lax.dot_general(a.astype(_BF), b.astype(_BF), (((1,), (1,)), ((), ())), preferred_element_type=_F32)


def _dot_tn(a, b):
    return lax.dot_general(a.astype(_BF), b.astype(_BF), (((0,), (0,)), ((), ())), preferred_element_type=_F32)


def _dot_hi(a, b):
    return jnp.dot(a, b, precision=_HI, preferred_element_type=_F32)


def _rowmap(fn, tiles, params, outs, accs=(), *, tm, name, carry=(), reverse=False):
    rows = tiles[0][0].shape[0]
    steps = rows // tm
    nt, npar, no, na, nc = len(tiles), len(params), len(outs), len(accs), len(carry)

    def step_index(i):
        return steps - 1 - i if reverse else i

    in_specs, operands = [], []
    for arr, r, w, cb, first in tiles:
        r = tm if r is None else r
        w = arr.shape[1] if w is None else w
        assert arr.shape[0] >= (first + steps) * r and (w % LANE == 0 or w == arr.shape[1]), (name, arr.shape, r, w)
        in_specs.append(pl.BlockSpec((r, w), functools.partial(lambda i, cb, first: (first + step_index(i), cb), cb=cb, first=first)))
        operands.append(arr)
    for p in params:
        in_specs.append(pl.BlockSpec(p.shape, functools.partial(lambda i, nd: (0,) * nd, nd=p.ndim)))
        operands.append(p)
    out_specs, out_shape = [], []
    for r, cols, dt in outs:
        r = tm if r is None else r
        out_specs.append(pl.BlockSpec((r, cols), lambda i: (step_index(i), 0)))
        out_shape.append(jax.ShapeDtypeStruct((steps * r, cols), dt))
    for shp, dt in accs:
        out_specs.append(pl.BlockSpec(shp, functools.partial(lambda i, nd: (0,) * nd, nd=len(shp))))
        out_shape.append(jax.ShapeDtypeStruct(shp, dt))

    def body(*refs):
        t_refs = refs[:nt]
        p_refs = refs[nt:nt + npar]
        o_refs = refs[nt + npar:nt + npar + no]
        a_refs = refs[nt + npar + no:nt + npar + no + na]
        c_refs = refs[nt + npar + no + na:]
        if na or nc:
            @pl.when(pl.program_id(0) == 0)
            def _():
                for r in a_refs + c_refs:
                    r[...] = jnp.zeros(r.shape, r.dtype)
        vals = fn(*[r[...] for r in t_refs], *[r[...] for r in p_refs], *[r[...] for r in c_refs])
        vals = tuple(vals) if isinstance(vals, (tuple, list)) else (vals,)
        assert len(vals) == no + na + nc, (name, len(vals), no, na, nc)
        for r, v in zip(o_refs, vals[:no]):
            r[...] = v.astype(r.dtype)
        for r, v in zip(a_refs, vals[no:no + na]):
            r[...] += v.astype(r.dtype)
        for r, v in zip(c_refs, vals[no + na:]):
            r[...] = v.astype(r.dtype)

    res = _pcall(
        body, name=name, grid=(steps,), in_specs=in_specs, out_specs=out_specs, out_shape=out_shape,
        scratch_shapes=[pltpu.VMEM(shp, dt) for shp, dt in carry],
        compiler_params=_cp(1),
    )(*operands)
    return res


def _t(arr, width=None, cb=0, rows=None, first=0):
    return (arr, rows, width, cb, first)


def _mm_nn(a, b, *, out_dtype, name, scale=None, res=None, tm=1024, tn=1536):
    m, k = a.shape
    n = b.shape[1]
    tm, tn = _fit_rows(min(tm, m), k * a.dtype.itemsize), _fit_cols(n, tn, k * b.dtype.itemsize)

    def body(a_ref, b_ref, *rest):
        acc = _dot(a_ref[...], b_ref[...])
        if scale is not None:
            acc = acc * scale
        if res is not None:
            acc = acc + rest[0][...]
        rest[-1][...] = acc.astype(rest[-1].dtype)

    in_specs = [pl.BlockSpec((tm, k), lambda i, j: (i, 0)), pl.BlockSpec((k, tn), lambda i, j: (0, j))]
    operands = [a, b]
    if res is not None:
        in_specs.append(pl.BlockSpec((tm, tn), lambda i, j: (i, j)))
        operands.append(res)
    return _pcall(
        body, name=name, grid=(m // tm, n // tn), in_specs=in_specs,
        out_specs=pl.BlockSpec((tm, tn), lambda i, j: (i, j)), out_shape=jax.ShapeDtypeStruct((m, n), out_dtype),
        compiler_params=_cp(2),
    )(*operands)


def _mm_nt(a, b, *, out_dtype, name, scale=None, tm=1024, tn=1536):
    m, k = a.shape
    n = b.shape[0]
    tm, tn = _fit_rows(min(tm, m), k * a.dtype.itemsize), _fit_cols(n, tn, k * b.dtype.itemsize)

    def body(a_ref, b_ref, o_ref):
        acc = _dot_nt(a_ref[...], b_ref[...])
        if scale is not None:
            acc = acc * scale
        o_ref[...] = acc.astype(o_ref.dtype)

    return _pcall(
        body, name=name, grid=(m // tm, n // tn),
        in_specs=[pl.BlockSpec((tm, k), lambda i, j: (i, 0)), pl.BlockSpec((tn, k), lambda i, j: (j, 0))],
        out_specs=pl.BlockSpec((tm, tn), lambda i, j: (i, j)), out_shape=jax.ShapeDtypeStruct((m, n), out_dtype),
        compiler_params=_cp(2),
    )(a, b)


def _mm_tn(a, b, *, name, scale=None, t1=1024, tn=1536, ts=1024):
    s, k1 = a.shape
    n = b.shape[1]
    t1, tn, ts = _tile(k1, t1), _tile(n, tn), min(ts, s)
    steps = s // ts

    def body(a_ref, b_ref, o_ref):
        @pl.when(pl.program_id(2) == 0)
        def _():
            o_ref[...] = jnp.zeros(o_ref.shape, o_ref.dtype)

        o_ref[...] += _dot_tn(a_ref[...], b_ref[...])
        if scale is not None:
            @pl.when(pl.program_id(2) == steps - 1)
            def _():
                o_ref[...] = o_ref[...] * scale

    return _pcall(
        body, name=name, grid=(k1 // t1, n // tn, steps),
        in_specs=[pl.BlockSpec((ts, t1), lambda i, j, r: (r, i)), pl.BlockSpec((ts, tn), lambda i, j, r: (r, j))],
        out_specs=pl.BlockSpec((t1, tn), lambda i, j, r: (i, j)), out_shape=jax.ShapeDtypeStruct((k1, n), _F32),
        compiler_params=_cp(3),
    )(a, b)


def _heads(t, n, w=LANE):
    return [t[:, w * h:w * (h + 1)] for h in range(n)]


def _cat(parts):
    return jnp.concatenate(parts, axis=1)


def _rms(x, g):
    return x * lax.rsqrt(jnp.mean(x * x, axis=-1, keepdims=True) + EPS) * g


def _l2n(x):
    return x * lax.rsqrt(jnp.sum(x * x, axis=-1, keepdims=True) + EPS)


def _sigmoid(x):
    return 0.5 * (jnp.tanh(0.5 * x) + 1.0)


def _silu(x):
    return x * _sigmoid(x)


def _softplus(x):
    return jnp.maximum(x, 0.0) + jnp.log(1.0 + jnp.exp(-jnp.abs(x)))


def _lane_pick(t, h):
    lane = lax.broadcasted_iota(jnp.int32, t.shape, 1)
    return jnp.sum(jnp.where(lane == h, t, 0.0), axis=1, keepdims=True)


def _lane_put(col, h, width=LANE):
    lane = lax.broadcasted_iota(jnp.int32, (col.shape[0], width), 1)
    return jnp.where(lane == h, col, 0.0)


def _vjp(fwd, ins, cts):
    outs, pull = jax.vjp(fwd, *ins)
    outs = outs if isinstance(outs, (tuple, list)) else (outs,)
    cts = tuple(c.astype(o.dtype) for c, o in zip(cts, outs))
    return pull(cts if len(cts) > 1 else cts[0])


def _rot_half_matrix():
    r = lax.broadcasted_iota(jnp.int32, (LANE, LANE), 0)
    c = lax.broadcasted_iota(jnp.int32, (LANE, LANE), 1)
    half = QK_ROPE // 2
    return jnp.where((c < half) & (r == c + half), -1.0, jnp.where((c >= half) & (c < QK_ROPE) & (r == c - half), 1.0, 0.0))


def _rope(x, cos_t, sin_t):
    return x * cos_t + _dot_hi(x, _rot_half_matrix()) * sin_t


def _mem_attn(q, km, vm):
    lane_q = lax.broadcasted_iota(jnp.int32, q.shape, 1)
    lane_v = lax.broadcasted_iota(jnp.int32, vm.shape, 1)
    out = jnp.zeros(q.shape, _F32)
    for h in range(MEM_HEADS):
        lo, hi = MEM_HEAD_DIM * h, MEM_HEAD_DIM * (h + 1)
        qh = jnp.where((lane_q >= lo) & (lane_q < hi), q, 0.0)
        vh = jnp.where((lane_v >= lo) & (lane_v < hi), vm, 0.0)
        sc = _dot_nt(qh, km) * MEM_HEAD_DIM ** -0.5
        sc = sc - lax.stop_gradient(jnp.max(sc, axis=-1, keepdims=True))
        p = jnp.exp(sc)
        p = p / jnp.sum(p, axis=-1, keepdims=True)
        out = out + _dot(p, vh)
    return out


_PAIR = 2 * CHUNK


def _pair_masks():
    ri = lax.broadcasted_iota(jnp.int32, (_PAIR, _PAIR), 0)
    ci = lax.broadcasted_iota(jnp.int32, (_PAIR, _PAIR), 1)
    same = (ri >= CHUNK) == (ci >= CHUNK)
    return same, same & (ri >= ci), same & (ri > ci), ri == ci, same & (ri <= ci)


_NN = (((2,), (1,)), ((0,), (0,)))
_NT = (((2,), (2,)), ((0,), (0,)))
_TN = (((1,), (1,)), ((0,), (0,)))


def _bdot(a, b, dims):
    return lax.dot_general(a.astype(_BF), b.astype(_BF), dims, preferred_element_type=_F32)


def _dot3(a, b, dims):
    a_hi, b_hi = a.astype(_BF), b.astype(_BF)
    a_lo, b_lo = (a - a_hi.astype(_F32)).astype(_BF), (b - b_hi.astype(_F32)).astype(_BF)
    d = lambda x, y: lax.dot_general(x, y, dims, preferred_element_type=_F32)
    return d(a_hi, b_hi) + (d(a_hi, b_lo) + d(a_lo, b_hi))


@jax.custom_vjp
def _mm3(a, b):
    return _dot3(a, b, _NN)


_mm3.defvjp(lambda a, b: (_dot3(a, b, _NN), (a, b)), lambda res, g: (_dot3(g, res[1], _NT), _dot3(res[0], g, _TN)))


def _neumann_inverse(a):
    eye = jnp.where(_pair_masks()[3], 1.0, 0.0)
    n = -a
    t_inv = eye + n
    for _ in range(5):
        n = _dot3(n, n, _NN)
        t_inv = t_inv + _dot3(t_inv, n, _NN)
    return t_inv


@jax.custom_vjp
def _unit_lower_inverse(a):
    return _neumann_inverse(a)


def _unit_lower_inverse_fwd(a):
    t_inv = _neumann_inverse(a)
    return t_inv, t_inv


def _unit_lower_inverse_bwd(t_inv, g):
    return (-_dot3(t_inv, _dot3(g, t_inv, _NT), _TN),)


_unit_lower_inverse.defvjp(_unit_lower_inverse_fwd, _unit_lower_inverse_bwd)


def _gdn_intra_head(q, k, v, beta, gl):
    same, causal, strict, eye, upper = _pair_masks()
    gl_row = jnp.sum(jnp.where(eye, gl, 0.0), axis=-2, keepdims=True)
    g_col = jnp.sum(jnp.where(causal, gl_row, 0.0), axis=-1, keepdims=True)
    g_row = jnp.sum(jnp.where(upper, gl, 0.0), axis=-2, keepdims=True)
    g_last = jnp.sum(jnp.where(same, gl_row, 0.0), axis=-1, keepdims=True)
    decay = jnp.where(causal, jnp.exp(jnp.where(causal, g_col - g_row, 0.0)), 0.0)
    kb = k * beta
    a = jnp.where(strict, _bdot(kb, k, _NT) * decay, 0.0)
    t_inv = _unit_lower_inverse(a)
    e_g = jnp.exp(g_col)
    u = _mm3(t_inv, v * beta)
    w = _mm3(t_inv, kb * e_g)
    qk = _bdot(q, k, _NT) * decay
    return w, u, q * e_g, k * jnp.exp(g_last - g_col), qk, jnp.exp(g_last)


def _gdn_scan_head(s, qd_a, kd_a, w_a, u_a, qk_a, dc_a, qd_b, kd_b, w_b, u_b, qk_b, dc_b):
    zeros = jnp.zeros((HEADS, CHUNK, HEAD_DIM), _F32)
    vn_a = u_a - _bdot(w_a, s, _NN)
    o_a = _bdot(qd_a, s, _NN) + _bdot(qk_a, jnp.concatenate([vn_a, zeros], axis=1), _NN)
    s1 = s * dc_a + _bdot(kd_a, vn_a, _TN)
    vn_b = u_b - _bdot(w_b, s1, _NN)
    o_b = _bdot(qd_b, s1, _NN) + _bdot(qk_b, jnp.concatenate([zeros, vn_b], axis=1), _NN)
    s2 = s1 * dc_b + _bdot(kd_b, vn_b, _TN)
    return o_a, o_b, s2


def _pick_scalar(t, row, lane_i):
    ri = lax.broadcasted_iota(jnp.int32, t.shape, 0)
    ci = lax.broadcasted_iota(jnp.int32, t.shape, 1)
    return jnp.sum(jnp.sum(jnp.where((ri == row) & (ci == lane_i), t, 0.0), axis=1, keepdims=True), axis=0, keepdims=True)


def _put_scalar(val, row, lane_i, shape):
    ri = lax.broadcasted_iota(jnp.int32, shape, 0)
    ci = lax.broadcasted_iota(jnp.int32, shape, 1)
    return jnp.where((ri == row) & (ci == lane_i), val, 0.0)


def _by_head(t):
    return jnp.stack(_heads(t, HEADS))


def _from_heads(t):
    return _cat([t[h] for h in range(HEADS)])


def _state_by_head(s):
    return jnp.stack([s[HEAD_DIM * h:HEAD_DIM * (h + 1), :] for h in range(HEADS)])


def _scan_ins(qd, kd, w, u, qk, dcrow, state):
    ins = [_state_by_head(state)]
    for r0 in (0, CHUNK):
        rs = slice(r0, r0 + CHUNK)
        ins += [_by_head(t[rs, :]) for t in (qd, kd, w, u, qk)]
        ins.append(jnp.stack([_pick_scalar(dcrow, r0, h) for h in range(HEADS)]))
    return ins


def _gdn_scan_fwd_fn(qd, kd, w, u, qk, dcrow, state):
    o_a, o_b, s2 = _gdn_scan_head(*_scan_ins(qd, kd, w, u, qk, dcrow, state))
    return jnp.concatenate([_from_heads(o_a), _from_heads(o_b)], axis=0), state, s2.reshape(state.shape)


def _gdn_scan_bwd_fn(qd, kd, w, u, qk, dcrow, state, d_o, d_state):
    cts = (_by_head(d_o[0:CHUNK, :]), _by_head(d_o[CHUNK:_PAIR, :]), _state_by_head(d_state))
    g = _vjp(_gdn_scan_head, _scan_ins(qd, kd, w, u, qk, dcrow, state), cts)
    grads = tuple(jnp.concatenate([_from_heads(g[1 + t]), _from_heads(g[7 + t])], axis=0) for t in range(5))
    d_dcrow = sum(_put_scalar(g[6][h], 0, h, dcrow.shape) + _put_scalar(g[12][h], CHUNK, h, dcrow.shape) for h in range(HEADS))
    return grads + (d_dcrow, g[0].reshape(state.shape))


def _gdn_intra_ins(q, k, v, bg):
    return [_by_head(q), _by_head(k), _by_head(v), jnp.stack([_lane_pick(bg, h) for h in range(HEADS)]),
            jnp.stack([_lane_pick(bg, HEADS + h) for h in range(HEADS)])]


def _gdn_intra_fwd_fn(q, k, v, bg):
    res = _gdn_intra_head(*_gdn_intra_ins(q, k, v, bg))
    dcrow = sum(_lane_put(res[5][h], h) for h in range(HEADS))
    return tuple(_from_heads(r) for r in res[:5]) + (dcrow,)


def _gdn_intra_bwd_fn(q, k, v, bg, d_w, d_u, d_qd, d_kd, d_qk, d_dcrow):
    cts = tuple(_by_head(d) for d in (d_w, d_u, d_qd, d_kd, d_qk)) + (jnp.stack([_lane_pick(d_dcrow, h) for h in range(HEADS)]),)
    g = _vjp(_gdn_intra_head, _gdn_intra_ins(q, k, v, bg), cts)
    d_bg = sum(_lane_put(g[3][h], h) + _lane_put(g[4][h], HEADS + h) for h in range(HEADS))
    return tuple(_from_heads(g[t]) for t in range(3)) + (d_bg,)


def _gdn_gates(ba, alog, dtb):
    lane = lax.broadcasted_iota(jnp.int32, ba.shape, 1)
    beta = _sigmoid(ba)
    g = -jnp.exp(alog) * _softplus(ba + dtb)
    return jnp.where(lane < HEADS, beta, jnp.where(lane < 2 * HEADS, g, 0.0))


def _gdn_q_head(c):
    return _l2n(_silu(c)) * HEAD_DIM ** -0.5


def _gdn_k_head(c):
    return _l2n(_silu(c))


def _gdn_prep_fwd_fn(qkv_c, ba, alog, dtb):
    hs = _heads(qkv_c, 3 * HEADS)
    q = _cat([_gdn_q_head(c) for c in hs[:HEADS]])
    k = _cat([_gdn_k_head(c) for c in hs[HEADS:2 * HEADS]])
    v = _cat([_silu(c) for c in hs[2 * HEADS:]])
    return q, k, v, _gdn_gates(ba, alog, dtb)


def _gdn_prep_bwd_fn(qkv_c, ba, d_q, d_k, d_v, d_bg, alog, dtb):
    hs = _heads(qkv_c, 3 * HEADS)
    dqs, dks, dvs = _heads(d_q, HEADS), _heads(d_k, HEADS), _heads(d_v, HEADS)
    parts = [_vjp(_gdn_q_head, [hs[h]], (dqs[h],))[0] for h in range(HEADS)]
    parts += [_vjp(_gdn_k_head, [hs[HEADS + h]], (dks[h],))[0] for h in range(HEADS)]
    parts += [_vjp(_silu, [hs[2 * HEADS + h]], (dvs[h],))[0] for h in range(HEADS)]
    d_ba, d_alog, d_dtb = _vjp(_gdn_gates, [ba, alog, dtb], (d_bg,))
    return _cat(parts), d_ba, d_alog, d_dtb


def _a_out_head(o, gate, gain):
    return _rms(o, gain) * _silu(gate)


def _a_post_fwd_fn(o, gate, qm, gain, mem_kv):
    parts = [_a_out_head(oh, gh, gain) for oh, gh in zip(_heads(o, HEADS), _heads(gate, HEADS))]
    parts.append(_mem_attn(qm, mem_kv[:, :MEM_WIDTH], mem_kv[:, MEM_WIDTH:]))
    return (_cat(parts),)


def _a_post_bwd_fn(o, gate, qm, d_cat, gain, mem_kv):
    d_os, d_gates = [], []
    d_gain = jnp.zeros(gain.shape, _F32)
    dc = _heads(d_cat, HEADS + 2)
    for h, (oh, gh) in enumerate(zip(_heads(o, HEADS), _heads(gate, HEADS))):
        g = _vjp(_a_out_head, [oh, gh, gain], (dc[h],))
        d_os.append(g[0])
        d_gates.append(g[1])
        d_gain = d_gain + g[2]
    d_qm, d_km, d_vm = _vjp(_mem_attn, [qm, mem_kv[:, :MEM_WIDTH], mem_kv[:, MEM_WIDTH:]], (d_cat[:, A_WIDTH:],))
    return _cat(d_os), _cat(d_gates + [d_qm]), d_gain, _cat([d_km, d_vm])


def _b_post_fwd_fn(o, qm, mem_kv):
    return (_cat([o.astype(_F32), _mem_attn(qm, mem_kv[:, :MEM_WIDTH], mem_kv[:, MEM_WIDTH:])]),)


def _b_post_bwd_fn(qm, d_cat_m, mem_kv):
    d_qm, d_km, d_vm = _vjp(_mem_attn, [qm, mem_kv[:, :MEM_WIDTH], mem_kv[:, MEM_WIDTH:]], (d_cat_m,))
    return d_qm, _cat([d_km, d_vm])


ATTN_SCALE = (HEAD_DIM + QK_ROPE) ** -0.5


def _rope_q_fwd_fn(qf, cos_t, sin_t):
    hs = _heads(qf, 2 * HEADS)
    return (_cat(hs[:HEADS] + [_rope(x, cos_t, sin_t) for x in hs[HEADS:]]) * ATTN_SCALE,)


def _rope_q_bwd_fn(d_qn, d_qr, cos_t, sin_t):
    f = lambda x: _rope(x, cos_t, sin_t)
    return (_cat([d_qn] + [_vjp(f, [x], (x,))[0] for x in _heads(d_qr, HEADS)]) * ATTN_SCALE,)


def _kv_prep_fwd_fn(ckr, cos_t, sin_t, gain):
    return _rms(ckr[:, :KV_LORA], gain), _rope(ckr[:, KV_LORA:], cos_t, sin_t)


def _kv_prep_bwd_fn(ckr, d_ckv, d_kr, cos_t, sin_t, gain):
    d_lat, d_gain = _vjp(_rms, [ckr[:, :KV_LORA], gain], (d_ckv,))
    f = lambda x: _rope(x, cos_t, sin_t)
    d_rope = _vjp(f, [ckr[:, KV_LORA:]], (d_kr,))[0]
    return _cat([d_lat, d_rope]), d_gain


def _rms_fwd(x, gain, *, name, tm=1024, out_dtype=_BF):
    tm = min(tm, x.shape[0])
    return _rowmap(lambda x_, g_: (_rms(x_.astype(_F32), g_),), [_t(x)], [gain], [(None, x.shape[1], out_dtype)], tm=tm, name=name)[0]


def _rms_bwd(x, d_xn, d_res, gain, *, name, tm=512):
    tm = min(tm, x.shape[0])

    def fn(x_, dxn_, *rest):
        g_ = rest[-1]
        dx, dg = _vjp(_rms, [x_.astype(_F32), g_], (dxn_.astype(_F32),))
        if d_res is not None:
            dx = dx + rest[0]
        return dx, dg

    tiles = [_t(x), _t(d_xn)] + ([_t(d_res)] if d_res is not None else [])
    return _rowmap(fn, tiles, [gain], [(None, x.shape[1], _F32)], [(gain.shape, _F32)], tm=tm, name=name)


FFN_COL_TILE = 1408
FFN_ROW_TILE = 512
FFN_DXN_ROW_TILE = 512
FFN_DOWN_ROW_TILE = 1024


def _ffn_gate_up(x, gain, w_gu, idx, *, name):
    s = x.shape[0]
    tm, tf = min(FFN_ROW_TILE, s), FFN_COL_TILE
    nf = D_FF // tf

    def body(x_ref, gain_ref, wg_ref, wu_ref, xn_ref, g_ref, u_ref, a_ref):
        @pl.when(pl.program_id(1) == 0)
        def _():
            xn_ref[...] = _rms(x_ref[...], gain_ref[...]).astype(xn_ref.dtype)

        xn = xn_ref[...]
        g, u = _dot(xn, wg_ref[...]), _dot(xn, wu_ref[...])
        g_ref[...] = g.astype(g_ref.dtype)
        u_ref[...] = u.astype(u_ref.dtype)
        a_ref[...] = (_silu(g) * u).astype(a_ref.dtype)

    col = pl.BlockSpec((tm, tf), lambda i, j: (i, j))
    wide = jax.ShapeDtypeStruct((s, D_FF), _BF)
    return _pcall(
        body, name=name, grid=(s // tm, nf),
        in_specs=[pl.BlockSpec((tm, D_MODEL), lambda i, j: (i, 0)), pl.BlockSpec((1, D_MODEL), lambda i, j: (0, 0)),
                  pl.BlockSpec((None, D_MODEL, tf), lambda i, j: (j, idx, 0)), pl.BlockSpec((None, D_MODEL, tf), lambda i, j: (nf + j, idx, 0))],
        out_specs=[pl.BlockSpec((tm, D_MODEL), lambda i, j: (i, 0)), col, col, col],
        out_shape=[jax.ShapeDtypeStruct((s, D_MODEL), _BF), wide, wide, wide],
        compiler_params=_cp(2),
    )(x, gain, w_gu, w_gu)


FFN_SHARD_ROWS = D_FF // N_CHIPS


def _w_down_specs(idx):
    return [pl.BlockSpec((None, FFN_SHARD_ROWS, D_MODEL), functools.partial(lambda i, j, q: (2 * j + q, idx, 0), q=q)) for q in (0, 1)]


def _ffn_down(a, w_down, idx, x, *, name):
    s = a.shape[0]
    tm = min(FFN_DOWN_ROW_TILE, s)

    def body(a_ref, w0_ref, w1_ref, w2_ref, w3_ref, x_ref, o_ref):
        w = jnp.concatenate([w0_ref[...], w1_ref[...], w2_ref[...], w3_ref[...]], axis=0)
        o_ref[...] = x_ref[...] + 0.5 * _dot(a_ref[...], w)

    rows = pl.BlockSpec((tm, D_MODEL), lambda i: (i, 0))
    w_specs = [pl.BlockSpec((None, FFN_SHARD_ROWS, D_MODEL), functools.partial(lambda i, k: (k, idx, 0), k=k)) for k in range(N_CHIPS)]
    return _pcall(
        body, name=name, grid=(s // tm,), in_specs=[pl.BlockSpec((tm, D_FF), lambda i: (i, 0))] + w_specs + [rows],
        out_specs=rows, out_shape=jax.ShapeDtypeStruct((s, D_MODEL), _F32), compiler_params=_cp(1),
    )(a, w_down, w_down, w_down, w_down, x)


def _ffn_fwd(x, gain, w_gu, w_down, idx, *, name):
    xn, g, u, a = _ffn_gate_up(x, gain, w_gu, idx, name=name + "_gu")
    y = _ffn_down(a, w_down, idx, x, name=name + "_down")
    return y, (x, xn, g, u, a)


def _ffn_d_gate_up(d_y, g, u, w_down, idx, *, name):
    s = d_y.shape[0]
    tm, tf = min(FFN_ROW_TILE, s), FFN_COL_TILE

    def body(dy_ref, wa_ref, wb_ref, g_ref, u_ref, dg_ref, du_ref):
        da = _dot_nt(dy_ref[...], jnp.concatenate([wa_ref[...], wb_ref[...]], axis=0)) * 0.5
        gg, uu = g_ref[...].astype(_F32), u_ref[...].astype(_F32)
        sg = _sigmoid(gg)
        dg_ref[...] = (da * uu * sg * (1.0 + gg * (1.0 - sg))).astype(dg_ref.dtype)
        du_ref[...] = (da * gg * sg).astype(du_ref.dtype)

    col = pl.BlockSpec((tm, tf), lambda i, j: (i, j))
    wide = jax.ShapeDtypeStruct((s, D_FF), _BF)
    return _pcall(
        body, name=name, grid=(s // tm, D_FF // tf),
        in_specs=[pl.BlockSpec((tm, D_MODEL), lambda i, j: (i, 0))] + _w_down_specs(idx) + [col, col],
        out_specs=[col, col], out_shape=[wide, wide], compiler_params=_cp(2),
    )(d_y, w_down, w_down, g, u)


def _ffn_d_x(d_g, d_u, w_gu, idx, x, d_y, gain, *, name):
    s = x.shape[0]
    tm, tf = min(FFN_DXN_ROW_TILE, s), FFN_COL_TILE

    def body(dg_ref, du_ref, w0_ref, w1_ref, w2_ref, w3_ref, x_ref, dy_ref, gain_ref, dx_ref, dgain_ref):
        @pl.when(pl.program_id(0) == 0)
        def _():
            dgain_ref[...] = jnp.zeros(dgain_ref.shape, _F32)

        d_xn = (_dot_nt(dg_ref[:, 0:tf], w0_ref[...]) + _dot_nt(dg_ref[:, tf:2 * tf], w1_ref[...])
                + _dot_nt(du_ref[:, 0:tf], w2_ref[...]) + _dot_nt(du_ref[:, tf:2 * tf], w3_ref[...]))
        dx, dgain = _vjp(_rms, [x_ref[...], gain_ref[...]], (d_xn,))
        dx_ref[...] = dx + dy_ref[...]
        dgain_ref[...] += dgain

    wide = pl.BlockSpec((tm, D_FF), lambda i: (i, 0))
    rows = pl.BlockSpec((tm, D_MODEL), lambda i: (i, 0))
    one = pl.BlockSpec((1, D_MODEL), lambda i: (0, 0))
    w_specs = [pl.BlockSpec((None, D_MODEL, tf), functools.partial(lambda i, k: (k, idx, 0), k=k), pipeline_mode=pl.Buffered(1))
               for k in range(N_CHIPS)]
    return _pcall(
        body, name=name, grid=(s // tm,),
        in_specs=[wide, wide] + w_specs + [rows, rows, one],
        out_specs=[rows, one], out_shape=[jax.ShapeDtypeStruct((s, D_MODEL), _F32), jax.ShapeDtypeStruct((1, D_MODEL), _F32)],
        compiler_params=_cp(1),
    )(d_g, d_u, w_gu, w_gu, w_gu, w_gu, x, d_y, gain)


def _ffn_d_w_gu(xn, d_act, into, idx, first_chip, *, name, ts=1024):
    s = xn.shape[0]
    ts = min(ts, s)
    steps = s // ts

    def body(a_ref, b_ref, into_ref, o_ref):
        @pl.when(pl.program_id(1) == 0)
        def _():
            o_ref[...] = jnp.zeros(o_ref.shape, o_ref.dtype)

        o_ref[...] += _dot_tn(a_ref[...], b_ref[...])

    return _pcall(
        body, name=name, grid=(D_FF // FFN_COL_TILE, steps),
        in_specs=[pl.BlockSpec((ts, D_MODEL), lambda j, r: (r, 0)), pl.BlockSpec((ts, FFN_COL_TILE), lambda j, r: (r, j)),
                  pl.BlockSpec(memory_space=pl.ANY)],
        out_specs=pl.BlockSpec((None, None, D_MODEL, FFN_COL_TILE), lambda j, r: (first_chip + j, idx, 0, 0)),
        out_shape=jax.ShapeDtypeStruct(into.shape, into.dtype), input_output_aliases={2: 0}, compiler_params=_cp(2),
    )(xn, d_act, into)


def _ffn_d_w_down(a, d_y, into, idx, *, name, ts=1024):
    s = a.shape[0]
    ts = min(ts, s)
    steps = s // ts

    def body(a_ref, b_ref, into_ref, o_ref):
        @pl.when(pl.program_id(1) == 0)
        def _():
            o_ref[...] = jnp.zeros(o_ref.shape, o_ref.dtype)

        part = _dot_tn(a_ref[...], b_ref[...]) * 0.5
        o_ref[0] += part[0:FFN_SHARD_ROWS, :]
        o_ref[1] += part[FFN_SHARD_ROWS:2 * FFN_SHARD_ROWS, :]

    return _pcall(
        body, name=name, grid=(D_FF // FFN_COL_TILE, steps),
        in_specs=[pl.BlockSpec((ts, FFN_COL_TILE), lambda i, r: (r, i)), pl.BlockSpec((ts, D_MODEL), lambda i, r: (r, 0)),
                  pl.BlockSpec(memory_space=pl.ANY)],
        out_specs=pl.BlockSpec((2, None, FFN_SHARD_ROWS, D_MODEL), lambda i, r: (i, idx, 0, 0)),
        out_shape=jax.ShapeDtypeStruct(into.shape, into.dtype), input_output_aliases={2: 0}, compiler_params=_cp(2),
    )(a, d_y, into)


def _ffn_bwd(d_y, saved, gain, w_gu, w_down, idx, g_gu, g_down, *, name):
    x, xn, g, u, a = saved
    d_g, d_u = _ffn_d_gate_up(d_y, g, u, w_down, idx, name=name + "_dgu")
    g_down = _ffn_d_w_down(a, d_y, g_down, idx, name=name + "_dwd")
    g_gu = _ffn_d_w_gu(xn, d_g, g_gu, idx, 0, name=name + "_dwg")
    g_gu = _ffn_d_w_gu(xn, d_u, g_gu, idx, 2, name=name + "_dwu")
    d_x, d_gain = _ffn_d_x(d_g, d_u, w_gu, idx, x, d_y, gain, name=name + "_dx")
    return d_x, d_gain, g_gu, g_down


def _conv_fwd(h, w, *, name, tm=256):
    s = h.shape[0]
    tm = min(tm, s)
    c = 3 * A_WIDTH
    halo = SUBLANE

    def body(x_ref, prev_ref, w_ref, o_ref, buf):
        i = pl.program_id(0)
        buf[0:halo, :] = jnp.where(i == 0, 0.0, prev_ref[...])
        buf[halo:halo + tm, :] = x_ref[...]
        acc = jnp.zeros((tm, c), _F32)
        for j in range(CONV_K):
            acc = acc + buf[pl.ds(halo - (CONV_K - 1) + j, tm), :] * w_ref[j:j + 1, :]
        o_ref[...] = acc

    return _pcall(
        body, name=name, grid=(s // tm,),
        in_specs=[pl.BlockSpec((tm, c), lambda i: (i, 0)),
                  pl.BlockSpec((halo, c), lambda i: (jnp.maximum(i * (tm // halo) - 1, 0), 0)),
                  pl.BlockSpec(w.shape, lambda i: (0, 0))],
        out_specs=pl.BlockSpec((tm, c), lambda i: (i, 0)), out_shape=jax.ShapeDtypeStruct((s, c), _F32),
        scratch_shapes=[pltpu.VMEM((tm + 2 * halo, c), _F32)],
        compiler_params=_cp(1),
    )(h, h, w)


def _conv_bwd(h, d_y, w, *, name, tm=256):
    s = h.shape[0]
    tm = min(tm, s)
    c = 3 * A_WIDTH
    halo = SUBLANE
    steps = s // tm

    def body(x_ref, prev_ref, dy_ref, next_ref, w_ref, dx_ref, dw_ref, xbuf, dybuf):
        i = pl.program_id(0)

        @pl.when(i == 0)
        def _():
            dw_ref[...] = jnp.zeros(dw_ref.shape, dw_ref.dtype)

        xbuf[0:halo, :] = jnp.where(i == 0, 0.0, prev_ref[...])
        xbuf[halo:halo + tm, :] = x_ref[...]
        dybuf[0:tm, :] = dy_ref[...]
        dybuf[tm:tm + halo, :] = jnp.where(i == steps - 1, 0.0, next_ref[...])
        dy = dy_ref[...]
        acc = jnp.zeros((tm, c), _F32)
        for j in range(CONV_K):
            acc = acc + dybuf[pl.ds(CONV_K - 1 - j, tm), :] * w_ref[j:j + 1, :]
            dw_ref[j:j + 1, :] += jnp.sum(dy * xbuf[pl.ds(halo - (CONV_K - 1) + j, tm), :], axis=0, keepdims=True)
        dx_ref[...] = acc

    return _pcall(
        body, name=name, grid=(steps,),
        in_specs=[pl.BlockSpec((tm, c), lambda i: (i, 0)),
                  pl.BlockSpec((halo, c), lambda i: (jnp.maximum(i * (tm // halo) - 1, 0), 0)),
                  pl.BlockSpec((tm, c), lambda i: (i, 0)),
                  pl.BlockSpec((halo, c), lambda i: (jnp.minimum((i + 1) * (tm // halo), s // halo - 1), 0)),
                  pl.BlockSpec(w.shape, lambda i: (0, 0))],
        out_specs=[pl.BlockSpec((tm, c), lambda i: (i, 0)), pl.BlockSpec(w.shape, lambda i: (0, 0))],
        out_shape=[jax.ShapeDtypeStruct((s, c), _F32), jax.ShapeDtypeStruct(w.shape, _F32)],
        scratch_shapes=[pltpu.VMEM((tm + 2 * halo, c), _F32), pltpu.VMEM((tm + 2 * halo, c), _F32)],
        compiler_params=_cp(1),
    )(h, h, d_y, d_y, w)


ATTN_VMEM = 56 * 2 ** 20
ATTN_Q_BLOCK = 4096
ATTN_K_SUB = 256
ATTN_BWD_BLOCK = 1024
ATTN_BWD_SUB = 512


def _chunk_mask(shape, q_axis):
    qi = lax.broadcasted_iota(jnp.int32, shape, q_axis) // CHUNK
    ki = lax.broadcasted_iota(jnp.int32, shape, 1 - q_axis) // CHUNK
    return ki <= qi


def _rows(j, t):
    return pl.ds(pl.multiple_of(j * t, t), t)


def _chunk_mask_at(shape, q_axis, q_off):
    qi = (lax.broadcasted_iota(jnp.int32, shape, q_axis) + q_off) // CHUNK
    ki = lax.broadcasted_iota(jnp.int32, shape, 1 - q_axis) // CHUNK
    return ki <= qi


def _attn_fwd(q_all, kv, kr, *, name):
    s = q_all.shape[0]
    t = min(ATTN_Q_BLOCK, s)
    tk = min(ATTN_K_SUB, t)
    nq, sub, rep = s // t, t // tk, tk // LANE

    def body(qn_ref, qr_ref, kn_ref, kr_ref, v_ref, o_ref, lse_ref, m_sc, acc_sc):
        i = pl.program_id(1)
        m_sc[...] = jnp.full(m_sc.shape, -1e30, _F32)
        acc_sc[...] = jnp.zeros(acc_sc.shape, _F32)
        ones = jnp.ones((tk, LANE), _BF)

        def block(j, first_row):
            qs = slice(first_row, t)
            rows = _rows(j, tk)
            q = _cat([qn_ref[qs, :], qr_ref[qs, :]])
            sc = _dot_nt(q, _cat([kn_ref[rows, :], kr_ref[rows, :]]))
            if first_row is not None:
                sc = jnp.where(_chunk_mask(sc.shape, 0), sc, -1e30)
            m_prev = m_sc[qs, :]
            m_new = jnp.maximum(m_prev, jnp.max(sc, axis=-1, keepdims=True))
            alpha = jnp.exp(m_prev - m_new)
            p = jnp.exp(sc - _cat([m_new] * rep))
            acc_sc[qs, :] = _cat([alpha, alpha]) * acc_sc[qs, :] + _dot(p, _cat([v_ref[rows, :], ones]))
            m_sc[qs, :] = m_new

        def step(j, carry):
            block(j, None)
            return carry

        lax.fori_loop(0, i * sub, step, 0)
        for u in range(sub):
            block(i * sub + u, u * tk)
        row_sum = acc_sc[:, LANE:2 * LANE]
        o_ref[...] = acc_sc[:, 0:LANE] / row_sum
        lse_ref[...] = m_sc[...] + jnp.log(row_sum)

    return _pcall(
        body, name=name, grid=(HEADS, nq),
        in_specs=[pl.BlockSpec((t, LANE), lambda h, i: (i, h)),
                  pl.BlockSpec((t, LANE), lambda h, i: (i, HEADS + h)),
                  pl.BlockSpec((s, LANE), lambda h, i: (0, h), pipeline_mode=pl.Buffered(1)),
                  pl.BlockSpec((s, LANE), lambda h, i: (0, 0), pipeline_mode=pl.Buffered(1)),
                  pl.BlockSpec((s, LANE), lambda h, i: (0, HEADS + h), pipeline_mode=pl.Buffered(1))],
        out_specs=[pl.BlockSpec((t, LANE), lambda h, i: (i, h)), pl.BlockSpec((t, LANE), lambda h, i: (i, h))],
        out_shape=[jax.ShapeDtypeStruct((s, A_WIDTH), _F32), jax.ShapeDtypeStruct((s, A_WIDTH), _F32)],
        scratch_shapes=[pltpu.VMEM((t, LANE), _F32), pltpu.VMEM((t, 2 * LANE), _F32)],
        compiler_params=pltpu.CompilerParams(dimension_semantics=("arbitrary", "arbitrary"), vmem_limit_bytes=ATTN_VMEM),
    )(q_all, q_all, kv, kr, kv)


def _attn_bwd_prep(o, lse, d_cat, *, name):
    s = o.shape[0]
    t = min(ATTN_BWD_SUB, s)
    nq = s // t

    def body(o_ref, lse_ref, do_ref, dob_ref, lset_ref, dlt_ref):
        for h in range(HEADS):
            sl = slice(LANE * h, LANE * (h + 1))
            rows = slice(SUBLANE * h, SUBLANE * (h + 1))
            do = do_ref[:, sl]
            dl = jnp.broadcast_to(jnp.sum(o_ref[:, sl] * do, axis=-1, keepdims=True), (t, LANE))
            dob_ref[:, sl] = do.astype(dob_ref.dtype)
            dlt_ref[rows, :] = dl.T[0:SUBLANE, :]
            lset_ref[rows, :] = lse_ref[:, sl].T[0:SUBLANE, :]

    wide = pl.BlockSpec((t, A_WIDTH), lambda i: (i, 0))
    stat = pl.BlockSpec((HEADS * SUBLANE, t), lambda i: (i, 0))
    stat_shape = jax.ShapeDtypeStruct((nq * HEADS * SUBLANE, t), _F32)
    return _pcall(
        body, name=name, grid=(nq,), in_specs=[wide, wide, wide], out_specs=[wide, stat, stat],
        out_shape=[jax.ShapeDtypeStruct((s, A_WIDTH), _BF), stat_shape, stat_shape], compiler_params=_cp(1),
    )(o, lse, d_cat)


ATTN_BWD_VMEM = 56 * 2 ** 20


def _attn_bwd(q_all, kv, kr, lse_t, delta_t, d_o, *, name):
    s = q_all.shape[0]
    t = min(ATTN_BWD_BLOCK, s)
    tq = min(ATTN_BWD_SUB, t)
    nk, sub, nqs = s // t, t // tq, s // tq

    def body(kn_ref, kr_ref, v_ref, qn_ref, qr_ref, do_ref, lset_ref, dlt_ref, dkn_ref, dv_ref, dkr_ref, dqn_ref, dqr_ref, dk_sc, dv_sc):
        h, j = pl.program_id(0), pl.program_id(1)
        dk_sc[...] = jnp.zeros(dk_sc.shape, _F32)
        dv_sc[...] = jnp.zeros(dv_sc.shape, _F32)

        @pl.when(j == 0)
        def _():
            dqn_ref[...] = jnp.zeros(dqn_ref.shape, _F32)
            dqr_ref[...] = jnp.zeros(dqr_ref.shape, _F32)

        def block(i, query_off):
            ks = slice(0, t if query_off is None else query_off + tq)
            rows = _rows(i, tq)
            stat = pl.ds(pl.multiple_of((i * HEADS + h) * SUBLANE, SUBLANE), 1)
            q = _cat([qn_ref[rows, :], qr_ref[rows, :]])
            do = do_ref[rows, :]
            k = _cat([kn_ref[ks, :], kr_ref[ks, :]])
            p = jnp.exp(_dot_nt(k, q) - lset_ref[stat, :])
            if query_off is not None:
                p = jnp.where(_chunk_mask_at(p.shape, 1, query_off), p, 0.0)
            dv_sc[ks, :] += _dot(p, do)
            ds = p * (_dot_nt(v_ref[ks, :], do) - dlt_ref[stat, :])
            dk_sc[ks, :] += _dot(ds, q)
            dq = _dot_tn(ds, k)
            dqn_ref[rows, :] += dq[:, 0:LANE]
            dqr_ref[rows, :] += dq[:, LANE:2 * LANE]

        def step(i, carry):
            block(i, None)
            return carry

        for u in range(sub):
            block(j * sub + u, u * tq)
        lax.fori_loop((j + 1) * sub, nqs, step, 0)
        dkn_ref[...] = dk_sc[:, 0:LANE]
        dkr_ref[...] = dk_sc[:, LANE:2 * LANE]
        dv_ref[...] = dv_sc[...]

    once = pl.Buffered(1)
    stats = pl.BlockSpec((nqs * HEADS * SUBLANE, tq), lambda h, j: (0, 0), pipeline_mode=once)
    per_head = lambda at: pl.BlockSpec((s, LANE), at, pipeline_mode=once)
    return _pcall(
        body, name=name, grid=(HEADS, nk),
        in_specs=[pl.BlockSpec((t, LANE), lambda h, j: (j, h)),
                  pl.BlockSpec((t, LANE), lambda h, j: (j, 0)),
                  pl.BlockSpec((t, LANE), lambda h, j: (j, HEADS + h)),
                  per_head(lambda h, j: (0, h)), per_head(lambda h, j: (0, HEADS + h)), per_head(lambda h, j: (0, h)),
                  stats, stats],
        out_specs=[pl.BlockSpec((t, LANE), lambda h, j: (j, h))] * 3 + [per_head(lambda h, j: (0, h))] * 2,
        out_shape=[jax.ShapeDtypeStruct((s, A_WIDTH), _F32)] * 5,
        scratch_shapes=[pltpu.VMEM((t, 2 * LANE), _F32), pltpu.VMEM((t, LANE), _F32)],
        compiler_params=pltpu.CompilerParams(dimension_semantics=("arbitrary", "arbitrary"), vmem_limit_bytes=ATTN_BWD_VMEM),
    )(kv, kr, kv, q_all, q_all, d_o, lse_t, delta_t)


def _final_loss(x, tgt, gain, *, name, tm=512):
    tm = min(tm, x.shape[0])

    def fn(x_, t_, g_):
        def f(xx, gg):
            err = _rms(xx, gg) - t_
            return 0.5 * jnp.sum(jnp.sum(err * err, axis=1, keepdims=True) / D_MODEL, axis=0, keepdims=True)

        loss, pull = jax.vjp(f, x_, g_)
        dx, dg = pull(jnp.ones((1, 1), _F32))
        return dx, dg, jnp.broadcast_to(loss, (SUBLANE, LANE))

    return _rowmap(fn, [_t(x), _t(tgt)], [gain], [(None, D_MODEL, _F32)], [(gain.shape, _F32), ((SUBLANE, LANE), _F32)], tm=tm, name=name)


def _local_step(x, mem, cos_t, sin_t, tgt, p):
    s = x.shape[0]
    g = {}
    row = lambda a: a.reshape(1, -1)
    tm_e = min(256, s)

    mem_n = _rms_fwd(mem, row(p["mem_norm"]), name="mem_norm")
    mem_kv_all = _mm_nn(mem_n, p["w_mem_all"], out_dtype=_F32, name="mem_kv", tn=1024)
    mem_kv = [mem_kv_all[:, 2 * MEM_WIDTH * l:2 * MEM_WIDTH * (l + 1)] for l in range(DEPTH)]

    sv = []
    for i in range(N_A):
        l = i
        r = {}
        r["x0"] = x
        x, r["ffn1"] = _ffn_fwd(x, row(p["ffn1_norm"][l]), p["ffn_w_gu"], p["ffn_w_down"], l, name=f"a{i}_ffn1")
        r["x1"] = x
        xn = _rms_fwd(x, row(p["mix_norm"][l]), name=f"a{i}_mixnorm")
        h = _mm_nn(xn, p["a_w_in"][i], out_dtype=_F32, name=f"a{i}_in", tn=1152)
        qkv_c = _conv_fwd(h, p["a_conv"][i], name=f"a{i}_conv")
        alog, dtb = p["a_A_log_row"][i], p["a_dt_bias_row"][i]
        q, k, v, bg = _rowmap(_gdn_prep_fwd_fn, [_t(qkv_c), _t(h, LANE, 26)], [alog, dtb],
                              [(None, A_WIDTH, _F32)] * 3 + [(None, LANE, _F32)], tm=tm_e, name=f"a{i}_prep")
        w_, u_, qd, kd, qk, dcrow = _rowmap(_gdn_intra_fwd_fn, [_t(q), _t(k), _t(v), _t(bg)], [],
                                            [(None, A_WIDTH, _F32)] * 5 + [(None, LANE, _F32)], tm=_PAIR, name=f"a{i}_intra")
        o, states = _rowmap(_gdn_scan_fwd_fn, [_t(qd), _t(kd), _t(w_), _t(u_), _t(qk), _t(dcrow)], [],
                            [(None, A_WIDTH, _F32), (A_WIDTH, HEAD_DIM, _F32)], tm=_PAIR, name=f"a{i}_scan",
                            carry=[((A_WIDTH, HEAD_DIM), _F32)])
        gain_o = row(p["a_out_norm"][i])
        cat = _rowmap(_a_post_fwd_fn, [_t(o), _t(h, A_WIDTH, 3), _t(h, MEM_WIDTH, 12)], [gain_o, mem_kv[l]],
                      [(None, D_MODEL, _BF)], tm=tm_e, name=f"a{i}_post")[0]
        x = _mm_nn(cat, p["w_out"][l], out_dtype=_F32, name=f"a{i}_out", res=x, tn=1024)
        r.update(xn=xn, h=h, qkv_c=qkv_c, q=q, k=k, v=v, bg=bg, w=w_, u=u_, qd=qd, kd=kd, qk=qk, dcrow=dcrow, o=o, states=states, cat=cat)
        r["x2"] = x
        x, r["ffn2"] = _ffn_fwd(x, row(p["ffn2_norm"][l]), p["ffn_w_gu"], p["ffn_w_down"], DEPTH + l, name=f"a{i}_ffn2")
        sv.append(r)

    kvs = {"x": x}
    xn_kv = _rms_fwd(x, row(p["kv_in_norm"]), name="kv_innorm")
    ckr = _mm_nn(xn_kv, p["w_dkv"], out_dtype=_F32, name="kv_down")
    ckv, k_rope = _rowmap(_kv_prep_fwd_fn, [_t(ckr), _t(cos_t), _t(sin_t)], [row(p["kv_lat_norm"])],
                          [(None, KV_LORA, _BF), (None, LANE, _BF)], tm=tm_e, name="kv_prep")
    kvu = _mm_nn(ckv, p["w_ukv"], out_dtype=_BF, name="kv_up")
    kvs.update(xn=xn_kv, ckr=ckr, ckv=ckv)

    for j in range(N_B):
        l = N_A + j
        r = {}
        x, r["ffn1"] = _ffn_fwd(x, row(p["ffn1_norm"][l]), p["ffn_w_gu"], p["ffn_w_down"], l, name=f"b{j}_ffn1")
        r["x1"] = x
        xn = _rms_fwd(x, row(p["mix_norm"][l]), name=f"b{j}_mixnorm")
        h = _mm_nn(xn, p["b_w_in"][j], out_dtype=_F32, name=f"b{j}_in")
        gain_q = row(p["b_q_norm"][j])
        cqn = _rowmap(lambda c_, g_: (_rms(c_, g_),), [_t(h, Q_LORA, 0)], [gain_q], [(None, Q_LORA, _BF)], tm=tm_e, name=f"b{j}_qnorm")[0]
        qf = _mm_nn(cqn, p["b_w_uq"][j], out_dtype=_F32, name=f"b{j}_uq")
        q_all = _rowmap(_rope_q_fwd_fn, [_t(qf), _t(cos_t), _t(sin_t)], [], [(None, UQ_PAD, _BF)], tm=tm_e, name=f"b{j}_rope")[0]
        o_b, lse = _attn_fwd(q_all, kvu, k_rope, name=f"b{j}_attn")
        cat = _rowmap(_b_post_fwd_fn, [_t(o_b), _t(h, MEM_WIDTH, 1)], [mem_kv[l]], [(None, D_MODEL, _BF)], tm=tm_e, name=f"b{j}_post")[0]
        x = _mm_nn(cat, p["w_out"][l], out_dtype=_F32, name=f"b{j}_out", res=x, tn=1024)
        r.update(xn=xn, h=h, cqn=cqn, q_all=q_all, o_b=o_b, lse=lse, cat=cat)
        x, r["ffn2"] = _ffn_fwd(x, row(p["ffn2_norm"][l]), p["ffn_w_gu"], p["ffn_w_down"], DEPTH + l, name=f"b{j}_ffn2")
        sv.append(r)

    dx, g["final_norm"], loss = _final_loss(x, tgt, row(p["final_norm"]), name="loss")

    per_layer = lambda: [None] * DEPTH
    for n in ("ffn1_norm", "mix_norm", "ffn2_norm", "w_out", "mem_kv"):
        g[n] = per_layer()
    g["ffn_w_gu"] = jnp.zeros((N_CHIPS, 2 * DEPTH, D_MODEL, FFN_COL_TILE), _F32)
    g["ffn_w_down"] = jnp.zeros((N_CHIPS, 2 * DEPTH, FFN_SHARD_ROWS, D_MODEL), _F32)
    for n in ("a_w_in", "a_conv", "a_A_log_row", "a_dt_bias_row", "a_out_norm", "b_w_in", "b_q_norm", "b_w_uq"):
        g[n] = [None] * N_A
    d_kv_parts = []

    for j in reversed(range(N_B)):
        l = N_A + j
        r = sv[l]
        dx, g["ffn2_norm"][l], g["ffn_w_gu"], g["ffn_w_down"] = _ffn_bwd(
            dx, r["ffn2"], row(p["ffn2_norm"][l]), p["ffn_w_gu"], p["ffn_w_down"], DEPTH + l, g["ffn_w_gu"], g["ffn_w_down"], name=f"b{j}_ffn2b")
        d_cat = _mm_nt(dx, p["w_out"][l], out_dtype=_F32, name=f"b{j}_dcat", tn=1024)
        g["w_out"][l] = _mm_tn(r["cat"], dx, name=f"b{j}_dwout", tn=1024)
        d_qm, g["mem_kv"][l] = _rowmap(_b_post_bwd_fn, [_t(r["h"], MEM_WIDTH, 1), _t(d_cat, MEM_WIDTH, 3)], [mem_kv[l]],
                                      [(None, MEM_WIDTH, _F32)], [((N_MEM, 2 * MEM_WIDTH), _F32)], tm=tm_e, name=f"b{j}_postb")
        d_o, lse_t, delta_t = _attn_bwd_prep(r["o_b"], r["lse"], d_cat, name=f"b{j}_delta")
        dkn, dv, dkr, dqn, dqr = _attn_bwd(r["q_all"], kvu, k_rope, lse_t, delta_t, d_o, name=f"b{j}_attn_bwd")
        d_kv_parts.append((dkn, dv, dkr))
        d_qf = _rowmap(_rope_q_bwd_fn, [_t(dqn), _t(dqr), _t(cos_t), _t(sin_t)], [], [(None, UQ_PAD, _F32)], tm=tm_e, name=f"b{j}_ropeb")[0]
        d_cqn = _mm_nt(d_qf, p["b_w_uq"][j], out_dtype=_F32, name=f"b{j}_dcqn")
        g["b_w_uq"][j] = _mm_tn(r["cqn"], d_qf, name=f"b{j}_dwuq")
        gain_q = row(p["b_q_norm"][j])
        d_cq, g["b_q_norm"][j] = _rowmap(lambda c_, d_, g_: _vjp(_rms, [c_, g_], (d_,)), [_t(r["h"], Q_LORA, 0), _t(d_cqn)], [gain_q],
                                        [(None, Q_LORA, _F32)], [((1, Q_LORA), _F32)], tm=tm_e, name=f"b{j}_qnormb")
        d_h = jnp.concatenate([d_cq, d_qm], axis=1).astype(_BF)
        d_xn = _mm_nt(d_h, p["b_w_in"][j], out_dtype=_F32, name=f"b{j}_dxn", tn=1024)
        g["b_w_in"][j] = _mm_tn(r["xn"], d_h, name=f"b{j}_dwin")
        dx, g["mix_norm"][l] = _rms_bwd(r["x1"], d_xn, dx, row(p["mix_norm"][l]), name=f"b{j}_mixnormb")
        dx, g["ffn1_norm"][l], g["ffn_w_gu"], g["ffn_w_down"] = _ffn_bwd(
            dx, r["ffn1"], row(p["ffn1_norm"][l]), p["ffn_w_gu"], p["ffn_w_down"], l, g["ffn_w_gu"], g["ffn_w_down"], name=f"b{j}_ffn1b")

    def kv_sum(*parts):
        dkn = sum(parts[0::3][1:], parts[0])
        dv = sum(parts[1::3][1:], parts[1])
        dkr = sum(parts[2::3][1:], parts[2])
        return _cat([dkn, dv]), sum(_heads(dkr, HEADS)[1:], _heads(dkr, HEADS)[0])

    d_kvu, d_kr = _rowmap(kv_sum, [_t(a) for part in d_kv_parts for a in part], [], [(None, 2 * A_WIDTH, _F32), (None, LANE, _F32)],
                          tm=tm_e, name="kv_dsum")
    d_ckv = _mm_nt(d_kvu, p["w_ukv"], out_dtype=_F32, name="kv_dckv")
    g["w_ukv"] = _mm_tn(kvs["ckv"], d_kvu, name="kv_dwukv")
    d_ckr, g["kv_lat_norm"] = _rowmap(_kv_prep_bwd_fn, [_t(kvs["ckr"]), _t(d_ckv), _t(d_kr), _t(cos_t), _t(sin_t)], [row(p["kv_lat_norm"])],
                                     [(None, DKV_PAD, _F32)], [((1, KV_LORA), _F32)], tm=tm_e, name="kv_prepb")
    d_xn = _mm_nt(d_ckr, p["w_dkv"], out_dtype=_F32, name="kv_dxn", tn=1024)
    g["w_dkv"] = _mm_tn(kvs["xn"], d_ckr, name="kv_dwdkv")
    dx, g["kv_in_norm"] = _rms_bwd(kvs["x"], d_xn, dx, row(p["kv_in_norm"]), name="kv_innormb")

    for i in reversed(range(N_A)):
        l = i
        r = sv[l]
        dx, g["ffn2_norm"][l], g["ffn_w_gu"], g["ffn_w_down"] = _ffn_bwd(
            dx, r["ffn2"], row(p["ffn2_norm"][l]), p["ffn_w_gu"], p["ffn_w_down"], DEPTH + l, g["ffn_w_gu"], g["ffn_w_down"], name=f"a{i}_ffn2b")
        d_cat = _mm_nt(dx, p["w_out"][l], out_dtype=_F32, name=f"a{i}_dcat", tn=1024)
        g["w_out"][l] = _mm_tn(r["cat"], dx, name=f"a{i}_dwout", tn=1024)
        gain_o = row(p["a_out_norm"][i])
        h = r["h"]
        d_o, d_hpart, g["a_out_norm"][i], g["mem_kv"][l] = _rowmap(
            _a_post_bwd_fn, [_t(r["o"]), _t(h, A_WIDTH, 3), _t(h, MEM_WIDTH, 12), _t(d_cat)], [gain_o, mem_kv[l]],
            [(None, A_WIDTH, _F32), (None, D_MODEL, _F32)], [((1, HEAD_DIM), _F32), ((N_MEM, 2 * MEM_WIDTH), _F32)], tm=tm_e, name=f"a{i}_postb")
        d_qd, d_kd, d_w, d_u, d_qk, d_dcrow = _rowmap(
            _gdn_scan_bwd_fn, [_t(r["qd"]), _t(r["kd"]), _t(r["w"]), _t(r["u"]), _t(r["qk"]), _t(r["dcrow"]), _t(r["states"], rows=A_WIDTH), _t(d_o)], [],
            [(None, A_WIDTH, _F32)] * 5 + [(None, LANE, _F32)], tm=_PAIR, name=f"a{i}_scanb", carry=[((A_WIDTH, HEAD_DIM), _F32)], reverse=True)
        d_q, d_k, d_v, d_bg = _rowmap(
            _gdn_intra_bwd_fn, [_t(r["q"]), _t(r["k"]), _t(r["v"]), _t(r["bg"]), _t(d_w), _t(d_u), _t(d_qd), _t(d_kd), _t(d_qk), _t(d_dcrow)], [],
            [(None, A_WIDTH, _F32)] * 3 + [(None, LANE, _F32)], tm=_PAIR, name=f"a{i}_intrab")
        alog, dtb = p["a_A_log_row"][i], p["a_dt_bias_row"][i]
        d_qkv_c, d_ba, g["a_A_log_row"][i], g["a_dt_bias_row"][i] = _rowmap(
            _gdn_prep_bwd_fn, [_t(r["qkv_c"]), _t(h, LANE, 26), _t(d_q), _t(d_k), _t(d_v), _t(d_bg)], [alog, dtb],
            [(None, 3 * A_WIDTH, _F32), (None, LANE, _F32)], [((1, LANE), _F32), ((1, LANE), _F32)], tm=tm_e, name=f"a{i}_prepb")
        d_qkv, g["a_conv"][i] = _conv_bwd(h, d_qkv_c, p["a_conv"][i], name=f"a{i}_convb")
        d_h = jnp.concatenate([d_qkv, d_hpart, d_ba], axis=1).astype(_BF)
        d_xn = _mm_nt(d_h, p["a_w_in"][i], out_dtype=_F32, name=f"a{i}_dxn", tn=1024)
        g["a_w_in"][i] = _mm_tn(r["xn"], d_h, name=f"a{i}_dwin", tn=1152)
        dx, g["mix_norm"][l] = _rms_bwd(r["x1"], d_xn, dx, row(p["mix_norm"][l]), name=f"a{i}_mixnormb")
        dx, g["ffn1_norm"][l], g["ffn_w_gu"], g["ffn_w_down"] = _ffn_bwd(
            dx, r["ffn1"], row(p["ffn1_norm"][l]), p["ffn_w_gu"], p["ffn_w_down"], l, g["ffn_w_gu"], g["ffn_w_down"], name=f"a{i}_ffn1b")

    d_mem_kv_all = jnp.concatenate(g.pop("mem_kv"), axis=1)
    d_mem_n = _mm_nt(d_mem_kv_all, p["w_mem_all"], out_dtype=_F32, name="mem_dn", tn=1024)
    g["w_mem_all"] = _mm_tn(mem_n, d_mem_kv_all, name="mem_dw", tn=1024)
    _, g["mem_norm"] = _rms_bwd(mem, d_mem_n, None, row(p["mem_norm"]), name="mem_normb")
    return loss, dx, g


_NOPE_ROPE = HEAD_DIM + QK_ROPE
_QKV_GATE = 4 * A_WIDTH
_BETA_AT = _QKV_GATE + MEM_WIDTH


def _lane_row(vals, at):
    n = vals.shape[0]
    return jnp.concatenate([jnp.zeros((at,), _F32), vals.astype(_F32), jnp.zeros((LANE - at - n,), _F32)]).reshape(1, LANE)


def _compute_form(w, conv_f32, ffn_w_gu, ffn_w_down):
    p = {n: w[n] for n in ("ffn1_norm", "mix_norm", "ffn2_norm", "w_out", "mem_norm", "a_out_norm", "b_w_in", "b_q_norm", "kv_in_norm",
                           "kv_lat_norm", "final_norm")}
    p["ffn_w_gu"], p["ffn_w_down"] = ffn_w_gu, ffn_w_down
    wm = w["w_mem_kv"]
    p["w_mem_all"] = jnp.transpose(wm, (1, 0, 2)).reshape(D_MODEL, DEPTH * 2 * MEM_WIDTH)
    a = w["a_w_in"]
    pad = jnp.zeros((N_A, D_MODEL, A_IN_PAD - A_IN), a.dtype)
    p["a_w_in"] = jnp.concatenate([a[:, :, :_QKV_GATE], a[:, :, _QKV_GATE + 2 * HEADS:], a[:, :, _QKV_GATE:_QKV_GATE + 2 * HEADS], pad], axis=2)
    p["a_conv"] = jnp.concatenate([conv_f32, jnp.zeros((N_A, SUBLANE - CONV_K, 3 * A_WIDTH), _F32)], axis=1)
    p["a_A_log_row"] = [_lane_row(w["a_A_log"][i], HEADS) for i in range(N_A)]
    p["a_dt_bias_row"] = [_lane_row(w["a_dt_bias"][i], HEADS) for i in range(N_A)]
    uq = w["b_w_uq"].reshape(N_B, Q_LORA, HEADS, _NOPE_ROPE)
    rope = jnp.concatenate([uq[..., HEAD_DIM:], jnp.zeros((N_B, Q_LORA, HEADS, LANE - QK_ROPE), uq.dtype)], axis=-1)
    p["b_w_uq"] = jnp.concatenate([uq[..., :HEAD_DIM].reshape(N_B, Q_LORA, A_WIDTH), rope.reshape(N_B, Q_LORA, A_WIDTH)], axis=-1)
    dkv = w["w_dkv"]
    p["w_dkv"] = jnp.concatenate([dkv, jnp.zeros((D_MODEL, DKV_PAD - dkv.shape[1]), dkv.dtype)], axis=1)
    ukv = w["w_ukv"].reshape(KV_LORA, HEADS, 2 * HEAD_DIM)
    p["w_ukv"] = jnp.concatenate([ukv[..., :HEAD_DIM].reshape(KV_LORA, A_WIDTH), ukv[..., HEAD_DIM:].reshape(KV_LORA, A_WIDTH)], axis=-1)
    return p


def _natural_grads(g):
    st = lambda xs: jnp.stack(xs, axis=0)
    n = {}
    for k in ("ffn1_norm", "mix_norm", "ffn2_norm"):
        n[k] = st(g[k]).reshape(DEPTH, D_MODEL)
    for k in ("w_out", "b_w_in"):
        n[k] = st(g[k])
    n["mem_norm"] = g["mem_norm"].reshape(D_MODEL)
    n["w_mem_kv"] = jnp.transpose(g["w_mem_all"].reshape(D_MODEL, DEPTH, 2 * MEM_WIDTH), (1, 0, 2))
    a = st(g["a_w_in"])
    n["a_w_in"] = jnp.concatenate([a[:, :, :_QKV_GATE], a[:, :, _BETA_AT:_BETA_AT + 2 * HEADS], a[:, :, _QKV_GATE:_BETA_AT]], axis=2)
    n["a_conv"] = st(g["a_conv"])[:, :CONV_K]
    n["a_A_log"] = st(g["a_A_log_row"])[:, 0, HEADS:2 * HEADS]
    n["a_dt_bias"] = st(g["a_dt_bias_row"])[:, 0, HEADS:2 * HEADS]
    n["a_out_norm"] = st(g["a_out_norm"]).reshape(N_A, HEAD_DIM)
    n["b_q_norm"] = st(g["b_q_norm"]).reshape(N_B, Q_LORA)
    uq = st(g["b_w_uq"])
    nope = uq[:, :, :A_WIDTH].reshape(N_B, Q_LORA, HEADS, HEAD_DIM)
    rope = uq[:, :, A_WIDTH:].reshape(N_B, Q_LORA, HEADS, LANE)[..., :QK_ROPE]
    n["b_w_uq"] = jnp.concatenate([nope, rope], axis=-1).reshape(N_B, Q_LORA, HEADS * _NOPE_ROPE)
    n["kv_in_norm"] = g["kv_in_norm"].reshape(D_MODEL)
    n["w_dkv"] = g["w_dkv"][:, :KV_LORA + QK_ROPE]
    n["kv_lat_norm"] = g["kv_lat_norm"].reshape(KV_LORA)
    ukv = g["w_ukv"]
    n["w_ukv"] = jnp.concatenate([ukv[:, :A_WIDTH].reshape(KV_LORA, HEADS, HEAD_DIM), ukv[:, A_WIDTH:].reshape(KV_LORA, HEADS, HEAD_DIM)],
                                 axis=-1).reshape(KV_LORA, HEADS * 2 * HEAD_DIM)
    n["final_norm"] = g["final_norm"].reshape(D_MODEL)
    return n


def _rope_tables(positions):
    inv = ROPE_THETA ** (-jnp.arange(0, QK_ROPE, 2, dtype=_F32) / QK_ROPE)
    ang = positions.astype(_F32)[:, None] * inv
    z = jnp.zeros((positions.shape[0], LANE - QK_ROPE), _F32)
    cos, sin = jnp.cos(ang), jnp.sin(ang)
    return jnp.concatenate([cos, cos, z], axis=1), jnp.concatenate([sin, sin, z], axis=1)


_HBM = pl.BlockSpec(memory_space=pltpu.HBM)


def _place():
    x, y, c = lax.axis_index("x"), lax.axis_index("y"), lax.axis_index("c")
    return x, y, c, [(1 - x, y), (x, 1 - y), (1 - x, 1 - y)]


def _remote(src, dst, send_sem, recv_sem, to):
    return pltpu.make_async_remote_copy(src_ref=src, dst_ref=dst, send_sem=send_sem, recv_sem=recv_sem, device_id=to, device_id_type=_MESH)


def _gather_over_chips(shard, *, name):
    rows, cols = shard.shape
    half = rows // 2

    def body(w_ref, out_ref, send_sems, recv_sems):
        x, y, c, chips = _place()
        k = 2 * x + y

        def part(chip, h):
            return out_ref.at[chip, pl.ds(h * half, half), :]

        first = [_remote(w_ref.at[pl.ds(c * half, half), :], part(k, c), send_sems.at[j], recv_sems.at[j], (px, py, c))
                 for j, (px, py) in enumerate(chips)]
        for cp in first:
            cp.start()
        passed = []
        for j, (px, py) in enumerate(chips):
            got = part(2 * px + py, c)
            _remote(got, got, send_sems.at[j], recv_sems.at[j], (px, py, c)).wait_recv()
            fw = _remote(got, got, send_sems.at[3 + j], recv_sems.at[3 + j], (x, y, 1 - c))
            fw.start()
            passed.append(fw)
        for j, (px, py) in enumerate(chips):
            got = part(2 * px + py, 1 - c)
            _remote(got, got, send_sems.at[3 + j], recv_sems.at[3 + j], (x, y, 1 - c)).wait_recv()
        for cp in first + passed:
            cp.wait_send()

    others = _pcall(
        body, name=name, in_specs=[_HBM], out_specs=_HBM, out_shape=jax.ShapeDtypeStruct((N_CHIPS, rows, cols), shard.dtype),
        scratch_shapes=[pltpu.SemaphoreType.DMA((6,)), pltpu.SemaphoreType.DMA((6,))],
    )(shard)
    return lax.dynamic_update_slice(others, shard[None], (2 * lax.axis_index("x") + lax.axis_index("y"), 0, 0))


PAIR_COPIES = 4


def _scatter_over_chips(v, *, name):
    def body(v_ref, out_ref, send_sems, recv_sems):
        x, y, c, chips = _place()
        cps = [_remote(v_ref.at[2 * px + py], out_ref.at[j], send_sems.at[j], recv_sems.at[j], (px, py, c)) for j, (px, py) in enumerate(chips)]
        for cp in cps:
            cp.start()
        for cp in cps:
            cp.wait()

    return _pcall(body, name=name, in_specs=[_HBM], out_specs=_HBM, out_shape=jax.ShapeDtypeStruct((N_CHIPS - 1,) + v.shape[1:], v.dtype),
                  scratch_shapes=[pltpu.SemaphoreType.DMA((3,)), pltpu.SemaphoreType.DMA((3,))])(v)


def _all_reduce_small(v, *, name):
    def body(v_ref, out_ref, all_ref, send_sems, recv_sems):
        x, y, c, _ = _place()
        me = 4 * x + 2 * y + c
        all_ref[me] = v_ref[...]
        cps = []
        for f in range(1, N_DEV):
            fx, fy, fc = (f >> 2) & 1, (f >> 1) & 1, f & 1
            to = (x + fx - 2 * x * fx, y + fy - 2 * y * fy, c + fc - 2 * c * fc)
            cps.append(_remote(v_ref, all_ref.at[me], send_sems.at[f - 1], recv_sems.at[f - 1], to))
        for cp in cps:
            cp.start()
        for cp in cps:
            cp.wait()
        acc = all_ref[0]
        for d in range(1, N_DEV):
            acc = acc + all_ref[d]
        out_ref[...] = acc

    vm = pl.BlockSpec(memory_space=pltpu.VMEM)
    return _pcall(body, name=name, in_specs=[vm], out_specs=vm, out_shape=jax.ShapeDtypeStruct(v.shape, v.dtype),
                  scratch_shapes=[pltpu.VMEM((N_DEV,) + v.shape, v.dtype), pltpu.SemaphoreType.DMA((N_DEV - 1,)), pltpu.SemaphoreType.DMA((N_DEV - 1,))])(v)


_FFN_GU = ("ffn1_w_gu", "ffn2_w_gu")
_FFN_DOWN = ("ffn1_w_down", "ffn2_w_down")
_BIG = (("w_out", 1), ("w_mem_kv", 1), ("a_w_in", 2), ("a_conv", 2), ("b_w_in", 1), ("b_w_uq", 2), ("w_dkv", 0),
        ("w_ukv", 1))
_SMALL = ("ffn1_norm", "mix_norm", "ffn2_norm", "mem_norm", "a_A_log", "a_dt_bias", "a_out_norm", "b_q_norm", "kv_in_norm", "kv_lat_norm",
          "final_norm")
_WEIGHTS = ("ffn1_norm", "ffn1_w_gu", "ffn1_w_down", "mix_norm", "ffn2_norm", "ffn2_w_gu", "ffn2_w_down", "w_out", "mem_norm", "w_mem_kv",
            "a_w_in", "a_conv", "a_A_log", "a_dt_bias", "a_out_norm", "b_w_in", "b_q_norm", "b_w_uq", "kv_in_norm", "w_dkv", "kv_lat_norm",
            "w_ukv", "final_norm")


PACK_PIECE_ROWS = 16


def _piece_rows(shape):
    return -(-math.prod(shape) // (PACK_COLS * PACK_PIECE_ROWS)) * PACK_PIECE_ROWS


def _packed_rows(shapes):
    return sum(_piece_rows(s) for s in shapes)


def _pack(arrs, rows):
    pieces = []
    for a in arrs:
        n, r = a.size, _piece_rows(a.shape)
        flat = a.reshape(-1)
        if r * PACK_COLS != n:
            flat = jnp.concatenate([flat, jnp.zeros((r * PACK_COLS - n,), a.dtype)])
        pieces.append(flat.reshape(r, PACK_COLS))
    used = sum(p.shape[0] for p in pieces)
    if rows > used:
        pieces.append(jnp.zeros((rows - used, PACK_COLS), arrs[0].dtype))
    return jnp.concatenate(pieces, axis=0)


def _unpack(packed, shapes):
    off, out = 0, []
    for shp in shapes:
        n, r = math.prod(shp), _piece_rows(shp)
        piece = packed[off:off + r]
        out.append((piece if r * PACK_COLS == n else piece.reshape(-1)[:n]).reshape(shp))
        off += r
    return out


def _adamw_math(w_, g_, m_, v_):
    m2 = ADAM_B1 * m_ + (1.0 - ADAM_B1) * g_
    v2 = ADAM_B2 * v_ + (1.0 - ADAM_B2) * (g_ * g_)
    m_hat = m2 / (1.0 - ADAM_B1 ** ADAM_STEP)
    v_hat = v2 / (1.0 - ADAM_B2 ** ADAM_STEP)
    return -ADAM_LR * (m_hat / (jnp.sqrt(v_hat) + ADAM_EPS) + ADAM_WD * w_), m2, v2


def _adamw(w, g, m, v, *, name):
    return _rowmap(_adamw_math, [_t(w), _t(g), _t(m), _t(v)], [], [(None, w.shape[1], _F32)] * 3, tm=min(PACK_ROW_TILE, w.shape[0]), name=name)


def _pair_exchange_half(g, *, name):
    n, _, h, cols = g.shape

    def body(g_ref, out_ref, send_sems, recv_sems):
        x, y, c, _ = _place()
        cps = [_remote(g_ref.at[k, 1 - c], out_ref.at[k], send_sems.at[k], recv_sems.at[k], (x, y, 1 - c)) for k in range(n)]
        for cp in cps:
            cp.start()
        for cp in cps:
            cp.wait()

    return _pcall(body, name=name, in_specs=[_HBM], out_specs=_HBM, out_shape=jax.ShapeDtypeStruct((n, h, cols), g.dtype),
                  scratch_shapes=[pltpu.SemaphoreType.DMA((n,)), pltpu.SemaphoreType.DMA((n,))])(g)


def _add_half(g, other, c, *, name):
    n, _, h, cols = g.shape
    tm = min(PACK_ROW_TILE, h)

    def body(c_ref, g_ref, o_ref, sum_ref, narrow_ref):
        acc = g_ref[...] + o_ref[...]
        sum_ref[...] = acc
        narrow_ref[...] = acc.astype(narrow_ref.dtype)

    slab = pl.BlockSpec((None, tm, cols), lambda k, i, c_ref: (k, i, 0))
    return _pcall(
        body, name=name,
        grid_spec=pltpu.PrefetchScalarGridSpec(
            num_scalar_prefetch=1, grid=(n, h // tm),
            in_specs=[pl.BlockSpec((None, None, tm, cols), lambda k, i, c_ref: (k, c_ref[0], i, 0)), slab], out_specs=[slab, slab]),
        out_shape=[jax.ShapeDtypeStruct((n, h, cols), _F32), jax.ShapeDtypeStruct((n, h, cols), jnp.bfloat16)],
        compiler_params=_cp(2),
    )(jnp.reshape(c, (1,)).astype(jnp.int32), g, other)


def _add_own(chip_sum, from_chips, chip, *, name):
    _, h, cols = chip_sum.shape
    tm = min(PACK_ROW_TILE, h)

    def body(k_ref, own_ref, a_ref, b_ref, c_ref, o_ref):
        o_ref[...] = ((own_ref[...] + a_ref[...].astype(_F32)) + b_ref[...].astype(_F32)) + c_ref[...].astype(_F32)

    sent = [pl.BlockSpec((None, tm, cols), functools.partial(lambda i, k_ref, j: (j, i, 0), j=j)) for j in range(N_CHIPS - 1)]
    return _pcall(
        body, name=name,
        grid_spec=pltpu.PrefetchScalarGridSpec(
            num_scalar_prefetch=1, grid=(h // tm,),
            in_specs=[pl.BlockSpec((None, tm, cols), lambda i, k_ref: (k_ref[0], i, 0))] + sent,
            out_specs=pl.BlockSpec((tm, cols), lambda i, k_ref: (i, 0))),
        out_shape=jax.ShapeDtypeStruct((h, cols), _F32), compiler_params=_cp(1),
    )(jnp.reshape(chip, (1,)).astype(jnp.int32), chip_sum, from_chips, from_chips, from_chips)


def _pair_gather(mine, *, name):
    h, cols = mine.shape
    per = h // PAIR_COPIES
    assert per * PAIR_COPIES == h and per % SUBLANE == 0, mine.shape

    def body(v_ref, out_ref, send_sems, recv_sems):
        x, y, c, _ = _place()
        cps = [_remote(v_ref.at[pl.ds(q * per, per), :], out_ref.at[c, pl.ds(q * per, per), :], send_sems.at[q], recv_sems.at[q], (x, y, 1 - c))
               for q in range(PAIR_COPIES)]
        for cp in cps:
            cp.start()
        for q, cp in enumerate(cps):
            cp.wait_send()
            _remote(v_ref.at[pl.ds(q * per, per), :], out_ref.at[1 - c, pl.ds(q * per, per), :], send_sems.at[q], recv_sems.at[q],
                    (x, y, 1 - c)).wait_recv()

    both = _pcall(body, name=name, in_specs=[_HBM], out_specs=_HBM, out_shape=jax.ShapeDtypeStruct((2, h, cols), mine.dtype),
                  scratch_shapes=[pltpu.SemaphoreType.DMA((PAIR_COPIES,)), pltpu.SemaphoreType.DMA((PAIR_COPIES,))])(mine)
    return lax.dynamic_update_slice(both, mine[None], (lax.axis_index("c"), 0, 0))


def _reduce_over_devices(g, c, chip, *, name):
    n, rows, cols = g.shape
    g = g.reshape(n, 2, rows // 2, cols)
    chip_sum, narrow = _add_half(g, _pair_exchange_half(g, name=name + "_pair_sum"), c, name=name + "_add_pair")
    mine = _add_own(chip_sum, _scatter_over_chips(narrow, name=name + "_scatter"), chip, name=name + "_add_chips")
    return _pair_gather(mine, name=name + "_pair_gather").reshape(rows, cols)


def _adamw_at(w, m, v, g_all, first_row, *, name):
    tm = min(PACK_ROW_TILE, w.shape[0])
    assert w.shape[0] % tm == 0 and first_row % tm == 0, (name, w.shape, first_row)

    def fn(w_, m_, v_, g_):
        return _adamw_math(w_, g_, m_, v_) + (g_,)

    return _rowmap(fn, [_t(w), _t(m), _t(v), _t(g_all, first=first_row // tm)], [], [(None, w.shape[1], _F32)] * 4, tm=tm, name=name)


def _step(x, mem, positions, loss_target, w, m, v):
    cx, cy, cc = lax.axis_index("x"), lax.axis_index("y"), lax.axis_index("c")
    chip = 2 * cx + cy
    big = [n for n, _ in _BIG]
    shard_shapes = [w[n].shape for n in big]
    rows = -(-_packed_rows(shard_shapes) // (2 * PACK_ROW_TILE)) * 2 * PACK_ROW_TILE
    flat = lambda a: a.reshape(-1, a.shape[-1])

    ffn_w_gu = _gather_over_chips(jnp.concatenate([flat(w[n]) for n in _FFN_GU]).astype(_BF), name="gather_w_gu")
    ffn_w_down = _gather_over_chips(jnp.concatenate([flat(w[n]) for n in _FFN_DOWN]).astype(_BF), name="gather_w_down")
    w_pack = _pack([w[n] for n in big], rows)
    gathered = _gather_over_chips(w_pack.astype(_BF), name="gather_weights")
    pieces = [_unpack(gathered[k], shard_shapes) for k in range(N_CHIPS)]
    full = {n: jnp.concatenate([pieces[k][i] for k in range(N_CHIPS)], axis=ax) for i, (n, ax) in enumerate(_BIG)}
    for n in _SMALL:
        full[n] = w[n]
    conv = w["a_conv"]
    slots = jnp.stack([jnp.where((chip == k) & (cc == 0), conv, 0.0) for k in range(N_CHIPS)])
    conv_all = _unpack(_all_reduce_small(_pack([slots], _piece_rows(slots.shape)), name="gather_conv"), [slots.shape])[0]
    conv_full = jnp.concatenate([conv_all[k] for k in range(N_CHIPS)], axis=2)

    p = _compute_form(full, conv_full, ffn_w_gu, ffn_w_down)
    cos_t, sin_t = _rope_tables(positions[0])
    loss_tile, d_x, g = _local_step(x[0], mem[0], cos_t, sin_t, loss_target[0], p)
    gn = _natural_grads(g)

    def shard_of(a, ax, k):
        size = a.shape[ax] // N_CHIPS
        return lax.slice_in_dim(a, k * size, (k + 1) * size, axis=ax)

    grads, deltas, new_m, new_v = {}, {}, {}, {}
    for names, key in ((_FFN_GU, "ffn_w_gu"), (_FFN_DOWN, "ffn_w_down")):
        buf = g[key]
        reduced = _reduce_over_devices(buf.reshape(N_CHIPS, -1, buf.shape[-1]), cc, chip, name="grad_" + key)
        first = 0
        for n in names:
            d_, m_, v_, g_ = _adamw_at(flat(w[n]), flat(m[n]), flat(v[n]), reduced, first, name="adamw_" + n)
            grads[n], deltas[n], new_m[n], new_v[n] = (t.reshape(w[n].shape) for t in (g_, d_, m_, v_))
            first += flat(w[n]).shape[0]
    g_pack = jnp.stack([_pack([shard_of(gn[n], ax, k) for n, ax in _BIG], rows) for k in range(N_CHIPS)])
    g_big = _reduce_over_devices(g_pack, cc, chip, name="grad_misc")
    d_big, m_big, v_big = _adamw(w_pack, g_big, _pack([m[n] for n in big], rows), _pack([v[n] for n in big], rows), name="adamw_misc")

    small_shapes = [w[n].shape for n in _SMALL]
    small_rows = _packed_rows(small_shapes)
    g_small = _all_reduce_small(_pack([gn[n] for n in _SMALL], small_rows), name="grad_small")
    d_small, m_small, v_small = _adamw(_pack([w[n] for n in _SMALL], small_rows), g_small, _pack([m[n] for n in _SMALL], small_rows),
                                       _pack([v[n] for n in _SMALL], small_rows), name="adamw_small")

    for out, big_pack, small_pack in ((grads, g_big, g_small), (deltas, d_big, d_small), (new_m, m_big, m_small), (new_v, v_big, v_small)):
        out.update(zip(big, _unpack(big_pack, shard_shapes)))
        out.update(zip(_SMALL, _unpack(small_pack, small_shapes)))
    loss = lax.psum(loss_tile[0, 0], ("x", "y", "c"))
    return (loss, d_x[None], *[grads[n] for n in _WEIGHTS], *[deltas[n] for n in _WEIGHTS], *[new_m[n] for n in _WEIGHTS],
            *[new_v[n] for n in _WEIGHTS])


def kernel(x, mem, positions, ffn1_norm, ffn1_w_gu, ffn1_w_down, mix_norm, ffn2_norm, ffn2_w_gu, ffn2_w_down, w_out, mem_norm, w_mem_kv, a_w_in, a_conv, a_A_log, a_dt_bias, a_out_norm, b_w_in, b_q_norm, b_w_uq, kv_in_norm, w_dkv, kv_lat_norm, w_ukv, final_norm, loss_target, m_ffn1_norm, m_ffn1_w_gu, m_ffn1_w_down, m_mix_norm, m_ffn2_norm, m_ffn2_w_gu, m_ffn2_w_down, m_w_out, m_mem_norm, m_w_mem_kv, m_a_w_in, m_a_conv, m_a_A_log, m_a_dt_bias, m_a_out_norm, m_b_w_in, m_b_q_norm, m_b_w_uq, m_kv_in_norm, m_w_dkv, m_kv_lat_norm, m_w_ukv, m_final_norm, v_ffn1_norm, v_ffn1_w_gu, v_ffn1_w_down, v_mix_norm, v_ffn2_norm, v_ffn2_w_gu, v_ffn2_w_down, v_w_out, v_mem_norm, v_w_mem_kv, v_a_w_in, v_a_conv, v_a_A_log, v_a_dt_bias, v_a_out_norm, v_b_w_in, v_b_q_norm, v_b_w_uq, v_kv_in_norm, v_w_dkv, v_kv_lat_norm, v_w_ukv, v_final_norm):
    given = dict(locals())
    w = {n: given[n] for n in _WEIGHTS}
    m = {n: given["m_" + n] for n in _WEIGHTS}
    v = {n: given["v_" + n] for n in _WEIGHTS}
    return _step(x, mem, positions, loss_target, w, m, v)
```

```python
import functools
import math

import jax
import jax.numpy as jnp
from jax import lax
from jax.experimental import pallas as pl
from jax.experimental.pallas import tpu as pltpu

_BF = jnp.bfloat16
_F32 = jnp.float32
_HI = lax.Precision.HIGHEST
_MESH = pl.DeviceIdType.MESH

D_MODEL = 1024
DEPTH = 4
N_A = 2
N_B = 2
CHUNK = 64
EPS = 1e-6
HEADS = 6
HEAD_DIM = 128
A_WIDTH = HEADS * HEAD_DIM
CONV_K = 4
QK_ROPE = 64
Q_LORA = 256
KV_LORA = 256
N_MEM = 256
MEM_HEADS = 4
MEM_HEAD_DIM = 64
MEM_WIDTH = MEM_HEADS * MEM_HEAD_DIM
D_FF = 2816
ROPE_THETA = 10000.0
A_IN = 4 * A_WIDTH + 2 * HEADS + MEM_WIDTH
A_IN_PAD = 3456
UQ_PAD = 2 * A_WIDTH
DKV_PAD = KV_LORA + 128
LANE = 128
SUBLANE = 8

ADAM_LR = 0.001
ADAM_B1 = 0.9
ADAM_B2 = 0.999
ADAM_EPS = 1e-08
ADAM_WD = 0.01
ADAM_STEP = 10

N_CHIPS = 4
N_DEV = 8
PACK_COLS = 1024
PACK_ROW_TILE = 256


def _pcall(body, **kw):
    return pl.pallas_call(body, **kw)


VMEM_LIMIT_V7X = 48 * 2 ** 20
TILE_BYTES = 6 * 2 ** 20


def _cp(grid_rank):
    return pltpu.CompilerParams(dimension_semantics=("arbitrary",) * grid_rank, vmem_limit_bytes=VMEM_LIMIT_V7X)


def _fit_rows(rows, row_bytes):
    while rows > LANE and rows * row_bytes > TILE_BYTES:
        rows //= 2
    return rows


def _fit_cols(n, target, col_bytes):
    return _tile(n, max(LANE, min(target, TILE_BYTES // col_bytes)))


def _tile(n, target):
    best = None
    for t in range(LANE, min(n, target) + 1, LANE):
        if n % t == 0:
            best = t
    return best if best is not None else n


def _dot(a, b):
    return jnp.dot(a.astype(_BF), b.astype(_BF), preferred_element_type=_F32)


def _dot_nt(a, b):
    return lax.dot_general(a.astype(_BF), b.astype(_BF), (((1,), (1,)), ((), ())), preferred_element_type=_F32)


def _dot_tn(a, b):
    return lax.dot_general(a.astype(_BF), b.astype(_BF), (((0,), (0,)), ((), ())), preferred_element_type=_F32)


def _dot_hi(a, b):
    return jnp.dot(a, b, precision=_HI, preferred_element_type=_F32)


def _rowmap(fn, tiles, params, outs, accs=(), *, tm, name, carry=(), reverse=False):
    rows = tiles[0][0].shape[0]
    steps = rows // tm
    nt, npar, no, na, nc = len(tiles), len(params), len(outs), len(accs), len(carry)

    def step_index(i):
        return steps - 1 - i if reverse else i

    in_specs, operands = [], []
    for arr, r, w, cb, first in tiles:
        r = tm if r is None else r
        w = arr.shape[1] if w is None else w
        assert arr.shape[0] >= (first + steps) * r and (w % LANE == 0 or w == arr.shape[1]), (name, arr.shape, r, w)
        in_specs.append(pl.BlockSpec((r, w), functools.partial(lambda i, cb, first: (first + step_index(i), cb), cb=cb, first=first)))
        operands.append(arr)
    for p in params:
        in_specs.append(pl.BlockSpec(p.shape, functools.partial(lambda i, nd: (0,) * nd, nd=p.ndim)))
        operands.append(p)
    out_specs, out_shape = [], []
    for r, cols, dt in outs:
        r = tm if r is None else r
        out_specs.append(pl.BlockSpec((r, cols), lambda i: (step_index(i), 0)))
        out_shape.append(jax.ShapeDtypeStruct((steps * r, cols), dt))
    for shp, dt in accs:
        out_specs.append(pl.BlockSpec(shp, functools.partial(lambda i, nd: (0,) * nd, nd=len(shp))))
        out_shape.append(jax.ShapeDtypeStruct(shp, dt))

    def body(*refs):
        t_refs = refs[:nt]
        p_refs = refs[nt:nt + npar]
        o_refs = refs[nt + npar:nt + npar + no]
        a_refs = refs[nt + npar + no:nt + npar + no + na]
        c_refs = refs[nt + npar + no + na:]
        if na or nc:
            @pl.when(pl.program_id(0) == 0)
            def _():
                for r in a_refs + c_refs:
                    r[...] = jnp.zeros(r.shape, r.dtype)
        vals = fn(*[r[...] for r in t_refs], *[r[...] for r in p_refs], *[r[...] for r in c_refs])
        vals = tuple(vals) if isinstance(vals, (tuple, list)) else (vals,)
        assert len(vals) == no + na + nc, (name, len(vals), no, na, nc)
        for r, v in zip(o_refs, vals[:no]):
            r[...] = v.astype(r.dtype)
        for r, v in zip(a_refs, vals[no:no + na]):
            r[...] += v.astype(r.dtype)
        for r, v in zip(c_refs, vals[no + na:]):
            r[...] = v.astype(r.dtype)

    res = _pcall(
        body, name=name, grid=(steps,), in_specs=in_specs, out_specs=out_specs, out_shape=out_shape,
        scratch_shapes=[pltpu.VMEM(shp, dt) for shp, dt in carry],
        compiler_params=_cp(1),
    )(*operands)
    return res


def _t(arr, width=None, cb=0, rows=None, first=0):
    return (arr, rows, width, cb, first)


def _mm_nn(a, b, *, out_dtype, name, scale=None, res=None, tm=1024, tn=1536):
    m, k = a.shape
    n = b.shape[1]
    tm, tn = _fit_rows(min(tm, m), k * a.dtype.itemsize), _fit_cols(n, tn, k * b.dtype.itemsize)

    def body(a_ref, b_ref, *rest):
        acc = _dot(a_ref[...], b_ref[...])
        if scale is not None:
            acc = acc * scale
        if res is not None:
            acc = acc + rest[0][...]
        rest[-1][...] = acc.astype(rest[-1].dtype)

    in_specs = [pl.BlockSpec((tm, k), lambda i, j: (i, 0)), pl.BlockSpec((k, tn), lambda i, j: (0, j))]
    operands = [a, b]
    if res is not None:
        in_specs.append(pl.BlockSpec((tm, tn), lambda i, j: (i, j)))
        operands.append(res)
    return _pcall(
        body, name=name, grid=(m // tm, n // tn), in_specs=in_specs,
        out_specs=pl.BlockSpec((tm, tn), lambda i, j: (i, j)), out_shape=jax.ShapeDtypeStruct((m, n), out_dtype),
        compiler_params=_cp(2),
    )(*operands)


def _mm_nt(a, b, *, out_dtype, name, scale=None, tm=1024, tn=1536):
    m, k = a.shape
    n = b.shape[0]
    tm, tn = _fit_rows(min(tm, m), k * a.dtype.itemsize), _fit_cols(n, tn, k * b.dtype.itemsize)

    def body(a_ref, b_ref, o_ref):
        acc = _dot_nt(a_ref[...], b_ref[...])
        if scale is not None:
            acc = acc * scale
        o_ref[...] = acc.astype(o_ref.dtype)

    return _pcall(
        body, name=name, grid=(m // tm, n // tn),
        in_specs=[pl.BlockSpec((tm, k), lambda i, j: (i, 0)), pl.BlockSpec((tn, k), lambda i, j: (j, 0))],
        out_specs=pl.BlockSpec((tm, tn), lambda i, j: (i, j)), out_shape=jax.ShapeDtypeStruct((m, n), out_dtype),
        compiler_params=_cp(2),
    )(a, b)


def _mm_tn(a, b, *, name, scale=None, t1=1024, tn=1536, ts=1024):
    s, k1 = a.shape
    n = b.shape[1]
    t1, tn, ts = _tile(k1, t1), _tile(n, tn), min(ts, s)
    steps = s // ts

    def body(a_ref, b_ref, o_ref):
        @pl.when(pl.program_id(2) == 0)
        def _():
            o_ref[...] = jnp.zeros(o_ref.shape, o_ref.dtype)

        o_ref[...] += _dot_tn(a_ref[...], b_ref[...])
        if scale is not None:
            @pl.when(pl.program_id(2) == steps - 1)
            def _():
                o_ref[...] = o_ref[...] * scale

    return _pcall(
        body, name=name, grid=(k1 // t1, n // tn, steps),
        in_specs=[pl.BlockSpec((ts, t1), lambda i, j, r: (r, i)), pl.BlockSpec((ts, tn), lambda i, j, r: (r, j))],
        out_specs=pl.BlockSpec((t1, tn), lambda i, j, r: (i, j)), out_shape=jax.ShapeDtypeStruct((k1, n), _F32),
        compiler_params=_cp(3),
    )(a, b)


def _heads(t, n, w=LANE):
    return [t[:, w * h:w * (h + 1)] for h in range(n)]


def _cat(parts):
    return jnp.concatenate(parts, axis=1)


def _rms(x, g):
    return x * lax.rsqrt(jnp.mean(x * x, axis=-1, keepdims=True) + EPS) * g


def _l2n(x):
    return x * lax.rsqrt(jnp.sum(x * x, axis=-1, keepdims=True) + EPS)


def _sigmoid(x):
    return 0.5 * (jnp.tanh(0.5 * x) + 1.0)


def _silu(x):
    return x * _sigmoid(x)


def _softplus(x):
    return jnp.maximum(x, 0.0) + jnp.log(1.0 + jnp.exp(-jnp.abs(x)))


def _lane_pick(t, h):
    lane = lax.broadcasted_iota(jnp.int32, t.shape, 1)
    return jnp.sum(jnp.where(lane == h, t, 0.0), axis=1, keepdims=True)


def _lane_put(col, h, width=LANE):
    lane = lax.broadcasted_iota(jnp.int32, (col.shape[0], width), 1)
    return jnp.where(lane == h, col, 0.0)


def _vjp(fwd, ins, cts):
    outs, pull = jax.vjp(fwd, *ins)
    outs = outs if isinstance(outs, (tuple, list)) else (outs,)
    cts = tuple(c.astype(o.dtype) for c, o in zip(cts, outs))
    return pull(cts if len(cts) > 1 else cts[0])


def _rot_half_matrix():
    r = lax.broadcasted_iota(jnp.int32, (LANE, LANE), 0)
    c = lax.broadcasted_iota(jnp.int32, (LANE, LANE), 1)
    half = QK_ROPE // 2
    return jnp.where((c < half) & (r == c + half), -1.0, jnp.where((c >= half) & (c < QK_ROPE) & (r == c - half), 1.0, 0.0))


def _rope(x, cos_t, sin_t):
    return x * cos_t + _dot_hi(x, _rot_half_matrix()) * sin_t


def _mem_attn(q, km, vm):
    lane_q = lax.broadcasted_iota(jnp.int32, q.shape, 1)
    lane_v = lax.broadcasted_iota(jnp.int32, vm.shape, 1)
    out = jnp.zeros(q.shape, _F32)
    for h in range(MEM_HEADS):
        lo, hi = MEM_HEAD_DIM * h, MEM_HEAD_DIM * (h + 1)
        qh = jnp.where((lane_q >= lo) & (lane_q < hi), q, 0.0)
        vh = jnp.where((lane_v >= lo) & (lane_v < hi), vm, 0.0)
        sc = _dot_nt(qh, km) * MEM_HEAD_DIM ** -0.5
        sc = sc - lax.stop_gradient(jnp.max(sc, axis=-1, keepdims=True))
        p = jnp.exp(sc)
        p = p / jnp.sum(p, axis=-1, keepdims=True)
        out = out + _dot(p, vh)
    return out


_PAIR = 2 * CHUNK


def _pair_masks():
    ri = lax.broadcasted_iota(jnp.int32, (_PAIR, _PAIR), 0)
    ci = lax.broadcasted_iota(jnp.int32, (_PAIR, _PAIR), 1)
    same = (ri >= CHUNK) == (ci >= CHUNK)
    return same, same & (ri >= ci), same & (ri > ci), ri == ci, same & (ri <= ci)


_NN = (((2,), (1,)), ((0,), (0,)))
_NT = (((2,), (2,)), ((0,), (0,)))
_TN = (((1,), (1,)), ((0,), (0,)))


def _bdot(a, b, dims):
    return lax.dot_general(a.astype(_BF), b.astype(_BF), dims, preferred_element_type=_F32)


def _dot3(a, b, dims):
    a_hi, b_hi = a.astype(_BF), b.astype(_BF)
    a_lo, b_lo = (a - a_hi.astype(_F32)).astype(_BF), (b - b_hi.astype(_F32)).astype(_BF)
    d = lambda x, y: lax.dot_general(x, y, dims, preferred_element_type=_F32)
    return d(a_hi, b_hi) + (d(a_hi, b_lo) + d(a_lo, b_hi))


@jax.custom_vjp
def _mm3(a, b):
    return _dot3(a, b, _NN)


_mm3.defvjp(lambda a, b: (_dot3(a, b, _NN), (a, b)), lambda res, g: (_dot3(g, res[1], _NT), _dot3(res[0], g, _TN)))


def _neumann_inverse(a):
    eye = jnp.where(_pair_masks()[3], 1.0, 0.0)
    n = -a
    t_inv = eye + n
    for _ in range(5):
        n = _dot3(n, n, _NN)
        t_inv = t_inv + _dot3(t_inv, n, _NN)
    return t_inv


@jax.custom_vjp
def _unit_lower_inverse(a):
    return _neumann_inverse(a)


def _unit_lower_inverse_fwd(a):
    t_inv = _neumann_inverse(a)
    return t_inv, t_inv


def _unit_lower_inverse_bwd(t_inv, g):
    return (-_dot3(t_inv, _dot3(g, t_inv, _NT), _TN),)


_unit_lower_inverse.defvjp(_unit_lower_inverse_fwd, _unit_lower_inverse_bwd)


def _gdn_intra_head(q, k, v, beta, gl):
    same, causal, strict, eye, upper = _pair_masks()
    gl_row = jnp.sum(jnp.where(eye, gl, 0.0), axis=-2, keepdims=True)
    g_col = jnp.sum(jnp.where(causal, gl_row, 0.0), axis=-1, keepdims=True)
    g_row = jnp.sum(jnp.where(upper, gl, 0.0), axis=-2, keepdims=True)
    g_last = jnp.sum(jnp.where(same, gl_row, 0.0), axis=-1, keepdims=True)
    decay = jnp.where(causal, jnp.exp(jnp.where(causal, g_col - g_row, 0.0)), 0.0)
    kb = k * beta
    a = jnp.where(strict, _bdot(kb, k, _NT) * decay, 0.0)
    t_inv = _unit_lower_inverse(a)
    e_g = jnp.exp(g_col)
    u = _mm3(t_inv, v * beta)
    w = _mm3(t_inv, kb * e_g)
    qk = _bdot(q, k, _NT) * decay
    return w, u, q * e_g, k * jnp.exp(g_last - g_col), qk, jnp.exp(g_last)


def _gdn_scan_head(s, qd_a, kd_a, w_a, u_a, qk_a, dc_a, qd_b, kd_b, w_b, u_b, qk_b, dc_b):
    zeros = jnp.zeros((HEADS, CHUNK, HEAD_DIM), _F32)
    vn_a = u_a - _bdot(w_a, s, _NN)
    o_a = _bdot(qd_a, s, _NN) + _bdot(qk_a, jnp.concatenate([vn_a, zeros], axis=1), _NN)
    s1 = s * dc_a + _bdot(kd_a, vn_a, _TN)
    vn_b = u_b - _bdot(w_b, s1, _NN)
    o_b = _bdot(qd_b, s1, _NN) + _bdot(qk_b, jnp.concatenate([zeros, vn_b], axis=1), _NN)
    s2 = s1 * dc_b + _bdot(kd_b, vn_b, _TN)
    return o_a, o_b, s2


def _pick_scalar(t, row, lane_i):
    ri = lax.broadcasted_iota(jnp.int32, t.shape, 0)
    ci = lax.broadcasted_iota(jnp.int32, t.shape, 1)
    return jnp.sum(jnp.sum(jnp.where((ri == row) & (ci == lane_i), t, 0.0), axis=1, keepdims=True), axis=0, keepdims=True)


def _put_scalar(val, row, lane_i, shape):
    ri = lax.broadcasted_iota(jnp.int32, shape, 0)
    ci = lax.broadcasted_iota(jnp.int32, shape, 1)
    return jnp.where((ri == row) & (ci == lane_i), val, 0.0)


def _by_head(t):
    return jnp.stack(_heads(t, HEADS))


def _from_heads(t):
    return _cat([t[h] for h in range(HEADS)])


def _state_by_head(s):
    return jnp.stack([s[HEAD_DIM * h:HEAD_DIM * (h + 1), :] for h in range(HEADS)])


def _scan_ins(qd, kd, w, u, qk, dcrow, state):
    ins = [_state_by_head(state)]
    for r0 in (0, CHUNK):
        rs = slice(r0, r0 + CHUNK)
        ins += [_by_head(t[rs, :]) for t in (qd, kd, w, u, qk)]
        ins.append(jnp.stack([_pick_scalar(dcrow, r0, h) for h in range(HEADS)]))
    return ins


def _gdn_scan_fwd_fn(qd, kd, w, u, qk, dcrow, state):
    o_a, o_b, s2 = _gdn_scan_head(*_scan_ins(qd, kd, w, u, qk, dcrow, state))
    return jnp.concatenate([_from_heads(o_a), _from_heads(o_b)], axis=0), state, s2.reshape(state.shape)


def _gdn_scan_bwd_fn(qd, kd, w, u, qk, dcrow, state, d_o, d_state):
    cts = (_by_head(d_o[0:CHUNK, :]), _by_head(d_o[CHUNK:_PAIR, :]), _state_by_head(d_state))
    g = _vjp(_gdn_scan_head, _scan_ins(qd, kd, w, u, qk, dcrow, state), cts)
    grads = tuple(jnp.concatenate([_from_heads(g[1 + t]), _from_heads(g[7 + t])], axis=0) for t in range(5))
    d_dcrow = sum(_put_scalar(g[6][h], 0, h, dcrow.shape) + _put_scalar(g[12][h], CHUNK, h, dcrow.shape) for h in range(HEADS))
    return grads + (d_dcrow, g[0].reshape(state.shape))


def _gdn_intra_ins(q, k, v, bg):
    return [_by_head(q), _by_head(k), _by_head(v), jnp.stack([_lane_pick(bg, h) for h in range(HEADS)]),
            jnp.stack([_lane_pick(bg, HEADS + h) for h in range(HEADS)])]


def _gdn_intra_fwd_fn(q, k, v, bg):
    res = _gdn_intra_head(*_gdn_intra_ins(q, k, v, bg))
    dcrow = sum(_lane_put(res[5][h], h) for h in range(HEADS))
    return tuple(_from_heads(r) for r in res[:5]) + (dcrow,)


def _gdn_intra_bwd_fn(q, k, v, bg, d_w, d_u, d_qd, d_kd, d_qk, d_dcrow):
    cts = tuple(_by_head(d) for d in (d_w, d_u, d_qd, d_kd, d_qk)) + (jnp.stack([_lane_pick(d_dcrow, h) for h in range(HEADS)]),)
    g = _vjp(_gdn_intra_head, _gdn_intra_ins(q, k, v, bg), cts)
    d_bg = sum(_lane_put(g[3][h], h) + _lane_put(g[4][h], HEADS + h) for h in range(HEADS))
    return tuple(_from_heads(g[t]) for t in range(3)) + (d_bg,)


def _gdn_gates(ba, alog, dtb):
    lane = lax.broadcasted_iota(jnp.int32, ba.shape, 1)
    beta = _sigmoid(ba)
    g = -jnp.exp(alog) * _softplus(ba + dtb)
    return jnp.where(lane < HEADS, beta, jnp.where(lane < 2 * HEADS, g, 0.0))


def _gdn_q_head(c):
    return _l2n(_silu(c)) * HEAD_DIM ** -0.5


def _gdn_k_head(c):
    return _l2n(_silu(c))


def _gdn_prep_fwd_fn(qkv_c, ba, alog, dtb):
    hs = _heads(qkv_c, 3 * HEADS)
    q = _cat([_gdn_q_head(c) for c in hs[:HEADS]])
    k = _cat([_gdn_k_head(c) for c in hs[HEADS:2 * HEADS]])
    v = _cat([_silu(c) for c in hs[2 * HEADS:]])
    return q, k, v, _gdn_gates(ba, alog, dtb)


def _gdn_prep_bwd_fn(qkv_c, ba, d_q, d_k, d_v, d_bg, alog, dtb):
    hs = _heads(qkv_c, 3 * HEADS)
    dqs, dks, dvs = _heads(d_q, HEADS), _heads(d_k, HEADS), _heads(d_v, HEADS)
    parts = [_vjp(_gdn_q_head, [hs[h]], (dqs[h],))[0] for h in range(HEADS)]
    parts += [_vjp(_gdn_k_head, [hs[HEADS + h]], (dks[h],))[0] for h in range(HEADS)]
    parts += [_vjp(_silu, [hs[2 * HEADS + h]], (dvs[h],))[0] for h in range(HEADS)]
    d_ba, d_alog, d_dtb = _vjp(_gdn_gates, [ba, alog, dtb], (d_bg,))
    return _cat(parts), d_ba, d_alog, d_dtb


def _a_out_head(o, gate, gain):
    return _rms(o, gain) * _silu(gate)


def _a_post_fwd_fn(o, gate, qm, gain, mem_kv):
    parts = [_a_out_head(oh, gh, gain) for oh, gh in zip(_heads(o, HEADS), _heads(gate, HEADS))]
    parts.append(_mem_attn(qm, mem_kv[:, :MEM_WIDTH], mem_kv[:, MEM_WIDTH:]))
    return (_cat(parts),)


def _a_post_bwd_fn(o, gate, qm, d_cat, gain, mem_kv):
    d_os, d_gates = [], []
    d_gain = jnp.zeros(gain.shape, _F32)
    dc = _heads(d_cat, HEADS + 2)
    for h, (oh, gh) in enumerate(zip(_heads(o, HEADS), _heads(gate, HEADS))):
        g = _vjp(_a_out_head, [oh, gh, gain], (dc[h],))
        d_os.append(g[0])
        d_gates.append(g[1])
        d_gain = d_gain + g[2]
    d_qm, d_km, d_vm = _vjp(_mem_attn, [qm, mem_kv[:, :MEM_WIDTH], mem_kv[:, MEM_WIDTH:]], (d_cat[:, A_WIDTH:],))
    return _cat(d_os), _cat(d_gates + [d_qm]), d_gain, _cat([d_km, d_vm])


def _b_post_fwd_fn(o, qm, mem_kv):
    return (_cat([o.astype(_F32), _mem_attn(qm, mem_kv[:, :MEM_WIDTH], mem_kv[:, MEM_WIDTH:])]),)


def _b_post_bwd_fn(qm, d_cat_m, mem_kv):
    d_qm, d_km, d_vm = _vjp(_mem_attn, [qm, mem_kv[:, :MEM_WIDTH], mem_kv[:, MEM_WIDTH:]], (d_cat_m,))
    return d_qm, _cat([d_km, d_vm])


ATTN_SCALE = (HEAD_DIM + QK_ROPE) ** -0.5


def _rope_q_fwd_fn(qf, cos_t, sin_t):
    hs = _heads(qf, 2 * HEADS)
    return (_cat(hs[:HEADS] + [_rope(x, cos_t, sin_t) for x in hs[HEADS:]]) * ATTN_SCALE,)


def _rope_q_bwd_fn(d_qn, d_qr, cos_t, sin_t):
    f = lambda x: _rope(x, cos_t, sin_t)
    return (_cat([d_qn] + [_vjp(f, [x], (x,))[0] for x in _heads(d_qr, HEADS)]) * ATTN_SCALE,)


def _kv_prep_fwd_fn(ckr, cos_t, sin_t, gain):
    return _rms(ckr[:, :KV_LORA], gain), _rope(ckr[:, KV_LORA:], cos_t, sin_t)


def _kv_prep_bwd_fn(ckr, d_ckv, d_kr, cos_t, sin_t, gain):
    d_lat, d_gain = _vjp(_rms, [ckr[:, :KV_LORA], gain], (d_ckv,))
    f = lambda x: _rope(x, cos_t, sin_t)
    d_rope = _vjp(f, [ckr[:, KV_LORA:]], (d_kr,))[0]
    return _cat([d_lat, d_rope]), d_gain


def _rms_fwd(x, gain, *, name, tm=1024, out_dtype=_BF):
    tm = min(tm, x.shape[0])
    return _rowmap(lambda x_, g_: (_rms(x_.astype(_F32), g_),), [_t(x)], [gain], [(None, x.shape[1], out_dtype)], tm=tm, name=name)[0]


def _rms_bwd(x, d_xn, d_res, gain, *, name, tm=512):
    tm = min(tm, x.shape[0])

    def fn(x_, dxn_, *rest):
        g_ = rest[-1]
        dx, dg = _vjp(_rms, [x_.astype(_F32), g_], (dxn_.astype(_F32),))
        if d_res is not None:
            dx = dx + rest[0]
        return dx, dg

    tiles = [_t(x), _t(d_xn)] + ([_t(d_res)] if d_res is not None else [])
    return _rowmap(fn, tiles, [gain], [(None, x.shape[1], _F32)], [(gain.shape, _F32)], tm=tm, name=name)


FFN_COL_TILE = 1408
FFN_ROW_TILE = 512
FFN_DXN_ROW_TILE = 512
FFN_DOWN_ROW_TILE = 1024


def _ffn_gate_up(x, gain, w_gu, idx, *, name):
    s = x.shape[0]
    tm, tf = min(FFN_ROW_TILE, s), FFN_COL_TILE
    nf = D_FF // tf

    def body(x_ref, gain_ref, wg_ref, wu_ref, xn_ref, g_ref, u_ref, a_ref):
        @pl.when(pl.program_id(1) == 0)
        def _():
            xn_ref[...] = _rms(x_ref[...], gain_ref[...]).astype(xn_ref.dtype)

        xn = xn_ref[...]
        g, u = _dot(xn, wg_ref[...]), _dot(xn, wu_ref[...])
        g_ref[...] = g.astype(g_ref.dtype)
        u_ref[...] = u.astype(u_ref.dtype)
        a_ref[...] = (_silu(g) * u).astype(a_ref.dtype)

    col = pl.BlockSpec((tm, tf), lambda i, j: (i, j))
    wide = jax.ShapeDtypeStruct((s, D_FF), _BF)
    return _pcall(
        body, name=name, grid=(s // tm, nf),
        in_specs=[pl.BlockSpec((tm, D_MODEL), lambda i, j: (i, 0)), pl.BlockSpec((1, D_MODEL), lambda i, j: (0, 0)),
                  pl.BlockSpec((None, D_MODEL, tf), lambda i, j: (j, idx, 0)), pl.BlockSpec((None, D_MODEL, tf), lambda i, j: (nf + j, idx, 0))],
        out_specs=[pl.BlockSpec((tm, D_MODEL), lambda i, j: (i, 0)), col, col, col],
        out_shape=[jax.ShapeDtypeStruct((s, D_MODEL), _BF), wide, wide, wide],
        compiler_params=_cp(2),
    )(x, gain, w_gu, w_gu)


FFN_SHARD_ROWS = D_FF // N_CHIPS


def _w_down_specs(idx):
    return [pl.BlockSpec((None, FFN_SHARD_ROWS, D_MODEL), functools.partial(lambda i, j, q: (2 * j + q, idx, 0), q=q)) for q in (0, 1)]


def _ffn_down(a, w_down, idx, x, *, name):
    s = a.shape[0]
    tm = min(FFN_DOWN_ROW_TILE, s)

    def body(a_ref, w0_ref, w1_ref, w2_ref, w3_ref, x_ref, o_ref):
        w = jnp.concatenate([w0_ref[...], w1_ref[...], w2_ref[...], w3_ref[...]], axis=0)
        o_ref[...] = x_ref[...] + 0.5 * _dot(a_ref[...], w)

    rows = pl.BlockSpec((tm, D_MODEL), lambda i: (i, 0))
    w_specs = [pl.BlockSpec((None, FFN_SHARD_ROWS, D_MODEL), functools.partial(lambda i, k: (k, idx, 0), k=k)) for k in range(N_CHIPS)]
    return _pcall(
        body, name=name, grid=(s // tm,), in_specs=[pl.BlockSpec((tm, D_FF), lambda i: (i, 0))] + w_specs + [rows],
        out_specs=rows, out_shape=jax.ShapeDtypeStruct((s, D_MODEL), _F32), compiler_params=_cp(1),
    )(a, w_down, w_down, w_down, w_down, x)


def _ffn_fwd(x, gain, w_gu, w_down, idx, *, name):
    xn, g, u, a = _ffn_gate_up(x, gain, w_gu, idx, name=name + "_gu")
    y = _ffn_down(a, w_down, idx, x, name=name + "_down")
    return y, (x, xn, g, u, a)


def _ffn_d_gate_up(d_y, g, u, w_down, idx, *, name):
    s = d_y.shape[0]
    tm, tf = min(FFN_ROW_TILE, s), FFN_COL_TILE

    def body(dy_ref, wa_ref, wb_ref, g_ref, u_ref, dg_ref, du_ref):
        da = _dot_nt(dy_ref[...], jnp.concatenate([wa_ref[...], wb_ref[...]], axis=0)) * 0.5
        gg, uu = g_ref[...].astype(_F32), u_ref[...].astype(_F32)
        sg = _sigmoid(gg)
        dg_ref[...] = (da * uu * sg * (1.0 + gg * (1.0 - sg))).astype(dg_ref.dtype)
        du_ref[...] = (da * gg * sg).astype(du_ref.dtype)

    col = pl.BlockSpec((tm, tf), lambda i, j: (i, j))
    wide = jax.ShapeDtypeStruct((s, D_FF), _BF)
    return _pcall(
        body, name=name, grid=(s // tm, D_FF // tf),
        in_specs=[pl.BlockSpec((tm, D_MODEL), lambda i, j: (i, 0))] + _w_down_specs(idx) + [col, col],
        out_specs=[col, col], out_shape=[wide, wide], compiler_params=_cp(2),
    )(d_y, w_down, w_down, g, u)


def _ffn_d_x(d_g, d_u, w_gu, idx, x, d_y, gain, *, name):
    s = x.shape[0]
    tm, tf = min(FFN_DXN_ROW_TILE, s), FFN_COL_TILE

    def body(dg_ref, du_ref, w0_ref, w1_ref, w2_ref, w3_ref, x_ref, dy_ref, gain_ref, dx_ref, dgain_ref):
        @pl.when(pl.program_id(0) == 0)
        def _():
            dgain_ref[...] = jnp.zeros(dgain_ref.shape, _F32)

        d_xn = (_dot_nt(dg_ref[:, 0:tf], w0_ref[...]) + _dot_nt(dg_ref[:, tf:2 * tf], w1_ref[...])
                + _dot_nt(du_ref[:, 0:tf], w2_ref[...]) + _dot_nt(du_ref[:, tf:2 * tf], w3_ref[...]))
        dx, dgain = _vjp(_rms, [x_ref[...], gain_ref[...]], (d_xn,))
        dx_ref[...] = dx + dy_ref[...]
        dgain_ref[...] += dgain

    wide = pl.BlockSpec((tm, D_FF), lambda i: (i, 0))
    rows = pl.BlockSpec((tm, D_MODEL), lambda i: (i, 0))
    one = pl.BlockSpec((1, D_MODEL), lambda i: (0, 0))
    w_specs = [pl.BlockSpec((None, D_MODEL, tf), functools.partial(lambda i, k: (k, idx, 0), k=k), pipeline_mode=pl.Buffered(1))
               for k in range(N_CHIPS)]
    return _pcall(
        body, name=name, grid=(s // tm,),
        in_specs=[wide, wide] + w_specs + [rows, rows, one],
        out_specs=[rows, one], out_shape=[jax.ShapeDtypeStruct((s, D_MODEL), _F32), jax.ShapeDtypeStruct((1, D_MODEL), _F32)],
        compiler_params=_cp(1),
    )(d_g, d_u, w_gu, w_gu, w_gu, w_gu, x, d_y, gain)


def _ffn_d_w_gu(xn, d_act, into, idx, first_chip, *, name, ts=1024):
    s = xn.shape[0]
    ts = min(ts, s)
    steps = s // ts

    def body(a_ref, b_ref, into_ref, o_ref):
        @pl.when(pl.program_id(1) == 0)
        def _():
            o_ref[...] = jnp.zeros(o_ref.shape, o_ref.dtype)

        o_ref[...] += _dot_tn(a_ref[...], b_ref[...])

    return _pcall(
        body, name=name, grid=(D_FF // FFN_COL_TILE, steps),
        in_specs=[pl.BlockSpec((ts, D_MODEL), lambda j, r: (r, 0)), pl.BlockSpec((ts, FFN_COL_TILE), lambda j, r: (r, j)),
                  pl.BlockSpec(memory_space=pl.ANY)],
        out_specs=pl.BlockSpec((None, None, D_MODEL, FFN_COL_TILE), lambda j, r: (first_chip + j, idx, 0, 0)),
        out_shape=jax.ShapeDtypeStruct(into.shape, into.dtype), input_output_aliases={2: 0}, compiler_params=_cp(2),
    )(xn, d_act, into)


def _ffn_d_w_down(a, d_y, into, idx, *, name, ts=1024):
    s = a.shape[0]
    ts = min(ts, s)
    steps = s // ts

    def body(a_ref, b_ref, into_ref, o_ref):
        @pl.when(pl.program_id(1) == 0)
        def _():
            o_ref[...] = jnp.zeros(o_ref.shape, o_ref.dtype)

        part = _dot_tn(a_ref[...], b_ref[...]) * 0.5
        o_ref[0] += part[0:FFN_SHARD_ROWS, :]
        o_ref[1] += part[FFN_SHARD_ROWS:2 * FFN_SHARD_ROWS, :]

    return _pcall(
        body, name=name, grid=(D_FF // FFN_COL_TILE, steps),
        in_specs=[pl.BlockSpec((ts, FFN_COL_TILE), lambda i, r: (r, i)), pl.BlockSpec((ts, D_MODEL), lambda i, r: (r, 0)),
                  pl.BlockSpec(memory_space=pl.ANY)],
        out_specs=pl.BlockSpec((2, None, FFN_SHARD_ROWS, D_MODEL), lambda i, r: (i, idx, 0, 0)),
        out_shape=jax.ShapeDtypeStruct(into.shape, into.dtype), input_output_aliases={2: 0}, compiler_params=_cp(2),
    )(a, d_y, into)


def _ffn_bwd(d_y, saved, gain, w_gu, w_down, idx, g_gu, g_down, *, name):
    x, xn, g, u, a = saved
    d_g, d_u = _ffn_d_gate_up(d_y, g, u, w_down, idx, name=name + "_dgu")
    g_down = _ffn_d_w_down(a, d_y, g_down, idx, name=name + "_dwd")
    g_gu = _ffn_d_w_gu(xn, d_g, g_gu, idx, 0, name=name + "_dwg")
    g_gu = _ffn_d_w_gu(xn, d_u, g_gu, idx, 2, name=name + "_dwu")
    d_x, d_gain = _ffn_d_x(d_g, d_u, w_gu, idx, x, d_y, gain, name=name + "_dx")
    return d_x, d_gain, g_gu, g_down


def _conv_fwd(h, w, *, name, tm=256):
    s = h.shape[0]
    tm = min(tm, s)
    c = 3 * A_WIDTH
    halo = SUBLANE

    def body(x_ref, prev_ref, w_ref, o_ref, buf):
        i = pl.program_id(0)
        buf[0:halo, :] = jnp.where(i == 0, 0.0, prev_ref[...])
        buf[halo:halo + tm, :] = x_ref[...]
        acc = jnp.zeros((tm, c), _F32)
        for j in range(CONV_K):
            acc = acc + buf[pl.ds(halo - (CONV_K - 1) + j, tm), :] * w_ref[j:j + 1, :]
        o_ref[...] = acc

    return _pcall(
        body, name=name, grid=(s // tm,),
        in_specs=[pl.BlockSpec((tm, c), lambda i: (i, 0)),
                  pl.BlockSpec((halo, c), lambda i: (jnp.maximum(i * (tm // halo) - 1, 0), 0)),
                  pl.BlockSpec(w.shape, lambda i: (0, 0))],
        out_specs=pl.BlockSpec((tm, c), lambda i: (i, 0)), out_shape=jax.ShapeDtypeStruct((s, c), _F32),
        scratch_shapes=[pltpu.VMEM((tm + 2 * halo, c), _F32)],
        compiler_params=_cp(1),
    )(h, h, w)


def _conv_bwd(h, d_y, w, *, name, tm=256):
    s = h.shape[0]
    tm = min(tm, s)
    c = 3 * A_WIDTH
    halo = SUBLANE
    steps = s // tm

    def body(x_ref, prev_ref, dy_ref, next_ref, w_ref, dx_ref, dw_ref, xbuf, dybuf):
        i = pl.program_id(0)

        @pl.when(i == 0)
        def _():
            dw_ref[...] = jnp.zeros(dw_ref.shape, dw_ref.dtype)

        xbuf[0:halo, :] = jnp.where(i == 0, 0.0, prev_ref[...])
        xbuf[halo:halo + tm, :] = x_ref[...]
        dybuf[0:tm, :] = dy_ref[...]
        dybuf[tm:tm + halo, :] = jnp.where(i == steps - 1, 0.0, next_ref[...])
        dy = dy_ref[...]
        acc = jnp.zeros((tm, c), _F32)
        for j in range(CONV_K):
            acc = acc + dybuf[pl.ds(CONV_K - 1 - j, tm), :] * w_ref[j:j + 1, :]
            dw_ref[j:j + 1, :] += jnp.sum(dy * xbuf[pl.ds(halo - (CONV_K - 1) + j, tm), :], axis=0, keepdims=True)
        dx_ref[...] = acc.astype(dx_ref.dtype)

    return _pcall(
        body, name=name, grid=(steps,),
        in_specs=[pl.BlockSpec((tm, c), lambda i: (i, 0)),
                  pl.BlockSpec((halo, c), lambda i: (jnp.maximum(i * (tm // halo) - 1, 0), 0)),
                  pl.BlockSpec((tm, c), lambda i: (i, 0)),
                  pl.BlockSpec((halo, c), lambda i: (jnp.minimum((i + 1) * (tm // halo), s // halo - 1), 0)),
                  pl.BlockSpec(w.shape, lambda i: (0, 0))],
        out_specs=[pl.BlockSpec((tm, c), lambda i: (i, 0)), pl.BlockSpec(w.shape, lambda i: (0, 0))],
        out_shape=[jax.ShapeDtypeStruct((s, c), _BF), jax.ShapeDtypeStruct(w.shape, _F32)],
        scratch_shapes=[pltpu.VMEM((tm + 2 * halo, c), _F32), pltpu.VMEM((tm + 2 * halo, c), _F32)],
        compiler_params=_cp(1),
    )(h, h, d_y, d_y, w)


ATTN_VMEM = 56 * 2 ** 20
ATTN_Q_BLOCK = 4096
ATTN_K_SUB = 256
ATTN_BWD_BLOCK = 2048
ATTN_BWD_SUB = 512


def _chunk_mask(shape, q_axis):
    qi = lax.broadcasted_iota(jnp.int32, shape, q_axis) // CHUNK
    ki = lax.broadcasted_iota(jnp.int32, shape, 1 - q_axis) // CHUNK
    return ki <= qi


def _rows(j, t):
    return pl.ds(pl.multiple_of(j * t, t), t)


def _chunk_mask_at(shape, q_axis, q_off):
    qi = (lax.broadcasted_iota(jnp.int32, shape, q_axis) + q_off) // CHUNK
    ki = lax.broadcasted_iota(jnp.int32, shape, 1 - q_axis) // CHUNK
    return ki <= qi


def _attn_fwd(q_all, kv, kr, *, name):
    s = q_all.shape[0]
    t = min(ATTN_Q_BLOCK, s)
    tk = min(ATTN_K_SUB, t)
    nq, sub, rep = s // t, t // tk, tk // LANE

    def body(qn_ref, qr_ref, kn_ref, kr_ref, v_ref, o_ref, lse_ref, m_sc, acc_sc):
        i = pl.program_id(1)
        m_sc[...] = jnp.full(m_sc.shape, -1e30, _F32)
        acc_sc[...] = jnp.zeros(acc_sc.shape, _F32)
        ones = jnp.ones((tk, LANE), _BF)

        def block(j, first_row):
            qs = slice(first_row, t)
            rows = _rows(j, tk)
            q = _cat([qn_ref[qs, :], qr_ref[qs, :]])
            sc = _dot_nt(q, _cat([kn_ref[rows, :], kr_ref[rows, :]]))
            if first_row is not None:
                sc = jnp.where(_chunk_mask(sc.shape, 0), sc, -1e30)
            m_prev = m_sc[qs, :]
            m_new = jnp.maximum(m_prev, jnp.max(sc, axis=-1, keepdims=True))
            alpha = jnp.exp(m_prev - m_new)
            p = jnp.exp(sc - _cat([m_new] * rep))
            acc_sc[qs, :] = _cat([alpha, alpha]) * acc_sc[qs, :] + _dot(p, _cat([v_ref[rows, :], ones]))
            m_sc[qs, :] = m_new

        def step(j, carry):
            block(j, None)
            return carry

        lax.fori_loop(0, i * sub, step, 0)
        for u in range(sub):
            block(i * sub + u, u * tk)
        row_sum = acc_sc[:, LANE:2 * LANE]
        o_ref[...] = acc_sc[:, 0:LANE] / row_sum
        lse_ref[...] = m_sc[...] + jnp.log(row_sum)

    return _pcall(
        body, name=name, grid=(HEADS, nq),
        in_specs=[pl.BlockSpec((t, LANE), lambda h, i: (i, h)),
                  pl.BlockSpec((t, LANE), lambda h, i: (i, HEADS + h)),
                  pl.BlockSpec((s, LANE), lambda h, i: (0, h), pipeline_mode=pl.Buffered(1)),
                  pl.BlockSpec((s, LANE), lambda h, i: (0, 0), pipeline_mode=pl.Buffered(1)),
                  pl.BlockSpec((s, LANE), lambda h, i: (0, HEADS + h), pipeline_mode=pl.Buffered(1))],
        out_specs=[pl.BlockSpec((t, LANE), lambda h, i: (i, h)), pl.BlockSpec((t, LANE), lambda h, i: (i, h))],
        out_shape=[jax.ShapeDtypeStruct((s, A_WIDTH), _F32), jax.ShapeDtypeStruct((s, A_WIDTH), _F32)],
        scratch_shapes=[pltpu.VMEM((t, LANE), _F32), pltpu.VMEM((t, 2 * LANE), _F32)],
        compiler_params=pltpu.CompilerParams(dimension_semantics=("arbitrary", "arbitrary"), vmem_limit_bytes=ATTN_VMEM),
    )(q_all, q_all, kv, kr, kv)


def _attn_bwd_prep(o, lse, d_cat, *, name):
    s = o.shape[0]
    t = min(ATTN_BWD_SUB, s)
    nq = s // t

    def body(o_ref, lse_ref, do_ref, dob_ref, lset_ref, dlt_ref):
        for h in range(HEADS):
            sl = slice(LANE * h, LANE * (h + 1))
            rows = slice(SUBLANE * h, SUBLANE * (h + 1))
            do = do_ref[:, sl]
            dl = jnp.broadcast_to(jnp.sum(o_ref[:, sl] * do, axis=-1, keepdims=True), (t, LANE))
            dob_ref[:, sl] = do.astype(dob_ref.dtype)
            dlt_ref[rows, :] = dl.T[0:SUBLANE, :]
            lset_ref[rows, :] = lse_ref[:, sl].T[0:SUBLANE, :]

    wide = pl.BlockSpec((t, A_WIDTH), lambda i: (i, 0))
    stat = pl.BlockSpec((HEADS * SUBLANE, t), lambda i: (i, 0))
    stat_shape = jax.ShapeDtypeStruct((nq * HEADS * SUBLANE, t), _F32)
    return _pcall(
        body, name=name, grid=(nq,), in_specs=[wide, wide, wide], out_specs=[wide, stat, stat],
        out_shape=[jax.ShapeDtypeStruct((s, A_WIDTH), _BF), stat_shape, stat_shape], compiler_params=_cp(1),
    )(o, lse, d_cat)


ATTN_BWD_VMEM = 56 * 2 ** 20


def _attn_bwd(q_all, kv, kr, lse_t, delta_t, d_o, *, name):
    s = q_all.shape[0]
    t = min(ATTN_BWD_BLOCK, s)
    tq = min(ATTN_BWD_SUB, t)
    nk, sub, nqs = s // t, t // tq, s // tq

    def body(kn_ref, kr_ref, v_ref, qn_ref, qr_ref, do_ref, lset_ref, dlt_ref, dkn_ref, dv_ref, dkr_ref, dqn_ref, dqr_ref, dk_sc, dv_sc):
        h, j = pl.program_id(0), pl.program_id(1)
        dk_sc[...] = jnp.zeros(dk_sc.shape, _F32)
        dv_sc[...] = jnp.zeros(dv_sc.shape, _F32)

        @pl.when(j == 0)
        def _():
            dqn_ref[...] = jnp.zeros(dqn_ref.shape, _F32)
            dqr_ref[...] = jnp.zeros(dqr_ref.shape, _F32)

        def block(i, query_off):
            ks = slice(0, t if query_off is None else query_off + tq)
            rows = _rows(i, tq)
            stat = pl.ds(pl.multiple_of((i * HEADS + h) * SUBLANE, SUBLANE), 1)
            q = _cat([qn_ref[rows, :], qr_ref[rows, :]])
            do = do_ref[rows, :]
            k = _cat([kn_ref[ks, :], kr_ref[ks, :]])
            p = jnp.exp(_dot_nt(k, q) - lset_ref[stat, :])
            if query_off is not None:
                p = jnp.where(_chunk_mask_at(p.shape, 1, query_off), p, 0.0)
            dv_sc[ks, :] += _dot(p, do)
            ds = p * (_dot_nt(v_ref[ks, :], do) - dlt_ref[stat, :])
            dk_sc[ks, :] += _dot(ds, q)
            dq = _dot_tn(ds, k)
            dqn_ref[rows, :] += dq[:, 0:LANE]
            dqr_ref[rows, :] += dq[:, LANE:2 * LANE]

        def step(i, carry):
            block(i, None)
            return carry

        for u in range(sub):
            block(j * sub + u, u * tq)
        lax.fori_loop((j + 1) * sub, nqs, step, 0)
        dkn_ref[...] = dk_sc[:, 0:LANE]
        dkr_ref[...] = dk_sc[:, LANE:2 * LANE]
        dv_ref[...] = dv_sc[...]

    once = pl.Buffered(1)
    stats = pl.BlockSpec((nqs * HEADS * SUBLANE, tq), lambda h, j: (0, 0), pipeline_mode=once)
    per_head = lambda at: pl.BlockSpec((s, LANE), at, pipeline_mode=once)
    return _pcall(
        body, name=name, grid=(HEADS, nk),
        in_specs=[pl.BlockSpec((t, LANE), lambda h, j: (j, h)),
                  pl.BlockSpec((t, LANE), lambda h, j: (j, 0)),
                  pl.BlockSpec((t, LANE), lambda h, j: (j, HEADS + h)),
                  per_head(lambda h, j: (0, h)), per_head(lambda h, j: (0, HEADS + h)), per_head(lambda h, j: (0, h)),
                  stats, stats],
        out_specs=[pl.BlockSpec((t, LANE), lambda h, j: (j, h))] * 3 + [per_head(lambda h, j: (0, h))] * 2,
        out_shape=[jax.ShapeDtypeStruct((s, A_WIDTH), _F32)] * 5,
        scratch_shapes=[pltpu.VMEM((t, 2 * LANE), _F32), pltpu.VMEM((t, LANE), _F32)],
        compiler_params=pltpu.CompilerParams(dimension_semantics=("arbitrary", "arbitrary"), vmem_limit_bytes=ATTN_BWD_VMEM),
    )(kv, kr, kv, q_all, q_all, d_o, lse_t, delta_t)


def _final_loss(x, tgt, gain, *, name, tm=512):
    tm = min(tm, x.shape[0])

    def fn(x_, t_, g_):
        def f(xx, gg):
            err = _rms(xx, gg) - t_
            return 0.5 * jnp.sum(jnp.sum(err * err, axis=1, keepdims=True) / D_MODEL, axis=0, keepdims=True)

        loss, pull = jax.vjp(f, x_, g_)
        dx, dg = pull(jnp.ones((1, 1), _F32))
        return dx, dg, jnp.broadcast_to(loss, (SUBLANE, LANE))

    return _rowmap(fn, [_t(x), _t(tgt)], [gain], [(None, D_MODEL, _F32)], [(gain.shape, _F32), ((SUBLANE, LANE), _F32)], tm=tm, name=name)


def _local_step(x, mem, cos_t, sin_t, tgt, p):
    s = x.shape[0]
    g = {}
    row = lambda a: a.reshape(1, -1)
    tm_e = min(256, s)

    mem_n = _rms_fwd(mem, row(p["mem_norm"]), name="mem_norm")
    mem_kv_all = _mm_nn(mem_n, p["w_mem_all"], out_dtype=_F32, name="mem_kv", tn=1024)
    mem_kv = [mem_kv_all[:, 2 * MEM_WIDTH * l:2 * MEM_WIDTH * (l + 1)] for l in range(DEPTH)]

    sv = []
    for i in range(N_A):
        l = i
        r = {}
        r["x0"] = x
        x, r["ffn1"] = _ffn_fwd(x, row(p["ffn1_norm"][l]), p["ffn_w_gu"], p["ffn_w_down"], l, name=f"a{i}_ffn1")
        r["x1"] = x
        xn = _rms_fwd(x, row(p["mix_norm"][l]), name=f"a{i}_mixnorm")
        h = _mm_nn(xn, p["a_w_in"][i], out_dtype=_F32, name=f"a{i}_in", tn=1152)
        qkv_c = _conv_fwd(h, p["a_conv"][i], name=f"a{i}_conv")
        alog, dtb = p["a_A_log_row"][i], p["a_dt_bias_row"][i]
        q, k, v, bg = _rowmap(_gdn_prep_fwd_fn, [_t(qkv_c), _t(h, LANE, 26)], [alog, dtb],
                              [(None, A_WIDTH, _F32)] * 3 + [(None, LANE, _F32)], tm=tm_e, name=f"a{i}_prep")
        w_, u_, qd, kd, qk, dcrow = _rowmap(_gdn_intra_fwd_fn, [_t(q), _t(k), _t(v), _t(bg)], [],
                                            [(None, A_WIDTH, _F32)] * 5 + [(None, LANE, _F32)], tm=_PAIR, name=f"a{i}_intra")
        o, states = _rowmap(_gdn_scan_fwd_fn, [_t(qd), _t(kd), _t(w_), _t(u_), _t(qk), _t(dcrow)], [],
                            [(None, A_WIDTH, _F32), (A_WIDTH, HEAD_DIM, _F32)], tm=_PAIR, name=f"a{i}_scan",
                            carry=[((A_WIDTH, HEAD_DIM), _F32)])
        gain_o = row(p["a_out_norm"][i])
        cat = _rowmap(_a_post_fwd_fn, [_t(o), _t(h, A_WIDTH, 3), _t(h, MEM_WIDTH, 12)], [gain_o, mem_kv[l]],
                      [(None, D_MODEL, _BF)], tm=tm_e, name=f"a{i}_post")[0]
        x = _mm_nn(cat, p["w_out"][l], out_dtype=_F32, name=f"a{i}_out", res=x, tn=1024)
        r.update(xn=xn, h=h, qkv_c=qkv_c, q=q, k=k, v=v, bg=bg, w=w_, u=u_, qd=qd, kd=kd, qk=qk, dcrow=dcrow, o=o, states=states, cat=cat)
        r["x2"] = x
        x, r["ffn2"] = _ffn_fwd(x, row(p["ffn2_norm"][l]), p["ffn_w_gu"], p["ffn_w_down"], DEPTH + l, name=f"a{i}_ffn2")
        sv.append(r)

    kvs = {"x": x}
    xn_kv = _rms_fwd(x, row(p["kv_in_norm"]), name="kv_innorm")
    ckr = _mm_nn(xn_kv, p["w_dkv"], out_dtype=_F32, name="kv_down")
    ckv, k_rope = _rowmap(_kv_prep_fwd_fn, [_t(ckr), _t(cos_t), _t(sin_t)], [row(p["kv_lat_norm"])],
                          [(None, KV_LORA, _BF), (None, LANE, _BF)], tm=tm_e, name="kv_prep")
    kvu = _mm_nn(ckv, p["w_ukv"], out_dtype=_BF, name="kv_up")
    kvs.update(xn=xn_kv, ckr=ckr, ckv=ckv)

    for j in range(N_B):
        l = N_A + j
        r = {}
        x, r["ffn1"] = _ffn_fwd(x, row(p["ffn1_norm"][l]), p["ffn_w_gu"], p["ffn_w_down"], l, name=f"b{j}_ffn1")
        r["x1"] = x
        xn = _rms_fwd(x, row(p["mix_norm"][l]), name=f"b{j}_mixnorm")
        h = _mm_nn(xn, p["b_w_in"][j], out_dtype=_F32, name=f"b{j}_in")
        gain_q = row(p["b_q_norm"][j])
        cqn = _rowmap(lambda c_, g_: (_rms(c_, g_),), [_t(h, Q_LORA, 0)], [gain_q], [(None, Q_LORA, _BF)], tm=tm_e, name=f"b{j}_qnorm")[0]
        qf = _mm_nn(cqn, p["b_w_uq"][j], out_dtype=_F32, name=f"b{j}_uq")
        q_all = _rowmap(_rope_q_fwd_fn, [_t(qf), _t(cos_t), _t(sin_t)], [], [(None, UQ_PAD, _BF)], tm=tm_e, name=f"b{j}_rope")[0]
        o_b, lse = _attn_fwd(q_all, kvu, k_rope, name=f"b{j}_attn")
        cat = _rowmap(_b_post_fwd_fn, [_t(o_b), _t(h, MEM_WIDTH, 1)], [mem_kv[l]], [(None, D_MODEL, _BF)], tm=tm_e, name=f"b{j}_post")[0]
        x = _mm_nn(cat, p["w_out"][l], out_dtype=_F32, name=f"b{j}_out", res=x, tn=1024)
        r.update(xn=xn, h=h, cqn=cqn, q_all=q_all, o_b=o_b, lse=lse, cat=cat)
        x, r["ffn2"] = _ffn_fwd(x, row(p["ffn2_norm"][l]), p["ffn_w_gu"], p["ffn_w_down"], DEPTH + l, name=f"b{j}_ffn2")
        sv.append(r)

    dx, g["final_norm"], loss = _final_loss(x, tgt, row(p["final_norm"]), name="loss")

    per_layer = lambda: [None] * DEPTH
    for n in ("ffn1_norm", "mix_norm", "ffn2_norm", "w_out", "mem_kv"):
        g[n] = per_layer()
    g["ffn_w_gu"] = jnp.zeros((N_CHIPS, 2 * DEPTH, D_MODEL, FFN_COL_TILE), _F32)
    g["ffn_w_down"] = jnp.zeros((N_CHIPS, 2 * DEPTH, FFN_SHARD_ROWS, D_MODEL), _F32)
    for n in ("a_w_in", "a_conv", "a_A_log_row", "a_dt_bias_row", "a_out_norm", "b_w_in", "b_q_norm", "b_w_uq"):
        g[n] = [None] * N_A
    d_kv_parts = []

    for j in reversed(range(N_B)):
        l = N_A + j
        r = sv[l]
        dx, g["ffn2_norm"][l], g["ffn_w_gu"], g["ffn_w_down"] = _ffn_bwd(
            dx, r["ffn2"], row(p["ffn2_norm"][l]), p["ffn_w_gu"], p["ffn_w_down"], DEPTH + l, g["ffn_w_gu"], g["ffn_w_down"], name=f"b{j}_ffn2b")
        d_cat = _mm_nt(dx, p["w_out"][l], out_dtype=_F32, name=f"b{j}_dcat", tn=1024)
        g["w_out"][l] = _mm_tn(r["cat"], dx, name=f"b{j}_dwout", tn=1024)
        d_qm, g["mem_kv"][l] = _rowmap(_b_post_bwd_fn, [_t(r["h"], MEM_WIDTH, 1), _t(d_cat, MEM_WIDTH, 3)], [mem_kv[l]],
                                      [(None, MEM_WIDTH, _BF)], [((N_MEM, 2 * MEM_WIDTH), _F32)], tm=tm_e, name=f"b{j}_postb")
        d_o, lse_t, delta_t = _attn_bwd_prep(r["o_b"], r["lse"], d_cat, name=f"b{j}_delta")
        dkn, dv, dkr, dqn, dqr = _attn_bwd(r["q_all"], kvu, k_rope, lse_t, delta_t, d_o, name=f"b{j}_attn_bwd")
        d_kv_parts.append((dkn, dv, dkr))
        d_qf = _rowmap(_rope_q_bwd_fn, [_t(dqn), _t(dqr), _t(cos_t), _t(sin_t)], [], [(None, UQ_PAD, _F32)], tm=tm_e, name=f"b{j}_ropeb")[0]
        d_cqn = _mm_nt(d_qf, p["b_w_uq"][j], out_dtype=_F32, name=f"b{j}_dcqn")
        g["b_w_uq"][j] = _mm_tn(r["cqn"], d_qf, name=f"b{j}_dwuq")
        gain_q = row(p["b_q_norm"][j])
        d_cq, g["b_q_norm"][j] = _rowmap(lambda c_, d_, g_: _vjp(_rms, [c_, g_], (d_,)), [_t(r["h"], Q_LORA, 0), _t(d_cqn)], [gain_q],
                                        [(None, Q_LORA, _BF)], [((1, Q_LORA), _F32)], tm=tm_e, name=f"b{j}_qnormb")
        d_h = jnp.concatenate([d_cq, d_qm], axis=1)
        d_xn = _mm_nt(d_h, p["b_w_in"][j], out_dtype=_F32, name=f"b{j}_dxn", tn=1024)
        g["b_w_in"][j] = _mm_tn(r["xn"], d_h, name=f"b{j}_dwin")
        dx, g["mix_norm"][l] = _rms_bwd(r["x1"], d_xn, dx, row(p["mix_norm"][l]), name=f"b{j}_mixnormb")
        dx, g["ffn1_norm"][l], g["ffn_w_gu"], g["ffn_w_down"] = _ffn_bwd(
            dx, r["ffn1"], row(p["ffn1_norm"][l]), p["ffn_w_gu"], p["ffn_w_down"], l, g["ffn_w_gu"], g["ffn_w_down"], name=f"b{j}_ffn1b")

    def kv_sum(*parts):
        dkn = sum(parts[0::3][1:], parts[0])
        dv = sum(parts[1::3][1:], parts[1])
        dkr = sum(parts[2::3][1:], parts[2])
        return _cat([dkn, dv]), sum(_heads(dkr, HEADS)[1:], _heads(dkr, HEADS)[0])

    d_kvu, d_kr = _rowmap(kv_sum, [_t(a) for part in d_kv_parts for a in part], [], [(None, 2 * A_WIDTH, _F32), (None, LANE, _F32)],
                          tm=tm_e, name="kv_dsum")
    d_ckv = _mm_nt(d_kvu, p["w_ukv"], out_dtype=_F32, name="kv_dckv")
    g["w_ukv"] = _mm_tn(kvs["ckv"], d_kvu, name="kv_dwukv")
    d_ckr, g["kv_lat_norm"] = _rowmap(_kv_prep_bwd_fn, [_t(kvs["ckr"]), _t(d_ckv), _t(d_kr), _t(cos_t), _t(sin_t)], [row(p["kv_lat_norm"])],
                                     [(None, DKV_PAD, _F32)], [((1, KV_LORA), _F32)], tm=tm_e, name="kv_prepb")
    d_xn = _mm_nt(d_ckr, p["w_dkv"], out_dtype=_F32, name="kv_dxn", tn=1024)
    g["w_dkv"] = _mm_tn(kvs["xn"], d_ckr, name="kv_dwdkv")
    dx, g["kv_in_norm"] = _rms_bwd(kvs["x"], d_xn, dx, row(p["kv_in_norm"]), name="kv_innormb")

    for i in reversed(range(N_A)):
        l = i
        r = sv[l]
        dx, g["ffn2_norm"][l], g["ffn_w_gu"], g["ffn_w_down"] = _ffn_bwd(
            dx, r["ffn2"], row(p["ffn2_norm"][l]), p["ffn_w_gu"], p["ffn_w_down"], DEPTH + l, g["ffn_w_gu"], g["ffn_w_down"], name=f"a{i}_ffn2b")
        d_cat = _mm_nt(dx, p["w_out"][l], out_dtype=_F32, name=f"a{i}_dcat", tn=1024)
        g["w_out"][l] = _mm_tn(r["cat"], dx, name=f"a{i}_dwout", tn=1024)
        gain_o = row(p["a_out_norm"][i])
        h = r["h"]
        d_o, d_hpart, g["a_out_norm"][i], g["mem_kv"][l] = _rowmap(
            _a_post_bwd_fn, [_t(r["o"]), _t(h, A_WIDTH, 3), _t(h, MEM_WIDTH, 12), _t(d_cat)], [gain_o, mem_kv[l]],
            [(None, A_WIDTH, _F32), (None, D_MODEL, _BF)], [((1, HEAD_DIM), _F32), ((N_MEM, 2 * MEM_WIDTH), _F32)], tm=tm_e, name=f"a{i}_postb")
        d_qd, d_kd, d_w, d_u, d_qk, d_dcrow = _rowmap(
            _gdn_scan_bwd_fn, [_t(r["qd"]), _t(r["kd"]), _t(r["w"]), _t(r["u"]), _t(r["qk"]), _t(r["dcrow"]), _t(r["states"], rows=A_WIDTH), _t(d_o)], [],
            [(None, A_WIDTH, _F32)] * 5 + [(None, LANE, _F32)], tm=_PAIR, name=f"a{i}_scanb", carry=[((A_WIDTH, HEAD_DIM), _F32)], reverse=True)
        d_q, d_k, d_v, d_bg = _rowmap(
            _gdn_intra_bwd_fn, [_t(r["q"]), _t(r["k"]), _t(r["v"]), _t(r["bg"]), _t(d_w), _t(d_u), _t(d_qd), _t(d_kd), _t(d_qk), _t(d_dcrow)], [],
            [(None, A_WIDTH, _F32)] * 3 + [(None, LANE, _F32)], tm=_PAIR, name=f"a{i}_intrab")
        alog, dtb = p["a_A_log_row"][i], p["a_dt_bias_row"][i]
        d_qkv_c, d_ba, g["a_A_log_row"][i], g["a_dt_bias_row"][i] = _rowmap(
            _gdn_prep_bwd_fn, [_t(r["qkv_c"]), _t(h, LANE, 26), _t(d_q), _t(d_k), _t(d_v), _t(d_bg)], [alog, dtb],
            [(None, 3 * A_WIDTH, _F32), (None, LANE, _BF)], [((1, LANE), _F32), ((1, LANE), _F32)], tm=tm_e, name=f"a{i}_prepb")
        d_qkv, g["a_conv"][i] = _conv_bwd(h, d_qkv_c, p["a_conv"][i], name=f"a{i}_convb")
        d_h = jnp.concatenate([d_qkv, d_hpart, d_ba], axis=1)
        d_xn = _mm_nt(d_h, p["a_w_in"][i], out_dtype=_F32, name=f"a{i}_dxn", tn=1024)
        g["a_w_in"][i] = _mm_tn(r["xn"], d_h, name=f"a{i}_dwin", tn=1152)
        dx, g["mix_norm"][l] = _rms_bwd(r["x1"], d_xn, dx, row(p["mix_norm"][l]), name=f"a{i}_mixnormb")
        dx, g["ffn1_norm"][l], g["ffn_w_gu"], g["ffn_w_down"] = _ffn_bwd(
            dx, r["ffn1"], row(p["ffn1_norm"][l]), p["ffn_w_gu"], p["ffn_w_down"], l, g["ffn_w_gu"], g["ffn_w_down"], name=f"a{i}_ffn1b")

    d_mem_kv_all = jnp.concatenate(g.pop("mem_kv"), axis=1)
    d_mem_n = _mm_nt(d_mem_kv_all, p["w_mem_all"], out_dtype=_F32, name="mem_dn", tn=1024)
    g["w_mem_all"] = _mm_tn(mem_n, d_mem_kv_all, name="mem_dw", tn=1024)
    _, g["mem_norm"] = _rms_bwd(mem, d_mem_n, None, row(p["mem_norm"]), name="mem_normb")
    return loss, dx, g


_NOPE_ROPE = HEAD_DIM + QK_ROPE
_QKV_GATE = 4 * A_WIDTH
_BETA_AT = _QKV_GATE + MEM_WIDTH


def _lane_row(vals, at):
    n = vals.shape[0]
    return jnp.concatenate([jnp.zeros((at,), _F32), vals.astype(_F32), jnp.zeros((LANE - at - n,), _F32)]).reshape(1, LANE)


def _compute_form(w, conv_f32, ffn_w_gu, ffn_w_down):
    p = {n: w[n] for n in ("ffn1_norm", "mix_norm", "ffn2_norm", "w_out", "mem_norm", "a_out_norm", "b_w_in", "b_q_norm", "kv_in_norm",
                           "kv_lat_norm", "final_norm")}
    p["ffn_w_gu"], p["ffn_w_down"] = ffn_w_gu, ffn_w_down
    wm = w["w_mem_kv"]
    p["w_mem_all"] = jnp.transpose(wm, (1, 0, 2)).reshape(D_MODEL, DEPTH * 2 * MEM_WIDTH)
    a = w["a_w_in"]
    pad = jnp.zeros((N_A, D_MODEL, A_IN_PAD - A_IN), a.dtype)
    p["a_w_in"] = jnp.concatenate([a[:, :, :_QKV_GATE], a[:, :, _QKV_GATE + 2 * HEADS:], a[:, :, _QKV_GATE:_QKV_GATE + 2 * HEADS], pad], axis=2)
    p["a_conv"] = jnp.concatenate([conv_f32, jnp.zeros((N_A, SUBLANE - CONV_K, 3 * A_WIDTH), _F32)], axis=1)
    p["a_A_log_row"] = [_lane_row(w["a_A_log"][i], HEADS) for i in range(N_A)]
    p["a_dt_bias_row"] = [_lane_row(w["a_dt_bias"][i], HEADS) for i in range(N_A)]
    uq = w["b_w_uq"].reshape(N_B, Q_LORA, HEADS, _NOPE_ROPE)
    rope = jnp.concatenate([uq[..., HEAD_DIM:], jnp.zeros((N_B, Q_LORA, HEADS, LANE - QK_ROPE), uq.dtype)], axis=-1)
    p["b_w_uq"] = jnp.concatenate([uq[..., :HEAD_DIM].reshape(N_B, Q_LORA, A_WIDTH), rope.reshape(N_B, Q_LORA, A_WIDTH)], axis=-1)
    dkv = w["w_dkv"]
    p["w_dkv"] = jnp.concatenate([dkv, jnp.zeros((D_MODEL, DKV_PAD - dkv.shape[1]), dkv.dtype)], axis=1)
    ukv = w["w_ukv"].reshape(KV_LORA, HEADS, 2 * HEAD_DIM)
    p["w_ukv"] = jnp.concatenate([ukv[..., :HEAD_DIM].reshape(KV_LORA, A_WIDTH), ukv[..., HEAD_DIM:].reshape(KV_LORA, A_WIDTH)], axis=-1)
    return p


def _natural_grads(g):
    st = lambda xs: jnp.stack(xs, axis=0)
    n = {}
    for k in ("ffn1_norm", "mix_norm", "ffn2_norm"):
        n[k] = st(g[k]).reshape(DEPTH, D_MODEL)
    for k in ("w_out", "b_w_in"):
        n[k] = st(g[k])
    n["mem_norm"] = g["mem_norm"].reshape(D_MODEL)
    n["w_mem_kv"] = jnp.transpose(g["w_mem_all"].reshape(D_MODEL, DEPTH, 2 * MEM_WIDTH), (1, 0, 2))
    a = st(g["a_w_in"])
    n["a_w_in"] = jnp.concatenate([a[:, :, :_QKV_GATE], a[:, :, _BETA_AT:_BETA_AT + 2 * HEADS], a[:, :, _QKV_GATE:_BETA_AT]], axis=2)
    n["a_conv"] = st(g["a_conv"])[:, :CONV_K]
    n["a_A_log"] = st(g["a_A_log_row"])[:, 0, HEADS:2 * HEADS]
    n["a_dt_bias"] = st(g["a_dt_bias_row"])[:, 0, HEADS:2 * HEADS]
    n["a_out_norm"] = st(g["a_out_norm"]).reshape(N_A, HEAD_DIM)
    n["b_q_norm"] = st(g["b_q_norm"]).reshape(N_B, Q_LORA)
    uq = st(g["b_w_uq"])
    nope = uq[:, :, :A_WIDTH].reshape(N_B, Q_LORA, HEADS, HEAD_DIM)
    rope = uq[:, :, A_WIDTH:].reshape(N_B, Q_LORA, HEADS, LANE)[..., :QK_ROPE]
    n["b_w_uq"] = jnp.concatenate([nope, rope], axis=-1).reshape(N_B, Q_LORA, HEADS * _NOPE_ROPE)
    n["kv_in_norm"] = g["kv_in_norm"].reshape(D_MODEL)
    n["w_dkv"] = g["w_dkv"][:, :KV_LORA + QK_ROPE]
    n["kv_lat_norm"] = g["kv_lat_norm"].reshape(KV_LORA)
    ukv = g["w_ukv"]
    n["w_ukv"] = jnp.concatenate([ukv[:, :A_WIDTH].reshape(KV_LORA, HEADS, HEAD_DIM), ukv[:, A_WIDTH:].reshape(KV_LORA, HEADS, HEAD_DIM)],
                                 axis=-1).reshape(KV_LORA, HEADS * 2 * HEAD_DIM)
    n["final_norm"] = g["final_norm"].reshape(D_MODEL)
    return n


def _rope_tables(positions):
    inv = ROPE_THETA ** (-jnp.arange(0, QK_ROPE, 2, dtype=_F32) / QK_ROPE)
    ang = positions.astype(_F32)[:, None] * inv
    z = jnp.zeros((positions.shape[0], LANE - QK_ROPE), _F32)
    cos, sin = jnp.cos(ang), jnp.sin(ang)
    return jnp.concatenate([cos, cos, z], axis=1), jnp.concatenate([sin, sin, z], axis=1)


_HBM = pl.BlockSpec(memory_space=pltpu.HBM)


def _place():
    x, y, c = lax.axis_index("x"), lax.axis_index("y"), lax.axis_index("c")
    return x, y, c, [(1 - x, y), (x, 1 - y), (1 - x, 1 - y)]


def _remote(src, dst, send_sem, recv_sem, to):
    return pltpu.make_async_remote_copy(src_ref=src, dst_ref=dst, send_sem=send_sem, recv_sem=recv_sem, device_id=to, device_id_type=_MESH)


def _gather_over_chips(shard, *, name):
    rows, cols = shard.shape
    half = rows // 2

    def body(w_ref, out_ref, send_sems, recv_sems):
        x, y, c, chips = _place()
        k = 2 * x + y

        def part(chip, h):
            return out_ref.at[chip, pl.ds(h * half, half), :]

        first = [_remote(w_ref.at[pl.ds(c * half, half), :], part(k, c), send_sems.at[j], recv_sems.at[j], (px, py, c))
                 for j, (px, py) in enumerate(chips)]
        for cp in first:
            cp.start()
        passed = []
        for j, (px, py) in enumerate(chips):
            got = part(2 * px + py, c)
            _remote(got, got, send_sems.at[j], recv_sems.at[j], (px, py, c)).wait_recv()
            fw = _remote(got, got, send_sems.at[3 + j], recv_sems.at[3 + j], (x, y, 1 - c))
            fw.start()
            passed.append(fw)
        for j, (px, py) in enumerate(chips):
            got = part(2 * px + py, 1 - c)
            _remote(got, got, send_sems.at[3 + j], recv_sems.at[3 + j], (x, y, 1 - c)).wait_recv()
        for cp in first + passed:
            cp.wait_send()

    others = _pcall(
        body, name=name, in_specs=[_HBM], out_specs=_HBM, out_shape=jax.ShapeDtypeStruct((N_CHIPS, rows, cols), shard.dtype),
        scratch_shapes=[pltpu.SemaphoreType.DMA((6,)), pltpu.SemaphoreType.DMA((6,))],
    )(shard)
    return lax.dynamic_update_slice(others, shard[None], (2 * lax.axis_index("x") + lax.axis_index("y"), 0, 0))


PAIR_COPIES = 4


def _scatter_over_chips(v, *, name):
    def body(v_ref, out_ref, send_sems, recv_sems):
        x, y, c, chips = _place()
        cps = [_remote(v_ref.at[2 * px + py], out_ref.at[j], send_sems.at[j], recv_sems.at[j], (px, py, c)) for j, (px, py) in enumerate(chips)]
        for cp in cps:
            cp.start()
        for cp in cps:
            cp.wait()

    return _pcall(body, name=name, in_specs=[_HBM], out_specs=_HBM, out_shape=jax.ShapeDtypeStruct((N_CHIPS - 1,) + v.shape[1:], v.dtype),
                  scratch_shapes=[pltpu.SemaphoreType.DMA((3,)), pltpu.SemaphoreType.DMA((3,))])(v)


def _all_reduce_small(v, *, name):
    def body(v_ref, out_ref, all_ref, send_sems, recv_sems):
        x, y, c, _ = _place()
        me = 4 * x + 2 * y + c
        all_ref[me] = v_ref[...]
        cps = []
        for f in range(1, N_DEV):
            fx, fy, fc = (f >> 2) & 1, (f >> 1) & 1, f & 1
            to = (x + fx - 2 * x * fx, y + fy - 2 * y * fy, c + fc - 2 * c * fc)
            cps.append(_remote(v_ref, all_ref.at[me], send_sems.at[f - 1], recv_sems.at[f - 1], to))
        for cp in cps:
            cp.start()
        for cp in cps:
            cp.wait()
        acc = all_ref[0]
        for d in range(1, N_DEV):
            acc = acc + all_ref[d]
        out_ref[...] = acc

    vm = pl.BlockSpec(memory_space=pltpu.VMEM)
    return _pcall(body, name=name, in_specs=[vm], out_specs=vm, out_shape=jax.ShapeDtypeStruct(v.shape, v.dtype),
                  scratch_shapes=[pltpu.VMEM((N_DEV,) + v.shape, v.dtype), pltpu.SemaphoreType.DMA((N_DEV - 1,)), pltpu.SemaphoreType.DMA((N_DEV - 1,))])(v)


_FFN_GU = ("ffn1_w_gu", "ffn2_w_gu")
_FFN_DOWN = ("ffn1_w_down", "ffn2_w_down")
_BIG = (("w_out", 1), ("w_mem_kv", 1), ("a_w_in", 2), ("a_conv", 2), ("b_w_in", 1), ("b_w_uq", 2), ("w_dkv", 0),
        ("w_ukv", 1))
_SMALL = ("ffn1_norm", "mix_norm", "ffn2_norm", "mem_norm", "a_A_log", "a_dt_bias", "a_out_norm", "b_q_norm", "kv_in_norm", "kv_lat_norm",
          "final_norm")
_WEIGHTS = ("ffn1_norm", "ffn1_w_gu", "ffn1_w_down", "mix_norm", "ffn2_norm", "ffn2_w_gu", "ffn2_w_down", "w_out", "mem_norm", "w_mem_kv",
            "a_w_in", "a_conv", "a_A_log", "a_dt_bias", "a_out_norm", "b_w_in", "b_q_norm", "b_w_uq", "kv_in_norm", "w_dkv", "kv_lat_norm",
            "w_ukv", "final_norm")


PACK_PIECE_ROWS = 16


def _piece_rows(shape):
    return -(-math.prod(shape) // (PACK_COLS * PACK_PIECE_ROWS)) * PACK_PIECE_ROWS


def _packed_rows(shapes):
    return sum(_piece_rows(s) for s in shapes)


def _pack(arrs, rows):
    pieces = []
    for a in arrs:
        n, r = a.size, _piece_rows(a.shape)
        flat = a.reshape(-1)
        if r * PACK_COLS != n:
            flat = jnp.concatenate([flat, jnp.zeros((r * PACK_COLS - n,), a.dtype)])
        pieces.append(flat.reshape(r, PACK_COLS))
    used = sum(p.shape[0] for p in pieces)
    if rows > used:
        pieces.append(jnp.zeros((rows - used, PACK_COLS), arrs[0].dtype))
    return jnp.concatenate(pieces, axis=0)


def _unpack(packed, shapes):
    off, out = 0, []
    for shp in shapes:
        n, r = math.prod(shp), _piece_rows(shp)
        piece = packed[off:off + r]
        out.append((piece if r * PACK_COLS == n else piece.reshape(-1)[:n]).reshape(shp))
        off += r
    return out


def _adamw_math(w_, g_, m_, v_):
    m2 = ADAM_B1 * m_ + (1.0 - ADAM_B1) * g_
    v2 = ADAM_B2 * v_ + (1.0 - ADAM_B2) * (g_ * g_)
    m_hat = m2 / (1.0 - ADAM_B1 ** ADAM_STEP)
    v_hat = v2 / (1.0 - ADAM_B2 ** ADAM_STEP)
    return -ADAM_LR * (m_hat / (jnp.sqrt(v_hat) + ADAM_EPS) + ADAM_WD * w_), m2, v2


def _adamw(w, g, m, v, *, name):
    return _rowmap(_adamw_math, [_t(w), _t(g), _t(m), _t(v)], [], [(None, w.shape[1], _F32)] * 3, tm=min(PACK_ROW_TILE, w.shape[0]), name=name)


def _pair_exchange_half(g, *, name):
    n, _, h, cols = g.shape

    def body(g_ref, out_ref, send_sems, recv_sems):
        x, y, c, _ = _place()
        cps = [_remote(g_ref.at[k, 1 - c], out_ref.at[k], send_sems.at[k], recv_sems.at[k], (x, y, 1 - c)) for k in range(n)]
        for cp in cps:
            cp.start()
        for cp in cps:
            cp.wait()

    return _pcall(body, name=name, in_specs=[_HBM], out_specs=_HBM, out_shape=jax.ShapeDtypeStruct((n, h, cols), g.dtype),
                  scratch_shapes=[pltpu.SemaphoreType.DMA((n,)), pltpu.SemaphoreType.DMA((n,))])(g)


def _add_half(g, other, c, *, name):
    n, _, h, cols = g.shape
    tm = min(PACK_ROW_TILE, h)

    def body(c_ref, g_ref, o_ref, sum_ref, narrow_ref):
        acc = g_ref[...] + o_ref[...]
        sum_ref[...] = acc
        narrow_ref[...] = acc.astype(narrow_ref.dtype)

    slab = pl.BlockSpec((None, tm, cols), lambda k, i, c_ref: (k, i, 0))
    return _pcall(
        body, name=name,
        grid_spec=pltpu.PrefetchScalarGridSpec(
            num_scalar_prefetch=1, grid=(n, h // tm),
            in_specs=[pl.BlockSpec((None, None, tm, cols), lambda k, i, c_ref: (k, c_ref[0], i, 0)), slab], out_specs=[slab, slab]),
        out_shape=[jax.ShapeDtypeStruct((n, h, cols), _F32), jax.ShapeDtypeStruct((n, h, cols), jnp.bfloat16)],
        compiler_params=_cp(2),
    )(jnp.reshape(c, (1,)).astype(jnp.int32), g, other)


def _add_own(chip_sum, from_chips, chip, *, name):
    _, h, cols = chip_sum.shape
    tm = min(PACK_ROW_TILE, h)

    def body(k_ref, own_ref, a_ref, b_ref, c_ref, o_ref):
        o_ref[...] = ((own_ref[...] + a_ref[...].astype(_F32)) + b_ref[...].astype(_F32)) + c_ref[...].astype(_F32)

    sent = [pl.BlockSpec((None, tm, cols), functools.partial(lambda i, k_ref, j: (j, i, 0), j=j)) for j in range(N_CHIPS - 1)]
    return _pcall(
        body, name=name,
        grid_spec=pltpu.PrefetchScalarGridSpec(
            num_scalar_prefetch=1, grid=(h // tm,),
            in_specs=[pl.BlockSpec((None, tm, cols), lambda i, k_ref: (k_ref[0], i, 0))] + sent,
            out_specs=pl.BlockSpec((tm, cols), lambda i, k_ref: (i, 0))),
        out_shape=jax.ShapeDtypeStruct((h, cols), _F32), compiler_params=_cp(1),
    )(jnp.reshape(chip, (1,)).astype(jnp.int32), chip_sum, from_chips, from_chips, from_chips)


def _pair_gather(mine, *, name):
    h, cols = mine.shape
    per = h // PAIR_COPIES
    assert per * PAIR_COPIES == h and per % SUBLANE == 0, mine.shape

    def body(v_ref, out_ref, send_sems, recv_sems):
        x, y, c, _ = _place()
        cps = [_remote(v_ref.at[pl.ds(q * per, per), :], out_ref.at[c, pl.ds(q * per, per), :], send_sems.at[q], recv_sems.at[q], (x, y, 1 - c))
               for q in range(PAIR_COPIES)]
        for cp in cps:
            cp.start()
        for q, cp in enumerate(cps):
            cp.wait_send()
            _remote(v_ref.at[pl.ds(q * per, per), :], out_ref.at[1 - c, pl.ds(q * per, per), :], send_sems.at[q], recv_sems.at[q],
                    (x, y, 1 - c)).wait_recv()

    both = _pcall(body, name=name, in_specs=[_HBM], out_specs=_HBM, out_shape=jax.ShapeDtypeStruct((2, h, cols), mine.dtype),
                  scratch_shapes=[pltpu.SemaphoreType.DMA((PAIR_COPIES,)), pltpu.SemaphoreType.DMA((PAIR_COPIES,))])(mine)
    return lax.dynamic_update_slice(both, mine[None], (lax.axis_index("c"), 0, 0))


def _reduce_over_devices(g, c, chip, *, name):
    n, rows, cols = g.shape
    g = g.reshape(n, 2, rows // 2, cols)
    chip_sum, narrow = _add_half(g, _pair_exchange_half(g, name=name + "_pair_sum"), c, name=name + "_add_pair")
    mine = _add_own(chip_sum, _scatter_over_chips(narrow, name=name + "_scatter"), chip, name=name + "_add_chips")
    return _pair_gather(mine, name=name + "_pair_gather").reshape(rows, cols)


def _adamw_at(w, m, v, g_all, first_row, *, name):
    tm = min(PACK_ROW_TILE, w.shape[0])
    assert w.shape[0] % tm == 0 and first_row % tm == 0, (name, w.shape, first_row)

    def fn(w_, m_, v_, g_):
        return _adamw_math(w_, g_, m_, v_) + (g_,)

    return _rowmap(fn, [_t(w), _t(m), _t(v), _t(g_all, first=first_row // tm)], [], [(None, w.shape[1], _F32)] * 4, tm=tm, name=name)


def _step(x, mem, positions, loss_target, w, m, v):
    cx, cy, cc = lax.axis_index("x"), lax.axis_index("y"), lax.axis_index("c")
    chip = 2 * cx + cy
    big = [n for n, _ in _BIG]
    shard_shapes = [w[n].shape for n in big]
    rows = -(-_packed_rows(shard_shapes) // (2 * PACK_ROW_TILE)) * 2 * PACK_ROW_TILE
    flat = lambda a: a.reshape(-1, a.shape[-1])

    ffn_w_gu = _gather_over_chips(jnp.concatenate([flat(w[n]) for n in _FFN_GU]).astype(_BF), name="gather_w_gu")
    ffn_w_down = _gather_over_chips(jnp.concatenate([flat(w[n]) for n in _FFN_DOWN]).astype(_BF), name="gather_w_down")
    w_pack = _pack([w[n] for n in big], rows)
    gathered = _gather_over_chips(w_pack.astype(_BF), name="gather_weights")
    pieces = [_unpack(gathered[k], shard_shapes) for k in range(N_CHIPS)]
    full = {n: jnp.concatenate([pieces[k][i] for k in range(N_CHIPS)], axis=ax) for i, (n, ax) in enumerate(_BIG)}
    for n in _SMALL:
        full[n] = w[n]
    conv = w["a_conv"]
    slots = jnp.stack([jnp.where((chip == k) & (cc == 0), conv, 0.0) for k in range(N_CHIPS)])
    conv_all = _unpack(_all_reduce_small(_pack([slots], _piece_rows(slots.shape)), name="gather_conv"), [slots.shape])[0]
    conv_full = jnp.concatenate([conv_all[k] for k in range(N_CHIPS)], axis=2)

    p = _compute_form(full, conv_full, ffn_w_gu, ffn_w_down)
    cos_t, sin_t = _rope_tables(positions[0])
    loss_tile, d_x, g = _local_step(x[0], mem[0], cos_t, sin_t, loss_target[0], p)
    gn = _natural_grads(g)

    def shard_of(a, ax, k):
        size = a.shape[ax] // N_CHIPS
        return lax.slice_in_dim(a, k * size, (k + 1) * size, axis=ax)

    grads, deltas, new_m, new_v = {}, {}, {}, {}
    for names, key in ((_FFN_GU, "ffn_w_gu"), (_FFN_DOWN, "ffn_w_down")):
        buf = g[key]
        reduced = _reduce_over_devices(buf.reshape(N_CHIPS, -1, buf.shape[-1]), cc, chip, name="grad_" + key)
        first = 0
        for n in names:
            d_, m_, v_, g_ = _adamw_at(flat(w[n]), flat(m[n]), flat(v[n]), reduced, first, name="adamw_" + n)
            grads[n], deltas[n], new_m[n], new_v[n] = (t.reshape(w[n].shape) for t in (g_, d_, m_, v_))
            first += flat(w[n]).shape[0]
    g_pack = jnp.stack([_pack([shard_of(gn[n], ax, k) for n, ax in _BIG], rows) for k in range(N_CHIPS)])
    g_big = _reduce_over_devices(g_pack, cc, chip, name="grad_misc")
    d_big, m_big, v_big = _adamw(w_pack, g_big, _pack([m[n] for n in big], rows), _pack([v[n] for n in big], rows), name="adamw_misc")

    small_shapes = [w[n].shape for n in _SMALL]
    small_rows = _packed_rows(small_shapes)
    g_small = _all_reduce_small(_pack([gn[n] for n in _SMALL], small_rows), name="grad_small")
    d_small, m_small, v_small = _adamw(_pack([w[n] for n in _SMALL], small_rows), g_small, _pack([m[n] for n in _SMALL], small_rows),
                                       _pack([v[n] for n in _SMALL], small_rows), name="adamw_small")

    for out, big_pack, small_pack in ((grads, g_big, g_small), (deltas, d_big, d_small), (new_m, m_big, m_small), (new_v, v_big, v_small)):
        out.update(zip(big, _unpack(big_pack, shard_shapes)))
        out.update(zip(_SMALL, _unpack(small_pack, small_shapes)))
    loss = lax.psum(loss_tile[0, 0], ("x", "y", "c"))
    return (loss, d_x[None], *[grads[n] for n in _WEIGHTS], *[deltas[n] for n in _WEIGHTS], *[new_m[n] for n in _WEIGHTS],
            *[new_v[n] for n in _WEIGHTS])


def kernel(x, mem, positions, ffn1_norm, ffn1_w_gu, ffn1_w_down, mix_norm, ffn2_norm, ffn2_w_gu, ffn2_w_down, w_out, mem_norm, w_mem_kv, a_w_in, a_conv, a_A_log, a_dt_bias, a_out_norm, b_w_in, b_q_norm, b_w_uq, kv_in_norm, w_dkv, kv_lat_norm, w_ukv, final_norm, loss_target, m_ffn1_norm, m_ffn1_w_gu, m_ffn1_w_down, m_mix_norm, m_ffn2_norm, m_ffn2_w_gu, m_ffn2_w_down, m_w_out, m_mem_norm, m_w_mem_kv, m_a_w_in, m_a_conv, m_a_A_log, m_a_dt_bias, m_a_out_norm, m_b_w_in, m_b_q_norm, m_b_w_uq, m_kv_in_norm, m_w_dkv, m_kv_lat_norm, m_w_ukv, m_final_norm, v_ffn1_norm, v_ffn1_w_gu, v_ffn1_w_down, v_mix_norm, v_ffn2_norm, v_ffn2_w_gu, v_ffn2_w_down, v_w_out, v_mem_norm, v_w_mem_kv, v_a_w_in, v_a_conv, v_a_A_log, v_a_dt_bias, v_a_out_norm, v_b_w_in, v_b_q_norm, v_b_w_uq, v_kv_in_norm, v_w_dkv, v_kv_lat_norm, v_w_ukv, v_final_norm):
    given = dict(locals())
    w = {n: given[n] for n in _WEIGHTS}
    m = {n: given["m_" + n] for n in _WEIGHTS}
    v = {n: given["v_" + n] for n in _WEIGHTS}
    return _step(x, mem, positions, loss_target, w, m, v)
```

```python
import functools
import math

import jax
import jax.numpy as jnp
from jax import lax
from jax.experimental import pallas as pl
from jax.experimental.pallas import tpu as pltpu

_BF = jnp.bfloat16
_F32 = jnp.float32
_HI = lax.Precision.HIGHEST
_MESH = pl.DeviceIdType.MESH

D_MODEL = 1024
DEPTH = 4
N_A = 2
N_B = 2
CHUNK = 64
EPS = 1e-6
HEADS = 6
HEAD_DIM = 128
A_WIDTH = HEADS * HEAD_DIM
CONV_K = 4
QK_ROPE = 64
Q_LORA = 256
KV_LORA = 256
N_MEM = 256
MEM_HEADS = 4
MEM_HEAD_DIM = 64
MEM_WIDTH = MEM_HEADS * MEM_HEAD_DIM
D_FF = 2816
ROPE_THETA = 10000.0
A_IN = 4 * A_WIDTH + 2 * HEADS + MEM_WIDTH
A_IN_PAD = 3456
UQ_PAD = 2 * A_WIDTH
DKV_PAD = KV_LORA + 128
LANE = 128
SUBLANE = 8

ADAM_LR = 0.001
ADAM_B1 = 0.9
ADAM_B2 = 0.999
ADAM_EPS = 1e-08
ADAM_WD = 0.01
ADAM_STEP = 10

N_CHIPS = 4
N_DEV = 8
PACK_COLS = 1024
PACK_ROW_TILE = 256


def _pcall(body, **kw):
    return pl.pallas_call(body, **kw)


VMEM_LIMIT_V7X = 48 * 2 ** 20
TILE_BYTES = 6 * 2 ** 20


def _cp(grid_rank):
    return pltpu.CompilerParams(dimension_semantics=("arbitrary",) * grid_rank, vmem_limit_bytes=VMEM_LIMIT_V7X)


def _fit_rows(rows, row_bytes):
    while rows > LANE and rows * row_bytes > TILE_BYTES:
        rows //= 2
    return rows


def _fit_cols(n, target, col_bytes):
    return _tile(n, max(LANE, min(target, TILE_BYTES // col_bytes)))


def _tile(n, target):
    best = None
    for t in range(LANE, min(n, target) + 1, LANE):
        if n % t == 0:
            best = t
    return best if best is not None else n


def _dot(a, b):
    return jnp.dot(a.astype(_BF), b.astype(_BF), preferred_element_type=_F32)


def _dot_nt(a, b):
    return lax.dot_general(a.astype(_BF), b.astype(_BF), (((1,), (1,)), ((), ())), preferred_element_type=_F32)


def _dot_tn(a, b):
    return lax.dot_general(a.astype(_BF), b.astype(_BF), (((0,), (0,)), ((), ())), preferred_element_type=_F32)


def _dot_hi(a, b):
    return jnp.dot(a, b, precision=_HI, preferred_element_type=_F32)


def _rowmap(fn, tiles, params, outs, accs=(), *, tm, name, carry=(), reverse=False):
    rows = tiles[0][0].shape[0]
    steps = rows // tm
    nt, npar, no, na, nc = len(tiles), len(params), len(outs), len(accs), len(carry)

    def step_index(i):
        return steps - 1 - i if reverse else i

    in_specs, operands = [], []
    for arr, r, w, cb, first in tiles:
        r = tm if r is None else r
        w = arr.shape[1] if w is None else w
        assert arr.shape[0] >= (first + steps) * r and (w % LANE == 0 or w == arr.shape[1]), (name, arr.shape, r, w)
        in_specs.append(pl.BlockSpec((r, w), functools.partial(lambda i, cb, first: (first + step_index(i), cb), cb=cb, first=first)))
        operands.append(arr)
    for p in params:
        in_specs.append(pl.BlockSpec(p.shape, functools.partial(lambda i, nd: (0,) * nd, nd=p.ndim)))
        operands.append(p)
    out_specs, out_shape = [], []
    for r, cols, dt in outs:
        r = tm if r is None else r
        out_specs.append(pl.BlockSpec((r, cols), lambda i: (step_index(i), 0)))
        out_shape.append(jax.ShapeDtypeStruct((steps * r, cols), dt))
    for shp, dt in accs:
        out_specs.append(pl.BlockSpec(shp, functools.partial(lambda i, nd: (0,) * nd, nd=len(shp))))
        out_shape.append(jax.ShapeDtypeStruct(shp, dt))

    def body(*refs):
        t_refs = refs[:nt]
        p_refs = refs[nt:nt + npar]
        o_refs = refs[nt + npar:nt + npar + no]
        a_refs = refs[nt + npar + no:nt + npar + no + na]
        c_refs = refs[nt + npar + no + na:]
        if na or nc:
            @pl.when(pl.program_id(0) == 0)
            def _():
                for r in a_refs + c_refs:
                    r[...] = jnp.zeros(r.shape, r.dtype)
        vals = fn(*[r[...] for r in t_refs], *[r[...] for r in p_refs], *[r[...] for r in c_refs])
        vals = tuple(vals) if isinstance(vals, (tuple, list)) else (vals,)
        assert len(vals) == no + na + nc, (name, len(vals), no, na, nc)
        for r, v in zip(o_refs, vals[:no]):
            r[...] = v.astype(r.dtype)
        for r, v in zip(a_refs, vals[no:no + na]):
            r[...] += v.astype(r.dtype)
        for r, v in zip(c_refs, vals[no + na:]):
            r[...] = v.astype(r.dtype)

    res = _pcall(
        body, name=name, grid=(steps,), in_specs=in_specs, out_specs=out_specs, out_shape=out_shape,
        scratch_shapes=[pltpu.VMEM(shp, dt) for shp, dt in carry],
        compiler_params=_cp(1),
    )(*operands)
    return res


def _t(arr, width=None, cb=0, rows=None, first=0):
    return (arr, rows, width, cb, first)


def _mm_nn(a, b, *, out_dtype, name, scale=None, res=None, tm=1024, tn=1536):
    m, k = a.shape
    n = b.shape[1]
    tm, tn = _fit_rows(min(tm, m), k * a.dtype.itemsize), _fit_cols(n, tn, k * b.dtype.itemsize)

    def body(a_ref, b_ref, *rest):
        acc = _dot(a_ref[...], b_ref[...])
        if scale is not None:
            acc = acc * scale
        if res is not None:
            acc = acc + rest[0][...]
        rest[-1][...] = acc.astype(rest[-1].dtype)

    in_specs = [pl.BlockSpec((tm, k), lambda i, j: (i, 0)), pl.BlockSpec((k, tn), lambda i, j: (0, j))]
    operands = [a, b]
    if res is not None:
        in_specs.append(pl.BlockSpec((tm, tn), lambda i, j: (i, j)))
        operands.append(res)
    return _pcall(
        body, name=name, grid=(m // tm, n // tn), in_specs=in_specs,
        out_specs=pl.BlockSpec((tm, tn), lambda i, j: (i, j)), out_shape=jax.ShapeDtypeStruct((m, n), out_dtype),
        compiler_params=_cp(2),
    )(*operands)


def _mm_nt(a, b, *, out_dtype, name, scale=None, tm=1024, tn=1536):
    m, k = a.shape
    n = b.shape[0]
    tm, tn = _fit_rows(min(tm, m), k * a.dtype.itemsize), _fit_cols(n, tn, k * b.dtype.itemsize)

    def body(a_ref, b_ref, o_ref):
        acc = _dot_nt(a_ref[...], b_ref[...])
        if scale is not None:
            acc = acc * scale
        o_ref[...] = acc.astype(o_ref.dtype)

    return _pcall(
        body, name=name, grid=(m // tm, n // tn),
        in_specs=[pl.BlockSpec((tm, k), lambda i, j: (i, 0)), pl.BlockSpec((tn, k), lambda i, j: (j, 0))],
        out_specs=pl.BlockSpec((tm, tn), lambda i, j: (i, j)), out_shape=jax.ShapeDtypeStruct((m, n), out_dtype),
        compiler_params=_cp(2),
    )(a, b)


def _mm_tn(a, b, *, name, scale=None, t1=1024, tn=1536, ts=1024):
    s, k1 = a.shape
    n = b.shape[1]
    t1, tn, ts = _tile(k1, t1), _tile(n, tn), min(ts, s)
    steps = s // ts

    def body(a_ref, b_ref, o_ref):
        @pl.when(pl.program_id(2) == 0)
        def _():
            o_ref[...] = jnp.zeros(o_ref.shape, o_ref.dtype)

        o_ref[...] += _dot_tn(a_ref[...], b_ref[...])
        if scale is not None:
            @pl.when(pl.program_id(2) == steps - 1)
            def _():
                o_ref[...] = o_ref[...] * scale

    return _pcall(
        body, name=name, grid=(k1 // t1, n // tn, steps),
        in_specs=[pl.BlockSpec((ts, t1), lambda i, j, r: (r, i)), pl.BlockSpec((ts, tn), lambda i, j, r: (r, j))],
        out_specs=pl.BlockSpec((t1, tn), lambda i, j, r: (i, j)), out_shape=jax.ShapeDtypeStruct((k1, n), _F32),
        compiler_params=_cp(3),
    )(a, b)


def _heads(t, n, w=LANE):
    return [t[:, w * h:w * (h + 1)] for h in range(n)]


def _cat(parts):
    return jnp.concatenate(parts, axis=1)


def _rms(x, g):
    return x * lax.rsqrt(jnp.mean(x * x, axis=-1, keepdims=True) + EPS) * g


def _l2n(x):
    return x * lax.rsqrt(jnp.sum(x * x, axis=-1, keepdims=True) + EPS)


def _sigmoid(x):
    return 0.5 * (jnp.tanh(0.5 * x) + 1.0)


def _silu(x):
    return x * _sigmoid(x)


def _softplus(x):
    return jnp.maximum(x, 0.0) + jnp.log(1.0 + jnp.exp(-jnp.abs(x)))


def _lane_pick(t, h):
    lane = lax.broadcasted_iota(jnp.int32, t.shape, 1)
    return jnp.sum(jnp.where(lane == h, t, 0.0), axis=1, keepdims=True)


def _lane_put(col, h, width=LANE):
    lane = lax.broadcasted_iota(jnp.int32, (col.shape[0], width), 1)
    return jnp.where(lane == h, col, 0.0)


def _vjp(fwd, ins, cts):
    outs, pull = jax.vjp(fwd, *ins)
    outs = outs if isinstance(outs, (tuple, list)) else (outs,)
    cts = tuple(c.astype(o.dtype) for c, o in zip(cts, outs))
    return pull(cts if len(cts) > 1 else cts[0])


def _rot_half_matrix():
    r = lax.broadcasted_iota(jnp.int32, (LANE, LANE), 0)
    c = lax.broadcasted_iota(jnp.int32, (LANE, LANE), 1)
    half = QK_ROPE // 2
    return jnp.where((c < half) & (r == c + half), -1.0, jnp.where((c >= half) & (c < QK_ROPE) & (r == c - half), 1.0, 0.0))


def _rope(x, cos_t, sin_t):
    return x * cos_t + _dot_hi(x, _rot_half_matrix()) * sin_t


def _mem_attn(q, km, vm):
    lane_q = lax.broadcasted_iota(jnp.int32, q.shape, 1)
    lane_v = lax.broadcasted_iota(jnp.int32, vm.shape, 1)
    out = jnp.zeros(q.shape, _F32)
    for h in range(MEM_HEADS):
        lo, hi = MEM_HEAD_DIM * h, MEM_HEAD_DIM * (h + 1)
        qh = jnp.where((lane_q >= lo) & (lane_q < hi), q, 0.0)
        vh = jnp.where((lane_v >= lo) & (lane_v < hi), vm, 0.0)
        sc = _dot_nt(qh, km) * MEM_HEAD_DIM ** -0.5
        sc = sc - lax.stop_gradient(jnp.max(sc, axis=-1, keepdims=True))
        p = jnp.exp(sc)
        p = p / jnp.sum(p, axis=-1, keepdims=True)
        out = out + _dot(p, vh)
    return out


_PAIR = 2 * CHUNK


def _pair_masks():
    ri = lax.broadcasted_iota(jnp.int32, (_PAIR, _PAIR), 0)
    ci = lax.broadcasted_iota(jnp.int32, (_PAIR, _PAIR), 1)
    same = (ri >= CHUNK) == (ci >= CHUNK)
    return same, same & (ri >= ci), same & (ri > ci), ri == ci, same & (ri <= ci)


_NN = (((2,), (1,)), ((0,), (0,)))
_NT = (((2,), (2,)), ((0,), (0,)))
_TN = (((1,), (1,)), ((0,), (0,)))


def _bdot(a, b, dims):
    return lax.dot_general(a.astype(_BF), b.astype(_BF), dims, preferred_element_type=_F32)


def _dot3(a, b, dims):
    a_hi, b_hi = a.astype(_BF), b.astype(_BF)
    a_lo, b_lo = (a - a_hi.astype(_F32)).astype(_BF), (b - b_hi.astype(_F32)).astype(_BF)
    d = lambda x, y: lax.dot_general(x, y, dims, preferred_element_type=_F32)
    return d(a_hi, b_hi) + (d(a_hi, b_lo) + d(a_lo, b_hi))


@jax.custom_vjp
def _mm3(a, b):
    return _dot3(a, b, _NN)


_mm3.defvjp(lambda a, b: (_dot3(a, b, _NN), (a, b)), lambda res, g: (_dot3(g, res[1], _NT), _dot3(res[0], g, _TN)))


def _neumann_inverse(a):
    eye = jnp.where(_pair_masks()[3], 1.0, 0.0)
    n = -a
    t_inv = eye + n
    for _ in range(5):
        n = _dot3(n, n, _NN)
        t_inv = t_inv + _dot3(t_inv, n, _NN)
    return t_inv


@jax.custom_vjp
def _unit_lower_inverse(a):
    return _neumann_inverse(a)


def _unit_lower_inverse_fwd(a):
    t_inv = _neumann_inverse(a)
    return t_inv, t_inv


def _unit_lower_inverse_bwd(t_inv, g):
    return (-_dot3(t_inv, _dot3(g, t_inv, _NT), _TN),)


_unit_lower_inverse.defvjp(_unit_lower_inverse_fwd, _unit_lower_inverse_bwd)


def _gdn_intra_head(q, k, v, beta, gl):
    same, causal, strict, eye, upper = _pair_masks()
    gl_row = jnp.sum(jnp.where(eye, gl, 0.0), axis=-2, keepdims=True)
    g_col = jnp.sum(jnp.where(causal, gl_row, 0.0), axis=-1, keepdims=True)
    g_row = jnp.sum(jnp.where(upper, gl, 0.0), axis=-2, keepdims=True)
    g_last = jnp.sum(jnp.where(same, gl_row, 0.0), axis=-1, keepdims=True)
    decay = jnp.where(causal, jnp.exp(jnp.where(causal, g_col - g_row, 0.0)), 0.0)
    kb = k * beta
    a = jnp.where(strict, _bdot(kb, k, _NT) * decay, 0.0)
    t_inv = _unit_lower_inverse(a)
    e_g = jnp.exp(g_col)
    u = _mm3(t_inv, v * beta)
    w = _mm3(t_inv, kb * e_g)
    qk = _bdot(q, k, _NT) * decay
    return w, u, q * e_g, k * jnp.exp(g_last - g_col), qk, jnp.exp(g_last)


def _gdn_scan_head(s, qd_a, kd_a, w_a, u_a, qk_a, dc_a, qd_b, kd_b, w_b, u_b, qk_b, dc_b):
    zeros = jnp.zeros((HEADS, CHUNK, HEAD_DIM), _F32)
    vn_a = u_a - _bdot(w_a, s, _NN)
    o_a = _bdot(qd_a, s, _NN) + _bdot(qk_a, jnp.concatenate([vn_a, zeros], axis=1), _NN)
    s1 = s * dc_a + _bdot(kd_a, vn_a, _TN)
    vn_b = u_b - _bdot(w_b, s1, _NN)
    o_b = _bdot(qd_b, s1, _NN) + _bdot(qk_b, jnp.concatenate([zeros, vn_b], axis=1), _NN)
    s2 = s1 * dc_b + _bdot(kd_b, vn_b, _TN)
    return o_a, o_b, s2


def _pick_scalar(t, row, lane_i):
    ri = lax.broadcasted_iota(jnp.int32, t.shape, 0)
    ci = lax.broadcasted_iota(jnp.int32, t.shape, 1)
    return jnp.sum(jnp.sum(jnp.where((ri == row) & (ci == lane_i), t, 0.0), axis=1, keepdims=True), axis=0, keepdims=True)


def _put_scalar(val, row, lane_i, shape):
    ri = lax.broadcasted_iota(jnp.int32, shape, 0)
    ci = lax.broadcasted_iota(jnp.int32, shape, 1)
    return jnp.where((ri == row) & (ci == lane_i), val, 0.0)


def _by_head(t):
    return jnp.stack(_heads(t, HEADS))


def _from_heads(t):
    return _cat([t[h] for h in range(HEADS)])


def _state_by_head(s):
    return jnp.stack([s[HEAD_DIM * h:HEAD_DIM * (h + 1), :] for h in range(HEADS)])


def _scan_ins(qd, kd, w, u, qk, dcrow, state):
    ins = [_state_by_head(state)]
    for r0 in (0, CHUNK):
        rs = slice(r0, r0 + CHUNK)
        ins += [_by_head(t[rs, :]) for t in (qd, kd, w, u, qk)]
        ins.append(jnp.stack([_pick_scalar(dcrow, r0, h) for h in range(HEADS)]))
    return ins


def _gdn_scan_fwd_fn(qd, kd, w, u, qk, dcrow, state):
    o_a, o_b, s2 = _gdn_scan_head(*_scan_ins(qd, kd, w, u, qk, dcrow, state))
    return jnp.concatenate([_from_heads(o_a), _from_heads(o_b)], axis=0), state, s2.reshape(state.shape)


def _gdn_scan_bwd_fn(qd, kd, w, u, qk, dcrow, state, d_o, d_state):
    cts = (_by_head(d_o[0:CHUNK, :]), _by_head(d_o[CHUNK:_PAIR, :]), _state_by_head(d_state))
    g = _vjp(_gdn_scan_head, _scan_ins(qd, kd, w, u, qk, dcrow, state), cts)
    grads = tuple(jnp.concatenate([_from_heads(g[1 + t]), _from_heads(g[7 + t])], axis=0) for t in range(5))
    d_dcrow = sum(_put_scalar(g[6][h], 0, h, dcrow.shape) + _put_scalar(g[12][h], CHUNK, h, dcrow.shape) for h in range(HEADS))
    return grads + (d_dcrow, g[0].reshape(state.shape))


def _gdn_intra_ins(q, k, v, bg):
    return [_by_head(q), _by_head(k), _by_head(v), jnp.stack([_lane_pick(bg, h) for h in range(HEADS)]),
            jnp.stack([_lane_pick(bg, HEADS + h) for h in range(HEADS)])]


def _gdn_intra_fwd_fn(q, k, v, bg):
    res = _gdn_intra_head(*_gdn_intra_ins(q, k, v, bg))
    dcrow = sum(_lane_put(res[5][h], h) for h in range(HEADS))
    return tuple(_from_heads(r) for r in res[:5]) + (dcrow,)


def _gdn_intra_bwd_fn(q, k, v, bg, d_w, d_u, d_qd, d_kd, d_qk, d_dcrow):
    cts = tuple(_by_head(d) for d in (d_w, d_u, d_qd, d_kd, d_qk)) + (jnp.stack([_lane_pick(d_dcrow, h) for h in range(HEADS)]),)
    g = _vjp(_gdn_intra_head, _gdn_intra_ins(q, k, v, bg), cts)
    d_bg = sum(_lane_put(g[3][h], h) + _lane_put(g[4][h], HEADS + h) for h in range(HEADS))
    return tuple(_from_heads(g[t]) for t in range(3)) + (d_bg,)


def _gdn_gates(ba, alog, dtb):
    lane = lax.broadcasted_iota(jnp.int32, ba.shape, 1)
    beta = _sigmoid(ba)
    g = -jnp.exp(alog) * _softplus(ba + dtb)
    return jnp.where(lane < HEADS, beta, jnp.where(lane < 2 * HEADS, g, 0.0))


def _gdn_q_head(c):
    return _l2n(_silu(c)) * HEAD_DIM ** -0.5


def _gdn_k_head(c):
    return _l2n(_silu(c))


def _gdn_prep_fwd_fn(qkv_c, ba, alog, dtb):
    hs = _heads(qkv_c, 3 * HEADS)
    q = _cat([_gdn_q_head(c) for c in hs[:HEADS]])
    k = _cat([_gdn_k_head(c) for c in hs[HEADS:2 * HEADS]])
    v = _cat([_silu(c) for c in hs[2 * HEADS:]])
    return q, k, v, _gdn_gates(ba, alog, dtb)


def _gdn_prep_bwd_fn(qkv_c, ba, d_q, d_k, d_v, d_bg, alog, dtb):
    hs = _heads(qkv_c, 3 * HEADS)
    dqs, dks, dvs = _heads(d_q, HEADS), _heads(d_k, HEADS), _heads(d_v, HEADS)
    parts = [_vjp(_gdn_q_head, [hs[h]], (dqs[h],))[0] for h in range(HEADS)]
    parts += [_vjp(_gdn_k_head, [hs[HEADS + h]], (dks[h],))[0] for h in range(HEADS)]
    parts += [_vjp(_silu, [hs[2 * HEADS + h]], (dvs[h],))[0] for h in range(HEADS)]
    d_ba, d_alog, d_dtb = _vjp(_gdn_gates, [ba, alog, dtb], (d_bg,))
    return _cat(parts), d_ba, d_alog, d_dtb


def _a_out_head(o, gate, gain):
    return _rms(o, gain) * _silu(gate)


def _a_post_fwd_fn(o, gate, qm, gain, mem_kv):
    parts = [_a_out_head(oh, gh, gain) for oh, gh in zip(_heads(o, HEADS), _heads(gate, HEADS))]
    parts.append(_mem_attn(qm, mem_kv[:, :MEM_WIDTH], mem_kv[:, MEM_WIDTH:]))
    return (_cat(parts),)


def _a_post_bwd_fn(o, gate, qm, d_cat, gain, mem_kv):
    d_os, d_gates = [], []
    d_gain = jnp.zeros(gain.shape, _F32)
    dc = _heads(d_cat, HEADS + 2)
    for h, (oh, gh) in enumerate(zip(_heads(o, HEADS), _heads(gate, HEADS))):
        g = _vjp(_a_out_head, [oh, gh, gain], (dc[h],))
        d_os.append(g[0])
        d_gates.append(g[1])
        d_gain = d_gain + g[2]
    d_qm, d_km, d_vm = _vjp(_mem_attn, [qm, mem_kv[:, :MEM_WIDTH], mem_kv[:, MEM_WIDTH:]], (d_cat[:, A_WIDTH:],))
    return _cat(d_os), _cat(d_gates + [d_qm]), d_gain, _cat([d_km, d_vm])


def _b_post_fwd_fn(o, qm, mem_kv):
    return (_cat([o.astype(_F32), _mem_attn(qm, mem_kv[:, :MEM_WIDTH], mem_kv[:, MEM_WIDTH:])]),)


def _b_post_bwd_fn(qm, d_cat_m, mem_kv):
    d_qm, d_km, d_vm = _vjp(_mem_attn, [qm, mem_kv[:, :MEM_WIDTH], mem_kv[:, MEM_WIDTH:]], (d_cat_m,))
    return d_qm, _cat([d_km, d_vm])


ATTN_SCALE = (HEAD_DIM + QK_ROPE) ** -0.5


def _rope_q_fwd_fn(qf, cos_t, sin_t):
    hs = _heads(qf, 2 * HEADS)
    return (_cat(hs[:HEADS] + [_rope(x, cos_t, sin_t) for x in hs[HEADS:]]) * ATTN_SCALE,)


def _rope_q_bwd_fn(d_qn, d_qr, cos_t, sin_t):
    f = lambda x: _rope(x, cos_t, sin_t)
    return (_cat([d_qn] + [_vjp(f, [x], (x,))[0] for x in _heads(d_qr, HEADS)]) * ATTN_SCALE,)


def _kv_prep_fwd_fn(ckr, cos_t, sin_t, gain):
    return _rms(ckr[:, :KV_LORA], gain), _rope(ckr[:, KV_LORA:], cos_t, sin_t)


def _kv_prep_bwd_fn(ckr, d_ckv, d_kr, cos_t, sin_t, gain):
    d_lat, d_gain = _vjp(_rms, [ckr[:, :KV_LORA], gain], (d_ckv,))
    f = lambda x: _rope(x, cos_t, sin_t)
    d_rope = _vjp(f, [ckr[:, KV_LORA:]], (d_kr,))[0]
    return _cat([d_lat, d_rope]), d_gain


def _rms_fwd(x, gain, *, name, tm=1024, out_dtype=_BF):
    tm = min(tm, x.shape[0])
    return _rowmap(lambda x_, g_: (_rms(x_.astype(_F32), g_),), [_t(x)], [gain], [(None, x.shape[1], out_dtype)], tm=tm, name=name)[0]


def _rms_bwd(x, d_xn, d_res, gain, *, name, tm=512):
    tm = min(tm, x.shape[0])

    def fn(x_, dxn_, *rest):
        g_ = rest[-1]
        dx, dg = _vjp(_rms, [x_.astype(_F32), g_], (dxn_.astype(_F32),))
        if d_res is not None:
            dx = dx + rest[0]
        return dx, dg

    tiles = [_t(x), _t(d_xn)] + ([_t(d_res)] if d_res is not None else [])
    return _rowmap(fn, tiles, [gain], [(None, x.shape[1], _F32)], [(gain.shape, _F32)], tm=tm, name=name)


FFN_COL_TILE = 1408
FFN_ROW_TILE = 512
FFN_DXN_ROW_TILE = 512
FFN_DOWN_ROW_TILE = 1024


def _ffn_gate_up(x, gain, w_gu, idx, *, name):
    s = x.shape[0]
    tm, tf = min(FFN_ROW_TILE, s), FFN_COL_TILE
    nf = D_FF // tf

    def body(x_ref, gain_ref, wg_ref, wu_ref, xn_ref, g_ref, u_ref, a_ref):
        @pl.when(pl.program_id(1) == 0)
        def _():
            xn_ref[...] = _rms(x_ref[...], gain_ref[...]).astype(xn_ref.dtype)

        xn = xn_ref[...]
        g, u = _dot(xn, wg_ref[...]), _dot(xn, wu_ref[...])
        g_ref[...] = g.astype(g_ref.dtype)
        u_ref[...] = u.astype(u_ref.dtype)
        a_ref[...] = (_silu(g) * u).astype(a_ref.dtype)

    col = pl.BlockSpec((tm, tf), lambda i, j: (i, j))
    wide = jax.ShapeDtypeStruct((s, D_FF), _BF)
    return _pcall(
        body, name=name, grid=(s // tm, nf),
        in_specs=[pl.BlockSpec((tm, D_MODEL), lambda i, j: (i, 0)), pl.BlockSpec((1, D_MODEL), lambda i, j: (0, 0)),
                  pl.BlockSpec((None, D_MODEL, tf), lambda i, j: (j, idx, 0)), pl.BlockSpec((None, D_MODEL, tf), lambda i, j: (nf + j, idx, 0))],
        out_specs=[pl.BlockSpec((tm, D_MODEL), lambda i, j: (i, 0)), col, col, col],
        out_shape=[jax.ShapeDtypeStruct((s, D_MODEL), _BF), wide, wide, wide],
        compiler_params=_cp(2),
    )(x, gain, w_gu, w_gu)


FFN_SHARD_ROWS = D_FF // N_CHIPS


def _w_down_specs(idx):
    return [pl.BlockSpec((None, FFN_SHARD_ROWS, D_MODEL), functools.partial(lambda i, j, q: (2 * j + q, idx, 0), q=q)) for q in (0, 1)]


def _ffn_down(a, w_down, idx, x, *, name):
    s = a.shape[0]
    tm = min(FFN_DOWN_ROW_TILE, s)

    def body(a_ref, w0_ref, w1_ref, w2_ref, w3_ref, x_ref, o_ref):
        w = jnp.concatenate([w0_ref[...], w1_ref[...], w2_ref[...], w3_ref[...]], axis=0)
        o_ref[...] = x_ref[...] + 0.5 * _dot(a_ref[...], w)

    rows = pl.BlockSpec((tm, D_MODEL), lambda i: (i, 0))
    w_specs = [pl.BlockSpec((None, FFN_SHARD_ROWS, D_MODEL), functools.partial(lambda i, k: (k, idx, 0), k=k)) for k in range(N_CHIPS)]
    return _pcall(
        body, name=name, grid=(s // tm,), in_specs=[pl.BlockSpec((tm, D_FF), lambda i: (i, 0))] + w_specs + [rows],
        out_specs=rows, out_shape=jax.ShapeDtypeStruct((s, D_MODEL), _F32), compiler_params=_cp(1),
    )(a, w_down, w_down, w_down, w_down, x)


def _ffn_fwd(x, gain, w_gu, w_down, idx, *, name):
    xn, g, u, a = _ffn_gate_up(x, gain, w_gu, idx, name=name + "_gu")
    y = _ffn_down(a, w_down, idx, x, name=name + "_down")
    return y, (x, xn, g, u, a)


def _ffn_d_gate_up(d_y, g, u, w_down, idx, *, name):
    s = d_y.shape[0]
    tm, tf = min(FFN_ROW_TILE, s), FFN_COL_TILE

    def body(dy_ref, wa_ref, wb_ref, g_ref, u_ref, dg_ref, du_ref):
        da = _dot_nt(dy_ref[...], jnp.concatenate([wa_ref[...], wb_ref[...]], axis=0)) * 0.5
        gg, uu = g_ref[...].astype(_F32), u_ref[...].astype(_F32)
        sg = _sigmoid(gg)
        dg_ref[...] = (da * uu * sg * (1.0 + gg * (1.0 - sg))).astype(dg_ref.dtype)
        du_ref[...] = (da * gg * sg).astype(du_ref.dtype)

    col = pl.BlockSpec((tm, tf), lambda i, j: (i, j))
    wide = jax.ShapeDtypeStruct((s, D_FF), _BF)
    return _pcall(
        body, name=name, grid=(s // tm, D_FF // tf),
        in_specs=[pl.BlockSpec((tm, D_MODEL), lambda i, j: (i, 0))] + _w_down_specs(idx) + [col, col],
        out_specs=[col, col], out_shape=[wide, wide], compiler_params=_cp(2),
    )(d_y, w_down, w_down, g, u)


def _ffn_d_x(d_g, d_u, w_gu, idx, x, d_y, gain, *, name):
    s = x.shape[0]
    tm, tf = min(FFN_DXN_ROW_TILE, s), FFN_COL_TILE

    def body(dg_ref, du_ref, w0_ref, w1_ref, w2_ref, w3_ref, x_ref, dy_ref, gain_ref, dx_ref, dgain_ref):
        @pl.when(pl.program_id(0) == 0)
        def _():
            dgain_ref[...] = jnp.zeros(dgain_ref.shape, _F32)

        d_xn = (_dot_nt(dg_ref[:, 0:tf], w0_ref[...]) + _dot_nt(dg_ref[:, tf:2 * tf], w1_ref[...])
                + _dot_nt(du_ref[:, 0:tf], w2_ref[...]) + _dot_nt(du_ref[:, tf:2 * tf], w3_ref[...]))
        dx, dgain = _vjp(_rms, [x_ref[...], gain_ref[...]], (d_xn,))
        dx_ref[...] = dx + dy_ref[...]
        dgain_ref[...] += dgain

    wide = pl.BlockSpec((tm, D_FF), lambda i: (i, 0))
    rows = pl.BlockSpec((tm, D_MODEL), lambda i: (i, 0))
    one = pl.BlockSpec((1, D_MODEL), lambda i: (0, 0))
    w_specs = [pl.BlockSpec((None, D_MODEL, tf), functools.partial(lambda i, k: (k, idx, 0), k=k), pipeline_mode=pl.Buffered(1))
               for k in range(N_CHIPS)]
    return _pcall(
        body, name=name, grid=(s // tm,),
        in_specs=[wide, wide] + w_specs + [rows, rows, one],
        out_specs=[rows, one], out_shape=[jax.ShapeDtypeStruct((s, D_MODEL), _F32), jax.ShapeDtypeStruct((1, D_MODEL), _F32)],
        compiler_params=_cp(1),
    )(d_g, d_u, w_gu, w_gu, w_gu, w_gu, x, d_y, gain)


def _ffn_d_w_gu(xn, d_act, into, idx, first_chip, *, name, ts=1024):
    s = xn.shape[0]
    ts = min(ts, s)
    steps = s // ts

    def body(a_ref, b_ref, into_ref, o_ref):
        @pl.when(pl.program_id(1) == 0)
        def _():
            o_ref[...] = jnp.zeros(o_ref.shape, o_ref.dtype)

        o_ref[...] += _dot_tn(a_ref[...], b_ref[...])

    return _pcall(
        body, name=name, grid=(D_FF // FFN_COL_TILE, steps),
        in_specs=[pl.BlockSpec((ts, D_MODEL), lambda j, r: (r, 0)), pl.BlockSpec((ts, FFN_COL_TILE), lambda j, r: (r, j)),
                  pl.BlockSpec(memory_space=pl.ANY)],
        out_specs=pl.BlockSpec((None, None, D_MODEL, FFN_COL_TILE), lambda j, r: (first_chip + j, idx, 0, 0)),
        out_shape=jax.ShapeDtypeStruct(into.shape, into.dtype), input_output_aliases={2: 0}, compiler_params=_cp(2),
    )(xn, d_act, into)


def _ffn_d_w_down(a, d_y, into, idx, *, name, ts=1024):
    s = a.shape[0]
    ts = min(ts, s)
    steps = s // ts

    def body(a_ref, b_ref, into_ref, o_ref):
        @pl.when(pl.program_id(1) == 0)
        def _():
            o_ref[...] = jnp.zeros(o_ref.shape, o_ref.dtype)

        part = _dot_tn(a_ref[...], b_ref[...]) * 0.5
        o_ref[0] += part[0:FFN_SHARD_ROWS, :]
        o_ref[1] += part[FFN_SHARD_ROWS:2 * FFN_SHARD_ROWS, :]

    return _pcall(
        body, name=name, grid=(D_FF // FFN_COL_TILE, steps),
        in_specs=[pl.BlockSpec((ts, FFN_COL_TILE), lambda i, r: (r, i)), pl.BlockSpec((ts, D_MODEL), lambda i, r: (r, 0)),
                  pl.BlockSpec(memory_space=pl.ANY)],
        out_specs=pl.BlockSpec((2, None, FFN_SHARD_ROWS, D_MODEL), lambda i, r: (i, idx, 0, 0)),
        out_shape=jax.ShapeDtypeStruct(into.shape, into.dtype), input_output_aliases={2: 0}, compiler_params=_cp(2),
    )(a, d_y, into)


def _ffn_bwd(d_y, saved, gain, w_gu, w_down, idx, g_gu, g_down, *, name):
    x, xn, g, u, a = saved
    d_g, d_u = _ffn_d_gate_up(d_y, g, u, w_down, idx, name=name + "_dgu")
    g_down = _ffn_d_w_down(a, d_y, g_down, idx, name=name + "_dwd")
    g_gu = _ffn_d_w_gu(xn, d_g, g_gu, idx, 0, name=name + "_dwg")
    g_gu = _ffn_d_w_gu(xn, d_u, g_gu, idx, 2, name=name + "_dwu")
    d_x, d_gain = _ffn_d_x(d_g, d_u, w_gu, idx, x, d_y, gain, name=name + "_dx")
    return d_x, d_gain, g_gu, g_down


def _conv_fwd(h, w, *, name, tm=512):
    s = h.shape[0]
    tm = min(tm, s)
    c = 3 * A_WIDTH
    halo = SUBLANE

    def body(x_ref, prev_ref, w_ref, o_ref, buf):
        i = pl.program_id(0)
        buf[0:halo, :] = jnp.where(i == 0, 0.0, prev_ref[...])
        buf[halo:halo + tm, :] = x_ref[...]
        acc = jnp.zeros((tm, c), _F32)
        for j in range(CONV_K):
            acc = acc + buf[pl.ds(halo - (CONV_K - 1) + j, tm), :] * w_ref[j:j + 1, :]
        o_ref[...] = acc

    return _pcall(
        body, name=name, grid=(s // tm,),
        in_specs=[pl.BlockSpec((tm, c), lambda i: (i, 0)),
                  pl.BlockSpec((halo, c), lambda i: (jnp.maximum(i * (tm // halo) - 1, 0), 0)),
                  pl.BlockSpec(w.shape, lambda i: (0, 0))],
        out_specs=pl.BlockSpec((tm, c), lambda i: (i, 0)), out_shape=jax.ShapeDtypeStruct((s, c), _F32),
        scratch_shapes=[pltpu.VMEM((tm + 2 * halo, c), _F32)],
        compiler_params=_cp(1),
    )(h, h, w)


def _conv_bwd(h, d_y, w, *, name, tm=512):
    s = h.shape[0]
    tm = min(tm, s)
    c = 3 * A_WIDTH
    halo = SUBLANE
    steps = s // tm

    def body(x_ref, prev_ref, dy_ref, next_ref, w_ref, dx_ref, dw_ref, xbuf, dybuf):
        i = pl.program_id(0)

        @pl.when(i == 0)
        def _():
            dw_ref[...] = jnp.zeros(dw_ref.shape, dw_ref.dtype)

        xbuf[0:halo, :] = jnp.where(i == 0, 0.0, prev_ref[...])
        xbuf[halo:halo + tm, :] = x_ref[...]
        dybuf[0:tm, :] = dy_ref[...]
        dybuf[tm:tm + halo, :] = jnp.where(i == steps - 1, 0.0, next_ref[...])
        dy = dy_ref[...]
        acc = jnp.zeros((tm, c), _F32)
        for j in range(CONV_K):
            acc = acc + dybuf[pl.ds(CONV_K - 1 - j, tm), :] * w_ref[j:j + 1, :]
            dw_ref[j:j + 1, :] += jnp.sum(dy * xbuf[pl.ds(halo - (CONV_K - 1) + j, tm), :], axis=0, keepdims=True)
        dx_ref[...] = acc.astype(dx_ref.dtype)

    return _pcall(
        body, name=name, grid=(steps,),
        in_specs=[pl.BlockSpec((tm, c), lambda i: (i, 0)),
                  pl.BlockSpec((halo, c), lambda i: (jnp.maximum(i * (tm // halo) - 1, 0), 0)),
                  pl.BlockSpec((tm, c), lambda i: (i, 0)),
                  pl.BlockSpec((halo, c), lambda i: (jnp.minimum((i + 1) * (tm // halo), s // halo - 1), 0)),
                  pl.BlockSpec(w.shape, lambda i: (0, 0))],
        out_specs=[pl.BlockSpec((tm, c), lambda i: (i, 0)), pl.BlockSpec(w.shape, lambda i: (0, 0))],
        out_shape=[jax.ShapeDtypeStruct((s, c), _BF), jax.ShapeDtypeStruct(w.shape, _F32)],
        scratch_shapes=[pltpu.VMEM((tm + 2 * halo, c), _F32), pltpu.VMEM((tm + 2 * halo, c), _F32)],
        compiler_params=_cp(1),
    )(h, h, d_y, d_y, w)


ATTN_VMEM = 56 * 2 ** 20
ATTN_Q_BLOCK = 4096
ATTN_K_SUB = 256
ATTN_BWD_BLOCK = 2048
ATTN_BWD_SUB = 512


def _chunk_mask(shape, q_axis):
    qi = lax.broadcasted_iota(jnp.int32, shape, q_axis) // CHUNK
    ki = lax.broadcasted_iota(jnp.int32, shape, 1 - q_axis) // CHUNK
    return ki <= qi


def _rows(j, t):
    return pl.ds(pl.multiple_of(j * t, t), t)


def _chunk_mask_at(shape, q_axis, q_off):
    qi = (lax.broadcasted_iota(jnp.int32, shape, q_axis) + q_off) // CHUNK
    ki = lax.broadcasted_iota(jnp.int32, shape, 1 - q_axis) // CHUNK
    return ki <= qi


def _attn_fwd(q_all, kv, kr, *, name):
    s = q_all.shape[0]
    t = min(ATTN_Q_BLOCK, s)
    tk = min(ATTN_K_SUB, t)
    nq, sub, rep = s // t, t // tk, tk // LANE

    def body(qn_ref, qr_ref, kn_ref, kr_ref, v_ref, o_ref, lse_ref, m_sc, acc_sc):
        i = pl.program_id(1)
        m_sc[...] = jnp.full(m_sc.shape, -1e30, _F32)
        acc_sc[...] = jnp.zeros(acc_sc.shape, _F32)
        ones = jnp.ones((tk, LANE), _BF)

        def block(j, first_row):
            qs = slice(first_row, t)
            rows = _rows(j, tk)
            q = _cat([qn_ref[qs, :], qr_ref[qs, :]])
            sc = _dot_nt(q, _cat([kn_ref[rows, :], kr_ref[rows, :]]))
            if first_row is not None:
                sc = jnp.where(_chunk_mask(sc.shape, 0), sc, -1e30)
            m_prev = m_sc[qs, :]
            m_new = jnp.maximum(m_prev, jnp.max(sc, axis=-1, keepdims=True))
            alpha = jnp.exp(m_prev - m_new)
            p = jnp.exp(sc - _cat([m_new] * rep))
            acc_sc[qs, :] = _cat([alpha, alpha]) * acc_sc[qs, :] + _dot(p, _cat([v_ref[rows, :], ones]))
            m_sc[qs, :] = m_new

        def step(j, carry):
            block(j, None)
            return carry

        lax.fori_loop(0, i * sub, step, 0)
        for u in range(sub):
            block(i * sub + u, u * tk)
        row_sum = acc_sc[:, LANE:2 * LANE]
        o_ref[...] = acc_sc[:, 0:LANE] / row_sum
        lse_ref[...] = m_sc[...] + jnp.log(row_sum)

    return _pcall(
        body, name=name, grid=(HEADS, nq),
        in_specs=[pl.BlockSpec((t, LANE), lambda h, i: (i, h)),
                  pl.BlockSpec((t, LANE), lambda h, i: (i, HEADS + h)),
                  pl.BlockSpec((s, LANE), lambda h, i: (0, h), pipeline_mode=pl.Buffered(1)),
                  pl.BlockSpec((s, LANE), lambda h, i: (0, 0), pipeline_mode=pl.Buffered(1)),
                  pl.BlockSpec((s, LANE), lambda h, i: (0, HEADS + h), pipeline_mode=pl.Buffered(1))],
        out_specs=[pl.BlockSpec((t, LANE), lambda h, i: (i, h)), pl.BlockSpec((t, LANE), lambda h, i: (i, h))],
        out_shape=[jax.ShapeDtypeStruct((s, A_WIDTH), _F32), jax.ShapeDtypeStruct((s, A_WIDTH), _F32)],
        scratch_shapes=[pltpu.VMEM((t, LANE), _F32), pltpu.VMEM((t, 2 * LANE), _F32)],
        compiler_params=pltpu.CompilerParams(dimension_semantics=("arbitrary", "arbitrary"), vmem_limit_bytes=ATTN_VMEM),
    )(q_all, q_all, kv, kr, kv)


def _attn_bwd_prep(o, lse, d_cat, *, name):
    s = o.shape[0]
    t = min(ATTN_BWD_SUB, s)
    nq = s // t

    def body(o_ref, lse_ref, do_ref, dob_ref, lset_ref, dlt_ref):
        for h in range(HEADS):
            sl = slice(LANE * h, LANE * (h + 1))
            rows = slice(SUBLANE * h, SUBLANE * (h + 1))
            do = do_ref[:, sl]
            dl = jnp.broadcast_to(jnp.sum(o_ref[:, sl] * do, axis=-1, keepdims=True), (t, LANE))
            dob_ref[:, sl] = do.astype(dob_ref.dtype)
            dlt_ref[rows, :] = dl.T[0:SUBLANE, :]
            lset_ref[rows, :] = lse_ref[:, sl].T[0:SUBLANE, :]

    wide = pl.BlockSpec((t, A_WIDTH), lambda i: (i, 0))
    stat = pl.BlockSpec((HEADS * SUBLANE, t), lambda i: (i, 0))
    stat_shape = jax.ShapeDtypeStruct((nq * HEADS * SUBLANE, t), _F32)
    return _pcall(
        body, name=name, grid=(nq,), in_specs=[wide, wide, wide], out_specs=[wide, stat, stat],
        out_shape=[jax.ShapeDtypeStruct((s, A_WIDTH), _BF), stat_shape, stat_shape], compiler_params=_cp(1),
    )(o, lse, d_cat)


ATTN_BWD_VMEM = 56 * 2 ** 20


def _attn_bwd(q_all, kv, kr, lse_t, delta_t, d_o, *, name):
    s = q_all.shape[0]
    t = min(ATTN_BWD_BLOCK, s)
    tq = min(ATTN_BWD_SUB, t)
    nk, sub, nqs = s // t, t // tq, s // tq

    def body(kn_ref, kr_ref, v_ref, qn_ref, qr_ref, do_ref, lset_ref, dlt_ref, dkn_ref, dv_ref, dkr_ref, dqn_ref, dqr_ref, dk_sc, dv_sc):
        h, j = pl.program_id(0), pl.program_id(1)
        dk_sc[...] = jnp.zeros(dk_sc.shape, _F32)
        dv_sc[...] = jnp.zeros(dv_sc.shape, _F32)

        @pl.when(j == 0)
        def _():
            dqn_ref[...] = jnp.zeros(dqn_ref.shape, _F32)
            dqr_ref[...] = jnp.zeros(dqr_ref.shape, _F32)

        def block(i, query_off):
            ks = slice(0, t if query_off is None else query_off + tq)
            rows = _rows(i, tq)
            stat = pl.ds(pl.multiple_of((i * HEADS + h) * SUBLANE, SUBLANE), 1)
            q = _cat([qn_ref[rows, :], qr_ref[rows, :]])
            do = do_ref[rows, :]
            k = _cat([kn_ref[ks, :], kr_ref[ks, :]])
            p = jnp.exp(_dot_nt(k, q) - lset_ref[stat, :])
            if query_off is not None:
                p = jnp.where(_chunk_mask_at(p.shape, 1, query_off), p, 0.0)
            dv_sc[ks, :] += _dot(p, do)
            ds = p * (_dot_nt(v_ref[ks, :], do) - dlt_ref[stat, :])
            dk_sc[ks, :] += _dot(ds, q)
            dq = _dot_tn(ds, k)
            dqn_ref[rows, :] += dq[:, 0:LANE]
            dqr_ref[rows, :] += dq[:, LANE:2 * LANE]

        def step(i, carry):
            block(i, None)
            return carry

        for u in range(sub):
            block(j * sub + u, u * tq)
        lax.fori_loop((j + 1) * sub, nqs, step, 0)
        dkn_ref[...] = dk_sc[:, 0:LANE]
        dkr_ref[...] = dk_sc[:, LANE:2 * LANE]
        dv_ref[...] = dv_sc[...]

    once = pl.Buffered(1)
    stats = pl.BlockSpec((nqs * HEADS * SUBLANE, tq), lambda h, j: (0, 0), pipeline_mode=once)
    per_head = lambda at: pl.BlockSpec((s, LANE), at, pipeline_mode=once)
    return _pcall(
        body, name=name, grid=(HEADS, nk),
        in_specs=[pl.BlockSpec((t, LANE), lambda h, j: (j, h)),
                  pl.BlockSpec((t, LANE), lambda h, j: (j, 0)),
                  pl.BlockSpec((t, LANE), lambda h, j: (j, HEADS + h)),
                  per_head(lambda h, j: (0, h)), per_head(lambda h, j: (0, HEADS + h)), per_head(lambda h, j: (0, h)),
                  stats, stats],
        out_specs=[pl.BlockSpec((t, LANE), lambda h, j: (j, h))] * 3 + [per_head(lambda h, j: (0, h))] * 2,
        out_shape=[jax.ShapeDtypeStruct((s, A_WIDTH), _F32)] * 5,
        scratch_shapes=[pltpu.VMEM((t, 2 * LANE), _F32), pltpu.VMEM((t, LANE), _F32)],
        compiler_params=pltpu.CompilerParams(dimension_semantics=("arbitrary", "arbitrary"), vmem_limit_bytes=ATTN_BWD_VMEM),
    )(kv, kr, kv, q_all, q_all, d_o, lse_t, delta_t)


def _final_loss(x, tgt, gain, *, name, tm=512):
    tm = min(tm, x.shape[0])

    def fn(x_, t_, g_):
        def f(xx, gg):
            err = _rms(xx, gg) - t_
            return 0.5 * jnp.sum(jnp.sum(err * err, axis=1, keepdims=True) / D_MODEL, axis=0, keepdims=True)

        loss, pull = jax.vjp(f, x_, g_)
        dx, dg = pull(jnp.ones((1, 1), _F32))
        return dx, dg, jnp.broadcast_to(loss, (SUBLANE, LANE))

    return _rowmap(fn, [_t(x), _t(tgt)], [gain], [(None, D_MODEL, _F32)], [(gain.shape, _F32), ((SUBLANE, LANE), _F32)], tm=tm, name=name)


def _local_step(x, mem, cos_t, sin_t, tgt, p):
    s = x.shape[0]
    g = {}
    row = lambda a: a.reshape(1, -1)
    tm_e = min(512, s)

    mem_n = _rms_fwd(mem, row(p["mem_norm"]), name="mem_norm")
    mem_kv_all = _mm_nn(mem_n, p["w_mem_all"], out_dtype=_F32, name="mem_kv", tn=1024)
    mem_kv = [mem_kv_all[:, 2 * MEM_WIDTH * l:2 * MEM_WIDTH * (l + 1)] for l in range(DEPTH)]

    sv = []
    for i in range(N_A):
        l = i
        r = {}
        r["x0"] = x
        x, r["ffn1"] = _ffn_fwd(x, row(p["ffn1_norm"][l]), p["ffn_w_gu"], p["ffn_w_down"], l, name=f"a{i}_ffn1")
        r["x1"] = x
        xn = _rms_fwd(x, row(p["mix_norm"][l]), name=f"a{i}_mixnorm")
        h = _mm_nn(xn, p["a_w_in"][i], out_dtype=_F32, name=f"a{i}_in", tn=1152)
        qkv_c = _conv_fwd(h, p["a_conv"][i], name=f"a{i}_conv")
        alog, dtb = p["a_A_log_row"][i], p["a_dt_bias_row"][i]
        q, k, v, bg = _rowmap(_gdn_prep_fwd_fn, [_t(qkv_c), _t(h, LANE, 26)], [alog, dtb],
                              [(None, A_WIDTH, _F32)] * 3 + [(None, LANE, _F32)], tm=tm_e, name=f"a{i}_prep")
        w_, u_, qd, kd, qk, dcrow = _rowmap(_gdn_intra_fwd_fn, [_t(q), _t(k), _t(v), _t(bg)], [],
                                            [(None, A_WIDTH, _F32)] * 5 + [(None, LANE, _F32)], tm=_PAIR, name=f"a{i}_intra")
        o, states = _rowmap(_gdn_scan_fwd_fn, [_t(qd), _t(kd), _t(w_), _t(u_), _t(qk), _t(dcrow)], [],
                            [(None, A_WIDTH, _F32), (A_WIDTH, HEAD_DIM, _F32)], tm=_PAIR, name=f"a{i}_scan",
                            carry=[((A_WIDTH, HEAD_DIM), _F32)])
        gain_o = row(p["a_out_norm"][i])
        cat = _rowmap(_a_post_fwd_fn, [_t(o), _t(h, A_WIDTH, 3), _t(h, MEM_WIDTH, 12)], [gain_o, mem_kv[l]],
                      [(None, D_MODEL, _BF)], tm=tm_e, name=f"a{i}_post")[0]
        x = _mm_nn(cat, p["w_out"][l], out_dtype=_F32, name=f"a{i}_out", res=x, tn=1024)
        r.update(xn=xn, h=h, qkv_c=qkv_c, q=q, k=k, v=v, bg=bg, w=w_, u=u_, qd=qd, kd=kd, qk=qk, dcrow=dcrow, o=o, states=states, cat=cat)
        r["x2"] = x
        x, r["ffn2"] = _ffn_fwd(x, row(p["ffn2_norm"][l]), p["ffn_w_gu"], p["ffn_w_down"], DEPTH + l, name=f"a{i}_ffn2")
        sv.append(r)

    kvs = {"x": x}
    xn_kv = _rms_fwd(x, row(p["kv_in_norm"]), name="kv_innorm")
    ckr = _mm_nn(xn_kv, p["w_dkv"], out_dtype=_F32, name="kv_down")
    ckv, k_rope = _rowmap(_kv_prep_fwd_fn, [_t(ckr), _t(cos_t), _t(sin_t)], [row(p["kv_lat_norm"])],
                          [(None, KV_LORA, _BF), (None, LANE, _BF)], tm=tm_e, name="kv_prep")
    kvu = _mm_nn(ckv, p["w_ukv"], out_dtype=_BF, name="kv_up")
    kvs.update(xn=xn_kv, ckr=ckr, ckv=ckv)

    for j in range(N_B):
        l = N_A + j
        r = {}
        x, r["ffn1"] = _ffn_fwd(x, row(p["ffn1_norm"][l]), p["ffn_w_gu"], p["ffn_w_down"], l, name=f"b{j}_ffn1")
        r["x1"] = x
        xn = _rms_fwd(x, row(p["mix_norm"][l]), name=f"b{j}_mixnorm")
        h = _mm_nn(xn, p["b_w_in"][j], out_dtype=_F32, name=f"b{j}_in")
        gain_q = row(p["b_q_norm"][j])
        cqn = _rowmap(lambda c_, g_: (_rms(c_, g_),), [_t(h, Q_LORA, 0)], [gain_q], [(None, Q_LORA, _BF)], tm=tm_e, name=f"b{j}_qnorm")[0]
        qf = _mm_nn(cqn, p["b_w_uq"][j], out_dtype=_F32, name=f"b{j}_uq")
        q_all = _rowmap(_rope_q_fwd_fn, [_t(qf), _t(cos_t), _t(sin_t)], [], [(None, UQ_PAD, _BF)], tm=tm_e, name=f"b{j}_rope")[0]
        o_b, lse = _attn_fwd(q_all, kvu, k_rope, name=f"b{j}_attn")
        cat = _rowmap(_b_post_fwd_fn, [_t(o_b), _t(h, MEM_WIDTH, 1)], [mem_kv[l]], [(None, D_MODEL, _BF)], tm=tm_e, name=f"b{j}_post")[0]
        x = _mm_nn(cat, p["w_out"][l], out_dtype=_F32, name=f"b{j}_out", res=x, tn=1024)
        r.update(xn=xn, h=h, cqn=cqn, q_all=q_all, o_b=o_b, lse=lse, cat=cat)
        x, r["ffn2"] = _ffn_fwd(x, row(p["ffn2_norm"][l]), p["ffn_w_gu"], p["ffn_w_down"], DEPTH + l, name=f"b{j}_ffn2")
        sv.append(r)

    dx, g["final_norm"], loss = _final_loss(x, tgt, row(p["final_norm"]), name="loss")

    per_layer = lambda: [None] * DEPTH
    for n in ("ffn1_norm", "mix_norm", "ffn2_norm", "w_out", "mem_kv"):
        g[n] = per_layer()
    g["ffn_w_gu"] = jnp.zeros((N_CHIPS, 2 * DEPTH, D_MODEL, FFN_COL_TILE), _F32)
    g["ffn_w_down"] = jnp.zeros((N_CHIPS, 2 * DEPTH, FFN_SHARD_ROWS, D_MODEL), _F32)
    for n in ("a_w_in", "a_conv", "a_A_log_row", "a_dt_bias_row", "a_out_norm", "b_w_in", "b_q_norm", "b_w_uq"):
        g[n] = [None] * N_A
    d_kv_parts = []

    for j in reversed(range(N_B)):
        l = N_A + j
        r = sv[l]
        dx, g["ffn2_norm"][l], g["ffn_w_gu"], g["ffn_w_down"] = _ffn_bwd(
            dx, r["ffn2"], row(p["ffn2_norm"][l]), p["ffn_w_gu"], p["ffn_w_down"], DEPTH + l, g["ffn_w_gu"], g["ffn_w_down"], name=f"b{j}_ffn2b")
        d_cat = _mm_nt(dx, p["w_out"][l], out_dtype=_F32, name=f"b{j}_dcat", tn=1024)
        g["w_out"][l] = _mm_tn(r["cat"], dx, name=f"b{j}_dwout", tn=1024)
        d_qm, g["mem_kv"][l] = _rowmap(_b_post_bwd_fn, [_t(r["h"], MEM_WIDTH, 1), _t(d_cat, MEM_WIDTH, 3)], [mem_kv[l]],
                                      [(None, MEM_WIDTH, _BF)], [((N_MEM, 2 * MEM_WIDTH), _F32)], tm=tm_e, name=f"b{j}_postb")
        d_o, lse_t, delta_t = _attn_bwd_prep(r["o_b"], r["lse"], d_cat, name=f"b{j}_delta")
        dkn, dv, dkr, dqn, dqr = _attn_bwd(r["q_all"], kvu, k_rope, lse_t, delta_t, d_o, name=f"b{j}_attn_bwd")
        d_kv_parts.append((dkn, dv, dkr))
        d_qf = _rowmap(_rope_q_bwd_fn, [_t(dqn), _t(dqr), _t(cos_t), _t(sin_t)], [], [(None, UQ_PAD, _F32)], tm=tm_e, name=f"b{j}_ropeb")[0]
        d_cqn = _mm_nt(d_qf, p["b_w_uq"][j], out_dtype=_F32, name=f"b{j}_dcqn")
        g["b_w_uq"][j] = _mm_tn(r["cqn"], d_qf, name=f"b{j}_dwuq")
        gain_q = row(p["b_q_norm"][j])
        d_cq, g["b_q_norm"][j] = _rowmap(lambda c_, d_, g_: _vjp(_rms, [c_, g_], (d_,)), [_t(r["h"], Q_LORA, 0), _t(d_cqn)], [gain_q],
                                        [(None, Q_LORA, _BF)], [((1, Q_LORA), _F32)], tm=tm_e, name=f"b{j}_qnormb")
        d_h = jnp.concatenate([d_cq, d_qm], axis=1)
        d_xn = _mm_nt(d_h, p["b_w_in"][j], out_dtype=_F32, name=f"b{j}_dxn", tn=1024)
        g["b_w_in"][j] = _mm_tn(r["xn"], d_h, name=f"b{j}_dwin")
        dx, g["mix_norm"][l] = _rms_bwd(r["x1"], d_xn, dx, row(p["mix_norm"][l]), name=f"b{j}_mixnormb")
        dx, g["ffn1_norm"][l], g["ffn_w_gu"], g["ffn_w_down"] = _ffn_bwd(
            dx, r["ffn1"], row(p["ffn1_norm"][l]), p["ffn_w_gu"], p["ffn_w_down"], l, g["ffn_w_gu"], g["ffn_w_down"], name=f"b{j}_ffn1b")

    def kv_sum(*parts):
        dkn = sum(parts[0::3][1:], parts[0])
        dv = sum(parts[1::3][1:], parts[1])
        dkr = sum(parts[2::3][1:], parts[2])
        return _cat([dkn, dv]), sum(_heads(dkr, HEADS)[1:], _heads(dkr, HEADS)[0])

    d_kvu, d_kr = _rowmap(kv_sum, [_t(a) for part in d_kv_parts for a in part], [], [(None, 2 * A_WIDTH, _F32), (None, LANE, _F32)],
                          tm=tm_e, name="kv_dsum")
    d_ckv = _mm_nt(d_kvu, p["w_ukv"], out_dtype=_F32, name="kv_dckv")
    g["w_ukv"] = _mm_tn(kvs["ckv"], d_kvu, name="kv_dwukv")
    d_ckr, g["kv_lat_norm"] = _rowmap(_kv_prep_bwd_fn, [_t(kvs["ckr"]), _t(d_ckv), _t(d_kr), _t(cos_t), _t(sin_t)], [row(p["kv_lat_norm"])],
                                     [(None, DKV_PAD, _F32)], [((1, KV_LORA), _F32)], tm=tm_e, name="kv_prepb")
    d_xn = _mm_nt(d_ckr, p["w_dkv"], out_dtype=_F32, name="kv_dxn", tn=1024)
    g["w_dkv"] = _mm_tn(kvs["xn"], d_ckr, name="kv_dwdkv")
    dx, g["kv_in_norm"] = _rms_bwd(kvs["x"], d_xn, dx, row(p["kv_in_norm"]), name="kv_innormb")

    for i in reversed(range(N_A)):
        l = i
        r = sv[l]
        dx, g["ffn2_norm"][l], g["ffn_w_gu"], g["ffn_w_down"] = _ffn_bwd(
            dx, r["ffn2"], row(p["ffn2_norm"][l]), p["ffn_w_gu"], p["ffn_w_down"], DEPTH + l, g["ffn_w_gu"], g["ffn_w_down"], name=f"a{i}_ffn2b")
        d_cat = _mm_nt(dx, p["w_out"][l], out_dtype=_F32, name=f"a{i}_dcat", tn=1024)
        g["w_out"][l] = _mm_tn(r["cat"], dx, name=f"a{i}_dwout", tn=1024)
        gain_o = row(p["a_out_norm"][i])
        h = r["h"]
        d_o, d_hpart, g["a_out_norm"][i], g["mem_kv"][l] = _rowmap(
            _a_post_bwd_fn, [_t(r["o"]), _t(h, A_WIDTH, 3), _t(h, MEM_WIDTH, 12), _t(d_cat)], [gain_o, mem_kv[l]],
            [(None, A_WIDTH, _F32), (None, D_MODEL, _BF)], [((1, HEAD_DIM), _F32), ((N_MEM, 2 * MEM_WIDTH), _F32)], tm=tm_e, name=f"a{i}_postb")
        d_qd, d_kd, d_w, d_u, d_qk, d_dcrow = _rowmap(
            _gdn_scan_bwd_fn, [_t(r["qd"]), _t(r["kd"]), _t(r["w"]), _t(r["u"]), _t(r["qk"]), _t(r["dcrow"]), _t(r["states"], rows=A_WIDTH), _t(d_o)], [],
            [(None, A_WIDTH, _F32)] * 5 + [(None, LANE, _F32)], tm=_PAIR, name=f"a{i}_scanb", carry=[((A_WIDTH, HEAD_DIM), _F32)], reverse=True)
        d_q, d_k, d_v, d_bg = _rowmap(
            _gdn_intra_bwd_fn, [_t(r["q"]), _t(r["k"]), _t(r["v"]), _t(r["bg"]), _t(d_w), _t(d_u), _t(d_qd), _t(d_kd), _t(d_qk), _t(d_dcrow)], [],
            [(None, A_WIDTH, _F32)] * 3 + [(None, LANE, _F32)], tm=_PAIR, name=f"a{i}_intrab")
        alog, dtb = p["a_A_log_row"][i], p["a_dt_bias_row"][i]
        d_qkv_c, d_ba, g["a_A_log_row"][i], g["a_dt_bias_row"][i] = _rowmap(
            _gdn_prep_bwd_fn, [_t(r["qkv_c"]), _t(h, LANE, 26), _t(d_q), _t(d_k), _t(d_v), _t(d_bg)], [alog, dtb],
            [(None, 3 * A_WIDTH, _F32), (None, LANE, _BF)], [((1, LANE), _F32), ((1, LANE), _F32)], tm=tm_e, name=f"a{i}_prepb")
        d_qkv, g["a_conv"][i] = _conv_bwd(h, d_qkv_c, p["a_conv"][i], name=f"a{i}_convb")
        d_h = jnp.concatenate([d_qkv, d_hpart, d_ba], axis=1)
        d_xn = _mm_nt(d_h, p["a_w_in"][i], out_dtype=_F32, name=f"a{i}_dxn", tn=1024)
        g["a_w_in"][i] = _mm_tn(r["xn"], d_h, name=f"a{i}_dwin", tn=1152)
        dx, g["mix_norm"][l] = _rms_bwd(r["x1"], d_xn, dx, row(p["mix_norm"][l]), name=f"a{i}_mixnormb")
        dx, g["ffn1_norm"][l], g["ffn_w_gu"], g["ffn_w_down"] = _ffn_bwd(
            dx, r["ffn1"], row(p["ffn1_norm"][l]), p["ffn_w_gu"], p["ffn_w_down"], l, g["ffn_w_gu"], g["ffn_w_down"], name=f"a{i}_ffn1b")

    d_mem_kv_all = jnp.concatenate(g.pop("mem_kv"), axis=1)
    d_mem_n = _mm_nt(d_mem_kv_all, p["w_mem_all"], out_dtype=_F32, name="mem_dn", tn=1024)
    g["w_mem_all"] = _mm_tn(mem_n, d_mem_kv_all, name="mem_dw", tn=1024)
    _, g["mem_norm"] = _rms_bwd(mem, d_mem_n, None, row(p["mem_norm"]), name="mem_normb")
    return loss, dx, g


_NOPE_ROPE = HEAD_DIM + QK_ROPE
_QKV_GATE = 4 * A_WIDTH
_BETA_AT = _QKV_GATE + MEM_WIDTH


def _lane_row(vals, at):
    n = vals.shape[0]
    return jnp.concatenate([jnp.zeros((at,), _F32), vals.astype(_F32), jnp.zeros((LANE - at - n,), _F32)]).reshape(1, LANE)


def _compute_form(w, conv_f32, ffn_w_gu, ffn_w_down):
    p = {n: w[n] for n in ("ffn1_norm", "mix_norm", "ffn2_norm", "w_out", "mem_norm", "a_out_norm", "b_w_in", "b_q_norm", "kv_in_norm",
                           "kv_lat_norm", "final_norm")}
    p["ffn_w_gu"], p["ffn_w_down"] = ffn_w_gu, ffn_w_down
    wm = w["w_mem_kv"]
    p["w_mem_all"] = jnp.transpose(wm, (1, 0, 2)).reshape(D_MODEL, DEPTH * 2 * MEM_WIDTH)
    a = w["a_w_in"]
    pad = jnp.zeros((N_A, D_MODEL, A_IN_PAD - A_IN), a.dtype)
    p["a_w_in"] = jnp.concatenate([a[:, :, :_QKV_GATE], a[:, :, _QKV_GATE + 2 * HEADS:], a[:, :, _QKV_GATE:_QKV_GATE + 2 * HEADS], pad], axis=2)
    p["a_conv"] = jnp.concatenate([conv_f32, jnp.zeros((N_A, SUBLANE - CONV_K, 3 * A_WIDTH), _F32)], axis=1)
    p["a_A_log_row"] = [_lane_row(w["a_A_log"][i], HEADS) for i in range(N_A)]
    p["a_dt_bias_row"] = [_lane_row(w["a_dt_bias"][i], HEADS) for i in range(N_A)]
    uq = w["b_w_uq"].reshape(N_B, Q_LORA, HEADS, _NOPE_ROPE)
    rope = jnp.concatenate([uq[..., HEAD_DIM:], jnp.zeros((N_B, Q_LORA, HEADS, LANE - QK_ROPE), uq.dtype)], axis=-1)
    p["b_w_uq"] = jnp.concatenate([uq[..., :HEAD_DIM].reshape(N_B, Q_LORA, A_WIDTH), rope.reshape(N_B, Q_LORA, A_WIDTH)], axis=-1)
    dkv = w["w_dkv"]
    p["w_dkv"] = jnp.concatenate([dkv, jnp.zeros((D_MODEL, DKV_PAD - dkv.shape[1]), dkv.dtype)], axis=1)
    ukv = w["w_ukv"].reshape(KV_LORA, HEADS, 2 * HEAD_DIM)
    p["w_ukv"] = jnp.concatenate([ukv[..., :HEAD_DIM].reshape(KV_LORA, A_WIDTH), ukv[..., HEAD_DIM:].reshape(KV_LORA, A_WIDTH)], axis=-1)
    return p


def _natural_grads(g):
    st = lambda xs: jnp.stack(xs, axis=0)
    n = {}
    for k in ("ffn1_norm", "mix_norm", "ffn2_norm"):
        n[k] = st(g[k]).reshape(DEPTH, D_MODEL)
    for k in ("w_out", "b_w_in"):
        n[k] = st(g[k])
    n["mem_norm"] = g["mem_norm"].reshape(D_MODEL)
    n["w_mem_kv"] = jnp.transpose(g["w_mem_all"].reshape(D_MODEL, DEPTH, 2 * MEM_WIDTH), (1, 0, 2))
    a = st(g["a_w_in"])
    n["a_w_in"] = jnp.concatenate([a[:, :, :_QKV_GATE], a[:, :, _BETA_AT:_BETA_AT + 2 * HEADS], a[:, :, _QKV_GATE:_BETA_AT]], axis=2)
    n["a_conv"] = st(g["a_conv"])[:, :CONV_K]
    n["a_A_log"] = st(g["a_A_log_row"])[:, 0, HEADS:2 * HEADS]
    n["a_dt_bias"] = st(g["a_dt_bias_row"])[:, 0, HEADS:2 * HEADS]
    n["a_out_norm"] = st(g["a_out_norm"]).reshape(N_A, HEAD_DIM)
    n["b_q_norm"] = st(g["b_q_norm"]).reshape(N_B, Q_LORA)
    uq = st(g["b_w_uq"])
    nope = uq[:, :, :A_WIDTH].reshape(N_B, Q_LORA, HEADS, HEAD_DIM)
    rope = uq[:, :, A_WIDTH:].reshape(N_B, Q_LORA, HEADS, LANE)[..., :QK_ROPE]
    n["b_w_uq"] = jnp.concatenate([nope, rope], axis=-1).reshape(N_B, Q_LORA, HEADS * _NOPE_ROPE)
    n["kv_in_norm"] = g["kv_in_norm"].reshape(D_MODEL)
    n["w_dkv"] = g["w_dkv"][:, :KV_LORA + QK_ROPE]
    n["kv_lat_norm"] = g["kv_lat_norm"].reshape(KV_LORA)
    ukv = g["w_ukv"]
    n["w_ukv"] = jnp.concatenate([ukv[:, :A_WIDTH].reshape(KV_LORA, HEADS, HEAD_DIM), ukv[:, A_WIDTH:].reshape(KV_LORA, HEADS, HEAD_DIM)],
                                 axis=-1).reshape(KV_LORA, HEADS * 2 * HEAD_DIM)
    n["final_norm"] = g["final_norm"].reshape(D_MODEL)
    return n


def _rope_tables(positions):
    inv = ROPE_THETA ** (-jnp.arange(0, QK_ROPE, 2, dtype=_F32) / QK_ROPE)
    ang = positions.astype(_F32)[:, None] * inv
    z = jnp.zeros((positions.shape[0], LANE - QK_ROPE), _F32)
    cos, sin = jnp.cos(ang), jnp.sin(ang)
    return jnp.concatenate([cos, cos, z], axis=1), jnp.concatenate([sin, sin, z], axis=1)


_HBM = pl.BlockSpec(memory_space=pltpu.HBM)


def _place():
    x, y, c = lax.axis_index("x"), lax.axis_index("y"), lax.axis_index("c")
    return x, y, c, [(1 - x, y), (x, 1 - y), (1 - x, 1 - y)]


def _remote(src, dst, send_sem, recv_sem, to):
    return pltpu.make_async_remote_copy(src_ref=src, dst_ref=dst, send_sem=send_sem, recv_sem=recv_sem, device_id=to, device_id_type=_MESH)


def _gather_over_chips(shard, *, name):
    rows, cols = shard.shape
    half = rows // 2

    def body(w_ref, out_ref, send_sems, recv_sems):
        x, y, c, chips = _place()
        k = 2 * x + y

        def part(chip, h):
            return out_ref.at[chip, pl.ds(h * half, half), :]

        first = [_remote(w_ref.at[pl.ds(c * half, half), :], part(k, c), send_sems.at[j], recv_sems.at[j], (px, py, c))
                 for j, (px, py) in enumerate(chips)]
        for cp in first:
            cp.start()
        passed = []
        for j, (px, py) in enumerate(chips):
            got = part(2 * px + py, c)
            _remote(got, got, send_sems.at[j], recv_sems.at[j], (px, py, c)).wait_recv()
            fw = _remote(got, got, send_sems.at[3 + j], recv_sems.at[3 + j], (x, y, 1 - c))
            fw.start()
            passed.append(fw)
        for j, (px, py) in enumerate(chips):
            got = part(2 * px + py, 1 - c)
            _remote(got, got, send_sems.at[3 + j], recv_sems.at[3 + j], (x, y, 1 - c)).wait_recv()
        for cp in first + passed:
            cp.wait_send()

    others = _pcall(
        body, name=name, in_specs=[_HBM], out_specs=_HBM, out_shape=jax.ShapeDtypeStruct((N_CHIPS, rows, cols), shard.dtype),
        scratch_shapes=[pltpu.SemaphoreType.DMA((6,)), pltpu.SemaphoreType.DMA((6,))],
    )(shard)
    return lax.dynamic_update_slice(others, shard[None], (2 * lax.axis_index("x") + lax.axis_index("y"), 0, 0))


PAIR_COPIES = 4


def _scatter_over_chips(v, *, name):
    def body(v_ref, out_ref, send_sems, recv_sems):
        x, y, c, chips = _place()
        cps = [_remote(v_ref.at[2 * px + py], out_ref.at[j], send_sems.at[j], recv_sems.at[j], (px, py, c)) for j, (px, py) in enumerate(chips)]
        for cp in cps:
            cp.start()
        for cp in cps:
            cp.wait()

    return _pcall(body, name=name, in_specs=[_HBM], out_specs=_HBM, out_shape=jax.ShapeDtypeStruct((N_CHIPS - 1,) + v.shape[1:], v.dtype),
                  scratch_shapes=[pltpu.SemaphoreType.DMA((3,)), pltpu.SemaphoreType.DMA((3,))])(v)


def _all_reduce_small(v, *, name):
    def body(v_ref, out_ref, all_ref, send_sems, recv_sems):
        x, y, c, _ = _place()
        me = 4 * x + 2 * y + c
        all_ref[me] = v_ref[...]
        cps = []
        for f in range(1, N_DEV):
            fx, fy, fc = (f >> 2) & 1, (f >> 1) & 1, f & 1
            to = (x + fx - 2 * x * fx, y + fy - 2 * y * fy, c + fc - 2 * c * fc)
            cps.append(_remote(v_ref, all_ref.at[me], send_sems.at[f - 1], recv_sems.at[f - 1], to))
        for cp in cps:
            cp.start()
        for cp in cps:
            cp.wait()
        acc = all_ref[0]
        for d in range(1, N_DEV):
            acc = acc + all_ref[d]
        out_ref[...] = acc

    vm = pl.BlockSpec(memory_space=pltpu.VMEM)
    return _pcall(body, name=name, in_specs=[vm], out_specs=vm, out_shape=jax.ShapeDtypeStruct(v.shape, v.dtype),
                  scratch_shapes=[pltpu.VMEM((N_DEV,) + v.shape, v.dtype), pltpu.SemaphoreType.DMA((N_DEV - 1,)), pltpu.SemaphoreType.DMA((N_DEV - 1,))])(v)


_FFN_GU = ("ffn1_w_gu", "ffn2_w_gu")
_FFN_DOWN = ("ffn1_w_down", "ffn2_w_down")
_BIG = (("w_out", 1), ("w_mem_kv", 1), ("a_w_in", 2), ("a_conv", 2), ("b_w_in", 1), ("b_w_uq", 2), ("w_dkv", 0),
        ("w_ukv", 1))
_SMALL = ("ffn1_norm", "mix_norm", "ffn2_norm", "mem_norm", "a_A_log", "a_dt_bias", "a_out_norm", "b_q_norm", "kv_in_norm", "kv_lat_norm",
          "final_norm")
_WEIGHTS = ("ffn1_norm", "ffn1_w_gu", "ffn1_w_down", "mix_norm", "ffn2_norm", "ffn2_w_gu", "ffn2_w_down", "w_out", "mem_norm", "w_mem_kv",
            "a_w_in", "a_conv", "a_A_log", "a_dt_bias", "a_out_norm", "b_w_in", "b_q_norm", "b_w_uq", "kv_in_norm", "w_dkv", "kv_lat_norm",
            "w_ukv", "final_norm")


PACK_PIECE_ROWS = 16


def _piece_rows(shape):
    return -(-math.prod(shape) // (PACK_COLS * PACK_PIECE_ROWS)) * PACK_PIECE_ROWS


def _packed_rows(shapes):
    return sum(_piece_rows(s) for s in shapes)


def _pack(arrs, rows):
    pieces = []
    for a in arrs:
        n, r = a.size, _piece_rows(a.shape)
        flat = a.reshape(-1)
        if r * PACK_COLS != n:
            flat = jnp.concatenate([flat, jnp.zeros((r * PACK_COLS - n,), a.dtype)])
        pieces.append(flat.reshape(r, PACK_COLS))
    used = sum(p.shape[0] for p in pieces)
    if rows > used:
        pieces.append(jnp.zeros((rows - used, PACK_COLS), arrs[0].dtype))
    return jnp.concatenate(pieces, axis=0)


def _unpack(packed, shapes):
    off, out = 0, []
    for shp in shapes:
        n, r = math.prod(shp), _piece_rows(shp)
        piece = packed[off:off + r]
        out.append((piece if r * PACK_COLS == n else piece.reshape(-1)[:n]).reshape(shp))
        off += r
    return out


def _adamw_math(w_, g_, m_, v_):
    m2 = ADAM_B1 * m_ + (1.0 - ADAM_B1) * g_
    v2 = ADAM_B2 * v_ + (1.0 - ADAM_B2) * (g_ * g_)
    m_hat = m2 / (1.0 - ADAM_B1 ** ADAM_STEP)
    v_hat = v2 / (1.0 - ADAM_B2 ** ADAM_STEP)
    return -ADAM_LR * (m_hat / (jnp.sqrt(v_hat) + ADAM_EPS) + ADAM_WD * w_), m2, v2


def _adamw(w, g, m, v, *, name):
    return _rowmap(_adamw_math, [_t(w), _t(g), _t(m), _t(v)], [], [(None, w.shape[1], _F32)] * 3, tm=min(PACK_ROW_TILE, w.shape[0]), name=name)


def _pair_exchange_half(g, *, name):
    n, _, h, cols = g.shape

    def body(g_ref, out_ref, send_sems, recv_sems):
        x, y, c, _ = _place()
        cps = [_remote(g_ref.at[k, 1 - c], out_ref.at[k], send_sems.at[k], recv_sems.at[k], (x, y, 1 - c)) for k in range(n)]
        for cp in cps:
            cp.start()
        for cp in cps:
            cp.wait()

    return _pcall(body, name=name, in_specs=[_HBM], out_specs=_HBM, out_shape=jax.ShapeDtypeStruct((n, h, cols), g.dtype),
                  scratch_shapes=[pltpu.SemaphoreType.DMA((n,)), pltpu.SemaphoreType.DMA((n,))])(g)


def _add_half(g, other, c, *, name):
    n, _, h, cols = g.shape
    tm = min(PACK_ROW_TILE, h)

    def body(c_ref, g_ref, o_ref, sum_ref, narrow_ref):
        acc = g_ref[...] + o_ref[...]
        sum_ref[...] = acc
        narrow_ref[...] = acc.astype(narrow_ref.dtype)

    slab = pl.BlockSpec((None, tm, cols), lambda k, i, c_ref: (k, i, 0))
    return _pcall(
        body, name=name,
        grid_spec=pltpu.PrefetchScalarGridSpec(
            num_scalar_prefetch=1, grid=(n, h // tm),
            in_specs=[pl.BlockSpec((None, None, tm, cols), lambda k, i, c_ref: (k, c_ref[0], i, 0)), slab], out_specs=[slab, slab]),
        out_shape=[jax.ShapeDtypeStruct((n, h, cols), _F32), jax.ShapeDtypeStruct((n, h, cols), jnp.bfloat16)],
        compiler_params=_cp(2),
    )(jnp.reshape(c, (1,)).astype(jnp.int32), g, other)


def _add_own(chip_sum, from_chips, chip, *, name):
    _, h, cols = chip_sum.shape
    tm = min(PACK_ROW_TILE, h)

    def body(k_ref, own_ref, a_ref, b_ref, c_ref, o_ref):
        o_ref[...] = ((own_ref[...] + a_ref[...].astype(_F32)) + b_ref[...].astype(_F32)) + c_ref[...].astype(_F32)

    sent = [pl.BlockSpec((None, tm, cols), functools.partial(lambda i, k_ref, j: (j, i, 0), j=j)) for j in range(N_CHIPS - 1)]
    return _pcall(
        body, name=name,
        grid_spec=pltpu.PrefetchScalarGridSpec(
            num_scalar_prefetch=1, grid=(h // tm,),
            in_specs=[pl.BlockSpec((None, tm, cols), lambda i, k_ref: (k_ref[0], i, 0))] + sent,
            out_specs=pl.BlockSpec((tm, cols), lambda i, k_ref: (i, 0))),
        out_shape=jax.ShapeDtypeStruct((h, cols), _F32), compiler_params=_cp(1),
    )(jnp.reshape(chip, (1,)).astype(jnp.int32), chip_sum, from_chips, from_chips, from_chips)


def _pair_gather(mine, *, name):
    h, cols = mine.shape
    per = h // PAIR_COPIES
    assert per * PAIR_COPIES == h and per % SUBLANE == 0, mine.shape

    def body(v_ref, out_ref, send_sems, recv_sems):
        x, y, c, _ = _place()
        cps = [_remote(v_ref.at[pl.ds(q * per, per), :], out_ref.at[c, pl.ds(q * per, per), :], send_sems.at[q], recv_sems.at[q], (x, y, 1 - c))
               for q in range(PAIR_COPIES)]
        for cp in cps:
            cp.start()
        for q, cp in enumerate(cps):
            cp.wait_send()
            _remote(v_ref.at[pl.ds(q * per, per), :], out_ref.at[1 - c, pl.ds(q * per, per), :], send_sems.at[q], recv_sems.at[q],
                    (x, y, 1 - c)).wait_recv()

    both = _pcall(body, name=name, in_specs=[_HBM], out_specs=_HBM, out_shape=jax.ShapeDtypeStruct((2, h, cols), mine.dtype),
                  scratch_shapes=[pltpu.SemaphoreType.DMA((PAIR_COPIES,)), pltpu.SemaphoreType.DMA((PAIR_COPIES,))])(mine)
    return lax.dynamic_update_slice(both, mine[None], (lax.axis_index("c"), 0, 0))


def _reduce_over_devices(g, c, chip, *, name):
    n, rows, cols = g.shape
    g = g.reshape(n, 2, rows // 2, cols)
    chip_sum, narrow = _add_half(g, _pair_exchange_half(g, name=name + "_pair_sum"), c, name=name + "_add_pair")
    mine = _add_own(chip_sum, _scatter_over_chips(narrow, name=name + "_scatter"), chip, name=name + "_add_chips")
    return _pair_gather(mine, name=name + "_pair_gather").reshape(rows, cols)


def _adamw_at(w, m, v, g_all, first_row, *, name):
    tm = min(PACK_ROW_TILE, w.shape[0])
    assert w.shape[0] % tm == 0 and first_row % tm == 0, (name, w.shape, first_row)

    def fn(w_, m_, v_, g_):
        return _adamw_math(w_, g_, m_, v_) + (g_,)

    return _rowmap(fn, [_t(w), _t(m), _t(v), _t(g_all, first=first_row // tm)], [], [(None, w.shape[1], _F32)] * 4, tm=tm, name=name)


def _step(x, mem, positions, loss_target, w, m, v):
    cx, cy, cc = lax.axis_index("x"), lax.axis_index("y"), lax.axis_index("c")
    chip = 2 * cx + cy
    big = [n for n, _ in _BIG]
    shard_shapes = [w[n].shape for n in big]
    rows = -(-_packed_rows(shard_shapes) // (2 * PACK_ROW_TILE)) * 2 * PACK_ROW_TILE
    flat = lambda a: a.reshape(-1, a.shape[-1])

    ffn_w_gu = _gather_over_chips(jnp.concatenate([flat(w[n]) for n in _FFN_GU]).astype(_BF), name="gather_w_gu")
    ffn_w_down = _gather_over_chips(jnp.concatenate([flat(w[n]) for n in _FFN_DOWN]).astype(_BF), name="gather_w_down")
    w_pack = _pack([w[n] for n in big], rows)
    gathered = _gather_over_chips(w_pack.astype(_BF), name="gather_weights")
    pieces = [_unpack(gathered[k], shard_shapes) for k in range(N_CHIPS)]
    full = {n: jnp.concatenate([pieces[k][i] for k in range(N_CHIPS)], axis=ax) for i, (n, ax) in enumerate(_BIG)}
    for n in _SMALL:
        full[n] = w[n]
    conv = w["a_conv"]
    slots = jnp.stack([jnp.where((chip == k) & (cc == 0), conv, 0.0) for k in range(N_CHIPS)])
    conv_all = _unpack(_all_reduce_small(_pack([slots], _piece_rows(slots.shape)), name="gather_conv"), [slots.shape])[0]
    conv_full = jnp.concatenate([conv_all[k] for k in range(N_CHIPS)], axis=2)

    p = _compute_form(full, conv_full, ffn_w_gu, ffn_w_down)
    cos_t, sin_t = _rope_tables(positions[0])
    loss_tile, d_x, g = _local_step(x[0], mem[0], cos_t, sin_t, loss_target[0], p)
    gn = _natural_grads(g)

    def shard_of(a, ax, k):
        size = a.shape[ax] // N_CHIPS
        return lax.slice_in_dim(a, k * size, (k + 1) * size, axis=ax)

    grads, deltas, new_m, new_v = {}, {}, {}, {}
    for names, key in ((_FFN_GU, "ffn_w_gu"), (_FFN_DOWN, "ffn_w_down")):
        buf = g[key]
        reduced = _reduce_over_devices(buf.reshape(N_CHIPS, -1, buf.shape[-1]), cc, chip, name="grad_" + key)
        first = 0
        for n in names:
            d_, m_, v_, g_ = _adamw_at(flat(w[n]), flat(m[n]), flat(v[n]), reduced, first, name="adamw_" + n)
            grads[n], deltas[n], new_m[n], new_v[n] = (t.reshape(w[n].shape) for t in (g_, d_, m_, v_))
            first += flat(w[n]).shape[0]
    g_pack = jnp.stack([_pack([shard_of(gn[n], ax, k) for n, ax in _BIG], rows) for k in range(N_CHIPS)])
    g_big = _reduce_over_devices(g_pack, cc, chip, name="grad_misc")
    d_big, m_big, v_big = _adamw(w_pack, g_big, _pack([m[n] for n in big], rows), _pack([v[n] for n in big], rows), name="adamw_misc")

    small_shapes = [w[n].shape for n in _SMALL]
    small_rows = _packed_rows(small_shapes)
    g_small = _all_reduce_small(_pack([gn[n] for n in _SMALL], small_rows), name="grad_small")
    d_small, m_small, v_small = _adamw(_pack([w[n] for n in _SMALL], small_rows), g_small, _pack([m[n] for n in _SMALL], small_rows),
                                       _pack([v[n] for n in _SMALL], small_rows), name="adamw_small")

    for out, big_pack, small_pack in ((grads, g_big, g_small), (deltas, d_big, d_small), (new_m, m_big, m_small), (new_v, v_big, v_small)):
        out.update(zip(big, _unpack(big_pack, shard_shapes)))
        out.update(zip(_SMALL, _unpack(small_pack, small_shapes)))
    loss = lax.psum(loss_tile[0, 0], ("x", "y", "c"))
    return (loss, d_x[None], *[grads[n] for n in _WEIGHTS], *[deltas[n] for n in _WEIGHTS], *[new_m[n] for n in _WEIGHTS],
            *[new_v[n] for n in _WEIGHTS])


def kernel(x, mem, positions, ffn1_norm, ffn1_w_gu, ffn1_w_down, mix_norm, ffn2_norm, ffn2_w_gu, ffn2_w_down, w_out, mem_norm, w_mem_kv, a_w_in, a_conv, a_A_log, a_dt_bias, a_out_norm, b_w_in, b_q_norm, b_w_uq, kv_in_norm, w_dkv, kv_lat_norm, w_ukv, final_norm, loss_target, m_ffn1_norm, m_ffn1_w_gu, m_ffn1_w_down, m_mix_norm, m_ffn2_norm, m_ffn2_w_gu, m_ffn2_w_down, m_w_out, m_mem_norm, m_w_mem_kv, m_a_w_in, m_a_conv, m_a_A_log, m_a_dt_bias, m_a_out_norm, m_b_w_in, m_b_q_norm, m_b_w_uq, m_kv_in_norm, m_w_dkv, m_kv_lat_norm, m_w_ukv, m_final_norm, v_ffn1_norm, v_ffn1_w_gu, v_ffn1_w_down, v_mix_norm, v_ffn2_norm, v_ffn2_w_gu, v_ffn2_w_down, v_w_out, v_mem_norm, v_w_mem_kv, v_a_w_in, v_a_conv, v_a_A_log, v_a_dt_bias, v_a_out_norm, v_b_w_in, v_b_q_norm, v_b_w_uq, v_kv_in_norm, v_w_dkv, v_kv_lat_norm, v_w_ukv, v_final_norm):
    given = dict(locals())
    w = {n: given[n] for n in _WEIGHTS}
    m = {n: given["m_" + n] for n in _WEIGHTS}
    v = {n: given["v_" + n] for n in _WEIGHTS}
    return _step(x, mem, positions, loss_target, w, m, v)
```

```python
import functools
import math

import jax
import jax.numpy as jnp
from jax import lax
from jax.experimental import pallas as pl
from jax.experimental.pallas import tpu as pltpu

_BF = jnp.bfloat16
_F32 = jnp.float32
_HI = lax.Precision.HIGHEST
_MESH = pl.DeviceIdType.MESH

D_MODEL = 1024
DEPTH = 4
N_A = 2
N_B = 2
CHUNK = 64
EPS = 1e-6
HEADS = 6
HEAD_DIM = 128
A_WIDTH = HEADS * HEAD_DIM
CONV_K = 4
QK_ROPE = 64
Q_LORA = 256
KV_LORA = 256
N_MEM = 256
MEM_HEADS = 4
MEM_HEAD_DIM = 64
MEM_WIDTH = MEM_HEADS * MEM_HEAD_DIM
D_FF = 2816
ROPE_THETA = 10000.0
A_IN = 4 * A_WIDTH + 2 * HEADS + MEM_WIDTH
A_IN_PAD = 3456
UQ_PAD = 2 * A_WIDTH
DKV_PAD = KV_LORA + 128
LANE = 128
SUBLANE = 8

ADAM_LR = 0.001
ADAM_B1 = 0.9
ADAM_B2 = 0.999
ADAM_EPS = 1e-08
ADAM_WD = 0.01
ADAM_STEP = 10

N_CHIPS = 4
N_DEV = 8
PACK_COLS = 1024
PACK_ROW_TILE = 256


def _pcall(body, **kw):
    return pl.pallas_call(body, **kw)


VMEM_LIMIT_V7X = 48 * 2 ** 20
TILE_BYTES = 6 * 2 ** 20


def _cp(grid_rank):
    return pltpu.CompilerParams(dimension_semantics=("arbitrary",) * grid_rank, vmem_limit_bytes=VMEM_LIMIT_V7X)


def _fit_rows(rows, row_bytes):
    while rows > LANE and rows * row_bytes > TILE_BYTES:
        rows //= 2
    return rows


def _fit_cols(n, target, col_bytes):
    return _tile(n, max(LANE, min(target, TILE_BYTES // col_bytes)))


def _tile(n, target):
    best = None
    for t in range(LANE, min(n, target) + 1, LANE):
        if n % t == 0:
            best = t
    return best if best is not None else n


def _dot(a, b):
    return jnp.dot(a.astype(_BF), b.astype(_BF), preferred_element_type=_F32)


def _dot_nt(a, b):
    return lax.dot_general(a.astype(_BF), b.astype(_BF), (((1,), (1,)), ((), ())), preferred_element_type=_F32)


def _dot_tn(a, b):
    return lax.dot_general(a.astype(_BF), b.astype(_BF), (((0,), (0,)), ((), ())), preferred_element_type=_F32)


def _dot_hi(a, b):
    return jnp.dot(a, b, precision=_HI, preferred_element_type=_F32)


def _rowmap(fn, tiles, params, outs, accs=(), *, tm, name, carry=(), reverse=False):
    rows = tiles[0][0].shape[0]
    steps = rows // tm
    nt, npar, no, na, nc = len(tiles), len(params), len(outs), len(accs), len(carry)

    def step_index(i):
        return steps - 1 - i if reverse else i

    in_specs, operands = [], []
    for arr, r, w, cb, first in tiles:
        r = tm if r is None else r
        w = arr.shape[1] if w is None else w
        assert arr.shape[0] >= (first + steps) * r and (w % LANE == 0 or w == arr.shape[1]), (name, arr.shape, r, w)
        in_specs.append(pl.BlockSpec((r, w), functools.partial(lambda i, cb, first: (first + step_index(i), cb), cb=cb, first=first)))
        operands.append(arr)
    for p in params:
        in_specs.append(pl.BlockSpec(p.shape, functools.partial(lambda i, nd: (0,) * nd, nd=p.ndim)))
        operands.append(p)
    out_specs, out_shape = [], []
    for r, cols, dt in outs:
        r = tm if r is None else r
        out_specs.append(pl.BlockSpec((r, cols), lambda i: (step_index(i), 0)))
        out_shape.append(jax.ShapeDtypeStruct((steps * r, cols), dt))
    for shp, dt in accs:
        out_specs.append(pl.BlockSpec(shp, functools.partial(lambda i, nd: (0,) * nd, nd=len(shp))))
        out_shape.append(jax.ShapeDtypeStruct(shp, dt))

    def body(*refs):
        t_refs = refs[:nt]
        p_refs = refs[nt:nt + npar]
        o_refs = refs[nt + npar:nt + npar + no]
        a_refs = refs[nt + npar + no:nt + npar + no + na]
        c_refs = refs[nt + npar + no + na:]
        if na or nc:
            @pl.when(pl.program_id(0) == 0)
            def _():
                for r in a_refs + c_refs:
                    r[...] = jnp.zeros(r.shape, r.dtype)
        vals = fn(*[r[...] for r in t_refs], *[r[...] for r in p_refs], *[r[...] for r in c_refs])
        vals = tuple(vals) if isinstance(vals, (tuple, list)) else (vals,)
        assert len(vals) == no + na + nc, (name, len(vals), no, na, nc)
        for r, v in zip(o_refs, vals[:no]):
            r[...] = v.astype(r.dtype)
        for r, v in zip(a_refs, vals[no:no + na]):
            r[...] += v.astype(r.dtype)
        for r, v in zip(c_refs, vals[no + na:]):
            r[...] = v.astype(r.dtype)

    res = _pcall(
        body, name=name, grid=(steps,), in_specs=in_specs, out_specs=out_specs, out_shape=out_shape,
        scratch_shapes=[pltpu.VMEM(shp, dt) for shp, dt in carry],
        compiler_params=_cp(1),
    )(*operands)
    return res


def _t(arr, width=None, cb=0, rows=None, first=0):
    return (arr, rows, width, cb, first)


def _mm_nn(a, b, *, out_dtype, name, scale=None, res=None, tm=1024, tn=1536):
    m, k = a.shape
    n = b.shape[1]
    tm, tn = _fit_rows(min(tm, m), k * a.dtype.itemsize), _fit_cols(n, tn, k * b.dtype.itemsize)

    def body(a_ref, b_ref, *rest):
        acc = _dot(a_ref[...], b_ref[...])
        if scale is not None:
            acc = acc * scale
        if res is not None:
            acc = acc + rest[0][...]
        rest[-1][...] = acc.astype(rest[-1].dtype)

    in_specs = [pl.BlockSpec((tm, k), lambda i, j: (i, 0)), pl.BlockSpec((k, tn), lambda i, j: (0, j))]
    operands = [a, b]
    if res is not None:
        in_specs.append(pl.BlockSpec((tm, tn), lambda i, j: (i, j)))
        operands.append(res)
    return _pcall(
        body, name=name, grid=(m // tm, n // tn), in_specs=in_specs,
        out_specs=pl.BlockSpec((tm, tn), lambda i, j: (i, j)), out_shape=jax.ShapeDtypeStruct((m, n), out_dtype),
        compiler_params=_cp(2),
    )(*operands)


def _mm_nt(a, b, *, out_dtype, name, scale=None, tm=1024, tn=1536):
    m, k = a.shape
    n = b.shape[0]
    tm, tn = _fit_rows(min(tm, m), k * a.dtype.itemsize), _fit_cols(n, tn, k * b.dtype.itemsize)

    def body(a_ref, b_ref, o_ref):
        acc = _dot_nt(a_ref[...], b_ref[...])
        if scale is not None:
            acc = acc * scale
        o_ref[...] = acc.astype(o_ref.dtype)

    return _pcall(
        body, name=name, grid=(m // tm, n // tn),
        in_specs=[pl.BlockSpec((tm, k), lambda i, j: (i, 0)), pl.BlockSpec((tn, k), lambda i, j: (j, 0))],
        out_specs=pl.BlockSpec((tm, tn), lambda i, j: (i, j)), out_shape=jax.ShapeDtypeStruct((m, n), out_dtype),
        compiler_params=_cp(2),
    )(a, b)


def _mm_tn(a, b, *, name, scale=None, t1=1024, tn=1536, ts=1024):
    s, k1 = a.shape
    n = b.shape[1]
    t1, tn, ts = _tile(k1, t1), _tile(n, tn), min(ts, s)
    steps = s // ts

    def body(a_ref, b_ref, o_ref):
        @pl.when(pl.program_id(2) == 0)
        def _():
            o_ref[...] = jnp.zeros(o_ref.shape, o_ref.dtype)

        o_ref[...] += _dot_tn(a_ref[...], b_ref[...])
        if scale is not None:
            @pl.when(pl.program_id(2) == steps - 1)
            def _():
                o_ref[...] = o_ref[...] * scale

    return _pcall(
        body, name=name, grid=(k1 // t1, n // tn, steps),
        in_specs=[pl.BlockSpec((ts, t1), lambda i, j, r: (r, i)), pl.BlockSpec((ts, tn), lambda i, j, r: (r, j))],
        out_specs=pl.BlockSpec((t1, tn), lambda i, j, r: (i, j)), out_shape=jax.ShapeDtypeStruct((k1, n), _F32),
        compiler_params=_cp(3),
    )(a, b)


def _heads(t, n, w=LANE):
    return [t[:, w * h:w * (h + 1)] for h in range(n)]


def _cat(parts):
    return jnp.concatenate(parts, axis=1)


def _rms(x, g):
    return x * lax.rsqrt(jnp.mean(x * x, axis=-1, keepdims=True) + EPS) * g


def _l2n(x):
    return x * lax.rsqrt(jnp.sum(x * x, axis=-1, keepdims=True) + EPS)


def _sigmoid(x):
    return 0.5 * (jnp.tanh(0.5 * x) + 1.0)


def _silu(x):
    return x * _sigmoid(x)


def _softplus(x):
    return jnp.maximum(x, 0.0) + jnp.log(1.0 + jnp.exp(-jnp.abs(x)))


def _lane_pick(t, h):
    lane = lax.broadcasted_iota(jnp.int32, t.shape, 1)
    return jnp.sum(jnp.where(lane == h, t, 0.0), axis=1, keepdims=True)


def _lane_put(col, h, width=LANE):
    lane = lax.broadcasted_iota(jnp.int32, (col.shape[0], width), 1)
    return jnp.where(lane == h, col, 0.0)


def _vjp(fwd, ins, cts):
    outs, pull = jax.vjp(fwd, *ins)
    outs = outs if isinstance(outs, (tuple, list)) else (outs,)
    cts = tuple(c.astype(o.dtype) for c, o in zip(cts, outs))
    return pull(cts if len(cts) > 1 else cts[0])


def _rot_half_matrix():
    r = lax.broadcasted_iota(jnp.int32, (LANE, LANE), 0)
    c = lax.broadcasted_iota(jnp.int32, (LANE, LANE), 1)
    half = QK_ROPE // 2
    return jnp.where((c < half) & (r == c + half), -1.0, jnp.where((c >= half) & (c < QK_ROPE) & (r == c - half), 1.0, 0.0))


def _rope(x, cos_t, sin_t):
    return x * cos_t + _dot_hi(x, _rot_half_matrix()) * sin_t


def _mem_attn(q, km, vm):
    lane_q = lax.broadcasted_iota(jnp.int32, q.shape, 1)
    lane_v = lax.broadcasted_iota(jnp.int32, vm.shape, 1)
    out = jnp.zeros(q.shape, _F32)
    for h in range(MEM_HEADS):
        lo, hi = MEM_HEAD_DIM * h, MEM_HEAD_DIM * (h + 1)
        qh = jnp.where((lane_q >= lo) & (lane_q < hi), q, 0.0)
        vh = jnp.where((lane_v >= lo) & (lane_v < hi), vm, 0.0)
        sc = _dot_nt(qh, km) * MEM_HEAD_DIM ** -0.5
        sc = sc - lax.stop_gradient(jnp.max(sc, axis=-1, keepdims=True))
        p = jnp.exp(sc)
        p = p / jnp.sum(p, axis=-1, keepdims=True)
        out = out + _dot(p, vh)
    return out


_PAIR = 2 * CHUNK


def _pair_masks():
    ri = lax.broadcasted_iota(jnp.int32, (_PAIR, _PAIR), 0)
    ci = lax.broadcasted_iota(jnp.int32, (_PAIR, _PAIR), 1)
    same = (ri >= CHUNK) == (ci >= CHUNK)
    return same, same & (ri >= ci), same & (ri > ci), ri == ci, same & (ri <= ci)


_NN = (((2,), (1,)), ((0,), (0,)))
_NT = (((2,), (2,)), ((0,), (0,)))
_TN = (((1,), (1,)), ((0,), (0,)))


def _bdot(a, b, dims):
    return lax.dot_general(a.astype(_BF), b.astype(_BF), dims, preferred_element_type=_F32)


def _dot3(a, b, dims):
    a_hi, b_hi = a.astype(_BF), b.astype(_BF)
    a_lo, b_lo = (a - a_hi.astype(_F32)).astype(_BF), (b - b_hi.astype(_F32)).astype(_BF)
    d = lambda x, y: lax.dot_general(x, y, dims, preferred_element_type=_F32)
    return d(a_hi, b_hi) + (d(a_hi, b_lo) + d(a_lo, b_hi))


@jax.custom_vjp
def _mm3(a, b):
    return _dot3(a, b, _NN)


_mm3.defvjp(lambda a, b: (_dot3(a, b, _NN), (a, b)), lambda res, g: (_dot3(g, res[1], _NT), _dot3(res[0], g, _TN)))


def _neumann_inverse(a):
    eye = jnp.where(_pair_masks()[3], 1.0, 0.0)
    n = -a
    t_inv = eye + n
    for _ in range(5):
        n = _dot3(n, n, _NN)
        t_inv = t_inv + _dot3(t_inv, n, _NN)
    return t_inv


@jax.custom_vjp
def _unit_lower_inverse(a):
    return _neumann_inverse(a)


def _unit_lower_inverse_fwd(a):
    t_inv = _neumann_inverse(a)
    return t_inv, t_inv


def _unit_lower_inverse_bwd(t_inv, g):
    return (-_dot3(t_inv, _dot3(g, t_inv, _NT), _TN),)


_unit_lower_inverse.defvjp(_unit_lower_inverse_fwd, _unit_lower_inverse_bwd)


def _gdn_intra_head(q, k, v, beta, gl):
    same, causal, strict, eye, upper = _pair_masks()
    gl_row = jnp.sum(jnp.where(eye, gl, 0.0), axis=-2, keepdims=True)
    g_col = jnp.sum(jnp.where(causal, gl_row, 0.0), axis=-1, keepdims=True)
    g_row = jnp.sum(jnp.where(upper, gl, 0.0), axis=-2, keepdims=True)
    g_last = jnp.sum(jnp.where(same, gl_row, 0.0), axis=-1, keepdims=True)
    decay = jnp.where(causal, jnp.exp(jnp.where(causal, g_col - g_row, 0.0)), 0.0)
    kb = k * beta
    a = jnp.where(strict, _bdot(kb, k, _NT) * decay, 0.0)
    t_inv = _unit_lower_inverse(a)
    e_g = jnp.exp(g_col)
    u = _mm3(t_inv, v * beta)
    w = _mm3(t_inv, kb * e_g)
    qk = _bdot(q, k, _NT) * decay
    return w, u, q * e_g, k * jnp.exp(g_last - g_col), qk, jnp.exp(g_last)


def _gdn_scan_head(s, qd_a, kd_a, w_a, u_a, qk_a, dc_a, qd_b, kd_b, w_b, u_b, qk_b, dc_b):
    zeros = jnp.zeros((HEADS, CHUNK, HEAD_DIM), _F32)
    vn_a = u_a - _bdot(w_a, s, _NN)
    o_a = _bdot(qd_a, s, _NN) + _bdot(qk_a, jnp.concatenate([vn_a, zeros], axis=1), _NN)
    s1 = s * dc_a + _bdot(kd_a, vn_a, _TN)
    vn_b = u_b - _bdot(w_b, s1, _NN)
    o_b = _bdot(qd_b, s1, _NN) + _bdot(qk_b, jnp.concatenate([zeros, vn_b], axis=1), _NN)
    s2 = s1 * dc_b + _bdot(kd_b, vn_b, _TN)
    return o_a, o_b, s2


def _pick_scalar(t, row, lane_i):
    ri = lax.broadcasted_iota(jnp.int32, t.shape, 0)
    ci = lax.broadcasted_iota(jnp.int32, t.shape, 1)
    return jnp.sum(jnp.sum(jnp.where((ri == row) & (ci == lane_i), t, 0.0), axis=1, keepdims=True), axis=0, keepdims=True)


def _put_scalar(val, row, lane_i, shape):
    ri = lax.broadcasted_iota(jnp.int32, shape, 0)
    ci = lax.broadcasted_iota(jnp.int32, shape, 1)
    return jnp.where((ri == row) & (ci == lane_i), val, 0.0)


def _by_head(t):
    return jnp.stack(_heads(t, HEADS))


def _from_heads(t):
    return _cat([t[h] for h in range(HEADS)])


def _state_by_head(s):
    return jnp.stack([s[HEAD_DIM * h:HEAD_DIM * (h + 1), :] for h in range(HEADS)])


def _scan_ins(qd, kd, w, u, qk, dcrow, state):
    ins = [_state_by_head(state)]
    for r0 in (0, CHUNK):
        rs = slice(r0, r0 + CHUNK)
        ins += [_by_head(t[rs, :]) for t in (qd, kd, w, u, qk)]
        ins.append(jnp.stack([_pick_scalar(dcrow, r0, h) for h in range(HEADS)]))
    return ins


def _gdn_scan_fwd_fn(qd, kd, w, u, qk, dcrow, state):
    o_a, o_b, s2 = _gdn_scan_head(*_scan_ins(qd, kd, w, u, qk, dcrow, state))
    return jnp.concatenate([_from_heads(o_a), _from_heads(o_b)], axis=0), state, s2.reshape(state.shape)


def _gdn_scan_bwd_fn(qd, kd, w, u, qk, dcrow, state, d_o, d_state):
    cts = (_by_head(d_o[0:CHUNK, :]), _by_head(d_o[CHUNK:_PAIR, :]), _state_by_head(d_state))
    g = _vjp(_gdn_scan_head, _scan_ins(qd, kd, w, u, qk, dcrow, state), cts)
    grads = tuple(jnp.concatenate([_from_heads(g[1 + t]), _from_heads(g[7 + t])], axis=0) for t in range(5))
    d_dcrow = sum(_put_scalar(g[6][h], 0, h, dcrow.shape) + _put_scalar(g[12][h], CHUNK, h, dcrow.shape) for h in range(HEADS))
    return grads + (d_dcrow, g[0].reshape(state.shape))


GDN_INTRA_PAIRS = 2
_INTRA_ROWS = GDN_INTRA_PAIRS * _PAIR


def _by_pair_head(t):
    return jnp.stack([t[_PAIR * p:_PAIR * (p + 1), HEAD_DIM * h:HEAD_DIM * (h + 1)] for p in range(GDN_INTRA_PAIRS) for h in range(HEADS)])


def _from_pair_heads(t):
    return jnp.concatenate([_cat([t[p * HEADS + h] for h in range(HEADS)]) for p in range(GDN_INTRA_PAIRS)], axis=0)


def _lanes_by_pair_head(t, first):
    return jnp.stack([_lane_pick(t[_PAIR * p:_PAIR * (p + 1), :], first + h) for p in range(GDN_INTRA_PAIRS) for h in range(HEADS)])


def _gdn_intra_ins(q, k, v, bg):
    return [_by_pair_head(q), _by_pair_head(k), _by_pair_head(v), _lanes_by_pair_head(bg, 0), _lanes_by_pair_head(bg, HEADS)]


def _gdn_intra_fwd_fn(q, k, v, bg):
    res = _gdn_intra_head(*_gdn_intra_ins(q, k, v, bg))
    dcrow = jnp.concatenate([sum(_lane_put(res[5][p * HEADS + h], h) for h in range(HEADS)) for p in range(GDN_INTRA_PAIRS)], axis=0)
    return tuple(_from_pair_heads(r) for r in res[:5]) + (dcrow,)


def _gdn_intra_bwd_fn(q, k, v, bg, d_w, d_u, d_qd, d_kd, d_qk, d_dcrow):
    cts = tuple(_by_pair_head(d) for d in (d_w, d_u, d_qd, d_kd, d_qk)) + (_lanes_by_pair_head(d_dcrow, 0),)
    g = _vjp(_gdn_intra_head, _gdn_intra_ins(q, k, v, bg), cts)
    d_bg = jnp.concatenate([sum(_lane_put(g[3][p * HEADS + h], h) + _lane_put(g[4][p * HEADS + h], HEADS + h) for h in range(HEADS))
                            for p in range(GDN_INTRA_PAIRS)], axis=0)
    return tuple(_from_pair_heads(g[t]) for t in range(3)) + (d_bg,)


def _gdn_gates(ba, alog, dtb):
    lane = lax.broadcasted_iota(jnp.int32, ba.shape, 1)
    beta = _sigmoid(ba)
    g = -jnp.exp(alog) * _softplus(ba + dtb)
    return jnp.where(lane < HEADS, beta, jnp.where(lane < 2 * HEADS, g, 0.0))


def _gdn_q_head(c):
    return _l2n(_silu(c)) * HEAD_DIM ** -0.5


def _gdn_k_head(c):
    return _l2n(_silu(c))


def _gdn_prep_fwd_fn(qkv_c, ba, alog, dtb):
    hs = _heads(qkv_c, 3 * HEADS)
    q = _cat([_gdn_q_head(c) for c in hs[:HEADS]])
    k = _cat([_gdn_k_head(c) for c in hs[HEADS:2 * HEADS]])
    v = _cat([_silu(c) for c in hs[2 * HEADS:]])
    return q, k, v, _gdn_gates(ba, alog, dtb)


def _gdn_prep_bwd_fn(qkv_c, ba, d_q, d_k, d_v, d_bg, alog, dtb):
    hs = _heads(qkv_c, 3 * HEADS)
    dqs, dks, dvs = _heads(d_q, HEADS), _heads(d_k, HEADS), _heads(d_v, HEADS)
    parts = [_vjp(_gdn_q_head, [hs[h]], (dqs[h],))[0] for h in range(HEADS)]
    parts += [_vjp(_gdn_k_head, [hs[HEADS + h]], (dks[h],))[0] for h in range(HEADS)]
    parts += [_vjp(_silu, [hs[2 * HEADS + h]], (dvs[h],))[0] for h in range(HEADS)]
    d_ba, d_alog, d_dtb = _vjp(_gdn_gates, [ba, alog, dtb], (d_bg,))
    return _cat(parts), d_ba, d_alog, d_dtb


def _a_out_head(o, gate, gain):
    return _rms(o, gain) * _silu(gate)


def _a_post_fwd_fn(o, gate, qm, gain, mem_kv):
    parts = [_a_out_head(oh, gh, gain) for oh, gh in zip(_heads(o, HEADS), _heads(gate, HEADS))]
    parts.append(_mem_attn(qm, mem_kv[:, :MEM_WIDTH], mem_kv[:, MEM_WIDTH:]))
    return (_cat(parts),)


def _a_post_bwd_fn(o, gate, qm, d_cat, gain, mem_kv):
    d_os, d_gates = [], []
    d_gain = jnp.zeros(gain.shape, _F32)
    dc = _heads(d_cat, HEADS + 2)
    for h, (oh, gh) in enumerate(zip(_heads(o, HEADS), _heads(gate, HEADS))):
        g = _vjp(_a_out_head, [oh, gh, gain], (dc[h],))
        d_os.append(g[0])
        d_gates.append(g[1])
        d_gain = d_gain + g[2]
    d_qm, d_km, d_vm = _vjp(_mem_attn, [qm, mem_kv[:, :MEM_WIDTH], mem_kv[:, MEM_WIDTH:]], (d_cat[:, A_WIDTH:],))
    return _cat(d_os), _cat(d_gates + [d_qm]), d_gain, _cat([d_km, d_vm])


def _b_post_fwd_fn(o, qm, mem_kv):
    return (_cat([o.astype(_F32), _mem_attn(qm, mem_kv[:, :MEM_WIDTH], mem_kv[:, MEM_WIDTH:])]),)


def _b_post_bwd_fn(qm, d_cat_m, mem_kv):
    d_qm, d_km, d_vm = _vjp(_mem_attn, [qm, mem_kv[:, :MEM_WIDTH], mem_kv[:, MEM_WIDTH:]], (d_cat_m,))
    return d_qm, _cat([d_km, d_vm])


ATTN_SCALE = (HEAD_DIM + QK_ROPE) ** -0.5


def _rope_q_fwd_fn(qf, cos_t, sin_t):
    hs = _heads(qf, 2 * HEADS)
    return (_cat(hs[:HEADS] + [_rope(x, cos_t, sin_t) for x in hs[HEADS:]]) * ATTN_SCALE,)


def _rope_q_bwd_fn(d_qn, d_qr, cos_t, sin_t):
    f = lambda x: _rope(x, cos_t, sin_t)
    return (_cat([d_qn] + [_vjp(f, [x], (x,))[0] for x in _heads(d_qr, HEADS)]) * ATTN_SCALE,)


def _kv_prep_fwd_fn(ckr, cos_t, sin_t, gain):
    return _rms(ckr[:, :KV_LORA], gain), _rope(ckr[:, KV_LORA:], cos_t, sin_t)


def _kv_prep_bwd_fn(ckr, d_ckv, d_kr, cos_t, sin_t, gain):
    d_lat, d_gain = _vjp(_rms, [ckr[:, :KV_LORA], gain], (d_ckv,))
    f = lambda x: _rope(x, cos_t, sin_t)
    d_rope = _vjp(f, [ckr[:, KV_LORA:]], (d_kr,))[0]
    return _cat([d_lat, d_rope]), d_gain


def _rms_fwd(x, gain, *, name, tm=1024, out_dtype=_BF):
    tm = min(tm, x.shape[0])
    return _rowmap(lambda x_, g_: (_rms(x_.astype(_F32), g_),), [_t(x)], [gain], [(None, x.shape[1], out_dtype)], tm=tm, name=name)[0]


def _rms_bwd(x, d_xn, d_res, gain, *, name, tm=512):
    tm = min(tm, x.shape[0])

    def fn(x_, dxn_, *rest):
        g_ = rest[-1]
        dx, dg = _vjp(_rms, [x_.astype(_F32), g_], (dxn_.astype(_F32),))
        if d_res is not None:
            dx = dx + rest[0]
        return dx, dg

    tiles = [_t(x), _t(d_xn)] + ([_t(d_res)] if d_res is not None else [])
    return _rowmap(fn, tiles, [gain], [(None, x.shape[1], _F32)], [(gain.shape, _F32)], tm=tm, name=name)


FFN_COL_TILE = 1408
FFN_ROW_TILE = 512
FFN_DXN_ROW_TILE = 512
FFN_DOWN_ROW_TILE = 1024


def _ffn_gate_up(x, gain, w_gu, idx, *, name):
    s = x.shape[0]
    tm, tf = min(FFN_ROW_TILE, s), FFN_COL_TILE
    nf = D_FF // tf

    def body(x_ref, gain_ref, wg_ref, wu_ref, xn_ref, g_ref, u_ref, a_ref):
        @pl.when(pl.program_id(1) == 0)
        def _():
            xn_ref[...] = _rms(x_ref[...], gain_ref[...]).astype(xn_ref.dtype)

        xn = xn_ref[...]
        g, u = _dot(xn, wg_ref[...]), _dot(xn, wu_ref[...])
        g_ref[...] = g.astype(g_ref.dtype)
        u_ref[...] = u.astype(u_ref.dtype)
        a_ref[...] = (_silu(g) * u).astype(a_ref.dtype)

    col = pl.BlockSpec((tm, tf), lambda i, j: (i, j))
    wide = jax.ShapeDtypeStruct((s, D_FF), _BF)
    return _pcall(
        body, name=name, grid=(s // tm, nf),
        in_specs=[pl.BlockSpec((tm, D_MODEL), lambda i, j: (i, 0)), pl.BlockSpec((1, D_MODEL), lambda i, j: (0, 0)),
                  pl.BlockSpec((None, D_MODEL, tf), lambda i, j: (j, idx, 0)), pl.BlockSpec((None, D_MODEL, tf), lambda i, j: (nf + j, idx, 0))],
        out_specs=[pl.BlockSpec((tm, D_MODEL), lambda i, j: (i, 0)), col, col, col],
        out_shape=[jax.ShapeDtypeStruct((s, D_MODEL), _BF), wide, wide, wide],
        compiler_params=_cp(2),
    )(x, gain, w_gu, w_gu)


FFN_SHARD_ROWS = D_FF // N_CHIPS


def _w_down_specs(idx):
    return [pl.BlockSpec((None, FFN_SHARD_ROWS, D_MODEL), functools.partial(lambda i, j, q: (2 * j + q, idx, 0), q=q)) for q in (0, 1)]


def _ffn_down(a, w_down, idx, x, *, name):
    s = a.shape[0]
    tm = min(FFN_DOWN_ROW_TILE, s)

    def body(a_ref, w0_ref, w1_ref, w2_ref, w3_ref, x_ref, o_ref):
        w = jnp.concatenate([w0_ref[...], w1_ref[...], w2_ref[...], w3_ref[...]], axis=0)
        o_ref[...] = x_ref[...] + 0.5 * _dot(a_ref[...], w)

    rows = pl.BlockSpec((tm, D_MODEL), lambda i: (i, 0))
    w_specs = [pl.BlockSpec((None, FFN_SHARD_ROWS, D_MODEL), functools.partial(lambda i, k: (k, idx, 0), k=k)) for k in range(N_CHIPS)]
    return _pcall(
        body, name=name, grid=(s // tm,), in_specs=[pl.BlockSpec((tm, D_FF), lambda i: (i, 0))] + w_specs + [rows],
        out_specs=rows, out_shape=jax.ShapeDtypeStruct((s, D_MODEL), _F32), compiler_params=_cp(1),
    )(a, w_down, w_down, w_down, w_down, x)


def _ffn_fwd(x, gain, w_gu, w_down, idx, *, name):
    xn, g, u, a = _ffn_gate_up(x, gain, w_gu, idx, name=name + "_gu")
    y = _ffn_down(a, w_down, idx, x, name=name + "_down")
    return y, (x, xn, g, u, a)


def _ffn_d_gate_up(d_y, g, u, w_down, idx, *, name):
    s = d_y.shape[0]
    tm, tf = min(FFN_ROW_TILE, s), FFN_COL_TILE

    def body(dy_ref, wa_ref, wb_ref, g_ref, u_ref, dg_ref, du_ref):
        da = _dot_nt(dy_ref[...], jnp.concatenate([wa_ref[...], wb_ref[...]], axis=0)) * 0.5
        gg, uu = g_ref[...].astype(_F32), u_ref[...].astype(_F32)
        sg = _sigmoid(gg)
        dg_ref[...] = (da * uu * sg * (1.0 + gg * (1.0 - sg))).astype(dg_ref.dtype)
        du_ref[...] = (da * gg * sg).astype(du_ref.dtype)

    col = pl.BlockSpec((tm, tf), lambda i, j: (i, j))
    wide = jax.ShapeDtypeStruct((s, D_FF), _BF)
    return _pcall(
        body, name=name, grid=(s // tm, D_FF // tf),
        in_specs=[pl.BlockSpec((tm, D_MODEL), lambda i, j: (i, 0))] + _w_down_specs(idx) + [col, col],
        out_specs=[col, col], out_shape=[wide, wide], compiler_params=_cp(2),
    )(d_y, w_down, w_down, g, u)


def _ffn_d_x(d_g, d_u, w_gu, idx, x, d_y, gain, *, name):
    s = x.shape[0]
    tm, tf = min(FFN_DXN_ROW_TILE, s), FFN_COL_TILE

    def body(dg_ref, du_ref, w0_ref, w1_ref, w2_ref, w3_ref, x_ref, dy_ref, gain_ref, dx_ref, dgain_ref):
        @pl.when(pl.program_id(0) == 0)
        def _():
            dgain_ref[...] = jnp.zeros(dgain_ref.shape, _F32)

        d_xn = (_dot_nt(dg_ref[:, 0:tf], w0_ref[...]) + _dot_nt(dg_ref[:, tf:2 * tf], w1_ref[...])
                + _dot_nt(du_ref[:, 0:tf], w2_ref[...]) + _dot_nt(du_ref[:, tf:2 * tf], w3_ref[...]))
        dx, dgain = _vjp(_rms, [x_ref[...], gain_ref[...]], (d_xn,))
        dx_ref[...] = dx + dy_ref[...]
        dgain_ref[...] += dgain

    wide = pl.BlockSpec((tm, D_FF), lambda i: (i, 0))
    rows = pl.BlockSpec((tm, D_MODEL), lambda i: (i, 0))
    one = pl.BlockSpec((1, D_MODEL), lambda i: (0, 0))
    w_specs = [pl.BlockSpec((None, D_MODEL, tf), functools.partial(lambda i, k: (k, idx, 0), k=k), pipeline_mode=pl.Buffered(1))
               for k in range(N_CHIPS)]
    return _pcall(
        body, name=name, grid=(s // tm,),
        in_specs=[wide, wide] + w_specs + [rows, rows, one],
        out_specs=[rows, one], out_shape=[jax.ShapeDtypeStruct((s, D_MODEL), _F32), jax.ShapeDtypeStruct((1, D_MODEL), _F32)],
        compiler_params=_cp(1),
    )(d_g, d_u, w_gu, w_gu, w_gu, w_gu, x, d_y, gain)


def _ffn_d_w_gu(xn, d_act, into, idx, first_chip, *, name, ts=1024):
    s = xn.shape[0]
    ts = min(ts, s)
    steps = s // ts

    def body(a_ref, b_ref, into_ref, o_ref):
        @pl.when(pl.program_id(1) == 0)
        def _():
            o_ref[...] = jnp.zeros(o_ref.shape, o_ref.dtype)

        o_ref[...] += _dot_tn(a_ref[...], b_ref[...])

    return _pcall(
        body, name=name, grid=(D_FF // FFN_COL_TILE, steps),
        in_specs=[pl.BlockSpec((ts, D_MODEL), lambda j, r: (r, 0)), pl.BlockSpec((ts, FFN_COL_TILE), lambda j, r: (r, j)),
                  pl.BlockSpec(memory_space=pl.ANY)],
        out_specs=pl.BlockSpec((None, None, D_MODEL, FFN_COL_TILE), lambda j, r: (first_chip + j, idx, 0, 0)),
        out_shape=jax.ShapeDtypeStruct(into.shape, into.dtype), input_output_aliases={2: 0}, compiler_params=_cp(2),
    )(xn, d_act, into)


def _ffn_d_w_down(a, d_y, into, idx, *, name, ts=1024):
    s = a.shape[0]
    ts = min(ts, s)
    steps = s // ts

    def body(a_ref, b_ref, into_ref, o_ref):
        @pl.when(pl.program_id(1) == 0)
        def _():
            o_ref[...] = jnp.zeros(o_ref.shape, o_ref.dtype)

        part = _dot_tn(a_ref[...], b_ref[...]) * 0.5
        o_ref[0] += part[0:FFN_SHARD_ROWS, :]
        o_ref[1] += part[FFN_SHARD_ROWS:2 * FFN_SHARD_ROWS, :]

    return _pcall(
        body, name=name, grid=(D_FF // FFN_COL_TILE, steps),
        in_specs=[pl.BlockSpec((ts, FFN_COL_TILE), lambda i, r: (r, i)), pl.BlockSpec((ts, D_MODEL), lambda i, r: (r, 0)),
                  pl.BlockSpec(memory_space=pl.ANY)],
        out_specs=pl.BlockSpec((2, None, FFN_SHARD_ROWS, D_MODEL), lambda i, r: (i, idx, 0, 0)),
        out_shape=jax.ShapeDtypeStruct(into.shape, into.dtype), input_output_aliases={2: 0}, compiler_params=_cp(2),
    )(a, d_y, into)


def _ffn_bwd(d_y, saved, gain, w_gu, w_down, idx, g_gu, g_down, *, name):
    x, xn, g, u, a = saved
    d_g, d_u = _ffn_d_gate_up(d_y, g, u, w_down, idx, name=name + "_dgu")
    g_down = _ffn_d_w_down(a, d_y, g_down, idx, name=name + "_dwd")
    g_gu = _ffn_d_w_gu(xn, d_g, g_gu, idx, 0, name=name + "_dwg")
    g_gu = _ffn_d_w_gu(xn, d_u, g_gu, idx, 2, name=name + "_dwu")
    d_x, d_gain = _ffn_d_x(d_g, d_u, w_gu, idx, x, d_y, gain, name=name + "_dx")
    return d_x, d_gain, g_gu, g_down


def _conv_fwd(h, w, *, name, tm=512):
    s = h.shape[0]
    tm = min(tm, s)
    c = 3 * A_WIDTH
    halo = SUBLANE

    def body(x_ref, prev_ref, w_ref, o_ref, buf):
        i = pl.program_id(0)
        buf[0:halo, :] = jnp.where(i == 0, 0.0, prev_ref[...])
        buf[halo:halo + tm, :] = x_ref[...]
        acc = jnp.zeros((tm, c), _F32)
        for j in range(CONV_K):
            acc = acc + buf[pl.ds(halo - (CONV_K - 1) + j, tm), :] * w_ref[j:j + 1, :]
        o_ref[...] = acc

    return _pcall(
        body, name=name, grid=(s // tm,),
        in_specs=[pl.BlockSpec((tm, c), lambda i: (i, 0)),
                  pl.BlockSpec((halo, c), lambda i: (jnp.maximum(i * (tm // halo) - 1, 0), 0)),
                  pl.BlockSpec(w.shape, lambda i: (0, 0))],
        out_specs=pl.BlockSpec((tm, c), lambda i: (i, 0)), out_shape=jax.ShapeDtypeStruct((s, c), _F32),
        scratch_shapes=[pltpu.VMEM((tm + 2 * halo, c), _F32)],
        compiler_params=_cp(1),
    )(h, h, w)


def _conv_bwd(h, d_y, w, *, name, tm=512):
    s = h.shape[0]
    tm = min(tm, s)
    c = 3 * A_WIDTH
    halo = SUBLANE
    steps = s // tm

    def body(x_ref, prev_ref, dy_ref, next_ref, w_ref, dx_ref, dw_ref, xbuf, dybuf):
        i = pl.program_id(0)

        @pl.when(i == 0)
        def _():
            dw_ref[...] = jnp.zeros(dw_ref.shape, dw_ref.dtype)

        xbuf[0:halo, :] = jnp.where(i == 0, 0.0, prev_ref[...])
        xbuf[halo:halo + tm, :] = x_ref[...]
        dybuf[0:tm, :] = dy_ref[...]
        dybuf[tm:tm + halo, :] = jnp.where(i == steps - 1, 0.0, next_ref[...])
        dy = dy_ref[...]
        acc = jnp.zeros((tm, c), _F32)
        for j in range(CONV_K):
            acc = acc + dybuf[pl.ds(CONV_K - 1 - j, tm), :] * w_ref[j:j + 1, :]
            dw_ref[j:j + 1, :] += jnp.sum(dy * xbuf[pl.ds(halo - (CONV_K - 1) + j, tm), :], axis=0, keepdims=True)
        dx_ref[...] = acc.astype(dx_ref.dtype)

    return _pcall(
        body, name=name, grid=(steps,),
        in_specs=[pl.BlockSpec((tm, c), lambda i: (i, 0)),
                  pl.BlockSpec((halo, c), lambda i: (jnp.maximum(i * (tm // halo) - 1, 0), 0)),
                  pl.BlockSpec((tm, c), lambda i: (i, 0)),
                  pl.BlockSpec((halo, c), lambda i: (jnp.minimum((i + 1) * (tm // halo), s // halo - 1), 0)),
                  pl.BlockSpec(w.shape, lambda i: (0, 0))],
        out_specs=[pl.BlockSpec((tm, c), lambda i: (i, 0)), pl.BlockSpec(w.shape, lambda i: (0, 0))],
        out_shape=[jax.ShapeDtypeStruct((s, c), _BF), jax.ShapeDtypeStruct(w.shape, _F32)],
        scratch_shapes=[pltpu.VMEM((tm + 2 * halo, c), _F32), pltpu.VMEM((tm + 2 * halo, c), _F32)],
        compiler_params=_cp(1),
    )(h, h, d_y, d_y, w)


ATTN_VMEM = 56 * 2 ** 20
ATTN_Q_BLOCK = 4096
ATTN_K_SUB = 256
ATTN_BWD_BLOCK = 2048
ATTN_BWD_SUB = 512


def _chunk_mask(shape, q_axis):
    qi = lax.broadcasted_iota(jnp.int32, shape, q_axis) // CHUNK
    ki = lax.broadcasted_iota(jnp.int32, shape, 1 - q_axis) // CHUNK
    return ki <= qi


def _rows(j, t):
    return pl.ds(pl.multiple_of(j * t, t), t)


def _chunk_mask_at(shape, q_axis, q_off):
    qi = (lax.broadcasted_iota(jnp.int32, shape, q_axis) + q_off) // CHUNK
    ki = lax.broadcasted_iota(jnp.int32, shape, 1 - q_axis) // CHUNK
    return ki <= qi


def _attn_fwd(q_all, kv, kr, *, name):
    s = q_all.shape[0]
    t = min(ATTN_Q_BLOCK, s)
    tk = min(ATTN_K_SUB, t)
    nq, sub, rep = s // t, t // tk, tk // LANE

    def body(qn_ref, qr_ref, kn_ref, kr_ref, v_ref, o_ref, lse_ref, m_sc, acc_sc):
        i = pl.program_id(1)
        m_sc[...] = jnp.full(m_sc.shape, -1e30, _F32)
        acc_sc[...] = jnp.zeros(acc_sc.shape, _F32)
        ones = jnp.ones((tk, LANE), _BF)

        def block(j, first_row):
            qs = slice(first_row, t)
            rows = _rows(j, tk)
            q = _cat([qn_ref[qs, :], qr_ref[qs, :]])
            sc = _dot_nt(q, _cat([kn_ref[rows, :], kr_ref[rows, :]]))
            if first_row is not None:
                sc = jnp.where(_chunk_mask(sc.shape, 0), sc, -1e30)
            m_prev = m_sc[qs, :]
            m_new = jnp.maximum(m_prev, jnp.max(sc, axis=-1, keepdims=True))
            alpha = jnp.exp(m_prev - m_new)
            p = jnp.exp(sc - _cat([m_new] * rep))
            acc_sc[qs, :] = _cat([alpha, alpha]) * acc_sc[qs, :] + _dot(p, _cat([v_ref[rows, :], ones]))
            m_sc[qs, :] = m_new

        def step(j, carry):
            block(j, None)
            return carry

        lax.fori_loop(0, i * sub, step, 0)
        for u in range(sub):
            block(i * sub + u, u * tk)
        row_sum = acc_sc[:, LANE:2 * LANE]
        o_ref[...] = acc_sc[:, 0:LANE] / row_sum
        lse_ref[...] = m_sc[...] + jnp.log(row_sum)

    return _pcall(
        body, name=name, grid=(HEADS, nq),
        in_specs=[pl.BlockSpec((t, LANE), lambda h, i: (i, h)),
                  pl.BlockSpec((t, LANE), lambda h, i: (i, HEADS + h)),
                  pl.BlockSpec((s, LANE), lambda h, i: (0, h), pipeline_mode=pl.Buffered(1)),
                  pl.BlockSpec((s, LANE), lambda h, i: (0, 0), pipeline_mode=pl.Buffered(1)),
                  pl.BlockSpec((s, LANE), lambda h, i: (0, HEADS + h), pipeline_mode=pl.Buffered(1))],
        out_specs=[pl.BlockSpec((t, LANE), lambda h, i: (i, h)), pl.BlockSpec((t, LANE), lambda h, i: (i, h))],
        out_shape=[jax.ShapeDtypeStruct((s, A_WIDTH), _F32), jax.ShapeDtypeStruct((s, A_WIDTH), _F32)],
        scratch_shapes=[pltpu.VMEM((t, LANE), _F32), pltpu.VMEM((t, 2 * LANE), _F32)],
        compiler_params=pltpu.CompilerParams(dimension_semantics=("arbitrary", "arbitrary"), vmem_limit_bytes=ATTN_VMEM),
    )(q_all, q_all, kv, kr, kv)


def _attn_bwd_prep(o, lse, d_cat, *, name):
    s = o.shape[0]
    t = min(ATTN_BWD_SUB, s)
    nq = s // t

    def body(o_ref, lse_ref, do_ref, dob_ref, lset_ref, dlt_ref):
        for h in range(HEADS):
            sl = slice(LANE * h, LANE * (h + 1))
            rows = slice(SUBLANE * h, SUBLANE * (h + 1))
            do = do_ref[:, sl]
            dl = jnp.broadcast_to(jnp.sum(o_ref[:, sl] * do, axis=-1, keepdims=True), (t, LANE))
            dob_ref[:, sl] = do.astype(dob_ref.dtype)
            dlt_ref[rows, :] = dl.T[0:SUBLANE, :]
            lset_ref[rows, :] = lse_ref[:, sl].T[0:SUBLANE, :]

    wide = pl.BlockSpec((t, A_WIDTH), lambda i: (i, 0))
    stat = pl.BlockSpec((HEADS * SUBLANE, t), lambda i: (i, 0))
    stat_shape = jax.ShapeDtypeStruct((nq * HEADS * SUBLANE, t), _F32)
    return _pcall(
        body, name=name, grid=(nq,), in_specs=[wide, wide, wide], out_specs=[wide, stat, stat],
        out_shape=[jax.ShapeDtypeStruct((s, A_WIDTH), _BF), stat_shape, stat_shape], compiler_params=_cp(1),
    )(o, lse, d_cat)


ATTN_BWD_VMEM = 56 * 2 ** 20


def _attn_bwd(q_all, kv, kr, lse_t, delta_t, d_o, *, name):
    s = q_all.shape[0]
    t = min(ATTN_BWD_BLOCK, s)
    tq = min(ATTN_BWD_SUB, t)
    nk, sub, nqs = s // t, t // tq, s // tq

    def body(kn_ref, kr_ref, v_ref, qn_ref, qr_ref, do_ref, lset_ref, dlt_ref, dkn_ref, dv_ref, dkr_ref, dqn_ref, dqr_ref, dk_sc, dv_sc):
        h, j = pl.program_id(0), pl.program_id(1)
        dk_sc[...] = jnp.zeros(dk_sc.shape, _F32)
        dv_sc[...] = jnp.zeros(dv_sc.shape, _F32)

        @pl.when(j == 0)
        def _():
            dqn_ref[...] = jnp.zeros(dqn_ref.shape, _F32)
            dqr_ref[...] = jnp.zeros(dqr_ref.shape, _F32)

        def block(i, query_off):
            ks = slice(0, t if query_off is None else query_off + tq)
            rows = _rows(i, tq)
            stat = pl.ds(pl.multiple_of((i * HEADS + h) * SUBLANE, SUBLANE), 1)
            q = _cat([qn_ref[rows, :], qr_ref[rows, :]])
            do = do_ref[rows, :]
            k = _cat([kn_ref[ks, :], kr_ref[ks, :]])
            p = jnp.exp(_dot_nt(k, q) - lset_ref[stat, :])
            if query_off is not None:
                p = jnp.where(_chunk_mask_at(p.shape, 1, query_off), p, 0.0)
            dv_sc[ks, :] += _dot(p, do)
            ds = p * (_dot_nt(v_ref[ks, :], do) - dlt_ref[stat, :])
            dk_sc[ks, :] += _dot(ds, q)
            dq = _dot_tn(ds, k)
            dqn_ref[rows, :] += dq[:, 0:LANE]
            dqr_ref[rows, :] += dq[:, LANE:2 * LANE]

        def step(i, carry):
            block(i, None)
            return carry

        for u in range(sub):
            block(j * sub + u, u * tq)
        lax.fori_loop((j + 1) * sub, nqs, step, 0)
        dkn_ref[...] = dk_sc[:, 0:LANE]
        dkr_ref[...] = dk_sc[:, LANE:2 * LANE]
        dv_ref[...] = dv_sc[...]

    once = pl.Buffered(1)
    stats = pl.BlockSpec((nqs * HEADS * SUBLANE, tq), lambda h, j: (0, 0), pipeline_mode=once)
    per_head = lambda at: pl.BlockSpec((s, LANE), at, pipeline_mode=once)
    return _pcall(
        body, name=name, grid=(HEADS, nk),
        in_specs=[pl.BlockSpec((t, LANE), lambda h, j: (j, h)),
                  pl.BlockSpec((t, LANE), lambda h, j: (j, 0)),
                  pl.BlockSpec((t, LANE), lambda h, j: (j, HEADS + h)),
                  per_head(lambda h, j: (0, h)), per_head(lambda h, j: (0, HEADS + h)), per_head(lambda h, j: (0, h)),
                  stats, stats],
        out_specs=[pl.BlockSpec((t, LANE), lambda h, j: (j, h))] * 3 + [per_head(lambda h, j: (0, h))] * 2,
        out_shape=[jax.ShapeDtypeStruct((s, A_WIDTH), _F32)] * 5,
        scratch_shapes=[pltpu.VMEM((t, 2 * LANE), _F32), pltpu.VMEM((t, LANE), _F32)],
        compiler_params=pltpu.CompilerParams(dimension_semantics=("arbitrary", "arbitrary"), vmem_limit_bytes=ATTN_BWD_VMEM),
    )(kv, kr, kv, q_all, q_all, d_o, lse_t, delta_t)


def _final_loss(x, tgt, gain, *, name, tm=512):
    tm = min(tm, x.shape[0])

    def fn(x_, t_, g_):
        def f(xx, gg):
            err = _rms(xx, gg) - t_
            return 0.5 * jnp.sum(jnp.sum(err * err, axis=1, keepdims=True) / D_MODEL, axis=0, keepdims=True)

        loss, pull = jax.vjp(f, x_, g_)
        dx, dg = pull(jnp.ones((1, 1), _F32))
        return dx, dg, jnp.broadcast_to(loss, (SUBLANE, LANE))

    return _rowmap(fn, [_t(x), _t(tgt)], [gain], [(None, D_MODEL, _F32)], [(gain.shape, _F32), ((SUBLANE, LANE), _F32)], tm=tm, name=name)


def _local_step(x, mem, cos_t, sin_t, tgt, p):
    s = x.shape[0]
    g = {}
    row = lambda a: a.reshape(1, -1)
    tm_e = min(512, s)

    mem_n = _rms_fwd(mem, row(p["mem_norm"]), name="mem_norm")
    mem_kv_all = _mm_nn(mem_n, p["w_mem_all"], out_dtype=_F32, name="mem_kv", tn=1024)
    mem_kv = [mem_kv_all[:, 2 * MEM_WIDTH * l:2 * MEM_WIDTH * (l + 1)] for l in range(DEPTH)]

    sv = []
    for i in range(N_A):
        l = i
        r = {}
        r["x0"] = x
        x, r["ffn1"] = _ffn_fwd(x, row(p["ffn1_norm"][l]), p["ffn_w_gu"], p["ffn_w_down"], l, name=f"a{i}_ffn1")
        r["x1"] = x
        xn = _rms_fwd(x, row(p["mix_norm"][l]), name=f"a{i}_mixnorm")
        h = _mm_nn(xn, p["a_w_in"][i], out_dtype=_F32, name=f"a{i}_in", tn=1152)
        qkv_c = _conv_fwd(h, p["a_conv"][i], name=f"a{i}_conv")
        alog, dtb = p["a_A_log_row"][i], p["a_dt_bias_row"][i]
        q, k, v, bg = _rowmap(_gdn_prep_fwd_fn, [_t(qkv_c), _t(h, LANE, 26)], [alog, dtb],
                              [(None, A_WIDTH, _F32)] * 3 + [(None, LANE, _F32)], tm=tm_e, name=f"a{i}_prep")
        w_, u_, qd, kd, qk, dcrow = _rowmap(_gdn_intra_fwd_fn, [_t(q), _t(k), _t(v), _t(bg)], [],
                                            [(None, A_WIDTH, _F32)] * 5 + [(None, LANE, _F32)], tm=_INTRA_ROWS, name=f"a{i}_intra")
        o, states = _rowmap(_gdn_scan_fwd_fn, [_t(qd), _t(kd), _t(w_), _t(u_), _t(qk), _t(dcrow)], [],
                            [(None, A_WIDTH, _F32), (A_WIDTH, HEAD_DIM, _F32)], tm=_PAIR, name=f"a{i}_scan",
                            carry=[((A_WIDTH, HEAD_DIM), _F32)])
        gain_o = row(p["a_out_norm"][i])
        cat = _rowmap(_a_post_fwd_fn, [_t(o), _t(h, A_WIDTH, 3), _t(h, MEM_WIDTH, 12)], [gain_o, mem_kv[l]],
                      [(None, D_MODEL, _BF)], tm=tm_e, name=f"a{i}_post")[0]
        x = _mm_nn(cat, p["w_out"][l], out_dtype=_F32, name=f"a{i}_out", res=x, tn=1024)
        r.update(xn=xn, h=h, qkv_c=qkv_c, q=q, k=k, v=v, bg=bg, w=w_, u=u_, qd=qd, kd=kd, qk=qk, dcrow=dcrow, o=o, states=states, cat=cat)
        r["x2"] = x
        x, r["ffn2"] = _ffn_fwd(x, row(p["ffn2_norm"][l]), p["ffn_w_gu"], p["ffn_w_down"], DEPTH + l, name=f"a{i}_ffn2")
        sv.append(r)

    kvs = {"x": x}
    xn_kv = _rms_fwd(x, row(p["kv_in_norm"]), name="kv_innorm")
    ckr = _mm_nn(xn_kv, p["w_dkv"], out_dtype=_F32, name="kv_down")
    ckv, k_rope = _rowmap(_kv_prep_fwd_fn, [_t(ckr), _t(cos_t), _t(sin_t)], [row(p["kv_lat_norm"])],
                          [(None, KV_LORA, _BF), (None, LANE, _BF)], tm=tm_e, name="kv_prep")
    kvu = _mm_nn(ckv, p["w_ukv"], out_dtype=_BF, name="kv_up")
    kvs.update(xn=xn_kv, ckr=ckr, ckv=ckv)

    for j in range(N_B):
        l = N_A + j
        r = {}
        x, r["ffn1"] = _ffn_fwd(x, row(p["ffn1_norm"][l]), p["ffn_w_gu"], p["ffn_w_down"], l, name=f"b{j}_ffn1")
        r["x1"] = x
        xn = _rms_fwd(x, row(p["mix_norm"][l]), name=f"b{j}_mixnorm")
        h = _mm_nn(xn, p["b_w_in"][j], out_dtype=_F32, name=f"b{j}_in")
        gain_q = row(p["b_q_norm"][j])
        cqn = _rowmap(lambda c_, g_: (_rms(c_, g_),), [_t(h, Q_LORA, 0)], [gain_q], [(None, Q_LORA, _BF)], tm=tm_e, name=f"b{j}_qnorm")[0]
        qf = _mm_nn(cqn, p["b_w_uq"][j], out_dtype=_F32, name=f"b{j}_uq")
        q_all = _rowmap(_rope_q_fwd_fn, [_t(qf), _t(cos_t), _t(sin_t)], [], [(None, UQ_PAD, _BF)], tm=tm_e, name=f"b{j}_rope")[0]
        o_b, lse = _attn_fwd(q_all, kvu, k_rope, name=f"b{j}_attn")
        cat = _rowmap(_b_post_fwd_fn, [_t(o_b), _t(h, MEM_WIDTH, 1)], [mem_kv[l]], [(None, D_MODEL, _BF)], tm=tm_e, name=f"b{j}_post")[0]
        x = _mm_nn(cat, p["w_out"][l], out_dtype=_F32, name=f"b{j}_out", res=x, tn=1024)
        r.update(xn=xn, h=h, cqn=cqn, q_all=q_all, o_b=o_b, lse=lse, cat=cat)
        x, r["ffn2"] = _ffn_fwd(x, row(p["ffn2_norm"][l]), p["ffn_w_gu"], p["ffn_w_down"], DEPTH + l, name=f"b{j}_ffn2")
        sv.append(r)

    dx, g["final_norm"], loss = _final_loss(x, tgt, row(p["final_norm"]), name="loss")

    per_layer = lambda: [None] * DEPTH
    for n in ("ffn1_norm", "mix_norm", "ffn2_norm", "w_out", "mem_kv"):
        g[n] = per_layer()
    g["ffn_w_gu"] = jnp.zeros((N_CHIPS, 2 * DEPTH, D_MODEL, FFN_COL_TILE), _F32)
    g["ffn_w_down"] = jnp.zeros((N_CHIPS, 2 * DEPTH, FFN_SHARD_ROWS, D_MODEL), _F32)
    for n in ("a_w_in", "a_conv", "a_A_log_row", "a_dt_bias_row", "a_out_norm", "b_w_in", "b_q_norm", "b_w_uq"):
        g[n] = [None] * N_A
    d_kv_parts = []

    for j in reversed(range(N_B)):
        l = N_A + j
        r = sv[l]
        dx, g["ffn2_norm"][l], g["ffn_w_gu"], g["ffn_w_down"] = _ffn_bwd(
            dx, r["ffn2"], row(p["ffn2_norm"][l]), p["ffn_w_gu"], p["ffn_w_down"], DEPTH + l, g["ffn_w_gu"], g["ffn_w_down"], name=f"b{j}_ffn2b")
        d_cat = _mm_nt(dx, p["w_out"][l], out_dtype=_F32, name=f"b{j}_dcat", tn=1024)
        g["w_out"][l] = _mm_tn(r["cat"], dx, name=f"b{j}_dwout", tn=1024)
        d_qm, g["mem_kv"][l] = _rowmap(_b_post_bwd_fn, [_t(r["h"], MEM_WIDTH, 1), _t(d_cat, MEM_WIDTH, 3)], [mem_kv[l]],
                                      [(None, MEM_WIDTH, _BF)], [((N_MEM, 2 * MEM_WIDTH), _F32)], tm=tm_e, name=f"b{j}_postb")
        d_o, lse_t, delta_t = _attn_bwd_prep(r["o_b"], r["lse"], d_cat, name=f"b{j}_delta")
        dkn, dv, dkr, dqn, dqr = _attn_bwd(r["q_all"], kvu, k_rope, lse_t, delta_t, d_o, name=f"b{j}_attn_bwd")
        d_kv_parts.append((dkn, dv, dkr))
        d_qf = _rowmap(_rope_q_bwd_fn, [_t(dqn), _t(dqr), _t(cos_t), _t(sin_t)], [], [(None, UQ_PAD, _F32)], tm=tm_e, name=f"b{j}_ropeb")[0]
        d_cqn = _mm_nt(d_qf, p["b_w_uq"][j], out_dtype=_F32, name=f"b{j}_dcqn")
        g["b_w_uq"][j] = _mm_tn(r["cqn"], d_qf, name=f"b{j}_dwuq")
        gain_q = row(p["b_q_norm"][j])
        d_cq, g["b_q_norm"][j] = _rowmap(lambda c_, d_, g_: _vjp(_rms, [c_, g_], (d_,)), [_t(r["h"], Q_LORA, 0), _t(d_cqn)], [gain_q],
                                        [(None, Q_LORA, _BF)], [((1, Q_LORA), _F32)], tm=tm_e, name=f"b{j}_qnormb")
        d_h = jnp.concatenate([d_cq, d_qm], axis=1)
        d_xn = _mm_nt(d_h, p["b_w_in"][j], out_dtype=_F32, name=f"b{j}_dxn", tn=1024)
        g["b_w_in"][j] = _mm_tn(r["xn"], d_h, name=f"b{j}_dwin")
        dx, g["mix_norm"][l] = _rms_bwd(r["x1"], d_xn, dx, row(p["mix_norm"][l]), name=f"b{j}_mixnormb")
        dx, g["ffn1_norm"][l], g["ffn_w_gu"], g["ffn_w_down"] = _ffn_bwd(
            dx, r["ffn1"], row(p["ffn1_norm"][l]), p["ffn_w_gu"], p["ffn_w_down"], l, g["ffn_w_gu"], g["ffn_w_down"], name=f"b{j}_ffn1b")

    def kv_sum(*parts):
        dkn = sum(parts[0::3][1:], parts[0])
        dv = sum(parts[1::3][1:], parts[1])
        dkr = sum(parts[2::3][1:], parts[2])
        return _cat([dkn, dv]), sum(_heads(dkr, HEADS)[1:], _heads(dkr, HEADS)[0])

    d_kvu, d_kr = _rowmap(kv_sum, [_t(a) for part in d_kv_parts for a in part], [], [(None, 2 * A_WIDTH, _F32), (None, LANE, _F32)],
                          tm=tm_e, name="kv_dsum")
    d_ckv = _mm_nt(d_kvu, p["w_ukv"], out_dtype=_F32, name="kv_dckv")
    g["w_ukv"] = _mm_tn(kvs["ckv"], d_kvu, name="kv_dwukv")
    d_ckr, g["kv_lat_norm"] = _rowmap(_kv_prep_bwd_fn, [_t(kvs["ckr"]), _t(d_ckv), _t(d_kr), _t(cos_t), _t(sin_t)], [row(p["kv_lat_norm"])],
                                     [(None, DKV_PAD, _F32)], [((1, KV_LORA), _F32)], tm=tm_e, name="kv_prepb")
    d_xn = _mm_nt(d_ckr, p["w_dkv"], out_dtype=_F32, name="kv_dxn", tn=1024)
    g["w_dkv"] = _mm_tn(kvs["xn"], d_ckr, name="kv_dwdkv")
    dx, g["kv_in_norm"] = _rms_bwd(kvs["x"], d_xn, dx, row(p["kv_in_norm"]), name="kv_innormb")

    for i in reversed(range(N_A)):
        l = i
        r = sv[l]
        dx, g["ffn2_norm"][l], g["ffn_w_gu"], g["ffn_w_down"] = _ffn_bwd(
            dx, r["ffn2"], row(p["ffn2_norm"][l]), p["ffn_w_gu"], p["ffn_w_down"], DEPTH + l, g["ffn_w_gu"], g["ffn_w_down"], name=f"a{i}_ffn2b")
        d_cat = _mm_nt(dx, p["w_out"][l], out_dtype=_F32, name=f"a{i}_dcat", tn=1024)
        g["w_out"][l] = _mm_tn(r["cat"], dx, name=f"a{i}_dwout", tn=1024)
        gain_o = row(p["a_out_norm"][i])
        h = r["h"]
        d_o, d_hpart, g["a_out_norm"][i], g["mem_kv"][l] = _rowmap(
            _a_post_bwd_fn, [_t(r["o"]), _t(h, A_WIDTH, 3), _t(h, MEM_WIDTH, 12), _t(d_cat)], [gain_o, mem_kv[l]],
            [(None, A_WIDTH, _F32), (None, D_MODEL, _BF)], [((1, HEAD_DIM), _F32), ((N_MEM, 2 * MEM_WIDTH), _F32)], tm=tm_e, name=f"a{i}_postb")
        d_qd, d_kd, d_w, d_u, d_qk, d_dcrow = _rowmap(
            _gdn_scan_bwd_fn, [_t(r["qd"]), _t(r["kd"]), _t(r["w"]), _t(r["u"]), _t(r["qk"]), _t(r["dcrow"]), _t(r["states"], rows=A_WIDTH), _t(d_o)], [],
            [(None, A_WIDTH, _F32)] * 5 + [(None, LANE, _F32)], tm=_PAIR, name=f"a{i}_scanb", carry=[((A_WIDTH, HEAD_DIM), _F32)], reverse=True)
        d_q, d_k, d_v, d_bg = _rowmap(
            _gdn_intra_bwd_fn, [_t(r["q"]), _t(r["k"]), _t(r["v"]), _t(r["bg"]), _t(d_w), _t(d_u), _t(d_qd), _t(d_kd), _t(d_qk), _t(d_dcrow)], [],
            [(None, A_WIDTH, _F32)] * 3 + [(None, LANE, _F32)], tm=_INTRA_ROWS, name=f"a{i}_intrab")
        alog, dtb = p["a_A_log_row"][i], p["a_dt_bias_row"][i]
        d_qkv_c, d_ba, g["a_A_log_row"][i], g["a_dt_bias_row"][i] = _rowmap(
            _gdn_prep_bwd_fn, [_t(r["qkv_c"]), _t(h, LANE, 26), _t(d_q), _t(d_k), _t(d_v), _t(d_bg)], [alog, dtb],
            [(None, 3 * A_WIDTH, _F32), (None, LANE, _BF)], [((1, LANE), _F32), ((1, LANE), _F32)], tm=tm_e, name=f"a{i}_prepb")
        d_qkv, g["a_conv"][i] = _conv_bwd(h, d_qkv_c, p["a_conv"][i], name=f"a{i}_convb")
        d_h = jnp.concatenate([d_qkv, d_hpart, d_ba], axis=1)
        d_xn = _mm_nt(d_h, p["a_w_in"][i], out_dtype=_F32, name=f"a{i}_dxn", tn=1024)
        g["a_w_in"][i] = _mm_tn(r["xn"], d_h, name=f"a{i}_dwin", tn=1152)
        dx, g["mix_norm"][l] = _rms_bwd(r["x1"], d_xn, dx, row(p["mix_norm"][l]), name=f"a{i}_mixnormb")
        dx, g["ffn1_norm"][l], g["ffn_w_gu"], g["ffn_w_down"] = _ffn_bwd(
            dx, r["ffn1"], row(p["ffn1_norm"][l]), p["ffn_w_gu"], p["ffn_w_down"], l, g["ffn_w_gu"], g["ffn_w_down"], name=f"a{i}_ffn1b")

    d_mem_kv_all = jnp.concatenate(g.pop("mem_kv"), axis=1)
    d_mem_n = _mm_nt(d_mem_kv_all, p["w_mem_all"], out_dtype=_F32, name="mem_dn", tn=1024)
    g["w_mem_all"] = _mm_tn(mem_n, d_mem_kv_all, name="mem_dw", tn=1024)
    _, g["mem_norm"] = _rms_bwd(mem, d_mem_n, None, row(p["mem_norm"]), name="mem_normb")
    return loss, dx, g


_NOPE_ROPE = HEAD_DIM + QK_ROPE
_QKV_GATE = 4 * A_WIDTH
_BETA_AT = _QKV_GATE + MEM_WIDTH


def _lane_row(vals, at):
    n = vals.shape[0]
    return jnp.concatenate([jnp.zeros((at,), _F32), vals.astype(_F32), jnp.zeros((LANE - at - n,), _F32)]).reshape(1, LANE)


def _compute_form(w, conv_f32, ffn_w_gu, ffn_w_down):
    p = {n: w[n] for n in ("ffn1_norm", "mix_norm", "ffn2_norm", "w_out", "mem_norm", "a_out_norm", "b_w_in", "b_q_norm", "kv_in_norm",
                           "kv_lat_norm", "final_norm")}
    p["ffn_w_gu"], p["ffn_w_down"] = ffn_w_gu, ffn_w_down
    wm = w["w_mem_kv"]
    p["w_mem_all"] = jnp.transpose(wm, (1, 0, 2)).reshape(D_MODEL, DEPTH * 2 * MEM_WIDTH)
    a = w["a_w_in"]
    pad = jnp.zeros((N_A, D_MODEL, A_IN_PAD - A_IN), a.dtype)
    p["a_w_in"] = jnp.concatenate([a[:, :, :_QKV_GATE], a[:, :, _QKV_GATE + 2 * HEADS:], a[:, :, _QKV_GATE:_QKV_GATE + 2 * HEADS], pad], axis=2)
    p["a_conv"] = jnp.concatenate([conv_f32, jnp.zeros((N_A, SUBLANE - CONV_K, 3 * A_WIDTH), _F32)], axis=1)
    p["a_A_log_row"] = [_lane_row(w["a_A_log"][i], HEADS) for i in range(N_A)]
    p["a_dt_bias_row"] = [_lane_row(w["a_dt_bias"][i], HEADS) for i in range(N_A)]
    uq = w["b_w_uq"].reshape(N_B, Q_LORA, HEADS, _NOPE_ROPE)
    rope = jnp.concatenate([uq[..., HEAD_DIM:], jnp.zeros((N_B, Q_LORA, HEADS, LANE - QK_ROPE), uq.dtype)], axis=-1)
    p["b_w_uq"] = jnp.concatenate([uq[..., :HEAD_DIM].reshape(N_B, Q_LORA, A_WIDTH), rope.reshape(N_B, Q_LORA, A_WIDTH)], axis=-1)
    dkv = w["w_dkv"]
    p["w_dkv"] = jnp.concatenate([dkv, jnp.zeros((D_MODEL, DKV_PAD - dkv.shape[1]), dkv.dtype)], axis=1)
    ukv = w["w_ukv"].reshape(KV_LORA, HEADS, 2 * HEAD_DIM)
    p["w_ukv"] = jnp.concatenate([ukv[..., :HEAD_DIM].reshape(KV_LORA, A_WIDTH), ukv[..., HEAD_DIM:].reshape(KV_LORA, A_WIDTH)], axis=-1)
    return p


def _natural_grads(g):
    st = lambda xs: jnp.stack(xs, axis=0)
    n = {}
    for k in ("ffn1_norm", "mix_norm", "ffn2_norm"):
        n[k] = st(g[k]).reshape(DEPTH, D_MODEL)
    for k in ("w_out", "b_w_in"):
        n[k] = st(g[k])
    n["mem_norm"] = g["mem_norm"].reshape(D_MODEL)
    n["w_mem_kv"] = jnp.transpose(g["w_mem_all"].reshape(D_MODEL, DEPTH, 2 * MEM_WIDTH), (1, 0, 2))
    a = st(g["a_w_in"])
    n["a_w_in"] = jnp.concatenate([a[:, :, :_QKV_GATE], a[:, :, _BETA_AT:_BETA_AT + 2 * HEADS], a[:, :, _QKV_GATE:_BETA_AT]], axis=2)
    n["a_conv"] = st(g["a_conv"])[:, :CONV_K]
    n["a_A_log"] = st(g["a_A_log_row"])[:, 0, HEADS:2 * HEADS]
    n["a_dt_bias"] = st(g["a_dt_bias_row"])[:, 0, HEADS:2 * HEADS]
    n["a_out_norm"] = st(g["a_out_norm"]).reshape(N_A, HEAD_DIM)
    n["b_q_norm"] = st(g["b_q_norm"]).reshape(N_B, Q_LORA)
    uq = st(g["b_w_uq"])
    nope = uq[:, :, :A_WIDTH].reshape(N_B, Q_LORA, HEADS, HEAD_DIM)
    rope = uq[:, :, A_WIDTH:].reshape(N_B, Q_LORA, HEADS, LANE)[..., :QK_ROPE]
    n["b_w_uq"] = jnp.concatenate([nope, rope], axis=-1).reshape(N_B, Q_LORA, HEADS * _NOPE_ROPE)
    n["kv_in_norm"] = g["kv_in_norm"].reshape(D_MODEL)
    n["w_dkv"] = g["w_dkv"][:, :KV_LORA + QK_ROPE]
    n["kv_lat_norm"] = g["kv_lat_norm"].reshape(KV_LORA)
    ukv = g["w_ukv"]
    n["w_ukv"] = jnp.concatenate([ukv[:, :A_WIDTH].reshape(KV_LORA, HEADS, HEAD_DIM), ukv[:, A_WIDTH:].reshape(KV_LORA, HEADS, HEAD_DIM)],
                                 axis=-1).reshape(KV_LORA, HEADS * 2 * HEAD_DIM)
    n["final_norm"] = g["final_norm"].reshape(D_MODEL)
    return n


def _rope_tables(positions):
    inv = ROPE_THETA ** (-jnp.arange(0, QK_ROPE, 2, dtype=_F32) / QK_ROPE)
    ang = positions.astype(_F32)[:, None] * inv
    z = jnp.zeros((positions.shape[0], LANE - QK_ROPE), _F32)
    cos, sin = jnp.cos(ang), jnp.sin(ang)
    return jnp.concatenate([cos, cos, z], axis=1), jnp.concatenate([sin, sin, z], axis=1)


_HBM = pl.BlockSpec(memory_space=pltpu.HBM)


def _place():
    x, y, c = lax.axis_index("x"), lax.axis_index("y"), lax.axis_index("c")
    return x, y, c, [(1 - x, y), (x, 1 - y), (1 - x, 1 - y)]


def _remote(src, dst, send_sem, recv_sem, to):
    return pltpu.make_async_remote_copy(src_ref=src, dst_ref=dst, send_sem=send_sem, recv_sem=recv_sem, device_id=to, device_id_type=_MESH)


def _gather_over_chips(shard, *, name):
    rows, cols = shard.shape
    half = rows // 2

    def body(w_ref, out_ref, send_sems, recv_sems):
        x, y, c, chips = _place()
        k = 2 * x + y

        def part(chip, h):
            return out_ref.at[chip, pl.ds(h * half, half), :]

        first = [_remote(w_ref.at[pl.ds(c * half, half), :], part(k, c), send_sems.at[j], recv_sems.at[j], (px, py, c))
                 for j, (px, py) in enumerate(chips)]
        for cp in first:
            cp.start()
        passed = []
        for j, (px, py) in enumerate(chips):
            got = part(2 * px + py, c)
            _remote(got, got, send_sems.at[j], recv_sems.at[j], (px, py, c)).wait_recv()
            fw = _remote(got, got, send_sems.at[3 + j], recv_sems.at[3 + j], (x, y, 1 - c))
            fw.start()
            passed.append(fw)
        for j, (px, py) in enumerate(chips):
            got = part(2 * px + py, 1 - c)
            _remote(got, got, send_sems.at[3 + j], recv_sems.at[3 + j], (x, y, 1 - c)).wait_recv()
        for cp in first + passed:
            cp.wait_send()

    others = _pcall(
        body, name=name, in_specs=[_HBM], out_specs=_HBM, out_shape=jax.ShapeDtypeStruct((N_CHIPS, rows, cols), shard.dtype),
        scratch_shapes=[pltpu.SemaphoreType.DMA((6,)), pltpu.SemaphoreType.DMA((6,))],
    )(shard)
    return lax.dynamic_update_slice(others, shard[None], (2 * lax.axis_index("x") + lax.axis_index("y"), 0, 0))


PAIR_COPIES = 4


def _scatter_over_chips(v, *, name):
    def body(v_ref, out_ref, send_sems, recv_sems):
        x, y, c, chips = _place()
        cps = [_remote(v_ref.at[2 * px + py], out_ref.at[j], send_sems.at[j], recv_sems.at[j], (px, py, c)) for j, (px, py) in enumerate(chips)]
        for cp in cps:
            cp.start()
        for cp in cps:
            cp.wait()

    return _pcall(body, name=name, in_specs=[_HBM], out_specs=_HBM, out_shape=jax.ShapeDtypeStruct((N_CHIPS - 1,) + v.shape[1:], v.dtype),
                  scratch_shapes=[pltpu.SemaphoreType.DMA((3,)), pltpu.SemaphoreType.DMA((3,))])(v)


def _all_reduce_small(v, *, name):
    def body(v_ref, out_ref, all_ref, send_sems, recv_sems):
        x, y, c, _ = _place()
        me = 4 * x + 2 * y + c
        all_ref[me] = v_ref[...]
        cps = []
        for f in range(1, N_DEV):
            fx, fy, fc = (f >> 2) & 1, (f >> 1) & 1, f & 1
            to = (x + fx - 2 * x * fx, y + fy - 2 * y * fy, c + fc - 2 * c * fc)
            cps.append(_remote(v_ref, all_ref.at[me], send_sems.at[f - 1], recv_sems.at[f - 1], to))
        for cp in cps:
            cp.start()
        for cp in cps:
            cp.wait()
        acc = all_ref[0]
        for d in range(1, N_DEV):
            acc = acc + all_ref[d]
        out_ref[...] = acc

    vm = pl.BlockSpec(memory_space=pltpu.VMEM)
    return _pcall(body, name=name, in_specs=[vm], out_specs=vm, out_shape=jax.ShapeDtypeStruct(v.shape, v.dtype),
                  scratch_shapes=[pltpu.VMEM((N_DEV,) + v.shape, v.dtype), pltpu.SemaphoreType.DMA((N_DEV - 1,)), pltpu.SemaphoreType.DMA((N_DEV - 1,))])(v)


_FFN_GU = ("ffn1_w_gu", "ffn2_w_gu")
_FFN_DOWN = ("ffn1_w_down", "ffn2_w_down")
_BIG = (("w_out", 1), ("w_mem_kv", 1), ("a_w_in", 2), ("a_conv", 2), ("b_w_in", 1), ("b_w_uq", 2), ("w_dkv", 0),
        ("w_ukv", 1))
_SMALL = ("ffn1_norm", "mix_norm", "ffn2_norm", "mem_norm", "a_A_log", "a_dt_bias", "a_out_norm", "b_q_norm", "kv_in_norm", "kv_lat_norm",
          "final_norm")
_WEIGHTS = ("ffn1_norm", "ffn1_w_gu", "ffn1_w_down", "mix_norm", "ffn2_norm", "ffn2_w_gu", "ffn2_w_down", "w_out", "mem_norm", "w_mem_kv",
            "a_w_in", "a_conv", "a_A_log", "a_dt_bias", "a_out_norm", "b_w_in", "b_q_norm", "b_w_uq", "kv_in_norm", "w_dkv", "kv_lat_norm",
            "w_ukv", "final_norm")


PACK_PIECE_ROWS = 16


def _piece_rows(shape):
    return -(-math.prod(shape) // (PACK_COLS * PACK_PIECE_ROWS)) * PACK_PIECE_ROWS


def _packed_rows(shapes):
    return sum(_piece_rows(s) for s in shapes)


def _pack(arrs, rows):
    pieces = []
    for a in arrs:
        n, r = a.size, _piece_rows(a.shape)
        flat = a.reshape(-1)
        if r * PACK_COLS != n:
            flat = jnp.concatenate([flat, jnp.zeros((r * PACK_COLS - n,), a.dtype)])
        pieces.append(flat.reshape(r, PACK_COLS))
    used = sum(p.shape[0] for p in pieces)
    if rows > used:
        pieces.append(jnp.zeros((rows - used, PACK_COLS), arrs[0].dtype))
    return jnp.concatenate(pieces, axis=0)


def _unpack(packed, shapes):
    off, out = 0, []
    for shp in shapes:
        n, r = math.prod(shp), _piece_rows(shp)
        piece = packed[off:off + r]
        out.append((piece if r * PACK_COLS == n else piece.reshape(-1)[:n]).reshape(shp))
        off += r
    return out


def _adamw_math(w_, g_, m_, v_):
    m2 = ADAM_B1 * m_ + (1.0 - ADAM_B1) * g_
    v2 = ADAM_B2 * v_ + (1.0 - ADAM_B2) * (g_ * g_)
    m_hat = m2 / (1.0 - ADAM_B1 ** ADAM_STEP)
    v_hat = v2 / (1.0 - ADAM_B2 ** ADAM_STEP)
    return -ADAM_LR * (m_hat / (jnp.sqrt(v_hat) + ADAM_EPS) + ADAM_WD * w_), m2, v2


def _adamw(w, g, m, v, *, name):
    return _rowmap(_adamw_math, [_t(w), _t(g), _t(m), _t(v)], [], [(None, w.shape[1], _F32)] * 3, tm=min(PACK_ROW_TILE, w.shape[0]), name=name)


def _pair_exchange_half(g, *, name):
    n, _, h, cols = g.shape

    def body(g_ref, out_ref, send_sems, recv_sems):
        x, y, c, _ = _place()
        cps = [_remote(g_ref.at[k, 1 - c], out_ref.at[k], send_sems.at[k], recv_sems.at[k], (x, y, 1 - c)) for k in range(n)]
        for cp in cps:
            cp.start()
        for cp in cps:
            cp.wait()

    return _pcall(body, name=name, in_specs=[_HBM], out_specs=_HBM, out_shape=jax.ShapeDtypeStruct((n, h, cols), g.dtype),
                  scratch_shapes=[pltpu.SemaphoreType.DMA((n,)), pltpu.SemaphoreType.DMA((n,))])(g)


def _add_half(g, other, c, *, name):
    n, _, h, cols = g.shape
    tm = min(PACK_ROW_TILE, h)

    def body(c_ref, g_ref, o_ref, sum_ref, narrow_ref):
        acc = g_ref[...] + o_ref[...]
        sum_ref[...] = acc
        narrow_ref[...] = acc.astype(narrow_ref.dtype)

    slab = pl.BlockSpec((None, tm, cols), lambda k, i, c_ref: (k, i, 0))
    return _pcall(
        body, name=name,
        grid_spec=pltpu.PrefetchScalarGridSpec(
            num_scalar_prefetch=1, grid=(n, h // tm),
            in_specs=[pl.BlockSpec((None, None, tm, cols), lambda k, i, c_ref: (k, c_ref[0], i, 0)), slab], out_specs=[slab, slab]),
        out_shape=[jax.ShapeDtypeStruct((n, h, cols), _F32), jax.ShapeDtypeStruct((n, h, cols), jnp.bfloat16)],
        compiler_params=_cp(2),
    )(jnp.reshape(c, (1,)).astype(jnp.int32), g, other)


def _add_own(chip_sum, from_chips, chip, *, name):
    _, h, cols = chip_sum.shape
    tm = min(PACK_ROW_TILE, h)

    def body(k_ref, own_ref, a_ref, b_ref, c_ref, o_ref):
        o_ref[...] = ((own_ref[...] + a_ref[...].astype(_F32)) + b_ref[...].astype(_F32)) + c_ref[...].astype(_F32)

    sent = [pl.BlockSpec((None, tm, cols), functools.partial(lambda i, k_ref, j: (j, i, 0), j=j)) for j in range(N_CHIPS - 1)]
    return _pcall(
        body, name=name,
        grid_spec=pltpu.PrefetchScalarGridSpec(
            num_scalar_prefetch=1, grid=(h // tm,),
            in_specs=[pl.BlockSpec((None, tm, cols), lambda i, k_ref: (k_ref[0], i, 0))] + sent,
            out_specs=pl.BlockSpec((tm, cols), lambda i, k_ref: (i, 0))),
        out_shape=jax.ShapeDtypeStruct((h, cols), _F32), compiler_params=_cp(1),
    )(jnp.reshape(chip, (1,)).astype(jnp.int32), chip_sum, from_chips, from_chips, from_chips)


def _pair_gather(mine, *, name):
    h, cols = mine.shape
    per = h // PAIR_COPIES
    assert per * PAIR_COPIES == h and per % SUBLANE == 0, mine.shape

    def body(v_ref, out_ref, send_sems, recv_sems):
        x, y, c, _ = _place()
        cps = [_remote(v_ref.at[pl.ds(q * per, per), :], out_ref.at[c, pl.ds(q * per, per), :], send_sems.at[q], recv_sems.at[q], (x, y, 1 - c))
               for q in range(PAIR_COPIES)]
        for cp in cps:
            cp.start()
        for q, cp in enumerate(cps):
            cp.wait_send()
            _remote(v_ref.at[pl.ds(q * per, per), :], out_ref.at[1 - c, pl.ds(q * per, per), :], send_sems.at[q], recv_sems.at[q],
                    (x, y, 1 - c)).wait_recv()

    both = _pcall(body, name=name, in_specs=[_HBM], out_specs=_HBM, out_shape=jax.ShapeDtypeStruct((2, h, cols), mine.dtype),
                  scratch_shapes=[pltpu.SemaphoreType.DMA((PAIR_COPIES,)), pltpu.SemaphoreType.DMA((PAIR_COPIES,))])(mine)
    return lax.dynamic_update_slice(both, mine[None], (lax.axis_index("c"), 0, 0))


def _reduce_over_devices(g, c, chip, *, name):
    n, rows, cols = g.shape
    g = g.reshape(n, 2, rows // 2, cols)
    chip_sum, narrow = _add_half(g, _pair_exchange_half(g, name=name + "_pair_sum"), c, name=name + "_add_pair")
    mine = _add_own(chip_sum, _scatter_over_chips(narrow, name=name + "_scatter"), chip, name=name + "_add_chips")
    return _pair_gather(mine, name=name + "_pair_gather").reshape(rows, cols)


def _adamw_at(w, m, v, g_all, first_row, *, name):
    tm = min(PACK_ROW_TILE, w.shape[0])
    assert w.shape[0] % tm == 0 and first_row % tm == 0, (name, w.shape, first_row)

    def fn(w_, m_, v_, g_):
        return _adamw_math(w_, g_, m_, v_) + (g_,)

    return _rowmap(fn, [_t(w), _t(m), _t(v), _t(g_all, first=first_row // tm)], [], [(None, w.shape[1], _F32)] * 4, tm=tm, name=name)


def _step(x, mem, positions, loss_target, w, m, v):
    cx, cy, cc = lax.axis_index("x"), lax.axis_index("y"), lax.axis_index("c")
    chip = 2 * cx + cy
    big = [n for n, _ in _BIG]
    shard_shapes = [w[n].shape for n in big]
    rows = -(-_packed_rows(shard_shapes) // (2 * PACK_ROW_TILE)) * 2 * PACK_ROW_TILE
    flat = lambda a: a.reshape(-1, a.shape[-1])

    ffn_w_gu = _gather_over_chips(jnp.concatenate([flat(w[n]) for n in _FFN_GU]).astype(_BF), name="gather_w_gu")
    ffn_w_down = _gather_over_chips(jnp.concatenate([flat(w[n]) for n in _FFN_DOWN]).astype(_BF), name="gather_w_down")
    w_pack = _pack([w[n] for n in big], rows)
    gathered = _gather_over_chips(w_pack.astype(_BF), name="gather_weights")
    pieces = [_unpack(gathered[k], shard_shapes) for k in range(N_CHIPS)]
    full = {n: jnp.concatenate([pieces[k][i] for k in range(N_CHIPS)], axis=ax) for i, (n, ax) in enumerate(_BIG)}
    for n in _SMALL:
        full[n] = w[n]
    conv = w["a_conv"]
    slots = jnp.stack([jnp.where((chip == k) & (cc == 0), conv, 0.0) for k in range(N_CHIPS)])
    conv_all = _unpack(_all_reduce_small(_pack([slots], _piece_rows(slots.shape)), name="gather_conv"), [slots.shape])[0]
    conv_full = jnp.concatenate([conv_all[k] for k in range(N_CHIPS)], axis=2)

    p = _compute_form(full, conv_full, ffn_w_gu, ffn_w_down)
    cos_t, sin_t = _rope_tables(positions[0])
    loss_tile, d_x, g = _local_step(x[0], mem[0], cos_t, sin_t, loss_target[0], p)
    gn = _natural_grads(g)

    def shard_of(a, ax, k):
        size = a.shape[ax] // N_CHIPS
        return lax.slice_in_dim(a, k * size, (k + 1) * size, axis=ax)

    grads, deltas, new_m, new_v = {}, {}, {}, {}
    for names, key in ((_FFN_GU, "ffn_w_gu"), (_FFN_DOWN, "ffn_w_down")):
        buf = g[key]
        reduced = _reduce_over_devices(buf.reshape(N_CHIPS, -1, buf.shape[-1]), cc, chip, name="grad_" + key)
        first = 0
        for n in names:
            d_, m_, v_, g_ = _adamw_at(flat(w[n]), flat(m[n]), flat(v[n]), reduced, first, name="adamw_" + n)
            grads[n], deltas[n], new_m[n], new_v[n] = (t.reshape(w[n].shape) for t in (g_, d_, m_, v_))
            first += flat(w[n]).shape[0]
    g_pack = jnp.stack([_pack([shard_of(gn[n], ax, k) for n, ax in _BIG], rows) for k in range(N_CHIPS)])
    g_big = _reduce_over_devices(g_pack, cc, chip, name="grad_misc")
    d_big, m_big, v_big = _adamw(w_pack, g_big, _pack([m[n] for n in big], rows), _pack([v[n] for n in big], rows), name="adamw_misc")

    small_shapes = [w[n].shape for n in _SMALL]
    small_rows = _packed_rows(small_shapes)
    g_small = _all_reduce_small(_pack([gn[n] for n in _SMALL], small_rows), name="grad_small")
    d_small, m_small, v_small = _adamw(_pack([w[n] for n in _SMALL], small_rows), g_small, _pack([m[n] for n in _SMALL], small_rows),
                                       _pack([v[n] for n in _SMALL], small_rows), name="adamw_small")

    for out, big_pack, small_pack in ((grads, g_big, g_small), (deltas, d_big, d_small), (new_m, m_big, m_small), (new_v, v_big, v_small)):
        out.update(zip(big, _unpack(big_pack, shard_shapes)))
        out.update(zip(_SMALL, _unpack(small_pack, small_shapes)))
    loss = lax.psum(loss_tile[0, 0], ("x", "y", "c"))
    return (loss, d_x[None], *[grads[n] for n in _WEIGHTS], *[deltas[n] for n in _WEIGHTS], *[new_m[n] for n in _WEIGHTS],
            *[new_v[n] for n in _WEIGHTS])


def kernel(x, mem, positions, ffn1_norm, ffn1_w_gu, ffn1_w_down, mix_norm, ffn2_norm, ffn2_w_gu, ffn2_w_down, w_out, mem_norm, w_mem_kv, a_w_in, a_conv, a_A_log, a_dt_bias, a_out_norm, b_w_in, b_q_norm, b_w_uq, kv_in_norm, w_dkv, kv_lat_norm, w_ukv, final_norm, loss_target, m_ffn1_norm, m_ffn1_w_gu, m_ffn1_w_down, m_mix_norm, m_ffn2_norm, m_ffn2_w_gu, m_ffn2_w_down, m_w_out, m_mem_norm, m_w_mem_kv, m_a_w_in, m_a_conv, m_a_A_log, m_a_dt_bias, m_a_out_norm, m_b_w_in, m_b_q_norm, m_b_w_uq, m_kv_in_norm, m_w_dkv, m_kv_lat_norm, m_w_ukv, m_final_norm, v_ffn1_norm, v_ffn1_w_gu, v_ffn1_w_down, v_mix_norm, v_ffn2_norm, v_ffn2_w_gu, v_ffn2_w_down, v_w_out, v_mem_norm, v_w_mem_kv, v_a_w_in, v_a_conv, v_a_A_log, v_a_dt_bias, v_a_out_norm, v_b_w_in, v_b_q_norm, v_b_w_uq, v_kv_in_norm, v_w_dkv, v_kv_lat_norm, v_w_ukv, v_final_norm):
    given = dict(locals())
    w = {n: given[n] for n in _WEIGHTS}
    m = {n: given["m_" + n] for n in _WEIGHTS}
    v = {n: given["v_" + n] for n in _WEIGHTS}
    return _step(x, mem, positions, loss_target, w, m, v)
```
